```python
import jax
import jax.numpy as jnp
from jax import lax
import numpy as np

D_MODEL = 2048
BATCH = 8
SEQ = 2048
DEPTH = 4

CHUNK = 64
Q_BLOCK = 128
ATTN_WIDTH = D_MODEL // 2
ATTN_HEAD_DIM = 128
ATTN_HEADS = ATTN_WIDTH // ATTN_HEAD_DIM
CONV_CH = D_MODEL // 4
DW_CONV_LEN = 31
LRU_WIDTH = D_MODEL // 4
LRU_BLOCKS = 4
LRU_BLOCK_DIM = LRU_WIDTH // LRU_BLOCKS
LRU_CONV_LEN = 4
LRU_C = 8.0
MIX_WIDTH = ATTN_WIDTH + CONV_CH + LRU_WIDTH
IN_COLS = 3 * ATTN_WIDTH + 2 * CONV_CH + 2 * LRU_WIDTH
SPLITS = [ATTN_WIDTH, 2 * ATTN_WIDTH, 3 * ATTN_WIDTH,
          3 * ATTN_WIDTH + CONV_CH, 3 * ATTN_WIDTH + 2 * CONV_CH,
          3 * ATTN_WIDTH + 2 * CONV_CH + LRU_WIDTH]
D_FF = -(-8 * D_MODEL // (3 * 256)) * 256
EPS = 1e-6

kernel_name = "hybrid_stickbreak_conformer_rglru_block"


def rms_norm(x, g):
    xf = x.astype(jnp.float32)
    y = xf * lax.rsqrt(jnp.mean(xf * xf, axis=-1, keepdims=True) + EPS)
    return (y * g.astype(jnp.float32)).astype(x.dtype)


def layer_norm(x, g, b):
    xf = x.astype(jnp.float32)
    mu = jnp.mean(xf, axis=-1, keepdims=True)
    xc = xf - mu
    y = xc * lax.rsqrt(jnp.mean(xc * xc, axis=-1, keepdims=True) + EPS)
    return (y * g.astype(jnp.float32) + b.astype(jnp.float32)).astype(x.dtype)


def causal_depthwise_conv(x, w, b):
    width, ch = w.shape
    y = lax.conv_general_dilated(x, w[:, None, :].astype(x.dtype), window_strides=(1,),
                                 padding=[(width - 1, 0)],
                                 dimension_numbers=('NWC', 'WIO', 'NWC'),
                                 feature_group_count=ch)
    return y + b.astype(x.dtype)


def stick_breaking_attention(q, k, v):
    _, s_len, _, dh = q.shape
    scale = dh ** -0.5
    outs = []
    for blk in range(s_len // Q_BLOCK):
        q0 = blk * Q_BLOCK
        kv_len = q0 + Q_BLOCK
        qb = q[:, q0:kv_len].astype(jnp.float32)
        kb = k[:, :kv_len].astype(jnp.float32)
        vb = v[:, :kv_len].astype(jnp.float32)
        z = jnp.einsum('bqhd,bkhd->bhqk', qb, kb) * scale
        q_pos = q0 + jnp.arange(Q_BLOCK)
        k_pos = jnp.arange(kv_len)
        mask = k_pos[None, :] < q_pos[:, None]
        log_keep = jnp.where(mask, jax.nn.log_sigmoid(-z), 0.0)
        log_later = lax.cumsum(log_keep, axis=3, reverse=True) - log_keep
        log_w = jax.nn.log_sigmoid(z) + log_later
        w = jnp.where(mask, jnp.exp(log_w), 0.0)
        outs.append(jnp.einsum('bhqk,bkhd->bqhd', w, vb))
    return jnp.concatenate(outs, axis=1).astype(v.dtype)


def conformer_conv(c_val, c_gate, dw_w, dw_b, ln_g, ln_b):
    u = c_val * jax.nn.sigmoid(c_gate)
    u = causal_depthwise_conv(u, dw_w, dw_b)
    u = layer_norm(u, ln_g, ln_b)
    return jax.nn.silu(u)


def griffin_recurrent(r_x, r_y, conv_w, conv_b, w_a, b_a, w_i, b_i, lam):
    xr = causal_depthwise_conv(r_x, conv_w, conv_b)
    bsz, s_len, width = xr.shape
    xb = xr.reshape(bsz, s_len, LRU_BLOCKS, LRU_BLOCK_DIM)
    gate_a = jnp.einsum('bsni,nio->bsno', xb, w_a).reshape(bsz, s_len, width) + b_a
    gate_i = jnp.einsum('bsni,nio->bsno', xb, w_i).reshape(bsz, s_len, width) + b_i
    r = jax.nn.sigmoid(gate_a.astype(jnp.float32))
    i = jax.nn.sigmoid(gate_i.astype(jnp.float32))
    log_a = -LRU_C * r * jax.nn.softplus(-lam.astype(jnp.float32))
    a = jnp.exp(log_a)
    b = jnp.sqrt(-jnp.expm1(2.0 * log_a)) * (i * xr.astype(jnp.float32))

    def combine(left, right):
        a1, b1 = left
        a2, b2 = right
        return a1 * a2, a2 * b1 + b2

    _, h = lax.associative_scan(combine, (a, b), axis=1)
    return (h * jax.nn.gelu(r_y.astype(jnp.float32))).astype(r_x.dtype)


def _fwd_setup_inputs(seed: int = 0) -> dict:
    key = jax.random.key(seed)
    ks = jax.random.split(key, 25)
    f32 = jnp.float32

    def nrm(k, shape, scale):
        return jax.random.normal(k, shape, f32) * scale

    def gain(k, shape):
        return 1.0 + 0.02 * jax.random.normal(k, shape, f32)

    u = jax.random.uniform(ks[20], (DEPTH, LRU_WIDTH), f32, 0.9, 0.999)
    a_base = u ** (1.0 / LRU_C)
    lam = jnp.log(a_base) - jnp.log1p(-a_base)
    return {
        'x': jax.random.normal(ks[0], (BATCH, SEQ, D_MODEL), f32),
        'w_in': nrm(ks[1], (DEPTH, D_MODEL, IN_COLS), D_MODEL ** -0.5),
        'w_out': nrm(ks[2], (DEPTH, MIX_WIDTH, D_MODEL), MIX_WIDTH ** -0.5),
        'g_pre_mix': gain(ks[3], (DEPTH, D_MODEL)),
        'g_post_mix': gain(ks[4], (DEPTH, D_MODEL)),
        'g_pre_ffn': gain(ks[5], (DEPTH, D_MODEL)),
        'g_post_ffn': gain(ks[6], (DEPTH, D_MODEL)),
        'g_attn_grp': gain(ks[7], (DEPTH, ATTN_WIDTH)),
        'g_conv_grp': gain(ks[8], (DEPTH, CONV_CH)),
        'g_lru_grp': gain(ks[9], (DEPTH, LRU_WIDTH)),
        'dw_conv_w': nrm(ks[10], (DEPTH, DW_CONV_LEN, CONV_CH), DW_CONV_LEN ** -0.5),
        'dw_conv_b': nrm(ks[11], (DEPTH, CONV_CH), 0.01),
        'conv_ln_g': gain(ks[12], (DEPTH, CONV_CH)),
        'conv_ln_b': nrm(ks[13], (DEPTH, CONV_CH), 0.01),
        'lru_conv_w': nrm(ks[14], (DEPTH, LRU_CONV_LEN, LRU_WIDTH), LRU_CONV_LEN ** -0.5),
        'lru_conv_b': nrm(ks[15], (DEPTH, LRU_WIDTH), 0.01),
        'lru_w_a': nrm(ks[16], (DEPTH, LRU_BLOCKS, LRU_BLOCK_DIM, LRU_BLOCK_DIM), LRU_BLOCK_DIM ** -0.5),
        'lru_b_a': nrm(ks[17], (DEPTH, LRU_WIDTH), 0.01),
        'lru_w_i': nrm(ks[18], (DEPTH, LRU_BLOCKS, LRU_BLOCK_DIM, LRU_BLOCK_DIM), LRU_BLOCK_DIM ** -0.5),
        'lru_b_i': nrm(ks[19], (DEPTH, LRU_WIDTH), 0.01),
        'lru_lambda': lam,
        'w_gate': nrm(ks[21], (DEPTH, D_MODEL, D_FF), D_MODEL ** -0.5),
        'w_up': nrm(ks[22], (DEPTH, D_MODEL, D_FF), D_MODEL ** -0.5),
        'w_down': nrm(ks[23], (DEPTH, D_FF, D_MODEL), D_FF ** -0.5),
    }


def _fwd_reference(x, w_in, w_out, g_pre_mix, g_post_mix, g_pre_ffn, g_post_ffn,
              g_attn_grp, g_conv_grp, g_lru_grp, dw_conv_w, dw_conv_b, conv_ln_g, conv_ln_b,
              lru_conv_w, lru_conv_b, lru_w_a, lru_b_a, lru_w_i, lru_b_i, lru_lambda,
              w_gate, w_up, w_down):
    bsz, s_len, _ = x.shape
    heads = (bsz, s_len, ATTN_HEADS, ATTN_HEAD_DIM)
    h = x
    for l in range(DEPTH):
        u = rms_norm(h, g_pre_mix[l])
        proj = jnp.einsum('bsd,dc->bsc', u, w_in[l])
        q, k, v, c_val, c_gate, r_x, r_y = jnp.split(proj, SPLITS, axis=-1)
        y_attn = stick_breaking_attention(q.reshape(heads), k.reshape(heads),
                                          v.reshape(heads)).reshape(bsz, s_len, ATTN_WIDTH)
        y_conv = conformer_conv(c_val, c_gate, dw_conv_w[l], dw_conv_b[l],
                                conv_ln_g[l], conv_ln_b[l])
        y_lru = griffin_recurrent(r_x, r_y, lru_conv_w[l], lru_conv_b[l], lru_w_a[l], lru_b_a[l],
                                  lru_w_i[l], lru_b_i[l], lru_lambda[l])
        mixed = jnp.concatenate([rms_norm(y_attn, g_attn_grp[l]),
                                 rms_norm(y_conv, g_conv_grp[l]),
                                 rms_norm(y_lru, g_lru_grp[l])], axis=-1)
        h = h + rms_norm(jnp.einsum('bsc,cd->bsd', mixed, w_out[l]), g_post_mix[l])
        u = rms_norm(h, g_pre_ffn[l])
        f = jax.nn.silu(jnp.einsum('bsd,df->bsf', u, w_gate[l])) * jnp.einsum('bsd,df->bsf', u, w_up[l])
        h = h + rms_norm(jnp.einsum('bsf,fd->bsd', f, w_down[l]), g_post_ffn[l])
    return h


import jax as _jax
import jax.numpy as _jnp

TWIN_FORMAT = 'train_step'
FWD_PARAMS = ['x', 'w_in', 'w_out', 'g_pre_mix', 'g_post_mix', 'g_pre_ffn', 'g_post_ffn', 'g_attn_grp', 'g_conv_grp', 'g_lru_grp', 'dw_conv_w', 'dw_conv_b', 'conv_ln_g', 'conv_ln_b', 'lru_conv_w', 'lru_conv_b', 'lru_w_a', 'lru_b_a', 'lru_w_i', 'lru_b_i', 'lru_lambda', 'w_gate', 'w_up', 'w_down']
TWIN_WEIGHTS = ['w_in', 'w_out', 'g_pre_mix', 'g_post_mix', 'g_pre_ffn', 'g_post_ffn', 'g_attn_grp', 'g_conv_grp', 'g_lru_grp', 'dw_conv_w', 'dw_conv_b', 'conv_ln_g', 'conv_ln_b', 'lru_conv_w', 'lru_conv_b', 'lru_w_a', 'lru_b_a', 'lru_w_i', 'lru_b_i', 'lru_lambda', 'w_gate', 'w_up', 'w_down']
TWIN_DIFF_INPUT = 'x'
TWIN_INPUTS = ['x', 'w_in', 'w_out', 'g_pre_mix', 'g_post_mix', 'g_pre_ffn', 'g_post_ffn', 'g_attn_grp', 'g_conv_grp', 'g_lru_grp', 'dw_conv_w', 'dw_conv_b', 'conv_ln_g', 'conv_ln_b', 'lru_conv_w', 'lru_conv_b', 'lru_w_a', 'lru_b_a', 'lru_w_i', 'lru_b_i', 'lru_lambda', 'w_gate', 'w_up', 'w_down', 'loss_target', 'm_w_in', 'm_w_out', 'm_g_pre_mix', 'm_g_post_mix', 'm_g_pre_ffn', 'm_g_post_ffn', 'm_g_attn_grp', 'm_g_conv_grp', 'm_g_lru_grp', 'm_dw_conv_w', 'm_dw_conv_b', 'm_conv_ln_g', 'm_conv_ln_b', 'm_lru_conv_w', 'm_lru_conv_b', 'm_lru_w_a', 'm_lru_b_a', 'm_lru_w_i', 'm_lru_b_i', 'm_lru_lambda', 'm_w_gate', 'm_w_up', 'm_w_down', 'v_w_in', 'v_w_out', 'v_g_pre_mix', 'v_g_post_mix', 'v_g_pre_ffn', 'v_g_post_ffn', 'v_g_attn_grp', 'v_g_conv_grp', 'v_g_lru_grp', 'v_dw_conv_w', 'v_dw_conv_b', 'v_conv_ln_g', 'v_conv_ln_b', 'v_lru_conv_w', 'v_lru_conv_b', 'v_lru_w_a', 'v_lru_b_a', 'v_lru_w_i', 'v_lru_b_i', 'v_lru_lambda', 'v_w_gate', 'v_w_up', 'v_w_down']
TWIN_OUTPUTS = ['loss', 'grad_x', 'grad_w_in', 'grad_w_out', 'grad_g_pre_mix', 'grad_g_post_mix', 'grad_g_pre_ffn', 'grad_g_post_ffn', 'grad_g_attn_grp', 'grad_g_conv_grp', 'grad_g_lru_grp', 'grad_dw_conv_w', 'grad_dw_conv_b', 'grad_conv_ln_g', 'grad_conv_ln_b', 'grad_lru_conv_w', 'grad_lru_conv_b', 'grad_lru_w_a', 'grad_lru_b_a', 'grad_lru_w_i', 'grad_lru_b_i', 'grad_lru_lambda', 'grad_w_gate', 'grad_w_up', 'grad_w_down', 'delta_w_in', 'delta_w_out', 'delta_g_pre_mix', 'delta_g_post_mix', 'delta_g_pre_ffn', 'delta_g_post_ffn', 'delta_g_attn_grp', 'delta_g_conv_grp', 'delta_g_lru_grp', 'delta_dw_conv_w', 'delta_dw_conv_b', 'delta_conv_ln_g', 'delta_conv_ln_b', 'delta_lru_conv_w', 'delta_lru_conv_b', 'delta_lru_w_a', 'delta_lru_b_a', 'delta_lru_w_i', 'delta_lru_b_i', 'delta_lru_lambda', 'delta_w_gate', 'delta_w_up', 'delta_w_down', 'new_m_w_in', 'new_m_w_out', 'new_m_g_pre_mix', 'new_m_g_post_mix', 'new_m_g_pre_ffn', 'new_m_g_post_ffn', 'new_m_g_attn_grp', 'new_m_g_conv_grp', 'new_m_g_lru_grp', 'new_m_dw_conv_w', 'new_m_dw_conv_b', 'new_m_conv_ln_g', 'new_m_conv_ln_b', 'new_m_lru_conv_w', 'new_m_lru_conv_b', 'new_m_lru_w_a', 'new_m_lru_b_a', 'new_m_lru_w_i', 'new_m_lru_b_i', 'new_m_lru_lambda', 'new_m_w_gate', 'new_m_w_up', 'new_m_w_down', 'new_v_w_in', 'new_v_w_out', 'new_v_g_pre_mix', 'new_v_g_post_mix', 'new_v_g_pre_ffn', 'new_v_g_post_ffn', 'new_v_g_attn_grp', 'new_v_g_conv_grp', 'new_v_g_lru_grp', 'new_v_dw_conv_w', 'new_v_dw_conv_b', 'new_v_conv_ln_g', 'new_v_conv_ln_b', 'new_v_lru_conv_w', 'new_v_lru_conv_b', 'new_v_lru_w_a', 'new_v_lru_b_a', 'new_v_lru_w_i', 'new_v_lru_b_i', 'new_v_lru_lambda', 'new_v_w_gate', 'new_v_w_up', 'new_v_w_down']
TWIN_LEAF_KINDS = {'loss': 'loss', 'grad_x': 'grad_x', 'grad_w_in': 'grad_w', 'grad_w_out': 'grad_w', 'grad_g_pre_mix': 'grad_w', 'grad_g_post_mix': 'grad_w', 'grad_g_pre_ffn': 'grad_w', 'grad_g_post_ffn': 'grad_w', 'grad_g_attn_grp': 'grad_w', 'grad_g_conv_grp': 'grad_w', 'grad_g_lru_grp': 'grad_w', 'grad_dw_conv_w': 'grad_w', 'grad_dw_conv_b': 'grad_w', 'grad_conv_ln_g': 'grad_w', 'grad_conv_ln_b': 'grad_w', 'grad_lru_conv_w': 'grad_w', 'grad_lru_conv_b': 'grad_w', 'grad_lru_w_a': 'grad_w', 'grad_lru_b_a': 'grad_w', 'grad_lru_w_i': 'grad_w', 'grad_lru_b_i': 'grad_w', 'grad_lru_lambda': 'grad_w', 'grad_w_gate': 'grad_w', 'grad_w_up': 'grad_w', 'grad_w_down': 'grad_w', 'delta_w_in': 'delta_w', 'delta_w_out': 'delta_w', 'delta_g_pre_mix': 'delta_w', 'delta_g_post_mix': 'delta_w', 'delta_g_pre_ffn': 'delta_w', 'delta_g_post_ffn': 'delta_w', 'delta_g_attn_grp': 'delta_w', 'delta_g_conv_grp': 'delta_w', 'delta_g_lru_grp': 'delta_w', 'delta_dw_conv_w': 'delta_w', 'delta_dw_conv_b': 'delta_w', 'delta_conv_ln_g': 'delta_w', 'delta_conv_ln_b': 'delta_w', 'delta_lru_conv_w': 'delta_w', 'delta_lru_conv_b': 'delta_w', 'delta_lru_w_a': 'delta_w', 'delta_lru_b_a': 'delta_w', 'delta_lru_w_i': 'delta_w', 'delta_lru_b_i': 'delta_w', 'delta_lru_lambda': 'delta_w', 'delta_w_gate': 'delta_w', 'delta_w_up': 'delta_w', 'delta_w_down': 'delta_w', 'new_m_w_in': 'new_m', 'new_m_w_out': 'new_m', 'new_m_g_pre_mix': 'new_m', 'new_m_g_post_mix': 'new_m', 'new_m_g_pre_ffn': 'new_m', 'new_m_g_post_ffn': 'new_m', 'new_m_g_attn_grp': 'new_m', 'new_m_g_conv_grp': 'new_m', 'new_m_g_lru_grp': 'new_m', 'new_m_dw_conv_w': 'new_m', 'new_m_dw_conv_b': 'new_m', 'new_m_conv_ln_g': 'new_m', 'new_m_conv_ln_b': 'new_m', 'new_m_lru_conv_w': 'new_m', 'new_m_lru_conv_b': 'new_m', 'new_m_lru_w_a': 'new_m', 'new_m_lru_b_a': 'new_m', 'new_m_lru_w_i': 'new_m', 'new_m_lru_b_i': 'new_m', 'new_m_lru_lambda': 'new_m', 'new_m_w_gate': 'new_m', 'new_m_w_up': 'new_m', 'new_m_w_down': 'new_m', 'new_v_w_in': 'new_v', 'new_v_w_out': 'new_v', 'new_v_g_pre_mix': 'new_v', 'new_v_g_post_mix': 'new_v', 'new_v_g_pre_ffn': 'new_v', 'new_v_g_post_ffn': 'new_v', 'new_v_g_attn_grp': 'new_v', 'new_v_g_conv_grp': 'new_v', 'new_v_g_lru_grp': 'new_v', 'new_v_dw_conv_w': 'new_v', 'new_v_dw_conv_b': 'new_v', 'new_v_conv_ln_g': 'new_v', 'new_v_conv_ln_b': 'new_v', 'new_v_lru_conv_w': 'new_v', 'new_v_lru_conv_b': 'new_v', 'new_v_lru_w_a': 'new_v', 'new_v_lru_b_a': 'new_v', 'new_v_lru_w_i': 'new_v', 'new_v_lru_b_i': 'new_v', 'new_v_lru_lambda': 'new_v', 'new_v_w_gate': 'new_v', 'new_v_w_up': 'new_v', 'new_v_w_down': 'new_v'}


def _forward(args):
    return _fwd_reference(*[args[k] for k in FWD_PARAMS])


def _output_shape():
    out = _jax.eval_shape(lambda: _forward(_fwd_setup_inputs(0)))
    return out.shape, out.dtype

N_MICROBATCH = 1
ADAM_LR = 0.001
ADAM_B1 = 0.9
ADAM_B2 = 0.999
ADAM_EPS = 1e-08
ADAM_WD = 0.01
ADAM_STEP = 10
PER_EXAMPLE_BATCH_AXIS = {'x': 0, 'loss_target': 0}
SHARED_INPUTS = []
_WEIGHT_DTYPES = {'w_in': _jnp.float32, 'w_out': _jnp.float32, 'g_pre_mix': _jnp.float32, 'g_post_mix': _jnp.float32, 'g_pre_ffn': _jnp.float32, 'g_post_ffn': _jnp.float32, 'g_attn_grp': _jnp.float32, 'g_conv_grp': _jnp.float32, 'g_lru_grp': _jnp.float32, 'dw_conv_w': _jnp.float32, 'dw_conv_b': _jnp.float32, 'conv_ln_g': _jnp.float32, 'conv_ln_b': _jnp.float32, 'lru_conv_w': _jnp.float32, 'lru_conv_b': _jnp.float32, 'lru_w_a': _jnp.float32, 'lru_b_a': _jnp.float32, 'lru_w_i': _jnp.float32, 'lru_b_i': _jnp.float32, 'lru_lambda': _jnp.float32, 'w_gate': _jnp.float32, 'w_up': _jnp.float32, 'w_down': _jnp.float32}
MOMENT_SCALE = {'w_in': 5.619627e-01, 'w_out': 1.311374e+00, 'g_pre_mix': 8.755231e-01, 'g_post_mix': 8.097181e+00, 'g_pre_ffn': 5.845352e-01, 'g_post_ffn': 7.964335e+00, 'g_attn_grp': 8.281073e-01, 'g_conv_grp': 2.054482e+00, 'g_lru_grp': 1.552680e+00, 'dw_conv_w': 7.172017e-01, 'dw_conv_b': 8.886180e+00, 'conv_ln_g': 3.465137e+00, 'conv_ln_b': 5.208524e+00, 'lru_conv_w': 1.500081e+00, 'lru_conv_b': 1.798396e+01, 'lru_w_a': 4.823618e-01, 'lru_b_a': 3.493542e-01, 'lru_w_i': 9.080441e-01, 'lru_b_i': 5.204976e-01, 'lru_lambda': 6.116172e-01, 'w_gate': 2.132264e-01, 'w_up': 2.685257e-01, 'w_down': 4.441849e-01}


def _to_microbatches(a, axis):
    t = _jnp.moveaxis(a, axis, 0)
    t = t.reshape((N_MICROBATCH, t.shape[0] // N_MICROBATCH) + t.shape[1:])
    return _jnp.moveaxis(t, 1, axis + 1)


def setup_inputs(seed: int = 0) -> dict:
    inp = _fwd_setup_inputs(seed)
    key = _jax.random.fold_in(_jax.random.key(seed), 7919)
    shape, _ = _output_shape()
    out = dict(inp)
    out["loss_target"] = _jax.random.normal(_jax.random.fold_in(key, 0), shape, _jnp.float32)
    for i, name in enumerate(TWIN_WEIGHTS):
        w = inp[name].astype(_jnp.float32)
        if MOMENT_SCALE is None:
            s = _jnp.sqrt(_jnp.mean(_jnp.square(w)) + 1e-30)
        else:
            s = MOMENT_SCALE[name]
        km, kv = _jax.random.split(_jax.random.fold_in(key, i + 1))
        out[name] = w
        out["m_" + name] = s * _jax.random.normal(km, w.shape, _jnp.float32)
        out["v_" + name] = (s * s) * _jax.random.uniform(kv, w.shape, _jnp.float32, 0.5, 1.5)
    if N_MICROBATCH > 1:
        for name, axis in PER_EXAMPLE_BATCH_AXIS.items():
            out[name] = _to_microbatches(out[name], axis)
    return {'x': out['x'], 'w_in': out['w_in'], 'w_out': out['w_out'], 'g_pre_mix': out['g_pre_mix'], 'g_post_mix': out['g_post_mix'], 'g_pre_ffn': out['g_pre_ffn'], 'g_post_ffn': out['g_post_ffn'], 'g_attn_grp': out['g_attn_grp'], 'g_conv_grp': out['g_conv_grp'], 'g_lru_grp': out['g_lru_grp'], 'dw_conv_w': out['dw_conv_w'], 'dw_conv_b': out['dw_conv_b'], 'conv_ln_g': out['conv_ln_g'], 'conv_ln_b': out['conv_ln_b'], 'lru_conv_w': out['lru_conv_w'], 'lru_conv_b': out['lru_conv_b'], 'lru_w_a': out['lru_w_a'], 'lru_b_a': out['lru_b_a'], 'lru_w_i': out['lru_w_i'], 'lru_b_i': out['lru_b_i'], 'lru_lambda': out['lru_lambda'], 'w_gate': out['w_gate'], 'w_up': out['w_up'], 'w_down': out['w_down'], 'loss_target': out['loss_target'], 'm_w_in': out['m_w_in'], 'm_w_out': out['m_w_out'], 'm_g_pre_mix': out['m_g_pre_mix'], 'm_g_post_mix': out['m_g_post_mix'], 'm_g_pre_ffn': out['m_g_pre_ffn'], 'm_g_post_ffn': out['m_g_post_ffn'], 'm_g_attn_grp': out['m_g_attn_grp'], 'm_g_conv_grp': out['m_g_conv_grp'], 'm_g_lru_grp': out['m_g_lru_grp'], 'm_dw_conv_w': out['m_dw_conv_w'], 'm_dw_conv_b': out['m_dw_conv_b'], 'm_conv_ln_g': out['m_conv_ln_g'], 'm_conv_ln_b': out['m_conv_ln_b'], 'm_lru_conv_w': out['m_lru_conv_w'], 'm_lru_conv_b': out['m_lru_conv_b'], 'm_lru_w_a': out['m_lru_w_a'], 'm_lru_b_a': out['m_lru_b_a'], 'm_lru_w_i': out['m_lru_w_i'], 'm_lru_b_i': out['m_lru_b_i'], 'm_lru_lambda': out['m_lru_lambda'], 'm_w_gate': out['m_w_gate'], 'm_w_up': out['m_w_up'], 'm_w_down': out['m_w_down'], 'v_w_in': out['v_w_in'], 'v_w_out': out['v_w_out'], 'v_g_pre_mix': out['v_g_pre_mix'], 'v_g_post_mix': out['v_g_post_mix'], 'v_g_pre_ffn': out['v_g_pre_ffn'], 'v_g_post_ffn': out['v_g_post_ffn'], 'v_g_attn_grp': out['v_g_attn_grp'], 'v_g_conv_grp': out['v_g_conv_grp'], 'v_g_lru_grp': out['v_g_lru_grp'], 'v_dw_conv_w': out['v_dw_conv_w'], 'v_dw_conv_b': out['v_dw_conv_b'], 'v_conv_ln_g': out['v_conv_ln_g'], 'v_conv_ln_b': out['v_conv_ln_b'], 'v_lru_conv_w': out['v_lru_conv_w'], 'v_lru_conv_b': out['v_lru_conv_b'], 'v_lru_w_a': out['v_lru_w_a'], 'v_lru_b_a': out['v_lru_b_a'], 'v_lru_w_i': out['v_lru_w_i'], 'v_lru_b_i': out['v_lru_b_i'], 'v_lru_lambda': out['v_lru_lambda'], 'v_w_gate': out['v_w_gate'], 'v_w_up': out['v_w_up'], 'v_w_down': out['v_w_down']}


def _loss(weights, diff, rest, loss_target):
    with _jax.named_scope("forward"):
        args = {**rest, TWIN_DIFF_INPUT: diff, **{k: w.astype(_WEIGHT_DTYPES[k]) for k, w in weights.items()}}
        y = _forward(args)
    with _jax.named_scope("loss_head"):
        err = _jnp.square(y.astype(_jnp.float32) - loss_target)
        return 0.5 * _jnp.sum(_jnp.mean(err, axis=-1)) if err.ndim else 0.5 * err


def _adamw(w, g, m, v):
    m = ADAM_B1 * m + (1.0 - ADAM_B1) * g
    v = ADAM_B2 * v + (1.0 - ADAM_B2) * _jnp.square(g)
    m_hat = m / (1.0 - ADAM_B1 ** ADAM_STEP)
    v_hat = v / (1.0 - ADAM_B2 ** ADAM_STEP)
    delta = -ADAM_LR * (m_hat / (_jnp.sqrt(v_hat) + ADAM_EPS) + ADAM_WD * w)
    return delta, m, v


def reference(x, w_in, w_out, g_pre_mix, g_post_mix, g_pre_ffn, g_post_ffn, g_attn_grp, g_conv_grp, g_lru_grp, dw_conv_w, dw_conv_b, conv_ln_g, conv_ln_b, lru_conv_w, lru_conv_b, lru_w_a, lru_b_a, lru_w_i, lru_b_i, lru_lambda, w_gate, w_up, w_down, loss_target, m_w_in, m_w_out, m_g_pre_mix, m_g_post_mix, m_g_pre_ffn, m_g_post_ffn, m_g_attn_grp, m_g_conv_grp, m_g_lru_grp, m_dw_conv_w, m_dw_conv_b, m_conv_ln_g, m_conv_ln_b, m_lru_conv_w, m_lru_conv_b, m_lru_w_a, m_lru_b_a, m_lru_w_i, m_lru_b_i, m_lru_lambda, m_w_gate, m_w_up, m_w_down, v_w_in, v_w_out, v_g_pre_mix, v_g_post_mix, v_g_pre_ffn, v_g_post_ffn, v_g_attn_grp, v_g_conv_grp, v_g_lru_grp, v_dw_conv_w, v_dw_conv_b, v_conv_ln_g, v_conv_ln_b, v_lru_conv_w, v_lru_conv_b, v_lru_w_a, v_lru_b_a, v_lru_w_i, v_lru_b_i, v_lru_lambda, v_w_gate, v_w_up, v_w_down):
    given = dict(x=x, w_in=w_in, w_out=w_out, g_pre_mix=g_pre_mix, g_post_mix=g_post_mix, g_pre_ffn=g_pre_ffn, g_post_ffn=g_post_ffn, g_attn_grp=g_attn_grp, g_conv_grp=g_conv_grp, g_lru_grp=g_lru_grp, dw_conv_w=dw_conv_w, dw_conv_b=dw_conv_b, conv_ln_g=conv_ln_g, conv_ln_b=conv_ln_b, lru_conv_w=lru_conv_w, lru_conv_b=lru_conv_b, lru_w_a=lru_w_a, lru_b_a=lru_b_a, lru_w_i=lru_w_i, lru_b_i=lru_b_i, lru_lambda=lru_lambda, w_gate=w_gate, w_up=w_up, w_down=w_down, loss_target=loss_target, m_w_in=m_w_in, m_w_out=m_w_out, m_g_pre_mix=m_g_pre_mix, m_g_post_mix=m_g_post_mix, m_g_pre_ffn=m_g_pre_ffn, m_g_post_ffn=m_g_post_ffn, m_g_attn_grp=m_g_attn_grp, m_g_conv_grp=m_g_conv_grp, m_g_lru_grp=m_g_lru_grp, m_dw_conv_w=m_dw_conv_w, m_dw_conv_b=m_dw_conv_b, m_conv_ln_g=m_conv_ln_g, m_conv_ln_b=m_conv_ln_b, m_lru_conv_w=m_lru_conv_w, m_lru_conv_b=m_lru_conv_b, m_lru_w_a=m_lru_w_a, m_lru_b_a=m_lru_b_a, m_lru_w_i=m_lru_w_i, m_lru_b_i=m_lru_b_i, m_lru_lambda=m_lru_lambda, m_w_gate=m_w_gate, m_w_up=m_w_up, m_w_down=m_w_down, v_w_in=v_w_in, v_w_out=v_w_out, v_g_pre_mix=v_g_pre_mix, v_g_post_mix=v_g_post_mix, v_g_pre_ffn=v_g_pre_ffn, v_g_post_ffn=v_g_post_ffn, v_g_attn_grp=v_g_attn_grp, v_g_conv_grp=v_g_conv_grp, v_g_lru_grp=v_g_lru_grp, v_dw_conv_w=v_dw_conv_w, v_dw_conv_b=v_dw_conv_b, v_conv_ln_g=v_conv_ln_g, v_conv_ln_b=v_conv_ln_b, v_lru_conv_w=v_lru_conv_w, v_lru_conv_b=v_lru_conv_b, v_lru_w_a=v_lru_w_a, v_lru_b_a=v_lru_b_a, v_lru_w_i=v_lru_w_i, v_lru_b_i=v_lru_b_i, v_lru_lambda=v_lru_lambda, v_w_gate=v_w_gate, v_w_up=v_w_up, v_w_down=v_w_down)
    weights = {n: given[n] for n in TWIN_WEIGHTS}
    shared = {n: given[n] for n in SHARED_INPUTS}
    per_example = {n: given[n] for n in ['x']}
    grad_fn = _jax.value_and_grad(_loss, argnums=(0, 1))

    def one_microbatch(ex, loss_target):
        ex = dict(ex)
        diff = ex.pop(TWIN_DIFF_INPUT)
        return grad_fn(weights, diff, {**shared, **ex}, loss_target)

    if N_MICROBATCH == 1:
        loss, (grad_w, grad_x) = one_microbatch(per_example, given["loss_target"])
    else:
        def body(carry, xs):
            loss_sum, grad_sum = carry
            l_k, (gw_k, gx_k) = one_microbatch(xs[0], xs[1])
            with _jax.named_scope("update"):
                return (loss_sum + l_k, _jax.tree.map(_jnp.add, grad_sum, gw_k)), gx_k

        init = (_jnp.zeros((), _jnp.float32), _jax.tree.map(_jnp.zeros_like, weights))
        (loss, grad_w), grad_x = _jax.lax.scan(body, init, (per_example, given["loss_target"]))
    with _jax.named_scope("update"):
        delta_w, new_m, new_v = {}, {}, {}
        for n in TWIN_WEIGHTS:
            delta_w[n], new_m[n], new_v[n] = _adamw(weights[n], grad_w[n], given["m_" + n], given["v_" + n])
    return (loss, grad_x, *[grad_w[n] for n in TWIN_WEIGHTS], *[delta_w[n] for n in TWIN_WEIGHTS],
            *[new_m[n] for n in TWIN_WEIGHTS], *[new_v[n] for n in TWIN_WEIGHTS])
```

```python
import functools
import math

import jax
import jax.numpy as jnp
from jax import lax
from jax.experimental import pallas as pl
from jax.experimental.pallas import tpu as pltpu

F32 = jnp.float32
BF16 = jnp.bfloat16

N_DEV = 8
EPS = 1e-6
HEAD_DIM = 128
DW_LEN = 31
LRU_LEN = 4
LRU_BLOCKS = 4
LRU_C = 8.0
ATT_T = 256
ROW_T = 256
CONV_HALO = 32
LRU_HALO = 8
LANE = 128
SUBLANE = 8
PACK_ROWS = 512
VMEM_LIMIT = 56 * 1024 * 1024

ADAM_LR = 0.001
ADAM_B1 = 0.9
ADAM_B2 = 0.999
ADAM_EPS = 1e-08
ADAM_WD = 0.01
ADAM_STEP = 10

MESH_AXES = ("x", "y", "c")
_DIMS = {
    "nn": (((1,), (0,)), ((), ())),
    "nt": (((1,), (1,)), ((), ())),
    "tn": (((0,), (0,)), ((), ())),
}


def _params(n_axes):
    return pltpu.CompilerParams(
        dimension_semantics=("arbitrary",) * n_axes, vmem_limit_bytes=VMEM_LIMIT)


def _dot(a, b, mode="nn"):
    return lax.dot_general(a, b, _DIMS[mode], preferred_element_type=F32)


def _sigmoid(x):
    return 1.0 / (1.0 + jnp.exp(-x))


def _softplus(x):
    return jnp.maximum(x, 0.0) + jnp.log(1.0 + jnp.exp(-jnp.abs(x)))


def _neg_expm1(x):
    series = x * (1.0 + x * (0.5 + x * (1.0 / 6 + x * (1.0 / 24 + x * (1.0 / 120 + x * (1.0 / 720))))))
    return jnp.where(x > -0.25, -series, 1.0 - jnp.exp(x))


_GELU_C = math.sqrt(2.0 / math.pi)


def _gelu_and_grad(x):
    inner = _GELU_C * (x + 0.044715 * x * x * x)
    t = jnp.tanh(inner)
    val = 0.5 * x * (1.0 + t)
    grad = 0.5 * (1.0 + t) + 0.5 * x * (1.0 - t * t) * _GELU_C * (1.0 + 3 * 0.044715 * x * x)
    return val, grad


def _rms_stats(x):
    r = lax.rsqrt(jnp.mean(x * x, axis=-1, keepdims=True) + EPS)
    return x * r, r


def _rms_bwd(dy, x, g):
    xn, r = _rms_stats(x)
    dxn = dy * g
    dx = r * (dxn - xn * jnp.mean(dxn * xn, axis=-1, keepdims=True))
    return dx, jnp.sum(dy * xn, axis=0, keepdims=True)


def _row_spec(tr, width, col=0):
    return pl.BlockSpec((tr, width), lambda i, col=col: (i, col))


def _vec_spec(width):
    return pl.BlockSpec((1, width), lambda i: (0, 0))


def _matmul(name, mode, operands, in_specs, out_shape, out_spec, grid, out_block):
    npairs = len(operands) // 2
    nk = grid[2]

    def body(*refs):
        o_ref = refs[2 * npairs]

        def partial():
            acc = None
            for p in range(npairs):
                d = _dot(refs[2 * p][...], refs[2 * p + 1][...], mode)
                acc = d if acc is None else acc + d
            return acc

        if nk == 1:
            o_ref[...] = partial().astype(o_ref.dtype)
        else:
            acc_ref = refs[2 * npairs + 1]
            k = pl.program_id(2)

            @pl.when(k == 0)
            def _():
                acc_ref[...] = jnp.zeros_like(acc_ref)

            acc_ref[...] += partial()

            @pl.when(k == nk - 1)
            def _():
                o_ref[...] = acc_ref[...].astype(o_ref.dtype)

    return pl.pallas_call(
        body, name=name, grid=grid, in_specs=in_specs, out_specs=out_spec, out_shape=out_shape,
        scratch_shapes=[] if nk == 1 else [pltpu.VMEM(out_block, F32)],
        compiler_params=_params(3),
    )(*operands)


def _tile(n, t):
    if n <= t:
        return n
    return max(k for k in range(SUBLANE, t + 1, SUBLANE) if n % k == 0)


def mm_proj(u, w):
    t, d = u.shape
    nblk, _, nb = w.shape
    tm = _tile(t, 1024)
    return _matmul(
        "mm_proj", "nn", (u, w),
        [pl.BlockSpec((tm, d), lambda j, i, k: (i, 0)), pl.BlockSpec((None, d, nb), lambda j, i, k: (j, 0, 0))],
        jax.ShapeDtypeStruct((t, nblk * nb), F32), pl.BlockSpec((tm, nb), lambda j, i, k: (i, j)),
        (nblk, t // tm, 1), (tm, nb))


def mm_plain(name, a, b, mode, out_dtype):
    if mode == "nn":
        (m, kk), n = a.shape, b.shape[1]
    elif mode == "nt":
        (m, kk), n = a.shape, b.shape[0]
    else:
        (kk, m), n = a.shape, b.shape[1]
    tm, tn = _tile(m, 1024), _tile(n, 1024)
    a_spec = (pl.BlockSpec((kk, tm), lambda i, j, k: (0, i)) if mode == "tn"
              else pl.BlockSpec((tm, kk), lambda i, j, k: (i, 0)))
    b_spec = (pl.BlockSpec((tn, kk), lambda i, j, k: (j, 0)) if mode == "nt"
              else pl.BlockSpec((kk, tn), lambda i, j, k: (0, j)))
    return _matmul(
        name, mode, (a, b), [a_spec, b_spec],
        jax.ShapeDtypeStruct((m, n), out_dtype), pl.BlockSpec((tm, tn), lambda i, j, k: (i, j)),
        (m // tm, n // tn, 1), (tm, tn))


def mm_down(f, w):
    nblk, t, fb = f.shape
    d = w.shape[2]
    tm, tn = _tile(t, 1024), _tile(d, 1024)
    return _matmul(
        "mm_down", "nn", (f, w),
        [pl.BlockSpec((None, tm, fb), lambda i, j, k: (k, i, 0)), pl.BlockSpec((None, fb, tn), lambda i, j, k: (k, 0, j))],
        jax.ShapeDtypeStruct((t, d), F32), pl.BlockSpec((tm, tn), lambda i, j, k: (i, j)),
        (t // tm, d // tn, nblk), (tm, tn))


def mm_dw_rows(name, a, g):
    nblk, t, fb = a.shape
    d = g.shape[1]
    tn = _tile(d, 1024)
    return _matmul(
        name, "tn", (a, g),
        [pl.BlockSpec((None, t, fb), lambda j, n, k: (j, 0, 0)), pl.BlockSpec((t, tn), lambda j, n, k: (0, n))],
        jax.ShapeDtypeStruct((nblk, fb, d), BF16), pl.BlockSpec((None, fb, tn), lambda j, n, k: (j, 0, n)),
        (nblk, d // tn, 1), (fb, tn))


def mm_dw_cols_stacked(name, u, g):
    t, d = u.shape
    nblk, _, fb = g.shape
    tmd = _tile(d, 1024)
    return _matmul(
        name, "tn", (u, g),
        [pl.BlockSpec((t, tmd), lambda j, i, k: (0, i)), pl.BlockSpec((None, t, fb), lambda j, i, k: (j, 0, 0))],
        jax.ShapeDtypeStruct((nblk, d, fb), BF16), pl.BlockSpec((None, tmd, fb), lambda j, i, k: (j, i, 0)),
        (nblk, d // tmd, 1), (tmd, fb))


def mm_dw_cols(name, u, g, nblk):
    t, d = u.shape
    nb = g.shape[1] // nblk
    tmd = _tile(d, 1024)
    return _matmul(
        name, "tn", (u, g),
        [pl.BlockSpec((t, tmd), lambda j, i, k: (0, i)), pl.BlockSpec((t, nb), lambda j, i, k: (0, j))],
        jax.ShapeDtypeStruct((nblk, d, nb), BF16), pl.BlockSpec((None, tmd, nb), lambda j, i, k: (j, i, 0)),
        (nblk, d // tmd, 1), (tmd, nb))


def mm_dx_cols(name, g, w):
    t = g.shape[0]
    nblk, d, nb = w.shape
    tm, tn = _tile(t, 1024), _tile(d, 1024)
    return _matmul(
        name, "nt", (g, w),
        [pl.BlockSpec((tm, nb), lambda i, j, k: (i, k)), pl.BlockSpec((None, tn, nb), lambda i, j, k: (k, j, 0))],
        jax.ShapeDtypeStruct((t, d), F32), pl.BlockSpec((tm, tn), lambda i, j, k: (i, j)),
        (t // tm, d // tn, nblk), (tm, tn))


def mm_dx_ffn(dgt, wg, dup, wu):
    nblk, t, fb = dgt.shape
    d = wg.shape[1]
    tm, tn = _tile(t, 1024), _tile(d, 1024)
    a_spec = pl.BlockSpec((None, tm, fb), lambda i, j, k: (k, i, 0))
    b_spec = pl.BlockSpec((None, tn, fb), lambda i, j, k: (k, j, 0))
    return _matmul(
        "mm_dx_ffn", "nt", (dgt, wg, dup, wu), [a_spec, b_spec, a_spec, b_spec],
        jax.ShapeDtypeStruct((t, d), F32), pl.BlockSpec((tm, tn), lambda i, j, k: (i, j)),
        (t // tm, d // tn, nblk), (tm, tn))


def ffn_up(u, wg, wu):
    t, d = u.shape
    nblk, _, fb = wg.shape
    tm = _tile(t, 512)

    def body(u_ref, wg_ref, wu_ref, gt_ref, up_ref, f_ref):
        uu = u_ref[...]
        gt = _dot(uu, wg_ref[...])
        up = _dot(uu, wu_ref[...])
        gt_ref[...] = gt
        up_ref[...] = up
        f_ref[...] = (gt * _sigmoid(gt) * up).astype(BF16)

    w_spec = pl.BlockSpec((None, d, fb), lambda j, i: (j, 0, 0))
    o_spec = pl.BlockSpec((None, tm, fb), lambda j, i: (j, i, 0))
    return pl.pallas_call(
        body, name="ffn_up", grid=(nblk, t // tm),
        in_specs=[pl.BlockSpec((tm, d), lambda j, i: (i, 0)), w_spec, w_spec],
        out_specs=[o_spec, o_spec, o_spec],
        out_shape=[jax.ShapeDtypeStruct((nblk, t, fb), F32), jax.ShapeDtypeStruct((nblk, t, fb), F32),
                   jax.ShapeDtypeStruct((nblk, t, fb), BF16)],
        compiler_params=_params(2),
    )(u, wg, wu)


def ffn_bwd(dd, wd, gt, up):
    t, d = dd.shape
    nblk, fb, _ = wd.shape
    tm = _tile(t, 512)

    def body(dd_ref, wd_ref, gt_ref, up_ref, dgt_ref, dup_ref):
        df = _dot(dd_ref[...], wd_ref[...], "nt")
        g = gt_ref[...]
        s = _sigmoid(g)
        dgt_ref[...] = (df * up_ref[...] * s * (1.0 + g * (1.0 - s))).astype(BF16)
        dup_ref[...] = (df * g * s).astype(BF16)

    s_spec = pl.BlockSpec((None, tm, fb), lambda j, i: (j, i, 0))
    return pl.pallas_call(
        body, name="ffn_bwd", grid=(nblk, t // tm),
        in_specs=[pl.BlockSpec((tm, d), lambda j, i: (i, 0)), pl.BlockSpec((None, fb, d), lambda j, i: (j, 0, 0)),
                  s_spec, s_spec],
        out_specs=[s_spec, s_spec],
        out_shape=[jax.ShapeDtypeStruct((nblk, t, fb), BF16)] * 2,
        compiler_params=_params(2),
    )(dd, wd, gt, up)


def rms_pre(h, g):
    t, d = h.shape
    tr = _tile(t, ROW_T)

    def body(h_ref, g_ref, o_ref):
        xn, _ = _rms_stats(h_ref[...])
        o_ref[...] = (xn * g_ref[...]).astype(BF16)

    return pl.pallas_call(
        body, name="rms_pre", grid=(t // tr,),
        in_specs=[_row_spec(tr, d), _vec_spec(d)], out_specs=_row_spec(tr, d),
        out_shape=jax.ShapeDtypeStruct((t, d), BF16), compiler_params=_params(1),
    )(h, g)


def res_norm(h, o, g_post, g_pre):
    t, d = h.shape
    tr = _tile(t, ROW_T)

    def body(h_ref, o_ref, gpo_ref, gpr_ref, h2_ref, u_ref):
        on, _ = _rms_stats(o_ref[...])
        h2 = h_ref[...] + on * gpo_ref[...]
        h2_ref[...] = h2
        hn, _ = _rms_stats(h2)
        u_ref[...] = (hn * gpr_ref[...]).astype(BF16)

    return pl.pallas_call(
        body, name="res_norm", grid=(t // tr,),
        in_specs=[_row_spec(tr, d), _row_spec(tr, d), _vec_spec(d), _vec_spec(d)],
        out_specs=[_row_spec(tr, d), _row_spec(tr, d)],
        out_shape=[jax.ShapeDtypeStruct((t, d), F32), jax.ShapeDtypeStruct((t, d), BF16)],
        compiler_params=_params(1),
    )(h, o, g_post, g_pre)


def final_loss(h2, dbr, g_post, target):
    t, d = h2.shape
    tr = _tile(t, ROW_T)

    def body(h2_ref, d_ref, g_ref, tg_ref, loss_ref, dy_ref, dd_ref, dg_ref):
        i = pl.program_id(0)

        @pl.when(i == 0)
        def _():
            loss_ref[...] = jnp.zeros_like(loss_ref)
            dg_ref[...] = jnp.zeros_like(dg_ref)

        x = d_ref[...]
        g = g_ref[...]
        xn, _ = _rms_stats(x)
        diff = h2_ref[...] + xn * g - tg_ref[...]
        loss_ref[...] += jnp.sum(jnp.sum(diff * diff, axis=1, keepdims=True), axis=0, keepdims=True)
        dy = diff * (1.0 / d)
        dy_ref[...] = dy
        dx, dg = _rms_bwd(dy, x, g)
        dd_ref[...] = dx.astype(BF16)
        dg_ref[...] += dg

    return pl.pallas_call(
        body, name="final_loss", grid=(t // tr,),
        in_specs=[_row_spec(tr, d), _row_spec(tr, d), _vec_spec(d), _row_spec(tr, d)],
        out_specs=[pl.BlockSpec((1, 1), lambda i: (0, 0)), _row_spec(tr, d), _row_spec(tr, d), _vec_spec(d)],
        out_shape=[jax.ShapeDtypeStruct((1, 1), F32), jax.ShapeDtypeStruct((t, d), F32),
                   jax.ShapeDtypeStruct((t, d), BF16), jax.ShapeDtypeStruct((1, d), F32)],
        compiler_params=_params(1),
    )(h2, dbr, g_post, target)


def norm_bwd(dh_out, du, h_in, g_pre, prev=None):
    t, d = h_in.shape
    tr = _tile(t, ROW_T)
    with_prev = prev is not None

    def body(*refs):
        if with_prev:
            dho_ref, du_ref, h_ref, gpr_ref, br_ref, gpo_ref, dh_ref, dgpr_ref, dbr_ref, dgpo_ref = refs
        else:
            dho_ref, du_ref, h_ref, gpr_ref, dh_ref, dgpr_ref = refs
        i = pl.program_id(0)

        @pl.when(i == 0)
        def _():
            dgpr_ref[...] = jnp.zeros_like(dgpr_ref)
            if with_prev:
                dgpo_ref[...] = jnp.zeros_like(dgpo_ref)

        dx, dg = _rms_bwd(du_ref[...], h_ref[...], gpr_ref[...])
        dh = dho_ref[...] + dx
        dh_ref[...] = dh
        dgpr_ref[...] += dg
        if with_prev:
            dbr, dg2 = _rms_bwd(dh, br_ref[...], gpo_ref[...])
            dbr_ref[...] = dbr.astype(BF16)
            dgpo_ref[...] += dg2

    row, vec = _row_spec(tr, d), _vec_spec(d)
    in_specs = [row, row, row, vec] + ([row, vec] if with_prev else [])
    out_specs = [row, vec] + ([row, vec] if with_prev else [])
    out_shape = [jax.ShapeDtypeStruct((t, d), F32), jax.ShapeDtypeStruct((1, d), F32)]
    if with_prev:
        out_shape += [jax.ShapeDtypeStruct((t, d), BF16), jax.ShapeDtypeStruct((1, d), F32)]
    args = (dh_out, du, h_in, g_pre) + (tuple(prev) if with_prev else ())
    return pl.pallas_call(
        body, name="norm_bwd_chain" if with_prev else "norm_bwd_first", grid=(t // tr,),
        in_specs=in_specs, out_specs=out_specs, out_shape=out_shape, compiler_params=_params(1),
    )(*args)


def mix_fwd(ya, yc, yl, ga, gc, gl):
    t, wa = ya.shape
    wc, wl = yc.shape[1], yl.shape[1]
    tr = _tile(t, ROW_T)

    def body(ya_ref, yc_ref, yl_ref, ga_ref, gc_ref, gl_ref, o_ref):
        o_ref[:, pl.ds(0, wa)] = (_rms_stats(ya_ref[...])[0] * ga_ref[...]).astype(BF16)
        o_ref[:, pl.ds(wa, wc)] = (_rms_stats(yc_ref[...])[0] * gc_ref[...]).astype(BF16)
        o_ref[:, pl.ds(wa + wc, wl)] = (_rms_stats(yl_ref[...])[0] * gl_ref[...]).astype(BF16)

    return pl.pallas_call(
        body, name="mix_fwd", grid=(t // tr,),
        in_specs=[_row_spec(tr, wa), _row_spec(tr, wc), _row_spec(tr, wl), _vec_spec(wa), _vec_spec(wc), _vec_spec(wl)],
        out_specs=_row_spec(tr, wa + wc + wl),
        out_shape=jax.ShapeDtypeStruct((t, wa + wc + wl), BF16), compiler_params=_params(1),
    )(ya, yc, yl, ga, gc, gl)


def mix_bwd(dmixed, ya, yc, yl, cpre, ga, gc, gl, lng, lnb):
    t, wa = ya.shape
    wc, wl = yc.shape[1], yl.shape[1]
    tr = _tile(t, ROW_T)

    def body(dm_ref, ya_ref, yc_ref, yl_ref, c_ref, ga_ref, gc_ref, gl_ref, lg_ref, lb_ref,
             dya_ref, dc_ref, dyl_ref, dga_ref, dgc_ref, dgl_ref, dlg_ref, dlb_ref):
        i = pl.program_id(0)

        @pl.when(i == 0)
        def _():
            for r in (dga_ref, dgc_ref, dgl_ref, dlg_ref, dlb_ref):
                r[...] = jnp.zeros_like(r)

        dya, dga = _rms_bwd(dm_ref[:, pl.ds(0, wa)], ya_ref[...], ga_ref[...])
        dya_ref[...] = dya
        dga_ref[...] += dga
        dyl, dgl = _rms_bwd(dm_ref[:, pl.ds(wa + wc, wl)], yl_ref[...], gl_ref[...])
        dyl_ref[...] = dyl
        dgl_ref[...] += dgl
        dyc, dgc = _rms_bwd(dm_ref[:, pl.ds(wa, wc)], yc_ref[...], gc_ref[...])
        dgc_ref[...] += dgc
        c = c_ref[...]
        xc = c - jnp.mean(c, axis=-1, keepdims=True)
        rstd = lax.rsqrt(jnp.mean(xc * xc, axis=-1, keepdims=True) + EPS)
        xhat = xc * rstd
        ln = xhat * lg_ref[...] + lb_ref[...]
        s = _sigmoid(ln)
        dln = dyc * s * (1.0 + ln * (1.0 - s))
        dlg_ref[...] += jnp.sum(dln * xhat, axis=0, keepdims=True)
        dlb_ref[...] += jnp.sum(dln, axis=0, keepdims=True)
        dxh = dln * lg_ref[...]
        dc_ref[...] = rstd * (dxh - jnp.mean(dxh, axis=-1, keepdims=True)
                              - xhat * jnp.mean(dxh * xhat, axis=-1, keepdims=True))

    return pl.pallas_call(
        body, name="mix_bwd", grid=(t // tr,),
        in_specs=[_row_spec(tr, wa + wc + wl), _row_spec(tr, wa), _row_spec(tr, wc), _row_spec(tr, wl), _row_spec(tr, wc),
                  _vec_spec(wa), _vec_spec(wc), _vec_spec(wl), _vec_spec(wc), _vec_spec(wc)],
        out_specs=[_row_spec(tr, wa), _row_spec(tr, wc), _row_spec(tr, wl),
                   _vec_spec(wa), _vec_spec(wc), _vec_spec(wl), _vec_spec(wc), _vec_spec(wc)],
        out_shape=[jax.ShapeDtypeStruct((t, wa), F32), jax.ShapeDtypeStruct((t, wc), F32), jax.ShapeDtypeStruct((t, wl), F32),
                   jax.ShapeDtypeStruct((1, wa), F32), jax.ShapeDtypeStruct((1, wc), F32), jax.ShapeDtypeStruct((1, wl), F32),
                   jax.ShapeDtypeStruct((1, wc), F32), jax.ShapeDtypeStruct((1, wc), F32)],
        compiler_params=_params(1),
    )(dmixed, ya, yc, yl, cpre, ga, gc, gl, lng, lnb)


def _hi_lo(x):
    hi = x.astype(BF16)
    return hi, (x - hi.astype(F32)).astype(BF16)


def _att_block(qb, kt, jj, scale, lower, tri_gt):
    z = _dot(qb, kt, "nt") * scale
    sp = _softplus(z)
    mask = jnp.logical_or(jj > 0, lower)
    lk = jnp.where(mask, -sp, 0.0)
    hi, lo = _hi_lo(lk)
    logw = (z - sp) + _dot(hi, tri_gt) + _dot(lo, tri_gt)
    return z, sp, mask, lk, logw


def attn_fwd(proj, n_heads):
    t = proj.shape[0]
    tb = _tile(t, ATT_T)
    nq = t // tb
    scale = HEAD_DIM ** -0.5

    def body(q_ref, k_ref, v_ref, o_ref, kb_ref, vb_ref):
        kb_ref[...] = k_ref[...].astype(BF16)
        vb_ref[...] = v_ref[...].astype(BF16)
        row = lax.broadcasted_iota(jnp.int32, (tb, tb), 0)
        col = lax.broadcasted_iota(jnp.int32, (tb, tb), 1)
        lower = col < row
        tri_gt = (row > col).astype(BF16)

        def qblock(i, _):
            q0 = pl.multiple_of(i * tb, tb)
            qb = q_ref[pl.ds(q0, tb), :].astype(BF16)

            def kblock(jj, carry):
                acc, run = carry
                k0 = pl.multiple_of((i - jj) * tb, tb)
                _, _, mask, lk, logw = _att_block(qb, kb_ref[pl.ds(k0, tb), :], jj, scale, lower, tri_gt)
                w = jnp.where(mask, jnp.exp(logw + run), 0.0)
                acc = acc + _dot(w.astype(BF16), vb_ref[pl.ds(k0, tb), :])
                return acc, run + jnp.sum(lk, axis=1, keepdims=True)

            acc, _ = lax.fori_loop(0, i + 1, kblock, (jnp.zeros((tb, HEAD_DIM), F32), jnp.zeros((tb, 1), F32)))
            o_ref[pl.ds(q0, tb), :] = acc
            return 0

        lax.fori_loop(0, nq, qblock, 0)

    def col_spec(base):
        return pl.BlockSpec((t, HEAD_DIM), lambda h, base=base: (0, base + h))

    return pl.pallas_call(
        body, name="attn_fwd", grid=(n_heads,),
        in_specs=[col_spec(0), col_spec(n_heads), col_spec(2 * n_heads)],
        out_specs=col_spec(0),
        out_shape=jax.ShapeDtypeStruct((t, n_heads * HEAD_DIM), F32),
        scratch_shapes=[pltpu.VMEM((t, HEAD_DIM), BF16), pltpu.VMEM((t, HEAD_DIM), BF16)],
        compiler_params=_params(1),
    )(proj, proj, proj)


def attn_bwd(proj, dy, n_heads):
    t = proj.shape[0]
    tb = _tile(t, ATT_T)
    nq = t // tb
    scale = HEAD_DIM ** -0.5

    def body(q_ref, k_ref, v_ref, dy_ref, dq_ref, dk_ref, dv_ref, qb_ref, kb_ref, vb_ref, dob_ref, dk_acc, dv_acc, run_s):
        qb_ref[...] = q_ref[...].astype(BF16)
        kb_ref[...] = k_ref[...].astype(BF16)
        vb_ref[...] = v_ref[...].astype(BF16)
        dob_ref[...] = dy_ref[...].astype(BF16)
        dk_acc[...] = jnp.zeros_like(dk_acc)
        dv_acc[...] = jnp.zeros_like(dv_acc)
        row = lax.broadcasted_iota(jnp.int32, (tb, tb), 0)
        col = lax.broadcasted_iota(jnp.int32, (tb, tb), 1)
        lower = col < row
        tri_gt = (row > col).astype(BF16)
        tri_lt = (row < col).astype(BF16)

        def qblock(i, _):
            q0 = pl.multiple_of(i * tb, tb)
            qb = qb_ref[pl.ds(q0, tb), :]
            dob = dob_ref[pl.ds(q0, tb), :]

            def sweep(jj, run):
                k0 = pl.multiple_of((i - jj) * tb, tb)
                sp = _softplus(_dot(qb, kb_ref[pl.ds(k0, tb), :], "nt") * scale)
                run_s[i - jj] = run
                return run + jnp.sum(jnp.where(jnp.logical_or(jj > 0, lower), -sp, 0.0), axis=1, keepdims=True)

            lax.fori_loop(0, i + 1, sweep, jnp.zeros((tb, 1), F32))

            def kblock(kbi, carry):
                dq, gsum = carry
                k0 = pl.multiple_of(kbi * tb, tb)
                kt = kb_ref[pl.ds(k0, tb), :]
                vt = vb_ref[pl.ds(k0, tb), :]
                z, sp, mask, _, logw = _att_block(qb, kt, i - kbi, scale, lower, tri_gt)
                w = jnp.where(mask, jnp.exp(logw + run_s[kbi]), 0.0)
                g = w * _dot(dob, vt, "nt")
                hi, lo = _hi_lo(g)
                before = _dot(hi, tri_lt) + _dot(lo, tri_lt) + gsum
                sig = jnp.exp(z - sp)
                dz = jnp.where(mask, g * (1.0 - sig) - before * sig, 0.0) * scale
                dzb = dz.astype(BF16)
                dk_acc[pl.ds(k0, tb), :] += _dot(dzb, qb, "tn")
                dv_acc[pl.ds(k0, tb), :] += _dot(w.astype(BF16), dob, "tn")
                return dq + _dot(dzb, kt), gsum + jnp.sum(g, axis=1, keepdims=True)

            dq, _ = lax.fori_loop(0, i + 1, kblock, (jnp.zeros((tb, HEAD_DIM), F32), jnp.zeros((tb, 1), F32)))
            dq_ref[pl.ds(q0, tb), :] = dq.astype(BF16)
            return 0

        lax.fori_loop(0, nq, qblock, 0)
        dk_ref[...] = dk_acc[...].astype(BF16)
        dv_ref[...] = dv_acc[...].astype(BF16)

    def col_spec(base):
        return pl.BlockSpec((t, HEAD_DIM), lambda h, base=base: (0, base + h))

    width = n_heads * HEAD_DIM
    return pl.pallas_call(
        body, name="attn_bwd", grid=(n_heads,),
        in_specs=[col_spec(0), col_spec(n_heads), col_spec(2 * n_heads), col_spec(0)],
        out_specs=[col_spec(0), col_spec(0), col_spec(0)],
        out_shape=[jax.ShapeDtypeStruct((t, width), BF16)] * 3,
        scratch_shapes=[pltpu.VMEM((t, HEAD_DIM), BF16)] * 4 + [pltpu.VMEM((t, HEAD_DIM), F32)] * 2
        + [pltpu.VMEM((nq, tb, 1), F32)],
        compiler_params=_params(1),
    )(proj, proj, proj, dy)


def _glu_halo(vc, gc, vp, gp, ubuf, i, tt, halo):
    uprev = vp[pl.ds(tt - halo, halo), :] * _sigmoid(gp[pl.ds(tt - halo, halo), :])
    ubuf[pl.ds(0, halo), :] = jnp.where(i > 0, uprev, 0.0)
    ubuf[pl.ds(halo, tt), :] = vc[...] * _sigmoid(gc[...])


def conv_fwd(proj, col0, cc, w, b, lng, lnb):
    t = proj.shape[0]
    tt = _tile(t, ROW_T)
    vi, gi = col0 // cc, col0 // cc + 1
    off = CONV_HALO - (DW_LEN - 1)

    def body(vc, gc, vp, gp, w_ref, b_ref, lg_ref, lb_ref, c_ref, y_ref, ubuf):
        i = pl.program_id(0)
        _glu_halo(vc, gc, vp, gp, ubuf, i, tt, CONV_HALO)
        for ch in range(cc // LANE):
            sl = pl.ds(ch * LANE, LANE)
            acc = jnp.zeros((tt, LANE), F32) + b_ref[:, sl]
            for tap in range(DW_LEN):
                acc = acc + w_ref[pl.ds(tap, 1), sl] * ubuf[pl.ds(off + tap, tt), sl]
            c_ref[:, sl] = acc
        c = c_ref[...]
        xc = c - jnp.mean(c, axis=-1, keepdims=True)
        ln = xc * lax.rsqrt(jnp.mean(xc * xc, axis=-1, keepdims=True) + EPS) * lg_ref[...] + lb_ref[...]
        y_ref[...] = ln * _sigmoid(ln)

    cur = lambda c: pl.BlockSpec((tt, cc), lambda i, c=c: (i, c))
    prev = lambda c: pl.BlockSpec((tt, cc), lambda i, c=c: (jnp.maximum(i - 1, 0), c))
    return pl.pallas_call(
        body, name="conv_fwd", grid=(t // tt,),
        in_specs=[cur(vi), cur(gi), prev(vi), prev(gi), pl.BlockSpec((DW_LEN, cc), lambda i: (0, 0)),
                  _vec_spec(cc), _vec_spec(cc), _vec_spec(cc)],
        out_specs=[_row_spec(tt, cc), _row_spec(tt, cc)],
        out_shape=[jax.ShapeDtypeStruct((t, cc), F32)] * 2,
        scratch_shapes=[pltpu.VMEM((CONV_HALO + tt, cc), F32)],
        compiler_params=_params(1),
    )(proj, proj, proj, proj, w, b, lng, lnb)


def conv_bwd(proj, col0, cc, dc, w):
    t = proj.shape[0]
    tt = _tile(t, ROW_T)
    nt = t // tt
    vi, gi = col0 // cc, col0 // cc + 1
    off = CONV_HALO - (DW_LEN - 1)

    def body(vc, gc, vp, gp, dcc, dcn, w_ref, dvg_ref, dw_ref, db_ref, ubuf, dbuf):
        i = pl.program_id(0)

        @pl.when(i == 0)
        def _():
            dw_ref[...] = jnp.zeros_like(dw_ref)
            db_ref[...] = jnp.zeros_like(db_ref)

        _glu_halo(vc, gc, vp, gp, ubuf, i, tt, CONV_HALO)
        dbuf[pl.ds(0, tt), :] = dcc[...]
        dbuf[pl.ds(tt, CONV_HALO), :] = jnp.where(i < nt - 1, dcn[pl.ds(0, CONV_HALO), :], 0.0)
        db_ref[...] += jnp.sum(dcc[...], axis=0, keepdims=True)
        for ch in range(cc // LANE):
            sl = pl.ds(ch * LANE, LANE)
            dcv = dbuf[pl.ds(0, tt), sl]
            du = jnp.zeros((tt, LANE), F32)
            for tap in range(DW_LEN):
                du = du + w_ref[pl.ds(tap, 1), sl] * dbuf[pl.ds(DW_LEN - 1 - tap, tt), sl]
                dw_ref[pl.ds(tap, 1), sl] += jnp.sum(dcv * ubuf[pl.ds(off + tap, tt), sl], axis=0, keepdims=True)
            s = _sigmoid(gc[:, sl])
            val = vc[:, sl]
            dvg_ref[:, sl] = (du * s).astype(BF16)
            dvg_ref[:, pl.ds(cc + ch * LANE, LANE)] = (du * val * s * (1.0 - s)).astype(BF16)

    cur = lambda c: pl.BlockSpec((tt, cc), lambda i, c=c: (i, c))
    prev = lambda c: pl.BlockSpec((tt, cc), lambda i, c=c: (jnp.maximum(i - 1, 0), c))
    return pl.pallas_call(
        body, name="conv_bwd", grid=(nt,),
        in_specs=[cur(vi), cur(gi), prev(vi), prev(gi), _row_spec(tt, cc),
                  pl.BlockSpec((tt, cc), lambda i: (jnp.minimum(i + 1, nt - 1), 0)),
                  pl.BlockSpec((DW_LEN, cc), lambda i: (0, 0))],
        out_specs=[_row_spec(tt, 2 * cc), pl.BlockSpec((DW_LEN, cc), lambda i: (0, 0)), _vec_spec(cc)],
        out_shape=[jax.ShapeDtypeStruct((t, 2 * cc), BF16), jax.ShapeDtypeStruct((DW_LEN, cc), F32),
                   jax.ShapeDtypeStruct((1, cc), F32)],
        scratch_shapes=[pltpu.VMEM((CONV_HALO + tt, cc), F32), pltpu.VMEM((tt + CONV_HALO, cc), F32)],
        compiler_params=_params(1),
    )(proj, proj, proj, proj, dc, dc, w)


def _lru_gates(xbuf, cw_ref, cb_ref, wa_ref, ba_ref, wi_ref, bi_ref, lam_ref, tt, wl):
    bd = wl // LRU_BLOCKS
    xr = jnp.zeros((tt, wl), F32) + cb_ref[...]
    for tap in range(LRU_LEN):
        xr = xr + cw_ref[pl.ds(tap, 1), :] * xbuf[pl.ds(LRU_HALO - (LRU_LEN - 1) + tap, tt), :]
    xb = xr.astype(BF16)
    ga = jnp.concatenate([_dot(xb[:, n * bd:(n + 1) * bd], wa_ref[n]) for n in range(LRU_BLOCKS)], axis=1) + ba_ref[...]
    gi = jnp.concatenate([_dot(xb[:, n * bd:(n + 1) * bd], wi_ref[n]) for n in range(LRU_BLOCKS)], axis=1) + bi_ref[...]
    r = _sigmoid(ga)
    ig = _sigmoid(gi)
    spl = _softplus(-lam_ref[...])
    log_a = -LRU_C * r * spl
    a = jnp.exp(log_a)
    m = jnp.sqrt(_neg_expm1(2.0 * log_a))
    return xr, xb, r, ig, spl, a, m


def _group_scan(a8, b8, reverse):
    rid = lax.broadcasted_iota(jnp.int32, a8.shape, 0)
    aa, bb = a8, b8
    for dist in (1, 2, 4):
        shift = SUBLANE - dist if reverse else dist
        a_sh = pltpu.roll(aa, shift, 0)
        b_sh = pltpu.roll(bb, shift, 0)
        valid = (rid < SUBLANE - dist) if reverse else (rid >= dist)
        bb = jnp.where(valid, aa * b_sh + bb, bb)
        aa = jnp.where(valid, aa * a_sh, aa)
    return aa, bb


def _pick_row(x8, r):
    rid = lax.broadcasted_iota(jnp.int32, x8.shape, 0)
    return jnp.sum(jnp.where(rid == r, x8, 0.0), axis=0, keepdims=True)


def lru_fwd(proj, col0, wl, cw, cb, wa, ba, wi, bi, lam):
    t = proj.shape[0]
    tt = _tile(t, ROW_T)
    xi, yi = col0 // wl, col0 // wl + 1

    def body(xc, xp, ry, cw_ref, cb_ref, wa_ref, ba_ref, wi_ref, bi_ref, lam_ref, hs_ref, y_ref,
             xbuf, a_s, b_s, hcar):
        i = pl.program_id(0)

        @pl.when(i == 0)
        def _():
            hcar[...] = jnp.zeros_like(hcar)

        xbuf[pl.ds(0, LRU_HALO), :] = jnp.where(i > 0, xp[pl.ds(tt - LRU_HALO, LRU_HALO), :], 0.0)
        xbuf[pl.ds(LRU_HALO, tt), :] = xc[...]
        xr, _, _, ig, _, a, m = _lru_gates(xbuf, cw_ref, cb_ref, wa_ref, ba_ref, wi_ref, bi_ref, lam_ref, tt, wl)
        a_s[...] = a
        b_s[...] = m * ig * xr

        def group(gidx, h):
            r0 = pl.multiple_of(gidx * SUBLANE, SUBLANE)
            aa, bb = _group_scan(a_s[pl.ds(r0, SUBLANE), :], b_s[pl.ds(r0, SUBLANE), :], False)
            h8 = aa * h + bb
            hs_ref[pl.ds(r0, SUBLANE), :] = h8
            return _pick_row(h8, SUBLANE - 1)

        hcar[...] = lax.fori_loop(0, tt // SUBLANE, group, hcar[...])
        gel, _ = _gelu_and_grad(ry[...])
        y_ref[...] = hs_ref[...] * gel

    cur = lambda c: pl.BlockSpec((tt, wl), lambda i, c=c: (i, c))
    full = lambda shape: pl.BlockSpec(shape, lambda i: (0,) * len(shape))
    return pl.pallas_call(
        body, name="lru_fwd", grid=(t // tt,),
        in_specs=[cur(xi), pl.BlockSpec((tt, wl), lambda i: (jnp.maximum(i - 1, 0), xi)), cur(yi),
                  full((LRU_LEN, wl)), _vec_spec(wl), full(wa.shape), _vec_spec(wl), full(wi.shape), _vec_spec(wl),
                  _vec_spec(wl)],
        out_specs=[_row_spec(tt, wl), _row_spec(tt, wl)],
        out_shape=[jax.ShapeDtypeStruct((t, wl), F32)] * 2,
        scratch_shapes=[pltpu.VMEM((LRU_HALO + tt, wl), F32), pltpu.VMEM((tt, wl), F32), pltpu.VMEM((tt, wl), F32),
                        pltpu.VMEM((1, wl), F32)],
        compiler_params=_params(1),
    )(proj, proj, proj, cw, cb, wa, ba, wi, bi, lam)


def lru_bwd(proj, col0, wl, hs, dy, cw, cb, wa, ba, wi, bi, lam):
    t = proj.shape[0]
    tt = _tile(t, ROW_T)
    nt = t // tt
    xi, yi = col0 // wl, col0 // wl + 1
    bd = wl // LRU_BLOCKS

    def body(xc, xp, ry, hc, hp, dy_ref, cw_ref, cb_ref, wa_ref, ba_ref, wi_ref, bi_ref, lam_ref,
             dxy_ref, dcw_ref, dcb_ref, dwa_ref, dba_ref, dwi_ref, dbi_ref, dlam_ref,
             xbuf, hbuf, abuf, e_s, dh_s, dxbuf, dhcar):
        i = pl.program_id(0)
        first = i == 0

        @pl.when(first)
        def _():
            for r in (dcw_ref, dcb_ref, dwa_ref, dba_ref, dwi_ref, dbi_ref, dlam_ref, dhcar):
                r[...] = jnp.zeros_like(r)
            abuf[pl.ds(tt, LRU_HALO), :] = jnp.zeros((LRU_HALO, wl), F32)
            dxbuf[pl.ds(tt, LRU_HALO), :] = jnp.zeros((LRU_HALO, wl), F32)

        has_prev = i < nt - 1
        xbuf[pl.ds(0, LRU_HALO), :] = jnp.where(has_prev, xp[pl.ds(tt - LRU_HALO, LRU_HALO), :], 0.0)
        xbuf[pl.ds(LRU_HALO, tt), :] = xc[...]
        hbuf[pl.ds(0, LRU_HALO), :] = jnp.where(has_prev, hp[pl.ds(tt - LRU_HALO, LRU_HALO), :], 0.0)
        hbuf[pl.ds(LRU_HALO, tt), :] = hc[...]
        xr, xb, r, ig, spl, a, m = _lru_gates(xbuf, cw_ref, cb_ref, wa_ref, ba_ref, wi_ref, bi_ref, lam_ref, tt, wl)
        gel, dgel = _gelu_and_grad(ry[...])
        dyv = dy_ref[...]
        e_s[...] = dyv * gel
        dxy_ref[:, pl.ds(wl, wl)] = (dyv * hc[...] * dgel).astype(BF16)
        abuf[pl.ds(0, tt), :] = a
        a_next = abuf[pl.ds(1, tt), :]
        dh_s[...] = a_next

        def group(it, dh_in):
            r0 = pl.multiple_of((tt // SUBLANE - 1 - it) * SUBLANE, SUBLANE)
            aa, bb = _group_scan(dh_s[pl.ds(r0, SUBLANE), :], e_s[pl.ds(r0, SUBLANE), :], True)
            dh8 = aa * dh_in + bb
            dh_s[pl.ds(r0, SUBLANE), :] = dh8
            return _pick_row(dh8, 0)

        dhcar[...] = lax.fori_loop(0, tt // SUBLANE, group, dhcar[...])
        abuf[pl.ds(tt, LRU_HALO), :] = a[0:LRU_HALO, :]
        dh = dh_s[...]
        h_m1 = hbuf[pl.ds(LRU_HALO - 1, tt), :]
        dlog_a = dh * h_m1 * a - dh * ig * xr * (a * a / m)
        dig = dh * m * xr
        dxr = dh * m * ig
        dga = dlog_a * (-LRU_C) * spl * r * (1.0 - r)
        dgi = dig * ig * (1.0 - ig)
        dlam_ref[...] += jnp.sum(dlog_a * r, axis=0, keepdims=True) * (LRU_C * _sigmoid(-lam_ref[...]))
        dba_ref[...] += jnp.sum(dga, axis=0, keepdims=True)
        dbi_ref[...] += jnp.sum(dgi, axis=0, keepdims=True)
        dgab = dga.astype(BF16)
        dgib = dgi.astype(BF16)
        back = []
        for n in range(LRU_BLOCKS):
            sl = slice(n * bd, (n + 1) * bd)
            dwa_ref[n] += _dot(xb[:, sl], dgab[:, sl], "tn")
            dwi_ref[n] += _dot(xb[:, sl], dgib[:, sl], "tn")
            back.append(_dot(dgab[:, sl], wa_ref[n], "nt") + _dot(dgib[:, sl], wi_ref[n], "nt"))
        dxr = dxr + jnp.concatenate(back, axis=1)
        dcb_ref[...] += jnp.sum(dxr, axis=0, keepdims=True)
        dxbuf[pl.ds(0, tt), :] = dxr
        drx = jnp.zeros((tt, wl), F32)
        for tap in range(LRU_LEN):
            drx = drx + cw_ref[pl.ds(tap, 1), :] * dxbuf[pl.ds(LRU_LEN - 1 - tap, tt), :]
            dcw_ref[pl.ds(tap, 1), :] += jnp.sum(
                dxr * xbuf[pl.ds(LRU_HALO - (LRU_LEN - 1) + tap, tt), :], axis=0, keepdims=True)
        dxbuf[pl.ds(tt, LRU_HALO), :] = dxr[0:LRU_HALO, :]
        dxy_ref[:, pl.ds(0, wl)] = drx.astype(BF16)

    rev = lambda c: pl.BlockSpec((tt, wl), lambda i, c=c: (nt - 1 - i, c))
    rev_prev = lambda c: pl.BlockSpec((tt, wl), lambda i, c=c: (jnp.maximum(nt - 2 - i, 0), c))
    full = lambda shape: pl.BlockSpec(shape, lambda i: (0,) * len(shape))
    vec = _vec_spec(wl)
    return pl.pallas_call(
        body, name="lru_bwd", grid=(nt,),
        in_specs=[rev(xi), rev_prev(xi), rev(yi), rev(0), rev_prev(0), rev(0),
                  full((LRU_LEN, wl)), vec, full(wa.shape), vec, full(wi.shape), vec, vec],
        out_specs=[pl.BlockSpec((tt, 2 * wl), lambda i: (nt - 1 - i, 0)), full((LRU_LEN, wl)), vec,
                   full(wa.shape), vec, full(wi.shape), vec, vec],
        out_shape=[jax.ShapeDtypeStruct((t, 2 * wl), BF16), jax.ShapeDtypeStruct((LRU_LEN, wl), F32),
                   jax.ShapeDtypeStruct((1, wl), F32), jax.ShapeDtypeStruct(wa.shape, F32),
                   jax.ShapeDtypeStruct((1, wl), F32), jax.ShapeDtypeStruct(wi.shape, F32),
                   jax.ShapeDtypeStruct((1, wl), F32), jax.ShapeDtypeStruct((1, wl), F32)],
        scratch_shapes=[pltpu.VMEM((LRU_HALO + tt, wl), F32), pltpu.VMEM((LRU_HALO + tt, wl), F32),
                        pltpu.VMEM((tt + LRU_HALO, wl), F32), pltpu.VMEM((tt, wl), F32), pltpu.VMEM((tt, wl), F32),
                        pltpu.VMEM((tt + LRU_HALO, wl), F32), pltpu.VMEM((1, wl), F32)],
        compiler_params=_params(1),
    )(proj, proj, proj, hs, hs, dy, cw, cb, wa, ba, wi, bi, lam)


def _adamw(w, g, m, v):
    m = ADAM_B1 * m + (1.0 - ADAM_B1) * g
    v = ADAM_B2 * v + (1.0 - ADAM_B2) * (g * g)
    m_hat = m / (1.0 - ADAM_B1 ** ADAM_STEP)
    v_hat = v / (1.0 - ADAM_B2 ** ADAM_STEP)
    delta = -ADAM_LR * (m_hat / (jnp.sqrt(v_hat) + ADAM_EPS) + ADAM_WD * w)
    return delta, m, v


def adam_big(name, w, m, v, parts):
    n_layers, rows, cols = w.shape
    tr = _tile(rows, 128 if cols > 1024 else 256)
    nrt = rows // tr

    def body(*refs):
        w_ref, m_ref, v_ref = refs[:3]
        part_refs = refs[3:3 + 4 * n_layers]
        g_ref, d_ref, mo_ref, vo_ref = refs[3 + 4 * n_layers:]
        layer = pl.program_id(0)
        for l in range(n_layers):
            @pl.when(layer == l)
            def _(l=l):
                g = part_refs[4 * l][...].astype(F32)
                for p in range(1, 4):
                    g = g + part_refs[4 * l + p][...].astype(F32)
                delta, mn, vn = _adamw(w_ref[...], g, m_ref[...], v_ref[...])
                g_ref[...] = g
                d_ref[...] = delta
                mo_ref[...] = mn
                vo_ref[...] = vn

    wspec = pl.BlockSpec((None, tr, cols), lambda l, i: (l, i, 0))
    operands, in_specs = [w, m, v], [wspec, wspec, wspec]
    for l in range(n_layers):
        mine, recv = parts[l]
        operands.append(mine)
        in_specs.append(pl.BlockSpec((tr, cols), lambda ll, i, l=l: (jnp.where(ll == l, i, 0), 0)))
        for p in range(3):
            operands.append(recv)
            in_specs.append(pl.BlockSpec((None, tr, cols), lambda ll, i, l=l, p=p: (p, jnp.where(ll == l, i, 0), 0)))
    return pl.pallas_call(
        body, name=name, grid=(n_layers, nrt), in_specs=in_specs, out_specs=[wspec] * 4,
        out_shape=[jax.ShapeDtypeStruct(w.shape, F32)] * 4, compiler_params=_params(2),
    )(*operands)


def adam_small(w, m, v, g):
    rows = w.shape[0]
    tr = _tile(rows, PACK_ROWS)

    def body(w_ref, m_ref, v_ref, g_ref, d_ref, mo_ref, vo_ref):
        delta, mn, vn = _adamw(w_ref[...], g_ref[...], m_ref[...], v_ref[...])
        d_ref[...] = delta
        mo_ref[...] = mn
        vo_ref[...] = vn

    spec = _row_spec(tr, LANE)
    return pl.pallas_call(
        body, name="adam_small", grid=(rows // tr,), in_specs=[spec] * 4, out_specs=[spec] * 3,
        out_shape=[jax.ShapeDtypeStruct(w.shape, F32)] * 3, compiler_params=_params(1),
    )(w, m, v, g)


def sum_parts(parts):
    _, rows, _ = parts.shape
    tr = _tile(rows, PACK_ROWS)

    def body(p_ref, o_ref):
        acc = p_ref[0]
        for k in range(1, N_DEV):
            acc = acc + p_ref[k]
        o_ref[...] = acc

    return pl.pallas_call(
        body, name="sum_parts", grid=(rows // tr,),
        in_specs=[pl.BlockSpec((N_DEV, tr, LANE), lambda i: (0, i, 0))], out_specs=_row_spec(tr, LANE),
        out_shape=jax.ShapeDtypeStruct((rows, LANE), F32), compiler_params=_params(1),
    )(parts)


def pair_add(a, b):
    n, rows, cols = a.shape
    tr = _tile(rows, 256)

    def body(a_ref, b_ref, o_ref):
        o_ref[...] = (a_ref[...].astype(F32) + b_ref[...].astype(F32)).astype(BF16)

    spec = pl.BlockSpec((None, tr, cols), lambda k, i: (k, i, 0))
    return pl.pallas_call(
        body, name="pair_add", grid=(n, rows // tr), in_specs=[spec, spec], out_specs=spec,
        out_shape=jax.ShapeDtypeStruct(a.shape, BF16), compiler_params=_params(2),
    )(a, b)


_HBM = pl.BlockSpec(memory_space=pltpu.HBM)


def _place():
    return lax.axis_index("x"), lax.axis_index("y"), lax.axis_index("c")


def _other_chips(x, y):
    return [(1 - x, y), (x, 1 - y), (1 - x, 1 - y)]


def all_gather(name, shard):
    def body(x_ref, out_ref, send_sems, recv_sems, local_sem):
        x, y, c = _place()
        me, sibling = (x, y, c), (x, y, 1 - c)
        chips = _other_chips(x, y)

        def slot(p):
            return out_ref.at[4 * p[0] + 2 * p[1] + p[2]]

        def copy(k, block, to, src=None):
            return pltpu.make_async_remote_copy(
                src_ref=slot(block) if src is None else src, dst_ref=slot(block),
                send_sem=send_sems.at[k], recv_sem=recv_sems.at[k],
                device_id=to, device_id_type=pl.DeviceIdType.MESH)

        mine = pltpu.make_async_copy(x_ref, slot(me), local_sem)
        mine.start()
        first = [copy(0, me, sibling, src=x_ref)]
        first += [copy(1 + j, me, (*chip, c), src=x_ref) for j, chip in enumerate(chips)]
        for cp in first:
            cp.start()
        passed = [copy(4 + j, (*chip, c), sibling) for j, chip in enumerate(chips)]
        for j, chip in enumerate(chips):
            copy(1 + j, (*chip, c), me).wait_recv()
            passed[j].start()
        copy(0, sibling, me).wait_recv()
        for j, chip in enumerate(chips):
            copy(4 + j, (*chip, 1 - c), me).wait_recv()
        for cp in first + passed:
            cp.wait_send()
        mine.wait()

    return pl.pallas_call(
        body, name=name, out_shape=jax.ShapeDtypeStruct((N_DEV,) + shard.shape, shard.dtype),
        in_specs=[_HBM], out_specs=_HBM,
        scratch_shapes=[pltpu.SemaphoreType.DMA((7,)), pltpu.SemaphoreType.DMA((7,)), pltpu.SemaphoreType.DMA(())],
    )(shard)


def scatter_pair(name, g):
    _, rows, cols = g.shape

    def body(g_ref, keep_ref, recv_ref, send_sems, recv_sems, local_sems):
        x, y, c = _place()
        local = [pltpu.make_async_copy(g_ref.at[2 * k + c], keep_ref.at[k], local_sems.at[k]) for k in range(4)]
        remote = [pltpu.make_async_remote_copy(
            src_ref=g_ref.at[2 * k + 1 - c], dst_ref=recv_ref.at[k], send_sem=send_sems.at[k], recv_sem=recv_sems.at[k],
            device_id=(x, y, 1 - c), device_id_type=pl.DeviceIdType.MESH) for k in range(4)]
        for cp in remote + local:
            cp.start()
        for cp in remote + local:
            cp.wait()

    shape = jax.ShapeDtypeStruct((4, rows, cols), g.dtype)
    return pl.pallas_call(
        body, name=name, out_shape=[shape, shape], in_specs=[_HBM], out_specs=[_HBM, _HBM],
        scratch_shapes=[pltpu.SemaphoreType.DMA((4,)), pltpu.SemaphoreType.DMA((4,)), pltpu.SemaphoreType.DMA((4,))],
    )(g)


def scatter_chips(name, p):
    _, rows, cols = p.shape

    def body(p_ref, mine_ref, recv_ref, send_sems, recv_sems, local_sem):
        x, y, c = _place()
        local = pltpu.make_async_copy(p_ref.at[2 * x + y], mine_ref, local_sem)
        remote = [pltpu.make_async_remote_copy(
            src_ref=p_ref.at[2 * px + py], dst_ref=recv_ref.at[k], send_sem=send_sems.at[k], recv_sem=recv_sems.at[k],
            device_id=(px, py, c), device_id_type=pl.DeviceIdType.MESH) for k, (px, py) in enumerate(_other_chips(x, y))]
        for cp in remote + [local]:
            cp.start()
        for cp in remote + [local]:
            cp.wait()

    return pl.pallas_call(
        body, name=name,
        out_shape=[jax.ShapeDtypeStruct((rows, cols), p.dtype), jax.ShapeDtypeStruct((3, rows, cols), p.dtype)],
        in_specs=[_HBM], out_specs=[_HBM, _HBM],
        scratch_shapes=[pltpu.SemaphoreType.DMA((3,)), pltpu.SemaphoreType.DMA((3,)), pltpu.SemaphoreType.DMA(())],
    )(p)


def reduce_scatter(tag, g):
    keep, recv = scatter_pair("rs_pair_" + tag, g)
    return scatter_chips("rs_chips_" + tag, pair_add(keep, recv))


_SMALL = ("g_pre_mix", "g_post_mix", "g_pre_ffn", "g_post_ffn", "g_attn_grp", "g_conv_grp", "g_lru_grp",
          "dw_conv_w", "dw_conv_b", "conv_ln_g", "conv_ln_b", "lru_conv_w", "lru_conv_b",
          "lru_w_a", "lru_b_a", "lru_w_i", "lru_b_i", "lru_lambda")
_COL_SHARDED_SMALL = ("dw_conv_w", "lru_conv_w")
_BIG = ("w_in", "w_out", "w_gate", "w_up", "w_down")
_ALL = ("w_in", "w_out", "g_pre_mix", "g_post_mix", "g_pre_ffn", "g_post_ffn", "g_attn_grp", "g_conv_grp", "g_lru_grp",
        "dw_conv_w", "dw_conv_b", "conv_ln_g", "conv_ln_b", "lru_conv_w", "lru_conv_b", "lru_w_a", "lru_b_a",
        "lru_w_i", "lru_b_i", "lru_lambda", "w_gate", "w_up", "w_down")


def _pack(arrays):
    flat = jnp.concatenate([a.reshape(-1) for a in arrays])
    pad = (-flat.shape[0]) % (PACK_ROWS * LANE)
    return jnp.pad(flat, (0, pad)).reshape(-1, LANE)


def _unpack(packed, shapes):
    flat = packed.reshape(-1)
    out, pos = [], 0
    for s in shapes:
        n = math.prod(s)
        out.append(flat[pos:pos + n].reshape(s))
        pos += n
    return out


def kernel(x, w_in, w_out, g_pre_mix, g_post_mix, g_pre_ffn, g_post_ffn, g_attn_grp, g_conv_grp, g_lru_grp, dw_conv_w, dw_conv_b, conv_ln_g, conv_ln_b, lru_conv_w, lru_conv_b, lru_w_a, lru_b_a, lru_w_i, lru_b_i, lru_lambda, w_gate, w_up, w_down, loss_target, m_w_in, m_w_out, m_g_pre_mix, m_g_post_mix, m_g_pre_ffn, m_g_post_ffn, m_g_attn_grp, m_g_conv_grp, m_g_lru_grp, m_dw_conv_w, m_dw_conv_b, m_conv_ln_g, m_conv_ln_b, m_lru_conv_w, m_lru_conv_b, m_lru_w_a, m_lru_b_a, m_lru_w_i, m_lru_b_i, m_lru_lambda, m_w_gate, m_w_up, m_w_down, v_w_in, v_w_out, v_g_pre_mix, v_g_post_mix, v_g_pre_ffn, v_g_post_ffn, v_g_attn_grp, v_g_conv_grp, v_g_lru_grp, v_dw_conv_w, v_dw_conv_b, v_conv_ln_g, v_conv_ln_b, v_lru_conv_w, v_lru_conv_b, v_lru_w_a, v_lru_b_a, v_lru_w_i, v_lru_b_i, v_lru_lambda, v_w_gate, v_w_up, v_w_down):
    env = dict(locals())
    wts = {n: env[n] for n in _ALL}
    mom = {n: env["m_" + n] for n in _ALL}
    var = {n: env["v_" + n] for n in _ALL}

    depth = w_in.shape[0]
    h = x[0]
    target = loss_target[0]
    t, d = h.shape
    attn_w = d // 2
    n_heads = attn_w // HEAD_DIM
    cc = d // 4
    wl = d // 4
    conv_col, lru_col = 3 * attn_w, 3 * attn_w + 2 * cc
    me = 4 * lax.axis_index("x") + 2 * lax.axis_index("y") + lax.axis_index("c")

    taps = all_gather("ag_taps", jnp.concatenate([dw_conv_w, lru_conv_w], axis=1))
    taps = jnp.moveaxis(taps, 0, 2).reshape(depth, DW_LEN + LRU_LEN, cc)
    dw_full, lcw_full = taps[:, :DW_LEN], taps[:, DW_LEN:]

    def vec(a, l):
        return a[l].reshape(1, -1)

    def gathered(name, l):
        return all_gather("ag_" + name, wts[name][l].astype(BF16))

    saved = []
    u1 = rms_pre(h, vec(g_pre_mix, 0))
    loss_sum = dh = dbr = None
    for l in range(depth):
        wg = {n: gathered(n, l) for n in _BIG}
        wg["w_out"] = wg["w_out"].reshape(attn_w + cc + wl, d)
        wa_b, wi_b = lru_w_a[l].astype(BF16), lru_w_i[l].astype(BF16)
        proj = mm_proj(u1, wg["w_in"])
        y_attn = attn_fwd(proj, n_heads)
        cpre, y_conv = conv_fwd(proj, conv_col, cc, dw_full[l], vec(dw_conv_b, l), vec(conv_ln_g, l), vec(conv_ln_b, l))
        hs, y_lru = lru_fwd(proj, lru_col, wl, lcw_full[l], vec(lru_conv_b, l), wa_b, vec(lru_b_a, l), wi_b,
                            vec(lru_b_i, l), vec(lru_lambda, l))
        mixed = mix_fwd(y_attn, y_conv, y_lru, vec(g_attn_grp, l), vec(g_conv_grp, l), vec(g_lru_grp, l))
        o = mm_plain("mm_out", mixed, wg["w_out"], "nn", F32)
        h2, u2 = res_norm(h, o, vec(g_post_mix, l), vec(g_pre_ffn, l))
        gt, up, f = ffn_up(u2, wg["w_gate"], wg["w_up"])
        dn = mm_down(f, wg["w_down"])
        saved.append(dict(wg=wg, wa_b=wa_b, wi_b=wi_b, h=h, u1=u1, proj=proj, y_attn=y_attn, cpre=cpre, y_conv=y_conv,
                          hs=hs, y_lru=y_lru, mixed=mixed, o=o, h2=h2, u2=u2, gt=gt, up=up, f=f, dn=dn))
        if l + 1 < depth:
            h, u1 = res_norm(h2, dn, vec(g_post_ffn, l), vec(g_pre_mix, l + 1))
        else:
            loss_sum, dh, dbr, dg_post_ffn = final_loss(h2, dn, vec(g_post_ffn, l), target)

    loss = lax.psum(0.5 * loss_sum[0, 0] / d, MESH_AXES)

    small = {n: [None] * depth for n in _SMALL}
    big_parts = {n: [None] * depth for n in _BIG}
    for l in reversed(range(depth)):
        s = saved[l]
        wg = s["wg"]
        small["g_post_ffn"][l] = dg_post_ffn
        dgt, dup = ffn_bwd(dbr, wg["w_down"], s["gt"], s["up"])
        big_parts["w_down"][l] = reduce_scatter("down", mm_dw_rows("mm_dw_down", s["f"], dbr))
        big_parts["w_gate"][l] = reduce_scatter("gate", mm_dw_cols_stacked("mm_dw_gate", s["u2"], dgt))
        big_parts["w_up"][l] = reduce_scatter("gate", mm_dw_cols_stacked("mm_dw_gate", s["u2"], dup))
        du2 = mm_dx_ffn(dgt, wg["w_gate"], dup, wg["w_up"])
        dh2, small["g_pre_ffn"][l], do, small["g_post_mix"][l] = norm_bwd(
            dh, du2, s["h2"], vec(g_pre_ffn, l), (s["o"], vec(g_post_mix, l)))
        dmixed = mm_plain("mm_dmixed", do, wg["w_out"], "nt", F32)
        dw_out = mm_plain("mm_dw_out", s["mixed"], do, "tn", BF16)
        big_parts["w_out"][l] = reduce_scatter("out", dw_out.reshape(N_DEV, -1, d))
        (dya, dc, dyl, small["g_attn_grp"][l], small["g_conv_grp"][l], small["g_lru_grp"][l],
         small["conv_ln_g"][l], small["conv_ln_b"][l]) = mix_bwd(
            dmixed, s["y_attn"], s["y_conv"], s["y_lru"], s["cpre"], vec(g_attn_grp, l), vec(g_conv_grp, l),
            vec(g_lru_grp, l), vec(conv_ln_g, l), vec(conv_ln_b, l))
        dq, dk, dv = attn_bwd(s["proj"], dya, n_heads)
        dvg, small["dw_conv_w"][l], small["dw_conv_b"][l] = conv_bwd(s["proj"], conv_col, cc, dc, dw_full[l])
        (dxy, small["lru_conv_w"][l], small["lru_conv_b"][l], small["lru_w_a"][l], small["lru_b_a"][l],
         small["lru_w_i"][l], small["lru_b_i"][l], small["lru_lambda"][l]) = lru_bwd(
            s["proj"], lru_col, wl, s["hs"], dyl, lcw_full[l], vec(lru_conv_b, l), s["wa_b"], vec(lru_b_a, l),
            s["wi_b"], vec(lru_b_i, l), vec(lru_lambda, l))
        dproj = jnp.concatenate([dq, dk, dv, dvg, dxy], axis=1)
        big_parts["w_in"][l] = reduce_scatter("in", mm_dw_cols("mm_dw_in", s["u1"], dproj, N_DEV))
        du1 = mm_dx_cols("mm_dx_in", dproj, wg["w_in"])
        if l > 0:
            p = saved[l - 1]
            dh, small["g_pre_mix"][l], dbr, dg_post_ffn = norm_bwd(
                dh2, du1, s["h"], vec(g_pre_mix, l), (p["dn"], vec(g_post_ffn, l - 1)))
        else:
            dh, small["g_pre_mix"][l] = norm_bwd(dh2, du1, s["h"], vec(g_pre_mix, l))
    grad_x = dh[None]

    small_shapes = [(depth,) + tuple(wts[n].shape[1:]) if n not in _COL_SHARDED_SMALL
                    else (depth, wts[n].shape[1], cc) for n in _SMALL]
    part = _pack([jnp.stack([a.reshape(shp[1:]) for a in small[n]]) for n, shp in zip(_SMALL, small_shapes)])
    g_small = _unpack(sum_parts(all_gather("ag_small", part)), small_shapes)
    grads = {}
    for n, g in zip(_SMALL, g_small):
        if n in _COL_SHARDED_SMALL:
            g = lax.dynamic_slice_in_dim(g, me * (cc // N_DEV), cc // N_DEV, axis=2)
        grads[n] = g
    local_shapes = [tuple(wts[n].shape) for n in _SMALL]
    d_small, m_small, v_small = adam_small(
        _pack([wts[n] for n in _SMALL]), _pack([mom[n] for n in _SMALL]), _pack([var[n] for n in _SMALL]),
        _pack([grads[n] for n in _SMALL]))
    delta = dict(zip(_SMALL, _unpack(d_small, local_shapes)))
    new_m = dict(zip(_SMALL, _unpack(m_small, local_shapes)))
    new_v = dict(zip(_SMALL, _unpack(v_small, local_shapes)))

    for n in _BIG:
        shape = wts[n].shape
        rows, cols = big_parts[n][0][0].shape
        view = (depth, rows, cols)
        g, dl, mn, vn = adam_big("adam_" + n, wts[n].reshape(view), mom[n].reshape(view), var[n].reshape(view),
                                 big_parts[n])
        grads[n], delta[n], new_m[n], new_v[n] = (a.reshape(shape) for a in (g, dl, mn, vn))

    return (loss, grad_x, *[grads[n] for n in _ALL], *[delta[n] for n in _ALL],
            *[new_m[n] for n in _ALL], *[new_v[n] for n in _ALL])
```

```python
import functools
import math

import jax
import jax.numpy as jnp
from jax import lax
from jax.experimental import pallas as pl
from jax.experimental.pallas import tpu as pltpu

F32 = jnp.float32
BF16 = jnp.bfloat16

N_DEV = 8
EPS = 1e-6
HEAD_DIM = 128
DW_LEN = 31
LRU_LEN = 4
LRU_BLOCKS = 4
LRU_C = 8.0
ATT_T = 256
ROW_T = 256
CONV_HALO = 32
LRU_HALO = 8
LANE = 128
SUBLANE = 8
PACK_ROWS = 512
VMEM_LIMIT = 56 * 1024 * 1024

ADAM_LR = 0.001
ADAM_B1 = 0.9
ADAM_B2 = 0.999
ADAM_EPS = 1e-08
ADAM_WD = 0.01
ADAM_STEP = 10

MESH_AXES = ("x", "y", "c")
_DIMS = {
    "nn": (((1,), (0,)), ((), ())),
    "nt": (((1,), (1,)), ((), ())),
    "tn": (((0,), (0,)), ((), ())),
}


def _params(n_axes):
    return pltpu.CompilerParams(
        dimension_semantics=("arbitrary",) * n_axes, vmem_limit_bytes=VMEM_LIMIT)


def _dot(a, b, mode="nn"):
    return lax.dot_general(a, b, _DIMS[mode], preferred_element_type=F32)


def _sigmoid(x):
    return 1.0 / (1.0 + jnp.exp(-x))


def _softplus(x):
    return jnp.maximum(x, 0.0) + jnp.log(1.0 + jnp.exp(-jnp.abs(x)))


def _neg_expm1(x):
    series = x * (1.0 + x * (0.5 + x * (1.0 / 6 + x * (1.0 / 24 + x * (1.0 / 120 + x * (1.0 / 720))))))
    return jnp.where(x > -0.25, -series, 1.0 - jnp.exp(x))


_GELU_C = math.sqrt(2.0 / math.pi)


def _gelu_and_grad(x):
    inner = _GELU_C * (x + 0.044715 * x * x * x)
    t = jnp.tanh(inner)
    val = 0.5 * x * (1.0 + t)
    grad = 0.5 * (1.0 + t) + 0.5 * x * (1.0 - t * t) * _GELU_C * (1.0 + 3 * 0.044715 * x * x)
    return val, grad


def _rms_stats(x):
    r = lax.rsqrt(jnp.mean(x * x, axis=-1, keepdims=True) + EPS)
    return x * r, r


def _rms_bwd(dy, x, g):
    xn, r = _rms_stats(x)
    dxn = dy * g
    dx = r * (dxn - xn * jnp.mean(dxn * xn, axis=-1, keepdims=True))
    return dx, jnp.sum(dy * xn, axis=0, keepdims=True)


def _row_spec(tr, width, col=0):
    return pl.BlockSpec((tr, width), lambda i, col=col: (i, col))


def _vec_spec(width):
    return pl.BlockSpec((1, width), lambda i: (0, 0))


def _matmul(name, mode, operands, in_specs, out_shape, out_spec, grid, out_block):
    npairs = len(operands) // 2
    nk = grid[2]

    def body(*refs):
        o_ref = refs[2 * npairs]

        def partial():
            acc = None
            for p in range(npairs):
                d = _dot(refs[2 * p][...], refs[2 * p + 1][...], mode)
                acc = d if acc is None else acc + d
            return acc

        if nk == 1:
            o_ref[...] = partial().astype(o_ref.dtype)
        else:
            acc_ref = refs[2 * npairs + 1]
            k = pl.program_id(2)

            @pl.when(k == 0)
            def _():
                acc_ref[...] = jnp.zeros_like(acc_ref)

            acc_ref[...] += partial()

            @pl.when(k == nk - 1)
            def _():
                o_ref[...] = acc_ref[...].astype(o_ref.dtype)

    return pl.pallas_call(
        body, name=name, grid=grid, in_specs=in_specs, out_specs=out_spec, out_shape=out_shape,
        scratch_shapes=[] if nk == 1 else [pltpu.VMEM(out_block, F32)],
        compiler_params=_params(3),
    )(*operands)


def _tile(n, t):
    if n <= t:
        return n
    return max(k for k in range(SUBLANE, t + 1, SUBLANE) if n % k == 0)


def mm_proj(u, w):
    t, d = u.shape
    nblk, _, nb = w.shape
    tm = _tile(t, 1024)
    return _matmul(
        "mm_proj", "nn", (u, w),
        [pl.BlockSpec((tm, d), lambda j, i, k: (i, 0)), pl.BlockSpec((None, d, nb), lambda j, i, k: (j, 0, 0))],
        jax.ShapeDtypeStruct((t, nblk * nb), F32), pl.BlockSpec((tm, nb), lambda j, i, k: (i, j)),
        (nblk, t // tm, 1), (tm, nb))


def mm_plain(name, a, b, mode, out_dtype):
    if mode == "nn":
        (m, kk), n = a.shape, b.shape[1]
    elif mode == "nt":
        (m, kk), n = a.shape, b.shape[0]
    else:
        (kk, m), n = a.shape, b.shape[1]
    tm, tn = _tile(m, 1024), _tile(n, 1024)
    a_spec = (pl.BlockSpec((kk, tm), lambda i, j, k: (0, i)) if mode == "tn"
              else pl.BlockSpec((tm, kk), lambda i, j, k: (i, 0)))
    b_spec = (pl.BlockSpec((tn, kk), lambda i, j, k: (j, 0)) if mode == "nt"
              else pl.BlockSpec((kk, tn), lambda i, j, k: (0, j)))
    return _matmul(
        name, mode, (a, b), [a_spec, b_spec],
        jax.ShapeDtypeStruct((m, n), out_dtype), pl.BlockSpec((tm, tn), lambda i, j, k: (i, j)),
        (m // tm, n // tn, 1), (tm, tn))


def mm_down(f, w):
    nblk, t, fb = f.shape
    d = w.shape[2]
    tm, tn = _tile(t, 1024), _tile(d, 1024)
    return _matmul(
        "mm_down", "nn", (f, w),
        [pl.BlockSpec((None, tm, fb), lambda i, j, k: (k, i, 0)), pl.BlockSpec((None, fb, tn), lambda i, j, k: (k, 0, j))],
        jax.ShapeDtypeStruct((t, d), F32), pl.BlockSpec((tm, tn), lambda i, j, k: (i, j)),
        (t // tm, d // tn, nblk), (tm, tn))


def mm_dw_rows(name, a, g):
    nblk, t, fb = a.shape
    d = g.shape[1]
    tn = _tile(d, 1024)
    return _matmul(
        name, "tn", (a, g),
        [pl.BlockSpec((None, t, fb), lambda j, n, k: (j, 0, 0)), pl.BlockSpec((t, tn), lambda j, n, k: (0, n))],
        jax.ShapeDtypeStruct((nblk, fb, d), BF16), pl.BlockSpec((None, fb, tn), lambda j, n, k: (j, 0, n)),
        (nblk, d // tn, 1), (fb, tn))


def mm_dw_cols_stacked(name, u, g):
    t, d = u.shape
    nblk, _, fb = g.shape
    tmd = _tile(d, 1024)
    return _matmul(
        name, "tn", (u, g),
        [pl.BlockSpec((t, tmd), lambda j, i, k: (0, i)), pl.BlockSpec((None, t, fb), lambda j, i, k: (j, 0, 0))],
        jax.ShapeDtypeStruct((nblk, d, fb), BF16), pl.BlockSpec((None, tmd, fb), lambda j, i, k: (j, i, 0)),
        (nblk, d // tmd, 1), (tmd, fb))


def mm_dw_cols(name, u, g, nblk):
    t, d = u.shape
    nb = g.shape[1] // nblk
    tmd = _tile(d, 1024)
    return _matmul(
        name, "tn", (u, g),
        [pl.BlockSpec((t, tmd), lambda j, i, k: (0, i)), pl.BlockSpec((t, nb), lambda j, i, k: (0, j))],
        jax.ShapeDtypeStruct((nblk, d, nb), BF16), pl.BlockSpec((None, tmd, nb), lambda j, i, k: (j, i, 0)),
        (nblk, d // tmd, 1), (tmd, nb))


def mm_dx_cols(name, g, w):
    t = g.shape[0]
    nblk, d, nb = w.shape
    tm, tn = _tile(t, 1024), _tile(d, 1024)
    return _matmul(
        name, "nt", (g, w),
        [pl.BlockSpec((tm, nb), lambda i, j, k: (i, k)), pl.BlockSpec((None, tn, nb), lambda i, j, k: (k, j, 0))],
        jax.ShapeDtypeStruct((t, d), F32), pl.BlockSpec((tm, tn), lambda i, j, k: (i, j)),
        (t // tm, d // tn, nblk), (tm, tn))


def mm_dx_ffn(dgt, wg, dup, wu):
    nblk, t, fb = dgt.shape
    d = wg.shape[1]
    tm, tn = _tile(t, 1024), _tile(d, 1024)
    a_spec = pl.BlockSpec((None, tm, fb), lambda i, j, k: (k, i, 0))
    b_spec = pl.BlockSpec((None, tn, fb), lambda i, j, k: (k, j, 0))
    return _matmul(
        "mm_dx_ffn", "nt", (dgt, wg, dup, wu), [a_spec, b_spec, a_spec, b_spec],
        jax.ShapeDtypeStruct((t, d), F32), pl.BlockSpec((tm, tn), lambda i, j, k: (i, j)),
        (t // tm, d // tn, nblk), (tm, tn))


def ffn_up(u, wg, wu):
    t, d = u.shape
    nblk, _, fb = wg.shape
    tm = _tile(t, 512)

    def body(u_ref, wg_ref, wu_ref, gt_ref, up_ref, f_ref):
        uu = u_ref[...]
        gt = _dot(uu, wg_ref[...])
        up = _dot(uu, wu_ref[...])
        gt_ref[...] = gt
        up_ref[...] = up
        f_ref[...] = (gt * _sigmoid(gt) * up).astype(BF16)

    w_spec = pl.BlockSpec((None, d, fb), lambda j, i: (j, 0, 0))
    o_spec = pl.BlockSpec((None, tm, fb), lambda j, i: (j, i, 0))
    return pl.pallas_call(
        body, name="ffn_up", grid=(nblk, t // tm),
        in_specs=[pl.BlockSpec((tm, d), lambda j, i: (i, 0)), w_spec, w_spec],
        out_specs=[o_spec, o_spec, o_spec],
        out_shape=[jax.ShapeDtypeStruct((nblk, t, fb), F32), jax.ShapeDtypeStruct((nblk, t, fb), F32),
                   jax.ShapeDtypeStruct((nblk, t, fb), BF16)],
        compiler_params=_params(2),
    )(u, wg, wu)


def ffn_bwd(dd, wd, gt, up):
    t, d = dd.shape
    nblk, fb, _ = wd.shape
    tm = _tile(t, 512)

    def body(dd_ref, wd_ref, gt_ref, up_ref, dgt_ref, dup_ref):
        df = _dot(dd_ref[...], wd_ref[...], "nt")
        g = gt_ref[...]
        s = _sigmoid(g)
        dgt_ref[...] = (df * up_ref[...] * s * (1.0 + g * (1.0 - s))).astype(BF16)
        dup_ref[...] = (df * g * s).astype(BF16)

    s_spec = pl.BlockSpec((None, tm, fb), lambda j, i: (j, i, 0))
    return pl.pallas_call(
        body, name="ffn_bwd", grid=(nblk, t // tm),
        in_specs=[pl.BlockSpec((tm, d), lambda j, i: (i, 0)), pl.BlockSpec((None, fb, d), lambda j, i: (j, 0, 0)),
                  s_spec, s_spec],
        out_specs=[s_spec, s_spec],
        out_shape=[jax.ShapeDtypeStruct((nblk, t, fb), BF16)] * 2,
        compiler_params=_params(2),
    )(dd, wd, gt, up)


def rms_pre(h, g):
    t, d = h.shape
    tr = _tile(t, ROW_T)

    def body(h_ref, g_ref, o_ref):
        xn, _ = _rms_stats(h_ref[...])
        o_ref[...] = (xn * g_ref[...]).astype(BF16)

    return pl.pallas_call(
        body, name="rms_pre", grid=(t // tr,),
        in_specs=[_row_spec(tr, d), _vec_spec(d)], out_specs=_row_spec(tr, d),
        out_shape=jax.ShapeDtypeStruct((t, d), BF16), compiler_params=_params(1),
    )(h, g)


def res_norm(h, o, g_post, g_pre):
    t, d = h.shape
    tr = _tile(t, ROW_T)

    def body(h_ref, o_ref, gpo_ref, gpr_ref, h2_ref, u_ref):
        on, _ = _rms_stats(o_ref[...])
        h2 = h_ref[...] + on * gpo_ref[...]
        h2_ref[...] = h2
        hn, _ = _rms_stats(h2)
        u_ref[...] = (hn * gpr_ref[...]).astype(BF16)

    return pl.pallas_call(
        body, name="res_norm", grid=(t // tr,),
        in_specs=[_row_spec(tr, d), _row_spec(tr, d), _vec_spec(d), _vec_spec(d)],
        out_specs=[_row_spec(tr, d), _row_spec(tr, d)],
        out_shape=[jax.ShapeDtypeStruct((t, d), F32), jax.ShapeDtypeStruct((t, d), BF16)],
        compiler_params=_params(1),
    )(h, o, g_post, g_pre)


def final_loss(h2, dbr, g_post, target):
    t, d = h2.shape
    tr = _tile(t, ROW_T)

    def body(h2_ref, d_ref, g_ref, tg_ref, loss_ref, dy_ref, dd_ref, dg_ref):
        i = pl.program_id(0)

        @pl.when(i == 0)
        def _():
            loss_ref[...] = jnp.zeros_like(loss_ref)
            dg_ref[...] = jnp.zeros_like(dg_ref)

        x = d_ref[...]
        g = g_ref[...]
        xn, _ = _rms_stats(x)
        diff = h2_ref[...] + xn * g - tg_ref[...]
        loss_ref[...] += jnp.sum(jnp.sum(diff * diff, axis=1, keepdims=True), axis=0, keepdims=True)
        dy = diff * (1.0 / d)
        dy_ref[...] = dy
        dx, dg = _rms_bwd(dy, x, g)
        dd_ref[...] = dx.astype(BF16)
        dg_ref[...] += dg

    return pl.pallas_call(
        body, name="final_loss", grid=(t // tr,),
        in_specs=[_row_spec(tr, d), _row_spec(tr, d), _vec_spec(d), _row_spec(tr, d)],
        out_specs=[pl.BlockSpec((1, 1), lambda i: (0, 0)), _row_spec(tr, d), _row_spec(tr, d), _vec_spec(d)],
        out_shape=[jax.ShapeDtypeStruct((1, 1), F32), jax.ShapeDtypeStruct((t, d), F32),
                   jax.ShapeDtypeStruct((t, d), BF16), jax.ShapeDtypeStruct((1, d), F32)],
        compiler_params=_params(1),
    )(h2, dbr, g_post, target)


def norm_bwd(dh_out, du, h_in, g_pre, prev=None):
    t, d = h_in.shape
    tr = _tile(t, ROW_T)
    with_prev = prev is not None

    def body(*refs):
        if with_prev:
            dho_ref, du_ref, h_ref, gpr_ref, br_ref, gpo_ref, dh_ref, dgpr_ref, dbr_ref, dgpo_ref = refs
        else:
            dho_ref, du_ref, h_ref, gpr_ref, dh_ref, dgpr_ref = refs
        i = pl.program_id(0)

        @pl.when(i == 0)
        def _():
            dgpr_ref[...] = jnp.zeros_like(dgpr_ref)
            if with_prev:
                dgpo_ref[...] = jnp.zeros_like(dgpo_ref)

        dx, dg = _rms_bwd(du_ref[...], h_ref[...], gpr_ref[...])
        dh = dho_ref[...] + dx
        dh_ref[...] = dh
        dgpr_ref[...] += dg
        if with_prev:
            dbr, dg2 = _rms_bwd(dh, br_ref[...], gpo_ref[...])
            dbr_ref[...] = dbr.astype(BF16)
            dgpo_ref[...] += dg2

    row, vec = _row_spec(tr, d), _vec_spec(d)
    in_specs = [row, row, row, vec] + ([row, vec] if with_prev else [])
    out_specs = [row, vec] + ([row, vec] if with_prev else [])
    out_shape = [jax.ShapeDtypeStruct((t, d), F32), jax.ShapeDtypeStruct((1, d), F32)]
    if with_prev:
        out_shape += [jax.ShapeDtypeStruct((t, d), BF16), jax.ShapeDtypeStruct((1, d), F32)]
    args = (dh_out, du, h_in, g_pre) + (tuple(prev) if with_prev else ())
    return pl.pallas_call(
        body, name="norm_bwd_chain" if with_prev else "norm_bwd_first", grid=(t // tr,),
        in_specs=in_specs, out_specs=out_specs, out_shape=out_shape, compiler_params=_params(1),
    )(*args)


def mix_fwd(ya, yc, yl, ga, gc, gl):
    t, wa = ya.shape
    wc, wl = yc.shape[1], yl.shape[1]
    tr = _tile(t, ROW_T)

    def body(ya_ref, yc_ref, yl_ref, ga_ref, gc_ref, gl_ref, o_ref):
        o_ref[:, pl.ds(0, wa)] = (_rms_stats(ya_ref[...])[0] * ga_ref[...]).astype(BF16)
        o_ref[:, pl.ds(wa, wc)] = (_rms_stats(yc_ref[...])[0] * gc_ref[...]).astype(BF16)
        o_ref[:, pl.ds(wa + wc, wl)] = (_rms_stats(yl_ref[...])[0] * gl_ref[...]).astype(BF16)

    return pl.pallas_call(
        body, name="mix_fwd", grid=(t // tr,),
        in_specs=[_row_spec(tr, wa), _row_spec(tr, wc), _row_spec(tr, wl), _vec_spec(wa), _vec_spec(wc), _vec_spec(wl)],
        out_specs=_row_spec(tr, wa + wc + wl),
        out_shape=jax.ShapeDtypeStruct((t, wa + wc + wl), BF16), compiler_params=_params(1),
    )(ya, yc, yl, ga, gc, gl)


def mix_bwd(dmixed, ya, yc, yl, cpre, ga, gc, gl, lng, lnb):
    t, wa = ya.shape
    wc, wl = yc.shape[1], yl.shape[1]
    tr = _tile(t, ROW_T)

    def body(dm_ref, ya_ref, yc_ref, yl_ref, c_ref, ga_ref, gc_ref, gl_ref, lg_ref, lb_ref,
             dya_ref, dc_ref, dyl_ref, dga_ref, dgc_ref, dgl_ref, dlg_ref, dlb_ref):
        i = pl.program_id(0)

        @pl.when(i == 0)
        def _():
            for r in (dga_ref, dgc_ref, dgl_ref, dlg_ref, dlb_ref):
                r[...] = jnp.zeros_like(r)

        dya, dga = _rms_bwd(dm_ref[:, pl.ds(0, wa)], ya_ref[...], ga_ref[...])
        dya_ref[...] = dya
        dga_ref[...] += dga
        dyl, dgl = _rms_bwd(dm_ref[:, pl.ds(wa + wc, wl)], yl_ref[...], gl_ref[...])
        dyl_ref[...] = dyl
        dgl_ref[...] += dgl
        dyc, dgc = _rms_bwd(dm_ref[:, pl.ds(wa, wc)], yc_ref[...], gc_ref[...])
        dgc_ref[...] += dgc
        c = c_ref[...]
        xc = c - jnp.mean(c, axis=-1, keepdims=True)
        rstd = lax.rsqrt(jnp.mean(xc * xc, axis=-1, keepdims=True) + EPS)
        xhat = xc * rstd
        ln = xhat * lg_ref[...] + lb_ref[...]
        s = _sigmoid(ln)
        dln = dyc * s * (1.0 + ln * (1.0 - s))
        dlg_ref[...] += jnp.sum(dln * xhat, axis=0, keepdims=True)
        dlb_ref[...] += jnp.sum(dln, axis=0, keepdims=True)
        dxh = dln * lg_ref[...]
        dc_ref[...] = rstd * (dxh - jnp.mean(dxh, axis=-1, keepdims=True)
                              - xhat * jnp.mean(dxh * xhat, axis=-1, keepdims=True))

    return pl.pallas_call(
        body, name="mix_bwd", grid=(t // tr,),
        in_specs=[_row_spec(tr, wa + wc + wl), _row_spec(tr, wa), _row_spec(tr, wc), _row_spec(tr, wl), _row_spec(tr, wc),
                  _vec_spec(wa), _vec_spec(wc), _vec_spec(wl), _vec_spec(wc), _vec_spec(wc)],
        out_specs=[_row_spec(tr, wa), _row_spec(tr, wc), _row_spec(tr, wl),
                   _vec_spec(wa), _vec_spec(wc), _vec_spec(wl), _vec_spec(wc), _vec_spec(wc)],
        out_shape=[jax.ShapeDtypeStruct((t, wa), F32), jax.ShapeDtypeStruct((t, wc), F32), jax.ShapeDtypeStruct((t, wl), F32),
                   jax.ShapeDtypeStruct((1, wa), F32), jax.ShapeDtypeStruct((1, wc), F32), jax.ShapeDtypeStruct((1, wl), F32),
                   jax.ShapeDtypeStruct((1, wc), F32), jax.ShapeDtypeStruct((1, wc), F32)],
        compiler_params=_params(1),
    )(dmixed, ya, yc, yl, cpre, ga, gc, gl, lng, lnb)


def _hi_lo(x):
    hi = x.astype(BF16)
    return hi, (x - hi.astype(F32)).astype(BF16)


def _att_block(qb, kt, jj, scale, lower, tri_gt):
    z = _dot(qb, kt, "nt") * scale
    sp = _softplus(z)
    mask = jnp.logical_or(jj > 0, lower)
    lk = jnp.where(mask, -sp, 0.0)
    hi, lo = _hi_lo(lk)
    logw = (z - sp) + _dot(hi, tri_gt) + _dot(lo, tri_gt)
    return z, sp, mask, lk, logw


def attn_fwd(proj, n_heads):
    t = proj.shape[0]
    tb = _tile(t, ATT_T)
    nq = t // tb
    scale = HEAD_DIM ** -0.5

    def body(q_ref, k_ref, v_ref, o_ref, kb_ref, vb_ref):
        kb_ref[...] = k_ref[...].astype(BF16)
        vb_ref[...] = v_ref[...].astype(BF16)
        row = lax.broadcasted_iota(jnp.int32, (tb, tb), 0)
        col = lax.broadcasted_iota(jnp.int32, (tb, tb), 1)
        lower = col < row
        tri_gt = (row > col).astype(BF16)

        def qblock(i, _):
            q0 = pl.multiple_of(i * tb, tb)
            qb = q_ref[pl.ds(q0, tb), :].astype(BF16)

            def kblock(jj, carry):
                acc, run = carry
                k0 = pl.multiple_of((i - jj) * tb, tb)
                _, _, mask, lk, logw = _att_block(qb, kb_ref[pl.ds(k0, tb), :], jj, scale, lower, tri_gt)
                w = jnp.where(mask, jnp.exp(logw + run), 0.0)
                acc = acc + _dot(w.astype(BF16), vb_ref[pl.ds(k0, tb), :])
                return acc, run + jnp.sum(lk, axis=1, keepdims=True)

            acc, _ = lax.fori_loop(0, i + 1, kblock, (jnp.zeros((tb, HEAD_DIM), F32), jnp.zeros((tb, 1), F32)))
            o_ref[pl.ds(q0, tb), :] = acc
            return 0

        lax.fori_loop(0, nq, qblock, 0)

    def col_spec(base):
        return pl.BlockSpec((t, HEAD_DIM), lambda h, base=base: (0, base + h))

    return pl.pallas_call(
        body, name="attn_fwd", grid=(n_heads,),
        in_specs=[col_spec(0), col_spec(n_heads), col_spec(2 * n_heads)],
        out_specs=col_spec(0),
        out_shape=jax.ShapeDtypeStruct((t, n_heads * HEAD_DIM), F32),
        scratch_shapes=[pltpu.VMEM((t, HEAD_DIM), BF16), pltpu.VMEM((t, HEAD_DIM), BF16)],
        compiler_params=_params(1),
    )(proj, proj, proj)


def attn_bwd(proj, dy, n_heads):
    t = proj.shape[0]
    tb = _tile(t, ATT_T)
    nq = t // tb
    scale = HEAD_DIM ** -0.5

    def body(q_ref, k_ref, v_ref, dy_ref, dq_ref, dk_ref, dv_ref, qb_ref, kb_ref, vb_ref, dob_ref, dk_acc, dv_acc, run_s):
        qb_ref[...] = q_ref[...].astype(BF16)
        kb_ref[...] = k_ref[...].astype(BF16)
        vb_ref[...] = v_ref[...].astype(BF16)
        dob_ref[...] = dy_ref[...].astype(BF16)
        dk_acc[...] = jnp.zeros_like(dk_acc)
        dv_acc[...] = jnp.zeros_like(dv_acc)
        row = lax.broadcasted_iota(jnp.int32, (tb, tb), 0)
        col = lax.broadcasted_iota(jnp.int32, (tb, tb), 1)
        lower = col < row
        tri_gt = (row > col).astype(BF16)
        tri_lt = (row < col).astype(BF16)

        def qblock(i, _):
            q0 = pl.multiple_of(i * tb, tb)
            qb = qb_ref[pl.ds(q0, tb), :]
            dob = dob_ref[pl.ds(q0, tb), :]

            def sweep(jj, run):
                k0 = pl.multiple_of((i - jj) * tb, tb)
                sp = _softplus(_dot(qb, kb_ref[pl.ds(k0, tb), :], "nt") * scale)
                run_s[i - jj] = run
                return run + jnp.sum(jnp.where(jnp.logical_or(jj > 0, lower), -sp, 0.0), axis=1, keepdims=True)

            lax.fori_loop(0, i + 1, sweep, jnp.zeros((tb, 1), F32))

            def kblock(kbi, carry):
                dq, gsum = carry
                k0 = pl.multiple_of(kbi * tb, tb)
                kt = kb_ref[pl.ds(k0, tb), :]
                vt = vb_ref[pl.ds(k0, tb), :]
                z, sp, mask, _, logw = _att_block(qb, kt, i - kbi, scale, lower, tri_gt)
                w = jnp.where(mask, jnp.exp(logw + run_s[kbi]), 0.0)
                g = w * _dot(dob, vt, "nt")
                hi, lo = _hi_lo(g)
                before = _dot(hi, tri_lt) + _dot(lo, tri_lt) + gsum
                sig = jnp.exp(z - sp)
                dz = jnp.where(mask, g * (1.0 - sig) - before * sig, 0.0) * scale
                dzb = dz.astype(BF16)
                dk_acc[pl.ds(k0, tb), :] += _dot(dzb, qb, "tn")
                dv_acc[pl.ds(k0, tb), :] += _dot(w.astype(BF16), dob, "tn")
                return dq + _dot(dzb, kt), gsum + jnp.sum(g, axis=1, keepdims=True)

            dq, _ = lax.fori_loop(0, i + 1, kblock, (jnp.zeros((tb, HEAD_DIM), F32), jnp.zeros((tb, 1), F32)))
            dq_ref[pl.ds(q0, tb), :] = dq.astype(BF16)
            return 0

        lax.fori_loop(0, nq, qblock, 0)
        dk_ref[...] = dk_acc[...].astype(BF16)
        dv_ref[...] = dv_acc[...].astype(BF16)

    def col_spec(base):
        return pl.BlockSpec((t, HEAD_DIM), lambda h, base=base: (0, base + h))

    width = n_heads * HEAD_DIM
    return pl.pallas_call(
        body, name="attn_bwd", grid=(n_heads,),
        in_specs=[col_spec(0), col_spec(n_heads), col_spec(2 * n_heads), col_spec(0)],
        out_specs=[col_spec(0), col_spec(0), col_spec(0)],
        out_shape=[jax.ShapeDtypeStruct((t, width), BF16)] * 3,
        scratch_shapes=[pltpu.VMEM((t, HEAD_DIM), BF16)] * 4 + [pltpu.VMEM((t, HEAD_DIM), F32)] * 2
        + [pltpu.VMEM((nq, tb, 1), F32)],
        compiler_params=_params(1),
    )(proj, proj, proj, dy)


def _glu_halo(vc, gc, vp, gp, ubuf, i, tt, halo):
    uprev = vp[pl.ds(tt - halo, halo), :] * _sigmoid(gp[pl.ds(tt - halo, halo), :])
    ubuf[pl.ds(0, halo), :] = jnp.where(i > 0, uprev, 0.0)
    ubuf[pl.ds(halo, tt), :] = vc[...] * _sigmoid(gc[...])


def conv_fwd(proj, col0, cc, w, b, lng, lnb):
    t = proj.shape[0]
    tt = _tile(t, ROW_T)
    vi, gi = col0 // cc, col0 // cc + 1
    off = CONV_HALO - (DW_LEN - 1)

    def body(vc, gc, vp, gp, w_ref, b_ref, lg_ref, lb_ref, c_ref, y_ref, ubuf):
        i = pl.program_id(0)
        _glu_halo(vc, gc, vp, gp, ubuf, i, tt, CONV_HALO)
        for ch in range(cc // LANE):
            sl = pl.ds(ch * LANE, LANE)
            acc = jnp.zeros((tt, LANE), F32) + b_ref[:, sl]
            for tap in range(DW_LEN):
                acc = acc + w_ref[pl.ds(tap, 1), sl] * ubuf[pl.ds(off + tap, tt), sl]
            c_ref[:, sl] = acc
        c = c_ref[...]
        xc = c - jnp.mean(c, axis=-1, keepdims=True)
        ln = xc * lax.rsqrt(jnp.mean(xc * xc, axis=-1, keepdims=True) + EPS) * lg_ref[...] + lb_ref[...]
        y_ref[...] = ln * _sigmoid(ln)

    cur = lambda c: pl.BlockSpec((tt, cc), lambda i, c=c: (i, c))
    prev = lambda c: pl.BlockSpec((tt, cc), lambda i, c=c: (jnp.maximum(i - 1, 0), c))
    return pl.pallas_call(
        body, name="conv_fwd", grid=(t // tt,),
        in_specs=[cur(vi), cur(gi), prev(vi), prev(gi), pl.BlockSpec((DW_LEN, cc), lambda i: (0, 0)),
                  _vec_spec(cc), _vec_spec(cc), _vec_spec(cc)],
        out_specs=[_row_spec(tt, cc), _row_spec(tt, cc)],
        out_shape=[jax.ShapeDtypeStruct((t, cc), F32)] * 2,
        scratch_shapes=[pltpu.VMEM((CONV_HALO + tt, cc), F32)],
        compiler_params=_params(1),
    )(proj, proj, proj, proj, w, b, lng, lnb)


def conv_bwd(proj, col0, cc, dc, w):
    t = proj.shape[0]
    tt = _tile(t, ROW_T)
    nt = t // tt
    vi, gi = col0 // cc, col0 // cc + 1
    off = CONV_HALO - (DW_LEN - 1)

    def body(vc, gc, vp, gp, dcc, dcn, w_ref, dvg_ref, dw_ref, db_ref, ubuf, dbuf):
        i = pl.program_id(0)

        @pl.when(i == 0)
        def _():
            dw_ref[...] = jnp.zeros_like(dw_ref)
            db_ref[...] = jnp.zeros_like(db_ref)

        _glu_halo(vc, gc, vp, gp, ubuf, i, tt, CONV_HALO)
        dbuf[pl.ds(0, tt), :] = dcc[...]
        dbuf[pl.ds(tt, CONV_HALO), :] = jnp.where(i < nt - 1, dcn[pl.ds(0, CONV_HALO), :], 0.0)
        db_ref[...] += jnp.sum(dcc[...], axis=0, keepdims=True)
        for ch in range(cc // LANE):
            sl = pl.ds(ch * LANE, LANE)
            dcv = dbuf[pl.ds(0, tt), sl]
            du = jnp.zeros((tt, LANE), F32)
            for tap in range(DW_LEN):
                du = du + w_ref[pl.ds(tap, 1), sl] * dbuf[pl.ds(DW_LEN - 1 - tap, tt), sl]
                dw_ref[pl.ds(tap, 1), sl] += jnp.sum(dcv * ubuf[pl.ds(off + tap, tt), sl], axis=0, keepdims=True)
            s = _sigmoid(gc[:, sl])
            val = vc[:, sl]
            dvg_ref[:, sl] = (du * s).astype(BF16)
            dvg_ref[:, pl.ds(cc + ch * LANE, LANE)] = (du * val * s * (1.0 - s)).astype(BF16)

    cur = lambda c: pl.BlockSpec((tt, cc), lambda i, c=c: (i, c))
    prev = lambda c: pl.BlockSpec((tt, cc), lambda i, c=c: (jnp.maximum(i - 1, 0), c))
    return pl.pallas_call(
        body, name="conv_bwd", grid=(nt,),
        in_specs=[cur(vi), cur(gi), prev(vi), prev(gi), _row_spec(tt, cc),
                  pl.BlockSpec((tt, cc), lambda i: (jnp.minimum(i + 1, nt - 1), 0)),
                  pl.BlockSpec((DW_LEN, cc), lambda i: (0, 0))],
        out_specs=[_row_spec(tt, 2 * cc), pl.BlockSpec((DW_LEN, cc), lambda i: (0, 0)), _vec_spec(cc)],
        out_shape=[jax.ShapeDtypeStruct((t, 2 * cc), BF16), jax.ShapeDtypeStruct((DW_LEN, cc), F32),
                   jax.ShapeDtypeStruct((1, cc), F32)],
        scratch_shapes=[pltpu.VMEM((CONV_HALO + tt, cc), F32), pltpu.VMEM((tt + CONV_HALO, cc), F32)],
        compiler_params=_params(1),
    )(proj, proj, proj, proj, dc, dc, w)


def _lru_gates(xbuf, cw_ref, cb_ref, wa_ref, ba_ref, wi_ref, bi_ref, lam_ref, tt, wl):
    bd = wl // LRU_BLOCKS
    xr = jnp.zeros((tt, wl), F32) + cb_ref[...]
    for tap in range(LRU_LEN):
        xr = xr + cw_ref[pl.ds(tap, 1), :] * xbuf[pl.ds(LRU_HALO - (LRU_LEN - 1) + tap, tt), :]
    xb = xr.astype(BF16)
    ga = jnp.concatenate([_dot(xb[:, n * bd:(n + 1) * bd], wa_ref[n]) for n in range(LRU_BLOCKS)], axis=1) + ba_ref[...]
    gi = jnp.concatenate([_dot(xb[:, n * bd:(n + 1) * bd], wi_ref[n]) for n in range(LRU_BLOCKS)], axis=1) + bi_ref[...]
    r = _sigmoid(ga)
    ig = _sigmoid(gi)
    spl = _softplus(-lam_ref[...])
    log_a = -LRU_C * r * spl
    a = jnp.exp(log_a)
    m = jnp.sqrt(_neg_expm1(2.0 * log_a))
    return xr, xb, r, ig, spl, a, m


def _group_scan(a8, b8, reverse):
    rid = lax.broadcasted_iota(jnp.int32, a8.shape, 0)
    aa, bb = a8, b8
    for dist in (1, 2, 4):
        shift = SUBLANE - dist if reverse else dist
        a_sh = pltpu.roll(aa, shift, 0)
        b_sh = pltpu.roll(bb, shift, 0)
        valid = (rid < SUBLANE - dist) if reverse else (rid >= dist)
        bb = jnp.where(valid, aa * b_sh + bb, bb)
        aa = jnp.where(valid, aa * a_sh, aa)
    return aa, bb


def _pick_row(x8, r):
    rid = lax.broadcasted_iota(jnp.int32, x8.shape, 0)
    return jnp.sum(jnp.where(rid == r, x8, 0.0), axis=0, keepdims=True)


def lru_fwd(proj, col0, wl, cw, cb, wa, ba, wi, bi, lam):
    t = proj.shape[0]
    tt = _tile(t, ROW_T)
    xi, yi = col0 // wl, col0 // wl + 1

    def body(xc, xp, ry, cw_ref, cb_ref, wa_ref, ba_ref, wi_ref, bi_ref, lam_ref, hs_ref, y_ref,
             xbuf, a_s, b_s, hcar):
        i = pl.program_id(0)

        @pl.when(i == 0)
        def _():
            hcar[...] = jnp.zeros_like(hcar)

        xbuf[pl.ds(0, LRU_HALO), :] = jnp.where(i > 0, xp[pl.ds(tt - LRU_HALO, LRU_HALO), :], 0.0)
        xbuf[pl.ds(LRU_HALO, tt), :] = xc[...]
        xr, _, _, ig, _, a, m = _lru_gates(xbuf, cw_ref, cb_ref, wa_ref, ba_ref, wi_ref, bi_ref, lam_ref, tt, wl)
        a_s[...] = a
        b_s[...] = m * ig * xr

        def group(gidx, h):
            r0 = pl.multiple_of(gidx * SUBLANE, SUBLANE)
            aa, bb = _group_scan(a_s[pl.ds(r0, SUBLANE), :], b_s[pl.ds(r0, SUBLANE), :], False)
            h8 = aa * h + bb
            hs_ref[pl.ds(r0, SUBLANE), :] = h8
            return _pick_row(h8, SUBLANE - 1)

        hcar[...] = lax.fori_loop(0, tt // SUBLANE, group, hcar[...])
        gel, _ = _gelu_and_grad(ry[...])
        y_ref[...] = hs_ref[...] * gel

    cur = lambda c: pl.BlockSpec((tt, wl), lambda i, c=c: (i, c))
    full = lambda shape: pl.BlockSpec(shape, lambda i: (0,) * len(shape))
    return pl.pallas_call(
        body, name="lru_fwd", grid=(t // tt,),
        in_specs=[cur(xi), pl.BlockSpec((tt, wl), lambda i: (jnp.maximum(i - 1, 0), xi)), cur(yi),
                  full((LRU_LEN, wl)), _vec_spec(wl), full(wa.shape), _vec_spec(wl), full(wi.shape), _vec_spec(wl),
                  _vec_spec(wl)],
        out_specs=[_row_spec(tt, wl), _row_spec(tt, wl)],
        out_shape=[jax.ShapeDtypeStruct((t, wl), F32)] * 2,
        scratch_shapes=[pltpu.VMEM((LRU_HALO + tt, wl), F32), pltpu.VMEM((tt, wl), F32), pltpu.VMEM((tt, wl), F32),
                        pltpu.VMEM((1, wl), F32)],
        compiler_params=_params(1),
    )(proj, proj, proj, cw, cb, wa, ba, wi, bi, lam)


def lru_bwd(proj, col0, wl, hs, dy, cw, cb, wa, ba, wi, bi, lam):
    t = proj.shape[0]
    tt = _tile(t, ROW_T)
    nt = t // tt
    xi, yi = col0 // wl, col0 // wl + 1
    bd = wl // LRU_BLOCKS

    def body(xc, xp, ry, hc, hp, dy_ref, cw_ref, cb_ref, wa_ref, ba_ref, wi_ref, bi_ref, lam_ref,
             dxy_ref, dcw_ref, dcb_ref, dwa_ref, dba_ref, dwi_ref, dbi_ref, dlam_ref,
             xbuf, hbuf, abuf, e_s, dh_s, dxbuf, dhcar):
        i = pl.program_id(0)
        first = i == 0

        @pl.when(first)
        def _():
            for r in (dcw_ref, dcb_ref, dwa_ref, dba_ref, dwi_ref, dbi_ref, dlam_ref, dhcar):
                r[...] = jnp.zeros_like(r)
            abuf[pl.ds(tt, LRU_HALO), :] = jnp.zeros((LRU_HALO, wl), F32)
            dxbuf[pl.ds(tt, LRU_HALO), :] = jnp.zeros((LRU_HALO, wl), F32)

        has_prev = i < nt - 1
        xbuf[pl.ds(0, LRU_HALO), :] = jnp.where(has_prev, xp[pl.ds(tt - LRU_HALO, LRU_HALO), :], 0.0)
        xbuf[pl.ds(LRU_HALO, tt), :] = xc[...]
        hbuf[pl.ds(0, LRU_HALO), :] = jnp.where(has_prev, hp[pl.ds(tt - LRU_HALO, LRU_HALO), :], 0.0)
        hbuf[pl.ds(LRU_HALO, tt), :] = hc[...]
        xr, xb, r, ig, spl, a, m = _lru_gates(xbuf, cw_ref, cb_ref, wa_ref, ba_ref, wi_ref, bi_ref, lam_ref, tt, wl)
        gel, dgel = _gelu_and_grad(ry[...])
        dyv = dy_ref[...]
        e_s[...] = dyv * gel
        dxy_ref[:, pl.ds(wl, wl)] = (dyv * hc[...] * dgel).astype(BF16)
        abuf[pl.ds(0, tt), :] = a
        a_next = abuf[pl.ds(1, tt), :]
        dh_s[...] = a_next

        def group(it, dh_in):
            r0 = pl.multiple_of((tt // SUBLANE - 1 - it) * SUBLANE, SUBLANE)
            aa, bb = _group_scan(dh_s[pl.ds(r0, SUBLANE), :], e_s[pl.ds(r0, SUBLANE), :], True)
            dh8 = aa * dh_in + bb
            dh_s[pl.ds(r0, SUBLANE), :] = dh8
            return _pick_row(dh8, 0)

        dhcar[...] = lax.fori_loop(0, tt // SUBLANE, group, dhcar[...])
        abuf[pl.ds(tt, LRU_HALO), :] = a[0:LRU_HALO, :]
        dh = dh_s[...]
        h_m1 = hbuf[pl.ds(LRU_HALO - 1, tt), :]
        dlog_a = dh * h_m1 * a - dh * ig * xr * (a * a / m)
        dig = dh * m * xr
        dxr = dh * m * ig
        dga = dlog_a * (-LRU_C) * spl * r * (1.0 - r)
        dgi = dig * ig * (1.0 - ig)
        dlam_ref[...] += jnp.sum(dlog_a * r, axis=0, keepdims=True) * (LRU_C * _sigmoid(-lam_ref[...]))
        dba_ref[...] += jnp.sum(dga, axis=0, keepdims=True)
        dbi_ref[...] += jnp.sum(dgi, axis=0, keepdims=True)
        dgab = dga.astype(BF16)
        dgib = dgi.astype(BF16)
        back = []
        for n in range(LRU_BLOCKS):
            sl = slice(n * bd, (n + 1) * bd)
            dwa_ref[n] += _dot(xb[:, sl], dgab[:, sl], "tn")
            dwi_ref[n] += _dot(xb[:, sl], dgib[:, sl], "tn")
            back.append(_dot(dgab[:, sl], wa_ref[n], "nt") + _dot(dgib[:, sl], wi_ref[n], "nt"))
        dxr = dxr + jnp.concatenate(back, axis=1)
        dcb_ref[...] += jnp.sum(dxr, axis=0, keepdims=True)
        dxbuf[pl.ds(0, tt), :] = dxr
        drx = jnp.zeros((tt, wl), F32)
        for tap in range(LRU_LEN):
            drx = drx + cw_ref[pl.ds(tap, 1), :] * dxbuf[pl.ds(LRU_LEN - 1 - tap, tt), :]
            dcw_ref[pl.ds(tap, 1), :] += jnp.sum(
                dxr * xbuf[pl.ds(LRU_HALO - (LRU_LEN - 1) + tap, tt), :], axis=0, keepdims=True)
        dxbuf[pl.ds(tt, LRU_HALO), :] = dxr[0:LRU_HALO, :]
        dxy_ref[:, pl.ds(0, wl)] = drx.astype(BF16)

    rev = lambda c: pl.BlockSpec((tt, wl), lambda i, c=c: (nt - 1 - i, c))
    rev_prev = lambda c: pl.BlockSpec((tt, wl), lambda i, c=c: (jnp.maximum(nt - 2 - i, 0), c))
    full = lambda shape: pl.BlockSpec(shape, lambda i: (0,) * len(shape))
    vec = _vec_spec(wl)
    return pl.pallas_call(
        body, name="lru_bwd", grid=(nt,),
        in_specs=[rev(xi), rev_prev(xi), rev(yi), rev(0), rev_prev(0), rev(0),
                  full((LRU_LEN, wl)), vec, full(wa.shape), vec, full(wi.shape), vec, vec],
        out_specs=[pl.BlockSpec((tt, 2 * wl), lambda i: (nt - 1 - i, 0)), full((LRU_LEN, wl)), vec,
                   full(wa.shape), vec, full(wi.shape), vec, vec],
        out_shape=[jax.ShapeDtypeStruct((t, 2 * wl), BF16), jax.ShapeDtypeStruct((LRU_LEN, wl), F32),
                   jax.ShapeDtypeStruct((1, wl), F32), jax.ShapeDtypeStruct(wa.shape, F32),
                   jax.ShapeDtypeStruct((1, wl), F32), jax.ShapeDtypeStruct(wi.shape, F32),
                   jax.ShapeDtypeStruct((1, wl), F32), jax.ShapeDtypeStruct((1, wl), F32)],
        scratch_shapes=[pltpu.VMEM((LRU_HALO + tt, wl), F32), pltpu.VMEM((LRU_HALO + tt, wl), F32),
                        pltpu.VMEM((tt + LRU_HALO, wl), F32), pltpu.VMEM((tt, wl), F32), pltpu.VMEM((tt, wl), F32),
                        pltpu.VMEM((tt + LRU_HALO, wl), F32), pltpu.VMEM((1, wl), F32)],
        compiler_params=_params(1),
    )(proj, proj, proj, hs, hs, dy, cw, cb, wa, ba, wi, bi, lam)


def _adamw(w, g, m, v):
    m = ADAM_B1 * m + (1.0 - ADAM_B1) * g
    v = ADAM_B2 * v + (1.0 - ADAM_B2) * (g * g)
    m_hat = m / (1.0 - ADAM_B1 ** ADAM_STEP)
    v_hat = v / (1.0 - ADAM_B2 ** ADAM_STEP)
    delta = -ADAM_LR * (m_hat / (jnp.sqrt(v_hat) + ADAM_EPS) + ADAM_WD * w)
    return delta, m, v


def adam_big(name, w, m, v, parts, chip):
    n_layers, rows, cols = w.shape
    tr = _tile(rows, 128 if cols > 1024 else 256)
    nrt = rows // tr

    def body(chip_ref, *refs):
        w_ref, m_ref, v_ref = refs[:3]
        part_refs = refs[3:3 + 4 * n_layers]
        g_ref, d_ref, mo_ref, vo_ref = refs[3 + 4 * n_layers:]
        layer = pl.program_id(0)
        for l in range(n_layers):
            @pl.when(layer == l)
            def _(l=l):
                g = part_refs[4 * l][...].astype(F32)
                for p in range(1, 4):
                    g = g + part_refs[4 * l + p][...].astype(F32)
                delta, mn, vn = _adamw(w_ref[...], g, m_ref[...], v_ref[...])
                g_ref[...] = g
                d_ref[...] = delta
                mo_ref[...] = mn
                vo_ref[...] = vn

    wspec = pl.BlockSpec((None, tr, cols), lambda l, i, chip_ref: (l, i, 0))
    operands, in_specs = [w, m, v], [wspec, wspec, wspec]
    for l in range(n_layers):
        mine, recv = parts[l]
        operands.append(mine)
        in_specs.append(pl.BlockSpec(
            (None, tr, cols), lambda ll, i, chip_ref, l=l: (chip_ref[0], jnp.where(ll == l, i, 0), 0)))
        for p in range(3):
            operands.append(recv)
            in_specs.append(pl.BlockSpec(
                (None, tr, cols), lambda ll, i, chip_ref, l=l, p=p: (p, jnp.where(ll == l, i, 0), 0)))
    return pl.pallas_call(
        body, name=name,
        grid_spec=pltpu.PrefetchScalarGridSpec(
            num_scalar_prefetch=1, grid=(n_layers, nrt), in_specs=in_specs, out_specs=[wspec] * 4),
        out_shape=[jax.ShapeDtypeStruct(w.shape, F32)] * 4, compiler_params=_params(2),
    )(chip, *operands)


def adam_small(w, m, v, g):
    rows = w.shape[0]
    tr = _tile(rows, PACK_ROWS)

    def body(w_ref, m_ref, v_ref, g_ref, d_ref, mo_ref, vo_ref):
        delta, mn, vn = _adamw(w_ref[...], g_ref[...], m_ref[...], v_ref[...])
        d_ref[...] = delta
        mo_ref[...] = mn
        vo_ref[...] = vn

    spec = _row_spec(tr, LANE)
    return pl.pallas_call(
        body, name="adam_small", grid=(rows // tr,), in_specs=[spec] * 4, out_specs=[spec] * 3,
        out_shape=[jax.ShapeDtypeStruct(w.shape, F32)] * 3, compiler_params=_params(1),
    )(w, m, v, g)


def sum_parts(parts):
    _, rows, _ = parts.shape
    tr = _tile(rows, PACK_ROWS)

    def body(p_ref, o_ref):
        acc = p_ref[0]
        for k in range(1, N_DEV):
            acc = acc + p_ref[k]
        o_ref[...] = acc

    return pl.pallas_call(
        body, name="sum_parts", grid=(rows // tr,),
        in_specs=[pl.BlockSpec((N_DEV, tr, LANE), lambda i: (0, i, 0))], out_specs=_row_spec(tr, LANE),
        out_shape=jax.ShapeDtypeStruct((rows, LANE), F32), compiler_params=_params(1),
    )(parts)


def pair_add(g, recv, core):
    n, rows, cols = recv.shape
    tr = _tile(rows, 256)

    def body(core_ref, a_ref, b_ref, o_ref):
        o_ref[...] = (a_ref[...].astype(F32) + b_ref[...].astype(F32)).astype(BF16)

    spec = pl.BlockSpec((None, tr, cols), lambda k, i, core_ref: (k, i, 0))
    return pl.pallas_call(
        body, name="pair_add",
        grid_spec=pltpu.PrefetchScalarGridSpec(
            num_scalar_prefetch=1, grid=(n, rows // tr),
            in_specs=[pl.BlockSpec((None, tr, cols), lambda k, i, core_ref: (2 * k + core_ref[0], i, 0)), spec],
            out_specs=spec),
        out_shape=jax.ShapeDtypeStruct(recv.shape, BF16), compiler_params=_params(2),
    )(core, g, recv)


def place_own(x, me, dtype):
    rows, cols = x.shape
    tr = _tile(rows, 256)

    def body(me_ref, x_ref, o_ref):
        o_ref[...] = x_ref[...].astype(dtype)

    return pl.pallas_call(
        body, name="place_own",
        grid_spec=pltpu.PrefetchScalarGridSpec(
            num_scalar_prefetch=1, grid=(rows // tr,),
            in_specs=[pl.BlockSpec((tr, cols), lambda i, me_ref: (i, 0))],
            out_specs=pl.BlockSpec((None, tr, cols), lambda i, me_ref: (me_ref[0], i, 0))),
        out_shape=jax.ShapeDtypeStruct((N_DEV, rows, cols), dtype), compiler_params=_params(1),
    )(me, x)


_HBM = pl.BlockSpec(memory_space=pltpu.HBM)


def _place():
    return lax.axis_index("x"), lax.axis_index("y"), lax.axis_index("c")


def _other_chips(x, y):
    return [(1 - x, y), (x, 1 - y), (1 - x, 1 - y)]


def all_gather(name, shard, me, dtype=None):
    def body(buf_ref, out_ref, send_sems, recv_sems):
        del buf_ref
        x, y, c = _place()
        mine, sibling = (x, y, c), (x, y, 1 - c)
        chips = _other_chips(x, y)

        def copy(k, block, to):
            slot = out_ref.at[4 * block[0] + 2 * block[1] + block[2]]
            return pltpu.make_async_remote_copy(
                src_ref=slot, dst_ref=slot, send_sem=send_sems.at[k], recv_sem=recv_sems.at[k],
                device_id=to, device_id_type=pl.DeviceIdType.MESH)

        first = [copy(0, mine, sibling)] + [copy(1 + j, mine, (*chip, c)) for j, chip in enumerate(chips)]
        for cp in first:
            cp.start()
        passed = [copy(4 + j, (*chip, c), sibling) for j, chip in enumerate(chips)]
        for j, chip in enumerate(chips):
            copy(1 + j, (*chip, c), mine).wait_recv()
            passed[j].start()
        copy(0, sibling, mine).wait_recv()
        for j, chip in enumerate(chips):
            copy(4 + j, (*chip, 1 - c), mine).wait_recv()
        for cp in first + passed:
            cp.wait_send()

    buf = place_own(shard, me, dtype or shard.dtype)
    return pl.pallas_call(
        body, name=name, out_shape=jax.ShapeDtypeStruct(buf.shape, buf.dtype),
        in_specs=[_HBM], out_specs=_HBM, input_output_aliases={0: 0},
        scratch_shapes=[pltpu.SemaphoreType.DMA((7,)), pltpu.SemaphoreType.DMA((7,))],
    )(buf)


def scatter_pair(name, g):
    _, rows, cols = g.shape

    def body(g_ref, recv_ref, send_sems, recv_sems):
        x, y, c = _place()
        remote = [pltpu.make_async_remote_copy(
            src_ref=g_ref.at[2 * k + 1 - c], dst_ref=recv_ref.at[k], send_sem=send_sems.at[k], recv_sem=recv_sems.at[k],
            device_id=(x, y, 1 - c), device_id_type=pl.DeviceIdType.MESH) for k in range(4)]
        for cp in remote:
            cp.start()
        for cp in remote:
            cp.wait()

    return pl.pallas_call(
        body, name=name, out_shape=jax.ShapeDtypeStruct((4, rows, cols), g.dtype), in_specs=[_HBM], out_specs=_HBM,
        scratch_shapes=[pltpu.SemaphoreType.DMA((4,)), pltpu.SemaphoreType.DMA((4,))],
    )(g)


def scatter_chips(name, p):
    _, rows, cols = p.shape

    def body(p_ref, recv_ref, send_sems, recv_sems):
        x, y, c = _place()
        remote = [pltpu.make_async_remote_copy(
            src_ref=p_ref.at[2 * px + py], dst_ref=recv_ref.at[k], send_sem=send_sems.at[k], recv_sem=recv_sems.at[k],
            device_id=(px, py, c), device_id_type=pl.DeviceIdType.MESH) for k, (px, py) in enumerate(_other_chips(x, y))]
        for cp in remote:
            cp.start()
        for cp in remote:
            cp.wait()

    return pl.pallas_call(
        body, name=name, out_shape=jax.ShapeDtypeStruct((3, rows, cols), p.dtype), in_specs=[_HBM], out_specs=_HBM,
        scratch_shapes=[pltpu.SemaphoreType.DMA((3,)), pltpu.SemaphoreType.DMA((3,))],
    )(p)


def reduce_scatter(tag, g, core):
    p = pair_add(g, scatter_pair("rs_pair_" + tag, g), core)
    return p, scatter_chips("rs_chips_" + tag, p)


_SMALL = ("g_pre_mix", "g_post_mix", "g_pre_ffn", "g_post_ffn", "g_attn_grp", "g_conv_grp", "g_lru_grp",
          "dw_conv_w", "dw_conv_b", "conv_ln_g", "conv_ln_b", "lru_conv_w", "lru_conv_b",
          "lru_w_a", "lru_b_a", "lru_w_i", "lru_b_i", "lru_lambda")
_COL_SHARDED_SMALL = ("dw_conv_w", "lru_conv_w")
_BIG = ("w_in", "w_out", "w_gate", "w_up", "w_down")
_ALL = ("w_in", "w_out", "g_pre_mix", "g_post_mix", "g_pre_ffn", "g_post_ffn", "g_attn_grp", "g_conv_grp", "g_lru_grp",
        "dw_conv_w", "dw_conv_b", "conv_ln_g", "conv_ln_b", "lru_conv_w", "lru_conv_b", "lru_w_a", "lru_b_a",
        "lru_w_i", "lru_b_i", "lru_lambda", "w_gate", "w_up", "w_down")


def _pack(arrays):
    flat = jnp.concatenate([a.reshape(-1) for a in arrays])
    pad = (-flat.shape[0]) % (PACK_ROWS * LANE)
    return jnp.pad(flat, (0, pad)).reshape(-1, LANE)


def _unpack(packed, shapes):
    flat = packed.reshape(-1)
    out, pos = [], 0
    for s in shapes:
        n = math.prod(s)
        out.append(flat[pos:pos + n].reshape(s))
        pos += n
    return out


def kernel(x, w_in, w_out, g_pre_mix, g_post_mix, g_pre_ffn, g_post_ffn, g_attn_grp, g_conv_grp, g_lru_grp, dw_conv_w, dw_conv_b, conv_ln_g, conv_ln_b, lru_conv_w, lru_conv_b, lru_w_a, lru_b_a, lru_w_i, lru_b_i, lru_lambda, w_gate, w_up, w_down, loss_target, m_w_in, m_w_out, m_g_pre_mix, m_g_post_mix, m_g_pre_ffn, m_g_post_ffn, m_g_attn_grp, m_g_conv_grp, m_g_lru_grp, m_dw_conv_w, m_dw_conv_b, m_conv_ln_g, m_conv_ln_b, m_lru_conv_w, m_lru_conv_b, m_lru_w_a, m_lru_b_a, m_lru_w_i, m_lru_b_i, m_lru_lambda, m_w_gate, m_w_up, m_w_down, v_w_in, v_w_out, v_g_pre_mix, v_g_post_mix, v_g_pre_ffn, v_g_post_ffn, v_g_attn_grp, v_g_conv_grp, v_g_lru_grp, v_dw_conv_w, v_dw_conv_b, v_conv_ln_g, v_conv_ln_b, v_lru_conv_w, v_lru_conv_b, v_lru_w_a, v_lru_b_a, v_lru_w_i, v_lru_b_i, v_lru_lambda, v_w_gate, v_w_up, v_w_down):
    env = dict(locals())
    wts = {n: env[n] for n in _ALL}
    mom = {n: env["m_" + n] for n in _ALL}
    var = {n: env["v_" + n] for n in _ALL}

    depth = w_in.shape[0]
    h = x[0]
    target = loss_target[0]
    t, d = h.shape
    attn_w = d // 2
    n_heads = attn_w // HEAD_DIM
    cc = d // 4
    wl = d // 4
    conv_col, lru_col = 3 * attn_w, 3 * attn_w + 2 * cc
    me = 4 * lax.axis_index("x") + 2 * lax.axis_index("y") + lax.axis_index("c")
    me_s = me.astype(jnp.int32).reshape(1)
    chip_s = (2 * lax.axis_index("x") + lax.axis_index("y")).astype(jnp.int32).reshape(1)
    core_s = lax.axis_index("c").astype(jnp.int32).reshape(1)

    n_taps = DW_LEN + LRU_LEN
    taps = jnp.concatenate([dw_conv_w, lru_conv_w], axis=1).reshape(depth * n_taps, cc // N_DEV)
    taps = all_gather("ag_taps", taps, me_s)
    taps = jnp.moveaxis(taps.reshape(N_DEV, depth, n_taps, cc // N_DEV), 0, 2).reshape(depth, n_taps, cc)
    dw_full, lcw_full = taps[:, :DW_LEN], taps[:, DW_LEN:]

    def vec(a, l):
        return a[l].reshape(1, -1)

    def gathered(name, l):
        return all_gather("ag_" + name, wts[name][l], me_s, BF16)

    saved = []
    u1 = rms_pre(h, vec(g_pre_mix, 0))
    loss_sum = dh = dbr = None
    for l in range(depth):
        wg = {n: gathered(n, l) for n in _BIG}
        wg["w_out"] = wg["w_out"].reshape(attn_w + cc + wl, d)
        wa_b, wi_b = lru_w_a[l].astype(BF16), lru_w_i[l].astype(BF16)
        proj = mm_proj(u1, wg["w_in"])
        y_attn = attn_fwd(proj, n_heads)
        cpre, y_conv = conv_fwd(proj, conv_col, cc, dw_full[l], vec(dw_conv_b, l), vec(conv_ln_g, l), vec(conv_ln_b, l))
        hs, y_lru = lru_fwd(proj, lru_col, wl, lcw_full[l], vec(lru_conv_b, l), wa_b, vec(lru_b_a, l), wi_b,
                            vec(lru_b_i, l), vec(lru_lambda, l))
        mixed = mix_fwd(y_attn, y_conv, y_lru, vec(g_attn_grp, l), vec(g_conv_grp, l), vec(g_lru_grp, l))
        o = mm_plain("mm_out", mixed, wg["w_out"], "nn", F32)
        h2, u2 = res_norm(h, o, vec(g_post_mix, l), vec(g_pre_ffn, l))
        gt, up, f = ffn_up(u2, wg["w_gate"], wg["w_up"])
        dn = mm_down(f, wg["w_down"])
        saved.append(dict(wg=wg, wa_b=wa_b, wi_b=wi_b, h=h, u1=u1, proj=proj, y_attn=y_attn, cpre=cpre, y_conv=y_conv,
                          hs=hs, y_lru=y_lru, mixed=mixed, o=o, h2=h2, u2=u2, gt=gt, up=up, f=f, dn=dn))
        if l + 1 < depth:
            h, u1 = res_norm(h2, dn, vec(g_post_ffn, l), vec(g_pre_mix, l + 1))
        else:
            loss_sum, dh, dbr, dg_post_ffn = final_loss(h2, dn, vec(g_post_ffn, l), target)

    loss = lax.psum(0.5 * loss_sum[0, 0] / d, MESH_AXES)

    small = {n: [None] * depth for n in _SMALL}
    big_parts = {n: [None] * depth for n in _BIG}
    for l in reversed(range(depth)):
        s = saved[l]
        wg = s["wg"]
        small["g_post_ffn"][l] = dg_post_ffn
        dgt, dup = ffn_bwd(dbr, wg["w_down"], s["gt"], s["up"])
        big_parts["w_down"][l] = reduce_scatter("down", mm_dw_rows("mm_dw_down", s["f"], dbr), core_s)
        big_parts["w_gate"][l] = reduce_scatter("gate", mm_dw_cols_stacked("mm_dw_gate", s["u2"], dgt), core_s)
        big_parts["w_up"][l] = reduce_scatter("gate", mm_dw_cols_stacked("mm_dw_gate", s["u2"], dup), core_s)
        du2 = mm_dx_ffn(dgt, wg["w_gate"], dup, wg["w_up"])
        dh2, small["g_pre_ffn"][l], do, small["g_post_mix"][l] = norm_bwd(
            dh, du2, s["h2"], vec(g_pre_ffn, l), (s["o"], vec(g_post_mix, l)))
        dmixed = mm_plain("mm_dmixed", do, wg["w_out"], "nt", F32)
        dw_out = mm_plain("mm_dw_out", s["mixed"], do, "tn", BF16)
        big_parts["w_out"][l] = reduce_scatter("out", dw_out.reshape(N_DEV, -1, d), core_s)
        (dya, dc, dyl, small["g_attn_grp"][l], small["g_conv_grp"][l], small["g_lru_grp"][l],
         small["conv_ln_g"][l], small["conv_ln_b"][l]) = mix_bwd(
            dmixed, s["y_attn"], s["y_conv"], s["y_lru"], s["cpre"], vec(g_attn_grp, l), vec(g_conv_grp, l),
            vec(g_lru_grp, l), vec(conv_ln_g, l), vec(conv_ln_b, l))
        dq, dk, dv = attn_bwd(s["proj"], dya, n_heads)
        dvg, small["dw_conv_w"][l], small["dw_conv_b"][l] = conv_bwd(s["proj"], conv_col, cc, dc, dw_full[l])
        (dxy, small["lru_conv_w"][l], small["lru_conv_b"][l], small["lru_w_a"][l], small["lru_b_a"][l],
         small["lru_w_i"][l], small["lru_b_i"][l], small["lru_lambda"][l]) = lru_bwd(
            s["proj"], lru_col, wl, s["hs"], dyl, lcw_full[l], vec(lru_conv_b, l), s["wa_b"], vec(lru_b_a, l),
            s["wi_b"], vec(lru_b_i, l), vec(lru_lambda, l))
        dproj = jnp.concatenate([dq, dk, dv, dvg, dxy], axis=1)
        big_parts["w_in"][l] = reduce_scatter("in", mm_dw_cols("mm_dw_in", s["u1"], dproj, N_DEV), core_s)
        du1 = mm_dx_cols("mm_dx_in", dproj, wg["w_in"])
        if l > 0:
            p = saved[l - 1]
            dh, small["g_pre_mix"][l], dbr, dg_post_ffn = norm_bwd(
                dh2, du1, s["h"], vec(g_pre_mix, l), (p["dn"], vec(g_post_ffn, l - 1)))
        else:
            dh, small["g_pre_mix"][l] = norm_bwd(dh2, du1, s["h"], vec(g_pre_mix, l))
    grad_x = dh[None]

    small_shapes = [(depth,) + tuple(wts[n].shape[1:]) if n not in _COL_SHARDED_SMALL
                    else (depth, wts[n].shape[1], cc) for n in _SMALL]
    part = _pack([jnp.stack([a.reshape(shp[1:]) for a in small[n]]) for n, shp in zip(_SMALL, small_shapes)])
    g_small = _unpack(sum_parts(all_gather("ag_small", part, me_s)), small_shapes)
    grads = {}
    for n, g in zip(_SMALL, g_small):
        if n in _COL_SHARDED_SMALL:
            g = lax.dynamic_slice_in_dim(g, me * (cc // N_DEV), cc // N_DEV, axis=2)
        grads[n] = g
    local_shapes = [tuple(wts[n].shape) for n in _SMALL]
    d_small, m_small, v_small = adam_small(
        _pack([wts[n] for n in _SMALL]), _pack([mom[n] for n in _SMALL]), _pack([var[n] for n in _SMALL]),
        _pack([grads[n] for n in _SMALL]))
    delta = dict(zip(_SMALL, _unpack(d_small, local_shapes)))
    new_m = dict(zip(_SMALL, _unpack(m_small, local_shapes)))
    new_v = dict(zip(_SMALL, _unpack(v_small, local_shapes)))

    for n in _BIG:
        shape = wts[n].shape
        _, rows, cols = big_parts[n][0][0].shape
        view = (depth, rows, cols)
        g, dl, mn, vn = adam_big("adam_" + n, wts[n].reshape(view), mom[n].reshape(view), var[n].reshape(view),
                                 big_parts[n], chip_s)
        grads[n], delta[n], new_m[n], new_v[n] = (a.reshape(shape) for a in (g, dl, mn, vn))

    return (loss, grad_x, *[grads[n] for n in _ALL], *[delta[n] for n in _ALL],
            *[new_m[n] for n in _ALL], *[new_v[n] for n in _ALL])
```

```python
import functools
import math

import jax
import jax.numpy as jnp
from jax import lax
from jax.experimental import pallas as pl
from jax.experimental.pallas import tpu as pltpu

F32 = jnp.float32
BF16 = jnp.bfloat16

N_DEV = 8
EPS = 1e-6
HEAD_DIM = 128
DW_LEN = 31
LRU_LEN = 4
LRU_BLOCKS = 4
LRU_C = 8.0
ATT_T = 256
ROW_T = 256
CONV_HALO = 32
LRU_HALO = 8
LANE = 128
SUBLANE = 8
PACK_ROWS = 512
VMEM_LIMIT = 56 * 1024 * 1024

ADAM_LR = 0.001
ADAM_B1 = 0.9
ADAM_B2 = 0.999
ADAM_EPS = 1e-08
ADAM_WD = 0.01
ADAM_STEP = 10

MESH_AXES = ("x", "y", "c")
_DIMS = {
    "nn": (((1,), (0,)), ((), ())),
    "nt": (((1,), (1,)), ((), ())),
    "tn": (((0,), (0,)), ((), ())),
}


def _params(n_axes):
    return pltpu.CompilerParams(
        dimension_semantics=("arbitrary",) * n_axes, vmem_limit_bytes=VMEM_LIMIT)


def _dot(a, b, mode="nn"):
    return lax.dot_general(a, b, _DIMS[mode], preferred_element_type=F32)


def _sigmoid(x):
    return 1.0 / (1.0 + jnp.exp(-x))


def _softplus(x):
    return jnp.maximum(x, 0.0) + jnp.log(1.0 + jnp.exp(-jnp.abs(x)))


def _neg_expm1(x):
    series = x * (1.0 + x * (0.5 + x * (1.0 / 6 + x * (1.0 / 24 + x * (1.0 / 120 + x * (1.0 / 720))))))
    return jnp.where(x > -0.25, -series, 1.0 - jnp.exp(x))


_GELU_C = math.sqrt(2.0 / math.pi)


def _gelu_and_grad(x):
    inner = _GELU_C * (x + 0.044715 * x * x * x)
    t = jnp.tanh(inner)
    val = 0.5 * x * (1.0 + t)
    grad = 0.5 * (1.0 + t) + 0.5 * x * (1.0 - t * t) * _GELU_C * (1.0 + 3 * 0.044715 * x * x)
    return val, grad


def _rms_stats(x):
    r = lax.rsqrt(jnp.mean(x * x, axis=-1, keepdims=True) + EPS)
    return x * r, r


def _rms_bwd(dy, x, g):
    xn, r = _rms_stats(x)
    dxn = dy * g
    dx = r * (dxn - xn * jnp.mean(dxn * xn, axis=-1, keepdims=True))
    return dx, jnp.sum(dy * xn, axis=0, keepdims=True)


def _row_spec(tr, width, col=0):
    return pl.BlockSpec((tr, width), lambda i, col=col: (i, col))


def _vec_spec(width):
    return pl.BlockSpec((1, width), lambda i: (0, 0))


def _matmul(name, mode, operands, in_specs, out_shape, out_spec, grid, out_block):
    npairs = len(operands) // 2
    nk = grid[2]

    def body(*refs):
        o_ref = refs[2 * npairs]

        def partial():
            acc = None
            for p in range(npairs):
                d = _dot(refs[2 * p][...], refs[2 * p + 1][...], mode)
                acc = d if acc is None else acc + d
            return acc

        if nk == 1:
            o_ref[...] = partial().astype(o_ref.dtype)
        else:
            acc_ref = refs[2 * npairs + 1]
            k = pl.program_id(2)

            @pl.when(k == 0)
            def _():
                acc_ref[...] = jnp.zeros_like(acc_ref)

            acc_ref[...] += partial()

            @pl.when(k == nk - 1)
            def _():
                o_ref[...] = acc_ref[...].astype(o_ref.dtype)

    return pl.pallas_call(
        body, name=name, grid=grid, in_specs=in_specs, out_specs=out_spec, out_shape=out_shape,
        scratch_shapes=[] if nk == 1 else [pltpu.VMEM(out_block, F32)],
        compiler_params=_params(3),
    )(*operands)


def _tile(n, t):
    if n <= t:
        return n
    return max(k for k in range(SUBLANE, t + 1, SUBLANE) if n % k == 0)


def mm_proj(u, w):
    t, d = u.shape
    nblk, _, nb = w.shape
    tm = _tile(t, 1024)
    return _matmul(
        "mm_proj", "nn", (u, w),
        [pl.BlockSpec((tm, d), lambda j, i, k: (i, 0)), pl.BlockSpec((None, d, nb), lambda j, i, k: (j, 0, 0))],
        jax.ShapeDtypeStruct((t, nblk * nb), F32), pl.BlockSpec((tm, nb), lambda j, i, k: (i, j)),
        (nblk, t // tm, 1), (tm, nb))


def mm_plain(name, a, b, mode, out_dtype):
    if mode == "nn":
        (m, kk), n = a.shape, b.shape[1]
    elif mode == "nt":
        (m, kk), n = a.shape, b.shape[0]
    else:
        (kk, m), n = a.shape, b.shape[1]
    tm, tn = _tile(m, 1024), _tile(n, 1024)
    a_spec = (pl.BlockSpec((kk, tm), lambda i, j, k: (0, i)) if mode == "tn"
              else pl.BlockSpec((tm, kk), lambda i, j, k: (i, 0)))
    b_spec = (pl.BlockSpec((tn, kk), lambda i, j, k: (j, 0)) if mode == "nt"
              else pl.BlockSpec((kk, tn), lambda i, j, k: (0, j)))
    return _matmul(
        name, mode, (a, b), [a_spec, b_spec],
        jax.ShapeDtypeStruct((m, n), out_dtype), pl.BlockSpec((tm, tn), lambda i, j, k: (i, j)),
        (m // tm, n // tn, 1), (tm, tn))


def mm_down(f, w):
    nblk, t, fb = f.shape
    d = w.shape[2]
    tm, tn = _tile(t, 1024), _tile(d, 1024)
    return _matmul(
        "mm_down", "nn", (f, w),
        [pl.BlockSpec((None, tm, fb), lambda i, j, k: (k, i, 0)), pl.BlockSpec((None, fb, tn), lambda i, j, k: (k, 0, j))],
        jax.ShapeDtypeStruct((t, d), F32), pl.BlockSpec((tm, tn), lambda i, j, k: (i, j)),
        (t // tm, d // tn, nblk), (tm, tn))


def mm_dw_rows(name, a, g):
    nblk, t, fb = a.shape
    d = g.shape[1]
    tn = _tile(d, 1024)
    return _matmul(
        name, "tn", (a, g),
        [pl.BlockSpec((None, t, fb), lambda j, n, k: (j, 0, 0)), pl.BlockSpec((t, tn), lambda j, n, k: (0, n))],
        jax.ShapeDtypeStruct((nblk, fb, d), BF16), pl.BlockSpec((None, fb, tn), lambda j, n, k: (j, 0, n)),
        (nblk, d // tn, 1), (fb, tn))


def mm_dw_cols_stacked(name, u, g):
    t, d = u.shape
    nblk, _, fb = g.shape
    tmd = _tile(d, 1024)
    return _matmul(
        name, "tn", (u, g),
        [pl.BlockSpec((t, tmd), lambda j, i, k: (0, i)), pl.BlockSpec((None, t, fb), lambda j, i, k: (j, 0, 0))],
        jax.ShapeDtypeStruct((nblk, d, fb), BF16), pl.BlockSpec((None, tmd, fb), lambda j, i, k: (j, i, 0)),
        (nblk, d // tmd, 1), (tmd, fb))


def mm_dw_cols(name, u, g, nblk):
    t, d = u.shape
    nb = g.shape[1] // nblk
    tmd = _tile(d, 1024)
    return _matmul(
        name, "tn", (u, g),
        [pl.BlockSpec((t, tmd), lambda j, i, k: (0, i)), pl.BlockSpec((t, nb), lambda j, i, k: (0, j))],
        jax.ShapeDtypeStruct((nblk, d, nb), BF16), pl.BlockSpec((None, tmd, nb), lambda j, i, k: (j, i, 0)),
        (nblk, d // tmd, 1), (tmd, nb))


def mm_dx_cols(name, g, w):
    t = g.shape[0]
    nblk, d, nb = w.shape
    tm, tn = _tile(t, 1024), _tile(d, 1024)
    return _matmul(
        name, "nt", (g, w),
        [pl.BlockSpec((tm, nb), lambda i, j, k: (i, k)), pl.BlockSpec((None, tn, nb), lambda i, j, k: (k, j, 0))],
        jax.ShapeDtypeStruct((t, d), F32), pl.BlockSpec((tm, tn), lambda i, j, k: (i, j)),
        (t // tm, d // tn, nblk), (tm, tn))


def mm_dx_ffn(dgt, wg, dup, wu):
    nblk, t, fb = dgt.shape
    d = wg.shape[1]
    tm, tn = _tile(t, 1024), _tile(d, 1024)
    a_spec = pl.BlockSpec((None, tm, fb), lambda i, j, k: (k, i, 0))
    b_spec = pl.BlockSpec((None, tn, fb), lambda i, j, k: (k, j, 0))
    return _matmul(
        "mm_dx_ffn", "nt", (dgt, wg, dup, wu), [a_spec, b_spec, a_spec, b_spec],
        jax.ShapeDtypeStruct((t, d), F32), pl.BlockSpec((tm, tn), lambda i, j, k: (i, j)),
        (t // tm, d // tn, nblk), (tm, tn))


def ffn_up(u, wg, wu):
    t, d = u.shape
    nblk, _, fb = wg.shape
    tm = _tile(t, 512)

    def body(u_ref, wg_ref, wu_ref, gt_ref, up_ref, f_ref):
        uu = u_ref[...]
        gt = _dot(uu, wg_ref[...])
        up = _dot(uu, wu_ref[...])
        gt_ref[...] = gt
        up_ref[...] = up
        f_ref[...] = (gt * _sigmoid(gt) * up).astype(BF16)

    w_spec = pl.BlockSpec((None, d, fb), lambda j, i: (j, 0, 0))
    o_spec = pl.BlockSpec((None, tm, fb), lambda j, i: (j, i, 0))
    return pl.pallas_call(
        body, name="ffn_up", grid=(nblk, t // tm),
        in_specs=[pl.BlockSpec((tm, d), lambda j, i: (i, 0)), w_spec, w_spec],
        out_specs=[o_spec, o_spec, o_spec],
        out_shape=[jax.ShapeDtypeStruct((nblk, t, fb), F32), jax.ShapeDtypeStruct((nblk, t, fb), F32),
                   jax.ShapeDtypeStruct((nblk, t, fb), BF16)],
        compiler_params=_params(2),
    )(u, wg, wu)


def ffn_bwd(dd, wd, gt, up):
    t, d = dd.shape
    nblk, fb, _ = wd.shape
    tm = _tile(t, 512)

    def body(dd_ref, wd_ref, gt_ref, up_ref, dgt_ref, dup_ref):
        df = _dot(dd_ref[...], wd_ref[...], "nt")
        g = gt_ref[...]
        s = _sigmoid(g)
        dgt_ref[...] = (df * up_ref[...] * s * (1.0 + g * (1.0 - s))).astype(BF16)
        dup_ref[...] = (df * g * s).astype(BF16)

    s_spec = pl.BlockSpec((None, tm, fb), lambda j, i: (j, i, 0))
    return pl.pallas_call(
        body, name="ffn_bwd", grid=(nblk, t // tm),
        in_specs=[pl.BlockSpec((tm, d), lambda j, i: (i, 0)), pl.BlockSpec((None, fb, d), lambda j, i: (j, 0, 0)),
                  s_spec, s_spec],
        out_specs=[s_spec, s_spec],
        out_shape=[jax.ShapeDtypeStruct((nblk, t, fb), BF16)] * 2,
        compiler_params=_params(2),
    )(dd, wd, gt, up)


def rms_pre(h, g):
    t, d = h.shape
    tr = _tile(t, ROW_T)

    def body(h_ref, g_ref, o_ref):
        xn, _ = _rms_stats(h_ref[...])
        o_ref[...] = (xn * g_ref[...]).astype(BF16)

    return pl.pallas_call(
        body, name="rms_pre", grid=(t // tr,),
        in_specs=[_row_spec(tr, d), _vec_spec(d)], out_specs=_row_spec(tr, d),
        out_shape=jax.ShapeDtypeStruct((t, d), BF16), compiler_params=_params(1),
    )(h, g)


def res_norm(h, o, g_post, g_pre):
    t, d = h.shape
    tr = _tile(t, ROW_T)

    def body(h_ref, o_ref, gpo_ref, gpr_ref, h2_ref, u_ref):
        on, _ = _rms_stats(o_ref[...])
        h2 = h_ref[...] + on * gpo_ref[...]
        h2_ref[...] = h2
        hn, _ = _rms_stats(h2)
        u_ref[...] = (hn * gpr_ref[...]).astype(BF16)

    return pl.pallas_call(
        body, name="res_norm", grid=(t // tr,),
        in_specs=[_row_spec(tr, d), _row_spec(tr, d), _vec_spec(d), _vec_spec(d)],
        out_specs=[_row_spec(tr, d), _row_spec(tr, d)],
        out_shape=[jax.ShapeDtypeStruct((t, d), F32), jax.ShapeDtypeStruct((t, d), BF16)],
        compiler_params=_params(1),
    )(h, o, g_post, g_pre)


def final_loss(h2, dbr, g_post, target):
    t, d = h2.shape
    tr = _tile(t, ROW_T)

    def body(h2_ref, d_ref, g_ref, tg_ref, loss_ref, dy_ref, dd_ref, dg_ref):
        i = pl.program_id(0)

        @pl.when(i == 0)
        def _():
            loss_ref[...] = jnp.zeros_like(loss_ref)
            dg_ref[...] = jnp.zeros_like(dg_ref)

        x = d_ref[...]
        g = g_ref[...]
        xn, _ = _rms_stats(x)
        diff = h2_ref[...] + xn * g - tg_ref[...]
        loss_ref[...] += jnp.sum(jnp.sum(diff * diff, axis=1, keepdims=True), axis=0, keepdims=True)
        dy = diff * (1.0 / d)
        dy_ref[...] = dy
        dx, dg = _rms_bwd(dy, x, g)
        dd_ref[...] = dx.astype(BF16)
        dg_ref[...] += dg

    return pl.pallas_call(
        body, name="final_loss", grid=(t // tr,),
        in_specs=[_row_spec(tr, d), _row_spec(tr, d), _vec_spec(d), _row_spec(tr, d)],
        out_specs=[pl.BlockSpec((1, 1), lambda i: (0, 0)), _row_spec(tr, d), _row_spec(tr, d), _vec_spec(d)],
        out_shape=[jax.ShapeDtypeStruct((1, 1), F32), jax.ShapeDtypeStruct((t, d), F32),
                   jax.ShapeDtypeStruct((t, d), BF16), jax.ShapeDtypeStruct((1, d), F32)],
        compiler_params=_params(1),
    )(h2, dbr, g_post, target)


def norm_bwd(dh_out, du, h_in, g_pre, prev=None):
    t, d = h_in.shape
    tr = _tile(t, ROW_T)
    with_prev = prev is not None

    def body(*refs):
        if with_prev:
            dho_ref, du_ref, h_ref, gpr_ref, br_ref, gpo_ref, dh_ref, dgpr_ref, dbr_ref, dgpo_ref = refs
        else:
            dho_ref, du_ref, h_ref, gpr_ref, dh_ref, dgpr_ref = refs
        i = pl.program_id(0)

        @pl.when(i == 0)
        def _():
            dgpr_ref[...] = jnp.zeros_like(dgpr_ref)
            if with_prev:
                dgpo_ref[...] = jnp.zeros_like(dgpo_ref)

        dx, dg = _rms_bwd(du_ref[...], h_ref[...], gpr_ref[...])
        dh = dho_ref[...] + dx
        dh_ref[...] = dh
        dgpr_ref[...] += dg
        if with_prev:
            dbr, dg2 = _rms_bwd(dh, br_ref[...], gpo_ref[...])
            dbr_ref[...] = dbr.astype(BF16)
            dgpo_ref[...] += dg2

    row, vec = _row_spec(tr, d), _vec_spec(d)
    in_specs = [row, row, row, vec] + ([row, vec] if with_prev else [])
    out_specs = [row, vec] + ([row, vec] if with_prev else [])
    out_shape = [jax.ShapeDtypeStruct((t, d), F32), jax.ShapeDtypeStruct((1, d), F32)]
    if with_prev:
        out_shape += [jax.ShapeDtypeStruct((t, d), BF16), jax.ShapeDtypeStruct((1, d), F32)]
    args = (dh_out, du, h_in, g_pre) + (tuple(prev) if with_prev else ())
    return pl.pallas_call(
        body, name="norm_bwd_chain" if with_prev else "norm_bwd_first", grid=(t // tr,),
        in_specs=in_specs, out_specs=out_specs, out_shape=out_shape, compiler_params=_params(1),
    )(*args)


def mix_fwd(ya, yc, yl, ga, gc, gl):
    t, wa = ya.shape
    wc, wl = yc.shape[1], yl.shape[1]
    tr = _tile(t, ROW_T)

    def body(ya_ref, yc_ref, yl_ref, ga_ref, gc_ref, gl_ref, o_ref):
        o_ref[:, pl.ds(0, wa)] = (_rms_stats(ya_ref[...])[0] * ga_ref[...]).astype(BF16)
        o_ref[:, pl.ds(wa, wc)] = (_rms_stats(yc_ref[...])[0] * gc_ref[...]).astype(BF16)
        o_ref[:, pl.ds(wa + wc, wl)] = (_rms_stats(yl_ref[...])[0] * gl_ref[...]).astype(BF16)

    return pl.pallas_call(
        body, name="mix_fwd", grid=(t // tr,),
        in_specs=[_row_spec(tr, wa), _row_spec(tr, wc), _row_spec(tr, wl), _vec_spec(wa), _vec_spec(wc), _vec_spec(wl)],
        out_specs=_row_spec(tr, wa + wc + wl),
        out_shape=jax.ShapeDtypeStruct((t, wa + wc + wl), BF16), compiler_params=_params(1),
    )(ya, yc, yl, ga, gc, gl)


def mix_bwd(dmixed, ya, yc, yl, cpre, ga, gc, gl, lng, lnb):
    t, wa = ya.shape
    wc, wl = yc.shape[1], yl.shape[1]
    tr = _tile(t, ROW_T)

    def body(dm_ref, ya_ref, yc_ref, yl_ref, c_ref, ga_ref, gc_ref, gl_ref, lg_ref, lb_ref,
             dya_ref, dc_ref, dyl_ref, dga_ref, dgc_ref, dgl_ref, dlg_ref, dlb_ref):
        i = pl.program_id(0)

        @pl.when(i == 0)
        def _():
            for r in (dga_ref, dgc_ref, dgl_ref, dlg_ref, dlb_ref):
                r[...] = jnp.zeros_like(r)

        dya, dga = _rms_bwd(dm_ref[:, pl.ds(0, wa)], ya_ref[...], ga_ref[...])
        dya_ref[...] = dya
        dga_ref[...] += dga
        dyl, dgl = _rms_bwd(dm_ref[:, pl.ds(wa + wc, wl)], yl_ref[...], gl_ref[...])
        dyl_ref[...] = dyl
        dgl_ref[...] += dgl
        dyc, dgc = _rms_bwd(dm_ref[:, pl.ds(wa, wc)], yc_ref[...], gc_ref[...])
        dgc_ref[...] += dgc
        c = c_ref[...]
        xc = c - jnp.mean(c, axis=-1, keepdims=True)
        rstd = lax.rsqrt(jnp.mean(xc * xc, axis=-1, keepdims=True) + EPS)
        xhat = xc * rstd
        ln = xhat * lg_ref[...] + lb_ref[...]
        s = _sigmoid(ln)
        dln = dyc * s * (1.0 + ln * (1.0 - s))
        dlg_ref[...] += jnp.sum(dln * xhat, axis=0, keepdims=True)
        dlb_ref[...] += jnp.sum(dln, axis=0, keepdims=True)
        dxh = dln * lg_ref[...]
        dc_ref[...] = rstd * (dxh - jnp.mean(dxh, axis=-1, keepdims=True)
                              - xhat * jnp.mean(dxh * xhat, axis=-1, keepdims=True))

    return pl.pallas_call(
        body, name="mix_bwd", grid=(t // tr,),
        in_specs=[_row_spec(tr, wa + wc + wl), _row_spec(tr, wa), _row_spec(tr, wc), _row_spec(tr, wl), _row_spec(tr, wc),
                  _vec_spec(wa), _vec_spec(wc), _vec_spec(wl), _vec_spec(wc), _vec_spec(wc)],
        out_specs=[_row_spec(tr, wa), _row_spec(tr, wc), _row_spec(tr, wl),
                   _vec_spec(wa), _vec_spec(wc), _vec_spec(wl), _vec_spec(wc), _vec_spec(wc)],
        out_shape=[jax.ShapeDtypeStruct((t, wa), F32), jax.ShapeDtypeStruct((t, wc), F32), jax.ShapeDtypeStruct((t, wl), F32),
                   jax.ShapeDtypeStruct((1, wa), F32), jax.ShapeDtypeStruct((1, wc), F32), jax.ShapeDtypeStruct((1, wl), F32),
                   jax.ShapeDtypeStruct((1, wc), F32), jax.ShapeDtypeStruct((1, wc), F32)],
        compiler_params=_params(1),
    )(dmixed, ya, yc, yl, cpre, ga, gc, gl, lng, lnb)


def _hi_lo(x):
    hi = x.astype(BF16)
    return hi, (x - hi.astype(F32)).astype(BF16)


def _att_block(qb, kt, jj, scale, lower, tri_gt):
    z = _dot(qb, kt, "nt") * scale
    sp = _softplus(z)
    mask = jnp.logical_or(jj > 0, lower)
    lk = jnp.where(mask, -sp, 0.0)
    hi, lo = _hi_lo(lk)
    logw = (z - sp) + _dot(hi, tri_gt) + _dot(lo, tri_gt)
    return z, sp, mask, lk, logw


def attn_fwd(proj, n_heads):
    t = proj.shape[0]
    tb = _tile(t, ATT_T)
    nq = t // tb
    scale = HEAD_DIM ** -0.5

    def body(q_ref, k_ref, v_ref, o_ref, kb_ref, vb_ref):
        kb_ref[...] = k_ref[...].astype(BF16)
        vb_ref[...] = v_ref[...].astype(BF16)
        row = lax.broadcasted_iota(jnp.int32, (tb, tb), 0)
        col = lax.broadcasted_iota(jnp.int32, (tb, tb), 1)
        lower = col < row
        tri_gt = (row > col).astype(BF16)

        def qblock(i, _):
            q0 = pl.multiple_of(i * tb, tb)
            qb = q_ref[pl.ds(q0, tb), :].astype(BF16)

            def kblock(jj, carry):
                acc, run = carry
                k0 = pl.multiple_of((i - jj) * tb, tb)
                _, _, mask, lk, logw = _att_block(qb, kb_ref[pl.ds(k0, tb), :], jj, scale, lower, tri_gt)
                w = jnp.where(mask, jnp.exp(logw + run), 0.0)
                acc = acc + _dot(w.astype(BF16), vb_ref[pl.ds(k0, tb), :])
                return acc, run + jnp.sum(lk, axis=1, keepdims=True)

            acc, _ = lax.fori_loop(0, i + 1, kblock, (jnp.zeros((tb, HEAD_DIM), F32), jnp.zeros((tb, 1), F32)))
            o_ref[pl.ds(q0, tb), :] = acc
            return 0

        lax.fori_loop(0, nq, qblock, 0)

    def col_spec(base):
        return pl.BlockSpec((t, HEAD_DIM), lambda h, base=base: (0, base + h))

    return pl.pallas_call(
        body, name="attn_fwd", grid=(n_heads,),
        in_specs=[col_spec(0), col_spec(n_heads), col_spec(2 * n_heads)],
        out_specs=col_spec(0),
        out_shape=jax.ShapeDtypeStruct((t, n_heads * HEAD_DIM), F32),
        scratch_shapes=[pltpu.VMEM((t, HEAD_DIM), BF16), pltpu.VMEM((t, HEAD_DIM), BF16)],
        compiler_params=_params(1),
    )(proj, proj, proj)


def attn_bwd(proj, dy, n_heads):
    t = proj.shape[0]
    tb = _tile(t, ATT_T)
    nq = t // tb
    scale = HEAD_DIM ** -0.5

    def body(q_ref, k_ref, v_ref, dy_ref, dq_ref, dk_ref, dv_ref, qb_ref, kb_ref, vb_ref, dob_ref, dk_acc, dv_acc, run_s):
        qb_ref[...] = q_ref[...].astype(BF16)
        kb_ref[...] = k_ref[...].astype(BF16)
        vb_ref[...] = v_ref[...].astype(BF16)
        dob_ref[...] = dy_ref[...].astype(BF16)
        dk_acc[...] = jnp.zeros_like(dk_acc)
        dv_acc[...] = jnp.zeros_like(dv_acc)
        row = lax.broadcasted_iota(jnp.int32, (tb, tb), 0)
        col = lax.broadcasted_iota(jnp.int32, (tb, tb), 1)
        lower = col < row
        tri_gt = (row > col).astype(BF16)
        tri_lt = (row < col).astype(BF16)

        def qblock(i, _):
            q0 = pl.multiple_of(i * tb, tb)
            qb = qb_ref[pl.ds(q0, tb), :]
            dob = dob_ref[pl.ds(q0, tb), :]

            def sweep(jj, run):
                k0 = pl.multiple_of((i - jj) * tb, tb)
                sp = _softplus(_dot(qb, kb_ref[pl.ds(k0, tb), :], "nt") * scale)
                run_s[i - jj] = run
                return run + jnp.sum(jnp.where(jnp.logical_or(jj > 0, lower), -sp, 0.0), axis=1, keepdims=True)

            lax.fori_loop(0, i + 1, sweep, jnp.zeros((tb, 1), F32))

            def kblock(kbi, carry):
                dq, gsum = carry
                k0 = pl.multiple_of(kbi * tb, tb)
                kt = kb_ref[pl.ds(k0, tb), :]
                vt = vb_ref[pl.ds(k0, tb), :]
                z, sp, mask, _, logw = _att_block(qb, kt, i - kbi, scale, lower, tri_gt)
                w = jnp.where(mask, jnp.exp(logw + run_s[kbi]), 0.0)
                g = w * _dot(dob, vt, "nt")
                hi, lo = _hi_lo(g)
                before = _dot(hi, tri_lt) + _dot(lo, tri_lt) + gsum
                sig = jnp.exp(z - sp)
                dz = jnp.where(mask, g * (1.0 - sig) - before * sig, 0.0) * scale
                dzb = dz.astype(BF16)
                dk_acc[pl.ds(k0, tb), :] += _dot(dzb, qb, "tn")
                dv_acc[pl.ds(k0, tb), :] += _dot(w.astype(BF16), dob, "tn")
                return dq + _dot(dzb, kt), gsum + jnp.sum(g, axis=1, keepdims=True)

            dq, _ = lax.fori_loop(0, i + 1, kblock, (jnp.zeros((tb, HEAD_DIM), F32), jnp.zeros((tb, 1), F32)))
            dq_ref[pl.ds(q0, tb), :] = dq.astype(BF16)
            return 0

        lax.fori_loop(0, nq, qblock, 0)
        dk_ref[...] = dk_acc[...].astype(BF16)
        dv_ref[...] = dv_acc[...].astype(BF16)

    def col_spec(base):
        return pl.BlockSpec((t, HEAD_DIM), lambda h, base=base: (0, base + h))

    width = n_heads * HEAD_DIM
    return pl.pallas_call(
        body, name="attn_bwd", grid=(n_heads,),
        in_specs=[col_spec(0), col_spec(n_heads), col_spec(2 * n_heads), col_spec(0)],
        out_specs=[col_spec(0), col_spec(0), col_spec(0)],
        out_shape=[jax.ShapeDtypeStruct((t, width), BF16)] * 3,
        scratch_shapes=[pltpu.VMEM((t, HEAD_DIM), BF16)] * 4 + [pltpu.VMEM((t, HEAD_DIM), F32)] * 2
        + [pltpu.VMEM((nq, tb, 1), F32)],
        compiler_params=_params(1),
    )(proj, proj, proj, dy)


def _glu_halo(vc, gc, vp, gp, ubuf, i, tt, halo):
    uprev = vp[pl.ds(tt - halo, halo), :] * _sigmoid(gp[pl.ds(tt - halo, halo), :])
    ubuf[pl.ds(0, halo), :] = jnp.where(i > 0, uprev, 0.0)
    ubuf[pl.ds(halo, tt), :] = vc[...] * _sigmoid(gc[...])


def conv_fwd(proj, col0, cc, w, b, lng, lnb):
    t = proj.shape[0]
    tt = _tile(t, ROW_T)
    vi, gi = col0 // cc, col0 // cc + 1
    off = CONV_HALO - (DW_LEN - 1)

    def body(vc, gc, vp, gp, w_ref, b_ref, lg_ref, lb_ref, c_ref, y_ref, ubuf):
        i = pl.program_id(0)
        _glu_halo(vc, gc, vp, gp, ubuf, i, tt, CONV_HALO)
        for ch in range(cc // LANE):
            sl = pl.ds(ch * LANE, LANE)
            acc = jnp.zeros((tt, LANE), F32) + b_ref[:, sl]
            for tap in range(DW_LEN):
                acc = acc + w_ref[pl.ds(tap, 1), sl] * ubuf[pl.ds(off + tap, tt), sl]
            c_ref[:, sl] = acc
        c = c_ref[...]
        xc = c - jnp.mean(c, axis=-1, keepdims=True)
        ln = xc * lax.rsqrt(jnp.mean(xc * xc, axis=-1, keepdims=True) + EPS) * lg_ref[...] + lb_ref[...]
        y_ref[...] = ln * _sigmoid(ln)

    cur = lambda c: pl.BlockSpec((tt, cc), lambda i, c=c: (i, c))
    prev = lambda c: pl.BlockSpec((tt, cc), lambda i, c=c: (jnp.maximum(i - 1, 0), c))
    return pl.pallas_call(
        body, name="conv_fwd", grid=(t // tt,),
        in_specs=[cur(vi), cur(gi), prev(vi), prev(gi), pl.BlockSpec((DW_LEN, cc), lambda i: (0, 0)),
                  _vec_spec(cc), _vec_spec(cc), _vec_spec(cc)],
        out_specs=[_row_spec(tt, cc), _row_spec(tt, cc)],
        out_shape=[jax.ShapeDtypeStruct((t, cc), F32)] * 2,
        scratch_shapes=[pltpu.VMEM((CONV_HALO + tt, cc), F32)],
        compiler_params=_params(1),
    )(proj, proj, proj, proj, w, b, lng, lnb)


def conv_bwd(proj, col0, cc, dc, w):
    t = proj.shape[0]
    tt = _tile(t, ROW_T)
    nt = t // tt
    vi, gi = col0 // cc, col0 // cc + 1
    off = CONV_HALO - (DW_LEN - 1)

    def body(vc, gc, vp, gp, dcc, dcn, w_ref, dvg_ref, dw_ref, db_ref, ubuf, dbuf):
        i = pl.program_id(0)

        @pl.when(i == 0)
        def _():
            dw_ref[...] = jnp.zeros_like(dw_ref)
            db_ref[...] = jnp.zeros_like(db_ref)

        _glu_halo(vc, gc, vp, gp, ubuf, i, tt, CONV_HALO)
        dbuf[pl.ds(0, tt), :] = dcc[...]
        dbuf[pl.ds(tt, CONV_HALO), :] = jnp.where(i < nt - 1, dcn[pl.ds(0, CONV_HALO), :], 0.0)
        db_ref[...] += jnp.sum(dcc[...], axis=0, keepdims=True)
        for ch in range(cc // LANE):
            sl = pl.ds(ch * LANE, LANE)
            dcv = dbuf[pl.ds(0, tt), sl]
            du = jnp.zeros((tt, LANE), F32)
            for tap in range(DW_LEN):
                du = du + w_ref[pl.ds(tap, 1), sl] * dbuf[pl.ds(DW_LEN - 1 - tap, tt), sl]
                dw_ref[pl.ds(tap, 1), sl] += jnp.sum(dcv * ubuf[pl.ds(off + tap, tt), sl], axis=0, keepdims=True)
            s = _sigmoid(gc[:, sl])
            val = vc[:, sl]
            dvg_ref[:, sl] = (du * s).astype(BF16)
            dvg_ref[:, pl.ds(cc + ch * LANE, LANE)] = (du * val * s * (1.0 - s)).astype(BF16)

    cur = lambda c: pl.BlockSpec((tt, cc), lambda i, c=c: (i, c))
    prev = lambda c: pl.BlockSpec((tt, cc), lambda i, c=c: (jnp.maximum(i - 1, 0), c))
    return pl.pallas_call(
        body, name="conv_bwd", grid=(nt,),
        in_specs=[cur(vi), cur(gi), prev(vi), prev(gi), _row_spec(tt, cc),
                  pl.BlockSpec((tt, cc), lambda i: (jnp.minimum(i + 1, nt - 1), 0)),
                  pl.BlockSpec((DW_LEN, cc), lambda i: (0, 0))],
        out_specs=[_row_spec(tt, 2 * cc), pl.BlockSpec((DW_LEN, cc), lambda i: (0, 0)), _vec_spec(cc)],
        out_shape=[jax.ShapeDtypeStruct((t, 2 * cc), BF16), jax.ShapeDtypeStruct((DW_LEN, cc), F32),
                   jax.ShapeDtypeStruct((1, cc), F32)],
        scratch_shapes=[pltpu.VMEM((CONV_HALO + tt, cc), F32), pltpu.VMEM((tt + CONV_HALO, cc), F32)],
        compiler_params=_params(1),
    )(proj, proj, proj, proj, dc, dc, w)


def _lru_gates(xbuf, cw_ref, cb_ref, wa_ref, ba_ref, wi_ref, bi_ref, lam_ref, tt, wl):
    bd = wl // LRU_BLOCKS
    xr = jnp.zeros((tt, wl), F32) + cb_ref[...]
    for tap in range(LRU_LEN):
        xr = xr + cw_ref[pl.ds(tap, 1), :] * xbuf[pl.ds(LRU_HALO - (LRU_LEN - 1) + tap, tt), :]
    xb = xr.astype(BF16)
    ga = jnp.concatenate([_dot(xb[:, n * bd:(n + 1) * bd], wa_ref[n]) for n in range(LRU_BLOCKS)], axis=1) + ba_ref[...]
    gi = jnp.concatenate([_dot(xb[:, n * bd:(n + 1) * bd], wi_ref[n]) for n in range(LRU_BLOCKS)], axis=1) + bi_ref[...]
    r = _sigmoid(ga)
    ig = _sigmoid(gi)
    spl = _softplus(-lam_ref[...])
    log_a = -LRU_C * r * spl
    a = jnp.exp(log_a)
    m = jnp.sqrt(_neg_expm1(2.0 * log_a))
    return xr, xb, r, ig, spl, a, m


def _group_scan(a8, b8, reverse):
    rid = lax.broadcasted_iota(jnp.int32, a8.shape, 0)
    aa, bb = a8, b8
    for dist in (1, 2, 4):
        shift = SUBLANE - dist if reverse else dist
        a_sh = pltpu.roll(aa, shift, 0)
        b_sh = pltpu.roll(bb, shift, 0)
        valid = (rid < SUBLANE - dist) if reverse else (rid >= dist)
        bb = jnp.where(valid, aa * b_sh + bb, bb)
        aa = jnp.where(valid, aa * a_sh, aa)
    return aa, bb


def _pick_row(x8, r):
    rid = lax.broadcasted_iota(jnp.int32, x8.shape, 0)
    return jnp.sum(jnp.where(rid == r, x8, 0.0), axis=0, keepdims=True)


def lru_fwd(proj, col0, wl, cw, cb, wa, ba, wi, bi, lam):
    t = proj.shape[0]
    tt = _tile(t, ROW_T)
    xi, yi = col0 // wl, col0 // wl + 1

    def body(xc, xp, ry, cw_ref, cb_ref, wa_ref, ba_ref, wi_ref, bi_ref, lam_ref, hs_ref, y_ref,
             xbuf, a_s, b_s, hcar):
        i = pl.program_id(0)

        @pl.when(i == 0)
        def _():
            hcar[...] = jnp.zeros_like(hcar)

        xbuf[pl.ds(0, LRU_HALO), :] = jnp.where(i > 0, xp[pl.ds(tt - LRU_HALO, LRU_HALO), :], 0.0)
        xbuf[pl.ds(LRU_HALO, tt), :] = xc[...]
        xr, _, _, ig, _, a, m = _lru_gates(xbuf, cw_ref, cb_ref, wa_ref, ba_ref, wi_ref, bi_ref, lam_ref, tt, wl)
        a_s[...] = a
        b_s[...] = m * ig * xr

        def group(gidx, h):
            r0 = pl.multiple_of(gidx * SUBLANE, SUBLANE)
            aa, bb = _group_scan(a_s[pl.ds(r0, SUBLANE), :], b_s[pl.ds(r0, SUBLANE), :], False)
            h8 = aa * h + bb
            hs_ref[pl.ds(r0, SUBLANE), :] = h8
            return _pick_row(h8, SUBLANE - 1)

        hcar[...] = lax.fori_loop(0, tt // SUBLANE, group, hcar[...])
        gel, _ = _gelu_and_grad(ry[...])
        y_ref[...] = hs_ref[...] * gel

    cur = lambda c: pl.BlockSpec((tt, wl), lambda i, c=c: (i, c))
    full = lambda shape: pl.BlockSpec(shape, lambda i: (0,) * len(shape))
    return pl.pallas_call(
        body, name="lru_fwd", grid=(t // tt,),
        in_specs=[cur(xi), pl.BlockSpec((tt, wl), lambda i: (jnp.maximum(i - 1, 0), xi)), cur(yi),
                  full((LRU_LEN, wl)), _vec_spec(wl), full(wa.shape), _vec_spec(wl), full(wi.shape), _vec_spec(wl),
                  _vec_spec(wl)],
        out_specs=[_row_spec(tt, wl), _row_spec(tt, wl)],
        out_shape=[jax.ShapeDtypeStruct((t, wl), F32)] * 2,
        scratch_shapes=[pltpu.VMEM((LRU_HALO + tt, wl), F32), pltpu.VMEM((tt, wl), F32), pltpu.VMEM((tt, wl), F32),
                        pltpu.VMEM((1, wl), F32)],
        compiler_params=_params(1),
    )(proj, proj, proj, cw, cb, wa, ba, wi, bi, lam)


def lru_bwd(proj, col0, wl, hs, dy, cw, cb, wa, ba, wi, bi, lam):
    t = proj.shape[0]
    tt = _tile(t, ROW_T)
    nt = t // tt
    xi, yi = col0 // wl, col0 // wl + 1
    bd = wl // LRU_BLOCKS

    def body(xc, xp, ry, hc, hp, dy_ref, cw_ref, cb_ref, wa_ref, ba_ref, wi_ref, bi_ref, lam_ref,
             dxy_ref, dcw_ref, dcb_ref, dwa_ref, dba_ref, dwi_ref, dbi_ref, dlam_ref,
             xbuf, hbuf, abuf, e_s, dh_s, dxbuf, dhcar):
        i = pl.program_id(0)
        first = i == 0

        @pl.when(first)
        def _():
            for r in (dcw_ref, dcb_ref, dwa_ref, dba_ref, dwi_ref, dbi_ref, dlam_ref, dhcar):
                r[...] = jnp.zeros_like(r)
            abuf[pl.ds(tt, LRU_HALO), :] = jnp.zeros((LRU_HALO, wl), F32)
            dxbuf[pl.ds(tt, LRU_HALO), :] = jnp.zeros((LRU_HALO, wl), F32)

        has_prev = i < nt - 1
        xbuf[pl.ds(0, LRU_HALO), :] = jnp.where(has_prev, xp[pl.ds(tt - LRU_HALO, LRU_HALO), :], 0.0)
        xbuf[pl.ds(LRU_HALO, tt), :] = xc[...]
        hbuf[pl.ds(0, LRU_HALO), :] = jnp.where(has_prev, hp[pl.ds(tt - LRU_HALO, LRU_HALO), :], 0.0)
        hbuf[pl.ds(LRU_HALO, tt), :] = hc[...]
        xr, xb, r, ig, spl, a, m = _lru_gates(xbuf, cw_ref, cb_ref, wa_ref, ba_ref, wi_ref, bi_ref, lam_ref, tt, wl)
        gel, dgel = _gelu_and_grad(ry[...])
        dyv = dy_ref[...]
        e_s[...] = dyv * gel
        dxy_ref[:, pl.ds(wl, wl)] = (dyv * hc[...] * dgel).astype(BF16)
        abuf[pl.ds(0, tt), :] = a
        a_next = abuf[pl.ds(1, tt), :]
        dh_s[...] = a_next

        def group(it, dh_in):
            r0 = pl.multiple_of((tt // SUBLANE - 1 - it) * SUBLANE, SUBLANE)
            aa, bb = _group_scan(dh_s[pl.ds(r0, SUBLANE), :], e_s[pl.ds(r0, SUBLANE), :], True)
            dh8 = aa * dh_in + bb
            dh_s[pl.ds(r0, SUBLANE), :] = dh8
            return _pick_row(dh8, 0)

        dhcar[...] = lax.fori_loop(0, tt // SUBLANE, group, dhcar[...])
        abuf[pl.ds(tt, LRU_HALO), :] = a[0:LRU_HALO, :]
        dh = dh_s[...]
        h_m1 = hbuf[pl.ds(LRU_HALO - 1, tt), :]
        dlog_a = dh * h_m1 * a - dh * ig * xr * (a * a / m)
        dig = dh * m * xr
        dxr = dh * m * ig
        dga = dlog_a * (-LRU_C) * spl * r * (1.0 - r)
        dgi = dig * ig * (1.0 - ig)
        dlam_ref[...] += jnp.sum(dlog_a * r, axis=0, keepdims=True) * (LRU_C * _sigmoid(-lam_ref[...]))
        dba_ref[...] += jnp.sum(dga, axis=0, keepdims=True)
        dbi_ref[...] += jnp.sum(dgi, axis=0, keepdims=True)
        dgab = dga.astype(BF16)
        dgib = dgi.astype(BF16)
        back = []
        for n in range(LRU_BLOCKS):
            sl = slice(n * bd, (n + 1) * bd)
            dwa_ref[n] += _dot(xb[:, sl], dgab[:, sl], "tn")
            dwi_ref[n] += _dot(xb[:, sl], dgib[:, sl], "tn")
            back.append(_dot(dgab[:, sl], wa_ref[n], "nt") + _dot(dgib[:, sl], wi_ref[n], "nt"))
        dxr = dxr + jnp.concatenate(back, axis=1)
        dcb_ref[...] += jnp.sum(dxr, axis=0, keepdims=True)
        dxbuf[pl.ds(0, tt), :] = dxr
        drx = jnp.zeros((tt, wl), F32)
        for tap in range(LRU_LEN):
            drx = drx + cw_ref[pl.ds(tap, 1), :] * dxbuf[pl.ds(LRU_LEN - 1 - tap, tt), :]
            dcw_ref[pl.ds(tap, 1), :] += jnp.sum(
                dxr * xbuf[pl.ds(LRU_HALO - (LRU_LEN - 1) + tap, tt), :], axis=0, keepdims=True)
        dxbuf[pl.ds(tt, LRU_HALO), :] = dxr[0:LRU_HALO, :]
        dxy_ref[:, pl.ds(0, wl)] = drx.astype(BF16)

    rev = lambda c: pl.BlockSpec((tt, wl), lambda i, c=c: (nt - 1 - i, c))
    rev_prev = lambda c: pl.BlockSpec((tt, wl), lambda i, c=c: (jnp.maximum(nt - 2 - i, 0), c))
    full = lambda shape: pl.BlockSpec(shape, lambda i: (0,) * len(shape))
    vec = _vec_spec(wl)
    return pl.pallas_call(
        body, name="lru_bwd", grid=(nt,),
        in_specs=[rev(xi), rev_prev(xi), rev(yi), rev(0), rev_prev(0), rev(0),
                  full((LRU_LEN, wl)), vec, full(wa.shape), vec, full(wi.shape), vec, vec],
        out_specs=[pl.BlockSpec((tt, 2 * wl), lambda i: (nt - 1 - i, 0)), full((LRU_LEN, wl)), vec,
                   full(wa.shape), vec, full(wi.shape), vec, vec],
        out_shape=[jax.ShapeDtypeStruct((t, 2 * wl), BF16), jax.ShapeDtypeStruct((LRU_LEN, wl), F32),
                   jax.ShapeDtypeStruct((1, wl), F32), jax.ShapeDtypeStruct(wa.shape, F32),
                   jax.ShapeDtypeStruct((1, wl), F32), jax.ShapeDtypeStruct(wi.shape, F32),
                   jax.ShapeDtypeStruct((1, wl), F32), jax.ShapeDtypeStruct((1, wl), F32)],
        scratch_shapes=[pltpu.VMEM((LRU_HALO + tt, wl), F32), pltpu.VMEM((LRU_HALO + tt, wl), F32),
                        pltpu.VMEM((tt + LRU_HALO, wl), F32), pltpu.VMEM((tt, wl), F32), pltpu.VMEM((tt, wl), F32),
                        pltpu.VMEM((tt + LRU_HALO, wl), F32), pltpu.VMEM((1, wl), F32)],
        compiler_params=_params(1),
    )(proj, proj, proj, hs, hs, dy, cw, cb, wa, ba, wi, bi, lam)


def _adamw(w, g, m, v):
    m = ADAM_B1 * m + (1.0 - ADAM_B1) * g
    v = ADAM_B2 * v + (1.0 - ADAM_B2) * (g * g)
    m_hat = m / (1.0 - ADAM_B1 ** ADAM_STEP)
    v_hat = v / (1.0 - ADAM_B2 ** ADAM_STEP)
    delta = -ADAM_LR * (m_hat / (jnp.sqrt(v_hat) + ADAM_EPS) + ADAM_WD * w)
    return delta, m, v


def adam_big(name, w, m, v, parts, chip):
    n_layers, rows, cols = w.shape
    tr = _tile(rows, 128 if cols > 1024 else 256)
    nrt = rows // tr

    def body(chip_ref, *refs):
        w_ref, m_ref, v_ref = refs[:3]
        part_refs = refs[3:3 + 4 * n_layers]
        g_ref, d_ref, mo_ref, vo_ref = refs[3 + 4 * n_layers:]
        layer = pl.program_id(0)
        for l in range(n_layers):
            @pl.when(layer == l)
            def _(l=l):
                g = part_refs[4 * l][...].astype(F32)
                for p in range(1, 4):
                    g = g + part_refs[4 * l + p][...].astype(F32)
                delta, mn, vn = _adamw(w_ref[...], g, m_ref[...], v_ref[...])
                g_ref[...] = g
                d_ref[...] = delta
                mo_ref[...] = mn
                vo_ref[...] = vn

    wspec = pl.BlockSpec((None, tr, cols), lambda l, i, chip_ref: (l, i, 0))
    operands, in_specs = [w, m, v], [wspec, wspec, wspec]
    for l in range(n_layers):
        mine, recv = parts[l]
        operands.append(mine)
        in_specs.append(pl.BlockSpec(
            (None, tr, cols), lambda ll, i, chip_ref, l=l: (chip_ref[0], jnp.where(ll == l, i, 0), 0)))
        for p in range(3):
            operands.append(recv)
            in_specs.append(pl.BlockSpec(
                (None, tr, cols), lambda ll, i, chip_ref, l=l, p=p: (p, jnp.where(ll == l, i, 0), 0)))
    return pl.pallas_call(
        body, name=name,
        grid_spec=pltpu.PrefetchScalarGridSpec(
            num_scalar_prefetch=1, grid=(n_layers, nrt), in_specs=in_specs, out_specs=[wspec] * 4),
        out_shape=[jax.ShapeDtypeStruct(w.shape, F32)] * 4, compiler_params=_params(2),
    )(chip, *operands)


def adam_small(w, m, v, g):
    rows = w.shape[0]
    tr = _tile(rows, PACK_ROWS)

    def body(w_ref, m_ref, v_ref, g_ref, d_ref, mo_ref, vo_ref):
        delta, mn, vn = _adamw(w_ref[...], g_ref[...], m_ref[...], v_ref[...])
        d_ref[...] = delta
        mo_ref[...] = mn
        vo_ref[...] = vn

    spec = _row_spec(tr, LANE)
    return pl.pallas_call(
        body, name="adam_small", grid=(rows // tr,), in_specs=[spec] * 4, out_specs=[spec] * 3,
        out_shape=[jax.ShapeDtypeStruct(w.shape, F32)] * 3, compiler_params=_params(1),
    )(w, m, v, g)


def sum_parts(parts):
    _, rows, _ = parts.shape
    tr = _tile(rows, PACK_ROWS)

    def body(p_ref, o_ref):
        acc = p_ref[0]
        for k in range(1, N_DEV):
            acc = acc + p_ref[k]
        o_ref[...] = acc

    return pl.pallas_call(
        body, name="sum_parts", grid=(rows // tr,),
        in_specs=[pl.BlockSpec((N_DEV, tr, LANE), lambda i: (0, i, 0))], out_specs=_row_spec(tr, LANE),
        out_shape=jax.ShapeDtypeStruct((rows, LANE), F32), compiler_params=_params(1),
    )(parts)


def pair_add(g, recv, core):
    n, rows, cols = recv.shape
    tr = _tile(rows, 256)

    def body(core_ref, a_ref, b_ref, o_ref):
        o_ref[...] = (a_ref[...].astype(F32) + b_ref[...].astype(F32)).astype(BF16)

    spec = pl.BlockSpec((None, tr, cols), lambda k, i, core_ref: (k, i, 0))
    return pl.pallas_call(
        body, name="pair_add",
        grid_spec=pltpu.PrefetchScalarGridSpec(
            num_scalar_prefetch=1, grid=(n, rows // tr),
            in_specs=[pl.BlockSpec((None, tr, cols), lambda k, i, core_ref: (2 * k + core_ref[0], i, 0)), spec],
            out_specs=spec),
        out_shape=jax.ShapeDtypeStruct(recv.shape, BF16), compiler_params=_params(2),
    )(core, g, recv)


def place_own(x, me, dtype):
    rows, cols = x.shape
    tr = _tile(rows, 256)

    def body(me_ref, x_ref, o_ref):
        o_ref[...] = x_ref[...].astype(dtype)

    return pl.pallas_call(
        body, name="place_own",
        grid_spec=pltpu.PrefetchScalarGridSpec(
            num_scalar_prefetch=1, grid=(rows // tr,),
            in_specs=[pl.BlockSpec((tr, cols), lambda i, me_ref: (i, 0))],
            out_specs=pl.BlockSpec((None, tr, cols), lambda i, me_ref: (me_ref[0], i, 0))),
        out_shape=jax.ShapeDtypeStruct((N_DEV, rows, cols), dtype), compiler_params=_params(1),
    )(me, x)


_HBM = pl.BlockSpec(memory_space=pltpu.HBM)


def _place():
    return lax.axis_index("x"), lax.axis_index("y"), lax.axis_index("c")


def _other_chips(x, y):
    return [(1 - x, y), (x, 1 - y), (1 - x, 1 - y)]


def all_gather(name, shard, me, dtype=None):
    def body(buf_ref, out_ref, send_sems, recv_sems):
        del buf_ref
        x, y, c = _place()
        mine, sibling = (x, y, c), (x, y, 1 - c)
        chips = _other_chips(x, y)

        def copy(k, block, to):
            slot = out_ref.at[4 * block[0] + 2 * block[1] + block[2]]
            return pltpu.make_async_remote_copy(
                src_ref=slot, dst_ref=slot, send_sem=send_sems.at[k], recv_sem=recv_sems.at[k],
                device_id=to, device_id_type=pl.DeviceIdType.MESH)

        first = [copy(0, mine, sibling)] + [copy(1 + j, mine, (*chip, c)) for j, chip in enumerate(chips)]
        for cp in first:
            cp.start()
        passed = [copy(4 + j, (*chip, c), sibling) for j, chip in enumerate(chips)]
        for j, chip in enumerate(chips):
            copy(1 + j, (*chip, c), mine).wait_recv()
            passed[j].start()
        copy(0, sibling, mine).wait_recv()
        for j, chip in enumerate(chips):
            copy(4 + j, (*chip, 1 - c), mine).wait_recv()
        for cp in first + passed:
            cp.wait_send()

    buf = place_own(shard, me, dtype or shard.dtype)
    return pl.pallas_call(
        body, name=name, out_shape=jax.ShapeDtypeStruct(buf.shape, buf.dtype),
        in_specs=[_HBM], out_specs=_HBM, input_output_aliases={0: 0},
        scratch_shapes=[pltpu.SemaphoreType.DMA((7,)), pltpu.SemaphoreType.DMA((7,))],
    )(buf)


def _own_block_copies(src_refs, dst_refs, send_sems, recv_sems):
    x, y, c = _place()
    peers = [(x, y, 1 - c)] + [(*chip, c) for chip in _other_chips(x, y)]
    copies = []
    for b, (src, dst) in enumerate(zip(src_refs, dst_refs)):
        for k, peer in enumerate(peers):
            copies.append((
                pltpu.make_async_remote_copy(
                    src_ref=src.at[4 * x + 2 * y + c], dst_ref=dst.at[4 * x + 2 * y + c],
                    send_sem=send_sems.at[4 * b + k], recv_sem=recv_sems.at[4 * b + k],
                    device_id=peer, device_id_type=pl.DeviceIdType.MESH),
                pltpu.make_async_remote_copy(
                    src_ref=src.at[4 * x + 2 * y + c], dst_ref=dst.at[4 * peer[0] + 2 * peer[1] + peer[2]],
                    send_sem=send_sems.at[4 * b + k], recv_sem=recv_sems.at[4 * b + k],
                    device_id=peer, device_id_type=pl.DeviceIdType.MESH)))
    return copies


def gather_start(name, bufs, after):
    n = len(bufs)

    def body(*refs):
        send_sems, recv_sems = refs[n + 1], refs[n + 2]
        thru = refs[n + 3:2 * n + 3]
        for send, _ in _own_block_copies(thru, thru, send_sems, recv_sems):
            send.start()
        refs[2 * n + 3][...] = jnp.zeros((SUBLANE, LANE), F32)

    return pl.pallas_call(
        body, name=name,
        out_shape=(pltpu.SemaphoreType.DMA((4 * n,)), pltpu.SemaphoreType.DMA((4 * n,)),
                   *[pltpu.HBM(b.shape, b.dtype) for b in bufs], jax.ShapeDtypeStruct((SUBLANE, LANE), F32)),
        in_specs=(*(_HBM,) * n, _ANY), out_specs=(_SEM, _SEM, *(_HBM,) * n, _TOKEN),
        input_output_aliases={b: 2 + b for b in range(n)},
        compiler_params=pltpu.CompilerParams(has_side_effects=_EFFECT),
    )(*[_hbm(b) for b in bufs], after)


def gather_wait(name, state, after):
    send_sems, recv_sems, *bufs = state[:-1]
    n = len(bufs)

    def body(*refs):
        ins = refs[:n]
        send_sems, recv_sems = refs[n], refs[n + 1]
        for send, arrival in _own_block_copies(ins, ins, send_sems, recv_sems):
            send.wait_send()
            arrival.wait_recv()

    return pl.pallas_call(
        body, name=name, out_shape=tuple(pltpu.HBM(b.shape, b.dtype) for b in bufs),
        in_specs=(*(_HBM,) * n, _SEM, _SEM, _ANY), out_specs=(_HBM,) * n,
        input_output_aliases={b: b for b in range(n)},
        compiler_params=pltpu.CompilerParams(has_side_effects=_EFFECT),
    )(*bufs, send_sems, recv_sems, after)


def gather_finish(name, bufs):
    n = len(bufs)

    def body(*refs):
        outs = refs[n:2 * n]
        send_sems, recv_sems = refs[2 * n], refs[2 * n + 1]
        x, y, c = _place()
        copies = []
        for b, out in enumerate(outs):
            for k, chip in enumerate(_other_chips(x, y)):
                sem = 3 * b + k
                copies.append((
                    pltpu.make_async_remote_copy(
                        src_ref=out.at[4 * chip[0] + 2 * chip[1] + c], dst_ref=out.at[4 * chip[0] + 2 * chip[1] + c],
                        send_sem=send_sems.at[sem], recv_sem=recv_sems.at[sem],
                        device_id=(x, y, 1 - c), device_id_type=pl.DeviceIdType.MESH),
                    pltpu.make_async_remote_copy(
                        src_ref=out.at[4 * chip[0] + 2 * chip[1] + c], dst_ref=out.at[4 * chip[0] + 2 * chip[1] + 1 - c],
                        send_sem=send_sems.at[sem], recv_sem=recv_sems.at[sem],
                        device_id=(x, y, 1 - c), device_id_type=pl.DeviceIdType.MESH)))
        for send, _ in copies:
            send.start()
        for send, arrival in copies:
            send.wait_send()
            arrival.wait_recv()

    return pl.pallas_call(
        body, name=name, out_shape=tuple(jax.ShapeDtypeStruct(b.shape, b.dtype) for b in bufs),
        in_specs=(_HBM,) * n, out_specs=(_HBM,) * n, input_output_aliases={b: b for b in range(n)},
        scratch_shapes=[pltpu.SemaphoreType.DMA((3 * n,)), pltpu.SemaphoreType.DMA((3 * n,))],
    )(*bufs)


def scatter_pair(name, g):
    _, rows, cols = g.shape

    def body(g_ref, recv_ref, send_sems, recv_sems):
        x, y, c = _place()
        remote = [pltpu.make_async_remote_copy(
            src_ref=g_ref.at[2 * k + 1 - c], dst_ref=recv_ref.at[k], send_sem=send_sems.at[k], recv_sem=recv_sems.at[k],
            device_id=(x, y, 1 - c), device_id_type=pl.DeviceIdType.MESH) for k in range(4)]
        for cp in remote:
            cp.start()
        for cp in remote:
            cp.wait()

    return pl.pallas_call(
        body, name=name, out_shape=jax.ShapeDtypeStruct((4, rows, cols), g.dtype), in_specs=[_HBM], out_specs=_HBM,
        scratch_shapes=[pltpu.SemaphoreType.DMA((4,)), pltpu.SemaphoreType.DMA((4,))],
    )(g)


def scatter_chips(name, p):
    _, rows, cols = p.shape

    def body(p_ref, recv_ref, send_sems, recv_sems):
        x, y, c = _place()
        remote = [pltpu.make_async_remote_copy(
            src_ref=p_ref.at[2 * px + py], dst_ref=recv_ref.at[k], send_sem=send_sems.at[k], recv_sem=recv_sems.at[k],
            device_id=(px, py, c), device_id_type=pl.DeviceIdType.MESH) for k, (px, py) in enumerate(_other_chips(x, y))]
        for cp in remote:
            cp.start()
        for cp in remote:
            cp.wait()

    return pl.pallas_call(
        body, name=name, out_shape=jax.ShapeDtypeStruct((3, rows, cols), p.dtype), in_specs=[_HBM], out_specs=_HBM,
        scratch_shapes=[pltpu.SemaphoreType.DMA((3,)), pltpu.SemaphoreType.DMA((3,))],
    )(p)


_SEM = pl.BlockSpec(memory_space=pltpu.SEMAPHORE)
_ANY = pl.BlockSpec(memory_space=pl.ANY)
_TOKEN = pl.BlockSpec(memory_space=pltpu.VMEM)
_EFFECT = pltpu.SideEffectType.DATAFLOW_SIDE_EFFECTING


def _hbm(a):
    return pltpu.with_memory_space_constraint(a, pltpu.HBM)


def _chip_copies(p_ref, land_ref, send_sems, recv_sems):
    x, y, c = _place()
    return [pltpu.make_async_remote_copy(
        src_ref=p_ref.at[2 * px + py], dst_ref=land_ref.at[k], send_sem=send_sems.at[k], recv_sem=recv_sems.at[k],
        device_id=(px, py, c), device_id_type=pl.DeviceIdType.MESH) for k, (px, py) in enumerate(_other_chips(x, y))]


def scatter_chips_start(name, p):
    _, rows, cols = p.shape

    def body(p_ref, land_ref, send_sems, recv_sems, p_thru, land_thru, token):
        for cp in _chip_copies(p_ref, land_ref, send_sems, recv_sems):
            cp.start()
        token[...] = jnp.zeros_like(token)

    return pl.pallas_call(
        body, name=name,
        out_shape=(pltpu.SemaphoreType.DMA((3,)), pltpu.SemaphoreType.DMA((3,)), pltpu.HBM(p.shape, p.dtype),
                   pltpu.HBM((3, rows, cols), p.dtype), jax.ShapeDtypeStruct((SUBLANE, LANE), F32)),
        in_specs=(_HBM, _HBM), out_specs=(_SEM, _SEM, _HBM, _HBM, _TOKEN), input_output_aliases={0: 2, 1: 3},
        compiler_params=pltpu.CompilerParams(has_side_effects=_EFFECT),
    )(_hbm(p), _hbm(lax.empty((3, rows, cols), p.dtype)))


def scatter_chips_wait(name, send_sems, recv_sems, p_thru, land_thru, after):
    def body(p_ref, land_ref, send_sems, recv_sems, after_ref, p_out, land_out):
        for cp in _chip_copies(p_ref, land_ref, send_sems, recv_sems):
            cp.wait_send()
            cp.wait_recv()

    return pl.pallas_call(
        body, name=name,
        out_shape=(pltpu.HBM(p_thru.shape, p_thru.dtype), pltpu.HBM(land_thru.shape, land_thru.dtype)),
        in_specs=(_HBM, _HBM, _SEM, _SEM, _ANY), out_specs=(_HBM, _HBM), input_output_aliases={0: 0, 1: 1},
        compiler_params=pltpu.CompilerParams(has_side_effects=_EFFECT),
    )(p_thru, land_thru, send_sems, recv_sems, after)


def reduce_scatter_start(tag, g, core):
    p = pair_add(g, scatter_pair("rs_pair_" + tag, g), core)
    return scatter_chips_start("rs_start_" + tag, p)


def reduce_scatter_wait(tag, state, after):
    send_sems, recv_sems, p_thru, land_thru, _ = state
    return scatter_chips_wait("rs_wait_" + tag, send_sems, recv_sems, p_thru, land_thru, after)


_SMALL = ("g_pre_mix", "g_post_mix", "g_pre_ffn", "g_post_ffn", "g_attn_grp", "g_conv_grp", "g_lru_grp",
          "dw_conv_w", "dw_conv_b", "conv_ln_g", "conv_ln_b", "lru_conv_w", "lru_conv_b",
          "lru_w_a", "lru_b_a", "lru_w_i", "lru_b_i", "lru_lambda")
_COL_SHARDED_SMALL = ("dw_conv_w", "lru_conv_w")
_BIG = ("w_in", "w_out", "w_gate", "w_up", "w_down")
_ALL = ("w_in", "w_out", "g_pre_mix", "g_post_mix", "g_pre_ffn", "g_post_ffn", "g_attn_grp", "g_conv_grp", "g_lru_grp",
        "dw_conv_w", "dw_conv_b", "conv_ln_g", "conv_ln_b", "lru_conv_w", "lru_conv_b", "lru_w_a", "lru_b_a",
        "lru_w_i", "lru_b_i", "lru_lambda", "w_gate", "w_up", "w_down")


def _pack(arrays):
    flat = jnp.concatenate([a.reshape(-1) for a in arrays])
    pad = (-flat.shape[0]) % (PACK_ROWS * LANE)
    return jnp.pad(flat, (0, pad)).reshape(-1, LANE)


def _unpack(packed, shapes):
    flat = packed.reshape(-1)
    out, pos = [], 0
    for s in shapes:
        n = math.prod(s)
        out.append(flat[pos:pos + n].reshape(s))
        pos += n
    return out


def kernel(x, w_in, w_out, g_pre_mix, g_post_mix, g_pre_ffn, g_post_ffn, g_attn_grp, g_conv_grp, g_lru_grp, dw_conv_w, dw_conv_b, conv_ln_g, conv_ln_b, lru_conv_w, lru_conv_b, lru_w_a, lru_b_a, lru_w_i, lru_b_i, lru_lambda, w_gate, w_up, w_down, loss_target, m_w_in, m_w_out, m_g_pre_mix, m_g_post_mix, m_g_pre_ffn, m_g_post_ffn, m_g_attn_grp, m_g_conv_grp, m_g_lru_grp, m_dw_conv_w, m_dw_conv_b, m_conv_ln_g, m_conv_ln_b, m_lru_conv_w, m_lru_conv_b, m_lru_w_a, m_lru_b_a, m_lru_w_i, m_lru_b_i, m_lru_lambda, m_w_gate, m_w_up, m_w_down, v_w_in, v_w_out, v_g_pre_mix, v_g_post_mix, v_g_pre_ffn, v_g_post_ffn, v_g_attn_grp, v_g_conv_grp, v_g_lru_grp, v_dw_conv_w, v_dw_conv_b, v_conv_ln_g, v_conv_ln_b, v_lru_conv_w, v_lru_conv_b, v_lru_w_a, v_lru_b_a, v_lru_w_i, v_lru_b_i, v_lru_lambda, v_w_gate, v_w_up, v_w_down):
    env = dict(locals())
    wts = {n: env[n] for n in _ALL}
    mom = {n: env["m_" + n] for n in _ALL}
    var = {n: env["v_" + n] for n in _ALL}

    depth = w_in.shape[0]
    h = x[0]
    target = loss_target[0]
    t, d = h.shape
    attn_w = d // 2
    n_heads = attn_w // HEAD_DIM
    cc = d // 4
    wl = d // 4
    conv_col, lru_col = 3 * attn_w, 3 * attn_w + 2 * cc
    me = 4 * lax.axis_index("x") + 2 * lax.axis_index("y") + lax.axis_index("c")
    me_s = me.astype(jnp.int32).reshape(1)
    chip_s = (2 * lax.axis_index("x") + lax.axis_index("y")).astype(jnp.int32).reshape(1)
    core_s = lax.axis_index("c").astype(jnp.int32).reshape(1)

    n_taps = DW_LEN + LRU_LEN
    taps = jnp.concatenate([dw_conv_w, lru_conv_w], axis=1).reshape(depth * n_taps, cc // N_DEV)
    taps = all_gather("ag_taps", taps, me_s)
    taps = jnp.moveaxis(taps.reshape(N_DEV, depth, n_taps, cc // N_DEV), 0, 2).reshape(depth, n_taps, cc)
    dw_full, lcw_full = taps[:, :DW_LEN], taps[:, DW_LEN:]

    def vec(a, l):
        return a[l].reshape(1, -1)

    ag_state, started = [], me_s
    for l in range(depth):
        ag_state.append(gather_start(f"ag_start_{l}", [place_own(wts[n][l], me_s, BF16) for n in _BIG], started))
        started = ag_state[l][-1]
    started = started[0:1, 0:1]

    saved = []
    u1 = rms_pre(h, vec(g_pre_mix, 0) + started)
    loss_sum = dh = dbr = None
    for l in range(depth):
        bufs = gather_finish("ag_finish", gather_wait(f"ag_wait_{l}", ag_state[l], u1))
        wg = dict(zip(_BIG, bufs))
        wg["w_out"] = wg["w_out"].reshape(attn_w + cc + wl, d)
        wa_b, wi_b = lru_w_a[l].astype(BF16), lru_w_i[l].astype(BF16)
        proj = mm_proj(u1, wg["w_in"])
        y_attn = attn_fwd(proj, n_heads)
        cpre, y_conv = conv_fwd(proj, conv_col, cc, dw_full[l], vec(dw_conv_b, l), vec(conv_ln_g, l), vec(conv_ln_b, l))
        hs, y_lru = lru_fwd(proj, lru_col, wl, lcw_full[l], vec(lru_conv_b, l), wa_b, vec(lru_b_a, l), wi_b,
                            vec(lru_b_i, l), vec(lru_lambda, l))
        mixed = mix_fwd(y_attn, y_conv, y_lru, vec(g_attn_grp, l), vec(g_conv_grp, l), vec(g_lru_grp, l))
        o = mm_plain("mm_out", mixed, wg["w_out"], "nn", F32)
        h2, u2 = res_norm(h, o, vec(g_post_mix, l), vec(g_pre_ffn, l))
        gt, up, f = ffn_up(u2, wg["w_gate"], wg["w_up"])
        dn = mm_down(f, wg["w_down"])
        saved.append(dict(wg=wg, wa_b=wa_b, wi_b=wi_b, h=h, u1=u1, proj=proj, y_attn=y_attn, cpre=cpre, y_conv=y_conv,
                          hs=hs, y_lru=y_lru, mixed=mixed, o=o, h2=h2, u2=u2, gt=gt, up=up, f=f, dn=dn))
        if l + 1 < depth:
            h, u1 = res_norm(h2, dn, vec(g_post_ffn, l), vec(g_pre_mix, l + 1))
        else:
            loss_sum, dh, dbr, dg_post_ffn = final_loss(h2, dn, vec(g_post_ffn, l), target)

    loss = lax.psum(0.5 * loss_sum[0, 0] / d, MESH_AXES)

    small = {n: [None] * depth for n in _SMALL}
    rs_state = {n: [None] * depth for n in _BIG}
    for l in reversed(range(depth)):
        s = saved[l]
        wg = s["wg"]
        small["g_post_ffn"][l] = dg_post_ffn
        dgt, dup = ffn_bwd(dbr, wg["w_down"], s["gt"], s["up"])
        rs_state["w_down"][l] = reduce_scatter_start(f"down_{l}", mm_dw_rows("mm_dw_down", s["f"], dbr), core_s)
        rs_state["w_gate"][l] = reduce_scatter_start(
            f"gate_{l}", mm_dw_cols_stacked("mm_dw_gate", s["u2"], dgt), core_s)
        rs_state["w_up"][l] = reduce_scatter_start(f"up_{l}", mm_dw_cols_stacked("mm_dw_gate", s["u2"], dup), core_s)
        du2 = mm_dx_ffn(dgt, wg["w_gate"], dup, wg["w_up"])
        started = sum(rs_state[n][l][-1][0:1, 0:1] for n in ("w_down", "w_gate", "w_up"))
        dh2, small["g_pre_ffn"][l], do, small["g_post_mix"][l] = norm_bwd(
            dh, du2, s["h2"], vec(g_pre_ffn, l) + started, (s["o"], vec(g_post_mix, l)))
        dmixed = mm_plain("mm_dmixed", do, wg["w_out"], "nt", F32)
        dw_out = mm_plain("mm_dw_out", s["mixed"], do, "tn", BF16)
        rs_state["w_out"][l] = reduce_scatter_start(f"out_{l}", dw_out.reshape(N_DEV, -1, d), core_s)
        (dya, dc, dyl, small["g_attn_grp"][l], small["g_conv_grp"][l], small["g_lru_grp"][l],
         small["conv_ln_g"][l], small["conv_ln_b"][l]) = mix_bwd(
            dmixed, s["y_attn"], s["y_conv"], s["y_lru"], s["cpre"],
            vec(g_attn_grp, l) + rs_state["w_out"][l][-1][0:1, 0:1], vec(g_conv_grp, l),
            vec(g_lru_grp, l), vec(conv_ln_g, l), vec(conv_ln_b, l))
        dq, dk, dv = attn_bwd(s["proj"], dya, n_heads)
        dvg, small["dw_conv_w"][l], small["dw_conv_b"][l] = conv_bwd(s["proj"], conv_col, cc, dc, dw_full[l])
        (dxy, small["lru_conv_w"][l], small["lru_conv_b"][l], small["lru_w_a"][l], small["lru_b_a"][l],
         small["lru_w_i"][l], small["lru_b_i"][l], small["lru_lambda"][l]) = lru_bwd(
            s["proj"], lru_col, wl, s["hs"], dyl, lcw_full[l], vec(lru_conv_b, l), s["wa_b"], vec(lru_b_a, l),
            s["wi_b"], vec(lru_b_i, l), vec(lru_lambda, l))
        dproj = jnp.concatenate([dq, dk, dv, dvg, dxy], axis=1)
        rs_state["w_in"][l] = reduce_scatter_start(f"in_{l}", mm_dw_cols("mm_dw_in", s["u1"], dproj, N_DEV), core_s)
        du1 = mm_dx_cols("mm_dx_in", dproj, wg["w_in"])
        g_pre = vec(g_pre_mix, l) + rs_state["w_in"][l][-1][0:1, 0:1]
        if l > 0:
            p = saved[l - 1]
            dh, small["g_pre_mix"][l], dbr, dg_post_ffn = norm_bwd(
                dh2, du1, s["h"], g_pre, (p["dn"], vec(g_post_ffn, l - 1)))
        else:
            dh, small["g_pre_mix"][l] = norm_bwd(dh2, du1, s["h"], g_pre)
    grad_x = dh[None]
    big_parts = {n: [reduce_scatter_wait(f"{n[2:]}_{l}", rs_state[n][l], dh) for l in range(depth)] for n in _BIG}

    small_shapes = [(depth,) + tuple(wts[n].shape[1:]) if n not in _COL_SHARDED_SMALL
                    else (depth, wts[n].shape[1], cc) for n in _SMALL]
    part = _pack([jnp.stack([a.reshape(shp[1:]) for a in small[n]]) for n, shp in zip(_SMALL, small_shapes)])
    g_small = _unpack(sum_parts(all_gather("ag_small", part, me_s)), small_shapes)
    grads = {}
    for n, g in zip(_SMALL, g_small):
        if n in _COL_SHARDED_SMALL:
            g = lax.dynamic_slice_in_dim(g, me * (cc // N_DEV), cc // N_DEV, axis=2)
        grads[n] = g
    local_shapes = [tuple(wts[n].shape) for n in _SMALL]
    d_small, m_small, v_small = adam_small(
        _pack([wts[n] for n in _SMALL]), _pack([mom[n] for n in _SMALL]), _pack([var[n] for n in _SMALL]),
        _pack([grads[n] for n in _SMALL]))
    delta = dict(zip(_SMALL, _unpack(d_small, local_shapes)))
    new_m = dict(zip(_SMALL, _unpack(m_small, local_shapes)))
    new_v = dict(zip(_SMALL, _unpack(v_small, local_shapes)))

    for n in _BIG:
        shape = wts[n].shape
        _, rows, cols = big_parts[n][0][0].shape
        view = (depth, rows, cols)
        g, dl, mn, vn = adam_big("adam_" + n, wts[n].reshape(view), mom[n].reshape(view), var[n].reshape(view),
                                 big_parts[n], chip_s)
        grads[n], delta[n], new_m[n], new_v[n] = (a.reshape(shape) for a in (g, dl, mn, vn))

    return (loss, grad_x, *[grads[n] for n in _ALL], *[delta[n] for n in _ALL],
            *[new_m[n] for n in _ALL], *[new_v[n] for n in _ALL])
```

```python
import functools
import math

import jax
import jax.numpy as jnp
from jax import lax
from jax.experimental import pallas as pl
from jax.experimental.pallas import tpu as pltpu

F32 = jnp.float32
BF16 = jnp.bfloat16

N_DEV = 8
EPS = 1e-6
HEAD_DIM = 128
DW_LEN = 31
LRU_LEN = 4
LRU_BLOCKS = 4
LRU_C = 8.0
ATT_T = 256
ROW_T = 256
CONV_HALO = 32
LRU_HALO = 8
LANE = 128
SUBLANE = 8
PACK_ROWS = 512
VMEM_LIMIT = 56 * 1024 * 1024

ADAM_LR = 0.001
ADAM_B1 = 0.9
ADAM_B2 = 0.999
ADAM_EPS = 1e-08
ADAM_WD = 0.01
ADAM_STEP = 10

MESH_AXES = ("x", "y", "c")
_DIMS = {
    "nn": (((1,), (0,)), ((), ())),
    "nt": (((1,), (1,)), ((), ())),
    "tn": (((0,), (0,)), ((), ())),
}


def _params(n_axes):
    return pltpu.CompilerParams(
        dimension_semantics=("arbitrary",) * n_axes, vmem_limit_bytes=VMEM_LIMIT)


def _dot(a, b, mode="nn"):
    return lax.dot_general(a, b, _DIMS[mode], preferred_element_type=F32)


def _sigmoid(x):
    return 1.0 / (1.0 + jnp.exp(-x))


def _softplus(x):
    return jnp.maximum(x, 0.0) + jnp.log(1.0 + jnp.exp(-jnp.abs(x)))


def _neg_expm1(x):
    series = x * (1.0 + x * (0.5 + x * (1.0 / 6 + x * (1.0 / 24 + x * (1.0 / 120 + x * (1.0 / 720))))))
    return jnp.where(x > -0.25, -series, 1.0 - jnp.exp(x))


_GELU_C = math.sqrt(2.0 / math.pi)


def _gelu_and_grad(x):
    inner = _GELU_C * (x + 0.044715 * x * x * x)
    t = jnp.tanh(inner)
    val = 0.5 * x * (1.0 + t)
    grad = 0.5 * (1.0 + t) + 0.5 * x * (1.0 - t * t) * _GELU_C * (1.0 + 3 * 0.044715 * x * x)
    return val, grad


def _rms_stats(x):
    r = lax.rsqrt(jnp.mean(x * x, axis=-1, keepdims=True) + EPS)
    return x * r, r


def _rms_bwd(dy, x, g):
    xn, r = _rms_stats(x)
    dxn = dy * g
    dx = r * (dxn - xn * jnp.mean(dxn * xn, axis=-1, keepdims=True))
    return dx, jnp.sum(dy * xn, axis=0, keepdims=True)


def _row_spec(tr, width, col=0):
    return pl.BlockSpec((tr, width), lambda i, col=col: (i, col))


def _vec_spec(width):
    return pl.BlockSpec((1, width), lambda i: (0, 0))


def _matmul(name, mode, operands, in_specs, out_shape, out_spec, grid, out_block):
    npairs = len(operands) // 2
    nk = grid[2]

    def body(*refs):
        o_ref = refs[2 * npairs]

        def partial():
            acc = None
            for p in range(npairs):
                d = _dot(refs[2 * p][...], refs[2 * p + 1][...], mode)
                acc = d if acc is None else acc + d
            return acc

        if nk == 1:
            o_ref[...] = partial().astype(o_ref.dtype)
        else:
            acc_ref = refs[2 * npairs + 1]
            k = pl.program_id(2)

            @pl.when(k == 0)
            def _():
                acc_ref[...] = jnp.zeros_like(acc_ref)

            acc_ref[...] += partial()

            @pl.when(k == nk - 1)
            def _():
                o_ref[...] = acc_ref[...].astype(o_ref.dtype)

    return pl.pallas_call(
        body, name=name, grid=grid, in_specs=in_specs, out_specs=out_spec, out_shape=out_shape,
        scratch_shapes=[] if nk == 1 else [pltpu.VMEM(out_block, F32)],
        compiler_params=_params(3),
    )(*operands)


def _tile(n, t):
    if n <= t:
        return n
    return max(k for k in range(SUBLANE, t + 1, SUBLANE) if n % k == 0)


def mm_proj(u, w):
    t, d = u.shape
    nblk, _, nb = w.shape
    tm = _tile(t, 1024)
    return _matmul(
        "mm_proj", "nn", (u, w),
        [pl.BlockSpec((tm, d), lambda j, i, k: (i, 0)), pl.BlockSpec((None, d, nb), lambda j, i, k: (j, 0, 0))],
        jax.ShapeDtypeStruct((t, nblk * nb), F32), pl.BlockSpec((tm, nb), lambda j, i, k: (i, j)),
        (nblk, t // tm, 1), (tm, nb))


def mm_plain(name, a, b, mode, out_dtype):
    if mode == "nn":
        (m, kk), n = a.shape, b.shape[1]
    elif mode == "nt":
        (m, kk), n = a.shape, b.shape[0]
    else:
        (kk, m), n = a.shape, b.shape[1]
    tm, tn = _tile(m, 1024), _tile(n, 1024)
    a_spec = (pl.BlockSpec((kk, tm), lambda i, j, k: (0, i)) if mode == "tn"
              else pl.BlockSpec((tm, kk), lambda i, j, k: (i, 0)))
    b_spec = (pl.BlockSpec((tn, kk), lambda i, j, k: (j, 0)) if mode == "nt"
              else pl.BlockSpec((kk, tn), lambda i, j, k: (0, j)))
    return _matmul(
        name, mode, (a, b), [a_spec, b_spec],
        jax.ShapeDtypeStruct((m, n), out_dtype), pl.BlockSpec((tm, tn), lambda i, j, k: (i, j)),
        (m // tm, n // tn, 1), (tm, tn))


def mm_down(f, w):
    nblk, t, fb = f.shape
    d = w.shape[2]
    tm, tn = _tile(t, 1024), _tile(d, 1024)
    return _matmul(
        "mm_down", "nn", (f, w),
        [pl.BlockSpec((None, tm, fb), lambda i, j, k: (k, i, 0)), pl.BlockSpec((None, fb, tn), lambda i, j, k: (k, 0, j))],
        jax.ShapeDtypeStruct((t, d), F32), pl.BlockSpec((tm, tn), lambda i, j, k: (i, j)),
        (t // tm, d // tn, nblk), (tm, tn))


def mm_dw_rows(name, a, g):
    nblk, t, fb = a.shape
    d = g.shape[1]
    tn = _tile(d, 1024)
    return _matmul(
        name, "tn", (a, g),
        [pl.BlockSpec((None, t, fb), lambda j, n, k: (j, 0, 0)), pl.BlockSpec((t, tn), lambda j, n, k: (0, n))],
        jax.ShapeDtypeStruct((nblk, fb, d), BF16), pl.BlockSpec((None, fb, tn), lambda j, n, k: (j, 0, n)),
        (nblk, d // tn, 1), (fb, tn))


def mm_dw_cols_stacked(name, u, g):
    t, d = u.shape
    nblk, _, fb = g.shape
    tmd = _tile(d, 1024)
    return _matmul(
        name, "tn", (u, g),
        [pl.BlockSpec((t, tmd), lambda j, i, k: (0, i)), pl.BlockSpec((None, t, fb), lambda j, i, k: (j, 0, 0))],
        jax.ShapeDtypeStruct((nblk, d, fb), BF16), pl.BlockSpec((None, tmd, fb), lambda j, i, k: (j, i, 0)),
        (nblk, d // tmd, 1), (tmd, fb))


def mm_dw_cols(name, u, g, nblk):
    t, d = u.shape
    nb = g.shape[1] // nblk
    tmd = _tile(d, 1024)
    return _matmul(
        name, "tn", (u, g),
        [pl.BlockSpec((t, tmd), lambda j, i, k: (0, i)), pl.BlockSpec((t, nb), lambda j, i, k: (0, j))],
        jax.ShapeDtypeStruct((nblk, d, nb), BF16), pl.BlockSpec((None, tmd, nb), lambda j, i, k: (j, i, 0)),
        (nblk, d // tmd, 1), (tmd, nb))


def mm_dx_cols(name, g, w):
    t = g.shape[0]
    nblk, d, nb = w.shape
    tm, tn = _tile(t, 1024), _tile(d, 1024)
    return _matmul(
        name, "nt", (g, w),
        [pl.BlockSpec((tm, nb), lambda i, j, k: (i, k)), pl.BlockSpec((None, tn, nb), lambda i, j, k: (k, j, 0))],
        jax.ShapeDtypeStruct((t, d), F32), pl.BlockSpec((tm, tn), lambda i, j, k: (i, j)),
        (t // tm, d // tn, nblk), (tm, tn))


def mm_dx_ffn(dgt, wg, dup, wu):
    nblk, t, fb = dgt.shape
    d = wg.shape[1]
    tm, tn = _tile(t, 1024), _tile(d, 1024)
    a_spec = pl.BlockSpec((None, tm, fb), lambda i, j, k: (k, i, 0))
    b_spec = pl.BlockSpec((None, tn, fb), lambda i, j, k: (k, j, 0))
    return _matmul(
        "mm_dx_ffn", "nt", (dgt, wg, dup, wu), [a_spec, b_spec, a_spec, b_spec],
        jax.ShapeDtypeStruct((t, d), F32), pl.BlockSpec((tm, tn), lambda i, j, k: (i, j)),
        (t // tm, d // tn, nblk), (tm, tn))


def ffn_up(u, wg, wu):
    t, d = u.shape
    nblk, _, fb = wg.shape
    tm = _tile(t, 512)

    def body(u_ref, wg_ref, wu_ref, gt_ref, up_ref, f_ref):
        uu = u_ref[...]
        gt = _dot(uu, wg_ref[...])
        up = _dot(uu, wu_ref[...])
        gt_ref[...] = gt
        up_ref[...] = up
        f_ref[...] = (gt * _sigmoid(gt) * up).astype(BF16)

    w_spec = pl.BlockSpec((None, d, fb), lambda j, i: (j, 0, 0))
    o_spec = pl.BlockSpec((None, tm, fb), lambda j, i: (j, i, 0))
    return pl.pallas_call(
        body, name="ffn_up", grid=(nblk, t // tm),
        in_specs=[pl.BlockSpec((tm, d), lambda j, i: (i, 0)), w_spec, w_spec],
        out_specs=[o_spec, o_spec, o_spec],
        out_shape=[jax.ShapeDtypeStruct((nblk, t, fb), F32), jax.ShapeDtypeStruct((nblk, t, fb), F32),
                   jax.ShapeDtypeStruct((nblk, t, fb), BF16)],
        compiler_params=_params(2),
    )(u, wg, wu)


def ffn_bwd(dd, wd, gt, up):
    t, d = dd.shape
    nblk, fb, _ = wd.shape
    tm = _tile(t, 512)

    def body(dd_ref, wd_ref, gt_ref, up_ref, dgt_ref, dup_ref):
        df = _dot(dd_ref[...], wd_ref[...], "nt")
        g = gt_ref[...]
        s = _sigmoid(g)
        dgt_ref[...] = (df * up_ref[...] * s * (1.0 + g * (1.0 - s))).astype(BF16)
        dup_ref[...] = (df * g * s).astype(BF16)

    s_spec = pl.BlockSpec((None, tm, fb), lambda j, i: (j, i, 0))
    return pl.pallas_call(
        body, name="ffn_bwd", grid=(nblk, t // tm),
        in_specs=[pl.BlockSpec((tm, d), lambda j, i: (i, 0)), pl.BlockSpec((None, fb, d), lambda j, i: (j, 0, 0)),
                  s_spec, s_spec],
        out_specs=[s_spec, s_spec],
        out_shape=[jax.ShapeDtypeStruct((nblk, t, fb), BF16)] * 2,
        compiler_params=_params(2),
    )(dd, wd, gt, up)


def rms_pre(h, g):
    t, d = h.shape
    tr = _tile(t, ROW_T)

    def body(h_ref, g_ref, o_ref):
        xn, _ = _rms_stats(h_ref[...])
        o_ref[...] = (xn * g_ref[...]).astype(BF16)

    return pl.pallas_call(
        body, name="rms_pre", grid=(t // tr,),
        in_specs=[_row_spec(tr, d), _vec_spec(d)], out_specs=_row_spec(tr, d),
        out_shape=jax.ShapeDtypeStruct((t, d), BF16), compiler_params=_params(1),
    )(h, g)


def res_norm(h, o, g_post, g_pre):
    t, d = h.shape
    tr = _tile(t, ROW_T)

    def body(h_ref, o_ref, gpo_ref, gpr_ref, h2_ref, u_ref):
        on, _ = _rms_stats(o_ref[...])
        h2 = h_ref[...] + on * gpo_ref[...]
        h2_ref[...] = h2
        hn, _ = _rms_stats(h2)
        u_ref[...] = (hn * gpr_ref[...]).astype(BF16)

    return pl.pallas_call(
        body, name="res_norm", grid=(t // tr,),
        in_specs=[_row_spec(tr, d), _row_spec(tr, d), _vec_spec(d), _vec_spec(d)],
        out_specs=[_row_spec(tr, d), _row_spec(tr, d)],
        out_shape=[jax.ShapeDtypeStruct((t, d), F32), jax.ShapeDtypeStruct((t, d), BF16)],
        compiler_params=_params(1),
    )(h, o, g_post, g_pre)


def final_loss(h2, dbr, g_post, target):
    t, d = h2.shape
    tr = _tile(t, ROW_T)

    def body(h2_ref, d_ref, g_ref, tg_ref, loss_ref, dy_ref, dd_ref, dg_ref):
        i = pl.program_id(0)

        @pl.when(i == 0)
        def _():
            loss_ref[...] = jnp.zeros_like(loss_ref)
            dg_ref[...] = jnp.zeros_like(dg_ref)

        x = d_ref[...]
        g = g_ref[...]
        xn, _ = _rms_stats(x)
        diff = h2_ref[...] + xn * g - tg_ref[...]
        loss_ref[...] += jnp.sum(jnp.sum(diff * diff, axis=1, keepdims=True), axis=0, keepdims=True)
        dy = diff * (1.0 / d)
        dy_ref[...] = dy
        dx, dg = _rms_bwd(dy, x, g)
        dd_ref[...] = dx.astype(BF16)
        dg_ref[...] += dg

    return pl.pallas_call(
        body, name="final_loss", grid=(t // tr,),
        in_specs=[_row_spec(tr, d), _row_spec(tr, d), _vec_spec(d), _row_spec(tr, d)],
        out_specs=[pl.BlockSpec((1, 1), lambda i: (0, 0)), _row_spec(tr, d), _row_spec(tr, d), _vec_spec(d)],
        out_shape=[jax.ShapeDtypeStruct((1, 1), F32), jax.ShapeDtypeStruct((t, d), F32),
                   jax.ShapeDtypeStruct((t, d), BF16), jax.ShapeDtypeStruct((1, d), F32)],
        compiler_params=_params(1),
    )(h2, dbr, g_post, target)


def norm_bwd(dh_out, du, h_in, g_pre, prev=None):
    t, d = h_in.shape
    tr = _tile(t, ROW_T)
    with_prev = prev is not None

    def body(*refs):
        if with_prev:
            dho_ref, du_ref, h_ref, gpr_ref, br_ref, gpo_ref, dh_ref, dgpr_ref, dbr_ref, dgpo_ref = refs
        else:
            dho_ref, du_ref, h_ref, gpr_ref, dh_ref, dgpr_ref = refs
        i = pl.program_id(0)

        @pl.when(i == 0)
        def _():
            dgpr_ref[...] = jnp.zeros_like(dgpr_ref)
            if with_prev:
                dgpo_ref[...] = jnp.zeros_like(dgpo_ref)

        dx, dg = _rms_bwd(du_ref[...], h_ref[...], gpr_ref[...])
        dh = dho_ref[...] + dx
        dh_ref[...] = dh
        dgpr_ref[...] += dg
        if with_prev:
            dbr, dg2 = _rms_bwd(dh, br_ref[...], gpo_ref[...])
            dbr_ref[...] = dbr.astype(BF16)
            dgpo_ref[...] += dg2

    row, vec = _row_spec(tr, d), _vec_spec(d)
    in_specs = [row, row, row, vec] + ([row, vec] if with_prev else [])
    out_specs = [row, vec] + ([row, vec] if with_prev else [])
    out_shape = [jax.ShapeDtypeStruct((t, d), F32), jax.ShapeDtypeStruct((1, d), F32)]
    if with_prev:
        out_shape += [jax.ShapeDtypeStruct((t, d), BF16), jax.ShapeDtypeStruct((1, d), F32)]
    args = (dh_out, du, h_in, g_pre) + (tuple(prev) if with_prev else ())
    return pl.pallas_call(
        body, name="norm_bwd_chain" if with_prev else "norm_bwd_first", grid=(t // tr,),
        in_specs=in_specs, out_specs=out_specs, out_shape=out_shape, compiler_params=_params(1),
    )(*args)


def mix_fwd(ya, yc, yl, ga, gc, gl):
    t, wa = ya.shape
    wc, wl = yc.shape[1], yl.shape[1]
    tr = _tile(t, ROW_T)

    def body(ya_ref, yc_ref, yl_ref, ga_ref, gc_ref, gl_ref, o_ref):
        o_ref[:, pl.ds(0, wa)] = (_rms_stats(ya_ref[...])[0] * ga_ref[...]).astype(BF16)
        o_ref[:, pl.ds(wa, wc)] = (_rms_stats(yc_ref[...])[0] * gc_ref[...]).astype(BF16)
        o_ref[:, pl.ds(wa + wc, wl)] = (_rms_stats(yl_ref[...])[0] * gl_ref[...]).astype(BF16)

    return pl.pallas_call(
        body, name="mix_fwd", grid=(t // tr,),
        in_specs=[_row_spec(tr, wa), _row_spec(tr, wc), _row_spec(tr, wl), _vec_spec(wa), _vec_spec(wc), _vec_spec(wl)],
        out_specs=_row_spec(tr, wa + wc + wl),
        out_shape=jax.ShapeDtypeStruct((t, wa + wc + wl), BF16), compiler_params=_params(1),
    )(ya, yc, yl, ga, gc, gl)


def mix_bwd(dmixed, ya, yc, yl, cpre, ga, gc, gl, lng, lnb):
    t, wa = ya.shape
    wc, wl = yc.shape[1], yl.shape[1]
    tr = _tile(t, ROW_T)

    def body(dm_ref, ya_ref, yc_ref, yl_ref, c_ref, ga_ref, gc_ref, gl_ref, lg_ref, lb_ref,
             dya_ref, dc_ref, dyl_ref, dga_ref, dgc_ref, dgl_ref, dlg_ref, dlb_ref):
        i = pl.program_id(0)

        @pl.when(i == 0)
        def _():
            for r in (dga_ref, dgc_ref, dgl_ref, dlg_ref, dlb_ref):
                r[...] = jnp.zeros_like(r)

        dya, dga = _rms_bwd(dm_ref[:, pl.ds(0, wa)], ya_ref[...], ga_ref[...])
        dya_ref[...] = dya
        dga_ref[...] += dga
        dyl, dgl = _rms_bwd(dm_ref[:, pl.ds(wa + wc, wl)], yl_ref[...], gl_ref[...])
        dyl_ref[...] = dyl
        dgl_ref[...] += dgl
        dyc, dgc = _rms_bwd(dm_ref[:, pl.ds(wa, wc)], yc_ref[...], gc_ref[...])
        dgc_ref[...] += dgc
        c = c_ref[...]
        xc = c - jnp.mean(c, axis=-1, keepdims=True)
        rstd = lax.rsqrt(jnp.mean(xc * xc, axis=-1, keepdims=True) + EPS)
        xhat = xc * rstd
        ln = xhat * lg_ref[...] + lb_ref[...]
        s = _sigmoid(ln)
        dln = dyc * s * (1.0 + ln * (1.0 - s))
        dlg_ref[...] += jnp.sum(dln * xhat, axis=0, keepdims=True)
        dlb_ref[...] += jnp.sum(dln, axis=0, keepdims=True)
        dxh = dln * lg_ref[...]
        dc_ref[...] = rstd * (dxh - jnp.mean(dxh, axis=-1, keepdims=True)
                              - xhat * jnp.mean(dxh * xhat, axis=-1, keepdims=True))

    return pl.pallas_call(
        body, name="mix_bwd", grid=(t // tr,),
        in_specs=[_row_spec(tr, wa + wc + wl), _row_spec(tr, wa), _row_spec(tr, wc), _row_spec(tr, wl), _row_spec(tr, wc),
                  _vec_spec(wa), _vec_spec(wc), _vec_spec(wl), _vec_spec(wc), _vec_spec(wc)],
        out_specs=[_row_spec(tr, wa), _row_spec(tr, wc), _row_spec(tr, wl),
                   _vec_spec(wa), _vec_spec(wc), _vec_spec(wl), _vec_spec(wc), _vec_spec(wc)],
        out_shape=[jax.ShapeDtypeStruct((t, wa), F32), jax.ShapeDtypeStruct((t, wc), F32), jax.ShapeDtypeStruct((t, wl), F32),
                   jax.ShapeDtypeStruct((1, wa), F32), jax.ShapeDtypeStruct((1, wc), F32), jax.ShapeDtypeStruct((1, wl), F32),
                   jax.ShapeDtypeStruct((1, wc), F32), jax.ShapeDtypeStruct((1, wc), F32)],
        compiler_params=_params(1),
    )(dmixed, ya, yc, yl, cpre, ga, gc, gl, lng, lnb)


def _hi_lo(x):
    hi = x.astype(BF16)
    return hi, (x - hi.astype(F32)).astype(BF16)


def _att_block(qb, kt, jj, scale, lower, tri_gt):
    z = _dot(qb, kt, "nt") * scale
    sp = _softplus(z)
    mask = jnp.logical_or(jj > 0, lower)
    lk = jnp.where(mask, -sp, 0.0)
    hi, lo = _hi_lo(lk)
    logw = (z - sp) + _dot(hi, tri_gt) + _dot(lo, tri_gt)
    return z, sp, mask, lk, logw


def attn_fwd(proj, n_heads):
    t = proj.shape[0]
    tb = _tile(t, ATT_T)
    nq = t // tb
    scale = HEAD_DIM ** -0.5

    def body(q_ref, k_ref, v_ref, o_ref, kb_ref, vb_ref):
        kb_ref[...] = k_ref[...].astype(BF16)
        vb_ref[...] = v_ref[...].astype(BF16)
        row = lax.broadcasted_iota(jnp.int32, (tb, tb), 0)
        col = lax.broadcasted_iota(jnp.int32, (tb, tb), 1)
        lower = col < row
        tri_gt = (row > col).astype(BF16)

        def qblock(i, _):
            q0 = pl.multiple_of(i * tb, tb)
            qb = q_ref[pl.ds(q0, tb), :].astype(BF16)

            def kblock(jj, carry):
                acc, run = carry
                k0 = pl.multiple_of((i - jj) * tb, tb)
                _, _, mask, lk, logw = _att_block(qb, kb_ref[pl.ds(k0, tb), :], jj, scale, lower, tri_gt)
                w = jnp.where(mask, jnp.exp(logw + run), 0.0)
                acc = acc + _dot(w.astype(BF16), vb_ref[pl.ds(k0, tb), :])
                return acc, run + jnp.sum(lk, axis=1, keepdims=True)

            acc, _ = lax.fori_loop(0, i + 1, kblock, (jnp.zeros((tb, HEAD_DIM), F32), jnp.zeros((tb, 1), F32)))
            o_ref[pl.ds(q0, tb), :] = acc
            return 0

        lax.fori_loop(0, nq, qblock, 0)

    def col_spec(base):
        return pl.BlockSpec((t, HEAD_DIM), lambda h, base=base: (0, base + h))

    return pl.pallas_call(
        body, name="attn_fwd", grid=(n_heads,),
        in_specs=[col_spec(0), col_spec(n_heads), col_spec(2 * n_heads)],
        out_specs=col_spec(0),
        out_shape=jax.ShapeDtypeStruct((t, n_heads * HEAD_DIM), F32),
        scratch_shapes=[pltpu.VMEM((t, HEAD_DIM), BF16), pltpu.VMEM((t, HEAD_DIM), BF16)],
        compiler_params=_params(1),
    )(proj, proj, proj)


def attn_bwd(proj, dy, n_heads):
    t = proj.shape[0]
    tb = _tile(t, ATT_T)
    nq = t // tb
    scale = HEAD_DIM ** -0.5

    def body(q_ref, k_ref, v_ref, dy_ref, dq_ref, dk_ref, dv_ref, qb_ref, kb_ref, vb_ref, dob_ref, dk_acc, dv_acc, run_s):
        qb_ref[...] = q_ref[...].astype(BF16)
        kb_ref[...] = k_ref[...].astype(BF16)
        vb_ref[...] = v_ref[...].astype(BF16)
        dob_ref[...] = dy_ref[...].astype(BF16)
        dk_acc[...] = jnp.zeros_like(dk_acc)
        dv_acc[...] = jnp.zeros_like(dv_acc)
        row = lax.broadcasted_iota(jnp.int32, (tb, tb), 0)
        col = lax.broadcasted_iota(jnp.int32, (tb, tb), 1)
        lower = col < row
        tri_gt = (row > col).astype(BF16)
        tri_lt = (row < col).astype(BF16)

        def qblock(i, _):
            q0 = pl.multiple_of(i * tb, tb)
            qb = qb_ref[pl.ds(q0, tb), :]
            dob = dob_ref[pl.ds(q0, tb), :]

            def sweep(jj, run):
                k0 = pl.multiple_of((i - jj) * tb, tb)
                sp = _softplus(_dot(qb, kb_ref[pl.ds(k0, tb), :], "nt") * scale)
                run_s[i - jj] = run
                return run + jnp.sum(jnp.where(jnp.logical_or(jj > 0, lower), -sp, 0.0), axis=1, keepdims=True)

            lax.fori_loop(0, i + 1, sweep, jnp.zeros((tb, 1), F32))

            def kblock(kbi, carry):
                dq, gsum = carry
                k0 = pl.multiple_of(kbi * tb, tb)
                kt = kb_ref[pl.ds(k0, tb), :]
                vt = vb_ref[pl.ds(k0, tb), :]
                z, sp, mask, _, logw = _att_block(qb, kt, i - kbi, scale, lower, tri_gt)
                w = jnp.where(mask, jnp.exp(logw + run_s[kbi]), 0.0)
                g = w * _dot(dob, vt, "nt")
                hi, lo = _hi_lo(g)
                before = _dot(hi, tri_lt) + _dot(lo, tri_lt) + gsum
                sig = jnp.exp(z - sp)
                dz = jnp.where(mask, g * (1.0 - sig) - before * sig, 0.0) * scale
                dzb = dz.astype(BF16)
                dk_acc[pl.ds(k0, tb), :] += _dot(dzb, qb, "tn")
                dv_acc[pl.ds(k0, tb), :] += _dot(w.astype(BF16), dob, "tn")
                return dq + _dot(dzb, kt), gsum + jnp.sum(g, axis=1, keepdims=True)

            dq, _ = lax.fori_loop(0, i + 1, kblock, (jnp.zeros((tb, HEAD_DIM), F32), jnp.zeros((tb, 1), F32)))
            dq_ref[pl.ds(q0, tb), :] = dq.astype(BF16)
            return 0

        lax.fori_loop(0, nq, qblock, 0)
        dk_ref[...] = dk_acc[...].astype(BF16)
        dv_ref[...] = dv_acc[...].astype(BF16)

    def col_spec(base):
        return pl.BlockSpec((t, HEAD_DIM), lambda h, base=base: (0, base + h))

    width = n_heads * HEAD_DIM
    return pl.pallas_call(
        body, name="attn_bwd", grid=(n_heads,),
        in_specs=[col_spec(0), col_spec(n_heads), col_spec(2 * n_heads), col_spec(0)],
        out_specs=[col_spec(0), col_spec(0), col_spec(0)],
        out_shape=[jax.ShapeDtypeStruct((t, width), BF16)] * 3,
        scratch_shapes=[pltpu.VMEM((t, HEAD_DIM), BF16)] * 4 + [pltpu.VMEM((t, HEAD_DIM), F32)] * 2
        + [pltpu.VMEM((nq, tb, 1), F32)],
        compiler_params=_params(1),
    )(proj, proj, proj, dy)


def _glu_halo(vc, gc, vp, gp, ubuf, i, tt, halo):
    uprev = vp[pl.ds(tt - halo, halo), :] * _sigmoid(gp[pl.ds(tt - halo, halo), :])
    ubuf[pl.ds(0, halo), :] = jnp.where(i > 0, uprev, 0.0)
    ubuf[pl.ds(halo, tt), :] = vc[...] * _sigmoid(gc[...])


def conv_fwd(proj, col0, cc, w, b, lng, lnb):
    t = proj.shape[0]
    tt = _tile(t, ROW_T)
    vi, gi = col0 // cc, col0 // cc + 1
    off = CONV_HALO - (DW_LEN - 1)

    def body(vc, gc, vp, gp, w_ref, b_ref, lg_ref, lb_ref, c_ref, y_ref, ubuf):
        i = pl.program_id(0)
        _glu_halo(vc, gc, vp, gp, ubuf, i, tt, CONV_HALO)
        for ch in range(cc // LANE):
            sl = pl.ds(ch * LANE, LANE)
            acc = jnp.zeros((tt, LANE), F32) + b_ref[:, sl]
            for tap in range(DW_LEN):
                acc = acc + w_ref[pl.ds(tap, 1), sl] * ubuf[pl.ds(off + tap, tt), sl]
            c_ref[:, sl] = acc
        c = c_ref[...]
        xc = c - jnp.mean(c, axis=-1, keepdims=True)
        ln = xc * lax.rsqrt(jnp.mean(xc * xc, axis=-1, keepdims=True) + EPS) * lg_ref[...] + lb_ref[...]
        y_ref[...] = ln * _sigmoid(ln)

    cur = lambda c: pl.BlockSpec((tt, cc), lambda i, c=c: (i, c))
    prev = lambda c: pl.BlockSpec((tt, cc), lambda i, c=c: (jnp.maximum(i - 1, 0), c))
    return pl.pallas_call(
        body, name="conv_fwd", grid=(t // tt,),
        in_specs=[cur(vi), cur(gi), prev(vi), prev(gi), pl.BlockSpec((DW_LEN, cc), lambda i: (0, 0)),
                  _vec_spec(cc), _vec_spec(cc), _vec_spec(cc)],
        out_specs=[_row_spec(tt, cc), _row_spec(tt, cc)],
        out_shape=[jax.ShapeDtypeStruct((t, cc), F32)] * 2,
        scratch_shapes=[pltpu.VMEM((CONV_HALO + tt, cc), F32)],
        compiler_params=_params(1),
    )(proj, proj, proj, proj, w, b, lng, lnb)


def conv_bwd(proj, col0, cc, dc, w):
    t = proj.shape[0]
    tt = _tile(t, ROW_T)
    nt = t // tt
    vi, gi = col0 // cc, col0 // cc + 1
    off = CONV_HALO - (DW_LEN - 1)

    def body(vc, gc, vp, gp, dcc, dcn, w_ref, dvg_ref, dw_ref, db_ref, ubuf, dbuf):
        i = pl.program_id(0)

        @pl.when(i == 0)
        def _():
            dw_ref[...] = jnp.zeros_like(dw_ref)
            db_ref[...] = jnp.zeros_like(db_ref)

        _glu_halo(vc, gc, vp, gp, ubuf, i, tt, CONV_HALO)
        dbuf[pl.ds(0, tt), :] = dcc[...]
        dbuf[pl.ds(tt, CONV_HALO), :] = jnp.where(i < nt - 1, dcn[pl.ds(0, CONV_HALO), :], 0.0)
        db_ref[...] += jnp.sum(dcc[...], axis=0, keepdims=True)
        for ch in range(cc // LANE):
            sl = pl.ds(ch * LANE, LANE)
            dcv = dbuf[pl.ds(0, tt), sl]
            du = jnp.zeros((tt, LANE), F32)
            for tap in range(DW_LEN):
                du = du + w_ref[pl.ds(tap, 1), sl] * dbuf[pl.ds(DW_LEN - 1 - tap, tt), sl]
                dw_ref[pl.ds(tap, 1), sl] += jnp.sum(dcv * ubuf[pl.ds(off + tap, tt), sl], axis=0, keepdims=True)
            s = _sigmoid(gc[:, sl])
            val = vc[:, sl]
            dvg_ref[:, sl] = (du * s).astype(BF16)
            dvg_ref[:, pl.ds(cc + ch * LANE, LANE)] = (du * val * s * (1.0 - s)).astype(BF16)

    cur = lambda c: pl.BlockSpec((tt, cc), lambda i, c=c: (i, c))
    prev = lambda c: pl.BlockSpec((tt, cc), lambda i, c=c: (jnp.maximum(i - 1, 0), c))
    return pl.pallas_call(
        body, name="conv_bwd", grid=(nt,),
        in_specs=[cur(vi), cur(gi), prev(vi), prev(gi), _row_spec(tt, cc),
                  pl.BlockSpec((tt, cc), lambda i: (jnp.minimum(i + 1, nt - 1), 0)),
                  pl.BlockSpec((DW_LEN, cc), lambda i: (0, 0))],
        out_specs=[_row_spec(tt, 2 * cc), pl.BlockSpec((DW_LEN, cc), lambda i: (0, 0)), _vec_spec(cc)],
        out_shape=[jax.ShapeDtypeStruct((t, 2 * cc), BF16), jax.ShapeDtypeStruct((DW_LEN, cc), F32),
                   jax.ShapeDtypeStruct((1, cc), F32)],
        scratch_shapes=[pltpu.VMEM((CONV_HALO + tt, cc), F32), pltpu.VMEM((tt + CONV_HALO, cc), F32)],
        compiler_params=_params(1),
    )(proj, proj, proj, proj, dc, dc, w)


def _lru_gates(xbuf, cw_ref, cb_ref, wa_ref, ba_ref, wi_ref, bi_ref, lam_ref, tt, wl):
    bd = wl // LRU_BLOCKS
    xr = jnp.zeros((tt, wl), F32) + cb_ref[...]
    for tap in range(LRU_LEN):
        xr = xr + cw_ref[pl.ds(tap, 1), :] * xbuf[pl.ds(LRU_HALO - (LRU_LEN - 1) + tap, tt), :]
    xb = xr.astype(BF16)
    ga = jnp.concatenate([_dot(xb[:, n * bd:(n + 1) * bd], wa_ref[n]) for n in range(LRU_BLOCKS)], axis=1) + ba_ref[...]
    gi = jnp.concatenate([_dot(xb[:, n * bd:(n + 1) * bd], wi_ref[n]) for n in range(LRU_BLOCKS)], axis=1) + bi_ref[...]
    r = _sigmoid(ga)
    ig = _sigmoid(gi)
    spl = _softplus(-lam_ref[...])
    log_a = -LRU_C * r * spl
    a = jnp.exp(log_a)
    m = jnp.sqrt(_neg_expm1(2.0 * log_a))
    return xr, xb, r, ig, spl, a, m


def _group_scan(a8, b8, reverse):
    rid = lax.broadcasted_iota(jnp.int32, a8.shape, 0)
    aa, bb = a8, b8
    for dist in (1, 2, 4):
        shift = SUBLANE - dist if reverse else dist
        a_sh = pltpu.roll(aa, shift, 0)
        b_sh = pltpu.roll(bb, shift, 0)
        valid = (rid < SUBLANE - dist) if reverse else (rid >= dist)
        bb = jnp.where(valid, aa * b_sh + bb, bb)
        aa = jnp.where(valid, aa * a_sh, aa)
    return aa, bb


def _pick_row(x8, r):
    rid = lax.broadcasted_iota(jnp.int32, x8.shape, 0)
    return jnp.sum(jnp.where(rid == r, x8, 0.0), axis=0, keepdims=True)


def lru_fwd(proj, col0, wl, cw, cb, wa, ba, wi, bi, lam):
    t = proj.shape[0]
    tt = _tile(t, ROW_T)
    xi, yi = col0 // wl, col0 // wl + 1

    def body(xc, xp, ry, cw_ref, cb_ref, wa_ref, ba_ref, wi_ref, bi_ref, lam_ref, hs_ref, y_ref,
             xbuf, a_s, b_s, hcar):
        i = pl.program_id(0)

        @pl.when(i == 0)
        def _():
            hcar[...] = jnp.zeros_like(hcar)

        xbuf[pl.ds(0, LRU_HALO), :] = jnp.where(i > 0, xp[pl.ds(tt - LRU_HALO, LRU_HALO), :], 0.0)
        xbuf[pl.ds(LRU_HALO, tt), :] = xc[...]
        xr, _, _, ig, _, a, m = _lru_gates(xbuf, cw_ref, cb_ref, wa_ref, ba_ref, wi_ref, bi_ref, lam_ref, tt, wl)
        a_s[...] = a
        b_s[...] = m * ig * xr

        def group(gidx, h):
            r0 = pl.multiple_of(gidx * SUBLANE, SUBLANE)
            aa, bb = _group_scan(a_s[pl.ds(r0, SUBLANE), :], b_s[pl.ds(r0, SUBLANE), :], False)
            h8 = aa * h + bb
            hs_ref[pl.ds(r0, SUBLANE), :] = h8
            return _pick_row(h8, SUBLANE - 1)

        hcar[...] = lax.fori_loop(0, tt // SUBLANE, group, hcar[...])
        gel, _ = _gelu_and_grad(ry[...])
        y_ref[...] = hs_ref[...] * gel

    cur = lambda c: pl.BlockSpec((tt, wl), lambda i, c=c: (i, c))
    full = lambda shape: pl.BlockSpec(shape, lambda i: (0,) * len(shape))
    return pl.pallas_call(
        body, name="lru_fwd", grid=(t // tt,),
        in_specs=[cur(xi), pl.BlockSpec((tt, wl), lambda i: (jnp.maximum(i - 1, 0), xi)), cur(yi),
                  full((LRU_LEN, wl)), _vec_spec(wl), full(wa.shape), _vec_spec(wl), full(wi.shape), _vec_spec(wl),
                  _vec_spec(wl)],
        out_specs=[_row_spec(tt, wl), _row_spec(tt, wl)],
        out_shape=[jax.ShapeDtypeStruct((t, wl), F32)] * 2,
        scratch_shapes=[pltpu.VMEM((LRU_HALO + tt, wl), F32), pltpu.VMEM((tt, wl), F32), pltpu.VMEM((tt, wl), F32),
                        pltpu.VMEM((1, wl), F32)],
        compiler_params=_params(1),
    )(proj, proj, proj, cw, cb, wa, ba, wi, bi, lam)


def lru_bwd(proj, col0, wl, hs, dy, cw, cb, wa, ba, wi, bi, lam):
    t = proj.shape[0]
    tt = _tile(t, ROW_T)
    nt = t // tt
    xi, yi = col0 // wl, col0 // wl + 1
    bd = wl // LRU_BLOCKS

    def body(xc, xp, ry, hc, hp, dy_ref, cw_ref, cb_ref, wa_ref, ba_ref, wi_ref, bi_ref, lam_ref,
             dxy_ref, dcw_ref, dcb_ref, dwa_ref, dba_ref, dwi_ref, dbi_ref, dlam_ref,
             xbuf, hbuf, abuf, e_s, dh_s, dxbuf, dhcar):
        i = pl.program_id(0)
        first = i == 0

        @pl.when(first)
        def _():
            for r in (dcw_ref, dcb_ref, dwa_ref, dba_ref, dwi_ref, dbi_ref, dlam_ref, dhcar):
                r[...] = jnp.zeros_like(r)
            abuf[pl.ds(tt, LRU_HALO), :] = jnp.zeros((LRU_HALO, wl), F32)
            dxbuf[pl.ds(tt, LRU_HALO), :] = jnp.zeros((LRU_HALO, wl), F32)

        has_prev = i < nt - 1
        xbuf[pl.ds(0, LRU_HALO), :] = jnp.where(has_prev, xp[pl.ds(tt - LRU_HALO, LRU_HALO), :], 0.0)
        xbuf[pl.ds(LRU_HALO, tt), :] = xc[...]
        hbuf[pl.ds(0, LRU_HALO), :] = jnp.where(has_prev, hp[pl.ds(tt - LRU_HALO, LRU_HALO), :], 0.0)
        hbuf[pl.ds(LRU_HALO, tt), :] = hc[...]
        xr, xb, r, ig, spl, a, m = _lru_gates(xbuf, cw_ref, cb_ref, wa_ref, ba_ref, wi_ref, bi_ref, lam_ref, tt, wl)
        gel, dgel = _gelu_and_grad(ry[...])
        dyv = dy_ref[...]
        e_s[...] = dyv * gel
        dxy_ref[:, pl.ds(wl, wl)] = (dyv * hc[...] * dgel).astype(BF16)
        abuf[pl.ds(0, tt), :] = a
        a_next = abuf[pl.ds(1, tt), :]
        dh_s[...] = a_next

        def group(it, dh_in):
            r0 = pl.multiple_of((tt // SUBLANE - 1 - it) * SUBLANE, SUBLANE)
            aa, bb = _group_scan(dh_s[pl.ds(r0, SUBLANE), :], e_s[pl.ds(r0, SUBLANE), :], True)
            dh8 = aa * dh_in + bb
            dh_s[pl.ds(r0, SUBLANE), :] = dh8
            return _pick_row(dh8, 0)

        dhcar[...] = lax.fori_loop(0, tt // SUBLANE, group, dhcar[...])
        abuf[pl.ds(tt, LRU_HALO), :] = a[0:LRU_HALO, :]
        dh = dh_s[...]
        h_m1 = hbuf[pl.ds(LRU_HALO - 1, tt), :]
        dlog_a = dh * h_m1 * a - dh * ig * xr * (a * a / m)
        dig = dh * m * xr
        dxr = dh * m * ig
        dga = dlog_a * (-LRU_C) * spl * r * (1.0 - r)
        dgi = dig * ig * (1.0 - ig)
        dlam_ref[...] += jnp.sum(dlog_a * r, axis=0, keepdims=True) * (LRU_C * _sigmoid(-lam_ref[...]))
        dba_ref[...] += jnp.sum(dga, axis=0, keepdims=True)
        dbi_ref[...] += jnp.sum(dgi, axis=0, keepdims=True)
        dgab = dga.astype(BF16)
        dgib = dgi.astype(BF16)
        back = []
        for n in range(LRU_BLOCKS):
            sl = slice(n * bd, (n + 1) * bd)
            dwa_ref[n] += _dot(xb[:, sl], dgab[:, sl], "tn")
            dwi_ref[n] += _dot(xb[:, sl], dgib[:, sl], "tn")
            back.append(_dot(dgab[:, sl], wa_ref[n], "nt") + _dot(dgib[:, sl], wi_ref[n], "nt"))
        dxr = dxr + jnp.concatenate(back, axis=1)
        dcb_ref[...] += jnp.sum(dxr, axis=0, keepdims=True)
        dxbuf[pl.ds(0, tt), :] = dxr
        drx = jnp.zeros((tt, wl), F32)
        for tap in range(LRU_LEN):
            drx = drx + cw_ref[pl.ds(tap, 1), :] * dxbuf[pl.ds(LRU_LEN - 1 - tap, tt), :]
            dcw_ref[pl.ds(tap, 1), :] += jnp.sum(
                dxr * xbuf[pl.ds(LRU_HALO - (LRU_LEN - 1) + tap, tt), :], axis=0, keepdims=True)
        dxbuf[pl.ds(tt, LRU_HALO), :] = dxr[0:LRU_HALO, :]
        dxy_ref[:, pl.ds(0, wl)] = drx.astype(BF16)

    rev = lambda c: pl.BlockSpec((tt, wl), lambda i, c=c: (nt - 1 - i, c))
    rev_prev = lambda c: pl.BlockSpec((tt, wl), lambda i, c=c: (jnp.maximum(nt - 2 - i, 0), c))
    full = lambda shape: pl.BlockSpec(shape, lambda i: (0,) * len(shape))
    vec = _vec_spec(wl)
    return pl.pallas_call(
        body, name="lru_bwd", grid=(nt,),
        in_specs=[rev(xi), rev_prev(xi), rev(yi), rev(0), rev_prev(0), rev(0),
                  full((LRU_LEN, wl)), vec, full(wa.shape), vec, full(wi.shape), vec, vec],
        out_specs=[pl.BlockSpec((tt, 2 * wl), lambda i: (nt - 1 - i, 0)), full((LRU_LEN, wl)), vec,
                   full(wa.shape), vec, full(wi.shape), vec, vec],
        out_shape=[jax.ShapeDtypeStruct((t, 2 * wl), BF16), jax.ShapeDtypeStruct((LRU_LEN, wl), F32),
                   jax.ShapeDtypeStruct((1, wl), F32), jax.ShapeDtypeStruct(wa.shape, F32),
                   jax.ShapeDtypeStruct((1, wl), F32), jax.ShapeDtypeStruct(wi.shape, F32),
                   jax.ShapeDtypeStruct((1, wl), F32), jax.ShapeDtypeStruct((1, wl), F32)],
        scratch_shapes=[pltpu.VMEM((LRU_HALO + tt, wl), F32), pltpu.VMEM((LRU_HALO + tt, wl), F32),
                        pltpu.VMEM((tt + LRU_HALO, wl), F32), pltpu.VMEM((tt, wl), F32), pltpu.VMEM((tt, wl), F32),
                        pltpu.VMEM((tt + LRU_HALO, wl), F32), pltpu.VMEM((1, wl), F32)],
        compiler_params=_params(1),
    )(proj, proj, proj, hs, hs, dy, cw, cb, wa, ba, wi, bi, lam)


def _adamw(w, g, m, v):
    m = ADAM_B1 * m + (1.0 - ADAM_B1) * g
    v = ADAM_B2 * v + (1.0 - ADAM_B2) * (g * g)
    m_hat = m / (1.0 - ADAM_B1 ** ADAM_STEP)
    v_hat = v / (1.0 - ADAM_B2 ** ADAM_STEP)
    delta = -ADAM_LR * (m_hat / (jnp.sqrt(v_hat) + ADAM_EPS) + ADAM_WD * w)
    return delta, m, v


def adam_big(name, w, m, v, parts, chip):
    n_layers, rows, cols = w.shape
    tr = _tile(rows, 128 if cols > 1024 else 256)
    nrt = rows // tr

    def body(chip_ref, *refs):
        w_ref, m_ref, v_ref = refs[:3]
        part_refs = refs[3:3 + 4 * n_layers]
        g_ref, d_ref, mo_ref, vo_ref = refs[3 + 4 * n_layers:]
        layer = pl.program_id(0)
        for l in range(n_layers):
            @pl.when(layer == l)
            def _(l=l):
                g = part_refs[4 * l][...].astype(F32)
                for p in range(1, 4):
                    g = g + part_refs[4 * l + p][...].astype(F32)
                delta, mn, vn = _adamw(w_ref[...], g, m_ref[...], v_ref[...])
                g_ref[...] = g
                d_ref[...] = delta
                mo_ref[...] = mn
                vo_ref[...] = vn

    wspec = pl.BlockSpec((None, tr, cols), lambda l, i, chip_ref: (l, i, 0))
    operands, in_specs = [w, m, v], [wspec, wspec, wspec]
    for l in range(n_layers):
        mine, recv = parts[l]
        operands.append(mine)
        in_specs.append(pl.BlockSpec(
            (None, tr, cols), lambda ll, i, chip_ref, l=l: (chip_ref[0], jnp.where(ll == l, i, 0), 0)))
        for p in range(3):
            operands.append(recv)
            in_specs.append(pl.BlockSpec(
                (None, tr, cols), lambda ll, i, chip_ref, l=l, p=p: (p, jnp.where(ll == l, i, 0), 0)))
    return pl.pallas_call(
        body, name=name,
        grid_spec=pltpu.PrefetchScalarGridSpec(
            num_scalar_prefetch=1, grid=(n_layers, nrt), in_specs=in_specs, out_specs=[wspec] * 4),
        out_shape=[jax.ShapeDtypeStruct(w.shape, F32)] * 4, compiler_params=_params(2),
    )(chip, *operands)


def adam_small(w, m, v, g):
    rows = w.shape[0]
    tr = _tile(rows, PACK_ROWS)

    def body(w_ref, m_ref, v_ref, g_ref, d_ref, mo_ref, vo_ref):
        delta, mn, vn = _adamw(w_ref[...], g_ref[...], m_ref[...], v_ref[...])
        d_ref[...] = delta
        mo_ref[...] = mn
        vo_ref[...] = vn

    spec = _row_spec(tr, LANE)
    return pl.pallas_call(
        body, name="adam_small", grid=(rows // tr,), in_specs=[spec] * 4, out_specs=[spec] * 3,
        out_shape=[jax.ShapeDtypeStruct(w.shape, F32)] * 3, compiler_params=_params(1),
    )(w, m, v, g)


def sum_parts(parts):
    _, rows, _ = parts.shape
    tr = _tile(rows, PACK_ROWS)

    def body(p_ref, o_ref):
        acc = p_ref[0]
        for k in range(1, N_DEV):
            acc = acc + p_ref[k]
        o_ref[...] = acc

    return pl.pallas_call(
        body, name="sum_parts", grid=(rows // tr,),
        in_specs=[pl.BlockSpec((N_DEV, tr, LANE), lambda i: (0, i, 0))], out_specs=_row_spec(tr, LANE),
        out_shape=jax.ShapeDtypeStruct((rows, LANE), F32), compiler_params=_params(1),
    )(parts)


def pair_add(g, recv, core):
    n, rows, cols = recv.shape
    tr = _tile(rows, 256)

    def body(core_ref, a_ref, b_ref, o_ref):
        o_ref[...] = (a_ref[...].astype(F32) + b_ref[...].astype(F32)).astype(BF16)

    spec = pl.BlockSpec((None, tr, cols), lambda k, i, core_ref: (k, i, 0))
    return pl.pallas_call(
        body, name="pair_add",
        grid_spec=pltpu.PrefetchScalarGridSpec(
            num_scalar_prefetch=1, grid=(n, rows // tr),
            in_specs=[pl.BlockSpec((None, tr, cols), lambda k, i, core_ref: (2 * k + core_ref[0], i, 0)), spec],
            out_specs=spec),
        out_shape=jax.ShapeDtypeStruct(recv.shape, BF16), compiler_params=_params(2),
    )(core, g, recv)


def place_own(x, me, dtype):
    rows, cols = x.shape
    tr = _tile(rows, 256)

    def body(me_ref, x_ref, o_ref):
        o_ref[...] = x_ref[...].astype(dtype)

    return pl.pallas_call(
        body, name="place_own",
        grid_spec=pltpu.PrefetchScalarGridSpec(
            num_scalar_prefetch=1, grid=(rows // tr,),
            in_specs=[pl.BlockSpec((tr, cols), lambda i, me_ref: (i, 0))],
            out_specs=pl.BlockSpec((None, tr, cols), lambda i, me_ref: (me_ref[0], i, 0))),
        out_shape=jax.ShapeDtypeStruct((N_DEV, rows, cols), dtype), compiler_params=_params(1),
    )(me, x)


_HBM = pl.BlockSpec(memory_space=pltpu.HBM)


def _place():
    return lax.axis_index("x"), lax.axis_index("y"), lax.axis_index("c")


def _other_chips(x, y):
    return [(1 - x, y), (x, 1 - y), (1 - x, 1 - y)]


def all_gather(name, shard, me, dtype=None):
    def body(buf_ref, out_ref, send_sems, recv_sems):
        del buf_ref
        x, y, c = _place()
        mine, sibling = (x, y, c), (x, y, 1 - c)
        chips = _other_chips(x, y)

        def copy(k, block, to):
            slot = out_ref.at[4 * block[0] + 2 * block[1] + block[2]]
            return pltpu.make_async_remote_copy(
                src_ref=slot, dst_ref=slot, send_sem=send_sems.at[k], recv_sem=recv_sems.at[k],
                device_id=to, device_id_type=pl.DeviceIdType.MESH)

        first = [copy(0, mine, sibling)] + [copy(1 + j, mine, (*chip, c)) for j, chip in enumerate(chips)]
        for cp in first:
            cp.start()
        passed = [copy(4 + j, (*chip, c), sibling) for j, chip in enumerate(chips)]
        for j, chip in enumerate(chips):
            copy(1 + j, (*chip, c), mine).wait_recv()
            passed[j].start()
        copy(0, sibling, mine).wait_recv()
        for j, chip in enumerate(chips):
            copy(4 + j, (*chip, 1 - c), mine).wait_recv()
        for cp in first + passed:
            cp.wait_send()

    buf = place_own(shard, me, dtype or shard.dtype)
    return pl.pallas_call(
        body, name=name, out_shape=jax.ShapeDtypeStruct(buf.shape, buf.dtype),
        in_specs=[_HBM], out_specs=_HBM, input_output_aliases={0: 0},
        scratch_shapes=[pltpu.SemaphoreType.DMA((7,)), pltpu.SemaphoreType.DMA((7,))],
    )(buf)


def _own_block_copies(src_refs, dst_refs, send_sems, recv_sems):
    x, y, c = _place()
    peers = [(x, y, 1 - c)] + [(*chip, c) for chip in _other_chips(x, y)]
    copies = []
    for b, (src, dst) in enumerate(zip(src_refs, dst_refs)):
        for k, peer in enumerate(peers):
            copies.append((
                pltpu.make_async_remote_copy(
                    src_ref=src.at[4 * x + 2 * y + c], dst_ref=dst.at[4 * x + 2 * y + c],
                    send_sem=send_sems.at[4 * b + k], recv_sem=recv_sems.at[4 * b + k],
                    device_id=peer, device_id_type=pl.DeviceIdType.MESH),
                pltpu.make_async_remote_copy(
                    src_ref=src.at[4 * x + 2 * y + c], dst_ref=dst.at[4 * peer[0] + 2 * peer[1] + peer[2]],
                    send_sem=send_sems.at[4 * b + k], recv_sem=recv_sems.at[4 * b + k],
                    device_id=peer, device_id_type=pl.DeviceIdType.MESH)))
    return copies


def gather_start(name, bufs, after):
    n = len(bufs)

    def body(*refs):
        send_sems, recv_sems = refs[n + 1], refs[n + 2]
        thru = refs[n + 3:2 * n + 3]
        for send, _ in _own_block_copies(thru, thru, send_sems, recv_sems):
            send.start()
        refs[2 * n + 3][...] = jnp.zeros((SUBLANE, LANE), F32)

    return pl.pallas_call(
        body, name=name,
        out_shape=(pltpu.SemaphoreType.DMA((4 * n,)), pltpu.SemaphoreType.DMA((4 * n,)),
                   *[pltpu.HBM(b.shape, b.dtype) for b in bufs], jax.ShapeDtypeStruct((SUBLANE, LANE), F32)),
        in_specs=(*(_HBM,) * n, _ANY), out_specs=(_SEM, _SEM, *(_HBM,) * n, _TOKEN),
        input_output_aliases={b: 2 + b for b in range(n)},
        compiler_params=pltpu.CompilerParams(has_side_effects=_EFFECT),
    )(*[_hbm(b) for b in bufs], after)


def gather_wait(name, state, first, count, after):
    send_sems, recv_sems = state[:2]
    bufs = state[2 + first:2 + first + count]
    n = len(bufs)

    def body(*refs):
        ins = refs[:n]
        send_sems, recv_sems = refs[n], refs[n + 1]
        shift = 4 * first
        for send, arrival in _own_block_copies(
                ins, ins, send_sems.at[pl.ds(shift, 4 * n)], recv_sems.at[pl.ds(shift, 4 * n)]):
            send.wait_send()
            arrival.wait_recv()

    return pl.pallas_call(
        body, name=name, out_shape=tuple(pltpu.HBM(b.shape, b.dtype) for b in bufs),
        in_specs=(*(_HBM,) * n, _SEM, _SEM, _ANY), out_specs=(_HBM,) * n,
        input_output_aliases={b: b for b in range(n)},
        compiler_params=pltpu.CompilerParams(has_side_effects=_EFFECT),
    )(*bufs, send_sems, recv_sems, after)


def gather_finish(name, bufs):
    n = len(bufs)

    def body(*refs):
        outs = refs[n:2 * n]
        send_sems, recv_sems = refs[2 * n], refs[2 * n + 1]
        x, y, c = _place()
        copies = []
        for b, out in enumerate(outs):
            for k, chip in enumerate(_other_chips(x, y)):
                sem = 3 * b + k
                copies.append((
                    pltpu.make_async_remote_copy(
                        src_ref=out.at[4 * chip[0] + 2 * chip[1] + c], dst_ref=out.at[4 * chip[0] + 2 * chip[1] + c],
                        send_sem=send_sems.at[sem], recv_sem=recv_sems.at[sem],
                        device_id=(x, y, 1 - c), device_id_type=pl.DeviceIdType.MESH),
                    pltpu.make_async_remote_copy(
                        src_ref=out.at[4 * chip[0] + 2 * chip[1] + c], dst_ref=out.at[4 * chip[0] + 2 * chip[1] + 1 - c],
                        send_sem=send_sems.at[sem], recv_sem=recv_sems.at[sem],
                        device_id=(x, y, 1 - c), device_id_type=pl.DeviceIdType.MESH)))
        for send, _ in copies:
            send.start()
        for send, arrival in copies:
            send.wait_send()
            arrival.wait_recv()

    return pl.pallas_call(
        body, name=name, out_shape=tuple(jax.ShapeDtypeStruct(b.shape, b.dtype) for b in bufs),
        in_specs=(_HBM,) * n, out_specs=(_HBM,) * n, input_output_aliases={b: b for b in range(n)},
        scratch_shapes=[pltpu.SemaphoreType.DMA((3 * n,)), pltpu.SemaphoreType.DMA((3 * n,))],
    )(*bufs)


def scatter_pair(name, g):
    _, rows, cols = g.shape

    def body(g_ref, recv_ref, send_sems, recv_sems):
        x, y, c = _place()
        remote = [pltpu.make_async_remote_copy(
            src_ref=g_ref.at[2 * k + 1 - c], dst_ref=recv_ref.at[k], send_sem=send_sems.at[k], recv_sem=recv_sems.at[k],
            device_id=(x, y, 1 - c), device_id_type=pl.DeviceIdType.MESH) for k in range(4)]
        for cp in remote:
            cp.start()
        for cp in remote:
            cp.wait()

    return pl.pallas_call(
        body, name=name, out_shape=jax.ShapeDtypeStruct((4, rows, cols), g.dtype), in_specs=[_HBM], out_specs=_HBM,
        scratch_shapes=[pltpu.SemaphoreType.DMA((4,)), pltpu.SemaphoreType.DMA((4,))],
    )(g)


def scatter_chips(name, p):
    _, rows, cols = p.shape

    def body(p_ref, recv_ref, send_sems, recv_sems):
        x, y, c = _place()
        remote = [pltpu.make_async_remote_copy(
            src_ref=p_ref.at[2 * px + py], dst_ref=recv_ref.at[k], send_sem=send_sems.at[k], recv_sem=recv_sems.at[k],
            device_id=(px, py, c), device_id_type=pl.DeviceIdType.MESH) for k, (px, py) in enumerate(_other_chips(x, y))]
        for cp in remote:
            cp.start()
        for cp in remote:
            cp.wait()

    return pl.pallas_call(
        body, name=name, out_shape=jax.ShapeDtypeStruct((3, rows, cols), p.dtype), in_specs=[_HBM], out_specs=_HBM,
        scratch_shapes=[pltpu.SemaphoreType.DMA((3,)), pltpu.SemaphoreType.DMA((3,))],
    )(p)


_SEM = pl.BlockSpec(memory_space=pltpu.SEMAPHORE)
_ANY = pl.BlockSpec(memory_space=pl.ANY)
_TOKEN = pl.BlockSpec(memory_space=pltpu.VMEM)
_EFFECT = pltpu.SideEffectType.DATAFLOW_SIDE_EFFECTING


def _hbm(a):
    return pltpu.with_memory_space_constraint(a, pltpu.HBM)


def _chip_copies(p_ref, land_ref, send_sems, recv_sems):
    x, y, c = _place()
    return [pltpu.make_async_remote_copy(
        src_ref=p_ref.at[2 * px + py], dst_ref=land_ref.at[k], send_sem=send_sems.at[k], recv_sem=recv_sems.at[k],
        device_id=(px, py, c), device_id_type=pl.DeviceIdType.MESH) for k, (px, py) in enumerate(_other_chips(x, y))]


def scatter_chips_start(name, p):
    _, rows, cols = p.shape

    def body(p_ref, land_ref, send_sems, recv_sems, p_thru, land_thru, token):
        for cp in _chip_copies(p_ref, land_ref, send_sems, recv_sems):
            cp.start()
        token[...] = jnp.zeros_like(token)

    return pl.pallas_call(
        body, name=name,
        out_shape=(pltpu.SemaphoreType.DMA((3,)), pltpu.SemaphoreType.DMA((3,)), pltpu.HBM(p.shape, p.dtype),
                   pltpu.HBM((3, rows, cols), p.dtype), jax.ShapeDtypeStruct((SUBLANE, LANE), F32)),
        in_specs=(_HBM, _HBM), out_specs=(_SEM, _SEM, _HBM, _HBM, _TOKEN), input_output_aliases={0: 2, 1: 3},
        compiler_params=pltpu.CompilerParams(has_side_effects=_EFFECT),
    )(_hbm(p), _hbm(lax.empty((3, rows, cols), p.dtype)))


def scatter_chips_wait(name, send_sems, recv_sems, p_thru, land_thru, after):
    def body(p_ref, land_ref, send_sems, recv_sems, after_ref, p_out, land_out):
        for cp in _chip_copies(p_ref, land_ref, send_sems, recv_sems):
            cp.wait_send()
            cp.wait_recv()

    return pl.pallas_call(
        body, name=name,
        out_shape=(pltpu.HBM(p_thru.shape, p_thru.dtype), pltpu.HBM(land_thru.shape, land_thru.dtype)),
        in_specs=(_HBM, _HBM, _SEM, _SEM, _ANY), out_specs=(_HBM, _HBM), input_output_aliases={0: 0, 1: 1},
        compiler_params=pltpu.CompilerParams(has_side_effects=_EFFECT),
    )(p_thru, land_thru, send_sems, recv_sems, after)


def reduce_scatter_start(tag, g, core):
    p = pair_add(g, scatter_pair("rs_pair_" + tag, g), core)
    return scatter_chips_start("rs_start_" + tag, p)


def reduce_scatter_wait(tag, state, after):
    send_sems, recv_sems, p_thru, land_thru, _ = state
    return scatter_chips_wait("rs_wait_" + tag, send_sems, recv_sems, p_thru, land_thru, after)


_SMALL = ("g_pre_mix", "g_post_mix", "g_pre_ffn", "g_post_ffn", "g_attn_grp", "g_conv_grp", "g_lru_grp",
          "dw_conv_w", "dw_conv_b", "conv_ln_g", "conv_ln_b", "lru_conv_w", "lru_conv_b",
          "lru_w_a", "lru_b_a", "lru_w_i", "lru_b_i", "lru_lambda")
_COL_SHARDED_SMALL = ("dw_conv_w", "lru_conv_w")
_BIG = ("w_in", "w_out", "w_gate", "w_up", "w_down")
_ALL = ("w_in", "w_out", "g_pre_mix", "g_post_mix", "g_pre_ffn", "g_post_ffn", "g_attn_grp", "g_conv_grp", "g_lru_grp",
        "dw_conv_w", "dw_conv_b", "conv_ln_g", "conv_ln_b", "lru_conv_w", "lru_conv_b", "lru_w_a", "lru_b_a",
        "lru_w_i", "lru_b_i", "lru_lambda", "w_gate", "w_up", "w_down")


def _pack(arrays):
    flat = jnp.concatenate([a.reshape(-1) for a in arrays])
    pad = (-flat.shape[0]) % (PACK_ROWS * LANE)
    return jnp.pad(flat, (0, pad)).reshape(-1, LANE)


def _unpack(packed, shapes):
    flat = packed.reshape(-1)
    out, pos = [], 0
    for s in shapes:
        n = math.prod(s)
        out.append(flat[pos:pos + n].reshape(s))
        pos += n
    return out


def kernel(x, w_in, w_out, g_pre_mix, g_post_mix, g_pre_ffn, g_post_ffn, g_attn_grp, g_conv_grp, g_lru_grp, dw_conv_w, dw_conv_b, conv_ln_g, conv_ln_b, lru_conv_w, lru_conv_b, lru_w_a, lru_b_a, lru_w_i, lru_b_i, lru_lambda, w_gate, w_up, w_down, loss_target, m_w_in, m_w_out, m_g_pre_mix, m_g_post_mix, m_g_pre_ffn, m_g_post_ffn, m_g_attn_grp, m_g_conv_grp, m_g_lru_grp, m_dw_conv_w, m_dw_conv_b, m_conv_ln_g, m_conv_ln_b, m_lru_conv_w, m_lru_conv_b, m_lru_w_a, m_lru_b_a, m_lru_w_i, m_lru_b_i, m_lru_lambda, m_w_gate, m_w_up, m_w_down, v_w_in, v_w_out, v_g_pre_mix, v_g_post_mix, v_g_pre_ffn, v_g_post_ffn, v_g_attn_grp, v_g_conv_grp, v_g_lru_grp, v_dw_conv_w, v_dw_conv_b, v_conv_ln_g, v_conv_ln_b, v_lru_conv_w, v_lru_conv_b, v_lru_w_a, v_lru_b_a, v_lru_w_i, v_lru_b_i, v_lru_lambda, v_w_gate, v_w_up, v_w_down):
    env = dict(locals())
    wts = {n: env[n] for n in _ALL}
    mom = {n: env["m_" + n] for n in _ALL}
    var = {n: env["v_" + n] for n in _ALL}

    depth = w_in.shape[0]
    h = x[0]
    target = loss_target[0]
    t, d = h.shape
    attn_w = d // 2
    n_heads = attn_w // HEAD_DIM
    cc = d // 4
    wl = d // 4
    conv_col, lru_col = 3 * attn_w, 3 * attn_w + 2 * cc
    me = 4 * lax.axis_index("x") + 2 * lax.axis_index("y") + lax.axis_index("c")
    me_s = me.astype(jnp.int32).reshape(1)
    chip_s = (2 * lax.axis_index("x") + lax.axis_index("y")).astype(jnp.int32).reshape(1)
    core_s = lax.axis_index("c").astype(jnp.int32).reshape(1)

    n_taps = DW_LEN + LRU_LEN
    taps = jnp.concatenate([dw_conv_w, lru_conv_w], axis=1).reshape(depth * n_taps, cc // N_DEV)
    taps = all_gather("ag_taps", taps, me_s)
    taps = jnp.moveaxis(taps.reshape(N_DEV, depth, n_taps, cc // N_DEV), 0, 2).reshape(depth, n_taps, cc)
    dw_full, lcw_full = taps[:, :DW_LEN], taps[:, DW_LEN:]

    def vec(a, l):
        return a[l].reshape(1, -1)

    ag_state, started = [], taps
    for l in range(depth):
        ag_state.append(gather_start(f"ag_start_{l}", [place_own(wts[n][l], me_s, BF16) for n in _BIG], started))
        started = ag_state[l][-1]
    started = started[0:1, 0:1]

    saved = []
    u1 = rms_pre(h, vec(g_pre_mix, 0) + started)
    loss_sum = dh = dbr = None
    for l in range(depth):
        n_first = 1 if l == 0 else len(_BIG)
        wg = dict(zip(_BIG, gather_finish("ag_finish_a", gather_wait(f"ag_wait_{l}a", ag_state[l], 0, n_first, u1))))
        wa_b, wi_b = lru_w_a[l].astype(BF16), lru_w_i[l].astype(BF16)
        proj = mm_proj(u1, wg["w_in"])
        y_attn = attn_fwd(proj, n_heads)
        cpre, y_conv = conv_fwd(proj, conv_col, cc, dw_full[l], vec(dw_conv_b, l), vec(conv_ln_g, l), vec(conv_ln_b, l))
        hs, y_lru = lru_fwd(proj, lru_col, wl, lcw_full[l], vec(lru_conv_b, l), wa_b, vec(lru_b_a, l), wi_b,
                            vec(lru_b_i, l), vec(lru_lambda, l))
        mixed = mix_fwd(y_attn, y_conv, y_lru, vec(g_attn_grp, l), vec(g_conv_grp, l), vec(g_lru_grp, l))
        if n_first < len(_BIG):
            rest = gather_wait(f"ag_wait_{l}b", ag_state[l], n_first, len(_BIG) - n_first, mixed)
            wg.update(zip(_BIG[n_first:], gather_finish("ag_finish_b", rest)))
        wg["w_out"] = wg["w_out"].reshape(attn_w + cc + wl, d)
        o = mm_plain("mm_out", mixed, wg["w_out"], "nn", F32)
        h2, u2 = res_norm(h, o, vec(g_post_mix, l), vec(g_pre_ffn, l))
        gt, up, f = ffn_up(u2, wg["w_gate"], wg["w_up"])
        dn = mm_down(f, wg["w_down"])
        saved.append(dict(wg=wg, wa_b=wa_b, wi_b=wi_b, h=h, u1=u1, proj=proj, y_attn=y_attn, cpre=cpre, y_conv=y_conv,
                          hs=hs, y_lru=y_lru, mixed=mixed, o=o, h2=h2, u2=u2, gt=gt, up=up, f=f, dn=dn))
        if l + 1 < depth:
            h, u1 = res_norm(h2, dn, vec(g_post_ffn, l), vec(g_pre_mix, l + 1))
        else:
            loss_sum, dh, dbr, dg_post_ffn = final_loss(h2, dn, vec(g_post_ffn, l), target)

    loss = lax.psum(0.5 * loss_sum[0, 0] / d, MESH_AXES)

    small = {n: [None] * depth for n in _SMALL}
    rs_state = {n: [None] * depth for n in _BIG}
    for l in reversed(range(depth)):
        s = saved[l]
        wg = s["wg"]
        small["g_post_ffn"][l] = dg_post_ffn
        dgt, dup = ffn_bwd(dbr, wg["w_down"], s["gt"], s["up"])
        rs_state["w_down"][l] = reduce_scatter_start(f"down_{l}", mm_dw_rows("mm_dw_down", s["f"], dbr), core_s)
        rs_state["w_gate"][l] = reduce_scatter_start(
            f"gate_{l}", mm_dw_cols_stacked("mm_dw_gate", s["u2"], dgt), core_s)
        rs_state["w_up"][l] = reduce_scatter_start(f"up_{l}", mm_dw_cols_stacked("mm_dw_gate", s["u2"], dup), core_s)
        du2 = mm_dx_ffn(dgt, wg["w_gate"], dup, wg["w_up"])
        started = sum(rs_state[n][l][-1][0:1, 0:1] for n in ("w_down", "w_gate", "w_up"))
        dh2, small["g_pre_ffn"][l], do, small["g_post_mix"][l] = norm_bwd(
            dh, du2, s["h2"], vec(g_pre_ffn, l) + started, (s["o"], vec(g_post_mix, l)))
        dmixed = mm_plain("mm_dmixed", do, wg["w_out"], "nt", F32)
        dw_out = mm_plain("mm_dw_out", s["mixed"], do, "tn", BF16)
        rs_state["w_out"][l] = reduce_scatter_start(f"out_{l}", dw_out.reshape(N_DEV, -1, d), core_s)
        (dya, dc, dyl, small["g_attn_grp"][l], small["g_conv_grp"][l], small["g_lru_grp"][l],
         small["conv_ln_g"][l], small["conv_ln_b"][l]) = mix_bwd(
            dmixed, s["y_attn"], s["y_conv"], s["y_lru"], s["cpre"],
            vec(g_attn_grp, l) + rs_state["w_out"][l][-1][0:1, 0:1], vec(g_conv_grp, l),
            vec(g_lru_grp, l), vec(conv_ln_g, l), vec(conv_ln_b, l))
        dq, dk, dv = attn_bwd(s["proj"], dya, n_heads)
        dvg, small["dw_conv_w"][l], small["dw_conv_b"][l] = conv_bwd(s["proj"], conv_col, cc, dc, dw_full[l])
        (dxy, small["lru_conv_w"][l], small["lru_conv_b"][l], small["lru_w_a"][l], small["lru_b_a"][l],
         small["lru_w_i"][l], small["lru_b_i"][l], small["lru_lambda"][l]) = lru_bwd(
            s["proj"], lru_col, wl, s["hs"], dyl, lcw_full[l], vec(lru_conv_b, l), s["wa_b"], vec(lru_b_a, l),
            s["wi_b"], vec(lru_b_i, l), vec(lru_lambda, l))
        dproj = jnp.concatenate([dq, dk, dv, dvg, dxy], axis=1)
        rs_state["w_in"][l] = reduce_scatter_start(f"in_{l}", mm_dw_cols("mm_dw_in", s["u1"], dproj, N_DEV), core_s)
        du1 = mm_dx_cols("mm_dx_in", dproj, wg["w_in"])
        g_pre = vec(g_pre_mix, l) + rs_state["w_in"][l][-1][0:1, 0:1]
        if l > 0:
            p = saved[l - 1]
            dh, small["g_pre_mix"][l], dbr, dg_post_ffn = norm_bwd(
                dh2, du1, s["h"], g_pre, (p["dn"], vec(g_post_ffn, l - 1)))
        else:
            dh, small["g_pre_mix"][l] = norm_bwd(dh2, du1, s["h"], g_pre)
    grad_x = dh[None]
    big_parts = {n: [reduce_scatter_wait(f"{n[2:]}_{l}", rs_state[n][l], dh) for l in range(depth)] for n in _BIG}

    small_shapes = [(depth,) + tuple(wts[n].shape[1:]) if n not in _COL_SHARDED_SMALL
                    else (depth, wts[n].shape[1], cc) for n in _SMALL]
    part = _pack([jnp.stack([a.reshape(shp[1:]) for a in small[n]]) for n, shp in zip(_SMALL, small_shapes)])
    g_small = _unpack(sum_parts(all_gather("ag_small", part, me_s)), small_shapes)
    grads = {}
    for n, g in zip(_SMALL, g_small):
        if n in _COL_SHARDED_SMALL:
            g = lax.dynamic_slice_in_dim(g, me * (cc // N_DEV), cc // N_DEV, axis=2)
        grads[n] = g
    local_shapes = [tuple(wts[n].shape) for n in _SMALL]
    d_small, m_small, v_small = adam_small(
        _pack([wts[n] for n in _SMALL]), _pack([mom[n] for n in _SMALL]), _pack([var[n] for n in _SMALL]),
        _pack([grads[n] for n in _SMALL]))
    delta = dict(zip(_SMALL, _unpack(d_small, local_shapes)))
    new_m = dict(zip(_SMALL, _unpack(m_small, local_shapes)))
    new_v = dict(zip(_SMALL, _unpack(v_small, local_shapes)))

    for n in _BIG:
        shape = wts[n].shape
        _, rows, cols = big_parts[n][0][0].shape
        view = (depth, rows, cols)
        g, dl, mn, vn = adam_big("adam_" + n, wts[n].reshape(view), mom[n].reshape(view), var[n].reshape(view),
                                 big_parts[n], chip_s)
        grads[n], delta[n], new_m[n], new_v[n] = (a.reshape(shape) for a in (g, dl, mn, vn))

    return (loss, grad_x, *[grads[n] for n in _ALL], *[delta[n] for n in _ALL],
            *[new_m[n] for n in _ALL], *[new_v[n] for n in _ALL])
```

```python
import functools
import math

import jax
import jax.numpy as jnp
from jax import lax
from jax.experimental import pallas as pl
from jax.experimental.pallas import tpu as pltpu

F32 = jnp.float32
BF16 = jnp.bfloat16

N_DEV = 8
EPS = 1e-6
HEAD_DIM = 128
DW_LEN = 31
LRU_LEN = 4
LRU_BLOCKS = 4
LRU_C = 8.0
ATT_T = 256
ROW_T = 256
CONV_HALO = 32
LRU_HALO = 8
LANE = 128
SUBLANE = 8
PACK_ROWS = 512
VMEM_LIMIT = 56 * 1024 * 1024

ADAM_LR = 0.001
ADAM_B1 = 0.9
ADAM_B2 = 0.999
ADAM_EPS = 1e-08
ADAM_WD = 0.01
ADAM_STEP = 10

MESH_AXES = ("x", "y", "c")
_DIMS = {
    "nn": (((1,), (0,)), ((), ())),
    "nt": (((1,), (1,)), ((), ())),
    "tn": (((0,), (0,)), ((), ())),
}


def _params(n_axes):
    return pltpu.CompilerParams(
        dimension_semantics=("arbitrary",) * n_axes, vmem_limit_bytes=VMEM_LIMIT)


def _dot(a, b, mode="nn"):
    return lax.dot_general(a, b, _DIMS[mode], preferred_element_type=F32)


def _sigmoid(x):
    return 1.0 / (1.0 + jnp.exp(-x))


def _softplus(x):
    return jnp.maximum(x, 0.0) + jnp.log(1.0 + jnp.exp(-jnp.abs(x)))


def _neg_expm1(x):
    series = x * (1.0 + x * (0.5 + x * (1.0 / 6 + x * (1.0 / 24 + x * (1.0 / 120 + x * (1.0 / 720))))))
    return jnp.where(x > -0.25, -series, 1.0 - jnp.exp(x))


_GELU_C = math.sqrt(2.0 / math.pi)


def _gelu_and_grad(x):
    inner = _GELU_C * (x + 0.044715 * x * x * x)
    t = jnp.tanh(inner)
    val = 0.5 * x * (1.0 + t)
    grad = 0.5 * (1.0 + t) + 0.5 * x * (1.0 - t * t) * _GELU_C * (1.0 + 3 * 0.044715 * x * x)
    return val, grad


def _rms_stats(x):
    r = lax.rsqrt(jnp.mean(x * x, axis=-1, keepdims=True) + EPS)
    return x * r, r


def _rms_bwd(dy, x, g):
    xn, r = _rms_stats(x)
    dxn = dy * g
    dx = r * (dxn - xn * jnp.mean(dxn * xn, axis=-1, keepdims=True))
    return dx, jnp.sum(dy * xn, axis=0, keepdims=True)


def _row_spec(tr, width, col=0):
    return pl.BlockSpec((tr, width), lambda i, col=col: (i, col))


def _vec_spec(width):
    return pl.BlockSpec((1, width), lambda i: (0, 0))


def _matmul(name, mode, operands, in_specs, out_shape, out_spec, grid, out_block):
    npairs = len(operands) // 2
    nk = grid[2]

    def body(*refs):
        o_ref = refs[2 * npairs]

        def partial():
            acc = None
            for p in range(npairs):
                d = _dot(refs[2 * p][...], refs[2 * p + 1][...], mode)
                acc = d if acc is None else acc + d
            return acc

        if nk == 1:
            o_ref[...] = partial().astype(o_ref.dtype)
        else:
            acc_ref = refs[2 * npairs + 1]
            k = pl.program_id(2)

            @pl.when(k == 0)
            def _():
                acc_ref[...] = jnp.zeros_like(acc_ref)

            acc_ref[...] += partial()

            @pl.when(k == nk - 1)
            def _():
                o_ref[...] = acc_ref[...].astype(o_ref.dtype)

    return pl.pallas_call(
        body, name=name, grid=grid, in_specs=in_specs, out_specs=out_spec, out_shape=out_shape,
        scratch_shapes=[] if nk == 1 else [pltpu.VMEM(out_block, F32)],
        compiler_params=_params(3),
    )(*operands)


def _tile(n, t):
    if n <= t:
        return n
    return max(k for k in range(SUBLANE, t + 1, SUBLANE) if n % k == 0)


def mm_proj(u, w):
    t, d = u.shape
    nblk, _, nb = w.shape
    tm = _tile(t, 1024)
    return _matmul(
        "mm_proj", "nn", (u, w),
        [pl.BlockSpec((tm, d), lambda j, i, k: (i, 0)), pl.BlockSpec((None, d, nb), lambda j, i, k: (j, 0, 0))],
        jax.ShapeDtypeStruct((t, nblk * nb), F32), pl.BlockSpec((tm, nb), lambda j, i, k: (i, j)),
        (nblk, t // tm, 1), (tm, nb))


def mm_plain(name, a, b, mode, out_dtype):
    if mode == "nn":
        (m, kk), n = a.shape, b.shape[1]
    elif mode == "nt":
        (m, kk), n = a.shape, b.shape[0]
    else:
        (kk, m), n = a.shape, b.shape[1]
    tm, tn = _tile(m, 1024), _tile(n, 1024)
    a_spec = (pl.BlockSpec((kk, tm), lambda i, j, k: (0, i)) if mode == "tn"
              else pl.BlockSpec((tm, kk), lambda i, j, k: (i, 0)))
    b_spec = (pl.BlockSpec((tn, kk), lambda i, j, k: (j, 0)) if mode == "nt"
              else pl.BlockSpec((kk, tn), lambda i, j, k: (0, j)))
    return _matmul(
        name, mode, (a, b), [a_spec, b_spec],
        jax.ShapeDtypeStruct((m, n), out_dtype), pl.BlockSpec((tm, tn), lambda i, j, k: (i, j)),
        (m // tm, n // tn, 1), (tm, tn))


def mm_down(f, w):
    nblk, t, fb = f.shape
    d = w.shape[2]
    tm, tn = _tile(t, 1024), _tile(d, 1024)
    return _matmul(
        "mm_down", "nn", (f, w),
        [pl.BlockSpec((None, tm, fb), lambda i, j, k: (k, i, 0)), pl.BlockSpec((None, fb, tn), lambda i, j, k: (k, 0, j))],
        jax.ShapeDtypeStruct((t, d), F32), pl.BlockSpec((tm, tn), lambda i, j, k: (i, j)),
        (t // tm, d // tn, nblk), (tm, tn))


def mm_dw_rows(name, a, g):
    nblk, t, fb = a.shape
    d = g.shape[1]
    tn = _tile(d, 1024)
    return _matmul(
        name, "tn", (a, g),
        [pl.BlockSpec((None, t, fb), lambda j, n, k: (j, 0, 0)), pl.BlockSpec((t, tn), lambda j, n, k: (0, n))],
        jax.ShapeDtypeStruct((nblk, fb, d), BF16), pl.BlockSpec((None, fb, tn), lambda j, n, k: (j, 0, n)),
        (nblk, d // tn, 1), (fb, tn))


def mm_dw_cols(name, u, g, nblk):
    t, d = u.shape
    nb = g.shape[1] // nblk
    tmd = _tile(d, 1024)
    return _matmul(
        name, "tn", (u, g),
        [pl.BlockSpec((t, tmd), lambda j, i, k: (0, i)), pl.BlockSpec((t, nb), lambda j, i, k: (0, j))],
        jax.ShapeDtypeStruct((nblk, d, nb), BF16), pl.BlockSpec((None, tmd, nb), lambda j, i, k: (j, i, 0)),
        (nblk, d // tmd, 1), (tmd, nb))


def mm_dx_cols(name, g, w):
    t = g.shape[0]
    nblk, d, nb = w.shape
    tm, tn = _tile(t, 1024), _tile(d, 1024)
    return _matmul(
        name, "nt", (g, w),
        [pl.BlockSpec((tm, nb), lambda i, j, k: (i, k)), pl.BlockSpec((None, tn, nb), lambda i, j, k: (k, j, 0))],
        jax.ShapeDtypeStruct((t, d), F32), pl.BlockSpec((tm, tn), lambda i, j, k: (i, j)),
        (t // tm, d // tn, nblk), (tm, tn))


def mm_dx_ffn(dgt, wg, dup, wu):
    nblk, t, fb = dgt.shape
    d = wg.shape[2]
    tm, tn = _tile(t, 1024), _tile(d, 1024)
    a_spec = pl.BlockSpec((None, tm, fb), lambda i, j, k: (k, i, 0))
    b_spec = pl.BlockSpec((None, fb, tn), lambda i, j, k: (k, 0, j))
    return _matmul(
        "mm_dx_ffn", "nn", (dgt, wg, dup, wu), [a_spec, b_spec, a_spec, b_spec],
        jax.ShapeDtypeStruct((t, d), F32), pl.BlockSpec((tm, tn), lambda i, j, k: (i, j)),
        (t // tm, d // tn, nblk), (tm, tn))


def ffn_up(u, wg, wu):
    t, d = u.shape
    nblk, fb, _ = wg.shape
    tm = _tile(t, 512)

    def body(u_ref, wg_ref, wu_ref, gt_ref, up_ref, f_ref):
        uu = u_ref[...]
        gt = _dot(uu, wg_ref[...], "nt")
        up = _dot(uu, wu_ref[...], "nt")
        gt_ref[...] = gt
        up_ref[...] = up
        f_ref[...] = (gt * _sigmoid(gt) * up).astype(BF16)

    w_spec = pl.BlockSpec((None, fb, d), lambda j, i: (j, 0, 0))
    o_spec = pl.BlockSpec((None, tm, fb), lambda j, i: (j, i, 0))
    return pl.pallas_call(
        body, name="ffn_up", grid=(nblk, t // tm),
        in_specs=[pl.BlockSpec((tm, d), lambda j, i: (i, 0)), w_spec, w_spec],
        out_specs=[o_spec, o_spec, o_spec],
        out_shape=[jax.ShapeDtypeStruct((nblk, t, fb), F32), jax.ShapeDtypeStruct((nblk, t, fb), F32),
                   jax.ShapeDtypeStruct((nblk, t, fb), BF16)],
        compiler_params=_params(2),
    )(u, wg, wu)


def ffn_bwd(dd, wd, gt, up):
    t, d = dd.shape
    nblk, fb, _ = wd.shape
    tm = _tile(t, 512)

    def body(dd_ref, wd_ref, gt_ref, up_ref, dgt_ref, dup_ref):
        df = _dot(dd_ref[...], wd_ref[...], "nt")
        g = gt_ref[...]
        s = _sigmoid(g)
        dgt_ref[...] = (df * up_ref[...] * s * (1.0 + g * (1.0 - s))).astype(BF16)
        dup_ref[...] = (df * g * s).astype(BF16)

    s_spec = pl.BlockSpec((None, tm, fb), lambda j, i: (j, i, 0))
    return pl.pallas_call(
        body, name="ffn_bwd", grid=(nblk, t // tm),
        in_specs=[pl.BlockSpec((tm, d), lambda j, i: (i, 0)), pl.BlockSpec((None, fb, d), lambda j, i: (j, 0, 0)),
                  s_spec, s_spec],
        out_specs=[s_spec, s_spec],
        out_shape=[jax.ShapeDtypeStruct((nblk, t, fb), BF16)] * 2,
        compiler_params=_params(2),
    )(dd, wd, gt, up)


def rms_pre(h, g):
    t, d = h.shape
    tr = _tile(t, ROW_T)

    def body(h_ref, g_ref, o_ref):
        xn, _ = _rms_stats(h_ref[...])
        o_ref[...] = (xn * g_ref[...]).astype(BF16)

    return pl.pallas_call(
        body, name="rms_pre", grid=(t // tr,),
        in_specs=[_row_spec(tr, d), _vec_spec(d)], out_specs=_row_spec(tr, d),
        out_shape=jax.ShapeDtypeStruct((t, d), BF16), compiler_params=_params(1),
    )(h, g)


def res_norm(h, o, g_post, g_pre):
    t, d = h.shape
    tr = _tile(t, ROW_T)

    def body(h_ref, o_ref, gpo_ref, gpr_ref, h2_ref, u_ref):
        on, _ = _rms_stats(o_ref[...])
        h2 = h_ref[...] + on * gpo_ref[...]
        h2_ref[...] = h2
        hn, _ = _rms_stats(h2)
        u_ref[...] = (hn * gpr_ref[...]).astype(BF16)

    return pl.pallas_call(
        body, name="res_norm", grid=(t // tr,),
        in_specs=[_row_spec(tr, d), _row_spec(tr, d), _vec_spec(d), _vec_spec(d)],
        out_specs=[_row_spec(tr, d), _row_spec(tr, d)],
        out_shape=[jax.ShapeDtypeStruct((t, d), F32), jax.ShapeDtypeStruct((t, d), BF16)],
        compiler_params=_params(1),
    )(h, o, g_post, g_pre)


def final_loss(h2, dbr, g_post, target):
    t, d = h2.shape
    tr = _tile(t, ROW_T)

    def body(h2_ref, d_ref, g_ref, tg_ref, loss_ref, dy_ref, dd_ref, dg_ref):
        i = pl.program_id(0)

        @pl.when(i == 0)
        def _():
            loss_ref[...] = jnp.zeros_like(loss_ref)
            dg_ref[...] = jnp.zeros_like(dg_ref)

        x = d_ref[...]
        g = g_ref[...]
        xn, _ = _rms_stats(x)
        diff = h2_ref[...] + xn * g - tg_ref[...]
        loss_ref[...] += jnp.sum(jnp.sum(diff * diff, axis=1, keepdims=True), axis=0, keepdims=True)
        dy = diff * (1.0 / d)
        dy_ref[...] = dy
        dx, dg = _rms_bwd(dy, x, g)
        dd_ref[...] = dx.astype(BF16)
        dg_ref[...] += dg

    return pl.pallas_call(
        body, name="final_loss", grid=(t // tr,),
        in_specs=[_row_spec(tr, d), _row_spec(tr, d), _vec_spec(d), _row_spec(tr, d)],
        out_specs=[pl.BlockSpec((1, 1), lambda i: (0, 0)), _row_spec(tr, d), _row_spec(tr, d), _vec_spec(d)],
        out_shape=[jax.ShapeDtypeStruct((1, 1), F32), jax.ShapeDtypeStruct((t, d), F32),
                   jax.ShapeDtypeStruct((t, d), BF16), jax.ShapeDtypeStruct((1, d), F32)],
        compiler_params=_params(1),
    )(h2, dbr, g_post, target)


def norm_bwd(dh_out, du, h_in, g_pre, prev=None):
    t, d = h_in.shape
    tr = _tile(t, ROW_T)
    with_prev = prev is not None

    def body(*refs):
        if with_prev:
            dho_ref, du_ref, h_ref, gpr_ref, br_ref, gpo_ref, dh_ref, dgpr_ref, dbr_ref, dgpo_ref = refs
        else:
            dho_ref, du_ref, h_ref, gpr_ref, dh_ref, dgpr_ref = refs
        i = pl.program_id(0)

        @pl.when(i == 0)
        def _():
            dgpr_ref[...] = jnp.zeros_like(dgpr_ref)
            if with_prev:
                dgpo_ref[...] = jnp.zeros_like(dgpo_ref)

        dx, dg = _rms_bwd(du_ref[...], h_ref[...], gpr_ref[...])
        dh = dho_ref[...] + dx
        dh_ref[...] = dh
        dgpr_ref[...] += dg
        if with_prev:
            dbr, dg2 = _rms_bwd(dh, br_ref[...], gpo_ref[...])
            dbr_ref[...] = dbr.astype(BF16)
            dgpo_ref[...] += dg2

    row, vec = _row_spec(tr, d), _vec_spec(d)
    in_specs = [row, row, row, vec] + ([row, vec] if with_prev else [])
    out_specs = [row, vec] + ([row, vec] if with_prev else [])
    out_shape = [jax.ShapeDtypeStruct((t, d), F32), jax.ShapeDtypeStruct((1, d), F32)]
    if with_prev:
        out_shape += [jax.ShapeDtypeStruct((t, d), BF16), jax.ShapeDtypeStruct((1, d), F32)]
    args = (dh_out, du, h_in, g_pre) + (tuple(prev) if with_prev else ())
    return pl.pallas_call(
        body, name="norm_bwd_chain" if with_prev else "norm_bwd_first", grid=(t // tr,),
        in_specs=in_specs, out_specs=out_specs, out_shape=out_shape, compiler_params=_params(1),
    )(*args)


def mix_fwd(ya, yc, yl, ga, gc, gl):
    t, wa = ya.shape
    wc, wl = yc.shape[1], yl.shape[1]
    tr = _tile(t, ROW_T)

    def body(ya_ref, yc_ref, yl_ref, ga_ref, gc_ref, gl_ref, o_ref):
        o_ref[:, pl.ds(0, wa)] = (_rms_stats(ya_ref[...])[0] * ga_ref[...]).astype(BF16)
        o_ref[:, pl.ds(wa, wc)] = (_rms_stats(yc_ref[...])[0] * gc_ref[...]).astype(BF16)
        o_ref[:, pl.ds(wa + wc, wl)] = (_rms_stats(yl_ref[...])[0] * gl_ref[...]).astype(BF16)

    return pl.pallas_call(
        body, name="mix_fwd", grid=(t // tr,),
        in_specs=[_row_spec(tr, wa), _row_spec(tr, wc), _row_spec(tr, wl), _vec_spec(wa), _vec_spec(wc), _vec_spec(wl)],
        out_specs=_row_spec(tr, wa + wc + wl),
        out_shape=jax.ShapeDtypeStruct((t, wa + wc + wl), BF16), compiler_params=_params(1),
    )(ya, yc, yl, ga, gc, gl)


def mix_bwd(dmixed, ya, yc, yl, cpre, ga, gc, gl, lng, lnb):
    t, wa = ya.shape
    wc, wl = yc.shape[1], yl.shape[1]
    tr = _tile(t, ROW_T)

    def body(dm_ref, ya_ref, yc_ref, yl_ref, c_ref, ga_ref, gc_ref, gl_ref, lg_ref, lb_ref,
             dya_ref, dc_ref, dyl_ref, dga_ref, dgc_ref, dgl_ref, dlg_ref, dlb_ref):
        i = pl.program_id(0)

        @pl.when(i == 0)
        def _():
            for r in (dga_ref, dgc_ref, dgl_ref, dlg_ref, dlb_ref):
                r[...] = jnp.zeros_like(r)

        dya, dga = _rms_bwd(dm_ref[:, pl.ds(0, wa)], ya_ref[...], ga_ref[...])
        dya_ref[...] = dya
        dga_ref[...] += dga
        dyl, dgl = _rms_bwd(dm_ref[:, pl.ds(wa + wc, wl)], yl_ref[...], gl_ref[...])
        dyl_ref[...] = dyl
        dgl_ref[...] += dgl
        dyc, dgc = _rms_bwd(dm_ref[:, pl.ds(wa, wc)], yc_ref[...], gc_ref[...])
        dgc_ref[...] += dgc
        c = c_ref[...]
        xc = c - jnp.mean(c, axis=-1, keepdims=True)
        rstd = lax.rsqrt(jnp.mean(xc * xc, axis=-1, keepdims=True) + EPS)
        xhat = xc * rstd
        ln = xhat * lg_ref[...] + lb_ref[...]
        s = _sigmoid(ln)
        dln = dyc * s * (1.0 + ln * (1.0 - s))
        dlg_ref[...] += jnp.sum(dln * xhat, axis=0, keepdims=True)
        dlb_ref[...] += jnp.sum(dln, axis=0, keepdims=True)
        dxh = dln * lg_ref[...]
        dc_ref[...] = rstd * (dxh - jnp.mean(dxh, axis=-1, keepdims=True)
                              - xhat * jnp.mean(dxh * xhat, axis=-1, keepdims=True))

    return pl.pallas_call(
        body, name="mix_bwd", grid=(t // tr,),
        in_specs=[_row_spec(tr, wa + wc + wl), _row_spec(tr, wa), _row_spec(tr, wc), _row_spec(tr, wl), _row_spec(tr, wc),
                  _vec_spec(wa), _vec_spec(wc), _vec_spec(wl), _vec_spec(wc), _vec_spec(wc)],
        out_specs=[_row_spec(tr, wa), _row_spec(tr, wc), _row_spec(tr, wl),
                   _vec_spec(wa), _vec_spec(wc), _vec_spec(wl), _vec_spec(wc), _vec_spec(wc)],
        out_shape=[jax.ShapeDtypeStruct((t, wa), F32), jax.ShapeDtypeStruct((t, wc), F32), jax.ShapeDtypeStruct((t, wl), F32),
                   jax.ShapeDtypeStruct((1, wa), F32), jax.ShapeDtypeStruct((1, wc), F32), jax.ShapeDtypeStruct((1, wl), F32),
                   jax.ShapeDtypeStruct((1, wc), F32), jax.ShapeDtypeStruct((1, wc), F32)],
        compiler_params=_params(1),
    )(dmixed, ya, yc, yl, cpre, ga, gc, gl, lng, lnb)


def _hi_lo(x):
    hi = x.astype(BF16)
    return hi, (x - hi.astype(F32)).astype(BF16)


def _att_block(qb, kt, jj, scale, lower, tri_gt):
    z = _dot(qb, kt, "nt") * scale
    sp = _softplus(z)
    mask = jnp.logical_or(jj > 0, lower)
    lk = jnp.where(mask, -sp, 0.0)
    hi, lo = _hi_lo(lk)
    logw = (z - sp) + _dot(hi, tri_gt) + _dot(lo, tri_gt)
    return z, sp, mask, lk, logw


def attn_fwd(proj, n_heads):
    t = proj.shape[0]
    tb = _tile(t, ATT_T)
    nq = t // tb
    scale = HEAD_DIM ** -0.5

    def body(q_ref, k_ref, v_ref, o_ref, kb_ref, vb_ref):
        kb_ref[...] = k_ref[...].astype(BF16)
        vb_ref[...] = v_ref[...].astype(BF16)
        row = lax.broadcasted_iota(jnp.int32, (tb, tb), 0)
        col = lax.broadcasted_iota(jnp.int32, (tb, tb), 1)
        lower = col < row
        tri_gt = (row > col).astype(BF16)

        def qblock(i, _):
            q0 = pl.multiple_of(i * tb, tb)
            qb = q_ref[pl.ds(q0, tb), :].astype(BF16)

            def kblock(jj, carry):
                acc, run = carry
                k0 = pl.multiple_of((i - jj) * tb, tb)
                _, _, mask, lk, logw = _att_block(qb, kb_ref[pl.ds(k0, tb), :], jj, scale, lower, tri_gt)
                w = jnp.where(mask, jnp.exp(logw + run), 0.0)
                acc = acc + _dot(w.astype(BF16), vb_ref[pl.ds(k0, tb), :])
                return acc, run + jnp.sum(lk, axis=1, keepdims=True)

            acc, _ = lax.fori_loop(0, i + 1, kblock, (jnp.zeros((tb, HEAD_DIM), F32), jnp.zeros((tb, 1), F32)))
            o_ref[pl.ds(q0, tb), :] = acc
            return 0

        lax.fori_loop(0, nq, qblock, 0)

    def col_spec(base):
        return pl.BlockSpec((t, HEAD_DIM), lambda h, base=base: (0, base + h))

    return pl.pallas_call(
        body, name="attn_fwd", grid=(n_heads,),
        in_specs=[col_spec(0), col_spec(n_heads), col_spec(2 * n_heads)],
        out_specs=col_spec(0),
        out_shape=jax.ShapeDtypeStruct((t, n_heads * HEAD_DIM), F32),
        scratch_shapes=[pltpu.VMEM((t, HEAD_DIM), BF16), pltpu.VMEM((t, HEAD_DIM), BF16)],
        compiler_params=_params(1),
    )(proj, proj, proj)


def attn_bwd(proj, dy, n_heads):
    t = proj.shape[0]
    tb = _tile(t, ATT_T)
    nq = t // tb
    scale = HEAD_DIM ** -0.5

    def body(q_ref, k_ref, v_ref, dy_ref, dq_ref, dk_ref, dv_ref, qb_ref, kb_ref, vb_ref, dob_ref, dk_acc, dv_acc, run_s):
        qb_ref[...] = q_ref[...].astype(BF16)
        kb_ref[...] = k_ref[...].astype(BF16)
        vb_ref[...] = v_ref[...].astype(BF16)
        dob_ref[...] = dy_ref[...].astype(BF16)
        dk_acc[...] = jnp.zeros_like(dk_acc)
        dv_acc[...] = jnp.zeros_like(dv_acc)
        row = lax.broadcasted_iota(jnp.int32, (tb, tb), 0)
        col = lax.broadcasted_iota(jnp.int32, (tb, tb), 1)
        lower = col < row
        tri_gt = (row > col).astype(BF16)
        tri_lt = (row < col).astype(BF16)

        def qblock(i, _):
            q0 = pl.multiple_of(i * tb, tb)
            qb = qb_ref[pl.ds(q0, tb), :]
            dob = dob_ref[pl.ds(q0, tb), :]

            def sweep(jj, run):
                k0 = pl.multiple_of((i - jj) * tb, tb)
                sp = _softplus(_dot(qb, kb_ref[pl.ds(k0, tb), :], "nt") * scale)
                run_s[i - jj] = run
                return run + jnp.sum(jnp.where(jnp.logical_or(jj > 0, lower), -sp, 0.0), axis=1, keepdims=True)

            lax.fori_loop(0, i + 1, sweep, jnp.zeros((tb, 1), F32))

            def kblock(kbi, carry):
                dq, gsum = carry
                k0 = pl.multiple_of(kbi * tb, tb)
                kt = kb_ref[pl.ds(k0, tb), :]
                vt = vb_ref[pl.ds(k0, tb), :]
                z, sp, mask, _, logw = _att_block(qb, kt, i - kbi, scale, lower, tri_gt)
                w = jnp.where(mask, jnp.exp(logw + run_s[kbi]), 0.0)
                g = w * _dot(dob, vt, "nt")
                hi, lo = _hi_lo(g)
                before = _dot(hi, tri_lt) + _dot(lo, tri_lt) + gsum
                sig = jnp.exp(z - sp)
                dz = jnp.where(mask, g * (1.0 - sig) - before * sig, 0.0) * scale
                dzb = dz.astype(BF16)
                dk_acc[pl.ds(k0, tb), :] += _dot(dzb, qb, "tn")
                dv_acc[pl.ds(k0, tb), :] += _dot(w.astype(BF16), dob, "tn")
                return dq + _dot(dzb, kt), gsum + jnp.sum(g, axis=1, keepdims=True)

            dq, _ = lax.fori_loop(0, i + 1, kblock, (jnp.zeros((tb, HEAD_DIM), F32), jnp.zeros((tb, 1), F32)))
            dq_ref[pl.ds(q0, tb), :] = dq.astype(BF16)
            return 0

        lax.fori_loop(0, nq, qblock, 0)
        dk_ref[...] = dk_acc[...].astype(BF16)
        dv_ref[...] = dv_acc[...].astype(BF16)

    def col_spec(base):
        return pl.BlockSpec((t, HEAD_DIM), lambda h, base=base: (0, base + h))

    width = n_heads * HEAD_DIM
    return pl.pallas_call(
        body, name="attn_bwd", grid=(n_heads,),
        in_specs=[col_spec(0), col_spec(n_heads), col_spec(2 * n_heads), col_spec(0)],
        out_specs=[col_spec(0), col_spec(0), col_spec(0)],
        out_shape=[jax.ShapeDtypeStruct((t, width), BF16)] * 3,
        scratch_shapes=[pltpu.VMEM((t, HEAD_DIM), BF16)] * 4 + [pltpu.VMEM((t, HEAD_DIM), F32)] * 2
        + [pltpu.VMEM((nq, tb, 1), F32)],
        compiler_params=_params(1),
    )(proj, proj, proj, dy)


def _glu_halo(vc, gc, vp, gp, ubuf, i, tt, halo):
    uprev = vp[pl.ds(tt - halo, halo), :] * _sigmoid(gp[pl.ds(tt - halo, halo), :])
    ubuf[pl.ds(0, halo), :] = jnp.where(i > 0, uprev, 0.0)
    ubuf[pl.ds(halo, tt), :] = vc[...] * _sigmoid(gc[...])


def conv_fwd(proj, col0, cc, w, b, lng, lnb):
    t = proj.shape[0]
    tt = _tile(t, ROW_T)
    vi, gi = col0 // cc, col0 // cc + 1
    off = CONV_HALO - (DW_LEN - 1)

    def body(vc, gc, vp, gp, w_ref, b_ref, lg_ref, lb_ref, c_ref, y_ref, ubuf):
        i = pl.program_id(0)
        _glu_halo(vc, gc, vp, gp, ubuf, i, tt, CONV_HALO)
        for ch in range(cc // LANE):
            sl = pl.ds(ch * LANE, LANE)
            acc = jnp.zeros((tt, LANE), F32) + b_ref[:, sl]
            for tap in range(DW_LEN):
                acc = acc + w_ref[pl.ds(tap, 1), sl] * ubuf[pl.ds(off + tap, tt), sl]
            c_ref[:, sl] = acc
        c = c_ref[...]
        xc = c - jnp.mean(c, axis=-1, keepdims=True)
        ln = xc * lax.rsqrt(jnp.mean(xc * xc, axis=-1, keepdims=True) + EPS) * lg_ref[...] + lb_ref[...]
        y_ref[...] = ln * _sigmoid(ln)

    cur = lambda c: pl.BlockSpec((tt, cc), lambda i, c=c: (i, c))
    prev = lambda c: pl.BlockSpec((tt, cc), lambda i, c=c: (jnp.maximum(i - 1, 0), c))
    return pl.pallas_call(
        body, name="conv_fwd", grid=(t // tt,),
        in_specs=[cur(vi), cur(gi), prev(vi), prev(gi), pl.BlockSpec((DW_LEN, cc), lambda i: (0, 0)),
                  _vec_spec(cc), _vec_spec(cc), _vec_spec(cc)],
        out_specs=[_row_spec(tt, cc), _row_spec(tt, cc)],
        out_shape=[jax.ShapeDtypeStruct((t, cc), F32)] * 2,
        scratch_shapes=[pltpu.VMEM((CONV_HALO + tt, cc), F32)],
        compiler_params=_params(1),
    )(proj, proj, proj, proj, w, b, lng, lnb)


def conv_bwd(proj, col0, cc, dc, w):
    t = proj.shape[0]
    tt = _tile(t, ROW_T)
    nt = t // tt
    vi, gi = col0 // cc, col0 // cc + 1
    off = CONV_HALO - (DW_LEN - 1)

    def body(vc, gc, vp, gp, dcc, dcn, w_ref, dvg_ref, dw_ref, db_ref, ubuf, dbuf):
        i = pl.program_id(0)

        @pl.when(i == 0)
        def _():
            dw_ref[...] = jnp.zeros_like(dw_ref)
            db_ref[...] = jnp.zeros_like(db_ref)

        _glu_halo(vc, gc, vp, gp, ubuf, i, tt, CONV_HALO)
        dbuf[pl.ds(0, tt), :] = dcc[...]
        dbuf[pl.ds(tt, CONV_HALO), :] = jnp.where(i < nt - 1, dcn[pl.ds(0, CONV_HALO), :], 0.0)
        db_ref[...] += jnp.sum(dcc[...], axis=0, keepdims=True)
        for ch in range(cc // LANE):
            sl = pl.ds(ch * LANE, LANE)
            dcv = dbuf[pl.ds(0, tt), sl]
            du = jnp.zeros((tt, LANE), F32)
            for tap in range(DW_LEN):
                du = du + w_ref[pl.ds(tap, 1), sl] * dbuf[pl.ds(DW_LEN - 1 - tap, tt), sl]
                dw_ref[pl.ds(tap, 1), sl] += jnp.sum(dcv * ubuf[pl.ds(off + tap, tt), sl], axis=0, keepdims=True)
            s = _sigmoid(gc[:, sl])
            val = vc[:, sl]
            dvg_ref[:, sl] = (du * s).astype(BF16)
            dvg_ref[:, pl.ds(cc + ch * LANE, LANE)] = (du * val * s * (1.0 - s)).astype(BF16)

    cur = lambda c: pl.BlockSpec((tt, cc), lambda i, c=c: (i, c))
    prev = lambda c: pl.BlockSpec((tt, cc), lambda i, c=c: (jnp.maximum(i - 1, 0), c))
    return pl.pallas_call(
        body, name="conv_bwd", grid=(nt,),
        in_specs=[cur(vi), cur(gi), prev(vi), prev(gi), _row_spec(tt, cc),
                  pl.BlockSpec((tt, cc), lambda i: (jnp.minimum(i + 1, nt - 1), 0)),
                  pl.BlockSpec((DW_LEN, cc), lambda i: (0, 0))],
        out_specs=[_row_spec(tt, 2 * cc), pl.BlockSpec((DW_LEN, cc), lambda i: (0, 0)), _vec_spec(cc)],
        out_shape=[jax.ShapeDtypeStruct((t, 2 * cc), BF16), jax.ShapeDtypeStruct((DW_LEN, cc), F32),
                   jax.ShapeDtypeStruct((1, cc), F32)],
        scratch_shapes=[pltpu.VMEM((CONV_HALO + tt, cc), F32), pltpu.VMEM((tt + CONV_HALO, cc), F32)],
        compiler_params=_params(1),
    )(proj, proj, proj, proj, dc, dc, w)


def _lru_gates(xbuf, cw_ref, cb_ref, wa_ref, ba_ref, wi_ref, bi_ref, lam_ref, tt, wl):
    bd = wl // LRU_BLOCKS
    xr = jnp.zeros((tt, wl), F32) + cb_ref[...]
    for tap in range(LRU_LEN):
        xr = xr + cw_ref[pl.ds(tap, 1), :] * xbuf[pl.ds(LRU_HALO - (LRU_LEN - 1) + tap, tt), :]
    xb = xr.astype(BF16)
    ga = jnp.concatenate([_dot(xb[:, n * bd:(n + 1) * bd], wa_ref[n]) for n in range(LRU_BLOCKS)], axis=1) + ba_ref[...]
    gi = jnp.concatenate([_dot(xb[:, n * bd:(n + 1) * bd], wi_ref[n]) for n in range(LRU_BLOCKS)], axis=1) + bi_ref[...]
    r = _sigmoid(ga)
    ig = _sigmoid(gi)
    spl = _softplus(-lam_ref[...])
    log_a = -LRU_C * r * spl
    a = jnp.exp(log_a)
    m = jnp.sqrt(_neg_expm1(2.0 * log_a))
    return xr, xb, r, ig, spl, a, m


def _group_scan(a8, b8, reverse):
    rid = lax.broadcasted_iota(jnp.int32, a8.shape, 0)
    aa, bb = a8, b8
    for dist in (1, 2, 4):
        shift = SUBLANE - dist if reverse else dist
        a_sh = pltpu.roll(aa, shift, 0)
        b_sh = pltpu.roll(bb, shift, 0)
        valid = (rid < SUBLANE - dist) if reverse else (rid >= dist)
        bb = jnp.where(valid, aa * b_sh + bb, bb)
        aa = jnp.where(valid, aa * a_sh, aa)
    return aa, bb


def _pick_row(x8, r):
    rid = lax.broadcasted_iota(jnp.int32, x8.shape, 0)
    return jnp.sum(jnp.where(rid == r, x8, 0.0), axis=0, keepdims=True)


def lru_fwd(proj, col0, wl, cw, cb, wa, ba, wi, bi, lam):
    t = proj.shape[0]
    tt = _tile(t, ROW_T)
    xi, yi = col0 // wl, col0 // wl + 1

    def body(xc, xp, ry, cw_ref, cb_ref, wa_ref, ba_ref, wi_ref, bi_ref, lam_ref, hs_ref, y_ref,
             xbuf, a_s, b_s, hcar):
        i = pl.program_id(0)

        @pl.when(i == 0)
        def _():
            hcar[...] = jnp.zeros_like(hcar)

        xbuf[pl.ds(0, LRU_HALO), :] = jnp.where(i > 0, xp[pl.ds(tt - LRU_HALO, LRU_HALO), :], 0.0)
        xbuf[pl.ds(LRU_HALO, tt), :] = xc[...]
        xr, _, _, ig, _, a, m = _lru_gates(xbuf, cw_ref, cb_ref, wa_ref, ba_ref, wi_ref, bi_ref, lam_ref, tt, wl)
        a_s[...] = a
        b_s[...] = m * ig * xr

        def group(gidx, h):
            r0 = pl.multiple_of(gidx * SUBLANE, SUBLANE)
            aa, bb = _group_scan(a_s[pl.ds(r0, SUBLANE), :], b_s[pl.ds(r0, SUBLANE), :], False)
            h8 = aa * h + bb
            hs_ref[pl.ds(r0, SUBLANE), :] = h8
            return _pick_row(h8, SUBLANE - 1)

        hcar[...] = lax.fori_loop(0, tt // SUBLANE, group, hcar[...])
        gel, _ = _gelu_and_grad(ry[...])
        y_ref[...] = hs_ref[...] * gel

    cur = lambda c: pl.BlockSpec((tt, wl), lambda i, c=c: (i, c))
    full = lambda shape: pl.BlockSpec(shape, lambda i: (0,) * len(shape))
    return pl.pallas_call(
        body, name="lru_fwd", grid=(t // tt,),
        in_specs=[cur(xi), pl.BlockSpec((tt, wl), lambda i: (jnp.maximum(i - 1, 0), xi)), cur(yi),
                  full((LRU_LEN, wl)), _vec_spec(wl), full(wa.shape), _vec_spec(wl), full(wi.shape), _vec_spec(wl),
                  _vec_spec(wl)],
        out_specs=[_row_spec(tt, wl), _row_spec(tt, wl)],
        out_shape=[jax.ShapeDtypeStruct((t, wl), F32)] * 2,
        scratch_shapes=[pltpu.VMEM((LRU_HALO + tt, wl), F32), pltpu.VMEM((tt, wl), F32), pltpu.VMEM((tt, wl), F32),
                        pltpu.VMEM((1, wl), F32)],
        compiler_params=_params(1),
    )(proj, proj, proj, cw, cb, wa, ba, wi, bi, lam)


def lru_bwd(proj, col0, wl, hs, dy, cw, cb, wa, ba, wi, bi, lam):
    t = proj.shape[0]
    tt = _tile(t, ROW_T)
    nt = t // tt
    xi, yi = col0 // wl, col0 // wl + 1
    bd = wl // LRU_BLOCKS

    def body(xc, xp, ry, hc, hp, dy_ref, cw_ref, cb_ref, wa_ref, ba_ref, wi_ref, bi_ref, lam_ref,
             dxy_ref, dcw_ref, dcb_ref, dwa_ref, dba_ref, dwi_ref, dbi_ref, dlam_ref,
             xbuf, hbuf, abuf, e_s, dh_s, dxbuf, dhcar):
        i = pl.program_id(0)
        first = i == 0

        @pl.when(first)
        def _():
            for r in (dcw_ref, dcb_ref, dwa_ref, dba_ref, dwi_ref, dbi_ref, dlam_ref, dhcar):
                r[...] = jnp.zeros_like(r)
            abuf[pl.ds(tt, LRU_HALO), :] = jnp.zeros((LRU_HALO, wl), F32)
            dxbuf[pl.ds(tt, LRU_HALO), :] = jnp.zeros((LRU_HALO, wl), F32)

        has_prev = i < nt - 1
        xbuf[pl.ds(0, LRU_HALO), :] = jnp.where(has_prev, xp[pl.ds(tt - LRU_HALO, LRU_HALO), :], 0.0)
        xbuf[pl.ds(LRU_HALO, tt), :] = xc[...]
        hbuf[pl.ds(0, LRU_HALO), :] = jnp.where(has_prev, hp[pl.ds(tt - LRU_HALO, LRU_HALO), :], 0.0)
        hbuf[pl.ds(LRU_HALO, tt), :] = hc[...]
        xr, xb, r, ig, spl, a, m = _lru_gates(xbuf, cw_ref, cb_ref, wa_ref, ba_ref, wi_ref, bi_ref, lam_ref, tt, wl)
        gel, dgel = _gelu_and_grad(ry[...])
        dyv = dy_ref[...]
        e_s[...] = dyv * gel
        dxy_ref[:, pl.ds(wl, wl)] = (dyv * hc[...] * dgel).astype(BF16)
        abuf[pl.ds(0, tt), :] = a
        a_next = abuf[pl.ds(1, tt), :]
        dh_s[...] = a_next

        def group(it, dh_in):
            r0 = pl.multiple_of((tt // SUBLANE - 1 - it) * SUBLANE, SUBLANE)
            aa, bb = _group_scan(dh_s[pl.ds(r0, SUBLANE), :], e_s[pl.ds(r0, SUBLANE), :], True)
            dh8 = aa * dh_in + bb
            dh_s[pl.ds(r0, SUBLANE), :] = dh8
            return _pick_row(dh8, 0)

        dhcar[...] = lax.fori_loop(0, tt // SUBLANE, group, dhcar[...])
        abuf[pl.ds(tt, LRU_HALO), :] = a[0:LRU_HALO, :]
        dh = dh_s[...]
        h_m1 = hbuf[pl.ds(LRU_HALO - 1, tt), :]
        dlog_a = dh * h_m1 * a - dh * ig * xr * (a * a / m)
        dig = dh * m * xr
        dxr = dh * m * ig
        dga = dlog_a * (-LRU_C) * spl * r * (1.0 - r)
        dgi = dig * ig * (1.0 - ig)
        dlam_ref[...] += jnp.sum(dlog_a * r, axis=0, keepdims=True) * (LRU_C * _sigmoid(-lam_ref[...]))
        dba_ref[...] += jnp.sum(dga, axis=0, keepdims=True)
        dbi_ref[...] += jnp.sum(dgi, axis=0, keepdims=True)
        dgab = dga.astype(BF16)
        dgib = dgi.astype(BF16)
        back = []
        for n in range(LRU_BLOCKS):
            sl = slice(n * bd, (n + 1) * bd)
            dwa_ref[n] += _dot(xb[:, sl], dgab[:, sl], "tn")
            dwi_ref[n] += _dot(xb[:, sl], dgib[:, sl], "tn")
            back.append(_dot(dgab[:, sl], wa_ref[n], "nt") + _dot(dgib[:, sl], wi_ref[n], "nt"))
        dxr = dxr + jnp.concatenate(back, axis=1)
        dcb_ref[...] += jnp.sum(dxr, axis=0, keepdims=True)
        dxbuf[pl.ds(0, tt), :] = dxr
        drx = jnp.zeros((tt, wl), F32)
        for tap in range(LRU_LEN):
            drx = drx + cw_ref[pl.ds(tap, 1), :] * dxbuf[pl.ds(LRU_LEN - 1 - tap, tt), :]
            dcw_ref[pl.ds(tap, 1), :] += jnp.sum(
                dxr * xbuf[pl.ds(LRU_HALO - (LRU_LEN - 1) + tap, tt), :], axis=0, keepdims=True)
        dxbuf[pl.ds(tt, LRU_HALO), :] = dxr[0:LRU_HALO, :]
        dxy_ref[:, pl.ds(0, wl)] = drx.astype(BF16)

    rev = lambda c: pl.BlockSpec((tt, wl), lambda i, c=c: (nt - 1 - i, c))
    rev_prev = lambda c: pl.BlockSpec((tt, wl), lambda i, c=c: (jnp.maximum(nt - 2 - i, 0), c))
    full = lambda shape: pl.BlockSpec(shape, lambda i: (0,) * len(shape))
    vec = _vec_spec(wl)
    return pl.pallas_call(
        body, name="lru_bwd", grid=(nt,),
        in_specs=[rev(xi), rev_prev(xi), rev(yi), rev(0), rev_prev(0), rev(0),
                  full((LRU_LEN, wl)), vec, full(wa.shape), vec, full(wi.shape), vec, vec],
        out_specs=[pl.BlockSpec((tt, 2 * wl), lambda i: (nt - 1 - i, 0)), full((LRU_LEN, wl)), vec,
                   full(wa.shape), vec, full(wi.shape), vec, vec],
        out_shape=[jax.ShapeDtypeStruct((t, 2 * wl), BF16), jax.ShapeDtypeStruct((LRU_LEN, wl), F32),
                   jax.ShapeDtypeStruct((1, wl), F32), jax.ShapeDtypeStruct(wa.shape, F32),
                   jax.ShapeDtypeStruct((1, wl), F32), jax.ShapeDtypeStruct(wi.shape, F32),
                   jax.ShapeDtypeStruct((1, wl), F32), jax.ShapeDtypeStruct((1, wl), F32)],
        scratch_shapes=[pltpu.VMEM((LRU_HALO + tt, wl), F32), pltpu.VMEM((LRU_HALO + tt, wl), F32),
                        pltpu.VMEM((tt + LRU_HALO, wl), F32), pltpu.VMEM((tt, wl), F32), pltpu.VMEM((tt, wl), F32),
                        pltpu.VMEM((tt + LRU_HALO, wl), F32), pltpu.VMEM((1, wl), F32)],
        compiler_params=_params(1),
    )(proj, proj, proj, hs, hs, dy, cw, cb, wa, ba, wi, bi, lam)


def _adamw(w, g, m, v):
    m = ADAM_B1 * m + (1.0 - ADAM_B1) * g
    v = ADAM_B2 * v + (1.0 - ADAM_B2) * (g * g)
    m_hat = m / (1.0 - ADAM_B1 ** ADAM_STEP)
    v_hat = v / (1.0 - ADAM_B2 ** ADAM_STEP)
    delta = -ADAM_LR * (m_hat / (jnp.sqrt(v_hat) + ADAM_EPS) + ADAM_WD * w)
    return delta, m, v


def adam_big(name, w, m, v, parts, chip):
    n_layers, rows, cols = w.shape
    tr = _tile(rows, 128 if cols > 1024 else 256)
    nrt = rows // tr

    def body(chip_ref, *refs):
        w_ref, m_ref, v_ref = refs[:3]
        part_refs = refs[3:3 + 4 * n_layers]
        g_ref, d_ref, mo_ref, vo_ref = refs[3 + 4 * n_layers:]
        layer = pl.program_id(0)
        for l in range(n_layers):
            @pl.when(layer == l)
            def _(l=l):
                g = part_refs[4 * l][...].astype(F32)
                for p in range(1, 4):
                    g = g + part_refs[4 * l + p][...].astype(F32)
                delta, mn, vn = _adamw(w_ref[...], g, m_ref[...], v_ref[...])
                g_ref[...] = g
                d_ref[...] = delta
                mo_ref[...] = mn
                vo_ref[...] = vn

    wspec = pl.BlockSpec((None, tr, cols), lambda l, i, chip_ref: (l, i, 0))
    operands, in_specs = [w, m, v], [wspec, wspec, wspec]
    for l in range(n_layers):
        mine, recv = parts[l]
        operands.append(mine)
        in_specs.append(pl.BlockSpec(
            (None, tr, cols), lambda ll, i, chip_ref, l=l: (chip_ref[0], jnp.where(ll == l, i, 0), 0)))
        for p in range(3):
            operands.append(recv)
            in_specs.append(pl.BlockSpec(
                (None, tr, cols), lambda ll, i, chip_ref, l=l, p=p: (p, jnp.where(ll == l, i, 0), 0)))
    return pl.pallas_call(
        body, name=name,
        grid_spec=pltpu.PrefetchScalarGridSpec(
            num_scalar_prefetch=1, grid=(n_layers, nrt), in_specs=in_specs, out_specs=[wspec] * 4),
        out_shape=[jax.ShapeDtypeStruct(w.shape, F32)] * 4, compiler_params=_params(2),
    )(chip, *operands)


def adam_small(w, m, v, g):
    rows = w.shape[0]
    tr = _tile(rows, PACK_ROWS)

    def body(w_ref, m_ref, v_ref, g_ref, d_ref, mo_ref, vo_ref):
        delta, mn, vn = _adamw(w_ref[...], g_ref[...], m_ref[...], v_ref[...])
        d_ref[...] = delta
        mo_ref[...] = mn
        vo_ref[...] = vn

    spec = _row_spec(tr, LANE)
    return pl.pallas_call(
        body, name="adam_small", grid=(rows // tr,), in_specs=[spec] * 4, out_specs=[spec] * 3,
        out_shape=[jax.ShapeDtypeStruct(w.shape, F32)] * 3, compiler_params=_params(1),
    )(w, m, v, g)


def sum_parts(parts):
    _, rows, _ = parts.shape
    tr = _tile(rows, PACK_ROWS)

    def body(p_ref, o_ref):
        acc = p_ref[0]
        for k in range(1, N_DEV):
            acc = acc + p_ref[k]
        o_ref[...] = acc

    return pl.pallas_call(
        body, name="sum_parts", grid=(rows // tr,),
        in_specs=[pl.BlockSpec((N_DEV, tr, LANE), lambda i: (0, i, 0))], out_specs=_row_spec(tr, LANE),
        out_shape=jax.ShapeDtypeStruct((rows, LANE), F32), compiler_params=_params(1),
    )(parts)


def pair_add(g, recv, core):
    n, rows, cols = recv.shape
    tr = _tile(rows, 256)

    def body(core_ref, a_ref, b_ref, o_ref):
        o_ref[...] = (a_ref[...].astype(F32) + b_ref[...].astype(F32)).astype(BF16)

    spec = pl.BlockSpec((None, tr, cols), lambda k, i, core_ref: (k, i, 0))
    return pl.pallas_call(
        body, name="pair_add",
        grid_spec=pltpu.PrefetchScalarGridSpec(
            num_scalar_prefetch=1, grid=(n, rows // tr),
            in_specs=[pl.BlockSpec((None, tr, cols), lambda k, i, core_ref: (2 * k + core_ref[0], i, 0)), spec],
            out_specs=spec),
        out_shape=jax.ShapeDtypeStruct(recv.shape, BF16), compiler_params=_params(2),
    )(core, g, recv)


def place_own(x, me, dtype):
    rows, cols = x.shape
    tr = _tile(rows, 256)

    def body(me_ref, x_ref, o_ref):
        o_ref[...] = x_ref[...].astype(dtype)

    return pl.pallas_call(
        body, name="place_own",
        grid_spec=pltpu.PrefetchScalarGridSpec(
            num_scalar_prefetch=1, grid=(rows // tr,),
            in_specs=[pl.BlockSpec((tr, cols), lambda i, me_ref: (i, 0))],
            out_specs=pl.BlockSpec((None, tr, cols), lambda i, me_ref: (me_ref[0], i, 0))),
        out_shape=jax.ShapeDtypeStruct((N_DEV, rows, cols), dtype), compiler_params=_params(1),
    )(me, x)


_HBM = pl.BlockSpec(memory_space=pltpu.HBM)


def _place():
    return lax.axis_index("x"), lax.axis_index("y"), lax.axis_index("c")


def _other_chips(x, y):
    return [(1 - x, y), (x, 1 - y), (1 - x, 1 - y)]


def all_gather(name, shard, me, dtype=None):
    def body(buf_ref, out_ref, send_sems, recv_sems):
        del buf_ref
        x, y, c = _place()
        mine, sibling = (x, y, c), (x, y, 1 - c)
        chips = _other_chips(x, y)

        def copy(k, block, to):
            slot = out_ref.at[4 * block[0] + 2 * block[1] + block[2]]
            return pltpu.make_async_remote_copy(
                src_ref=slot, dst_ref=slot, send_sem=send_sems.at[k], recv_sem=recv_sems.at[k],
                device_id=to, device_id_type=pl.DeviceIdType.MESH)

        first = [copy(0, mine, sibling)] + [copy(1 + j, mine, (*chip, c)) for j, chip in enumerate(chips)]
        for cp in first:
            cp.start()
        passed = [copy(4 + j, (*chip, c), sibling) for j, chip in enumerate(chips)]
        for j, chip in enumerate(chips):
            copy(1 + j, (*chip, c), mine).wait_recv()
            passed[j].start()
        copy(0, sibling, mine).wait_recv()
        for j, chip in enumerate(chips):
            copy(4 + j, (*chip, 1 - c), mine).wait_recv()
        for cp in first + passed:
            cp.wait_send()

    buf = place_own(shard, me, dtype or shard.dtype)
    return pl.pallas_call(
        body, name=name, out_shape=jax.ShapeDtypeStruct(buf.shape, buf.dtype),
        in_specs=[_HBM], out_specs=_HBM, input_output_aliases={0: 0},
        scratch_shapes=[pltpu.SemaphoreType.DMA((7,)), pltpu.SemaphoreType.DMA((7,))],
    )(buf)


def _own_block_copies(src_refs, dst_refs, send_sems, recv_sems):
    x, y, c = _place()
    peers = [(x, y, 1 - c)] + [(*chip, c) for chip in _other_chips(x, y)]
    copies = []
    for b, (src, dst) in enumerate(zip(src_refs, dst_refs)):
        for k, peer in enumerate(peers):
            copies.append((
                pltpu.make_async_remote_copy(
                    src_ref=src.at[4 * x + 2 * y + c], dst_ref=dst.at[4 * x + 2 * y + c],
                    send_sem=send_sems.at[4 * b + k], recv_sem=recv_sems.at[4 * b + k],
                    device_id=peer, device_id_type=pl.DeviceIdType.MESH),
                pltpu.make_async_remote_copy(
                    src_ref=src.at[4 * x + 2 * y + c], dst_ref=dst.at[4 * peer[0] + 2 * peer[1] + peer[2]],
                    send_sem=send_sems.at[4 * b + k], recv_sem=recv_sems.at[4 * b + k],
                    device_id=peer, device_id_type=pl.DeviceIdType.MESH)))
    return copies


def gather_start(name, bufs, after):
    n = len(bufs)

    def body(*refs):
        send_sems, recv_sems = refs[n + 1], refs[n + 2]
        thru = refs[n + 3:2 * n + 3]
        for send, _ in _own_block_copies(thru, thru, send_sems, recv_sems):
            send.start()
        refs[2 * n + 3][...] = jnp.zeros((SUBLANE, LANE), F32)

    return pl.pallas_call(
        body, name=name,
        out_shape=(pltpu.SemaphoreType.DMA((4 * n,)), pltpu.SemaphoreType.DMA((4 * n,)),
                   *[pltpu.HBM(b.shape, b.dtype) for b in bufs], jax.ShapeDtypeStruct((SUBLANE, LANE), F32)),
        in_specs=(*(_HBM,) * n, _ANY), out_specs=(_SEM, _SEM, *(_HBM,) * n, _TOKEN),
        input_output_aliases={b: 2 + b for b in range(n)},
        compiler_params=pltpu.CompilerParams(has_side_effects=_EFFECT),
    )(*[_hbm(b) for b in bufs], after)


def gather_wait(name, state, first, count, after):
    send_sems, recv_sems = state[:2]
    bufs = state[2 + first:2 + first + count]
    n = len(bufs)

    def body(*refs):
        ins = refs[:n]
        send_sems, recv_sems = refs[n], refs[n + 1]
        shift = 4 * first
        for send, arrival in _own_block_copies(
                ins, ins, send_sems.at[pl.ds(shift, 4 * n)], recv_sems.at[pl.ds(shift, 4 * n)]):
            send.wait_send()
            arrival.wait_recv()

    return pl.pallas_call(
        body, name=name, out_shape=tuple(pltpu.HBM(b.shape, b.dtype) for b in bufs),
        in_specs=(*(_HBM,) * n, _SEM, _SEM, _ANY), out_specs=(_HBM,) * n,
        input_output_aliases={b: b for b in range(n)},
        compiler_params=pltpu.CompilerParams(has_side_effects=_EFFECT),
    )(*bufs, send_sems, recv_sems, after)


def gather_finish(name, bufs):
    n = len(bufs)

    def body(*refs):
        outs = refs[n:2 * n]
        send_sems, recv_sems = refs[2 * n], refs[2 * n + 1]
        x, y, c = _place()
        copies = []
        for b, out in enumerate(outs):
            for k, chip in enumerate(_other_chips(x, y)):
                sem = 3 * b + k
                copies.append((
                    pltpu.make_async_remote_copy(
                        src_ref=out.at[4 * chip[0] + 2 * chip[1] + c], dst_ref=out.at[4 * chip[0] + 2 * chip[1] + c],
                        send_sem=send_sems.at[sem], recv_sem=recv_sems.at[sem],
                        device_id=(x, y, 1 - c), device_id_type=pl.DeviceIdType.MESH),
                    pltpu.make_async_remote_copy(
                        src_ref=out.at[4 * chip[0] + 2 * chip[1] + c], dst_ref=out.at[4 * chip[0] + 2 * chip[1] + 1 - c],
                        send_sem=send_sems.at[sem], recv_sem=recv_sems.at[sem],
                        device_id=(x, y, 1 - c), device_id_type=pl.DeviceIdType.MESH)))
        for send, _ in copies:
            send.start()
        for send, arrival in copies:
            send.wait_send()
            arrival.wait_recv()

    return pl.pallas_call(
        body, name=name, out_shape=tuple(jax.ShapeDtypeStruct(b.shape, b.dtype) for b in bufs),
        in_specs=(_HBM,) * n, out_specs=(_HBM,) * n, input_output_aliases={b: b for b in range(n)},
        scratch_shapes=[pltpu.SemaphoreType.DMA((3 * n,)), pltpu.SemaphoreType.DMA((3 * n,))],
    )(*bufs)


def scatter_pair(name, g):
    _, rows, cols = g.shape

    def body(g_ref, recv_ref, send_sems, recv_sems):
        x, y, c = _place()
        remote = [pltpu.make_async_remote_copy(
            src_ref=g_ref.at[2 * k + 1 - c], dst_ref=recv_ref.at[k], send_sem=send_sems.at[k], recv_sem=recv_sems.at[k],
            device_id=(x, y, 1 - c), device_id_type=pl.DeviceIdType.MESH) for k in range(4)]
        for cp in remote:
            cp.start()
        for cp in remote:
            cp.wait()

    return pl.pallas_call(
        body, name=name, out_shape=jax.ShapeDtypeStruct((4, rows, cols), g.dtype), in_specs=[_HBM], out_specs=_HBM,
        scratch_shapes=[pltpu.SemaphoreType.DMA((4,)), pltpu.SemaphoreType.DMA((4,))],
    )(g)


_SEM = pl.BlockSpec(memory_space=pltpu.SEMAPHORE)
_ANY = pl.BlockSpec(memory_space=pl.ANY)
_TOKEN = pl.BlockSpec(memory_space=pltpu.VMEM)
_EFFECT = pltpu.SideEffectType.DATAFLOW_SIDE_EFFECTING


def _hbm(a):
    return pltpu.with_memory_space_constraint(a, pltpu.HBM)


def _chip_copies(p_ref, land_ref, send_sems, recv_sems):
    x, y, c = _place()
    return [pltpu.make_async_remote_copy(
        src_ref=p_ref.at[2 * px + py], dst_ref=land_ref.at[k], send_sem=send_sems.at[k], recv_sem=recv_sems.at[k],
        device_id=(px, py, c), device_id_type=pl.DeviceIdType.MESH) for k, (px, py) in enumerate(_other_chips(x, y))]


def scatter_chips_start(name, p):
    _, rows, cols = p.shape

    def body(p_ref, land_ref, send_sems, recv_sems, p_thru, land_thru, token):
        for cp in _chip_copies(p_ref, land_ref, send_sems, recv_sems):
            cp.start()
        token[...] = jnp.zeros_like(token)

    return pl.pallas_call(
        body, name=name,
        out_shape=(pltpu.SemaphoreType.DMA((3,)), pltpu.SemaphoreType.DMA((3,)), pltpu.HBM(p.shape, p.dtype),
                   pltpu.HBM((3, rows, cols), p.dtype), jax.ShapeDtypeStruct((SUBLANE, LANE), F32)),
        in_specs=(_HBM, _HBM), out_specs=(_SEM, _SEM, _HBM, _HBM, _TOKEN), input_output_aliases={0: 2, 1: 3},
        compiler_params=pltpu.CompilerParams(has_side_effects=_EFFECT),
    )(_hbm(p), _hbm(lax.empty((3, rows, cols), p.dtype)))


def scatter_chips_wait(name, send_sems, recv_sems, p_thru, land_thru, after):
    def body(p_ref, land_ref, send_sems, recv_sems, after_ref, p_out, land_out):
        for cp in _chip_copies(p_ref, land_ref, send_sems, recv_sems):
            cp.wait_send()
            cp.wait_recv()

    return pl.pallas_call(
        body, name=name,
        out_shape=(pltpu.HBM(p_thru.shape, p_thru.dtype), pltpu.HBM(land_thru.shape, land_thru.dtype)),
        in_specs=(_HBM, _HBM, _SEM, _SEM, _ANY), out_specs=(_HBM, _HBM), input_output_aliases={0: 0, 1: 1},
        compiler_params=pltpu.CompilerParams(has_side_effects=_EFFECT),
    )(p_thru, land_thru, send_sems, recv_sems, after)


def reduce_scatter_start(tag, g, core):
    p = pair_add(g, scatter_pair("rs_pair_" + tag, g), core)
    return scatter_chips_start("rs_start_" + tag, p)


def reduce_scatter_wait(tag, state, after):
    send_sems, recv_sems, p_thru, land_thru, _ = state
    return scatter_chips_wait("rs_wait_" + tag, send_sems, recv_sems, p_thru, land_thru, after)


_SMALL = ("g_pre_mix", "g_post_mix", "g_pre_ffn", "g_post_ffn", "g_attn_grp", "g_conv_grp", "g_lru_grp",
          "dw_conv_w", "dw_conv_b", "conv_ln_g", "conv_ln_b", "lru_conv_w", "lru_conv_b",
          "lru_w_a", "lru_b_a", "lru_w_i", "lru_b_i", "lru_lambda")
_COL_SHARDED_SMALL = ("dw_conv_w", "lru_conv_w")
_BIG = ("w_in", "w_out", "w_gate", "w_up", "w_down")
_TRANSPOSED = ("w_gate", "w_up")
_ALL = ("w_in", "w_out", "g_pre_mix", "g_post_mix", "g_pre_ffn", "g_post_ffn", "g_attn_grp", "g_conv_grp", "g_lru_grp",
        "dw_conv_w", "dw_conv_b", "conv_ln_g", "conv_ln_b", "lru_conv_w", "lru_conv_b", "lru_w_a", "lru_b_a",
        "lru_w_i", "lru_b_i", "lru_lambda", "w_gate", "w_up", "w_down")


def _pack(arrays):
    flat = jnp.concatenate([a.reshape(-1) for a in arrays])
    pad = (-flat.shape[0]) % (PACK_ROWS * LANE)
    return jnp.pad(flat, (0, pad)).reshape(-1, LANE)


def _unpack(packed, shapes):
    flat = packed.reshape(-1)
    out, pos = [], 0
    for s in shapes:
        n = math.prod(s)
        out.append(flat[pos:pos + n].reshape(s))
        pos += n
    return out


def kernel(x, w_in, w_out, g_pre_mix, g_post_mix, g_pre_ffn, g_post_ffn, g_attn_grp, g_conv_grp, g_lru_grp, dw_conv_w, dw_conv_b, conv_ln_g, conv_ln_b, lru_conv_w, lru_conv_b, lru_w_a, lru_b_a, lru_w_i, lru_b_i, lru_lambda, w_gate, w_up, w_down, loss_target, m_w_in, m_w_out, m_g_pre_mix, m_g_post_mix, m_g_pre_ffn, m_g_post_ffn, m_g_attn_grp, m_g_conv_grp, m_g_lru_grp, m_dw_conv_w, m_dw_conv_b, m_conv_ln_g, m_conv_ln_b, m_lru_conv_w, m_lru_conv_b, m_lru_w_a, m_lru_b_a, m_lru_w_i, m_lru_b_i, m_lru_lambda, m_w_gate, m_w_up, m_w_down, v_w_in, v_w_out, v_g_pre_mix, v_g_post_mix, v_g_pre_ffn, v_g_post_ffn, v_g_attn_grp, v_g_conv_grp, v_g_lru_grp, v_dw_conv_w, v_dw_conv_b, v_conv_ln_g, v_conv_ln_b, v_lru_conv_w, v_lru_conv_b, v_lru_w_a, v_lru_b_a, v_lru_w_i, v_lru_b_i, v_lru_lambda, v_w_gate, v_w_up, v_w_down):
    env = dict(locals())
    wts = {n: env[n] for n in _ALL}
    mom = {n: env["m_" + n] for n in _ALL}
    var = {n: env["v_" + n] for n in _ALL}
    for group in (wts, mom, var):
        for n in _TRANSPOSED:
            group[n] = jnp.swapaxes(group[n], 1, 2)

    depth = w_in.shape[0]
    h = x[0]
    target = loss_target[0]
    t, d = h.shape
    attn_w = d // 2
    n_heads = attn_w // HEAD_DIM
    cc = d // 4
    wl = d // 4
    conv_col, lru_col = 3 * attn_w, 3 * attn_w + 2 * cc
    me = 4 * lax.axis_index("x") + 2 * lax.axis_index("y") + lax.axis_index("c")
    me_s = me.astype(jnp.int32).reshape(1)
    chip_s = (2 * lax.axis_index("x") + lax.axis_index("y")).astype(jnp.int32).reshape(1)
    core_s = lax.axis_index("c").astype(jnp.int32).reshape(1)

    n_taps = DW_LEN + LRU_LEN
    taps = jnp.concatenate([dw_conv_w, lru_conv_w], axis=1).reshape(depth * n_taps, cc // N_DEV)
    taps = all_gather("ag_taps", taps, me_s)
    taps = jnp.moveaxis(taps.reshape(N_DEV, depth, n_taps, cc // N_DEV), 0, 2).reshape(depth, n_taps, cc)
    dw_full, lcw_full = taps[:, :DW_LEN], taps[:, DW_LEN:]

    def vec(a, l):
        return a[l].reshape(1, -1)

    ag_state, started = [], taps
    for l in range(depth):
        ag_state.append(gather_start(f"ag_start_{l}", [place_own(wts[n][l], me_s, BF16) for n in _BIG], started))
        started = ag_state[l][-1]
    started = started[0:1, 0:1]

    saved = []
    u1 = rms_pre(h, vec(g_pre_mix, 0) + started)
    loss_sum = dh = dbr = None
    for l in range(depth):
        n_first = 1 if l == 0 else len(_BIG)
        wg = dict(zip(_BIG, gather_finish("ag_finish_a", gather_wait(f"ag_wait_{l}a", ag_state[l], 0, n_first, u1))))
        wa_b, wi_b = lru_w_a[l].astype(BF16), lru_w_i[l].astype(BF16)
        proj = mm_proj(u1, wg["w_in"])
        y_attn = attn_fwd(proj, n_heads)
        cpre, y_conv = conv_fwd(proj, conv_col, cc, dw_full[l], vec(dw_conv_b, l), vec(conv_ln_g, l), vec(conv_ln_b, l))
        hs, y_lru = lru_fwd(proj, lru_col, wl, lcw_full[l], vec(lru_conv_b, l), wa_b, vec(lru_b_a, l), wi_b,
                            vec(lru_b_i, l), vec(lru_lambda, l))
        mixed = mix_fwd(y_attn, y_conv, y_lru, vec(g_attn_grp, l), vec(g_conv_grp, l), vec(g_lru_grp, l))
        if n_first < len(_BIG):
            rest = gather_wait(f"ag_wait_{l}b", ag_state[l], n_first, len(_BIG) - n_first, mixed)
            wg.update(zip(_BIG[n_first:], gather_finish("ag_finish_b", rest)))
        wg["w_out"] = wg["w_out"].reshape(attn_w + cc + wl, d)
        o = mm_plain("mm_out", mixed, wg["w_out"], "nn", F32)
        h2, u2 = res_norm(h, o, vec(g_post_mix, l), vec(g_pre_ffn, l))
        gt, up, f = ffn_up(u2, wg["w_gate"], wg["w_up"])
        dn = mm_down(f, wg["w_down"])
        saved.append(dict(wg=wg, wa_b=wa_b, wi_b=wi_b, h=h, u1=u1, proj=proj, y_attn=y_attn, cpre=cpre, y_conv=y_conv,
                          hs=hs, y_lru=y_lru, mixed=mixed, o=o, h2=h2, u2=u2, gt=gt, up=up, f=f, dn=dn))
        if l + 1 < depth:
            h, u1 = res_norm(h2, dn, vec(g_post_ffn, l), vec(g_pre_mix, l + 1))
        else:
            loss_sum, dh, dbr, dg_post_ffn = final_loss(h2, dn, vec(g_post_ffn, l), target)

    loss = lax.psum(0.5 * loss_sum[0, 0] / d, MESH_AXES)

    small = {n: [None] * depth for n in _SMALL}
    rs_state = {n: [None] * depth for n in _BIG}
    for l in reversed(range(depth)):
        s = saved[l]
        wg = s["wg"]
        small["g_post_ffn"][l] = dg_post_ffn
        dgt, dup = ffn_bwd(dbr, wg["w_down"], s["gt"], s["up"])
        rs_state["w_down"][l] = reduce_scatter_start(f"down_{l}", mm_dw_rows("mm_dw_down", s["f"], dbr), core_s)
        rs_state["w_gate"][l] = reduce_scatter_start(f"gate_{l}", mm_dw_rows("mm_dw_gate", dgt, s["u2"]), core_s)
        rs_state["w_up"][l] = reduce_scatter_start(f"up_{l}", mm_dw_rows("mm_dw_gate", dup, s["u2"]), core_s)
        du2 = mm_dx_ffn(dgt, wg["w_gate"], dup, wg["w_up"])
        started = sum(rs_state[n][l][-1][0:1, 0:1] for n in ("w_down", "w_gate", "w_up"))
        dh2, small["g_pre_ffn"][l], do, small["g_post_mix"][l] = norm_bwd(
            dh, du2, s["h2"], vec(g_pre_ffn, l) + started, (s["o"], vec(g_post_mix, l)))
        dmixed = mm_plain("mm_dmixed", do, wg["w_out"], "nt", F32)
        dw_out = mm_plain("mm_dw_out", s["mixed"], do, "tn", BF16)
        rs_state["w_out"][l] = reduce_scatter_start(f"out_{l}", dw_out.reshape(N_DEV, -1, d), core_s)
        (dya, dc, dyl, small["g_attn_grp"][l], small["g_conv_grp"][l], small["g_lru_grp"][l],
         small["conv_ln_g"][l], small["conv_ln_b"][l]) = mix_bwd(
            dmixed, s["y_attn"], s["y_conv"], s["y_lru"], s["cpre"],
            vec(g_attn_grp, l) + rs_state["w_out"][l][-1][0:1, 0:1], vec(g_conv_grp, l),
            vec(g_lru_grp, l), vec(conv_ln_g, l), vec(conv_ln_b, l))
        dq, dk, dv = attn_bwd(s["proj"], dya, n_heads)
        dvg, small["dw_conv_w"][l], small["dw_conv_b"][l] = conv_bwd(s["proj"], conv_col, cc, dc, dw_full[l])
        (dxy, small["lru_conv_w"][l], small["lru_conv_b"][l], small["lru_w_a"][l], small["lru_b_a"][l],
         small["lru_w_i"][l], small["lru_b_i"][l], small["lru_lambda"][l]) = lru_bwd(
            s["proj"], lru_col, wl, s["hs"], dyl, lcw_full[l], vec(lru_conv_b, l), s["wa_b"], vec(lru_b_a, l),
            s["wi_b"], vec(lru_b_i, l), vec(lru_lambda, l))
        dproj = jnp.concatenate([dq, dk, dv, dvg, dxy], axis=1)
        rs_state["w_in"][l] = reduce_scatter_start(f"in_{l}", mm_dw_cols("mm_dw_in", s["u1"], dproj, N_DEV), core_s)
        du1 = mm_dx_cols("mm_dx_in", dproj, wg["w_in"])
        g_pre = vec(g_pre_mix, l) + rs_state["w_in"][l][-1][0:1, 0:1]
        if l > 0:
            p = saved[l - 1]
            dh, small["g_pre_mix"][l], dbr, dg_post_ffn = norm_bwd(
                dh2, du1, s["h"], g_pre, (p["dn"], vec(g_post_ffn, l - 1)))
        else:
            dh, small["g_pre_mix"][l] = norm_bwd(dh2, du1, s["h"], g_pre)
    grad_x = dh[None]
    big_parts = {n: [reduce_scatter_wait(f"{n[2:]}_{l}", rs_state[n][l], dh) for l in range(depth)] for n in _BIG}

    small_shapes = [(depth,) + tuple(wts[n].shape[1:]) if n not in _COL_SHARDED_SMALL
                    else (depth, wts[n].shape[1], cc) for n in _SMALL]
    part = _pack([jnp.stack([a.reshape(shp[1:]) for a in small[n]]) for n, shp in zip(_SMALL, small_shapes)])
    g_small = _unpack(sum_parts(all_gather("ag_small", part, me_s)), small_shapes)
    grads = {}
    for n, g in zip(_SMALL, g_small):
        if n in _COL_SHARDED_SMALL:
            g = lax.dynamic_slice_in_dim(g, me * (cc // N_DEV), cc // N_DEV, axis=2)
        grads[n] = g
    local_shapes = [tuple(wts[n].shape) for n in _SMALL]
    d_small, m_small, v_small = adam_small(
        _pack([wts[n] for n in _SMALL]), _pack([mom[n] for n in _SMALL]), _pack([var[n] for n in _SMALL]),
        _pack([grads[n] for n in _SMALL]))
    delta = dict(zip(_SMALL, _unpack(d_small, local_shapes)))
    new_m = dict(zip(_SMALL, _unpack(m_small, local_shapes)))
    new_v = dict(zip(_SMALL, _unpack(v_small, local_shapes)))

    for n in _BIG:
        shape = wts[n].shape
        _, rows, cols = big_parts[n][0][0].shape
        view = (depth, rows, cols)
        g, dl, mn, vn = adam_big("adam_" + n, wts[n].reshape(view), mom[n].reshape(view), var[n].reshape(view),
                                 big_parts[n], chip_s)
        grads[n], delta[n], new_m[n], new_v[n] = (a.reshape(shape) for a in (g, dl, mn, vn))
    for group in (grads, delta, new_m, new_v):
        for n in _TRANSPOSED:
            group[n] = jnp.swapaxes(group[n], 1, 2)

    return (loss, grad_x, *[grads[n] for n in _ALL], *[delta[n] for n in _ALL],
            *[new_m[n] for n in _ALL], *[new_v[n] for n in _ALL])
```

```python
import functools
import math

import jax
import jax.numpy as jnp
from jax import lax
from jax.experimental import pallas as pl
from jax.experimental.pallas import tpu as pltpu

F32 = jnp.float32
BF16 = jnp.bfloat16

N_DEV = 8
EPS = 1e-6
HEAD_DIM = 128
DW_LEN = 31
LRU_LEN = 4
LRU_BLOCKS = 4
LRU_C = 8.0
ATT_TQ = 512
ATT_TK = 512
ATT_SUM = 256
ATT_HEADS = 2
ROW_T = 256
CONV_HALO = 32
LRU_HALO = 8
LANE = 128
SUBLANE = 8
PACK_ROWS = 512
VMEM_LIMIT = 56 * 1024 * 1024

ADAM_LR = 0.001
ADAM_B1 = 0.9
ADAM_B2 = 0.999
ADAM_EPS = 1e-08
ADAM_WD = 0.01
ADAM_STEP = 10

MESH_AXES = ("x", "y", "c")
_DIMS = {
    "nn": (((1,), (0,)), ((), ())),
    "nt": (((1,), (1,)), ((), ())),
    "tn": (((0,), (0,)), ((), ())),
}


def _params(n_axes):
    return pltpu.CompilerParams(
        dimension_semantics=("arbitrary",) * n_axes, vmem_limit_bytes=VMEM_LIMIT)


def _dot(a, b, mode="nn"):
    return lax.dot_general(a, b, _DIMS[mode], preferred_element_type=F32)


def _sigmoid(x):
    return 1.0 / (1.0 + jnp.exp(-x))


def _softplus(x):
    return jnp.maximum(x, 0.0) + jnp.log(1.0 + jnp.exp(-jnp.abs(x)))


def _neg_expm1(x):
    series = x * (1.0 + x * (0.5 + x * (1.0 / 6 + x * (1.0 / 24 + x * (1.0 / 120 + x * (1.0 / 720))))))
    return jnp.where(x > -0.25, -series, 1.0 - jnp.exp(x))


_GELU_C = math.sqrt(2.0 / math.pi)


def _gelu_and_grad(x):
    inner = _GELU_C * (x + 0.044715 * x * x * x)
    t = jnp.tanh(inner)
    val = 0.5 * x * (1.0 + t)
    grad = 0.5 * (1.0 + t) + 0.5 * x * (1.0 - t * t) * _GELU_C * (1.0 + 3 * 0.044715 * x * x)
    return val, grad


def _rms_stats(x):
    r = lax.rsqrt(jnp.mean(x * x, axis=-1, keepdims=True) + EPS)
    return x * r, r


def _rms_bwd(dy, x, g):
    xn, r = _rms_stats(x)
    dxn = dy * g
    dx = r * (dxn - xn * jnp.mean(dxn * xn, axis=-1, keepdims=True))
    return dx, jnp.sum(dy * xn, axis=0, keepdims=True)


def _row_spec(tr, width, col=0):
    return pl.BlockSpec((tr, width), lambda i, col=col: (i, col))


def _vec_spec(width):
    return pl.BlockSpec((1, width), lambda i: (0, 0))


def _matmul(name, mode, operands, in_specs, out_shape, out_spec, grid, out_block):
    npairs = len(operands) // 2
    nk = grid[2]

    def body(*refs):
        o_ref = refs[2 * npairs]

        def partial():
            acc = None
            for p in range(npairs):
                d = _dot(refs[2 * p][...], refs[2 * p + 1][...], mode)
                acc = d if acc is None else acc + d
            return acc

        if nk == 1:
            o_ref[...] = partial().astype(o_ref.dtype)
        else:
            acc_ref = refs[2 * npairs + 1]
            k = pl.program_id(2)

            @pl.when(k == 0)
            def _():
                acc_ref[...] = jnp.zeros_like(acc_ref)

            acc_ref[...] += partial()

            @pl.when(k == nk - 1)
            def _():
                o_ref[...] = acc_ref[...].astype(o_ref.dtype)

    return pl.pallas_call(
        body, name=name, grid=grid, in_specs=in_specs, out_specs=out_spec, out_shape=out_shape,
        scratch_shapes=[] if nk == 1 else [pltpu.VMEM(out_block, F32)],
        compiler_params=_params(3),
    )(*operands)


def _tile(n, t):
    if n <= t:
        return n
    return max(k for k in range(SUBLANE, t + 1, SUBLANE) if n % k == 0)


def mm_proj(u, w):
    t, d = u.shape
    nblk, _, nb = w.shape
    tm = _tile(t, 1024)
    return _matmul(
        "mm_proj", "nn", (u, w),
        [pl.BlockSpec((tm, d), lambda j, i, k: (i, 0)), pl.BlockSpec((None, d, nb), lambda j, i, k: (j, 0, 0))],
        jax.ShapeDtypeStruct((t, nblk * nb), F32), pl.BlockSpec((tm, nb), lambda j, i, k: (i, j)),
        (nblk, t // tm, 1), (tm, nb))


def mm_plain(name, a, b, mode, out_dtype):
    if mode == "nn":
        (m, kk), n = a.shape, b.shape[1]
    elif mode == "nt":
        (m, kk), n = a.shape, b.shape[0]
    else:
        (kk, m), n = a.shape, b.shape[1]
    tm, tn = _tile(m, 1024), _tile(n, 1024)
    a_spec = (pl.BlockSpec((kk, tm), lambda i, j, k: (0, i)) if mode == "tn"
              else pl.BlockSpec((tm, kk), lambda i, j, k: (i, 0)))
    b_spec = (pl.BlockSpec((tn, kk), lambda i, j, k: (j, 0)) if mode == "nt"
              else pl.BlockSpec((kk, tn), lambda i, j, k: (0, j)))
    return _matmul(
        name, mode, (a, b), [a_spec, b_spec],
        jax.ShapeDtypeStruct((m, n), out_dtype), pl.BlockSpec((tm, tn), lambda i, j, k: (i, j)),
        (m // tm, n // tn, 1), (tm, tn))


def mm_down(f, w):
    nblk, t, fb = f.shape
    d = w.shape[2]
    tm, tn = _tile(t, 1024), _tile(d, 1024)
    return _matmul(
        "mm_down", "nn", (f, w),
        [pl.BlockSpec((None, tm, fb), lambda i, j, k: (k, i, 0)), pl.BlockSpec((None, fb, tn), lambda i, j, k: (k, 0, j))],
        jax.ShapeDtypeStruct((t, d), F32), pl.BlockSpec((tm, tn), lambda i, j, k: (i, j)),
        (t // tm, d // tn, nblk), (tm, tn))


def mm_dw_rows(name, a, g):
    nblk, t, fb = a.shape
    d = g.shape[1]
    tn = _tile(d, 1024)
    return _matmul(
        name, "tn", (a, g),
        [pl.BlockSpec((None, t, fb), lambda j, n, k: (j, 0, 0)), pl.BlockSpec((t, tn), lambda j, n, k: (0, n))],
        jax.ShapeDtypeStruct((nblk, fb, d), BF16), pl.BlockSpec((None, fb, tn), lambda j, n, k: (j, 0, n)),
        (nblk, d // tn, 1), (fb, tn))


def mm_dw_cols(name, u, g, nblk):
    t, d = u.shape
    nb = g.shape[1] // nblk
    tmd = _tile(d, 1024)
    return _matmul(
        name, "tn", (u, g),
        [pl.BlockSpec((t, tmd), lambda j, i, k: (0, i)), pl.BlockSpec((t, nb), lambda j, i, k: (0, j))],
        jax.ShapeDtypeStruct((nblk, d, nb), BF16), pl.BlockSpec((None, tmd, nb), lambda j, i, k: (j, i, 0)),
        (nblk, d // tmd, 1), (tmd, nb))


def mm_dx_cols(name, g, w):
    t = g.shape[0]
    nblk, d, nb = w.shape
    tm, tn = _tile(t, 1024), _tile(d, 1024)
    return _matmul(
        name, "nt", (g, w),
        [pl.BlockSpec((tm, nb), lambda i, j, k: (i, k)), pl.BlockSpec((None, tn, nb), lambda i, j, k: (k, j, 0))],
        jax.ShapeDtypeStruct((t, d), F32), pl.BlockSpec((tm, tn), lambda i, j, k: (i, j)),
        (t // tm, d // tn, nblk), (tm, tn))


def mm_dx_ffn(dgt, wg, dup, wu):
    nblk, t, fb = dgt.shape
    d = wg.shape[2]
    tm, tn = _tile(t, 1024), _tile(d, 1024)
    a_spec = pl.BlockSpec((None, tm, fb), lambda i, j, k: (k, i, 0))
    b_spec = pl.BlockSpec((None, fb, tn), lambda i, j, k: (k, 0, j))
    return _matmul(
        "mm_dx_ffn", "nn", (dgt, wg, dup, wu), [a_spec, b_spec, a_spec, b_spec],
        jax.ShapeDtypeStruct((t, d), F32), pl.BlockSpec((tm, tn), lambda i, j, k: (i, j)),
        (t // tm, d // tn, nblk), (tm, tn))


def ffn_up(u, wg, wu):
    t, d = u.shape
    nblk, fb, _ = wg.shape
    tm = _tile(t, 512)

    def body(u_ref, wg_ref, wu_ref, gt_ref, up_ref, f_ref):
        uu = u_ref[...]
        gt = _dot(uu, wg_ref[...], "nt")
        up = _dot(uu, wu_ref[...], "nt")
        gt_ref[...] = gt.astype(BF16)
        up_ref[...] = up.astype(BF16)
        f_ref[...] = (gt * _sigmoid(gt) * up).astype(BF16)

    w_spec = pl.BlockSpec((None, fb, d), lambda j, i: (j, 0, 0))
    o_spec = pl.BlockSpec((None, tm, fb), lambda j, i: (j, i, 0))
    return pl.pallas_call(
        body, name="ffn_up", grid=(nblk, t // tm),
        in_specs=[pl.BlockSpec((tm, d), lambda j, i: (i, 0)), w_spec, w_spec],
        out_specs=[o_spec, o_spec, o_spec],
        out_shape=[jax.ShapeDtypeStruct((nblk, t, fb), BF16)] * 3,
        compiler_params=_params(2),
    )(u, wg, wu)


def ffn_bwd(dd, wd, gt, up):
    t, d = dd.shape
    nblk, fb, _ = wd.shape
    tm = _tile(t, 512)

    def body(dd_ref, wd_ref, gt_ref, up_ref, dgt_ref, dup_ref):
        df = _dot(dd_ref[...], wd_ref[...], "nt")
        g = gt_ref[...].astype(F32)
        s = _sigmoid(g)
        dgt_ref[...] = (df * up_ref[...].astype(F32) * s * (1.0 + g * (1.0 - s))).astype(BF16)
        dup_ref[...] = (df * g * s).astype(BF16)

    s_spec = pl.BlockSpec((None, tm, fb), lambda j, i: (j, i, 0))
    return pl.pallas_call(
        body, name="ffn_bwd", grid=(nblk, t // tm),
        in_specs=[pl.BlockSpec((tm, d), lambda j, i: (i, 0)), pl.BlockSpec((None, fb, d), lambda j, i: (j, 0, 0)),
                  s_spec, s_spec],
        out_specs=[s_spec, s_spec],
        out_shape=[jax.ShapeDtypeStruct((nblk, t, fb), BF16)] * 2,
        compiler_params=_params(2),
    )(dd, wd, gt, up)


def rms_pre(h, g):
    t, d = h.shape
    tr = _tile(t, ROW_T)

    def body(h_ref, g_ref, o_ref):
        xn, _ = _rms_stats(h_ref[...])
        o_ref[...] = (xn * g_ref[...]).astype(BF16)

    return pl.pallas_call(
        body, name="rms_pre", grid=(t // tr,),
        in_specs=[_row_spec(tr, d), _vec_spec(d)], out_specs=_row_spec(tr, d),
        out_shape=jax.ShapeDtypeStruct((t, d), BF16), compiler_params=_params(1),
    )(h, g)


def res_norm(h, o, g_post, g_pre):
    t, d = h.shape
    tr = _tile(t, ROW_T)

    def body(h_ref, o_ref, gpo_ref, gpr_ref, h2_ref, u_ref):
        on, _ = _rms_stats(o_ref[...])
        h2 = h_ref[...] + on * gpo_ref[...]
        h2_ref[...] = h2
        hn, _ = _rms_stats(h2)
        u_ref[...] = (hn * gpr_ref[...]).astype(BF16)

    return pl.pallas_call(
        body, name="res_norm", grid=(t // tr,),
        in_specs=[_row_spec(tr, d), _row_spec(tr, d), _vec_spec(d), _vec_spec(d)],
        out_specs=[_row_spec(tr, d), _row_spec(tr, d)],
        out_shape=[jax.ShapeDtypeStruct((t, d), F32), jax.ShapeDtypeStruct((t, d), BF16)],
        compiler_params=_params(1),
    )(h, o, g_post, g_pre)


def final_loss(h2, dbr, g_post, target):
    t, d = h2.shape
    tr = _tile(t, ROW_T)

    def body(h2_ref, d_ref, g_ref, tg_ref, loss_ref, dy_ref, dd_ref, dg_ref):
        i = pl.program_id(0)

        @pl.when(i == 0)
        def _():
            loss_ref[...] = jnp.zeros_like(loss_ref)
            dg_ref[...] = jnp.zeros_like(dg_ref)

        x = d_ref[...]
        g = g_ref[...]
        xn, _ = _rms_stats(x)
        diff = h2_ref[...] + xn * g - tg_ref[...]
        loss_ref[...] += jnp.sum(jnp.sum(diff * diff, axis=1, keepdims=True), axis=0, keepdims=True)
        dy = diff * (1.0 / d)
        dy_ref[...] = dy
        dx, dg = _rms_bwd(dy, x, g)
        dd_ref[...] = dx.astype(BF16)
        dg_ref[...] += dg

    return pl.pallas_call(
        body, name="final_loss", grid=(t // tr,),
        in_specs=[_row_spec(tr, d), _row_spec(tr, d), _vec_spec(d), _row_spec(tr, d)],
        out_specs=[pl.BlockSpec((1, 1), lambda i: (0, 0)), _row_spec(tr, d), _row_spec(tr, d), _vec_spec(d)],
        out_shape=[jax.ShapeDtypeStruct((1, 1), F32), jax.ShapeDtypeStruct((t, d), F32),
                   jax.ShapeDtypeStruct((t, d), BF16), jax.ShapeDtypeStruct((1, d), F32)],
        compiler_params=_params(1),
    )(h2, dbr, g_post, target)


def norm_bwd(dh_out, du, h_in, g_pre, prev=None):
    t, d = h_in.shape
    tr = _tile(t, ROW_T)
    with_prev = prev is not None

    def body(*refs):
        if with_prev:
            dho_ref, du_ref, h_ref, gpr_ref, br_ref, gpo_ref, dh_ref, dgpr_ref, dbr_ref, dgpo_ref = refs
        else:
            dho_ref, du_ref, h_ref, gpr_ref, dh_ref, dgpr_ref = refs
        i = pl.program_id(0)

        @pl.when(i == 0)
        def _():
            dgpr_ref[...] = jnp.zeros_like(dgpr_ref)
            if with_prev:
                dgpo_ref[...] = jnp.zeros_like(dgpo_ref)

        dx, dg = _rms_bwd(du_ref[...], h_ref[...], gpr_ref[...])
        dh = dho_ref[...] + dx
        dh_ref[...] = dh
        dgpr_ref[...] += dg
        if with_prev:
            dbr, dg2 = _rms_bwd(dh, br_ref[...], gpo_ref[...])
            dbr_ref[...] = dbr.astype(BF16)
            dgpo_ref[...] += dg2

    row, vec = _row_spec(tr, d), _vec_spec(d)
    in_specs = [row, row, row, vec] + ([row, vec] if with_prev else [])
    out_specs = [row, vec] + ([row, vec] if with_prev else [])
    out_shape = [jax.ShapeDtypeStruct((t, d), F32), jax.ShapeDtypeStruct((1, d), F32)]
    if with_prev:
        out_shape += [jax.ShapeDtypeStruct((t, d), BF16), jax.ShapeDtypeStruct((1, d), F32)]
    args = (dh_out, du, h_in, g_pre) + (tuple(prev) if with_prev else ())
    return pl.pallas_call(
        body, name="norm_bwd_chain" if with_prev else "norm_bwd_first", grid=(t // tr,),
        in_specs=in_specs, out_specs=out_specs, out_shape=out_shape, compiler_params=_params(1),
    )(*args)


def mix_fwd(ya, yc, yl, ga, gc, gl):
    t, wa = ya.shape
    wc, wl = yc.shape[1], yl.shape[1]
    tr = _tile(t, ROW_T)

    def body(ya_ref, yc_ref, yl_ref, ga_ref, gc_ref, gl_ref, o_ref):
        o_ref[:, pl.ds(0, wa)] = (_rms_stats(ya_ref[...])[0] * ga_ref[...]).astype(BF16)
        o_ref[:, pl.ds(wa, wc)] = (_rms_stats(yc_ref[...])[0] * gc_ref[...]).astype(BF16)
        o_ref[:, pl.ds(wa + wc, wl)] = (_rms_stats(yl_ref[...])[0] * gl_ref[...]).astype(BF16)

    return pl.pallas_call(
        body, name="mix_fwd", grid=(t // tr,),
        in_specs=[_row_spec(tr, wa), _row_spec(tr, wc), _row_spec(tr, wl), _vec_spec(wa), _vec_spec(wc), _vec_spec(wl)],
        out_specs=_row_spec(tr, wa + wc + wl),
        out_shape=jax.ShapeDtypeStruct((t, wa + wc + wl), BF16), compiler_params=_params(1),
    )(ya, yc, yl, ga, gc, gl)


def mix_bwd(dmixed, ya, yc, yl, cpre, ga, gc, gl, lng, lnb):
    t, wa = ya.shape
    wc, wl = yc.shape[1], yl.shape[1]
    tr = _tile(t, ROW_T)

    def body(dm_ref, ya_ref, yc_ref, yl_ref, c_ref, ga_ref, gc_ref, gl_ref, lg_ref, lb_ref,
             dya_ref, dc_ref, dyl_ref, dga_ref, dgc_ref, dgl_ref, dlg_ref, dlb_ref):
        i = pl.program_id(0)

        @pl.when(i == 0)
        def _():
            for r in (dga_ref, dgc_ref, dgl_ref, dlg_ref, dlb_ref):
                r[...] = jnp.zeros_like(r)

        dya, dga = _rms_bwd(dm_ref[:, pl.ds(0, wa)], ya_ref[...], ga_ref[...])
        dya_ref[...] = dya
        dga_ref[...] += dga
        dyl, dgl = _rms_bwd(dm_ref[:, pl.ds(wa + wc, wl)], yl_ref[...], gl_ref[...])
        dyl_ref[...] = dyl
        dgl_ref[...] += dgl
        dyc, dgc = _rms_bwd(dm_ref[:, pl.ds(wa, wc)], yc_ref[...], gc_ref[...])
        dgc_ref[...] += dgc
        c = c_ref[...]
        xc = c - jnp.mean(c, axis=-1, keepdims=True)
        rstd = lax.rsqrt(jnp.mean(xc * xc, axis=-1, keepdims=True) + EPS)
        xhat = xc * rstd
        ln = xhat * lg_ref[...] + lb_ref[...]
        s = _sigmoid(ln)
        dln = dyc * s * (1.0 + ln * (1.0 - s))
        dlg_ref[...] += jnp.sum(dln * xhat, axis=0, keepdims=True)
        dlb_ref[...] += jnp.sum(dln, axis=0, keepdims=True)
        dxh = dln * lg_ref[...]
        dc_ref[...] = rstd * (dxh - jnp.mean(dxh, axis=-1, keepdims=True)
                              - xhat * jnp.mean(dxh * xhat, axis=-1, keepdims=True))

    return pl.pallas_call(
        body, name="mix_bwd", grid=(t // tr,),
        in_specs=[_row_spec(tr, wa + wc + wl), _row_spec(tr, wa), _row_spec(tr, wc), _row_spec(tr, wl), _row_spec(tr, wc),
                  _vec_spec(wa), _vec_spec(wc), _vec_spec(wl), _vec_spec(wc), _vec_spec(wc)],
        out_specs=[_row_spec(tr, wa), _row_spec(tr, wc), _row_spec(tr, wl),
                   _vec_spec(wa), _vec_spec(wc), _vec_spec(wl), _vec_spec(wc), _vec_spec(wc)],
        out_shape=[jax.ShapeDtypeStruct((t, wa), F32), jax.ShapeDtypeStruct((t, wc), F32), jax.ShapeDtypeStruct((t, wl), F32),
                   jax.ShapeDtypeStruct((1, wa), F32), jax.ShapeDtypeStruct((1, wc), F32), jax.ShapeDtypeStruct((1, wl), F32),
                   jax.ShapeDtypeStruct((1, wc), F32), jax.ShapeDtypeStruct((1, wc), F32)],
        compiler_params=_params(1),
    )(dmixed, ya, yc, yl, cpre, ga, gc, gl, lng, lnb)


def _hi_lo(x):
    hi = x.astype(BF16)
    return hi, (x - hi.astype(F32)).astype(BF16)


def _lane_sums(x, tri, reverse):
    nsub = x.shape[1] // tri.shape[0]
    order = range(nsub - 1, -1, -1) if reverse else range(nsub)
    parts, beyond = {}, None
    for b in order:
        blk = x[:, b * tri.shape[0]:(b + 1) * tri.shape[0]]
        hi, lo = _hi_lo(blk)
        c = _dot(hi, tri) + _dot(lo, tri)
        parts[b] = c if beyond is None else c + beyond
        tot = jnp.sum(blk, axis=1, keepdims=True)
        beyond = tot if beyond is None else beyond + tot
    return jnp.concatenate([parts[b] for b in range(nsub)], axis=1), beyond


def _att_strip(qb, kt, thresh, scale, diff, tri_gt):
    z = _dot(qb, kt, "nt") * scale
    sp = _softplus(z)
    mask = diff < thresh
    later, total = _lane_sums(jnp.where(mask, -sp, 0.0), tri_gt, True)
    return z, sp, mask, total, (z - sp) + later


def _att_consts(tq, tk):
    diff = lax.broadcasted_iota(jnp.int32, (tq, tk), 1) - lax.broadcasted_iota(jnp.int32, (tq, tk), 0)
    cb = min(tk, ATT_SUM)
    row = lax.broadcasted_iota(jnp.int32, (cb, cb), 0)
    col = lax.broadcasted_iota(jnp.int32, (cb, cb), 1)
    return diff, (row > col).astype(BF16), (row < col).astype(BF16)


def attn_fwd(proj, n_heads):
    t = proj.shape[0]
    tq, tk = _tile(t, ATT_TQ), _tile(t, ATT_TK)
    hp = ATT_HEADS
    wd = hp * HEAD_DIM
    scale = HEAD_DIM ** -0.5

    def body(q_ref, k_ref, v_ref, o_ref, kb_ref, vb_ref, acc_ref):
        kb_ref[...] = k_ref[...].astype(BF16)
        vb_ref[...] = v_ref[...].astype(BF16)
        diff, tri_gt, _ = _att_consts(tq, tk)

        def qblock(i, _):
            q0 = pl.multiple_of(i * tq, tq)
            heads = [pl.ds(h * HEAD_DIM, HEAD_DIM) for h in range(hp)]
            qbs = [q_ref[pl.ds(q0, tq), hs].astype(BF16) for hs in heads]
            acc_ref[...] = jnp.zeros_like(acc_ref)
            n_strips = (q0 + tq + tk - 1) // tk

            def strip(jj, runs):
                k0 = pl.multiple_of((n_strips - 1 - jj) * tk, tk)
                out = []
                for h, hs in enumerate(heads):
                    _, _, mask, total, logw = _att_strip(qbs[h], kb_ref[pl.ds(k0, tk), hs], q0 - k0, scale, diff, tri_gt)
                    w = jnp.where(mask, jnp.exp(logw + runs[h]), 0.0)
                    acc_ref[:, hs] += _dot(w.astype(BF16), vb_ref[pl.ds(k0, tk), hs])
                    out.append(runs[h] + total)
                return tuple(out)

            lax.fori_loop(0, n_strips, strip, tuple(jnp.zeros((tq, 1), F32) for _ in range(hp)))
            o_ref[pl.ds(q0, tq), :] = acc_ref[...]
            return 0

        lax.fori_loop(0, t // tq, qblock, 0)

    def col_spec(base):
        return pl.BlockSpec((t, wd), lambda h, base=base: (0, base + h))

    ng = n_heads // hp
    return pl.pallas_call(
        body, name="attn_fwd", grid=(ng,),
        in_specs=[col_spec(0), col_spec(ng), col_spec(2 * ng)],
        out_specs=col_spec(0),
        out_shape=jax.ShapeDtypeStruct((t, n_heads * HEAD_DIM), F32),
        scratch_shapes=[pltpu.VMEM((t, wd), BF16), pltpu.VMEM((t, wd), BF16), pltpu.VMEM((tq, wd), F32)],
        compiler_params=_params(1),
    )(proj, proj, proj)


def attn_bwd(proj, dy, n_heads):
    t = proj.shape[0]
    tq, tk = _tile(t, ATT_TQ), _tile(t, ATT_TK)
    hp = ATT_HEADS
    wd = hp * HEAD_DIM
    scale = HEAD_DIM ** -0.5

    def body(q_ref, k_ref, v_ref, dy_ref, dq_ref, dk_ref, dv_ref, qb_ref, kb_ref, vb_ref, dob_ref, dk_acc, dv_acc,
             dq_acc, run_s):
        qb_ref[...] = q_ref[...].astype(BF16)
        kb_ref[...] = k_ref[...].astype(BF16)
        vb_ref[...] = v_ref[...].astype(BF16)
        dob_ref[...] = dy_ref[...].astype(BF16)
        dk_acc[...] = jnp.zeros_like(dk_acc)
        dv_acc[...] = jnp.zeros_like(dv_acc)
        diff, tri_gt, tri_lt = _att_consts(tq, tk)

        def qblock(i, _):
            q0 = pl.multiple_of(i * tq, tq)
            heads = [pl.ds(h * HEAD_DIM, HEAD_DIM) for h in range(hp)]
            qbs = [qb_ref[pl.ds(q0, tq), hs] for hs in heads]
            dobs = [dob_ref[pl.ds(q0, tq), hs] for hs in heads]
            dq_acc[...] = jnp.zeros_like(dq_acc)
            n_strips = (q0 + tq + tk - 1) // tk

            def sweep(jj, runs):
                si = n_strips - 1 - jj
                k0 = pl.multiple_of(si * tk, tk)
                out = []
                for h, hs in enumerate(heads):
                    sp = _softplus(_dot(qbs[h], kb_ref[pl.ds(k0, tk), hs], "nt") * scale)
                    run_s[h, si] = runs[h]
                    out.append(runs[h] + jnp.sum(jnp.where(diff < q0 - k0, -sp, 0.0), axis=1, keepdims=True))
                return tuple(out)

            zero = tuple(jnp.zeros((tq, 1), F32) for _ in range(hp))
            lax.fori_loop(0, n_strips, sweep, zero)

            def strip(si, gsums):
                k0 = pl.multiple_of(si * tk, tk)
                out = []
                for h, hs in enumerate(heads):
                    kt = kb_ref[pl.ds(k0, tk), hs]
                    vt = vb_ref[pl.ds(k0, tk), hs]
                    z, sp, mask, _, logw = _att_strip(qbs[h], kt, q0 - k0, scale, diff, tri_gt)
                    w = jnp.where(mask, jnp.exp(logw + run_s[h, si]), 0.0)
                    g = w * _dot(dobs[h], vt, "nt")
                    before, gtot = _lane_sums(g, tri_lt, False)
                    sig = jnp.exp(z - sp)
                    dz = jnp.where(mask, g * (1.0 - sig) - (before + gsums[h]) * sig, 0.0) * scale
                    dzb = dz.astype(BF16)
                    dk_acc[pl.ds(k0, tk), hs] += _dot(dzb, qbs[h], "tn")
                    dv_acc[pl.ds(k0, tk), hs] += _dot(w.astype(BF16), dobs[h], "tn")
                    dq_acc[:, hs] += _dot(dzb, kt)
                    out.append(gsums[h] + gtot)
                return tuple(out)

            lax.fori_loop(0, n_strips, strip, zero)
            dq_ref[pl.ds(q0, tq), :] = dq_acc[...].astype(BF16)
            return 0

        lax.fori_loop(0, t // tq, qblock, 0)
        dk_ref[...] = dk_acc[...].astype(BF16)
        dv_ref[...] = dv_acc[...].astype(BF16)

    def col_spec(base):
        return pl.BlockSpec((t, wd), lambda h, base=base: (0, base + h))

    ng = n_heads // hp
    return pl.pallas_call(
        body, name="attn_bwd", grid=(ng,),
        in_specs=[col_spec(0), col_spec(ng), col_spec(2 * ng), col_spec(0)],
        out_specs=[col_spec(0), col_spec(0), col_spec(0)],
        out_shape=[jax.ShapeDtypeStruct((t, n_heads * HEAD_DIM), BF16)] * 3,
        scratch_shapes=[pltpu.VMEM((t, wd), BF16)] * 4 + [pltpu.VMEM((t, wd), F32)] * 2
        + [pltpu.VMEM((tq, wd), F32), pltpu.VMEM((hp, t // tk, tq, 1), F32)],
        compiler_params=_params(1),
    )(proj, proj, proj, dy)


def _glu_halo(vc, gc, vp, gp, ubuf, i, tt, halo):
    uprev = vp[pl.ds(tt - halo, halo), :] * _sigmoid(gp[pl.ds(tt - halo, halo), :])
    ubuf[pl.ds(0, halo), :] = jnp.where(i > 0, uprev, 0.0)
    ubuf[pl.ds(halo, tt), :] = vc[...] * _sigmoid(gc[...])


def conv_fwd(proj, col0, cc, w, b, lng, lnb):
    t = proj.shape[0]
    tt = _tile(t, ROW_T)
    vi, gi = col0 // cc, col0 // cc + 1
    off = CONV_HALO - (DW_LEN - 1)

    def body(vc, gc, vp, gp, w_ref, b_ref, lg_ref, lb_ref, c_ref, y_ref, ubuf):
        i = pl.program_id(0)
        _glu_halo(vc, gc, vp, gp, ubuf, i, tt, CONV_HALO)
        for ch in range(cc // LANE):
            sl = pl.ds(ch * LANE, LANE)
            acc = jnp.zeros((tt, LANE), F32) + b_ref[:, sl]
            for tap in range(DW_LEN):
                acc = acc + w_ref[pl.ds(tap, 1), sl] * ubuf[pl.ds(off + tap, tt), sl]
            c_ref[:, sl] = acc
        c = c_ref[...]
        xc = c - jnp.mean(c, axis=-1, keepdims=True)
        ln = xc * lax.rsqrt(jnp.mean(xc * xc, axis=-1, keepdims=True) + EPS) * lg_ref[...] + lb_ref[...]
        y_ref[...] = ln * _sigmoid(ln)

    cur = lambda c: pl.BlockSpec((tt, cc), lambda i, c=c: (i, c))
    prev = lambda c: pl.BlockSpec((tt, cc), lambda i, c=c: (jnp.maximum(i - 1, 0), c))
    return pl.pallas_call(
        body, name="conv_fwd", grid=(t // tt,),
        in_specs=[cur(vi), cur(gi), prev(vi), prev(gi), pl.BlockSpec((DW_LEN, cc), lambda i: (0, 0)),
                  _vec_spec(cc), _vec_spec(cc), _vec_spec(cc)],
        out_specs=[_row_spec(tt, cc), _row_spec(tt, cc)],
        out_shape=[jax.ShapeDtypeStruct((t, cc), F32)] * 2,
        scratch_shapes=[pltpu.VMEM((CONV_HALO + tt, cc), F32)],
        compiler_params=_params(1),
    )(proj, proj, proj, proj, w, b, lng, lnb)


def conv_bwd(proj, col0, cc, dc, w):
    t = proj.shape[0]
    tt = _tile(t, ROW_T)
    nt = t // tt
    vi, gi = col0 // cc, col0 // cc + 1
    off = CONV_HALO - (DW_LEN - 1)

    def body(vc, gc, vp, gp, dcc, dcn, w_ref, dvg_ref, dw_ref, db_ref, ubuf, dbuf):
        i = pl.program_id(0)

        @pl.when(i == 0)
        def _():
            dw_ref[...] = jnp.zeros_like(dw_ref)
            db_ref[...] = jnp.zeros_like(db_ref)

        _glu_halo(vc, gc, vp, gp, ubuf, i, tt, CONV_HALO)
        dbuf[pl.ds(0, tt), :] = dcc[...]
        dbuf[pl.ds(tt, CONV_HALO), :] = jnp.where(i < nt - 1, dcn[pl.ds(0, CONV_HALO), :], 0.0)
        db_ref[...] += jnp.sum(dcc[...], axis=0, keepdims=True)
        for ch in range(cc // LANE):
            sl = pl.ds(ch * LANE, LANE)
            dcv = dbuf[pl.ds(0, tt), sl]
            du = jnp.zeros((tt, LANE), F32)
            for tap in range(DW_LEN):
                du = du + w_ref[pl.ds(tap, 1), sl] * dbuf[pl.ds(DW_LEN - 1 - tap, tt), sl]
                dw_ref[pl.ds(tap, 1), sl] += jnp.sum(dcv * ubuf[pl.ds(off + tap, tt), sl], axis=0, keepdims=True)
            s = _sigmoid(gc[:, sl])
            val = vc[:, sl]
            dvg_ref[:, sl] = (du * s).astype(BF16)
            dvg_ref[:, pl.ds(cc + ch * LANE, LANE)] = (du * val * s * (1.0 - s)).astype(BF16)

    cur = lambda c: pl.BlockSpec((tt, cc), lambda i, c=c: (i, c))
    prev = lambda c: pl.BlockSpec((tt, cc), lambda i, c=c: (jnp.maximum(i - 1, 0), c))
    return pl.pallas_call(
        body, name="conv_bwd", grid=(nt,),
        in_specs=[cur(vi), cur(gi), prev(vi), prev(gi), _row_spec(tt, cc),
                  pl.BlockSpec((tt, cc), lambda i: (jnp.minimum(i + 1, nt - 1), 0)),
                  pl.BlockSpec((DW_LEN, cc), lambda i: (0, 0))],
        out_specs=[_row_spec(tt, 2 * cc), pl.BlockSpec((DW_LEN, cc), lambda i: (0, 0)), _vec_spec(cc)],
        out_shape=[jax.ShapeDtypeStruct((t, 2 * cc), BF16), jax.ShapeDtypeStruct((DW_LEN, cc), F32),
                   jax.ShapeDtypeStruct((1, cc), F32)],
        scratch_shapes=[pltpu.VMEM((CONV_HALO + tt, cc), F32), pltpu.VMEM((tt + CONV_HALO, cc), F32)],
        compiler_params=_params(1),
    )(proj, proj, proj, proj, dc, dc, w)


def _lru_gates(xbuf, cw_ref, cb_ref, wa_ref, ba_ref, wi_ref, bi_ref, lam_ref, tt, wl):
    bd = wl // LRU_BLOCKS
    xr = jnp.zeros((tt, wl), F32) + cb_ref[...]
    for tap in range(LRU_LEN):
        xr = xr + cw_ref[pl.ds(tap, 1), :] * xbuf[pl.ds(LRU_HALO - (LRU_LEN - 1) + tap, tt), :]
    xb = xr.astype(BF16)
    ga = jnp.concatenate([_dot(xb[:, n * bd:(n + 1) * bd], wa_ref[n]) for n in range(LRU_BLOCKS)], axis=1) + ba_ref[...]
    gi = jnp.concatenate([_dot(xb[:, n * bd:(n + 1) * bd], wi_ref[n]) for n in range(LRU_BLOCKS)], axis=1) + bi_ref[...]
    r = _sigmoid(ga)
    ig = _sigmoid(gi)
    spl = _softplus(-lam_ref[...])
    log_a = -LRU_C * r * spl
    a = jnp.exp(log_a)
    m = jnp.sqrt(_neg_expm1(2.0 * log_a))
    return xr, xb, r, ig, spl, a, m


def _group_scan(a8, b8, reverse):
    rid = lax.broadcasted_iota(jnp.int32, a8.shape, 0)
    aa, bb = a8, b8
    for dist in (1, 2, 4):
        shift = SUBLANE - dist if reverse else dist
        a_sh = pltpu.roll(aa, shift, 0)
        b_sh = pltpu.roll(bb, shift, 0)
        valid = (rid < SUBLANE - dist) if reverse else (rid >= dist)
        bb = jnp.where(valid, aa * b_sh + bb, bb)
        aa = jnp.where(valid, aa * a_sh, aa)
    return aa, bb


def _pick_row(x8, r):
    rid = lax.broadcasted_iota(jnp.int32, x8.shape, 0)
    return jnp.sum(jnp.where(rid == r, x8, 0.0), axis=0, keepdims=True)


def lru_fwd(proj, col0, wl, cw, cb, wa, ba, wi, bi, lam):
    t = proj.shape[0]
    tt = _tile(t, ROW_T)
    xi, yi = col0 // wl, col0 // wl + 1

    def body(xc, xp, ry, cw_ref, cb_ref, wa_ref, ba_ref, wi_ref, bi_ref, lam_ref, hs_ref, y_ref,
             xbuf, a_s, b_s, hcar):
        i = pl.program_id(0)

        @pl.when(i == 0)
        def _():
            hcar[...] = jnp.zeros_like(hcar)

        xbuf[pl.ds(0, LRU_HALO), :] = jnp.where(i > 0, xp[pl.ds(tt - LRU_HALO, LRU_HALO), :], 0.0)
        xbuf[pl.ds(LRU_HALO, tt), :] = xc[...]
        xr, _, _, ig, _, a, m = _lru_gates(xbuf, cw_ref, cb_ref, wa_ref, ba_ref, wi_ref, bi_ref, lam_ref, tt, wl)
        a_s[...] = a
        b_s[...] = m * ig * xr

        def group(gidx, h):
            r0 = pl.multiple_of(gidx * SUBLANE, SUBLANE)
            aa, bb = _group_scan(a_s[pl.ds(r0, SUBLANE), :], b_s[pl.ds(r0, SUBLANE), :], False)
            h8 = aa * h + bb
            hs_ref[pl.ds(r0, SUBLANE), :] = h8
            return _pick_row(h8, SUBLANE - 1)

        hcar[...] = lax.fori_loop(0, tt // SUBLANE, group, hcar[...])
        gel, _ = _gelu_and_grad(ry[...])
        y_ref[...] = hs_ref[...] * gel

    cur = lambda c: pl.BlockSpec((tt, wl), lambda i, c=c: (i, c))
    full = lambda shape: pl.BlockSpec(shape, lambda i: (0,) * len(shape))
    return pl.pallas_call(
        body, name="lru_fwd", grid=(t // tt,),
        in_specs=[cur(xi), pl.BlockSpec((tt, wl), lambda i: (jnp.maximum(i - 1, 0), xi)), cur(yi),
                  full((LRU_LEN, wl)), _vec_spec(wl), full(wa.shape), _vec_spec(wl), full(wi.shape), _vec_spec(wl),
                  _vec_spec(wl)],
        out_specs=[_row_spec(tt, wl), _row_spec(tt, wl)],
        out_shape=[jax.ShapeDtypeStruct((t, wl), F32)] * 2,
        scratch_shapes=[pltpu.VMEM((LRU_HALO + tt, wl), F32), pltpu.VMEM((tt, wl), F32), pltpu.VMEM((tt, wl), F32),
                        pltpu.VMEM((1, wl), F32)],
        compiler_params=_params(1),
    )(proj, proj, proj, cw, cb, wa, ba, wi, bi, lam)


def lru_bwd(proj, col0, wl, hs, dy, cw, cb, wa, ba, wi, bi, lam):
    t = proj.shape[0]
    tt = _tile(t, ROW_T)
    nt = t // tt
    xi, yi = col0 // wl, col0 // wl + 1
    bd = wl // LRU_BLOCKS

    def body(xc, xp, ry, hc, hp, dy_ref, cw_ref, cb_ref, wa_ref, ba_ref, wi_ref, bi_ref, lam_ref,
             dxy_ref, dcw_ref, dcb_ref, dwa_ref, dba_ref, dwi_ref, dbi_ref, dlam_ref,
             xbuf, hbuf, abuf, e_s, dh_s, dxbuf, dhcar):
        i = pl.program_id(0)
        first = i == 0

        @pl.when(first)
        def _():
            for r in (dcw_ref, dcb_ref, dwa_ref, dba_ref, dwi_ref, dbi_ref, dlam_ref, dhcar):
                r[...] = jnp.zeros_like(r)
            abuf[pl.ds(tt, LRU_HALO), :] = jnp.zeros((LRU_HALO, wl), F32)
            dxbuf[pl.ds(tt, LRU_HALO), :] = jnp.zeros((LRU_HALO, wl), F32)

        has_prev = i < nt - 1
        xbuf[pl.ds(0, LRU_HALO), :] = jnp.where(has_prev, xp[pl.ds(tt - LRU_HALO, LRU_HALO), :], 0.0)
        xbuf[pl.ds(LRU_HALO, tt), :] = xc[...]
        hbuf[pl.ds(0, LRU_HALO), :] = jnp.where(has_prev, hp[pl.ds(tt - LRU_HALO, LRU_HALO), :], 0.0)
        hbuf[pl.ds(LRU_HALO, tt), :] = hc[...]
        xr, xb, r, ig, spl, a, m = _lru_gates(xbuf, cw_ref, cb_ref, wa_ref, ba_ref, wi_ref, bi_ref, lam_ref, tt, wl)
        gel, dgel = _gelu_and_grad(ry[...])
        dyv = dy_ref[...]
        e_s[...] = dyv * gel
        dxy_ref[:, pl.ds(wl, wl)] = (dyv * hc[...] * dgel).astype(BF16)
        abuf[pl.ds(0, tt), :] = a
        a_next = abuf[pl.ds(1, tt), :]
        dh_s[...] = a_next

        def group(it, dh_in):
            r0 = pl.multiple_of((tt // SUBLANE - 1 - it) * SUBLANE, SUBLANE)
            aa, bb = _group_scan(dh_s[pl.ds(r0, SUBLANE), :], e_s[pl.ds(r0, SUBLANE), :], True)
            dh8 = aa * dh_in + bb
            dh_s[pl.ds(r0, SUBLANE), :] = dh8
            return _pick_row(dh8, 0)

        dhcar[...] = lax.fori_loop(0, tt // SUBLANE, group, dhcar[...])
        abuf[pl.ds(tt, LRU_HALO), :] = a[0:LRU_HALO, :]
        dh = dh_s[...]
        h_m1 = hbuf[pl.ds(LRU_HALO - 1, tt), :]
        dlog_a = dh * h_m1 * a - dh * ig * xr * (a * a / m)
        dig = dh * m * xr
        dxr = dh * m * ig
        dga = dlog_a * (-LRU_C) * spl * r * (1.0 - r)
        dgi = dig * ig * (1.0 - ig)
        dlam_ref[...] += jnp.sum(dlog_a * r, axis=0, keepdims=True) * (LRU_C * _sigmoid(-lam_ref[...]))
        dba_ref[...] += jnp.sum(dga, axis=0, keepdims=True)
        dbi_ref[...] += jnp.sum(dgi, axis=0, keepdims=True)
        dgab = dga.astype(BF16)
        dgib = dgi.astype(BF16)
        back = []
        for n in range(LRU_BLOCKS):
            sl = slice(n * bd, (n + 1) * bd)
            dwa_ref[n] += _dot(xb[:, sl], dgab[:, sl], "tn")
            dwi_ref[n] += _dot(xb[:, sl], dgib[:, sl], "tn")
            back.append(_dot(dgab[:, sl], wa_ref[n], "nt") + _dot(dgib[:, sl], wi_ref[n], "nt"))
        dxr = dxr + jnp.concatenate(back, axis=1)
        dcb_ref[...] += jnp.sum(dxr, axis=0, keepdims=True)
        dxbuf[pl.ds(0, tt), :] = dxr
        drx = jnp.zeros((tt, wl), F32)
        for tap in range(LRU_LEN):
            drx = drx + cw_ref[pl.ds(tap, 1), :] * dxbuf[pl.ds(LRU_LEN - 1 - tap, tt), :]
            dcw_ref[pl.ds(tap, 1), :] += jnp.sum(
                dxr * xbuf[pl.ds(LRU_HALO - (LRU_LEN - 1) + tap, tt), :], axis=0, keepdims=True)
        dxbuf[pl.ds(tt, LRU_HALO), :] = dxr[0:LRU_HALO, :]
        dxy_ref[:, pl.ds(0, wl)] = drx.astype(BF16)

    rev = lambda c: pl.BlockSpec((tt, wl), lambda i, c=c: (nt - 1 - i, c))
    rev_prev = lambda c: pl.BlockSpec((tt, wl), lambda i, c=c: (jnp.maximum(nt - 2 - i, 0), c))
    full = lambda shape: pl.BlockSpec(shape, lambda i: (0,) * len(shape))
    vec = _vec_spec(wl)
    return pl.pallas_call(
        body, name="lru_bwd", grid=(nt,),
        in_specs=[rev(xi), rev_prev(xi), rev(yi), rev(0), rev_prev(0), rev(0),
                  full((LRU_LEN, wl)), vec, full(wa.shape), vec, full(wi.shape), vec, vec],
        out_specs=[pl.BlockSpec((tt, 2 * wl), lambda i: (nt - 1 - i, 0)), full((LRU_LEN, wl)), vec,
                   full(wa.shape), vec, full(wi.shape), vec, vec],
        out_shape=[jax.ShapeDtypeStruct((t, 2 * wl), BF16), jax.ShapeDtypeStruct((LRU_LEN, wl), F32),
                   jax.ShapeDtypeStruct((1, wl), F32), jax.ShapeDtypeStruct(wa.shape, F32),
                   jax.ShapeDtypeStruct((1, wl), F32), jax.ShapeDtypeStruct(wi.shape, F32),
                   jax.ShapeDtypeStruct((1, wl), F32), jax.ShapeDtypeStruct((1, wl), F32)],
        scratch_shapes=[pltpu.VMEM((LRU_HALO + tt, wl), F32), pltpu.VMEM((LRU_HALO + tt, wl), F32),
                        pltpu.VMEM((tt + LRU_HALO, wl), F32), pltpu.VMEM((tt, wl), F32), pltpu.VMEM((tt, wl), F32),
                        pltpu.VMEM((tt + LRU_HALO, wl), F32), pltpu.VMEM((1, wl), F32)],
        compiler_params=_params(1),
    )(proj, proj, proj, hs, hs, dy, cw, cb, wa, ba, wi, bi, lam)


def _adamw(w, g, m, v):
    m = ADAM_B1 * m + (1.0 - ADAM_B1) * g
    v = ADAM_B2 * v + (1.0 - ADAM_B2) * (g * g)
    m_hat = m / (1.0 - ADAM_B1 ** ADAM_STEP)
    v_hat = v / (1.0 - ADAM_B2 ** ADAM_STEP)
    delta = -ADAM_LR * (m_hat / (jnp.sqrt(v_hat) + ADAM_EPS) + ADAM_WD * w)
    return delta, m, v


def adam_big(name, w, m, v, parts, chip):
    n_layers, rows, cols = w.shape
    tr = _tile(rows, 128 if cols > 1024 else 256)
    nrt = rows // tr

    def body(chip_ref, *refs):
        w_ref, m_ref, v_ref = refs[:3]
        part_refs = refs[3:3 + 4 * n_layers]
        g_ref, d_ref, mo_ref, vo_ref = refs[3 + 4 * n_layers:]
        layer = pl.program_id(0)
        for l in range(n_layers):
            @pl.when(layer == l)
            def _(l=l):
                g = part_refs[4 * l][...].astype(F32)
                for p in range(1, 4):
                    g = g + part_refs[4 * l + p][...].astype(F32)
                delta, mn, vn = _adamw(w_ref[...], g, m_ref[...], v_ref[...])
                g_ref[...] = g
                d_ref[...] = delta
                mo_ref[...] = mn
                vo_ref[...] = vn

    wspec = pl.BlockSpec((None, tr, cols), lambda l, i, chip_ref: (l, i, 0))
    operands, in_specs = [w, m, v], [wspec, wspec, wspec]
    for l in range(n_layers):
        mine, recv = parts[l]
        operands.append(mine)
        in_specs.append(pl.BlockSpec(
            (None, tr, cols), lambda ll, i, chip_ref, l=l: (chip_ref[0], jnp.where(ll == l, i, 0), 0)))
        for p in range(3):
            operands.append(recv)
            in_specs.append(pl.BlockSpec(
                (None, tr, cols), lambda ll, i, chip_ref, l=l, p=p: (p, jnp.where(ll == l, i, 0), 0)))
    return pl.pallas_call(
        body, name=name,
        grid_spec=pltpu.PrefetchScalarGridSpec(
            num_scalar_prefetch=1, grid=(n_layers, nrt), in_specs=in_specs, out_specs=[wspec] * 4),
        out_shape=[jax.ShapeDtypeStruct(w.shape, F32)] * 4, compiler_params=_params(2),
    )(chip, *operands)


def adam_small(w, m, v, g):
    rows = w.shape[0]
    tr = _tile(rows, PACK_ROWS)

    def body(w_ref, m_ref, v_ref, g_ref, d_ref, mo_ref, vo_ref):
        delta, mn, vn = _adamw(w_ref[...], g_ref[...], m_ref[...], v_ref[...])
        d_ref[...] = delta
        mo_ref[...] = mn
        vo_ref[...] = vn

    spec = _row_spec(tr, LANE)
    return pl.pallas_call(
        body, name="adam_small", grid=(rows // tr,), in_specs=[spec] * 4, out_specs=[spec] * 3,
        out_shape=[jax.ShapeDtypeStruct(w.shape, F32)] * 3, compiler_params=_params(1),
    )(w, m, v, g)


def sum_parts(parts):
    _, rows, _ = parts.shape
    tr = _tile(rows, PACK_ROWS)

    def body(p_ref, o_ref):
        acc = p_ref[0]
        for k in range(1, N_DEV):
            acc = acc + p_ref[k]
        o_ref[...] = acc

    return pl.pallas_call(
        body, name="sum_parts", grid=(rows // tr,),
        in_specs=[pl.BlockSpec((N_DEV, tr, LANE), lambda i: (0, i, 0))], out_specs=_row_spec(tr, LANE),
        out_shape=jax.ShapeDtypeStruct((rows, LANE), F32), compiler_params=_params(1),
    )(parts)


def pair_add(g, recv, core):
    n, rows, cols = recv.shape
    tr = _tile(rows, 256)

    def body(core_ref, a_ref, b_ref, o_ref):
        o_ref[...] = (a_ref[...].astype(F32) + b_ref[...].astype(F32)).astype(BF16)

    spec = pl.BlockSpec((None, tr, cols), lambda k, i, core_ref: (k, i, 0))
    return pl.pallas_call(
        body, name="pair_add",
        grid_spec=pltpu.PrefetchScalarGridSpec(
            num_scalar_prefetch=1, grid=(n, rows // tr),
            in_specs=[pl.BlockSpec((None, tr, cols), lambda k, i, core_ref: (2 * k + core_ref[0], i, 0)), spec],
            out_specs=spec),
        out_shape=jax.ShapeDtypeStruct(recv.shape, BF16), compiler_params=_params(2),
    )(core, g, recv)


def place_own(x, me, dtype):
    rows, cols = x.shape
    tr = _tile(rows, 256)

    def body(me_ref, x_ref, o_ref):
        o_ref[...] = x_ref[...].astype(dtype)

    return pl.pallas_call(
        body, name="place_own",
        grid_spec=pltpu.PrefetchScalarGridSpec(
            num_scalar_prefetch=1, grid=(rows // tr,),
            in_specs=[pl.BlockSpec((tr, cols), lambda i, me_ref: (i, 0))],
            out_specs=pl.BlockSpec((None, tr, cols), lambda i, me_ref: (me_ref[0], i, 0))),
        out_shape=jax.ShapeDtypeStruct((N_DEV, rows, cols), dtype), compiler_params=_params(1),
    )(me, x)


_HBM = pl.BlockSpec(memory_space=pltpu.HBM)


def _place():
    return lax.axis_index("x"), lax.axis_index("y"), lax.axis_index("c")


def _other_chips(x, y):
    return [(1 - x, y), (x, 1 - y), (1 - x, 1 - y)]


def all_gather(name, shard, me, dtype=None):
    def body(buf_ref, out_ref, send_sems, recv_sems):
        del buf_ref
        x, y, c = _place()
        mine, sibling = (x, y, c), (x, y, 1 - c)
        chips = _other_chips(x, y)

        def copy(k, block, to):
            slot = out_ref.at[4 * block[0] + 2 * block[1] + block[2]]
            return pltpu.make_async_remote_copy(
                src_ref=slot, dst_ref=slot, send_sem=send_sems.at[k], recv_sem=recv_sems.at[k],
                device_id=to, device_id_type=pl.DeviceIdType.MESH)

        first = [copy(0, mine, sibling)] + [copy(1 + j, mine, (*chip, c)) for j, chip in enumerate(chips)]
        for cp in first:
            cp.start()
        passed = [copy(4 + j, (*chip, c), sibling) for j, chip in enumerate(chips)]
        for j, chip in enumerate(chips):
            copy(1 + j, (*chip, c), mine).wait_recv()
            passed[j].start()
        copy(0, sibling, mine).wait_recv()
        for j, chip in enumerate(chips):
            copy(4 + j, (*chip, 1 - c), mine).wait_recv()
        for cp in first + passed:
            cp.wait_send()

    buf = place_own(shard, me, dtype or shard.dtype)
    return pl.pallas_call(
        body, name=name, out_shape=jax.ShapeDtypeStruct(buf.shape, buf.dtype),
        in_specs=[_HBM], out_specs=_HBM, input_output_aliases={0: 0},
        scratch_shapes=[pltpu.SemaphoreType.DMA((7,)), pltpu.SemaphoreType.DMA((7,))],
    )(buf)


def _own_block_copies(src_refs, dst_refs, send_sems, recv_sems, arrivals):
    x, y, c = _place()
    peers = [(x, y, 1 - c)] + [(*chip, c) for chip in _other_chips(x, y)]
    copies = []
    for b, (src, dst) in enumerate(zip(src_refs, dst_refs)):
        for k, peer in enumerate(peers):
            def copy(landing, b=b, k=k, peer=peer, src=src, dst=dst):
                return pltpu.make_async_remote_copy(
                    src_ref=src.at[4 * x + 2 * y + c], dst_ref=dst.at[landing],
                    send_sem=send_sems.at[4 * b + k], recv_sem=recv_sems.at[4 * b + k],
                    device_id=peer, device_id_type=pl.DeviceIdType.MESH)
            copies.append((copy(4 * x + 2 * y + c), copy(4 * peer[0] + 2 * peer[1] + peer[2]) if arrivals else None))
    return copies


def gather_start(name, bufs, after):
    n = len(bufs)

    def body(*refs):
        send_sems, recv_sems = refs[n + 1], refs[n + 2]
        thru = refs[n + 3:2 * n + 3]
        for send, _ in _own_block_copies(thru, thru, send_sems, recv_sems, False):
            send.start()
        refs[2 * n + 3][...] = jnp.zeros((SUBLANE, LANE), F32)

    return pl.pallas_call(
        body, name=name,
        out_shape=(pltpu.SemaphoreType.DMA((4 * n,)), pltpu.SemaphoreType.DMA((4 * n,)),
                   *[pltpu.HBM(b.shape, b.dtype) for b in bufs], jax.ShapeDtypeStruct((SUBLANE, LANE), F32)),
        in_specs=(*(_HBM,) * n, _ANY), out_specs=(_SEM, _SEM, *(_HBM,) * n, _TOKEN),
        input_output_aliases={b: 2 + b for b in range(n)},
        compiler_params=pltpu.CompilerParams(has_side_effects=_EFFECT),
    )(*[_hbm(b) for b in bufs], after)


def gather_wait(name, state, first, count, after):
    send_sems, recv_sems = state[:2]
    bufs = state[2 + first:2 + first + count]
    n = len(bufs)

    def body(*refs):
        ins = refs[:n]
        send_sems, recv_sems = refs[n], refs[n + 1]
        shift = 4 * first
        for send, arrival in _own_block_copies(
                ins, ins, send_sems.at[pl.ds(shift, 4 * n)], recv_sems.at[pl.ds(shift, 4 * n)], True):
            send.wait_send()
            arrival.wait_recv()

    return pl.pallas_call(
        body, name=name, out_shape=tuple(pltpu.HBM(b.shape, b.dtype) for b in bufs),
        in_specs=(*(_HBM,) * n, _SEM, _SEM, _ANY), out_specs=(_HBM,) * n,
        input_output_aliases={b: b for b in range(n)},
        compiler_params=pltpu.CompilerParams(has_side_effects=_EFFECT),
    )(*bufs, send_sems, recv_sems, after)


def gather_finish(name, bufs):
    n = len(bufs)

    def body(*refs):
        outs = refs[n:2 * n]
        send_sems, recv_sems = refs[2 * n], refs[2 * n + 1]
        x, y, c = _place()
        copies = []
        for b, out in enumerate(outs):
            for k, chip in enumerate(_other_chips(x, y)):
                sem = 3 * b + k
                copies.append((
                    pltpu.make_async_remote_copy(
                        src_ref=out.at[4 * chip[0] + 2 * chip[1] + c], dst_ref=out.at[4 * chip[0] + 2 * chip[1] + c],
                        send_sem=send_sems.at[sem], recv_sem=recv_sems.at[sem],
                        device_id=(x, y, 1 - c), device_id_type=pl.DeviceIdType.MESH),
                    pltpu.make_async_remote_copy(
                        src_ref=out.at[4 * chip[0] + 2 * chip[1] + c], dst_ref=out.at[4 * chip[0] + 2 * chip[1] + 1 - c],
                        send_sem=send_sems.at[sem], recv_sem=recv_sems.at[sem],
                        device_id=(x, y, 1 - c), device_id_type=pl.DeviceIdType.MESH)))
        for send, _ in copies:
            send.start()
        for send, arrival in copies:
            send.wait_send()
            arrival.wait_recv()

    return pl.pallas_call(
        body, name=name, out_shape=tuple(jax.ShapeDtypeStruct(b.shape, b.dtype) for b in bufs),
        in_specs=(_HBM,) * n, out_specs=(_HBM,) * n, input_output_aliases={b: b for b in range(n)},
        scratch_shapes=[pltpu.SemaphoreType.DMA((3 * n,)), pltpu.SemaphoreType.DMA((3 * n,))],
    )(*bufs)


def scatter_pair(name, g):
    _, rows, cols = g.shape

    def body(g_ref, recv_ref, send_sems, recv_sems):
        x, y, c = _place()
        remote = [pltpu.make_async_remote_copy(
            src_ref=g_ref.at[2 * k + 1 - c], dst_ref=recv_ref.at[k], send_sem=send_sems.at[k], recv_sem=recv_sems.at[k],
            device_id=(x, y, 1 - c), device_id_type=pl.DeviceIdType.MESH) for k in range(4)]
        for cp in remote:
            cp.start()
        for cp in remote:
            cp.wait()

    return pl.pallas_call(
        body, name=name, out_shape=jax.ShapeDtypeStruct((4, rows, cols), g.dtype), in_specs=[_HBM], out_specs=_HBM,
        scratch_shapes=[pltpu.SemaphoreType.DMA((4,)), pltpu.SemaphoreType.DMA((4,))],
    )(g)


_SEM = pl.BlockSpec(memory_space=pltpu.SEMAPHORE)
_ANY = pl.BlockSpec(memory_space=pl.ANY)
_TOKEN = pl.BlockSpec(memory_space=pltpu.VMEM)
_EFFECT = pltpu.SideEffectType.DATAFLOW_SIDE_EFFECTING


def _hbm(a):
    return pltpu.with_memory_space_constraint(a, pltpu.HBM)


def _chip_copies(p_ref, land_ref, send_sems, recv_sems):
    x, y, c = _place()
    return [pltpu.make_async_remote_copy(
        src_ref=p_ref.at[2 * px + py], dst_ref=land_ref.at[k], send_sem=send_sems.at[k], recv_sem=recv_sems.at[k],
        device_id=(px, py, c), device_id_type=pl.DeviceIdType.MESH) for k, (px, py) in enumerate(_other_chips(x, y))]


def scatter_chips_start(name, p):
    _, rows, cols = p.shape

    def body(p_ref, land_ref, send_sems, recv_sems, p_thru, land_thru, token):
        for cp in _chip_copies(p_ref, land_ref, send_sems, recv_sems):
            cp.start()
        token[...] = jnp.zeros_like(token)

    return pl.pallas_call(
        body, name=name,
        out_shape=(pltpu.SemaphoreType.DMA((3,)), pltpu.SemaphoreType.DMA((3,)), pltpu.HBM(p.shape, p.dtype),
                   pltpu.HBM((3, rows, cols), p.dtype), jax.ShapeDtypeStruct((SUBLANE, LANE), F32)),
        in_specs=(_HBM, _HBM), out_specs=(_SEM, _SEM, _HBM, _HBM, _TOKEN), input_output_aliases={0: 2, 1: 3},
        compiler_params=pltpu.CompilerParams(has_side_effects=_EFFECT),
    )(_hbm(p), _hbm(lax.empty((3, rows, cols), p.dtype)))


def scatter_chips_wait(name, send_sems, recv_sems, p_thru, land_thru, after):
    def body(p_ref, land_ref, send_sems, recv_sems, after_ref, p_out, land_out):
        for cp in _chip_copies(p_ref, land_ref, send_sems, recv_sems):
            cp.wait_send()
            cp.wait_recv()

    return pl.pallas_call(
        body, name=name,
        out_shape=(pltpu.HBM(p_thru.shape, p_thru.dtype), pltpu.HBM(land_thru.shape, land_thru.dtype)),
        in_specs=(_HBM, _HBM, _SEM, _SEM, _ANY), out_specs=(_HBM, _HBM), input_output_aliases={0: 0, 1: 1},
        compiler_params=pltpu.CompilerParams(has_side_effects=_EFFECT),
    )(p_thru, land_thru, send_sems, recv_sems, after)


def reduce_scatter_start(tag, g, core):
    p = pair_add(g, scatter_pair("rs_pair_" + tag, g), core)
    return scatter_chips_start("rs_start_" + tag, p)


def reduce_scatter_wait(tag, state, after):
    send_sems, recv_sems, p_thru, land_thru, _ = state
    return scatter_chips_wait("rs_wait_" + tag, send_sems, recv_sems, p_thru, land_thru, after)


_SMALL = ("g_pre_mix", "g_post_mix", "g_pre_ffn", "g_post_ffn", "g_attn_grp", "g_conv_grp", "g_lru_grp",
          "dw_conv_w", "dw_conv_b", "conv_ln_g", "conv_ln_b", "lru_conv_w", "lru_conv_b",
          "lru_w_a", "lru_b_a", "lru_w_i", "lru_b_i", "lru_lambda")
_COL_SHARDED_SMALL = ("dw_conv_w", "lru_conv_w")
_BIG = ("w_in", "w_out", "w_gate", "w_up", "w_down")
_TRANSPOSED = ("w_gate", "w_up")
_ALL = ("w_in", "w_out", "g_pre_mix", "g_post_mix", "g_pre_ffn", "g_post_ffn", "g_attn_grp", "g_conv_grp", "g_lru_grp",
        "dw_conv_w", "dw_conv_b", "conv_ln_g", "conv_ln_b", "lru_conv_w", "lru_conv_b", "lru_w_a", "lru_b_a",
        "lru_w_i", "lru_b_i", "lru_lambda", "w_gate", "w_up", "w_down")


def _pack(arrays):
    flat = jnp.concatenate([a.reshape(-1) for a in arrays])
    pad = (-flat.shape[0]) % (PACK_ROWS * LANE)
    return jnp.pad(flat, (0, pad)).reshape(-1, LANE)


def _unpack(packed, shapes):
    flat = packed.reshape(-1)
    out, pos = [], 0
    for s in shapes:
        n = math.prod(s)
        out.append(flat[pos:pos + n].reshape(s))
        pos += n
    return out


def kernel(x, w_in, w_out, g_pre_mix, g_post_mix, g_pre_ffn, g_post_ffn, g_attn_grp, g_conv_grp, g_lru_grp, dw_conv_w, dw_conv_b, conv_ln_g, conv_ln_b, lru_conv_w, lru_conv_b, lru_w_a, lru_b_a, lru_w_i, lru_b_i, lru_lambda, w_gate, w_up, w_down, loss_target, m_w_in, m_w_out, m_g_pre_mix, m_g_post_mix, m_g_pre_ffn, m_g_post_ffn, m_g_attn_grp, m_g_conv_grp, m_g_lru_grp, m_dw_conv_w, m_dw_conv_b, m_conv_ln_g, m_conv_ln_b, m_lru_conv_w, m_lru_conv_b, m_lru_w_a, m_lru_b_a, m_lru_w_i, m_lru_b_i, m_lru_lambda, m_w_gate, m_w_up, m_w_down, v_w_in, v_w_out, v_g_pre_mix, v_g_post_mix, v_g_pre_ffn, v_g_post_ffn, v_g_attn_grp, v_g_conv_grp, v_g_lru_grp, v_dw_conv_w, v_dw_conv_b, v_conv_ln_g, v_conv_ln_b, v_lru_conv_w, v_lru_conv_b, v_lru_w_a, v_lru_b_a, v_lru_w_i, v_lru_b_i, v_lru_lambda, v_w_gate, v_w_up, v_w_down):
    env = dict(locals())
    wts = {n: env[n] for n in _ALL}
    mom = {n: env["m_" + n] for n in _ALL}
    var = {n: env["v_" + n] for n in _ALL}
    for group in (wts, mom, var):
        for n in _TRANSPOSED:
            group[n] = jnp.swapaxes(group[n], 1, 2)

    depth = w_in.shape[0]
    h = x[0]
    target = loss_target[0]
    t, d = h.shape
    attn_w = d // 2
    n_heads = attn_w // HEAD_DIM
    cc = d // 4
    wl = d // 4
    conv_col, lru_col = 3 * attn_w, 3 * attn_w + 2 * cc
    me = 4 * lax.axis_index("x") + 2 * lax.axis_index("y") + lax.axis_index("c")
    me_s = me.astype(jnp.int32).reshape(1)
    chip_s = (2 * lax.axis_index("x") + lax.axis_index("y")).astype(jnp.int32).reshape(1)
    core_s = lax.axis_index("c").astype(jnp.int32).reshape(1)

    n_taps = DW_LEN + LRU_LEN
    taps = jnp.concatenate([dw_conv_w, lru_conv_w], axis=1).reshape(depth * n_taps, cc // N_DEV)
    taps = all_gather("ag_taps", taps, me_s)
    taps = jnp.moveaxis(taps.reshape(N_DEV, depth, n_taps, cc // N_DEV), 0, 2).reshape(depth, n_taps, cc)
    dw_full, lcw_full = taps[:, :DW_LEN], taps[:, DW_LEN:]

    def vec(a, l):
        return a[l].reshape(1, -1)

    ag_state, started = [], taps
    for l in range(depth):
        ag_state.append(gather_start(f"ag_start_{l}", [place_own(wts[n][l], me_s, BF16) for n in _BIG], started))
        started = ag_state[l][-1]
    started = started[0:1, 0:1]

    saved = []
    u1 = rms_pre(h, vec(g_pre_mix, 0) + started)
    loss_sum = dh = dbr = None
    for l in range(depth):
        n_first = 1 if l == 0 else len(_BIG)
        wg = dict(zip(_BIG, gather_finish("ag_finish_a", gather_wait(f"ag_wait_{l}a", ag_state[l], 0, n_first, u1))))
        wa_b, wi_b = lru_w_a[l].astype(BF16), lru_w_i[l].astype(BF16)
        proj = mm_proj(u1, wg["w_in"])
        y_attn = attn_fwd(proj, n_heads)
        cpre, y_conv = conv_fwd(proj, conv_col, cc, dw_full[l], vec(dw_conv_b, l), vec(conv_ln_g, l), vec(conv_ln_b, l))
        hs, y_lru = lru_fwd(proj, lru_col, wl, lcw_full[l], vec(lru_conv_b, l), wa_b, vec(lru_b_a, l), wi_b,
                            vec(lru_b_i, l), vec(lru_lambda, l))
        mixed = mix_fwd(y_attn, y_conv, y_lru, vec(g_attn_grp, l), vec(g_conv_grp, l), vec(g_lru_grp, l))
        if n_first < len(_BIG):
            rest = gather_wait(f"ag_wait_{l}b", ag_state[l], n_first, len(_BIG) - n_first, mixed)
            wg.update(zip(_BIG[n_first:], gather_finish("ag_finish_b", rest)))
        wg["w_out"] = wg["w_out"].reshape(attn_w + cc + wl, d)
        o = mm_plain("mm_out", mixed, wg["w_out"], "nn", F32)
        h2, u2 = res_norm(h, o, vec(g_post_mix, l), vec(g_pre_ffn, l))
        gt, up, f = ffn_up(u2, wg["w_gate"], wg["w_up"])
        dn = mm_down(f, wg["w_down"])
        saved.append(dict(wg=wg, wa_b=wa_b, wi_b=wi_b, h=h, u1=u1, proj=proj, y_attn=y_attn, cpre=cpre, y_conv=y_conv,
                          hs=hs, y_lru=y_lru, mixed=mixed, o=o, h2=h2, u2=u2, gt=gt, up=up, f=f, dn=dn))
        if l + 1 < depth:
            h, u1 = res_norm(h2, dn, vec(g_post_ffn, l), vec(g_pre_mix, l + 1))
        else:
            loss_sum, dh, dbr, dg_post_ffn = final_loss(h2, dn, vec(g_post_ffn, l), target)

    loss = lax.psum(0.5 * loss_sum[0, 0] / d, MESH_AXES)

    small = {n: [None] * depth for n in _SMALL}
    rs_state = {n: [None] * depth for n in _BIG}
    for l in reversed(range(depth)):
        s = saved[l]
        wg = s["wg"]
        small["g_post_ffn"][l] = dg_post_ffn
        dgt, dup = ffn_bwd(dbr, wg["w_down"], s["gt"], s["up"])
        rs_state["w_down"][l] = reduce_scatter_start(f"down_{l}", mm_dw_rows("mm_dw_down", s["f"], dbr), core_s)
        rs_state["w_gate"][l] = reduce_scatter_start(f"gate_{l}", mm_dw_rows("mm_dw_gate", dgt, s["u2"]), core_s)
        rs_state["w_up"][l] = reduce_scatter_start(f"up_{l}", mm_dw_rows("mm_dw_gate", dup, s["u2"]), core_s)
        du2 = mm_dx_ffn(dgt, wg["w_gate"], dup, wg["w_up"])
        started = sum(rs_state[n][l][-1][0:1, 0:1] for n in ("w_down", "w_gate", "w_up"))
        dh2, small["g_pre_ffn"][l], do, small["g_post_mix"][l] = norm_bwd(
            dh, du2, s["h2"], vec(g_pre_ffn, l) + started, (s["o"], vec(g_post_mix, l)))
        dmixed = mm_plain("mm_dmixed", do, wg["w_out"], "nt", F32)
        dw_out = mm_plain("mm_dw_out", s["mixed"], do, "tn", BF16)
        rs_state["w_out"][l] = reduce_scatter_start(f"out_{l}", dw_out.reshape(N_DEV, -1, d), core_s)
        (dya, dc, dyl, small["g_attn_grp"][l], small["g_conv_grp"][l], small["g_lru_grp"][l],
         small["conv_ln_g"][l], small["conv_ln_b"][l]) = mix_bwd(
            dmixed, s["y_attn"], s["y_conv"], s["y_lru"], s["cpre"],
            vec(g_attn_grp, l) + rs_state["w_out"][l][-1][0:1, 0:1], vec(g_conv_grp, l),
            vec(g_lru_grp, l), vec(conv_ln_g, l), vec(conv_ln_b, l))
        dq, dk, dv = attn_bwd(s["proj"], dya, n_heads)
        dvg, small["dw_conv_w"][l], small["dw_conv_b"][l] = conv_bwd(s["proj"], conv_col, cc, dc, dw_full[l])
        (dxy, small["lru_conv_w"][l], small["lru_conv_b"][l], small["lru_w_a"][l], small["lru_b_a"][l],
         small["lru_w_i"][l], small["lru_b_i"][l], small["lru_lambda"][l]) = lru_bwd(
            s["proj"], lru_col, wl, s["hs"], dyl, lcw_full[l], vec(lru_conv_b, l), s["wa_b"], vec(lru_b_a, l),
            s["wi_b"], vec(lru_b_i, l), vec(lru_lambda, l))
        dproj = jnp.concatenate([dq, dk, dv, dvg, dxy], axis=1)
        rs_state["w_in"][l] = reduce_scatter_start(f"in_{l}", mm_dw_cols("mm_dw_in", s["u1"], dproj, N_DEV), core_s)
        du1 = mm_dx_cols("mm_dx_in", dproj, wg["w_in"])
        g_pre = vec(g_pre_mix, l) + rs_state["w_in"][l][-1][0:1, 0:1]
        if l > 0:
            p = saved[l - 1]
            dh, small["g_pre_mix"][l], dbr, dg_post_ffn = norm_bwd(
                dh2, du1, s["h"], g_pre, (p["dn"], vec(g_post_ffn, l - 1)))
        else:
            dh, small["g_pre_mix"][l] = norm_bwd(dh2, du1, s["h"], g_pre)
    grad_x = dh[None]
    big_parts = {n: [reduce_scatter_wait(f"{n[2:]}_{l}", rs_state[n][l], dh) for l in range(depth)] for n in _BIG}

    small_shapes = [(depth,) + tuple(wts[n].shape[1:]) if n not in _COL_SHARDED_SMALL
                    else (depth, wts[n].shape[1], cc) for n in _SMALL]
    part = _pack([jnp.stack([a.reshape(shp[1:]) for a in small[n]]) for n, shp in zip(_SMALL, small_shapes)])
    g_small = _unpack(sum_parts(all_gather("ag_small", part, me_s)), small_shapes)
    grads = {}
    for n, g in zip(_SMALL, g_small):
        if n in _COL_SHARDED_SMALL:
            g = lax.dynamic_slice_in_dim(g, me * (cc // N_DEV), cc // N_DEV, axis=2)
        grads[n] = g
    local_shapes = [tuple(wts[n].shape) for n in _SMALL]
    d_small, m_small, v_small = adam_small(
        _pack([wts[n] for n in _SMALL]), _pack([mom[n] for n in _SMALL]), _pack([var[n] for n in _SMALL]),
        _pack([grads[n] for n in _SMALL]))
    delta = dict(zip(_SMALL, _unpack(d_small, local_shapes)))
    new_m = dict(zip(_SMALL, _unpack(m_small, local_shapes)))
    new_v = dict(zip(_SMALL, _unpack(v_small, local_shapes)))

    for n in _BIG:
        shape = wts[n].shape
        _, rows, cols = big_parts[n][0][0].shape
        view = (depth, rows, cols)
        g, dl, mn, vn = adam_big("adam_" + n, wts[n].reshape(view), mom[n].reshape(view), var[n].reshape(view),
                                 big_parts[n], chip_s)
        grads[n], delta[n], new_m[n], new_v[n] = (a.reshape(shape) for a in (g, dl, mn, vn))
    for group in (grads, delta, new_m, new_v):
        for n in _TRANSPOSED:
            group[n] = jnp.swapaxes(group[n], 1, 2)

    return (loss, grad_x, *[grads[n] for n in _ALL], *[delta[n] for n in _ALL],
            *[new_m[n] for n in _ALL], *[new_v[n] for n in _ALL])
```

```python
import functools
import math

import jax
import jax.numpy as jnp
from jax import lax
from jax.experimental import pallas as pl
from jax.experimental.pallas import tpu as pltpu

F32 = jnp.float32
BF16 = jnp.bfloat16

N_DEV = 8
EPS = 1e-6
HEAD_DIM = 128
DW_LEN = 31
LRU_LEN = 4
LRU_BLOCKS = 4
LRU_C = 8.0
ATT_TQ = 512
ATT_TK = 512
ATT_SUM = 256
ATT_HEADS = 2
ROW_T = 256
CONV_HALO = 32
LRU_HALO = 8
LANE = 128
SUBLANE = 8
PACK_ROWS = 512
VMEM_LIMIT = 56 * 1024 * 1024

ADAM_LR = 0.001
ADAM_B1 = 0.9
ADAM_B2 = 0.999
ADAM_EPS = 1e-08
ADAM_WD = 0.01
ADAM_STEP = 10

MESH_AXES = ("x", "y", "c")
_DIMS = {
    "nn": (((1,), (0,)), ((), ())),
    "nt": (((1,), (1,)), ((), ())),
    "tn": (((0,), (0,)), ((), ())),
}


def _params(n_axes):
    return pltpu.CompilerParams(
        dimension_semantics=("arbitrary",) * n_axes, vmem_limit_bytes=VMEM_LIMIT)


def _dot(a, b, mode="nn"):
    return lax.dot_general(a, b, _DIMS[mode], preferred_element_type=F32)


def _sigmoid(x):
    return 1.0 / (1.0 + jnp.exp(-x))


def _softplus(x):
    return jnp.maximum(x, 0.0) + jnp.log(1.0 + jnp.exp(-jnp.abs(x)))


def _neg_expm1(x):
    series = x * (1.0 + x * (0.5 + x * (1.0 / 6 + x * (1.0 / 24 + x * (1.0 / 120 + x * (1.0 / 720))))))
    return jnp.where(x > -0.25, -series, 1.0 - jnp.exp(x))


_GELU_C = math.sqrt(2.0 / math.pi)


def _gelu_and_grad(x):
    inner = _GELU_C * (x + 0.044715 * x * x * x)
    t = jnp.tanh(inner)
    val = 0.5 * x * (1.0 + t)
    grad = 0.5 * (1.0 + t) + 0.5 * x * (1.0 - t * t) * _GELU_C * (1.0 + 3 * 0.044715 * x * x)
    return val, grad


def _rms_stats(x):
    r = lax.rsqrt(jnp.mean(x * x, axis=-1, keepdims=True) + EPS)
    return x * r, r


def _rms_bwd(dy, x, g):
    xn, r = _rms_stats(x)
    dxn = dy * g
    dx = r * (dxn - xn * jnp.mean(dxn * xn, axis=-1, keepdims=True))
    return dx, jnp.sum(dy * xn, axis=0, keepdims=True)


def _row_spec(tr, width, col=0):
    return pl.BlockSpec((tr, width), lambda i, col=col: (i, col))


def _vec_spec(width):
    return pl.BlockSpec((1, width), lambda i: (0, 0))


def _matmul(name, mode, operands, in_specs, out_shape, out_spec, grid, out_block):
    npairs = len(operands) // 2
    nk = grid[2]

    def body(*refs):
        o_ref = refs[2 * npairs]

        def partial():
            acc = None
            for p in range(npairs):
                d = _dot(refs[2 * p][...], refs[2 * p + 1][...], mode)
                acc = d if acc is None else acc + d
            return acc

        if nk == 1:
            o_ref[...] = partial().astype(o_ref.dtype)
        else:
            acc_ref = refs[2 * npairs + 1]
            k = pl.program_id(2)

            @pl.when(k == 0)
            def _():
                acc_ref[...] = jnp.zeros_like(acc_ref)

            acc_ref[...] += partial()

            @pl.when(k == nk - 1)
            def _():
                o_ref[...] = acc_ref[...].astype(o_ref.dtype)

    return pl.pallas_call(
        body, name=name, grid=grid, in_specs=in_specs, out_specs=out_spec, out_shape=out_shape,
        scratch_shapes=[] if nk == 1 else [pltpu.VMEM(out_block, F32)],
        compiler_params=_params(3),
    )(*operands)


def _tile(n, t):
    if n <= t:
        return n
    return max(k for k in range(SUBLANE, t + 1, SUBLANE) if n % k == 0)


def mm_proj(u, w):
    t, d = u.shape
    nblk, _, nb = w.shape
    tm = _tile(t, 1024)
    return _matmul(
        "mm_proj", "nn", (u, w),
        [pl.BlockSpec((tm, d), lambda j, i, k: (i, 0)), pl.BlockSpec((None, d, nb), lambda j, i, k: (j, 0, 0))],
        jax.ShapeDtypeStruct((t, nblk * nb), F32), pl.BlockSpec((tm, nb), lambda j, i, k: (i, j)),
        (nblk, t // tm, 1), (tm, nb))


def mm_plain(name, a, b, mode, out_dtype):
    if mode == "nn":
        (m, kk), n = a.shape, b.shape[1]
    elif mode == "nt":
        (m, kk), n = a.shape, b.shape[0]
    else:
        (kk, m), n = a.shape, b.shape[1]
    tm, tn = _tile(m, 1024), _tile(n, 1024)
    a_spec = (pl.BlockSpec((kk, tm), lambda i, j, k: (0, i)) if mode == "tn"
              else pl.BlockSpec((tm, kk), lambda i, j, k: (i, 0)))
    b_spec = (pl.BlockSpec((tn, kk), lambda i, j, k: (j, 0)) if mode == "nt"
              else pl.BlockSpec((kk, tn), lambda i, j, k: (0, j)))
    return _matmul(
        name, mode, (a, b), [a_spec, b_spec],
        jax.ShapeDtypeStruct((m, n), out_dtype), pl.BlockSpec((tm, tn), lambda i, j, k: (i, j)),
        (m // tm, n // tn, 1), (tm, tn))


def mm_down(f, w):
    nblk, t, fb = f.shape
    d = w.shape[2]
    tm, tn = _tile(t, 1024), _tile(d, 1024)
    return _matmul(
        "mm_down", "nn", (f, w),
        [pl.BlockSpec((None, tm, fb), lambda i, j, k: (k, i, 0)), pl.BlockSpec((None, fb, tn), lambda i, j, k: (k, 0, j))],
        jax.ShapeDtypeStruct((t, d), F32), pl.BlockSpec((tm, tn), lambda i, j, k: (i, j)),
        (t // tm, d // tn, nblk), (tm, tn))


def mm_dw_half(name, kind, a, g, core, of_sibling, add=None):
    half = N_DEV // 2
    t = g.shape[0]

    def pick(k, core_ref):
        s = 1 - core_ref[0] if of_sibling else core_ref[0]
        return 2 * k + s

    if kind == "rows":
        rows, cols = a.shape[2], g.shape[1]
        tr, tc = rows, _tile(cols, 1024)
        a_spec = pl.BlockSpec((None, t, rows), lambda k, i, n, core_ref: (pick(k, core_ref), 0, 0))
        g_spec = pl.BlockSpec((t, tc), lambda k, i, n, core_ref: (0, n))
    elif kind == "cols":
        rows, cols = a.shape[1], g.shape[1] // N_DEV
        tr, tc = _tile(rows, 1024), cols
        a_spec = pl.BlockSpec((t, tr), lambda k, i, n, core_ref: (0, i))
        g_spec = pl.BlockSpec((t, cols), lambda k, i, n, core_ref: (0, pick(k, core_ref)))
    else:
        rows, cols = a.shape[1] // N_DEV, g.shape[1]
        tr, tc = rows, _tile(cols, 1024)
        a_spec = pl.BlockSpec((t, rows), lambda k, i, n, core_ref: (0, pick(k, core_ref)))
        g_spec = pl.BlockSpec((t, tc), lambda k, i, n, core_ref: (0, n))
    o_spec = pl.BlockSpec((None, tr, tc), lambda k, i, n, core_ref: (k, i, n))

    def body(core_ref, a_ref, g_ref, *rest):
        acc = _dot(a_ref[...], g_ref[...], "tn")
        if add is not None:
            acc = acc + rest[0][...].astype(F32)
        rest[-1][...] = acc.astype(BF16)

    return pl.pallas_call(
        body, name=name,
        grid_spec=pltpu.PrefetchScalarGridSpec(
            num_scalar_prefetch=1, grid=(half, rows // tr, cols // tc),
            in_specs=[a_spec, g_spec] + ([o_spec] if add is not None else []), out_specs=o_spec),
        out_shape=jax.ShapeDtypeStruct((half, rows, cols), BF16), compiler_params=_params(3),
    )(core, a, g, *(() if add is None else (add,)))


def mm_dx_cols(name, g, w):
    t = g.shape[0]
    nblk, d, nb = w.shape
    tm, tn = _tile(t, 1024), _tile(d, 1024)
    return _matmul(
        name, "nt", (g, w),
        [pl.BlockSpec((tm, nb), lambda i, j, k: (i, k)), pl.BlockSpec((None, tn, nb), lambda i, j, k: (k, j, 0))],
        jax.ShapeDtypeStruct((t, d), F32), pl.BlockSpec((tm, tn), lambda i, j, k: (i, j)),
        (t // tm, d // tn, nblk), (tm, tn))


def mm_dx_ffn(dgt, wg, dup, wu):
    nblk, t, fb = dgt.shape
    d = wg.shape[2]
    tm, tn = _tile(t, 1024), _tile(d, 1024)
    a_spec = pl.BlockSpec((None, tm, fb), lambda i, j, k: (k, i, 0))
    b_spec = pl.BlockSpec((None, fb, tn), lambda i, j, k: (k, 0, j))
    return _matmul(
        "mm_dx_ffn", "nn", (dgt, wg, dup, wu), [a_spec, b_spec, a_spec, b_spec],
        jax.ShapeDtypeStruct((t, d), F32), pl.BlockSpec((tm, tn), lambda i, j, k: (i, j)),
        (t // tm, d // tn, nblk), (tm, tn))


def ffn_up(u, wg, wu):
    t, d = u.shape
    nblk, fb, _ = wg.shape
    tm = _tile(t, 512)

    def body(u_ref, wg_ref, wu_ref, gt_ref, up_ref, f_ref):
        uu = u_ref[...]
        gt = _dot(uu, wg_ref[...], "nt")
        up = _dot(uu, wu_ref[...], "nt")
        gt_ref[...] = gt.astype(BF16)
        up_ref[...] = up.astype(BF16)
        f_ref[...] = (gt * _sigmoid(gt) * up).astype(BF16)

    w_spec = pl.BlockSpec((None, fb, d), lambda j, i: (j, 0, 0))
    o_spec = pl.BlockSpec((None, tm, fb), lambda j, i: (j, i, 0))
    return pl.pallas_call(
        body, name="ffn_up", grid=(nblk, t // tm),
        in_specs=[pl.BlockSpec((tm, d), lambda j, i: (i, 0)), w_spec, w_spec],
        out_specs=[o_spec, o_spec, o_spec],
        out_shape=[jax.ShapeDtypeStruct((nblk, t, fb), BF16)] * 3,
        compiler_params=_params(2),
    )(u, wg, wu)


def ffn_bwd(dd, wd, gt, up):
    t, d = dd.shape
    nblk, fb, _ = wd.shape
    tm = _tile(t, 512)

    def body(dd_ref, wd_ref, gt_ref, up_ref, dgt_ref, dup_ref):
        df = _dot(dd_ref[...], wd_ref[...], "nt")
        g = gt_ref[...].astype(F32)
        s = _sigmoid(g)
        dgt_ref[...] = (df * up_ref[...].astype(F32) * s * (1.0 + g * (1.0 - s))).astype(BF16)
        dup_ref[...] = (df * g * s).astype(BF16)

    s_spec = pl.BlockSpec((None, tm, fb), lambda j, i: (j, i, 0))
    return pl.pallas_call(
        body, name="ffn_bwd", grid=(nblk, t // tm),
        in_specs=[pl.BlockSpec((tm, d), lambda j, i: (i, 0)), pl.BlockSpec((None, fb, d), lambda j, i: (j, 0, 0)),
                  s_spec, s_spec],
        out_specs=[s_spec, s_spec],
        out_shape=[jax.ShapeDtypeStruct((nblk, t, fb), BF16)] * 2,
        compiler_params=_params(2),
    )(dd, wd, gt, up)


def rms_pre(h, g):
    t, d = h.shape
    tr = _tile(t, ROW_T)

    def body(h_ref, g_ref, o_ref):
        xn, _ = _rms_stats(h_ref[...])
        o_ref[...] = (xn * g_ref[...]).astype(BF16)

    return pl.pallas_call(
        body, name="rms_pre", grid=(t // tr,),
        in_specs=[_row_spec(tr, d), _vec_spec(d)], out_specs=_row_spec(tr, d),
        out_shape=jax.ShapeDtypeStruct((t, d), BF16), compiler_params=_params(1),
    )(h, g)


def res_norm(h, o, g_post, g_pre):
    t, d = h.shape
    tr = _tile(t, ROW_T)

    def body(h_ref, o_ref, gpo_ref, gpr_ref, h2_ref, u_ref):
        on, _ = _rms_stats(o_ref[...])
        h2 = h_ref[...] + on * gpo_ref[...]
        h2_ref[...] = h2
        hn, _ = _rms_stats(h2)
        u_ref[...] = (hn * gpr_ref[...]).astype(BF16)

    return pl.pallas_call(
        body, name="res_norm", grid=(t // tr,),
        in_specs=[_row_spec(tr, d), _row_spec(tr, d), _vec_spec(d), _vec_spec(d)],
        out_specs=[_row_spec(tr, d), _row_spec(tr, d)],
        out_shape=[jax.ShapeDtypeStruct((t, d), F32), jax.ShapeDtypeStruct((t, d), BF16)],
        compiler_params=_params(1),
    )(h, o, g_post, g_pre)


def final_loss(h2, dbr, g_post, target):
    t, d = h2.shape
    tr = _tile(t, ROW_T)

    def body(h2_ref, d_ref, g_ref, tg_ref, loss_ref, dy_ref, dd_ref, dg_ref):
        i = pl.program_id(0)

        @pl.when(i == 0)
        def _():
            loss_ref[...] = jnp.zeros_like(loss_ref)
            dg_ref[...] = jnp.zeros_like(dg_ref)

        x = d_ref[...]
        g = g_ref[...]
        xn, _ = _rms_stats(x)
        diff = h2_ref[...] + xn * g - tg_ref[...]
        loss_ref[...] += jnp.sum(jnp.sum(diff * diff, axis=1, keepdims=True), axis=0, keepdims=True)
        dy = diff * (1.0 / d)
        dy_ref[...] = dy
        dx, dg = _rms_bwd(dy, x, g)
        dd_ref[...] = dx.astype(BF16)
        dg_ref[...] += dg

    return pl.pallas_call(
        body, name="final_loss", grid=(t // tr,),
        in_specs=[_row_spec(tr, d), _row_spec(tr, d), _vec_spec(d), _row_spec(tr, d)],
        out_specs=[pl.BlockSpec((1, 1), lambda i: (0, 0)), _row_spec(tr, d), _row_spec(tr, d), _vec_spec(d)],
        out_shape=[jax.ShapeDtypeStruct((1, 1), F32), jax.ShapeDtypeStruct((t, d), F32),
                   jax.ShapeDtypeStruct((t, d), BF16), jax.ShapeDtypeStruct((1, d), F32)],
        compiler_params=_params(1),
    )(h2, dbr, g_post, target)


def norm_bwd(dh_out, du, h_in, g_pre, prev=None):
    t, d = h_in.shape
    tr = _tile(t, ROW_T)
    with_prev = prev is not None

    def body(*refs):
        if with_prev:
            dho_ref, du_ref, h_ref, gpr_ref, br_ref, gpo_ref, dh_ref, dgpr_ref, dbr_ref, dgpo_ref = refs
        else:
            dho_ref, du_ref, h_ref, gpr_ref, dh_ref, dgpr_ref = refs
        i = pl.program_id(0)

        @pl.when(i == 0)
        def _():
            dgpr_ref[...] = jnp.zeros_like(dgpr_ref)
            if with_prev:
                dgpo_ref[...] = jnp.zeros_like(dgpo_ref)

        dx, dg = _rms_bwd(du_ref[...], h_ref[...], gpr_ref[...])
        dh = dho_ref[...] + dx
        dh_ref[...] = dh
        dgpr_ref[...] += dg
        if with_prev:
            dbr, dg2 = _rms_bwd(dh, br_ref[...], gpo_ref[...])
            dbr_ref[...] = dbr.astype(BF16)
            dgpo_ref[...] += dg2

    row, vec = _row_spec(tr, d), _vec_spec(d)
    in_specs = [row, row, row, vec] + ([row, vec] if with_prev else [])
    out_specs = [row, vec] + ([row, vec] if with_prev else [])
    out_shape = [jax.ShapeDtypeStruct((t, d), F32), jax.ShapeDtypeStruct((1, d), F32)]
    if with_prev:
        out_shape += [jax.ShapeDtypeStruct((t, d), BF16), jax.ShapeDtypeStruct((1, d), F32)]
    args = (dh_out, du, h_in, g_pre) + (tuple(prev) if with_prev else ())
    return pl.pallas_call(
        body, name="norm_bwd_chain" if with_prev else "norm_bwd_first", grid=(t // tr,),
        in_specs=in_specs, out_specs=out_specs, out_shape=out_shape, compiler_params=_params(1),
    )(*args)


def mix_fwd(ya, yc, yl, ga, gc, gl):
    t, wa = ya.shape
    wc, wl = yc.shape[1], yl.shape[1]
    tr = _tile(t, ROW_T)

    def body(ya_ref, yc_ref, yl_ref, ga_ref, gc_ref, gl_ref, o_ref):
        o_ref[:, pl.ds(0, wa)] = (_rms_stats(ya_ref[...])[0] * ga_ref[...]).astype(BF16)
        o_ref[:, pl.ds(wa, wc)] = (_rms_stats(yc_ref[...])[0] * gc_ref[...]).astype(BF16)
        o_ref[:, pl.ds(wa + wc, wl)] = (_rms_stats(yl_ref[...])[0] * gl_ref[...]).astype(BF16)

    return pl.pallas_call(
        body, name="mix_fwd", grid=(t // tr,),
        in_specs=[_row_spec(tr, wa), _row_spec(tr, wc), _row_spec(tr, wl), _vec_spec(wa), _vec_spec(wc), _vec_spec(wl)],
        out_specs=_row_spec(tr, wa + wc + wl),
        out_shape=jax.ShapeDtypeStruct((t, wa + wc + wl), BF16), compiler_params=_params(1),
    )(ya, yc, yl, ga, gc, gl)


def mix_bwd(dmixed, ya, yc, yl, cpre, ga, gc, gl, lng, lnb):
    t, wa = ya.shape
    wc, wl = yc.shape[1], yl.shape[1]
    tr = _tile(t, ROW_T)

    def body(dm_ref, ya_ref, yc_ref, yl_ref, c_ref, ga_ref, gc_ref, gl_ref, lg_ref, lb_ref,
             dya_ref, dc_ref, dyl_ref, dga_ref, dgc_ref, dgl_ref, dlg_ref, dlb_ref):
        i = pl.program_id(0)

        @pl.when(i == 0)
        def _():
            for r in (dga_ref, dgc_ref, dgl_ref, dlg_ref, dlb_ref):
                r[...] = jnp.zeros_like(r)

        dya, dga = _rms_bwd(dm_ref[:, pl.ds(0, wa)], ya_ref[...], ga_ref[...])
        dya_ref[...] = dya
        dga_ref[...] += dga
        dyl, dgl = _rms_bwd(dm_ref[:, pl.ds(wa + wc, wl)], yl_ref[...], gl_ref[...])
        dyl_ref[...] = dyl
        dgl_ref[...] += dgl
        dyc, dgc = _rms_bwd(dm_ref[:, pl.ds(wa, wc)], yc_ref[...], gc_ref[...])
        dgc_ref[...] += dgc
        c = c_ref[...]
        xc = c - jnp.mean(c, axis=-1, keepdims=True)
        rstd = lax.rsqrt(jnp.mean(xc * xc, axis=-1, keepdims=True) + EPS)
        xhat = xc * rstd
        ln = xhat * lg_ref[...] + lb_ref[...]
        s = _sigmoid(ln)
        dln = dyc * s * (1.0 + ln * (1.0 - s))
        dlg_ref[...] += jnp.sum(dln * xhat, axis=0, keepdims=True)
        dlb_ref[...] += jnp.sum(dln, axis=0, keepdims=True)
        dxh = dln * lg_ref[...]
        dc_ref[...] = rstd * (dxh - jnp.mean(dxh, axis=-1, keepdims=True)
                              - xhat * jnp.mean(dxh * xhat, axis=-1, keepdims=True))

    return pl.pallas_call(
        body, name="mix_bwd", grid=(t // tr,),
        in_specs=[_row_spec(tr, wa + wc + wl), _row_spec(tr, wa), _row_spec(tr, wc), _row_spec(tr, wl), _row_spec(tr, wc),
                  _vec_spec(wa), _vec_spec(wc), _vec_spec(wl), _vec_spec(wc), _vec_spec(wc)],
        out_specs=[_row_spec(tr, wa), _row_spec(tr, wc), _row_spec(tr, wl),
                   _vec_spec(wa), _vec_spec(wc), _vec_spec(wl), _vec_spec(wc), _vec_spec(wc)],
        out_shape=[jax.ShapeDtypeStruct((t, wa), F32), jax.ShapeDtypeStruct((t, wc), F32), jax.ShapeDtypeStruct((t, wl), F32),
                   jax.ShapeDtypeStruct((1, wa), F32), jax.ShapeDtypeStruct((1, wc), F32), jax.ShapeDtypeStruct((1, wl), F32),
                   jax.ShapeDtypeStruct((1, wc), F32), jax.ShapeDtypeStruct((1, wc), F32)],
        compiler_params=_params(1),
    )(dmixed, ya, yc, yl, cpre, ga, gc, gl, lng, lnb)


def _hi_lo(x):
    hi = x.astype(BF16)
    return hi, (x - hi.astype(F32)).astype(BF16)


def _lane_sums(x, tri, reverse):
    nsub = x.shape[1] // tri.shape[0]
    order = range(nsub - 1, -1, -1) if reverse else range(nsub)
    parts, beyond = {}, None
    for b in order:
        blk = x[:, b * tri.shape[0]:(b + 1) * tri.shape[0]]
        hi, lo = _hi_lo(blk)
        c = _dot(hi, tri) + _dot(lo, tri)
        parts[b] = c if beyond is None else c + beyond
        tot = jnp.sum(blk, axis=1, keepdims=True)
        beyond = tot if beyond is None else beyond + tot
    return jnp.concatenate([parts[b] for b in range(nsub)], axis=1), beyond


def _att_strip(qb, kt, thresh, scale, diff, tri_gt):
    z = _dot(qb, kt, "nt") * scale
    sp = _softplus(z)
    mask = diff < thresh
    later, total = _lane_sums(jnp.where(mask, -sp, 0.0), tri_gt, True)
    return z, sp, mask, total, (z - sp) + later


def _att_consts(tq, tk):
    diff = lax.broadcasted_iota(jnp.int32, (tq, tk), 1) - lax.broadcasted_iota(jnp.int32, (tq, tk), 0)
    cb = min(tk, ATT_SUM)
    row = lax.broadcasted_iota(jnp.int32, (cb, cb), 0)
    col = lax.broadcasted_iota(jnp.int32, (cb, cb), 1)
    return diff, (row > col).astype(BF16), (row < col).astype(BF16)


def attn_fwd(proj, n_heads):
    t = proj.shape[0]
    tq, tk = _tile(t, ATT_TQ), _tile(t, ATT_TK)
    hp = ATT_HEADS
    wd = hp * HEAD_DIM
    scale = HEAD_DIM ** -0.5

    def body(q_ref, k_ref, v_ref, o_ref, kb_ref, vb_ref, acc_ref):
        kb_ref[...] = k_ref[...].astype(BF16)
        vb_ref[...] = v_ref[...].astype(BF16)
        diff, tri_gt, _ = _att_consts(tq, tk)

        def qblock(i, _):
            q0 = pl.multiple_of(i * tq, tq)
            heads = [pl.ds(h * HEAD_DIM, HEAD_DIM) for h in range(hp)]
            qbs = [q_ref[pl.ds(q0, tq), hs].astype(BF16) for hs in heads]
            acc_ref[...] = jnp.zeros_like(acc_ref)
            n_strips = (q0 + tq + tk - 1) // tk

            def strip(jj, runs):
                k0 = pl.multiple_of((n_strips - 1 - jj) * tk, tk)
                out = []
                for h, hs in enumerate(heads):
                    _, _, mask, total, logw = _att_strip(qbs[h], kb_ref[pl.ds(k0, tk), hs], q0 - k0, scale, diff, tri_gt)
                    w = jnp.where(mask, jnp.exp(logw + runs[h]), 0.0)
                    acc_ref[:, hs] += _dot(w.astype(BF16), vb_ref[pl.ds(k0, tk), hs])
                    out.append(runs[h] + total)
                return tuple(out)

            lax.fori_loop(0, n_strips, strip, tuple(jnp.zeros((tq, 1), F32) for _ in range(hp)))
            o_ref[pl.ds(q0, tq), :] = acc_ref[...]
            return 0

        lax.fori_loop(0, t // tq, qblock, 0)

    def col_spec(base):
        return pl.BlockSpec((t, wd), lambda h, base=base: (0, base + h))

    ng = n_heads // hp
    return pl.pallas_call(
        body, name="attn_fwd", grid=(ng,),
        in_specs=[col_spec(0), col_spec(ng), col_spec(2 * ng)],
        out_specs=col_spec(0),
        out_shape=jax.ShapeDtypeStruct((t, n_heads * HEAD_DIM), F32),
        scratch_shapes=[pltpu.VMEM((t, wd), BF16), pltpu.VMEM((t, wd), BF16), pltpu.VMEM((tq, wd), F32)],
        compiler_params=_params(1),
    )(proj, proj, proj)


def attn_bwd(proj, dy, n_heads):
    t = proj.shape[0]
    tq, tk = _tile(t, ATT_TQ), _tile(t, ATT_TK)
    hp = ATT_HEADS
    wd = hp * HEAD_DIM
    scale = HEAD_DIM ** -0.5

    def body(q_ref, k_ref, v_ref, dy_ref, dq_ref, dk_ref, dv_ref, qb_ref, kb_ref, vb_ref, dob_ref, dk_acc, dv_acc,
             dq_acc, run_s):
        qb_ref[...] = q_ref[...].astype(BF16)
        kb_ref[...] = k_ref[...].astype(BF16)
        vb_ref[...] = v_ref[...].astype(BF16)
        dob_ref[...] = dy_ref[...].astype(BF16)
        dk_acc[...] = jnp.zeros_like(dk_acc)
        dv_acc[...] = jnp.zeros_like(dv_acc)
        diff, tri_gt, tri_lt = _att_consts(tq, tk)

        def qblock(i, _):
            q0 = pl.multiple_of(i * tq, tq)
            heads = [pl.ds(h * HEAD_DIM, HEAD_DIM) for h in range(hp)]
            qbs = [qb_ref[pl.ds(q0, tq), hs] for hs in heads]
            dobs = [dob_ref[pl.ds(q0, tq), hs] for hs in heads]
            dq_acc[...] = jnp.zeros_like(dq_acc)
            n_strips = (q0 + tq + tk - 1) // tk

            def sweep(jj, runs):
                si = n_strips - 1 - jj
                k0 = pl.multiple_of(si * tk, tk)
                out = []
                for h, hs in enumerate(heads):
                    sp = _softplus(_dot(qbs[h], kb_ref[pl.ds(k0, tk), hs], "nt") * scale)
                    run_s[h, si] = runs[h]
                    out.append(runs[h] + jnp.sum(jnp.where(diff < q0 - k0, -sp, 0.0), axis=1, keepdims=True))
                return tuple(out)

            zero = tuple(jnp.zeros((tq, 1), F32) for _ in range(hp))
            lax.fori_loop(0, n_strips, sweep, zero)

            def strip(si, gsums):
                k0 = pl.multiple_of(si * tk, tk)
                out = []
                for h, hs in enumerate(heads):
                    kt = kb_ref[pl.ds(k0, tk), hs]
                    vt = vb_ref[pl.ds(k0, tk), hs]
                    z, sp, mask, _, logw = _att_strip(qbs[h], kt, q0 - k0, scale, diff, tri_gt)
                    w = jnp.where(mask, jnp.exp(logw + run_s[h, si]), 0.0)
                    g = w * _dot(dobs[h], vt, "nt")
                    before, gtot = _lane_sums(g, tri_lt, False)
                    sig = jnp.exp(z - sp)
                    dz = jnp.where(mask, g * (1.0 - sig) - (before + gsums[h]) * sig, 0.0) * scale
                    dzb = dz.astype(BF16)
                    dk_acc[pl.ds(k0, tk), hs] += _dot(dzb, qbs[h], "tn")
                    dv_acc[pl.ds(k0, tk), hs] += _dot(w.astype(BF16), dobs[h], "tn")
                    dq_acc[:, hs] += _dot(dzb, kt)
                    out.append(gsums[h] + gtot)
                return tuple(out)

            lax.fori_loop(0, n_strips, strip, zero)
            dq_ref[pl.ds(q0, tq), :] = dq_acc[...].astype(BF16)
            return 0

        lax.fori_loop(0, t // tq, qblock, 0)
        dk_ref[...] = dk_acc[...].astype(BF16)
        dv_ref[...] = dv_acc[...].astype(BF16)

    def col_spec(base):
        return pl.BlockSpec((t, wd), lambda h, base=base: (0, base + h))

    ng = n_heads // hp
    return pl.pallas_call(
        body, name="attn_bwd", grid=(ng,),
        in_specs=[col_spec(0), col_spec(ng), col_spec(2 * ng), col_spec(0)],
        out_specs=[col_spec(0), col_spec(0), col_spec(0)],
        out_shape=[jax.ShapeDtypeStruct((t, n_heads * HEAD_DIM), BF16)] * 3,
        scratch_shapes=[pltpu.VMEM((t, wd), BF16)] * 4 + [pltpu.VMEM((t, wd), F32)] * 2
        + [pltpu.VMEM((tq, wd), F32), pltpu.VMEM((hp, t // tk, tq, 1), F32)],
        compiler_params=_params(1),
    )(proj, proj, proj, dy)


def _glu_halo(vc, gc, vp, gp, ubuf, i, tt, halo):
    uprev = vp[pl.ds(tt - halo, halo), :] * _sigmoid(gp[pl.ds(tt - halo, halo), :])
    ubuf[pl.ds(0, halo), :] = jnp.where(i > 0, uprev, 0.0)
    ubuf[pl.ds(halo, tt), :] = vc[...] * _sigmoid(gc[...])


def conv_fwd(proj, col0, cc, w, b, lng, lnb):
    t = proj.shape[0]
    tt = _tile(t, ROW_T)
    vi, gi = col0 // cc, col0 // cc + 1
    off = CONV_HALO - (DW_LEN - 1)

    def body(vc, gc, vp, gp, w_ref, b_ref, lg_ref, lb_ref, c_ref, y_ref, ubuf):
        i = pl.program_id(0)
        _glu_halo(vc, gc, vp, gp, ubuf, i, tt, CONV_HALO)
        for ch in range(cc // LANE):
            sl = pl.ds(ch * LANE, LANE)
            acc = jnp.zeros((tt, LANE), F32) + b_ref[:, sl]
            for tap in range(DW_LEN):
                acc = acc + w_ref[pl.ds(tap, 1), sl] * ubuf[pl.ds(off + tap, tt), sl]
            c_ref[:, sl] = acc
        c = c_ref[...]
        xc = c - jnp.mean(c, axis=-1, keepdims=True)
        ln = xc * lax.rsqrt(jnp.mean(xc * xc, axis=-1, keepdims=True) + EPS) * lg_ref[...] + lb_ref[...]
        y_ref[...] = ln * _sigmoid(ln)

    cur = lambda c: pl.BlockSpec((tt, cc), lambda i, c=c: (i, c))
    prev = lambda c: pl.BlockSpec((tt, cc), lambda i, c=c: (jnp.maximum(i - 1, 0), c))
    return pl.pallas_call(
        body, name="conv_fwd", grid=(t // tt,),
        in_specs=[cur(vi), cur(gi), prev(vi), prev(gi), pl.BlockSpec((DW_LEN, cc), lambda i: (0, 0)),
                  _vec_spec(cc), _vec_spec(cc), _vec_spec(cc)],
        out_specs=[_row_spec(tt, cc), _row_spec(tt, cc)],
        out_shape=[jax.ShapeDtypeStruct((t, cc), F32)] * 2,
        scratch_shapes=[pltpu.VMEM((CONV_HALO + tt, cc), F32)],
        compiler_params=_params(1),
    )(proj, proj, proj, proj, w, b, lng, lnb)


def conv_bwd(proj, col0, cc, dc, w):
    t = proj.shape[0]
    tt = _tile(t, ROW_T)
    nt = t // tt
    vi, gi = col0 // cc, col0 // cc + 1
    off = CONV_HALO - (DW_LEN - 1)

    def body(vc, gc, vp, gp, dcc, dcn, w_ref, dvg_ref, dw_ref, db_ref, ubuf, dbuf):
        i = pl.program_id(0)

        @pl.when(i == 0)
        def _():
            dw_ref[...] = jnp.zeros_like(dw_ref)
            db_ref[...] = jnp.zeros_like(db_ref)

        _glu_halo(vc, gc, vp, gp, ubuf, i, tt, CONV_HALO)
        dbuf[pl.ds(0, tt), :] = dcc[...]
        dbuf[pl.ds(tt, CONV_HALO), :] = jnp.where(i < nt - 1, dcn[pl.ds(0, CONV_HALO), :], 0.0)
        db_ref[...] += jnp.sum(dcc[...], axis=0, keepdims=True)
        for ch in range(cc // LANE):
            sl = pl.ds(ch * LANE, LANE)
            dcv = dbuf[pl.ds(0, tt), sl]
            du = jnp.zeros((tt, LANE), F32)
            for tap in range(DW_LEN):
                du = du + w_ref[pl.ds(tap, 1), sl] * dbuf[pl.ds(DW_LEN - 1 - tap, tt), sl]
                dw_ref[pl.ds(tap, 1), sl] += jnp.sum(dcv * ubuf[pl.ds(off + tap, tt), sl], axis=0, keepdims=True)
            s = _sigmoid(gc[:, sl])
            val = vc[:, sl]
            dvg_ref[:, sl] = (du * s).astype(BF16)
            dvg_ref[:, pl.ds(cc + ch * LANE, LANE)] = (du * val * s * (1.0 - s)).astype(BF16)

    cur = lambda c: pl.BlockSpec((tt, cc), lambda i, c=c: (i, c))
    prev = lambda c: pl.BlockSpec((tt, cc), lambda i, c=c: (jnp.maximum(i - 1, 0), c))
    return pl.pallas_call(
        body, name="conv_bwd", grid=(nt,),
        in_specs=[cur(vi), cur(gi), prev(vi), prev(gi), _row_spec(tt, cc),
                  pl.BlockSpec((tt, cc), lambda i: (jnp.minimum(i + 1, nt - 1), 0)),
                  pl.BlockSpec((DW_LEN, cc), lambda i: (0, 0))],
        out_specs=[_row_spec(tt, 2 * cc), pl.BlockSpec((DW_LEN, cc), lambda i: (0, 0)), _vec_spec(cc)],
        out_shape=[jax.ShapeDtypeStruct((t, 2 * cc), BF16), jax.ShapeDtypeStruct((DW_LEN, cc), F32),
                   jax.ShapeDtypeStruct((1, cc), F32)],
        scratch_shapes=[pltpu.VMEM((CONV_HALO + tt, cc), F32), pltpu.VMEM((tt + CONV_HALO, cc), F32)],
        compiler_params=_params(1),
    )(proj, proj, proj, proj, dc, dc, w)


def _lru_gates(xbuf, cw_ref, cb_ref, wa_ref, ba_ref, wi_ref, bi_ref, lam_ref, tt, wl):
    bd = wl // LRU_BLOCKS
    xr = jnp.zeros((tt, wl), F32) + cb_ref[...]
    for tap in range(LRU_LEN):
        xr = xr + cw_ref[pl.ds(tap, 1), :] * xbuf[pl.ds(LRU_HALO - (LRU_LEN - 1) + tap, tt), :]
    xb = xr.astype(BF16)
    ga = jnp.concatenate([_dot(xb[:, n * bd:(n + 1) * bd], wa_ref[n]) for n in range(LRU_BLOCKS)], axis=1) + ba_ref[...]
    gi = jnp.concatenate([_dot(xb[:, n * bd:(n + 1) * bd], wi_ref[n]) for n in range(LRU_BLOCKS)], axis=1) + bi_ref[...]
    r = _sigmoid(ga)
    ig = _sigmoid(gi)
    spl = _softplus(-lam_ref[...])
    log_a = -LRU_C * r * spl
    a = jnp.exp(log_a)
    m = jnp.sqrt(_neg_expm1(2.0 * log_a))
    return xr, xb, r, ig, spl, a, m


def _group_scan(a8, b8, reverse):
    rid = lax.broadcasted_iota(jnp.int32, a8.shape, 0)
    aa, bb = a8, b8
    for dist in (1, 2, 4):
        shift = SUBLANE - dist if reverse else dist
        a_sh = pltpu.roll(aa, shift, 0)
        b_sh = pltpu.roll(bb, shift, 0)
        valid = (rid < SUBLANE - dist) if reverse else (rid >= dist)
        bb = jnp.where(valid, aa * b_sh + bb, bb)
        aa = jnp.where(valid, aa * a_sh, aa)
    return aa, bb


def _pick_row(x8, r):
    rid = lax.broadcasted_iota(jnp.int32, x8.shape, 0)
    return jnp.sum(jnp.where(rid == r, x8, 0.0), axis=0, keepdims=True)


def lru_fwd(proj, col0, wl, cw, cb, wa, ba, wi, bi, lam):
    t = proj.shape[0]
    tt = _tile(t, ROW_T)
    xi, yi = col0 // wl, col0 // wl + 1

    def body(xc, xp, ry, cw_ref, cb_ref, wa_ref, ba_ref, wi_ref, bi_ref, lam_ref, hs_ref, y_ref,
             xbuf, a_s, b_s, hcar):
        i = pl.program_id(0)

        @pl.when(i == 0)
        def _():
            hcar[...] = jnp.zeros_like(hcar)

        xbuf[pl.ds(0, LRU_HALO), :] = jnp.where(i > 0, xp[pl.ds(tt - LRU_HALO, LRU_HALO), :], 0.0)
        xbuf[pl.ds(LRU_HALO, tt), :] = xc[...]
        xr, _, _, ig, _, a, m = _lru_gates(xbuf, cw_ref, cb_ref, wa_ref, ba_ref, wi_ref, bi_ref, lam_ref, tt, wl)
        a_s[...] = a
        b_s[...] = m * ig * xr

        def group(gidx, h):
            r0 = pl.multiple_of(gidx * SUBLANE, SUBLANE)
            aa, bb = _group_scan(a_s[pl.ds(r0, SUBLANE), :], b_s[pl.ds(r0, SUBLANE), :], False)
            h8 = aa * h + bb
            hs_ref[pl.ds(r0, SUBLANE), :] = h8
            return _pick_row(h8, SUBLANE - 1)

        hcar[...] = lax.fori_loop(0, tt // SUBLANE, group, hcar[...])
        gel, _ = _gelu_and_grad(ry[...])
        y_ref[...] = hs_ref[...] * gel

    cur = lambda c: pl.BlockSpec((tt, wl), lambda i, c=c: (i, c))
    full = lambda shape: pl.BlockSpec(shape, lambda i: (0,) * len(shape))
    return pl.pallas_call(
        body, name="lru_fwd", grid=(t // tt,),
        in_specs=[cur(xi), pl.BlockSpec((tt, wl), lambda i: (jnp.maximum(i - 1, 0), xi)), cur(yi),
                  full((LRU_LEN, wl)), _vec_spec(wl), full(wa.shape), _vec_spec(wl), full(wi.shape), _vec_spec(wl),
                  _vec_spec(wl)],
        out_specs=[_row_spec(tt, wl), _row_spec(tt, wl)],
        out_shape=[jax.ShapeDtypeStruct((t, wl), F32)] * 2,
        scratch_shapes=[pltpu.VMEM((LRU_HALO + tt, wl), F32), pltpu.VMEM((tt, wl), F32), pltpu.VMEM((tt, wl), F32),
                        pltpu.VMEM((1, wl), F32)],
        compiler_params=_params(1),
    )(proj, proj, proj, cw, cb, wa, ba, wi, bi, lam)


def lru_bwd(proj, col0, wl, hs, dy, cw, cb, wa, ba, wi, bi, lam):
    t = proj.shape[0]
    tt = _tile(t, ROW_T)
    nt = t // tt
    xi, yi = col0 // wl, col0 // wl + 1
    bd = wl // LRU_BLOCKS

    def body(xc, xp, ry, hc, hp, dy_ref, cw_ref, cb_ref, wa_ref, ba_ref, wi_ref, bi_ref, lam_ref,
             dxy_ref, dcw_ref, dcb_ref, dwa_ref, dba_ref, dwi_ref, dbi_ref, dlam_ref,
             xbuf, hbuf, abuf, e_s, dh_s, dxbuf, dhcar):
        i = pl.program_id(0)
        first = i == 0

        @pl.when(first)
        def _():
            for r in (dcw_ref, dcb_ref, dwa_ref, dba_ref, dwi_ref, dbi_ref, dlam_ref, dhcar):
                r[...] = jnp.zeros_like(r)
            abuf[pl.ds(tt, LRU_HALO), :] = jnp.zeros((LRU_HALO, wl), F32)
            dxbuf[pl.ds(tt, LRU_HALO), :] = jnp.zeros((LRU_HALO, wl), F32)

        has_prev = i < nt - 1
        xbuf[pl.ds(0, LRU_HALO), :] = jnp.where(has_prev, xp[pl.ds(tt - LRU_HALO, LRU_HALO), :], 0.0)
        xbuf[pl.ds(LRU_HALO, tt), :] = xc[...]
        hbuf[pl.ds(0, LRU_HALO), :] = jnp.where(has_prev, hp[pl.ds(tt - LRU_HALO, LRU_HALO), :], 0.0)
        hbuf[pl.ds(LRU_HALO, tt), :] = hc[...]
        xr, xb, r, ig, spl, a, m = _lru_gates(xbuf, cw_ref, cb_ref, wa_ref, ba_ref, wi_ref, bi_ref, lam_ref, tt, wl)
        gel, dgel = _gelu_and_grad(ry[...])
        dyv = dy_ref[...]
        e_s[...] = dyv * gel
        dxy_ref[:, pl.ds(wl, wl)] = (dyv * hc[...] * dgel).astype(BF16)
        abuf[pl.ds(0, tt), :] = a
        a_next = abuf[pl.ds(1, tt), :]
        dh_s[...] = a_next

        def group(it, dh_in):
            r0 = pl.multiple_of((tt // SUBLANE - 1 - it) * SUBLANE, SUBLANE)
            aa, bb = _group_scan(dh_s[pl.ds(r0, SUBLANE), :], e_s[pl.ds(r0, SUBLANE), :], True)
            dh8 = aa * dh_in + bb
            dh_s[pl.ds(r0, SUBLANE), :] = dh8
            return _pick_row(dh8, 0)

        dhcar[...] = lax.fori_loop(0, tt // SUBLANE, group, dhcar[...])
        abuf[pl.ds(tt, LRU_HALO), :] = a[0:LRU_HALO, :]
        dh = dh_s[...]
        h_m1 = hbuf[pl.ds(LRU_HALO - 1, tt), :]
        dlog_a = dh * h_m1 * a - dh * ig * xr * (a * a / m)
        dig = dh * m * xr
        dxr = dh * m * ig
        dga = dlog_a * (-LRU_C) * spl * r * (1.0 - r)
        dgi = dig * ig * (1.0 - ig)
        dlam_ref[...] += jnp.sum(dlog_a * r, axis=0, keepdims=True) * (LRU_C * _sigmoid(-lam_ref[...]))
        dba_ref[...] += jnp.sum(dga, axis=0, keepdims=True)
        dbi_ref[...] += jnp.sum(dgi, axis=0, keepdims=True)
        dgab = dga.astype(BF16)
        dgib = dgi.astype(BF16)
        back = []
        for n in range(LRU_BLOCKS):
            sl = slice(n * bd, (n + 1) * bd)
            dwa_ref[n] += _dot(xb[:, sl], dgab[:, sl], "tn")
            dwi_ref[n] += _dot(xb[:, sl], dgib[:, sl], "tn")
            back.append(_dot(dgab[:, sl], wa_ref[n], "nt") + _dot(dgib[:, sl], wi_ref[n], "nt"))
        dxr = dxr + jnp.concatenate(back, axis=1)
        dcb_ref[...] += jnp.sum(dxr, axis=0, keepdims=True)
        dxbuf[pl.ds(0, tt), :] = dxr
        drx = jnp.zeros((tt, wl), F32)
        for tap in range(LRU_LEN):
            drx = drx + cw_ref[pl.ds(tap, 1), :] * dxbuf[pl.ds(LRU_LEN - 1 - tap, tt), :]
            dcw_ref[pl.ds(tap, 1), :] += jnp.sum(
                dxr * xbuf[pl.ds(LRU_HALO - (LRU_LEN - 1) + tap, tt), :], axis=0, keepdims=True)
        dxbuf[pl.ds(tt, LRU_HALO), :] = dxr[0:LRU_HALO, :]
        dxy_ref[:, pl.ds(0, wl)] = drx.astype(BF16)

    rev = lambda c: pl.BlockSpec((tt, wl), lambda i, c=c: (nt - 1 - i, c))
    rev_prev = lambda c: pl.BlockSpec((tt, wl), lambda i, c=c: (jnp.maximum(nt - 2 - i, 0), c))
    full = lambda shape: pl.BlockSpec(shape, lambda i: (0,) * len(shape))
    vec = _vec_spec(wl)
    return pl.pallas_call(
        body, name="lru_bwd", grid=(nt,),
        in_specs=[rev(xi), rev_prev(xi), rev(yi), rev(0), rev_prev(0), rev(0),
                  full((LRU_LEN, wl)), vec, full(wa.shape), vec, full(wi.shape), vec, vec],
        out_specs=[pl.BlockSpec((tt, 2 * wl), lambda i: (nt - 1 - i, 0)), full((LRU_LEN, wl)), vec,
                   full(wa.shape), vec, full(wi.shape), vec, vec],
        out_shape=[jax.ShapeDtypeStruct((t, 2 * wl), BF16), jax.ShapeDtypeStruct((LRU_LEN, wl), F32),
                   jax.ShapeDtypeStruct((1, wl), F32), jax.ShapeDtypeStruct(wa.shape, F32),
                   jax.ShapeDtypeStruct((1, wl), F32), jax.ShapeDtypeStruct(wi.shape, F32),
                   jax.ShapeDtypeStruct((1, wl), F32), jax.ShapeDtypeStruct((1, wl), F32)],
        scratch_shapes=[pltpu.VMEM((LRU_HALO + tt, wl), F32), pltpu.VMEM((LRU_HALO + tt, wl), F32),
                        pltpu.VMEM((tt + LRU_HALO, wl), F32), pltpu.VMEM((tt, wl), F32), pltpu.VMEM((tt, wl), F32),
                        pltpu.VMEM((tt + LRU_HALO, wl), F32), pltpu.VMEM((1, wl), F32)],
        compiler_params=_params(1),
    )(proj, proj, proj, hs, hs, dy, cw, cb, wa, ba, wi, bi, lam)


def _adamw(w, g, m, v):
    m = ADAM_B1 * m + (1.0 - ADAM_B1) * g
    v = ADAM_B2 * v + (1.0 - ADAM_B2) * (g * g)
    m_hat = m / (1.0 - ADAM_B1 ** ADAM_STEP)
    v_hat = v / (1.0 - ADAM_B2 ** ADAM_STEP)
    delta = -ADAM_LR * (m_hat / (jnp.sqrt(v_hat) + ADAM_EPS) + ADAM_WD * w)
    return delta, m, v


def adam_big(name, w, m, v, parts, chip):
    n_layers, rows, cols = w.shape
    tr = _tile(rows, 128 if cols > 1024 else 256)
    nrt = rows // tr

    def body(chip_ref, *refs):
        w_ref, m_ref, v_ref = refs[:3]
        part_refs = refs[3:3 + 4 * n_layers]
        g_ref, d_ref, mo_ref, vo_ref = refs[3 + 4 * n_layers:]
        layer = pl.program_id(0)
        for l in range(n_layers):
            @pl.when(layer == l)
            def _(l=l):
                g = part_refs[4 * l][...].astype(F32)
                for p in range(1, 4):
                    g = g + part_refs[4 * l + p][...].astype(F32)
                delta, mn, vn = _adamw(w_ref[...], g, m_ref[...], v_ref[...])
                g_ref[...] = g
                d_ref[...] = delta
                mo_ref[...] = mn
                vo_ref[...] = vn

    wspec = pl.BlockSpec((None, tr, cols), lambda l, i, chip_ref: (l, i, 0))
    operands, in_specs = [w, m, v], [wspec, wspec, wspec]
    for l in range(n_layers):
        mine, recv = parts[l]
        operands.append(mine)
        in_specs.append(pl.BlockSpec(
            (None, tr, cols), lambda ll, i, chip_ref, l=l: (chip_ref[0], jnp.where(ll == l, i, 0), 0)))
        for p in range(3):
            operands.append(recv)
            in_specs.append(pl.BlockSpec(
                (None, tr, cols), lambda ll, i, chip_ref, l=l, p=p: (p, jnp.where(ll == l, i, 0), 0)))
    return pl.pallas_call(
        body, name=name,
        grid_spec=pltpu.PrefetchScalarGridSpec(
            num_scalar_prefetch=1, grid=(n_layers, nrt), in_specs=in_specs, out_specs=[wspec] * 4),
        out_shape=[jax.ShapeDtypeStruct(w.shape, F32)] * 4, compiler_params=_params(2),
    )(chip, *operands)


def adam_small(w, m, v, g):
    rows = w.shape[0]
    tr = _tile(rows, PACK_ROWS)

    def body(w_ref, m_ref, v_ref, g_ref, d_ref, mo_ref, vo_ref):
        delta, mn, vn = _adamw(w_ref[...], g_ref[...], m_ref[...], v_ref[...])
        d_ref[...] = delta
        mo_ref[...] = mn
        vo_ref[...] = vn

    spec = _row_spec(tr, LANE)
    return pl.pallas_call(
        body, name="adam_small", grid=(rows // tr,), in_specs=[spec] * 4, out_specs=[spec] * 3,
        out_shape=[jax.ShapeDtypeStruct(w.shape, F32)] * 3, compiler_params=_params(1),
    )(w, m, v, g)


def sum_parts(parts):
    _, rows, _ = parts.shape
    tr = _tile(rows, PACK_ROWS)

    def body(p_ref, o_ref):
        acc = p_ref[0]
        for k in range(1, N_DEV):
            acc = acc + p_ref[k]
        o_ref[...] = acc

    return pl.pallas_call(
        body, name="sum_parts", grid=(rows // tr,),
        in_specs=[pl.BlockSpec((N_DEV, tr, LANE), lambda i: (0, i, 0))], out_specs=_row_spec(tr, LANE),
        out_shape=jax.ShapeDtypeStruct((rows, LANE), F32), compiler_params=_params(1),
    )(parts)


def place_own(x, me, dtype):
    rows, cols = x.shape
    tr = _tile(rows, 256)

    def body(me_ref, x_ref, o_ref):
        o_ref[...] = x_ref[...].astype(dtype)

    return pl.pallas_call(
        body, name="place_own",
        grid_spec=pltpu.PrefetchScalarGridSpec(
            num_scalar_prefetch=1, grid=(rows // tr,),
            in_specs=[pl.BlockSpec((tr, cols), lambda i, me_ref: (i, 0))],
            out_specs=pl.BlockSpec((None, tr, cols), lambda i, me_ref: (me_ref[0], i, 0))),
        out_shape=jax.ShapeDtypeStruct((N_DEV, rows, cols), dtype), compiler_params=_params(1),
    )(me, x)


_HBM = pl.BlockSpec(memory_space=pltpu.HBM)


def _place():
    return lax.axis_index("x"), lax.axis_index("y"), lax.axis_index("c")


def _other_chips(x, y):
    return [(1 - x, y), (x, 1 - y), (1 - x, 1 - y)]


def all_gather(name, shard, me, dtype=None):
    def body(buf_ref, out_ref, send_sems, recv_sems):
        del buf_ref
        x, y, c = _place()
        mine, sibling = (x, y, c), (x, y, 1 - c)
        chips = _other_chips(x, y)

        def copy(k, block, to):
            slot = out_ref.at[4 * block[0] + 2 * block[1] + block[2]]
            return pltpu.make_async_remote_copy(
                src_ref=slot, dst_ref=slot, send_sem=send_sems.at[k], recv_sem=recv_sems.at[k],
                device_id=to, device_id_type=pl.DeviceIdType.MESH)

        first = [copy(0, mine, sibling)] + [copy(1 + j, mine, (*chip, c)) for j, chip in enumerate(chips)]
        for cp in first:
            cp.start()
        passed = [copy(4 + j, (*chip, c), sibling) for j, chip in enumerate(chips)]
        for j, chip in enumerate(chips):
            copy(1 + j, (*chip, c), mine).wait_recv()
            passed[j].start()
        copy(0, sibling, mine).wait_recv()
        for j, chip in enumerate(chips):
            copy(4 + j, (*chip, 1 - c), mine).wait_recv()
        for cp in first + passed:
            cp.wait_send()

    buf = place_own(shard, me, dtype or shard.dtype)
    return pl.pallas_call(
        body, name=name, out_shape=jax.ShapeDtypeStruct(buf.shape, buf.dtype),
        in_specs=[_HBM], out_specs=_HBM, input_output_aliases={0: 0},
        scratch_shapes=[pltpu.SemaphoreType.DMA((7,)), pltpu.SemaphoreType.DMA((7,))],
    )(buf)


def _own_block_copies(src_refs, dst_refs, send_sems, recv_sems, arrivals):
    x, y, c = _place()
    peers = [(x, y, 1 - c)] + [(*chip, c) for chip in _other_chips(x, y)]
    copies = []
    for b, (src, dst) in enumerate(zip(src_refs, dst_refs)):
        for k, peer in enumerate(peers):
            def copy(landing, b=b, k=k, peer=peer, src=src, dst=dst):
                return pltpu.make_async_remote_copy(
                    src_ref=src.at[4 * x + 2 * y + c], dst_ref=dst.at[landing],
                    send_sem=send_sems.at[4 * b + k], recv_sem=recv_sems.at[4 * b + k],
                    device_id=peer, device_id_type=pl.DeviceIdType.MESH)
            copies.append((copy(4 * x + 2 * y + c), copy(4 * peer[0] + 2 * peer[1] + peer[2]) if arrivals else None))
    return copies


def gather_start(name, bufs, after):
    n = len(bufs)

    def body(*refs):
        send_sems, recv_sems = refs[n + 1], refs[n + 2]
        thru = refs[n + 3:2 * n + 3]
        for send, _ in _own_block_copies(thru, thru, send_sems, recv_sems, False):
            send.start()
        refs[2 * n + 3][...] = jnp.zeros((SUBLANE, LANE), F32)

    return pl.pallas_call(
        body, name=name,
        out_shape=(pltpu.SemaphoreType.DMA((4 * n,)), pltpu.SemaphoreType.DMA((4 * n,)),
                   *[pltpu.HBM(b.shape, b.dtype) for b in bufs], jax.ShapeDtypeStruct((SUBLANE, LANE), F32)),
        in_specs=(*(_HBM,) * n, _ANY), out_specs=(_SEM, _SEM, *(_HBM,) * n, _TOKEN),
        input_output_aliases={b: 2 + b for b in range(n)},
        compiler_params=pltpu.CompilerParams(has_side_effects=_EFFECT),
    )(*[_hbm(b) for b in bufs], after)


def gather_wait(name, state, first, count, after):
    send_sems, recv_sems = state[:2]
    bufs = state[2 + first:2 + first + count]
    n = len(bufs)

    def body(*refs):
        ins = refs[:n]
        send_sems, recv_sems = refs[n], refs[n + 1]
        shift = 4 * first
        for send, arrival in _own_block_copies(
                ins, ins, send_sems.at[pl.ds(shift, 4 * n)], recv_sems.at[pl.ds(shift, 4 * n)], True):
            send.wait_send()
            arrival.wait_recv()

    return pl.pallas_call(
        body, name=name, out_shape=tuple(pltpu.HBM(b.shape, b.dtype) for b in bufs),
        in_specs=(*(_HBM,) * n, _SEM, _SEM, _ANY), out_specs=(_HBM,) * n,
        input_output_aliases={b: b for b in range(n)},
        compiler_params=pltpu.CompilerParams(has_side_effects=_EFFECT),
    )(*bufs, send_sems, recv_sems, after)


def gather_finish(name, bufs):
    n = len(bufs)

    def body(*refs):
        outs = refs[n:2 * n]
        send_sems, recv_sems = refs[2 * n], refs[2 * n + 1]
        x, y, c = _place()
        copies = []
        for b, out in enumerate(outs):
            for k, chip in enumerate(_other_chips(x, y)):
                sem = 3 * b + k
                copies.append((
                    pltpu.make_async_remote_copy(
                        src_ref=out.at[4 * chip[0] + 2 * chip[1] + c], dst_ref=out.at[4 * chip[0] + 2 * chip[1] + c],
                        send_sem=send_sems.at[sem], recv_sem=recv_sems.at[sem],
                        device_id=(x, y, 1 - c), device_id_type=pl.DeviceIdType.MESH),
                    pltpu.make_async_remote_copy(
                        src_ref=out.at[4 * chip[0] + 2 * chip[1] + c], dst_ref=out.at[4 * chip[0] + 2 * chip[1] + 1 - c],
                        send_sem=send_sems.at[sem], recv_sem=recv_sems.at[sem],
                        device_id=(x, y, 1 - c), device_id_type=pl.DeviceIdType.MESH)))
        for send, _ in copies:
            send.start()
        for send, arrival in copies:
            send.wait_send()
            arrival.wait_recv()

    return pl.pallas_call(
        body, name=name, out_shape=tuple(jax.ShapeDtypeStruct(b.shape, b.dtype) for b in bufs),
        in_specs=(_HBM,) * n, out_specs=(_HBM,) * n, input_output_aliases={b: b for b in range(n)},
        scratch_shapes=[pltpu.SemaphoreType.DMA((3 * n,)), pltpu.SemaphoreType.DMA((3 * n,))],
    )(*bufs)


_SEM =pl.BlockSpec(memory_space=pltpu.SEMAPHORE)
_ANY = pl.BlockSpec(memory_space=pl.ANY)
_TOKEN = pl.BlockSpec(memory_space=pltpu.VMEM)
_EFFECT = pltpu.SideEffectType.DATAFLOW_SIDE_EFFECTING


def _hbm(a):
    return pltpu.with_memory_space_constraint(a, pltpu.HBM)


def _chip_copies(p_ref, land_ref, send_sems, recv_sems):
    x, y, c = _place()
    return [pltpu.make_async_remote_copy(
        src_ref=p_ref.at[2 * px + py], dst_ref=land_ref.at[k], send_sem=send_sems.at[k], recv_sem=recv_sems.at[k],
        device_id=(px, py, c), device_id_type=pl.DeviceIdType.MESH) for k, (px, py) in enumerate(_other_chips(x, y))]


def scatter_chips_start(name, p):
    _, rows, cols = p.shape

    def body(p_ref, land_ref, send_sems, recv_sems, p_thru, land_thru, token):
        for cp in _chip_copies(p_ref, land_ref, send_sems, recv_sems):
            cp.start()
        token[...] = jnp.zeros_like(token)

    return pl.pallas_call(
        body, name=name,
        out_shape=(pltpu.SemaphoreType.DMA((3,)), pltpu.SemaphoreType.DMA((3,)), pltpu.HBM(p.shape, p.dtype),
                   pltpu.HBM((3, rows, cols), p.dtype), jax.ShapeDtypeStruct((SUBLANE, LANE), F32)),
        in_specs=(_HBM, _HBM), out_specs=(_SEM, _SEM, _HBM, _HBM, _TOKEN), input_output_aliases={0: 2, 1: 3},
        compiler_params=pltpu.CompilerParams(has_side_effects=_EFFECT),
    )(_hbm(p), _hbm(lax.empty((3, rows, cols), p.dtype)))


def scatter_chips_wait(name, send_sems, recv_sems, p_thru, land_thru, after):
    def body(p_ref, land_ref, send_sems, recv_sems, after_ref, p_out, land_out):
        for cp in _chip_copies(p_ref, land_ref, send_sems, recv_sems):
            cp.wait_send()
            cp.wait_recv()

    return pl.pallas_call(
        body, name=name,
        out_shape=(pltpu.HBM(p_thru.shape, p_thru.dtype), pltpu.HBM(land_thru.shape, land_thru.dtype)),
        in_specs=(_HBM, _HBM, _SEM, _SEM, _ANY), out_specs=(_HBM, _HBM), input_output_aliases={0: 0, 1: 1},
        compiler_params=pltpu.CompilerParams(has_side_effects=_EFFECT),
    )(p_thru, land_thru, send_sems, recv_sems, after)


def _pair_copies(g_ref, land_ref, send_sems, recv_sems):
    x, y, c = _place()
    return [pltpu.make_async_remote_copy(
        src_ref=g_ref.at[k], dst_ref=land_ref.at[k], send_sem=send_sems.at[k], recv_sem=recv_sems.at[k],
        device_id=(x, y, 1 - c), device_id_type=pl.DeviceIdType.MESH) for k in range(N_DEV // 2)]


def pair_start(name, g, after):
    n = g.shape[0]

    def body(g_ref, land_ref, after_ref, send_sems, recv_sems, g_thru, land_thru, token):
        for cp in _pair_copies(g_ref, land_ref, send_sems, recv_sems):
            cp.start()
        token[...] = jnp.zeros_like(token)

    return pl.pallas_call(
        body, name=name,
        out_shape=(pltpu.SemaphoreType.DMA((n,)), pltpu.SemaphoreType.DMA((n,)), pltpu.HBM(g.shape, g.dtype),
                   pltpu.HBM(g.shape, g.dtype), jax.ShapeDtypeStruct((SUBLANE, LANE), F32)),
        in_specs=(_HBM, _HBM, _ANY), out_specs=(_SEM, _SEM, _HBM, _HBM, _TOKEN), input_output_aliases={0: 2, 1: 3},
        compiler_params=pltpu.CompilerParams(has_side_effects=_EFFECT),
    )(_hbm(g), _hbm(lax.empty(g.shape, g.dtype)), after)


def pair_wait(name, state, after):
    send_sems, recv_sems, g_thru, land_thru, _ = state

    def body(g_ref, land_ref, send_sems, recv_sems, after_ref, g_out, land_out):
        for cp in _pair_copies(g_ref, land_ref, send_sems, recv_sems):
            cp.wait_send()
            cp.wait_recv()

    return pl.pallas_call(
        body, name=name,
        out_shape=(pltpu.HBM(g_thru.shape, g_thru.dtype), pltpu.HBM(land_thru.shape, land_thru.dtype)),
        in_specs=(_HBM, _HBM, _SEM, _SEM, _ANY), out_specs=(_HBM, _HBM), input_output_aliases={0: 0, 1: 1},
        compiler_params=pltpu.CompilerParams(has_side_effects=_EFFECT),
    )(g_thru, land_thru, send_sems, recv_sems, after)[1]


def reduce_scatter_wait(tag, state, after):
    send_sems, recv_sems, p_thru, land_thru, _ = state
    return scatter_chips_wait("rs_wait_" + tag, send_sems, recv_sems, p_thru, land_thru, after)


_SMALL = ("g_pre_mix", "g_post_mix", "g_pre_ffn", "g_post_ffn", "g_attn_grp", "g_conv_grp", "g_lru_grp",
          "dw_conv_w", "dw_conv_b", "conv_ln_g", "conv_ln_b", "lru_conv_w", "lru_conv_b",
          "lru_w_a", "lru_b_a", "lru_w_i", "lru_b_i", "lru_lambda")
_COL_SHARDED_SMALL = ("dw_conv_w", "lru_conv_w")
_BIG = ("w_in", "w_out", "w_gate", "w_up", "w_down")
_TRANSPOSED = ("w_gate", "w_up")
_ALL = ("w_in", "w_out", "g_pre_mix", "g_post_mix", "g_pre_ffn", "g_post_ffn", "g_attn_grp", "g_conv_grp", "g_lru_grp",
        "dw_conv_w", "dw_conv_b", "conv_ln_g", "conv_ln_b", "lru_conv_w", "lru_conv_b", "lru_w_a", "lru_b_a",
        "lru_w_i", "lru_b_i", "lru_lambda", "w_gate", "w_up", "w_down")


def _pack(arrays):
    flat = jnp.concatenate([a.reshape(-1) for a in arrays])
    pad = (-flat.shape[0]) % (PACK_ROWS * LANE)
    return jnp.pad(flat, (0, pad)).reshape(-1, LANE)


def _unpack(packed, shapes):
    flat = packed.reshape(-1)
    out, pos = [], 0
    for s in shapes:
        n = math.prod(s)
        out.append(flat[pos:pos + n].reshape(s))
        pos += n
    return out


def kernel(x, w_in, w_out, g_pre_mix, g_post_mix, g_pre_ffn, g_post_ffn, g_attn_grp, g_conv_grp, g_lru_grp, dw_conv_w, dw_conv_b, conv_ln_g, conv_ln_b, lru_conv_w, lru_conv_b, lru_w_a, lru_b_a, lru_w_i, lru_b_i, lru_lambda, w_gate, w_up, w_down, loss_target, m_w_in, m_w_out, m_g_pre_mix, m_g_post_mix, m_g_pre_ffn, m_g_post_ffn, m_g_attn_grp, m_g_conv_grp, m_g_lru_grp, m_dw_conv_w, m_dw_conv_b, m_conv_ln_g, m_conv_ln_b, m_lru_conv_w, m_lru_conv_b, m_lru_w_a, m_lru_b_a, m_lru_w_i, m_lru_b_i, m_lru_lambda, m_w_gate, m_w_up, m_w_down, v_w_in, v_w_out, v_g_pre_mix, v_g_post_mix, v_g_pre_ffn, v_g_post_ffn, v_g_attn_grp, v_g_conv_grp, v_g_lru_grp, v_dw_conv_w, v_dw_conv_b, v_conv_ln_g, v_conv_ln_b, v_lru_conv_w, v_lru_conv_b, v_lru_w_a, v_lru_b_a, v_lru_w_i, v_lru_b_i, v_lru_lambda, v_w_gate, v_w_up, v_w_down):
    env = dict(locals())
    wts = {n: env[n] for n in _ALL}
    mom = {n: env["m_" + n] for n in _ALL}
    var = {n: env["v_" + n] for n in _ALL}
    for group in (wts, mom, var):
        for n in _TRANSPOSED:
            group[n] = jnp.swapaxes(group[n], 1, 2)

    depth = w_in.shape[0]
    h = x[0]
    target = loss_target[0]
    t, d = h.shape
    attn_w = d // 2
    n_heads = attn_w // HEAD_DIM
    cc = d // 4
    wl = d // 4
    conv_col, lru_col = 3 * attn_w, 3 * attn_w + 2 * cc
    me = 4 * lax.axis_index("x") + 2 * lax.axis_index("y") + lax.axis_index("c")
    me_s = me.astype(jnp.int32).reshape(1)
    chip_s = (2 * lax.axis_index("x") + lax.axis_index("y")).astype(jnp.int32).reshape(1)
    core_s = lax.axis_index("c").astype(jnp.int32).reshape(1)

    n_taps = DW_LEN + LRU_LEN
    taps = jnp.concatenate([dw_conv_w, lru_conv_w], axis=1).reshape(depth * n_taps, cc // N_DEV)
    taps = all_gather("ag_taps", taps, me_s)
    taps = jnp.moveaxis(taps.reshape(N_DEV, depth, n_taps, cc // N_DEV), 0, 2).reshape(depth, n_taps, cc)
    dw_full, lcw_full = taps[:, :DW_LEN], taps[:, DW_LEN:]

    def vec(a, l):
        return a[l].reshape(1, -1)

    ag_state, started = [], taps
    for l in range(depth):
        ag_state.append(gather_start(f"ag_start_{l}", [place_own(wts[n][l], me_s, BF16) for n in _BIG], started))
        started = ag_state[l][-1]
    started = started[0:1, 0:1]

    saved = []
    u1 = rms_pre(h, vec(g_pre_mix, 0) + started)
    loss_sum = dh = dbr = None
    for l in range(depth):
        n_first = 1 if l == 0 else len(_BIG)
        wg = dict(zip(_BIG, gather_finish("ag_finish_a", gather_wait(f"ag_wait_{l}a", ag_state[l], 0, n_first, u1))))
        wa_b, wi_b = lru_w_a[l].astype(BF16), lru_w_i[l].astype(BF16)
        proj = mm_proj(u1, wg["w_in"])
        y_attn = attn_fwd(proj, n_heads)
        cpre, y_conv = conv_fwd(proj, conv_col, cc, dw_full[l], vec(dw_conv_b, l), vec(conv_ln_g, l), vec(conv_ln_b, l))
        hs, y_lru = lru_fwd(proj, lru_col, wl, lcw_full[l], vec(lru_conv_b, l), wa_b, vec(lru_b_a, l), wi_b,
                            vec(lru_b_i, l), vec(lru_lambda, l))
        mixed = mix_fwd(y_attn, y_conv, y_lru, vec(g_attn_grp, l), vec(g_conv_grp, l), vec(g_lru_grp, l))
        if n_first < len(_BIG):
            rest = gather_wait(f"ag_wait_{l}b", ag_state[l], n_first, len(_BIG) - n_first, mixed)
            wg.update(zip(_BIG[n_first:], gather_finish("ag_finish_b", rest)))
        wg["w_out"] = wg["w_out"].reshape(attn_w + cc + wl, d)
        o = mm_plain("mm_out", mixed, wg["w_out"], "nn", F32)
        h2, u2 = res_norm(h, o, vec(g_post_mix, l), vec(g_pre_ffn, l))
        gt, up, f = ffn_up(u2, wg["w_gate"], wg["w_up"])
        dn = mm_down(f, wg["w_down"])
        saved.append(dict(wg=wg, wa_b=wa_b, wi_b=wi_b, h=h, u1=u1, proj=proj, y_attn=y_attn, cpre=cpre, y_conv=y_conv,
                          hs=hs, y_lru=y_lru, mixed=mixed, o=o, h2=h2, u2=u2, gt=gt, up=up, f=f, dn=dn))
        if l + 1 < depth:
            h, u1 = res_norm(h2, dn, vec(g_post_ffn, l), vec(g_pre_mix, l + 1))
        else:
            loss_sum, dh, dbr, dg_post_ffn = final_loss(h2, dn, vec(g_post_ffn, l), target)

    loss = lax.psum(0.5 * loss_sum[0, 0] / d, MESH_AXES)

    small = {n: [None] * depth for n in _SMALL}
    rs_state = {n: [None] * depth for n in _BIG}
    after = dbr
    for l in reversed(range(depth)):
        s = saved[l]
        wg = s["wg"]
        small["g_post_ffn"][l] = dg_post_ffn
        dgt, dup = ffn_bwd(dbr, wg["w_down"], s["gt"], s["up"])
        ffn_grads = (("w_down", "down", s["f"], dbr), ("w_gate", "gate", dgt, s["u2"]), ("w_up", "up", dup, s["u2"]))
        pairs = {}
        for n, tag, a, g in ffn_grads:
            theirs = mm_dw_half("mm_dw_" + tag, "rows", a, g, core_s, True)
            pairs[n] = pair_start(f"pair_start_{tag}_{l}", theirs, after)
            after = pairs[n][-1]
        for n, tag, a, g in ffn_grads:
            recv = pair_wait(f"pair_wait_{tag}_{l}", pairs[n], after)
            after = mm_dw_half("mm_dw_" + tag, "rows", a, g, core_s, False, add=recv)
            rs_state[n][l] = scatter_chips_start(f"rs_start_{tag}_{l}", after)
        du2 = mm_dx_ffn(dgt, wg["w_gate"], dup, wg["w_up"])
        started = sum(rs_state[n][l][-1][0:1, 0:1] for n in ("w_down", "w_gate", "w_up"))
        dh2, small["g_pre_ffn"][l], do, small["g_post_mix"][l] = norm_bwd(
            dh, du2, s["h2"], vec(g_pre_ffn, l) + started, (s["o"], vec(g_post_mix, l)))
        theirs = mm_dw_half("mm_dw_out", "take", s["mixed"], do, core_s, True)
        pairs["w_out"] = pair_start(f"pair_start_out_{l}", theirs, rs_state["w_up"][l][-1])
        dmixed = mm_plain("mm_dmixed", do, wg["w_out"], "nt", F32)
        recv = pair_wait(f"pair_wait_out_{l}", pairs["w_out"], dmixed)
        rs_state["w_out"][l] = scatter_chips_start(
            f"rs_start_out_{l}", mm_dw_half("mm_dw_out", "take", s["mixed"], do, core_s, False, add=recv))
        (dya, dc, dyl, small["g_attn_grp"][l], small["g_conv_grp"][l], small["g_lru_grp"][l],
         small["conv_ln_g"][l], small["conv_ln_b"][l]) = mix_bwd(
            dmixed, s["y_attn"], s["y_conv"], s["y_lru"], s["cpre"],
            vec(g_attn_grp, l) + rs_state["w_out"][l][-1][0:1, 0:1], vec(g_conv_grp, l),
            vec(g_lru_grp, l), vec(conv_ln_g, l), vec(conv_ln_b, l))
        dq, dk, dv = attn_bwd(s["proj"], dya, n_heads)
        dvg, small["dw_conv_w"][l], small["dw_conv_b"][l] = conv_bwd(s["proj"], conv_col, cc, dc, dw_full[l])
        (dxy, small["lru_conv_w"][l], small["lru_conv_b"][l], small["lru_w_a"][l], small["lru_b_a"][l],
         small["lru_w_i"][l], small["lru_b_i"][l], small["lru_lambda"][l]) = lru_bwd(
            s["proj"], lru_col, wl, s["hs"], dyl, lcw_full[l], vec(lru_conv_b, l), s["wa_b"], vec(lru_b_a, l),
            s["wi_b"], vec(lru_b_i, l), vec(lru_lambda, l))
        dproj = jnp.concatenate([dq, dk, dv, dvg, dxy], axis=1)
        theirs = mm_dw_half("mm_dw_in", "cols", s["u1"], dproj, core_s, True)
        pairs["w_in"] = pair_start(f"pair_start_in_{l}", theirs, rs_state["w_out"][l][-1])
        du1 = mm_dx_cols("mm_dx_in", dproj, wg["w_in"])
        recv = pair_wait(f"pair_wait_in_{l}", pairs["w_in"], du1)
        after = mm_dw_half("mm_dw_in", "cols", s["u1"], dproj, core_s, False, add=recv)
        rs_state["w_in"][l] = scatter_chips_start(f"rs_start_in_{l}", after)
        g_pre = vec(g_pre_mix, l) + rs_state["w_in"][l][-1][0:1, 0:1]
        if l > 0:
            p = saved[l - 1]
            dh, small["g_pre_mix"][l], dbr, dg_post_ffn = norm_bwd(
                dh2, du1, s["h"], g_pre, (p["dn"], vec(g_post_ffn, l - 1)))
        else:
            dh, small["g_pre_mix"][l] = norm_bwd(dh2, du1, s["h"], g_pre)
    grad_x = dh[None]
    big_parts = {n: [reduce_scatter_wait(f"{n[2:]}_{l}", rs_state[n][l], dh) for l in range(depth)] for n in _BIG}

    small_shapes = [(depth,) + tuple(wts[n].shape[1:]) if n not in _COL_SHARDED_SMALL
                    else (depth, wts[n].shape[1], cc) for n in _SMALL]
    part = _pack([jnp.stack([a.reshape(shp[1:]) for a in small[n]]) for n, shp in zip(_SMALL, small_shapes)])
    g_small = _unpack(sum_parts(all_gather("ag_small", part, me_s)), small_shapes)
    grads = {}
    for n, g in zip(_SMALL, g_small):
        if n in _COL_SHARDED_SMALL:
            g = lax.dynamic_slice_in_dim(g, me * (cc // N_DEV), cc // N_DEV, axis=2)
        grads[n] = g
    local_shapes = [tuple(wts[n].shape) for n in _SMALL]
    d_small, m_small, v_small = adam_small(
        _pack([wts[n] for n in _SMALL]), _pack([mom[n] for n in _SMALL]), _pack([var[n] for n in _SMALL]),
        _pack([grads[n] for n in _SMALL]))
    delta = dict(zip(_SMALL, _unpack(d_small, local_shapes)))
    new_m = dict(zip(_SMALL, _unpack(m_small, local_shapes)))
    new_v = dict(zip(_SMALL, _unpack(v_small, local_shapes)))

    for n in _BIG:
        shape = wts[n].shape
        _, rows, cols = big_parts[n][0][0].shape
        view = (depth, rows, cols)
        g, dl, mn, vn = adam_big("adam_" + n, wts[n].reshape(view), mom[n].reshape(view), var[n].reshape(view),
                                 big_parts[n], chip_s)
        grads[n], delta[n], new_m[n], new_v[n] = (a.reshape(shape) for a in (g, dl, mn, vn))
    for group in (grads, delta, new_m, new_v):
        for n in _TRANSPOSED:
            group[n] = jnp.swapaxes(group[n], 1, 2)

    return (loss, grad_x, *[grads[n] for n in _ALL], *[delta[n] for n in _ALL],
            *[new_m[n] for n in _ALL], *[new_v[n] for n in _ALL])
```

```python
import functools
import math

import jax
import jax.numpy as jnp
from jax import lax
from jax.experimental import pallas as pl
from jax.experimental.pallas import tpu as pltpu

F32 = jnp.float32
BF16 = jnp.bfloat16

N_DEV = 8
EPS = 1e-6
HEAD_DIM = 128
DW_LEN = 31
LRU_LEN = 4
LRU_BLOCKS = 4
LRU_C = 8.0
ATT_TQ = 512
ATT_TK = 512
ATT_SUM = 256
ATT_HEADS = 2
ROW_T = 256
CONV_HALO = 32
LRU_HALO = 8
LANE = 128
SUBLANE = 8
PACK_ROWS = 512
EXPOSED_GATHERS = 2
VMEM_LIMIT = 56 * 1024 * 1024

ADAM_LR = 0.001
ADAM_B1 = 0.9
ADAM_B2 = 0.999
ADAM_EPS = 1e-08
ADAM_WD = 0.01
ADAM_STEP = 10

MESH_AXES = ("x", "y", "c")
_DIMS = {
    "nn": (((1,), (0,)), ((), ())),
    "nt": (((1,), (1,)), ((), ())),
    "tn": (((0,), (0,)), ((), ())),
}


def _params(n_axes):
    return pltpu.CompilerParams(
        dimension_semantics=("arbitrary",) * n_axes, vmem_limit_bytes=VMEM_LIMIT)


def _dot(a, b, mode="nn"):
    return lax.dot_general(a, b, _DIMS[mode], preferred_element_type=F32)


def _sigmoid(x):
    return 1.0 / (1.0 + jnp.exp(-x))


def _softplus(x):
    return jnp.maximum(x, 0.0) + jnp.log(1.0 + jnp.exp(-jnp.abs(x)))


def _neg_expm1(x):
    series = x * (1.0 + x * (0.5 + x * (1.0 / 6 + x * (1.0 / 24 + x * (1.0 / 120 + x * (1.0 / 720))))))
    return jnp.where(x > -0.25, -series, 1.0 - jnp.exp(x))


_GELU_C = math.sqrt(2.0 / math.pi)


def _gelu_and_grad(x):
    inner = _GELU_C * (x + 0.044715 * x * x * x)
    t = jnp.tanh(inner)
    val = 0.5 * x * (1.0 + t)
    grad = 0.5 * (1.0 + t) + 0.5 * x * (1.0 - t * t) * _GELU_C * (1.0 + 3 * 0.044715 * x * x)
    return val, grad


def _rms_stats(x):
    r = lax.rsqrt(jnp.mean(x * x, axis=-1, keepdims=True) + EPS)
    return x * r, r


def _rms_bwd(dy, x, g):
    xn, r = _rms_stats(x)
    dxn = dy * g
    dx = r * (dxn - xn * jnp.mean(dxn * xn, axis=-1, keepdims=True))
    return dx, jnp.sum(dy * xn, axis=0, keepdims=True)


def _row_spec(tr, width, col=0):
    return pl.BlockSpec((tr, width), lambda i, col=col: (i, col))


def _vec_spec(width):
    return pl.BlockSpec((1, width), lambda i: (0, 0))


def _matmul(name, mode, operands, in_specs, out_shape, out_spec, grid, out_block):
    npairs = len(operands) // 2
    nk = grid[2]

    def body(*refs):
        o_ref = refs[2 * npairs]

        def partial():
            acc = None
            for p in range(npairs):
                d = _dot(refs[2 * p][...], refs[2 * p + 1][...], mode)
                acc = d if acc is None else acc + d
            return acc

        if nk == 1:
            o_ref[...] = partial().astype(o_ref.dtype)
        else:
            acc_ref = refs[2 * npairs + 1]
            k = pl.program_id(2)

            @pl.when(k == 0)
            def _():
                acc_ref[...] = jnp.zeros_like(acc_ref)

            acc_ref[...] += partial()

            @pl.when(k == nk - 1)
            def _():
                o_ref[...] = acc_ref[...].astype(o_ref.dtype)

    return pl.pallas_call(
        body, name=name, grid=grid, in_specs=in_specs, out_specs=out_spec, out_shape=out_shape,
        scratch_shapes=[] if nk == 1 else [pltpu.VMEM(out_block, F32)],
        compiler_params=_params(3),
    )(*operands)


def _tile(n, t):
    if n <= t:
        return n
    return max(k for k in range(SUBLANE, t + 1, SUBLANE) if n % k == 0)


def mm_proj(u, w):
    t, d = u.shape
    nblk, _, nb = w.shape
    tm = _tile(t, 1024)
    return _matmul(
        "mm_proj", "nn", (u, w),
        [pl.BlockSpec((tm, d), lambda j, i, k: (i, 0)), pl.BlockSpec((None, d, nb), lambda j, i, k: (j, 0, 0))],
        jax.ShapeDtypeStruct((t, nblk * nb), F32), pl.BlockSpec((tm, nb), lambda j, i, k: (i, j)),
        (nblk, t // tm, 1), (tm, nb))


def mm_plain(name, a, b, mode, out_dtype):
    if mode == "nn":
        (m, kk), n = a.shape, b.shape[1]
    elif mode == "nt":
        (m, kk), n = a.shape, b.shape[0]
    else:
        (kk, m), n = a.shape, b.shape[1]
    tm, tn = _tile(m, 1024), _tile(n, 1024)
    a_spec = (pl.BlockSpec((kk, tm), lambda i, j, k: (0, i)) if mode == "tn"
              else pl.BlockSpec((tm, kk), lambda i, j, k: (i, 0)))
    b_spec = (pl.BlockSpec((tn, kk), lambda i, j, k: (j, 0)) if mode == "nt"
              else pl.BlockSpec((kk, tn), lambda i, j, k: (0, j)))
    return _matmul(
        name, mode, (a, b), [a_spec, b_spec],
        jax.ShapeDtypeStruct((m, n), out_dtype), pl.BlockSpec((tm, tn), lambda i, j, k: (i, j)),
        (m // tm, n // tn, 1), (tm, tn))


def mm_down(f, w):
    nblk, t, fb = f.shape
    d = w.shape[2]
    tm, tn = _tile(t, 1024), _tile(d, 1024)
    return _matmul(
        "mm_down", "nn", (f, w),
        [pl.BlockSpec((None, tm, fb), lambda i, j, k: (k, i, 0)), pl.BlockSpec((None, fb, tn), lambda i, j, k: (k, 0, j))],
        jax.ShapeDtypeStruct((t, d), F32), pl.BlockSpec((tm, tn), lambda i, j, k: (i, j)),
        (t // tm, d // tn, nblk), (tm, tn))


def mm_dw_half(name, kind, a, g, core, of_sibling, add=None):
    half = N_DEV // 2
    t = g.shape[0]

    def pick(k, core_ref):
        s = 1 - core_ref[0] if of_sibling else core_ref[0]
        return 2 * k + s

    if kind == "rows":
        rows, cols = a.shape[2], g.shape[1]
        tr, tc = rows, _tile(cols, 1024)
        a_spec = pl.BlockSpec((None, t, rows), lambda k, i, n, core_ref: (pick(k, core_ref), 0, 0))
        g_spec = pl.BlockSpec((t, tc), lambda k, i, n, core_ref: (0, n))
    elif kind == "cols":
        rows, cols = a.shape[1], g.shape[1] // N_DEV
        tr, tc = _tile(rows, 1024), cols
        a_spec = pl.BlockSpec((t, tr), lambda k, i, n, core_ref: (0, i))
        g_spec = pl.BlockSpec((t, cols), lambda k, i, n, core_ref: (0, pick(k, core_ref)))
    else:
        rows, cols = a.shape[1] // N_DEV, g.shape[1]
        tr, tc = rows, _tile(cols, 1024)
        a_spec = pl.BlockSpec((t, rows), lambda k, i, n, core_ref: (0, pick(k, core_ref)))
        g_spec = pl.BlockSpec((t, tc), lambda k, i, n, core_ref: (0, n))
    o_spec = pl.BlockSpec((None, tr, tc), lambda k, i, n, core_ref: (k, i, n))

    def body(core_ref, a_ref, g_ref, *rest):
        acc = _dot(a_ref[...], g_ref[...], "tn")
        if add is not None:
            acc = acc + rest[0][...].astype(F32)
        rest[-1][...] = acc.astype(BF16)

    return pl.pallas_call(
        body, name=name,
        grid_spec=pltpu.PrefetchScalarGridSpec(
            num_scalar_prefetch=1, grid=(half, rows // tr, cols // tc),
            in_specs=[a_spec, g_spec] + ([o_spec] if add is not None else []), out_specs=o_spec),
        out_shape=jax.ShapeDtypeStruct((half, rows, cols), BF16), compiler_params=_params(3),
    )(core, a, g, *(() if add is None else (add,)))


def mm_dx_cols(name, g, w):
    t = g.shape[0]
    nblk, d, nb = w.shape
    tm, tn = _tile(t, 1024), _tile(d, 1024)
    return _matmul(
        name, "nt", (g, w),
        [pl.BlockSpec((tm, nb), lambda i, j, k: (i, k)), pl.BlockSpec((None, tn, nb), lambda i, j, k: (k, j, 0))],
        jax.ShapeDtypeStruct((t, d), F32), pl.BlockSpec((tm, tn), lambda i, j, k: (i, j)),
        (t // tm, d // tn, nblk), (tm, tn))


def mm_dx_ffn(dgt, wg, dup, wu):
    nblk, t, fb = dgt.shape
    d = wg.shape[2]
    tm, tn = _tile(t, 1024), _tile(d, 1024)
    a_spec = pl.BlockSpec((None, tm, fb), lambda i, j, k: (k, i, 0))
    b_spec = pl.BlockSpec((None, fb, tn), lambda i, j, k: (k, 0, j))
    return _matmul(
        "mm_dx_ffn", "nn", (dgt, wg, dup, wu), [a_spec, b_spec, a_spec, b_spec],
        jax.ShapeDtypeStruct((t, d), F32), pl.BlockSpec((tm, tn), lambda i, j, k: (i, j)),
        (t // tm, d // tn, nblk), (tm, tn))


def ffn_up(u, wg, wu):
    t, d = u.shape
    nblk, fb, _ = wg.shape
    tm = _tile(t, 512)

    def body(u_ref, wg_ref, wu_ref, gt_ref, up_ref, f_ref):
        uu = u_ref[...]
        gt = _dot(uu, wg_ref[...], "nt")
        up = _dot(uu, wu_ref[...], "nt")
        gt_ref[...] = gt.astype(BF16)
        up_ref[...] = up.astype(BF16)
        f_ref[...] = (gt * _sigmoid(gt) * up).astype(BF16)

    w_spec = pl.BlockSpec((None, fb, d), lambda j, i: (j, 0, 0))
    o_spec = pl.BlockSpec((None, tm, fb), lambda j, i: (j, i, 0))
    return pl.pallas_call(
        body, name="ffn_up", grid=(nblk, t // tm),
        in_specs=[pl.BlockSpec((tm, d), lambda j, i: (i, 0)), w_spec, w_spec],
        out_specs=[o_spec, o_spec, o_spec],
        out_shape=[jax.ShapeDtypeStruct((nblk, t, fb), BF16)] * 3,
        compiler_params=_params(2),
    )(u, wg, wu)


def ffn_bwd(dd, wd, gt, up):
    t, d = dd.shape
    nblk, fb, _ = wd.shape
    tm = _tile(t, 512)

    def body(dd_ref, wd_ref, gt_ref, up_ref, dgt_ref, dup_ref):
        df = _dot(dd_ref[...], wd_ref[...], "nt")
        g = gt_ref[...].astype(F32)
        s = _sigmoid(g)
        dgt_ref[...] = (df * up_ref[...].astype(F32) * s * (1.0 + g * (1.0 - s))).astype(BF16)
        dup_ref[...] = (df * g * s).astype(BF16)

    s_spec = pl.BlockSpec((None, tm, fb), lambda j, i: (j, i, 0))
    return pl.pallas_call(
        body, name="ffn_bwd", grid=(nblk, t // tm),
        in_specs=[pl.BlockSpec((tm, d), lambda j, i: (i, 0)), pl.BlockSpec((None, fb, d), lambda j, i: (j, 0, 0)),
                  s_spec, s_spec],
        out_specs=[s_spec, s_spec],
        out_shape=[jax.ShapeDtypeStruct((nblk, t, fb), BF16)] * 2,
        compiler_params=_params(2),
    )(dd, wd, gt, up)


def rms_pre(h, g):
    t, d = h.shape
    tr = _tile(t, ROW_T)

    def body(h_ref, g_ref, o_ref):
        xn, _ = _rms_stats(h_ref[...])
        o_ref[...] = (xn * g_ref[...]).astype(BF16)

    return pl.pallas_call(
        body, name="rms_pre", grid=(t // tr,),
        in_specs=[_row_spec(tr, d), _vec_spec(d)], out_specs=_row_spec(tr, d),
        out_shape=jax.ShapeDtypeStruct((t, d), BF16), compiler_params=_params(1),
    )(h, g)


def res_norm(h, o, g_post, g_pre):
    t, d = h.shape
    tr = _tile(t, ROW_T)

    def body(h_ref, o_ref, gpo_ref, gpr_ref, h2_ref, u_ref):
        on, _ = _rms_stats(o_ref[...])
        h2 = h_ref[...] + on * gpo_ref[...]
        h2_ref[...] = h2
        hn, _ = _rms_stats(h2)
        u_ref[...] = (hn * gpr_ref[...]).astype(BF16)

    return pl.pallas_call(
        body, name="res_norm", grid=(t // tr,),
        in_specs=[_row_spec(tr, d), _row_spec(tr, d), _vec_spec(d), _vec_spec(d)],
        out_specs=[_row_spec(tr, d), _row_spec(tr, d)],
        out_shape=[jax.ShapeDtypeStruct((t, d), F32), jax.ShapeDtypeStruct((t, d), BF16)],
        compiler_params=_params(1),
    )(h, o, g_post, g_pre)


def final_loss(h2, dbr, g_post, target):
    t, d = h2.shape
    tr = _tile(t, ROW_T)

    def body(h2_ref, d_ref, g_ref, tg_ref, loss_ref, dy_ref, dd_ref, dg_ref):
        i = pl.program_id(0)

        @pl.when(i == 0)
        def _():
            loss_ref[...] = jnp.zeros_like(loss_ref)
            dg_ref[...] = jnp.zeros_like(dg_ref)

        x = d_ref[...]
        g = g_ref[...]
        xn, _ = _rms_stats(x)
        diff = h2_ref[...] + xn * g - tg_ref[...]
        loss_ref[...] += jnp.sum(jnp.sum(diff * diff, axis=1, keepdims=True), axis=0, keepdims=True)
        dy = diff * (1.0 / d)
        dy_ref[...] = dy
        dx, dg = _rms_bwd(dy, x, g)
        dd_ref[...] = dx.astype(BF16)
        dg_ref[...] += dg

    return pl.pallas_call(
        body, name="final_loss", grid=(t // tr,),
        in_specs=[_row_spec(tr, d), _row_spec(tr, d), _vec_spec(d), _row_spec(tr, d)],
        out_specs=[pl.BlockSpec((1, 1), lambda i: (0, 0)), _row_spec(tr, d), _row_spec(tr, d), _vec_spec(d)],
        out_shape=[jax.ShapeDtypeStruct((1, 1), F32), jax.ShapeDtypeStruct((t, d), F32),
                   jax.ShapeDtypeStruct((t, d), BF16), jax.ShapeDtypeStruct((1, d), F32)],
        compiler_params=_params(1),
    )(h2, dbr, g_post, target)


def norm_bwd(dh_out, du, h_in, g_pre, prev=None):
    t, d = h_in.shape
    tr = _tile(t, ROW_T)
    with_prev = prev is not None

    def body(*refs):
        if with_prev:
            dho_ref, du_ref, h_ref, gpr_ref, br_ref, gpo_ref, dh_ref, dgpr_ref, dbr_ref, dgpo_ref = refs
        else:
            dho_ref, du_ref, h_ref, gpr_ref, dh_ref, dgpr_ref = refs
        i = pl.program_id(0)

        @pl.when(i == 0)
        def _():
            dgpr_ref[...] = jnp.zeros_like(dgpr_ref)
            if with_prev:
                dgpo_ref[...] = jnp.zeros_like(dgpo_ref)

        dx, dg = _rms_bwd(du_ref[...], h_ref[...], gpr_ref[...])
        dh = dho_ref[...] + dx
        dh_ref[...] = dh
        dgpr_ref[...] += dg
        if with_prev:
            dbr, dg2 = _rms_bwd(dh, br_ref[...], gpo_ref[...])
            dbr_ref[...] = dbr.astype(BF16)
            dgpo_ref[...] += dg2

    row, vec = _row_spec(tr, d), _vec_spec(d)
    in_specs = [row, row, row, vec] + ([row, vec] if with_prev else [])
    out_specs = [row, vec] + ([row, vec] if with_prev else [])
    out_shape = [jax.ShapeDtypeStruct((t, d), F32), jax.ShapeDtypeStruct((1, d), F32)]
    if with_prev:
        out_shape += [jax.ShapeDtypeStruct((t, d), BF16), jax.ShapeDtypeStruct((1, d), F32)]
    args = (dh_out, du, h_in, g_pre) + (tuple(prev) if with_prev else ())
    return pl.pallas_call(
        body, name="norm_bwd_chain" if with_prev else "norm_bwd_first", grid=(t // tr,),
        in_specs=in_specs, out_specs=out_specs, out_shape=out_shape, compiler_params=_params(1),
    )(*args)


def mix_fwd(ya, yc, yl, ga, gc, gl):
    t, wa = ya.shape
    wc, wl = yc.shape[1], yl.shape[1]
    tr = _tile(t, ROW_T)

    def body(ya_ref, yc_ref, yl_ref, ga_ref, gc_ref, gl_ref, o_ref):
        o_ref[:, pl.ds(0, wa)] = (_rms_stats(ya_ref[...])[0] * ga_ref[...]).astype(BF16)
        o_ref[:, pl.ds(wa, wc)] = (_rms_stats(yc_ref[...])[0] * gc_ref[...]).astype(BF16)
        o_ref[:, pl.ds(wa + wc, wl)] = (_rms_stats(yl_ref[...])[0] * gl_ref[...]).astype(BF16)

    return pl.pallas_call(
        body, name="mix_fwd", grid=(t // tr,),
        in_specs=[_row_spec(tr, wa), _row_spec(tr, wc), _row_spec(tr, wl), _vec_spec(wa), _vec_spec(wc), _vec_spec(wl)],
        out_specs=_row_spec(tr, wa + wc + wl),
        out_shape=jax.ShapeDtypeStruct((t, wa + wc + wl), BF16), compiler_params=_params(1),
    )(ya, yc, yl, ga, gc, gl)


def mix_bwd(dmixed, ya, yc, yl, cpre, ga, gc, gl, lng, lnb):
    t, wa = ya.shape
    wc, wl = yc.shape[1], yl.shape[1]
    tr = _tile(t, ROW_T)

    def body(dm_ref, ya_ref, yc_ref, yl_ref, c_ref, ga_ref, gc_ref, gl_ref, lg_ref, lb_ref,
             dya_ref, dc_ref, dyl_ref, dga_ref, dgc_ref, dgl_ref, dlg_ref, dlb_ref):
        i = pl.program_id(0)

        @pl.when(i == 0)
        def _():
            for r in (dga_ref, dgc_ref, dgl_ref, dlg_ref, dlb_ref):
                r[...] = jnp.zeros_like(r)

        dya, dga = _rms_bwd(dm_ref[:, pl.ds(0, wa)], ya_ref[...], ga_ref[...])
        dya_ref[...] = dya
        dga_ref[...] += dga
        dyl, dgl = _rms_bwd(dm_ref[:, pl.ds(wa + wc, wl)], yl_ref[...], gl_ref[...])
        dyl_ref[...] = dyl
        dgl_ref[...] += dgl
        dyc, dgc = _rms_bwd(dm_ref[:, pl.ds(wa, wc)], yc_ref[...], gc_ref[...])
        dgc_ref[...] += dgc
        c = c_ref[...]
        xc = c - jnp.mean(c, axis=-1, keepdims=True)
        rstd = lax.rsqrt(jnp.mean(xc * xc, axis=-1, keepdims=True) + EPS)
        xhat = xc * rstd
        ln = xhat * lg_ref[...] + lb_ref[...]
        s = _sigmoid(ln)
        dln = dyc * s * (1.0 + ln * (1.0 - s))
        dlg_ref[...] += jnp.sum(dln * xhat, axis=0, keepdims=True)
        dlb_ref[...] += jnp.sum(dln, axis=0, keepdims=True)
        dxh = dln * lg_ref[...]
        dc_ref[...] = rstd * (dxh - jnp.mean(dxh, axis=-1, keepdims=True)
                              - xhat * jnp.mean(dxh * xhat, axis=-1, keepdims=True))

    return pl.pallas_call(
        body, name="mix_bwd", grid=(t // tr,),
        in_specs=[_row_spec(tr, wa + wc + wl), _row_spec(tr, wa), _row_spec(tr, wc), _row_spec(tr, wl), _row_spec(tr, wc),
                  _vec_spec(wa), _vec_spec(wc), _vec_spec(wl), _vec_spec(wc), _vec_spec(wc)],
        out_specs=[_row_spec(tr, wa), _row_spec(tr, wc), _row_spec(tr, wl),
                   _vec_spec(wa), _vec_spec(wc), _vec_spec(wl), _vec_spec(wc), _vec_spec(wc)],
        out_shape=[jax.ShapeDtypeStruct((t, wa), F32), jax.ShapeDtypeStruct((t, wc), F32), jax.ShapeDtypeStruct((t, wl), F32),
                   jax.ShapeDtypeStruct((1, wa), F32), jax.ShapeDtypeStruct((1, wc), F32), jax.ShapeDtypeStruct((1, wl), F32),
                   jax.ShapeDtypeStruct((1, wc), F32), jax.ShapeDtypeStruct((1, wc), F32)],
        compiler_params=_params(1),
    )(dmixed, ya, yc, yl, cpre, ga, gc, gl, lng, lnb)


def _hi_lo(x):
    hi = x.astype(BF16)
    return hi, (x - hi.astype(F32)).astype(BF16)


def _lane_sums(x, tri, reverse):
    nsub = x.shape[1] // tri.shape[0]
    order = range(nsub - 1, -1, -1) if reverse else range(nsub)
    parts, beyond = {}, None
    for b in order:
        blk = x[:, b * tri.shape[0]:(b + 1) * tri.shape[0]]
        hi, lo = _hi_lo(blk)
        c = _dot(hi, tri) + _dot(lo, tri)
        parts[b] = c if beyond is None else c + beyond
        tot = jnp.sum(blk, axis=1, keepdims=True)
        beyond = tot if beyond is None else beyond + tot
    return jnp.concatenate([parts[b] for b in range(nsub)], axis=1), beyond


def _att_strip(qb, kt, thresh, scale, diff, tri_gt):
    z = _dot(qb, kt, "nt") * scale
    sp = _softplus(z)
    mask = diff < thresh
    later, total = _lane_sums(jnp.where(mask, -sp, 0.0), tri_gt, True)
    return z, sp, mask, total, (z - sp) + later


def _att_consts(tq, tk):
    diff = lax.broadcasted_iota(jnp.int32, (tq, tk), 1) - lax.broadcasted_iota(jnp.int32, (tq, tk), 0)
    cb = min(tk, ATT_SUM)
    row = lax.broadcasted_iota(jnp.int32, (cb, cb), 0)
    col = lax.broadcasted_iota(jnp.int32, (cb, cb), 1)
    return diff, (row > col).astype(BF16), (row < col).astype(BF16)


def attn_fwd(proj, n_heads):
    t = proj.shape[0]
    tq, tk = _tile(t, ATT_TQ), _tile(t, ATT_TK)
    hp = ATT_HEADS
    wd = hp * HEAD_DIM
    scale = HEAD_DIM ** -0.5

    def body(q_ref, k_ref, v_ref, o_ref, kb_ref, vb_ref, acc_ref):
        kb_ref[...] = k_ref[...].astype(BF16)
        vb_ref[...] = v_ref[...].astype(BF16)
        diff, tri_gt, _ = _att_consts(tq, tk)

        def qblock(i, _):
            q0 = pl.multiple_of(i * tq, tq)
            heads = [pl.ds(h * HEAD_DIM, HEAD_DIM) for h in range(hp)]
            qbs = [q_ref[pl.ds(q0, tq), hs].astype(BF16) for hs in heads]
            acc_ref[...] = jnp.zeros_like(acc_ref)
            n_strips = (q0 + tq + tk - 1) // tk

            def strip(jj, runs):
                k0 = pl.multiple_of((n_strips - 1 - jj) * tk, tk)
                out = []
                for h, hs in enumerate(heads):
                    _, _, mask, total, logw = _att_strip(qbs[h], kb_ref[pl.ds(k0, tk), hs], q0 - k0, scale, diff, tri_gt)
                    w = jnp.where(mask, jnp.exp(logw + runs[h]), 0.0)
                    acc_ref[:, hs] += _dot(w.astype(BF16), vb_ref[pl.ds(k0, tk), hs])
                    out.append(runs[h] + total)
                return tuple(out)

            lax.fori_loop(0, n_strips, strip, tuple(jnp.zeros((tq, 1), F32) for _ in range(hp)))
            o_ref[pl.ds(q0, tq), :] = acc_ref[...]
            return 0

        lax.fori_loop(0, t // tq, qblock, 0)

    def col_spec(base):
        return pl.BlockSpec((t, wd), lambda h, base=base: (0, base + h))

    ng = n_heads // hp
    return pl.pallas_call(
        body, name="attn_fwd", grid=(ng,),
        in_specs=[col_spec(0), col_spec(ng), col_spec(2 * ng)],
        out_specs=col_spec(0),
        out_shape=jax.ShapeDtypeStruct((t, n_heads * HEAD_DIM), F32),
        scratch_shapes=[pltpu.VMEM((t, wd), BF16), pltpu.VMEM((t, wd), BF16), pltpu.VMEM((tq, wd), F32)],
        compiler_params=_params(1),
    )(proj, proj, proj)


def attn_bwd(proj, dy, n_heads):
    t = proj.shape[0]
    tq, tk = _tile(t, ATT_TQ), _tile(t, ATT_TK)
    hp = ATT_HEADS
    wd = hp * HEAD_DIM
    scale = HEAD_DIM ** -0.5

    def body(q_ref, k_ref, v_ref, dy_ref, dq_ref, dk_ref, dv_ref, qb_ref, kb_ref, vb_ref, dob_ref, dk_acc, dv_acc,
             dq_acc, run_s):
        qb_ref[...] = q_ref[...].astype(BF16)
        kb_ref[...] = k_ref[...].astype(BF16)
        vb_ref[...] = v_ref[...].astype(BF16)
        dob_ref[...] = dy_ref[...].astype(BF16)
        dk_acc[...] = jnp.zeros_like(dk_acc)
        dv_acc[...] = jnp.zeros_like(dv_acc)
        diff, tri_gt, tri_lt = _att_consts(tq, tk)

        def qblock(i, _):
            q0 = pl.multiple_of(i * tq, tq)
            heads = [pl.ds(h * HEAD_DIM, HEAD_DIM) for h in range(hp)]
            qbs = [qb_ref[pl.ds(q0, tq), hs] for hs in heads]
            dobs = [dob_ref[pl.ds(q0, tq), hs] for hs in heads]
            dq_acc[...] = jnp.zeros_like(dq_acc)
            n_strips = (q0 + tq + tk - 1) // tk

            def sweep(jj, runs):
                si = n_strips - 1 - jj
                k0 = pl.multiple_of(si * tk, tk)
                out = []
                for h, hs in enumerate(heads):
                    sp = _softplus(_dot(qbs[h], kb_ref[pl.ds(k0, tk), hs], "nt") * scale)
                    run_s[h, si] = runs[h]
                    out.append(runs[h] + jnp.sum(jnp.where(diff < q0 - k0, -sp, 0.0), axis=1, keepdims=True))
                return tuple(out)

            zero = tuple(jnp.zeros((tq, 1), F32) for _ in range(hp))
            lax.fori_loop(0, n_strips, sweep, zero)

            def strip(si, gsums):
                k0 = pl.multiple_of(si * tk, tk)
                out = []
                for h, hs in enumerate(heads):
                    kt = kb_ref[pl.ds(k0, tk), hs]
                    vt = vb_ref[pl.ds(k0, tk), hs]
                    z, sp, mask, _, logw = _att_strip(qbs[h], kt, q0 - k0, scale, diff, tri_gt)
                    w = jnp.where(mask, jnp.exp(logw + run_s[h, si]), 0.0)
                    g = w * _dot(dobs[h], vt, "nt")
                    before, gtot = _lane_sums(g, tri_lt, False)
                    sig = jnp.exp(z - sp)
                    dz = jnp.where(mask, g * (1.0 - sig) - (before + gsums[h]) * sig, 0.0) * scale
                    dzb = dz.astype(BF16)
                    dk_acc[pl.ds(k0, tk), hs] += _dot(dzb, qbs[h], "tn")
                    dv_acc[pl.ds(k0, tk), hs] += _dot(w.astype(BF16), dobs[h], "tn")
                    dq_acc[:, hs] += _dot(dzb, kt)
                    out.append(gsums[h] + gtot)
                return tuple(out)

            lax.fori_loop(0, n_strips, strip, zero)
            dq_ref[pl.ds(q0, tq), :] = dq_acc[...].astype(BF16)
            return 0

        lax.fori_loop(0, t // tq, qblock, 0)
        dk_ref[...] = dk_acc[...].astype(BF16)
        dv_ref[...] = dv_acc[...].astype(BF16)

    def col_spec(base):
        return pl.BlockSpec((t, wd), lambda h, base=base: (0, base + h))

    ng = n_heads // hp
    return pl.pallas_call(
        body, name="attn_bwd", grid=(ng,),
        in_specs=[col_spec(0), col_spec(ng), col_spec(2 * ng), col_spec(0)],
        out_specs=[col_spec(0), col_spec(0), col_spec(0)],
        out_shape=[jax.ShapeDtypeStruct((t, n_heads * HEAD_DIM), BF16)] * 3,
        scratch_shapes=[pltpu.VMEM((t, wd), BF16)] * 4 + [pltpu.VMEM((t, wd), F32)] * 2
        + [pltpu.VMEM((tq, wd), F32), pltpu.VMEM((hp, t // tk, tq, 1), F32)],
        compiler_params=_params(1),
    )(proj, proj, proj, dy)


def _glu_halo(vc, gc, vp, gp, ubuf, i, tt, halo):
    uprev = vp[pl.ds(tt - halo, halo), :] * _sigmoid(gp[pl.ds(tt - halo, halo), :])
    ubuf[pl.ds(0, halo), :] = jnp.where(i > 0, uprev, 0.0)
    ubuf[pl.ds(halo, tt), :] = vc[...] * _sigmoid(gc[...])


def conv_fwd(proj, col0, cc, w, b, lng, lnb):
    t = proj.shape[0]
    tt = _tile(t, ROW_T)
    vi, gi = col0 // cc, col0 // cc + 1
    off = CONV_HALO - (DW_LEN - 1)

    def body(vc, gc, vp, gp, w_ref, b_ref, lg_ref, lb_ref, c_ref, y_ref, ubuf):
        i = pl.program_id(0)
        _glu_halo(vc, gc, vp, gp, ubuf, i, tt, CONV_HALO)
        for ch in range(cc // LANE):
            sl = pl.ds(ch * LANE, LANE)
            acc = jnp.zeros((tt, LANE), F32) + b_ref[:, sl]
            for tap in range(DW_LEN):
                acc = acc + w_ref[pl.ds(tap, 1), sl] * ubuf[pl.ds(off + tap, tt), sl]
            c_ref[:, sl] = acc
        c = c_ref[...]
        xc = c - jnp.mean(c, axis=-1, keepdims=True)
        ln = xc * lax.rsqrt(jnp.mean(xc * xc, axis=-1, keepdims=True) + EPS) * lg_ref[...] + lb_ref[...]
        y_ref[...] = ln * _sigmoid(ln)

    cur = lambda c: pl.BlockSpec((tt, cc), lambda i, c=c: (i, c))
    prev = lambda c: pl.BlockSpec((tt, cc), lambda i, c=c: (jnp.maximum(i - 1, 0), c))
    return pl.pallas_call(
        body, name="conv_fwd", grid=(t // tt,),
        in_specs=[cur(vi), cur(gi), prev(vi), prev(gi), pl.BlockSpec((DW_LEN, cc), lambda i: (0, 0)),
                  _vec_spec(cc), _vec_spec(cc), _vec_spec(cc)],
        out_specs=[_row_spec(tt, cc), _row_spec(tt, cc)],
        out_shape=[jax.ShapeDtypeStruct((t, cc), F32)] * 2,
        scratch_shapes=[pltpu.VMEM((CONV_HALO + tt, cc), F32)],
        compiler_params=_params(1),
    )(proj, proj, proj, proj, w, b, lng, lnb)


def conv_bwd(proj, col0, cc, dc, w):
    t = proj.shape[0]
    tt = _tile(t, ROW_T)
    nt = t // tt
    vi, gi = col0 // cc, col0 // cc + 1
    off = CONV_HALO - (DW_LEN - 1)

    def body(vc, gc, vp, gp, dcc, dcn, w_ref, dvg_ref, dw_ref, db_ref, ubuf, dbuf):
        i = pl.program_id(0)

        @pl.when(i == 0)
        def _():
            dw_ref[...] = jnp.zeros_like(dw_ref)
            db_ref[...] = jnp.zeros_like(db_ref)

        _glu_halo(vc, gc, vp, gp, ubuf, i, tt, CONV_HALO)
        dbuf[pl.ds(0, tt), :] = dcc[...]
        dbuf[pl.ds(tt, CONV_HALO), :] = jnp.where(i < nt - 1, dcn[pl.ds(0, CONV_HALO), :], 0.0)
        db_ref[...] += jnp.sum(dcc[...], axis=0, keepdims=True)
        for ch in range(cc // LANE):
            sl = pl.ds(ch * LANE, LANE)
            dcv = dbuf[pl.ds(0, tt), sl]
            du = jnp.zeros((tt, LANE), F32)
            for tap in range(DW_LEN):
                du = du + w_ref[pl.ds(tap, 1), sl] * dbuf[pl.ds(DW_LEN - 1 - tap, tt), sl]
                dw_ref[pl.ds(tap, 1), sl] += jnp.sum(dcv * ubuf[pl.ds(off + tap, tt), sl], axis=0, keepdims=True)
            s = _sigmoid(gc[:, sl])
            val = vc[:, sl]
            dvg_ref[:, sl] = (du * s).astype(BF16)
            dvg_ref[:, pl.ds(cc + ch * LANE, LANE)] = (du * val * s * (1.0 - s)).astype(BF16)

    cur = lambda c: pl.BlockSpec((tt, cc), lambda i, c=c: (i, c))
    prev = lambda c: pl.BlockSpec((tt, cc), lambda i, c=c: (jnp.maximum(i - 1, 0), c))
    return pl.pallas_call(
        body, name="conv_bwd", grid=(nt,),
        in_specs=[cur(vi), cur(gi), prev(vi), prev(gi), _row_spec(tt, cc),
                  pl.BlockSpec((tt, cc), lambda i: (jnp.minimum(i + 1, nt - 1), 0)),
                  pl.BlockSpec((DW_LEN, cc), lambda i: (0, 0))],
        out_specs=[_row_spec(tt, 2 * cc), pl.BlockSpec((DW_LEN, cc), lambda i: (0, 0)), _vec_spec(cc)],
        out_shape=[jax.ShapeDtypeStruct((t, 2 * cc), BF16), jax.ShapeDtypeStruct((DW_LEN, cc), F32),
                   jax.ShapeDtypeStruct((1, cc), F32)],
        scratch_shapes=[pltpu.VMEM((CONV_HALO + tt, cc), F32), pltpu.VMEM((tt + CONV_HALO, cc), F32)],
        compiler_params=_params(1),
    )(proj, proj, proj, proj, dc, dc, w)


def _lru_gates(xbuf, cw_ref, cb_ref, wa_ref, ba_ref, wi_ref, bi_ref, lam_ref, tt, wl):
    bd = wl // LRU_BLOCKS
    xr = jnp.zeros((tt, wl), F32) + cb_ref[...]
    for tap in range(LRU_LEN):
        xr = xr + cw_ref[pl.ds(tap, 1), :] * xbuf[pl.ds(LRU_HALO - (LRU_LEN - 1) + tap, tt), :]
    xb = xr.astype(BF16)
    ga = jnp.concatenate([_dot(xb[:, n * bd:(n + 1) * bd], wa_ref[n]) for n in range(LRU_BLOCKS)], axis=1) + ba_ref[...]
    gi = jnp.concatenate([_dot(xb[:, n * bd:(n + 1) * bd], wi_ref[n]) for n in range(LRU_BLOCKS)], axis=1) + bi_ref[...]
    r = _sigmoid(ga)
    ig = _sigmoid(gi)
    spl = _softplus(-lam_ref[...])
    log_a = -LRU_C * r * spl
    a = jnp.exp(log_a)
    m = jnp.sqrt(_neg_expm1(2.0 * log_a))
    return xr, xb, r, ig, spl, a, m


def _group_scan(a8, b8, reverse):
    rid = lax.broadcasted_iota(jnp.int32, a8.shape, 0)
    aa, bb = a8, b8
    for dist in (1, 2, 4):
        shift = SUBLANE - dist if reverse else dist
        a_sh = pltpu.roll(aa, shift, 0)
        b_sh = pltpu.roll(bb, shift, 0)
        valid = (rid < SUBLANE - dist) if reverse else (rid >= dist)
        bb = jnp.where(valid, aa * b_sh + bb, bb)
        aa = jnp.where(valid, aa * a_sh, aa)
    return aa, bb


def _pick_row(x8, r):
    rid = lax.broadcasted_iota(jnp.int32, x8.shape, 0)
    return jnp.sum(jnp.where(rid == r, x8, 0.0), axis=0, keepdims=True)


def lru_fwd(proj, col0, wl, cw, cb, wa, ba, wi, bi, lam):
    t = proj.shape[0]
    tt = _tile(t, ROW_T)
    xi, yi = col0 // wl, col0 // wl + 1

    def body(xc, xp, ry, cw_ref, cb_ref, wa_ref, ba_ref, wi_ref, bi_ref, lam_ref, hs_ref, y_ref,
             xbuf, a_s, b_s, hcar):
        i = pl.program_id(0)

        @pl.when(i == 0)
        def _():
            hcar[...] = jnp.zeros_like(hcar)

        xbuf[pl.ds(0, LRU_HALO), :] = jnp.where(i > 0, xp[pl.ds(tt - LRU_HALO, LRU_HALO), :], 0.0)
        xbuf[pl.ds(LRU_HALO, tt), :] = xc[...]
        xr, _, _, ig, _, a, m = _lru_gates(xbuf, cw_ref, cb_ref, wa_ref, ba_ref, wi_ref, bi_ref, lam_ref, tt, wl)
        a_s[...] = a
        b_s[...] = m * ig * xr

        def group(gidx, h):
            r0 = pl.multiple_of(gidx * SUBLANE, SUBLANE)
            aa, bb = _group_scan(a_s[pl.ds(r0, SUBLANE), :], b_s[pl.ds(r0, SUBLANE), :], False)
            h8 = aa * h + bb
            hs_ref[pl.ds(r0, SUBLANE), :] = h8
            return _pick_row(h8, SUBLANE - 1)

        hcar[...] = lax.fori_loop(0, tt // SUBLANE, group, hcar[...])
        gel, _ = _gelu_and_grad(ry[...])
        y_ref[...] = hs_ref[...] * gel

    cur = lambda c: pl.BlockSpec((tt, wl), lambda i, c=c: (i, c))
    full = lambda shape: pl.BlockSpec(shape, lambda i: (0,) * len(shape))
    return pl.pallas_call(
        body, name="lru_fwd", grid=(t // tt,),
        in_specs=[cur(xi), pl.BlockSpec((tt, wl), lambda i: (jnp.maximum(i - 1, 0), xi)), cur(yi),
                  full((LRU_LEN, wl)), _vec_spec(wl), full(wa.shape), _vec_spec(wl), full(wi.shape), _vec_spec(wl),
                  _vec_spec(wl)],
        out_specs=[_row_spec(tt, wl), _row_spec(tt, wl)],
        out_shape=[jax.ShapeDtypeStruct((t, wl), F32)] * 2,
        scratch_shapes=[pltpu.VMEM((LRU_HALO + tt, wl), F32), pltpu.VMEM((tt, wl), F32), pltpu.VMEM((tt, wl), F32),
                        pltpu.VMEM((1, wl), F32)],
        compiler_params=_params(1),
    )(proj, proj, proj, cw, cb, wa, ba, wi, bi, lam)


def lru_bwd(proj, col0, wl, hs, dy, cw, cb, wa, ba, wi, bi, lam):
    t = proj.shape[0]
    tt = _tile(t, ROW_T)
    nt = t // tt
    xi, yi = col0 // wl, col0 // wl + 1
    bd = wl // LRU_BLOCKS

    def body(xc, xp, ry, hc, hp, dy_ref, cw_ref, cb_ref, wa_ref, ba_ref, wi_ref, bi_ref, lam_ref,
             dxy_ref, dcw_ref, dcb_ref, dwa_ref, dba_ref, dwi_ref, dbi_ref, dlam_ref,
             xbuf, hbuf, abuf, e_s, dh_s, dxbuf, dhcar):
        i = pl.program_id(0)
        first = i == 0

        @pl.when(first)
        def _():
            for r in (dcw_ref, dcb_ref, dwa_ref, dba_ref, dwi_ref, dbi_ref, dlam_ref, dhcar):
                r[...] = jnp.zeros_like(r)
            abuf[pl.ds(tt, LRU_HALO), :] = jnp.zeros((LRU_HALO, wl), F32)
            dxbuf[pl.ds(tt, LRU_HALO), :] = jnp.zeros((LRU_HALO, wl), F32)

        has_prev = i < nt - 1
        xbuf[pl.ds(0, LRU_HALO), :] = jnp.where(has_prev, xp[pl.ds(tt - LRU_HALO, LRU_HALO), :], 0.0)
        xbuf[pl.ds(LRU_HALO, tt), :] = xc[...]
        hbuf[pl.ds(0, LRU_HALO), :] = jnp.where(has_prev, hp[pl.ds(tt - LRU_HALO, LRU_HALO), :], 0.0)
        hbuf[pl.ds(LRU_HALO, tt), :] = hc[...]
        xr, xb, r, ig, spl, a, m = _lru_gates(xbuf, cw_ref, cb_ref, wa_ref, ba_ref, wi_ref, bi_ref, lam_ref, tt, wl)
        gel, dgel = _gelu_and_grad(ry[...])
        dyv = dy_ref[...]
        e_s[...] = dyv * gel
        dxy_ref[:, pl.ds(wl, wl)] = (dyv * hc[...] * dgel).astype(BF16)
        abuf[pl.ds(0, tt), :] = a
        a_next = abuf[pl.ds(1, tt), :]
        dh_s[...] = a_next

        def group(it, dh_in):
            r0 = pl.multiple_of((tt // SUBLANE - 1 - it) * SUBLANE, SUBLANE)
            aa, bb = _group_scan(dh_s[pl.ds(r0, SUBLANE), :], e_s[pl.ds(r0, SUBLANE), :], True)
            dh8 = aa * dh_in + bb
            dh_s[pl.ds(r0, SUBLANE), :] = dh8
            return _pick_row(dh8, 0)

        dhcar[...] = lax.fori_loop(0, tt // SUBLANE, group, dhcar[...])
        abuf[pl.ds(tt, LRU_HALO), :] = a[0:LRU_HALO, :]
        dh = dh_s[...]
        h_m1 = hbuf[pl.ds(LRU_HALO - 1, tt), :]
        dlog_a = dh * h_m1 * a - dh * ig * xr * (a * a / m)
        dig = dh * m * xr
        dxr = dh * m * ig
        dga = dlog_a * (-LRU_C) * spl * r * (1.0 - r)
        dgi = dig * ig * (1.0 - ig)
        dlam_ref[...] += jnp.sum(dlog_a * r, axis=0, keepdims=True) * (LRU_C * _sigmoid(-lam_ref[...]))
        dba_ref[...] += jnp.sum(dga, axis=0, keepdims=True)
        dbi_ref[...] += jnp.sum(dgi, axis=0, keepdims=True)
        dgab = dga.astype(BF16)
        dgib = dgi.astype(BF16)
        back = []
        for n in range(LRU_BLOCKS):
            sl = slice(n * bd, (n + 1) * bd)
            dwa_ref[n] += _dot(xb[:, sl], dgab[:, sl], "tn")
            dwi_ref[n] += _dot(xb[:, sl], dgib[:, sl], "tn")
            back.append(_dot(dgab[:, sl], wa_ref[n], "nt") + _dot(dgib[:, sl], wi_ref[n], "nt"))
        dxr = dxr + jnp.concatenate(back, axis=1)
        dcb_ref[...] += jnp.sum(dxr, axis=0, keepdims=True)
        dxbuf[pl.ds(0, tt), :] = dxr
        drx = jnp.zeros((tt, wl), F32)
        for tap in range(LRU_LEN):
            drx = drx + cw_ref[pl.ds(tap, 1), :] * dxbuf[pl.ds(LRU_LEN - 1 - tap, tt), :]
            dcw_ref[pl.ds(tap, 1), :] += jnp.sum(
                dxr * xbuf[pl.ds(LRU_HALO - (LRU_LEN - 1) + tap, tt), :], axis=0, keepdims=True)
        dxbuf[pl.ds(tt, LRU_HALO), :] = dxr[0:LRU_HALO, :]
        dxy_ref[:, pl.ds(0, wl)] = drx.astype(BF16)

    rev = lambda c: pl.BlockSpec((tt, wl), lambda i, c=c: (nt - 1 - i, c))
    rev_prev = lambda c: pl.BlockSpec((tt, wl), lambda i, c=c: (jnp.maximum(nt - 2 - i, 0), c))
    full = lambda shape: pl.BlockSpec(shape, lambda i: (0,) * len(shape))
    vec = _vec_spec(wl)
    return pl.pallas_call(
        body, name="lru_bwd", grid=(nt,),
        in_specs=[rev(xi), rev_prev(xi), rev(yi), rev(0), rev_prev(0), rev(0),
                  full((LRU_LEN, wl)), vec, full(wa.shape), vec, full(wi.shape), vec, vec],
        out_specs=[pl.BlockSpec((tt, 2 * wl), lambda i: (nt - 1 - i, 0)), full((LRU_LEN, wl)), vec,
                   full(wa.shape), vec, full(wi.shape), vec, vec],
        out_shape=[jax.ShapeDtypeStruct((t, 2 * wl), BF16), jax.ShapeDtypeStruct((LRU_LEN, wl), F32),
                   jax.ShapeDtypeStruct((1, wl), F32), jax.ShapeDtypeStruct(wa.shape, F32),
                   jax.ShapeDtypeStruct((1, wl), F32), jax.ShapeDtypeStruct(wi.shape, F32),
                   jax.ShapeDtypeStruct((1, wl), F32), jax.ShapeDtypeStruct((1, wl), F32)],
        scratch_shapes=[pltpu.VMEM((LRU_HALO + tt, wl), F32), pltpu.VMEM((LRU_HALO + tt, wl), F32),
                        pltpu.VMEM((tt + LRU_HALO, wl), F32), pltpu.VMEM((tt, wl), F32), pltpu.VMEM((tt, wl), F32),
                        pltpu.VMEM((tt + LRU_HALO, wl), F32), pltpu.VMEM((1, wl), F32)],
        compiler_params=_params(1),
    )(proj, proj, proj, hs, hs, dy, cw, cb, wa, ba, wi, bi, lam)


def _adamw(w, g, m, v):
    m = ADAM_B1 * m + (1.0 - ADAM_B1) * g
    v = ADAM_B2 * v + (1.0 - ADAM_B2) * (g * g)
    m_hat = m / (1.0 - ADAM_B1 ** ADAM_STEP)
    v_hat = v / (1.0 - ADAM_B2 ** ADAM_STEP)
    delta = -ADAM_LR * (m_hat / (jnp.sqrt(v_hat) + ADAM_EPS) + ADAM_WD * w)
    return delta, m, v


def adam_big(name, w, m, v, parts, chip):
    n_layers, rows, cols = w.shape
    tr = _tile(rows, 128 if cols > 1024 else 256)
    nrt = rows // tr

    def body(chip_ref, *refs):
        w_ref, m_ref, v_ref = refs[:3]
        part_refs = refs[3:3 + 4 * n_layers]
        g_ref, d_ref, mo_ref, vo_ref = refs[3 + 4 * n_layers:]
        layer = pl.program_id(0)
        for l in range(n_layers):
            @pl.when(layer == l)
            def _(l=l):
                g = part_refs[4 * l][...].astype(F32)
                for p in range(1, 4):
                    g = g + part_refs[4 * l + p][...].astype(F32)
                delta, mn, vn = _adamw(w_ref[...], g, m_ref[...], v_ref[...])
                g_ref[...] = g
                d_ref[...] = delta
                mo_ref[...] = mn
                vo_ref[...] = vn

    wspec = pl.BlockSpec((None, tr, cols), lambda l, i, chip_ref: (l, i, 0))
    operands, in_specs = [w, m, v], [wspec, wspec, wspec]
    for l in range(n_layers):
        mine, recv = parts[l]
        operands.append(mine)
        in_specs.append(pl.BlockSpec(
            (None, tr, cols), lambda ll, i, chip_ref, l=l: (chip_ref[0], jnp.where(ll == l, i, 0), 0)))
        for p in range(3):
            operands.append(recv)
            in_specs.append(pl.BlockSpec(
                (None, tr, cols), lambda ll, i, chip_ref, l=l, p=p: (p, jnp.where(ll == l, i, 0), 0)))
    return pl.pallas_call(
        body, name=name,
        grid_spec=pltpu.PrefetchScalarGridSpec(
            num_scalar_prefetch=1, grid=(n_layers, nrt), in_specs=in_specs, out_specs=[wspec] * 4),
        out_shape=[jax.ShapeDtypeStruct(w.shape, F32)] * 4, compiler_params=_params(2),
    )(chip, *operands)


def adam_small(w, m, v, g):
    rows = w.shape[0]
    tr = _tile(rows, PACK_ROWS)

    def body(w_ref, m_ref, v_ref, g_ref, d_ref, mo_ref, vo_ref):
        delta, mn, vn = _adamw(w_ref[...], g_ref[...], m_ref[...], v_ref[...])
        d_ref[...] = delta
        mo_ref[...] = mn
        vo_ref[...] = vn

    spec = _row_spec(tr, LANE)
    return pl.pallas_call(
        body, name="adam_small", grid=(rows // tr,), in_specs=[spec] * 4, out_specs=[spec] * 3,
        out_shape=[jax.ShapeDtypeStruct(w.shape, F32)] * 3, compiler_params=_params(1),
    )(w, m, v, g)


def sum_parts(parts):
    _, rows, _ = parts.shape
    tr = _tile(rows, PACK_ROWS)

    def body(p_ref, o_ref):
        acc = p_ref[0]
        for k in range(1, N_DEV):
            acc = acc + p_ref[k]
        o_ref[...] = acc

    return pl.pallas_call(
        body, name="sum_parts", grid=(rows // tr,),
        in_specs=[pl.BlockSpec((N_DEV, tr, LANE), lambda i: (0, i, 0))], out_specs=_row_spec(tr, LANE),
        out_shape=jax.ShapeDtypeStruct((rows, LANE), F32), compiler_params=_params(1),
    )(parts)


def place_own(x, me, dtype):
    rows, cols = x.shape
    tr = _tile(rows, 256)

    def body(me_ref, x_ref, o_ref):
        o_ref[...] = x_ref[...].astype(dtype)

    return pl.pallas_call(
        body, name="place_own",
        grid_spec=pltpu.PrefetchScalarGridSpec(
            num_scalar_prefetch=1, grid=(rows // tr,),
            in_specs=[pl.BlockSpec((tr, cols), lambda i, me_ref: (i, 0))],
            out_specs=pl.BlockSpec((None, tr, cols), lambda i, me_ref: (me_ref[0], i, 0))),
        out_shape=jax.ShapeDtypeStruct((N_DEV, rows, cols), dtype), compiler_params=_params(1),
    )(me, x)


_HBM = pl.BlockSpec(memory_space=pltpu.HBM)


def _place():
    return lax.axis_index("x"), lax.axis_index("y"), lax.axis_index("c")


def _other_chips(x, y):
    return [(1 - x, y), (x, 1 - y), (1 - x, 1 - y)]


def all_gather(name, shard, me, dtype=None):
    def body(buf_ref, out_ref, send_sems, recv_sems):
        del buf_ref
        x, y, c = _place()
        mine, sibling = (x, y, c), (x, y, 1 - c)
        chips = _other_chips(x, y)

        def copy(k, block, to):
            slot = out_ref.at[4 * block[0] + 2 * block[1] + block[2]]
            return pltpu.make_async_remote_copy(
                src_ref=slot, dst_ref=slot, send_sem=send_sems.at[k], recv_sem=recv_sems.at[k],
                device_id=to, device_id_type=pl.DeviceIdType.MESH)

        first = [copy(0, mine, sibling)] + [copy(1 + j, mine, (*chip, c)) for j, chip in enumerate(chips)]
        for cp in first:
            cp.start()
        passed = [copy(4 + j, (*chip, c), sibling) for j, chip in enumerate(chips)]
        for j, chip in enumerate(chips):
            copy(1 + j, (*chip, c), mine).wait_recv()
            passed[j].start()
        copy(0, sibling, mine).wait_recv()
        for j, chip in enumerate(chips):
            copy(4 + j, (*chip, 1 - c), mine).wait_recv()
        for cp in first + passed:
            cp.wait_send()

    buf = place_own(shard, me, dtype or shard.dtype)
    return pl.pallas_call(
        body, name=name, out_shape=jax.ShapeDtypeStruct(buf.shape, buf.dtype),
        in_specs=[_HBM], out_specs=_HBM, input_output_aliases={0: 0},
        scratch_shapes=[pltpu.SemaphoreType.DMA((7,)), pltpu.SemaphoreType.DMA((7,))],
    )(buf)


def _own_block_copies(src_refs, dst_refs, send_sems, recv_sems, arrivals):
    x, y, c = _place()
    peers = [(x, y, 1 - c)] + [(*chip, c) for chip in _other_chips(x, y)]
    copies = []
    for b, (src, dst) in enumerate(zip(src_refs, dst_refs)):
        for k, peer in enumerate(peers):
            def copy(landing, b=b, k=k, peer=peer, src=src, dst=dst):
                return pltpu.make_async_remote_copy(
                    src_ref=src.at[4 * x + 2 * y + c], dst_ref=dst.at[landing],
                    send_sem=send_sems.at[4 * b + k], recv_sem=recv_sems.at[4 * b + k],
                    device_id=peer, device_id_type=pl.DeviceIdType.MESH)
            copies.append((copy(4 * x + 2 * y + c), copy(4 * peer[0] + 2 * peer[1] + peer[2]) if arrivals else None))
    return copies


def gather_start(name, bufs, after):
    n = len(bufs)

    def body(*refs):
        send_sems, recv_sems = refs[n + 1], refs[n + 2]
        thru = refs[n + 3:2 * n + 3]
        for send, _ in _own_block_copies(thru, thru, send_sems, recv_sems, False):
            send.start()
        refs[2 * n + 3][...] = jnp.zeros((SUBLANE, LANE), F32)

    return pl.pallas_call(
        body, name=name,
        out_shape=(pltpu.SemaphoreType.DMA((4 * n,)), pltpu.SemaphoreType.DMA((4 * n,)),
                   *[pltpu.HBM(b.shape, b.dtype) for b in bufs], jax.ShapeDtypeStruct((SUBLANE, LANE), F32)),
        in_specs=(*(_HBM,) * n, _ANY), out_specs=(_SEM, _SEM, *(_HBM,) * n, _TOKEN),
        input_output_aliases={b: 2 + b for b in range(n)},
        compiler_params=pltpu.CompilerParams(has_side_effects=_EFFECT),
    )(*[_hbm(b) for b in bufs], after)


def gather_wait(name, state, first, count, after):
    send_sems, recv_sems = state[:2]
    bufs = state[2 + first:2 + first + count]
    n = len(bufs)

    def body(*refs):
        ins = refs[:n]
        send_sems, recv_sems = refs[n], refs[n + 1]
        shift = 4 * first
        for send, arrival in _own_block_copies(
                ins, ins, send_sems.at[pl.ds(shift, 4 * n)], recv_sems.at[pl.ds(shift, 4 * n)], True):
            send.wait_send()
            arrival.wait_recv()

    return pl.pallas_call(
        body, name=name, out_shape=tuple(pltpu.HBM(b.shape, b.dtype) for b in bufs),
        in_specs=(*(_HBM,) * n, _SEM, _SEM, _ANY), out_specs=(_HBM,) * n,
        input_output_aliases={b: b for b in range(n)},
        compiler_params=pltpu.CompilerParams(has_side_effects=_EFFECT),
    )(*bufs, send_sems, recv_sems, after)


def gather_finish(name, bufs):
    n = len(bufs)

    def body(*refs):
        outs = refs[n:2 * n]
        send_sems, recv_sems = refs[2 * n], refs[2 * n + 1]
        x, y, c = _place()
        copies = []
        for b, out in enumerate(outs):
            for k, chip in enumerate(_other_chips(x, y)):
                sem = 3 * b + k
                copies.append((
                    pltpu.make_async_remote_copy(
                        src_ref=out.at[4 * chip[0] + 2 * chip[1] + c], dst_ref=out.at[4 * chip[0] + 2 * chip[1] + c],
                        send_sem=send_sems.at[sem], recv_sem=recv_sems.at[sem],
                        device_id=(x, y, 1 - c), device_id_type=pl.DeviceIdType.MESH),
                    pltpu.make_async_remote_copy(
                        src_ref=out.at[4 * chip[0] + 2 * chip[1] + c], dst_ref=out.at[4 * chip[0] + 2 * chip[1] + 1 - c],
                        send_sem=send_sems.at[sem], recv_sem=recv_sems.at[sem],
                        device_id=(x, y, 1 - c), device_id_type=pl.DeviceIdType.MESH)))
        for send, _ in copies:
            send.start()
        for send, arrival in copies:
            send.wait_send()
            arrival.wait_recv()

    return pl.pallas_call(
        body, name=name, out_shape=tuple(jax.ShapeDtypeStruct(b.shape, b.dtype) for b in bufs),
        in_specs=(_HBM,) * n, out_specs=(_HBM,) * n, input_output_aliases={b: b for b in range(n)},
        scratch_shapes=[pltpu.SemaphoreType.DMA((3 * n,)), pltpu.SemaphoreType.DMA((3 * n,))],
    )(*bufs)


_SEM =pl.BlockSpec(memory_space=pltpu.SEMAPHORE)
_ANY = pl.BlockSpec(memory_space=pl.ANY)
_TOKEN = pl.BlockSpec(memory_space=pltpu.VMEM)
_EFFECT = pltpu.SideEffectType.DATAFLOW_SIDE_EFFECTING


def _hbm(a):
    return pltpu.with_memory_space_constraint(a, pltpu.HBM)


def _chip_copies(p_ref, land_ref, send_sems, recv_sems):
    x, y, c = _place()
    return [pltpu.make_async_remote_copy(
        src_ref=p_ref.at[2 * px + py], dst_ref=land_ref.at[k], send_sem=send_sems.at[k], recv_sem=recv_sems.at[k],
        device_id=(px, py, c), device_id_type=pl.DeviceIdType.MESH) for k, (px, py) in enumerate(_other_chips(x, y))]


def scatter_chips_start(name, p):
    _, rows, cols = p.shape

    def body(p_ref, land_ref, send_sems, recv_sems, p_thru, land_thru, token):
        for cp in _chip_copies(p_ref, land_ref, send_sems, recv_sems):
            cp.start()
        token[...] = jnp.zeros_like(token)

    return pl.pallas_call(
        body, name=name,
        out_shape=(pltpu.SemaphoreType.DMA((3,)), pltpu.SemaphoreType.DMA((3,)), pltpu.HBM(p.shape, p.dtype),
                   pltpu.HBM((3, rows, cols), p.dtype), jax.ShapeDtypeStruct((SUBLANE, LANE), F32)),
        in_specs=(_HBM, _HBM), out_specs=(_SEM, _SEM, _HBM, _HBM, _TOKEN), input_output_aliases={0: 2, 1: 3},
        compiler_params=pltpu.CompilerParams(has_side_effects=_EFFECT),
    )(_hbm(p), _hbm(lax.empty((3, rows, cols), p.dtype)))


def scatter_chips_wait(name, send_sems, recv_sems, p_thru, land_thru, after):
    def body(p_ref, land_ref, send_sems, recv_sems, after_ref, p_out, land_out):
        for cp in _chip_copies(p_ref, land_ref, send_sems, recv_sems):
            cp.wait_send()
            cp.wait_recv()

    return pl.pallas_call(
        body, name=name,
        out_shape=(pltpu.HBM(p_thru.shape, p_thru.dtype), pltpu.HBM(land_thru.shape, land_thru.dtype)),
        in_specs=(_HBM, _HBM, _SEM, _SEM, _ANY), out_specs=(_HBM, _HBM), input_output_aliases={0: 0, 1: 1},
        compiler_params=pltpu.CompilerParams(has_side_effects=_EFFECT),
    )(p_thru, land_thru, send_sems, recv_sems, after)


def _pair_copies(g_ref, land_ref, send_sems, recv_sems):
    x, y, c = _place()
    return [pltpu.make_async_remote_copy(
        src_ref=g_ref.at[k], dst_ref=land_ref.at[k], send_sem=send_sems.at[k], recv_sem=recv_sems.at[k],
        device_id=(x, y, 1 - c), device_id_type=pl.DeviceIdType.MESH) for k in range(N_DEV // 2)]


def pair_start(name, g, after):
    n = g.shape[0]

    def body(g_ref, land_ref, after_ref, send_sems, recv_sems, g_thru, land_thru, token):
        for cp in _pair_copies(g_ref, land_ref, send_sems, recv_sems):
            cp.start()
        token[...] = jnp.zeros_like(token)

    return pl.pallas_call(
        body, name=name,
        out_shape=(pltpu.SemaphoreType.DMA((n,)), pltpu.SemaphoreType.DMA((n,)), pltpu.HBM(g.shape, g.dtype),
                   pltpu.HBM(g.shape, g.dtype), jax.ShapeDtypeStruct((SUBLANE, LANE), F32)),
        in_specs=(_HBM, _HBM, _ANY), out_specs=(_SEM, _SEM, _HBM, _HBM, _TOKEN), input_output_aliases={0: 2, 1: 3},
        compiler_params=pltpu.CompilerParams(has_side_effects=_EFFECT),
    )(_hbm(g), _hbm(lax.empty(g.shape, g.dtype)), after)


def pair_wait(name, state, after):
    send_sems, recv_sems, g_thru, land_thru, _ = state

    def body(g_ref, land_ref, send_sems, recv_sems, after_ref, g_out, land_out):
        for cp in _pair_copies(g_ref, land_ref, send_sems, recv_sems):
            cp.wait_send()
            cp.wait_recv()

    return pl.pallas_call(
        body, name=name,
        out_shape=(pltpu.HBM(g_thru.shape, g_thru.dtype), pltpu.HBM(land_thru.shape, land_thru.dtype)),
        in_specs=(_HBM, _HBM, _SEM, _SEM, _ANY), out_specs=(_HBM, _HBM), input_output_aliases={0: 0, 1: 1},
        compiler_params=pltpu.CompilerParams(has_side_effects=_EFFECT),
    )(g_thru, land_thru, send_sems, recv_sems, after)[1]


def reduce_scatter_wait(tag, state, after):
    send_sems, recv_sems, p_thru, land_thru, _ = state
    return scatter_chips_wait("rs_wait_" + tag, send_sems, recv_sems, p_thru, land_thru, after)


_SMALL = ("g_pre_mix", "g_post_mix", "g_pre_ffn", "g_post_ffn", "g_attn_grp", "g_conv_grp", "g_lru_grp",
          "dw_conv_w", "dw_conv_b", "conv_ln_g", "conv_ln_b", "lru_conv_w", "lru_conv_b",
          "lru_w_a", "lru_b_a", "lru_w_i", "lru_b_i", "lru_lambda")
_COL_SHARDED_SMALL = ("dw_conv_w", "lru_conv_w")
_BIG = ("w_in", "w_out", "w_gate", "w_up", "w_down")
_TRANSPOSED = ("w_gate", "w_up")
_ALL = ("w_in", "w_out", "g_pre_mix", "g_post_mix", "g_pre_ffn", "g_post_ffn", "g_attn_grp", "g_conv_grp", "g_lru_grp",
        "dw_conv_w", "dw_conv_b", "conv_ln_g", "conv_ln_b", "lru_conv_w", "lru_conv_b", "lru_w_a", "lru_b_a",
        "lru_w_i", "lru_b_i", "lru_lambda", "w_gate", "w_up", "w_down")


def _pack(arrays):
    flat = jnp.concatenate([a.reshape(-1) for a in arrays])
    pad = (-flat.shape[0]) % (PACK_ROWS * LANE)
    return jnp.pad(flat, (0, pad)).reshape(-1, LANE)


def _unpack(packed, shapes):
    flat = packed.reshape(-1)
    out, pos = [], 0
    for s in shapes:
        n = math.prod(s)
        out.append(flat[pos:pos + n].reshape(s))
        pos += n
    return out


def kernel(x, w_in, w_out, g_pre_mix, g_post_mix, g_pre_ffn, g_post_ffn, g_attn_grp, g_conv_grp, g_lru_grp, dw_conv_w, dw_conv_b, conv_ln_g, conv_ln_b, lru_conv_w, lru_conv_b, lru_w_a, lru_b_a, lru_w_i, lru_b_i, lru_lambda, w_gate, w_up, w_down, loss_target, m_w_in, m_w_out, m_g_pre_mix, m_g_post_mix, m_g_pre_ffn, m_g_post_ffn, m_g_attn_grp, m_g_conv_grp, m_g_lru_grp, m_dw_conv_w, m_dw_conv_b, m_conv_ln_g, m_conv_ln_b, m_lru_conv_w, m_lru_conv_b, m_lru_w_a, m_lru_b_a, m_lru_w_i, m_lru_b_i, m_lru_lambda, m_w_gate, m_w_up, m_w_down, v_w_in, v_w_out, v_g_pre_mix, v_g_post_mix, v_g_pre_ffn, v_g_post_ffn, v_g_attn_grp, v_g_conv_grp, v_g_lru_grp, v_dw_conv_w, v_dw_conv_b, v_conv_ln_g, v_conv_ln_b, v_lru_conv_w, v_lru_conv_b, v_lru_w_a, v_lru_b_a, v_lru_w_i, v_lru_b_i, v_lru_lambda, v_w_gate, v_w_up, v_w_down):
    env = dict(locals())
    wts = {n: env[n] for n in _ALL}
    mom = {n: env["m_" + n] for n in _ALL}
    var = {n: env["v_" + n] for n in _ALL}
    for group in (wts, mom, var):
        for n in _TRANSPOSED:
            group[n] = jnp.swapaxes(group[n], 1, 2)

    depth = w_in.shape[0]
    h = x[0]
    target = loss_target[0]
    t, d = h.shape
    attn_w = d // 2
    n_heads = attn_w // HEAD_DIM
    cc = d // 4
    wl = d // 4
    conv_col, lru_col = 3 * attn_w, 3 * attn_w + 2 * cc
    me = 4 * lax.axis_index("x") + 2 * lax.axis_index("y") + lax.axis_index("c")
    me_s = me.astype(jnp.int32).reshape(1)
    chip_s = (2 * lax.axis_index("x") + lax.axis_index("y")).astype(jnp.int32).reshape(1)
    core_s = lax.axis_index("c").astype(jnp.int32).reshape(1)

    n_taps = DW_LEN + LRU_LEN
    taps = jnp.concatenate([dw_conv_w, lru_conv_w], axis=1).reshape(depth * n_taps, cc // N_DEV)
    taps = all_gather("ag_taps", taps, me_s)
    taps = jnp.moveaxis(taps.reshape(N_DEV, depth, n_taps, cc // N_DEV), 0, 2).reshape(depth, n_taps, cc)
    dw_full, lcw_full = taps[:, :DW_LEN], taps[:, DW_LEN:]

    def vec(a, l):
        return a[l].reshape(1, -1)

    ag_state, started = [], taps
    for l in range(depth):
        ag_state.append(gather_start(f"ag_start_{l}", [place_own(wts[n][l], me_s, BF16) for n in _BIG], started))
        started = ag_state[l][-1]
    started = started[0:1, 0:1]

    saved = []
    u1 = rms_pre(h, vec(g_pre_mix, 0) + started)
    loss_sum = dh = dbr = None
    for l in range(depth):
        wg = {}

        def take(first, count, behind, l=l, wg=wg):
            if l < EXPOSED_GATHERS or first == 0:
                if l >= EXPOSED_GATHERS:
                    count = len(_BIG)
                got = gather_wait(f"ag_wait_{l}_{first}", ag_state[l], first, count, behind)
                wg.update(zip(_BIG[first:first + count], gather_finish(f"ag_finish_{count}", got)))

        take(0, 1, u1)
        wa_b, wi_b = lru_w_a[l].astype(BF16), lru_w_i[l].astype(BF16)
        proj = mm_proj(u1, wg["w_in"])
        y_attn = attn_fwd(proj, n_heads)
        cpre, y_conv = conv_fwd(proj, conv_col, cc, dw_full[l], vec(dw_conv_b, l), vec(conv_ln_g, l), vec(conv_ln_b, l))
        hs, y_lru = lru_fwd(proj, lru_col, wl, lcw_full[l], vec(lru_conv_b, l), wa_b, vec(lru_b_a, l), wi_b,
                            vec(lru_b_i, l), vec(lru_lambda, l))
        mixed = mix_fwd(y_attn, y_conv, y_lru, vec(g_attn_grp, l), vec(g_conv_grp, l), vec(g_lru_grp, l))
        take(1, 1, mixed)
        wg["w_out"] = wg["w_out"].reshape(attn_w + cc + wl, d)
        o = mm_plain("mm_out", mixed, wg["w_out"], "nn", F32)
        h2, u2 = res_norm(h, o, vec(g_post_mix, l), vec(g_pre_ffn, l))
        take(2, 2, u2)
        gt, up, f = ffn_up(u2, wg["w_gate"], wg["w_up"])
        take(4, 1, f)
        dn = mm_down(f, wg["w_down"])
        saved.append(dict(wg=wg, wa_b=wa_b, wi_b=wi_b, h=h, u1=u1, proj=proj, y_attn=y_attn, cpre=cpre, y_conv=y_conv,
                          hs=hs, y_lru=y_lru, mixed=mixed, o=o, h2=h2, u2=u2, gt=gt, up=up, f=f, dn=dn))
        if l + 1 < depth:
            h, u1 = res_norm(h2, dn, vec(g_post_ffn, l), vec(g_pre_mix, l + 1))
        else:
            loss_sum, dh, dbr, dg_post_ffn = final_loss(h2, dn, vec(g_post_ffn, l), target)

    loss = lax.psum(0.5 * loss_sum[0, 0] / d, MESH_AXES)

    small = {n: [None] * depth for n in _SMALL}
    rs_state = {n: [None] * depth for n in _BIG}
    after = dbr
    for l in reversed(range(depth)):
        s = saved[l]
        wg = s["wg"]
        small["g_post_ffn"][l] = dg_post_ffn
        dgt, dup = ffn_bwd(dbr, wg["w_down"], s["gt"], s["up"])
        ffn_grads = (("w_down", "down", s["f"], dbr), ("w_gate", "gate", dgt, s["u2"]), ("w_up", "up", dup, s["u2"]))
        pairs = {}
        for n, tag, a, g in ffn_grads:
            theirs = mm_dw_half("mm_dw_" + tag, "rows", a, g, core_s, True)
            pairs[n] = pair_start(f"pair_start_{tag}_{l}", theirs, after)
            after = pairs[n][-1]
        for n, tag, a, g in ffn_grads:
            recv = pair_wait(f"pair_wait_{tag}_{l}", pairs[n], after)
            after = mm_dw_half("mm_dw_" + tag, "rows", a, g, core_s, False, add=recv)
            rs_state[n][l] = scatter_chips_start(f"rs_start_{tag}_{l}", after)
        du2 = mm_dx_ffn(dgt, wg["w_gate"], dup, wg["w_up"])
        started = sum(rs_state[n][l][-1][0:1, 0:1] for n in ("w_down", "w_gate", "w_up"))
        dh2, small["g_pre_ffn"][l], do, small["g_post_mix"][l] = norm_bwd(
            dh, du2, s["h2"], vec(g_pre_ffn, l) + started, (s["o"], vec(g_post_mix, l)))
        theirs = mm_dw_half("mm_dw_out", "take", s["mixed"], do, core_s, True)
        pairs["w_out"] = pair_start(f"pair_start_out_{l}", theirs, rs_state["w_up"][l][-1])
        dmixed = mm_plain("mm_dmixed", do, wg["w_out"], "nt", F32)
        recv = pair_wait(f"pair_wait_out_{l}", pairs["w_out"], dmixed)
        rs_state["w_out"][l] = scatter_chips_start(
            f"rs_start_out_{l}", mm_dw_half("mm_dw_out", "take", s["mixed"], do, core_s, False, add=recv))
        (dya, dc, dyl, small["g_attn_grp"][l], small["g_conv_grp"][l], small["g_lru_grp"][l],
         small["conv_ln_g"][l], small["conv_ln_b"][l]) = mix_bwd(
            dmixed, s["y_attn"], s["y_conv"], s["y_lru"], s["cpre"],
            vec(g_attn_grp, l) + rs_state["w_out"][l][-1][0:1, 0:1], vec(g_conv_grp, l),
            vec(g_lru_grp, l), vec(conv_ln_g, l), vec(conv_ln_b, l))
        dq, dk, dv = attn_bwd(s["proj"], dya, n_heads)
        dvg, small["dw_conv_w"][l], small["dw_conv_b"][l] = conv_bwd(s["proj"], conv_col, cc, dc, dw_full[l])
        (dxy, small["lru_conv_w"][l], small["lru_conv_b"][l], small["lru_w_a"][l], small["lru_b_a"][l],
         small["lru_w_i"][l], small["lru_b_i"][l], small["lru_lambda"][l]) = lru_bwd(
            s["proj"], lru_col, wl, s["hs"], dyl, lcw_full[l], vec(lru_conv_b, l), s["wa_b"], vec(lru_b_a, l),
            s["wi_b"], vec(lru_b_i, l), vec(lru_lambda, l))
        dproj = jnp.concatenate([dq, dk, dv, dvg, dxy], axis=1)
        theirs = mm_dw_half("mm_dw_in", "cols", s["u1"], dproj, core_s, True)
        pairs["w_in"] = pair_start(f"pair_start_in_{l}", theirs, rs_state["w_out"][l][-1])
        du1 = mm_dx_cols("mm_dx_in", dproj, wg["w_in"])
        recv = pair_wait(f"pair_wait_in_{l}", pairs["w_in"], du1)
        after = mm_dw_half("mm_dw_in", "cols", s["u1"], dproj, core_s, False, add=recv)
        rs_state["w_in"][l] = scatter_chips_start(f"rs_start_in_{l}", after)
        g_pre = vec(g_pre_mix, l) + rs_state["w_in"][l][-1][0:1, 0:1]
        if l > 0:
            p = saved[l - 1]
            dh, small["g_pre_mix"][l], dbr, dg_post_ffn = norm_bwd(
                dh2, du1, s["h"], g_pre, (p["dn"], vec(g_post_ffn, l - 1)))
        else:
            dh, small["g_pre_mix"][l] = norm_bwd(dh2, du1, s["h"], g_pre)
    grad_x = dh[None]

    small_shapes = [(depth,) + tuple(wts[n].shape[1:]) if n not in _COL_SHARDED_SMALL
                    else (depth, wts[n].shape[1], cc) for n in _SMALL]
    part = _pack([jnp.stack([a.reshape(shp[1:]) for a in small[n]]) for n, shp in zip(_SMALL, small_shapes)])
    small_state = gather_start("ag_small_start", [place_own(part, me_s, F32)], dh)

    grads, delta, new_m, new_v = {}, {}, {}, {}
    behind = small_state[-1]
    for n in reversed(_BIG):
        parts = [reduce_scatter_wait(f"{n[2:]}_{l}", rs_state[n][l], behind) for l in range(depth)]
        shape = wts[n].shape
        _, rows, cols = parts[0][0].shape
        view = (depth, rows, cols)
        g, dl, mn, vn = adam_big("adam_" + n, wts[n].reshape(view), mom[n].reshape(view), var[n].reshape(view),
                                 parts, chip_s)
        grads[n], delta[n], new_m[n], new_v[n] = (a.reshape(shape) for a in (g, dl, mn, vn))
        behind = g
    for group in (grads, delta, new_m, new_v):
        for n in _TRANSPOSED:
            group[n] = jnp.swapaxes(group[n], 1, 2)

    gathered, = gather_finish("ag_small_finish", gather_wait("ag_small_wait", small_state, 0, 1, behind))
    g_small = _unpack(sum_parts(gathered), small_shapes)
    for n, g in zip(_SMALL, g_small):
        if n in _COL_SHARDED_SMALL:
            g = lax.dynamic_slice_in_dim(g, me * (cc // N_DEV), cc // N_DEV, axis=2)
        grads[n] = g
    local_shapes = [tuple(wts[n].shape) for n in _SMALL]
    d_small, m_small, v_small = adam_small(
        _pack([wts[n] for n in _SMALL]), _pack([mom[n] for n in _SMALL]), _pack([var[n] for n in _SMALL]),
        _pack([grads[n] for n in _SMALL]))
    delta.update(zip(_SMALL, _unpack(d_small, local_shapes)))
    new_m.update(zip(_SMALL, _unpack(m_small, local_shapes)))
    new_v.update(zip(_SMALL, _unpack(v_small, local_shapes)))

    return (loss, grad_x, *[grads[n] for n in _ALL], *[delta[n] for n in _ALL],
            *[new_m[n] for n in _ALL], *[new_v[n] for n in _ALL])
```

```python
import functools
import math

import jax
import jax.numpy as jnp
from jax import lax
from jax.experimental import pallas as pl
from jax.experimental.pallas import tpu as pltpu

F32 = jnp.float32
BF16 = jnp.bfloat16

N_DEV = 8
EPS = 1e-6
HEAD_DIM = 128
DW_LEN = 31
LRU_LEN = 4
LRU_BLOCKS = 4
LRU_C = 8.0
ATT_TQ = 512
ATT_TK = 512
ATT_SUM = 256
ATT_HEADS = 2
ROW_T = 256
CONV_HALO = 32
LRU_HALO = 8
LANE = 128
SUBLANE = 8
WIDE_TILE = 2048
PACK_ROWS = 512
EXPOSED_GATHERS = 4
VMEM_LIMIT = 56 * 1024 * 1024

ADAM_LR = 0.001
ADAM_B1 = 0.9
ADAM_B2 = 0.999
ADAM_EPS = 1e-08
ADAM_WD = 0.01
ADAM_STEP = 10

MESH_AXES = ("x", "y", "c")
_DIMS = {
    "nn": (((1,), (0,)), ((), ())),
    "nt": (((1,), (1,)), ((), ())),
    "tn": (((0,), (0,)), ((), ())),
}


def _params(n_axes):
    return pltpu.CompilerParams(
        dimension_semantics=("arbitrary",) * n_axes, vmem_limit_bytes=VMEM_LIMIT)


def _dot(a, b, mode="nn"):
    return lax.dot_general(a, b, _DIMS[mode], preferred_element_type=F32)


def _sigmoid(x):
    return 1.0 / (1.0 + jnp.exp(-x))


def _softplus(x):
    return jnp.maximum(x, 0.0) + jnp.log(1.0 + jnp.exp(-jnp.abs(x)))


def _neg_expm1(x):
    series = x * (1.0 + x * (0.5 + x * (1.0 / 6 + x * (1.0 / 24 + x * (1.0 / 120 + x * (1.0 / 720))))))
    return jnp.where(x > -0.25, -series, 1.0 - jnp.exp(x))


_GELU_C = math.sqrt(2.0 / math.pi)


def _gelu_and_grad(x):
    inner = _GELU_C * (x + 0.044715 * x * x * x)
    t = jnp.tanh(inner)
    val = 0.5 * x * (1.0 + t)
    grad = 0.5 * (1.0 + t) + 0.5 * x * (1.0 - t * t) * _GELU_C * (1.0 + 3 * 0.044715 * x * x)
    return val, grad


def _rms_stats(x):
    r = lax.rsqrt(jnp.mean(x * x, axis=-1, keepdims=True) + EPS)
    return x * r, r


def _rms_bwd(dy, x, g):
    xn, r = _rms_stats(x)
    dxn = dy * g
    dx = r * (dxn - xn * jnp.mean(dxn * xn, axis=-1, keepdims=True))
    return dx, jnp.sum(dy * xn, axis=0, keepdims=True)


def _row_spec(tr, width, col=0):
    return pl.BlockSpec((tr, width), lambda i, col=col: (i, col))


def _vec_spec(width):
    return pl.BlockSpec((1, width), lambda i: (0, 0))


def _matmul(name, mode, operands, in_specs, out_shape, out_spec, grid):
    npairs = len(operands) // 2
    nk = grid[2]
    assert nk == 1 or out_shape.dtype == F32

    def body(*refs):
        o_ref = refs[2 * npairs]

        def partial():
            acc = None
            for p in range(npairs):
                d = _dot(refs[2 * p][...], refs[2 * p + 1][...], mode)
                acc = d if acc is None else acc + d
            return acc

        if nk == 1:
            o_ref[...] = partial().astype(o_ref.dtype)
        else:
            k = pl.program_id(2)

            @pl.when(k == 0)
            def _():
                o_ref[...] = partial()

            @pl.when(k > 0)
            def _():
                o_ref[...] += partial()

    return pl.pallas_call(
        body, name=name, grid=grid, in_specs=in_specs, out_specs=out_spec, out_shape=out_shape,
        compiler_params=_params(3),
    )(*operands)


def _tile(n, t):
    if n <= t:
        return n
    return max(k for k in range(SUBLANE, t + 1, SUBLANE) if n % k == 0)


def mm_proj(u, w):
    t, d = u.shape
    nblk, _, nb = w.shape
    tm = _tile(t, WIDE_TILE)
    return _matmul(
        "mm_proj", "nn", (u, w),
        [pl.BlockSpec((tm, d), lambda j, i, k: (i, 0)), pl.BlockSpec((None, d, nb), lambda j, i, k: (j, 0, 0))],
        jax.ShapeDtypeStruct((t, nblk * nb), F32), pl.BlockSpec((tm, nb), lambda j, i, k: (i, j)),
        (nblk, t // tm, 1))


def mm_plain(name, a, b, mode, out_dtype):
    if mode == "nn":
        (m, kk), n = a.shape, b.shape[1]
    elif mode == "nt":
        (m, kk), n = a.shape, b.shape[0]
    else:
        (kk, m), n = a.shape, b.shape[1]
    tm, tn = _tile(m, 1024), _tile(n, 1024)
    a_spec = (pl.BlockSpec((kk, tm), lambda i, j, k: (0, i)) if mode == "tn"
              else pl.BlockSpec((tm, kk), lambda i, j, k: (i, 0)))
    b_spec = (pl.BlockSpec((tn, kk), lambda i, j, k: (j, 0)) if mode == "nt"
              else pl.BlockSpec((kk, tn), lambda i, j, k: (0, j)))
    return _matmul(
        name, mode, (a, b), [a_spec, b_spec],
        jax.ShapeDtypeStruct((m, n), out_dtype), pl.BlockSpec((tm, tn), lambda i, j, k: (i, j)),
        (m // tm, n // tn, 1))


def mm_down(f, w):
    nblk, t, fb = f.shape
    d = w.shape[2]
    tm, tn = _tile(t, 1024), _tile(d, WIDE_TILE)
    return _matmul(
        "mm_down", "nn", (f, w),
        [pl.BlockSpec((None, tm, fb), lambda i, j, k: (k, i, 0)), pl.BlockSpec((None, fb, tn), lambda i, j, k: (k, 0, j))],
        jax.ShapeDtypeStruct((t, d), F32), pl.BlockSpec((tm, tn), lambda i, j, k: (i, j)),
        (t // tm, d // tn, nblk))


def mm_dw_half(name, kind, a, g, core, of_sibling, add=None):
    half = N_DEV // 2
    t = g.shape[0]

    def pick(k, core_ref):
        s = 1 - core_ref[0] if of_sibling else core_ref[0]
        return 2 * k + s

    if kind == "rows":
        rows, cols = a.shape[2], g.shape[1]
        tr, tc = rows, _tile(cols, WIDE_TILE)
        a_spec = pl.BlockSpec((None, t, rows), lambda k, i, n, core_ref: (pick(k, core_ref), 0, 0))
        g_spec = pl.BlockSpec((t, tc), lambda k, i, n, core_ref: (0, n))
    elif kind == "cols":
        rows, cols = a.shape[1], g.shape[1] // N_DEV
        tr, tc = _tile(rows, WIDE_TILE), cols
        a_spec = pl.BlockSpec((t, tr), lambda k, i, n, core_ref: (0, i))
        g_spec = pl.BlockSpec((t, cols), lambda k, i, n, core_ref: (0, pick(k, core_ref)))
    else:
        rows, cols = a.shape[1] // N_DEV, g.shape[1]
        tr, tc = rows, _tile(cols, WIDE_TILE)
        a_spec = pl.BlockSpec((t, rows), lambda k, i, n, core_ref: (0, pick(k, core_ref)))
        g_spec = pl.BlockSpec((t, tc), lambda k, i, n, core_ref: (0, n))
    o_spec = pl.BlockSpec((None, tr, tc), lambda k, i, n, core_ref: (k, i, n))

    def body(core_ref, a_ref, g_ref, *rest):
        acc = _dot(a_ref[...], g_ref[...], "tn")
        if add is not None:
            acc = acc + rest[0][...].astype(F32)
        rest[-1][...] = acc.astype(BF16)

    return pl.pallas_call(
        body, name=name,
        grid_spec=pltpu.PrefetchScalarGridSpec(
            num_scalar_prefetch=1, grid=(half, rows // tr, cols // tc),
            in_specs=[a_spec, g_spec] + ([o_spec] if add is not None else []), out_specs=o_spec),
        out_shape=jax.ShapeDtypeStruct((half, rows, cols), BF16), compiler_params=_params(3),
    )(core, a, g, *(() if add is None else (add,)))


def mm_dx_cols(name, g, w):
    t = g.shape[0]
    nblk, d, nb = w.shape
    tm, tn = _tile(t, 1024), _tile(d, WIDE_TILE)
    return _matmul(
        name, "nt", (g, w),
        [pl.BlockSpec((tm, nb), lambda i, j, k: (i, k)), pl.BlockSpec((None, tn, nb), lambda i, j, k: (k, j, 0))],
        jax.ShapeDtypeStruct((t, d), F32), pl.BlockSpec((tm, tn), lambda i, j, k: (i, j)),
        (t // tm, d // tn, nblk))


def mm_dx_ffn(dgt, wg, dup, wu):
    nblk, t, fb = dgt.shape
    d = wg.shape[2]
    tm, tn = _tile(t, 1024), _tile(d, WIDE_TILE)
    a_spec = pl.BlockSpec((None, tm, fb), lambda i, j, k: (k, i, 0))
    b_spec = pl.BlockSpec((None, fb, tn), lambda i, j, k: (k, 0, j))
    return _matmul(
        "mm_dx_ffn", "nn", (dgt, wg, dup, wu), [a_spec, b_spec, a_spec, b_spec],
        jax.ShapeDtypeStruct((t, d), F32), pl.BlockSpec((tm, tn), lambda i, j, k: (i, j)),
        (t // tm, d // tn, nblk))


def ffn_up(u, wg, wu):
    t, d = u.shape
    nblk, fb, _ = wg.shape
    tm = _tile(t, 512)

    def body(u_ref, wg_ref, wu_ref, gt_ref, up_ref, f_ref):
        uu = u_ref[...]
        gt = _dot(uu, wg_ref[...], "nt")
        up = _dot(uu, wu_ref[...], "nt")
        gt_ref[...] = gt.astype(BF16)
        up_ref[...] = up.astype(BF16)
        f_ref[...] = (gt * _sigmoid(gt) * up).astype(BF16)

    w_spec = pl.BlockSpec((None, fb, d), lambda j, i: (j, 0, 0))
    o_spec = pl.BlockSpec((None, tm, fb), lambda j, i: (j, i, 0))
    return pl.pallas_call(
        body, name="ffn_up", grid=(nblk, t // tm),
        in_specs=[pl.BlockSpec((tm, d), lambda j, i: (i, 0)), w_spec, w_spec],
        out_specs=[o_spec, o_spec, o_spec],
        out_shape=[jax.ShapeDtypeStruct((nblk, t, fb), BF16)] * 3,
        compiler_params=_params(2),
    )(u, wg, wu)


def ffn_bwd(dd, wd, gt, up):
    t, d = dd.shape
    nblk, fb, _ = wd.shape
    tm = _tile(t, 512)

    def body(dd_ref, wd_ref, gt_ref, up_ref, dgt_ref, dup_ref):
        df = _dot(dd_ref[...], wd_ref[...], "nt")
        g = gt_ref[...].astype(F32)
        s = _sigmoid(g)
        dgt_ref[...] = (df * up_ref[...].astype(F32) * s * (1.0 + g * (1.0 - s))).astype(BF16)
        dup_ref[...] = (df * g * s).astype(BF16)

    s_spec = pl.BlockSpec((None, tm, fb), lambda j, i: (j, i, 0))
    return pl.pallas_call(
        body, name="ffn_bwd", grid=(nblk, t // tm),
        in_specs=[pl.BlockSpec((tm, d), lambda j, i: (i, 0)), pl.BlockSpec((None, fb, d), lambda j, i: (j, 0, 0)),
                  s_spec, s_spec],
        out_specs=[s_spec, s_spec],
        out_shape=[jax.ShapeDtypeStruct((nblk, t, fb), BF16)] * 2,
        compiler_params=_params(2),
    )(dd, wd, gt, up)


def rms_pre(h, g):
    t, d = h.shape
    tr = _tile(t, ROW_T)

    def body(h_ref, g_ref, o_ref):
        xn, _ = _rms_stats(h_ref[...])
        o_ref[...] = (xn * g_ref[...]).astype(BF16)

    return pl.pallas_call(
        body, name="rms_pre", grid=(t // tr,),
        in_specs=[_row_spec(tr, d), _vec_spec(d)], out_specs=_row_spec(tr, d),
        out_shape=jax.ShapeDtypeStruct((t, d), BF16), compiler_params=_params(1),
    )(h, g)


def res_norm(h, o, g_post, g_pre):
    t, d = h.shape
    tr = _tile(t, ROW_T)

    def body(h_ref, o_ref, gpo_ref, gpr_ref, h2_ref, u_ref):
        on, _ = _rms_stats(o_ref[...])
        h2 = h_ref[...] + on * gpo_ref[...]
        h2_ref[...] = h2
        hn, _ = _rms_stats(h2)
        u_ref[...] = (hn * gpr_ref[...]).astype(BF16)

    return pl.pallas_call(
        body, name="res_norm", grid=(t // tr,),
        in_specs=[_row_spec(tr, d), _row_spec(tr, d), _vec_spec(d), _vec_spec(d)],
        out_specs=[_row_spec(tr, d), _row_spec(tr, d)],
        out_shape=[jax.ShapeDtypeStruct((t, d), F32), jax.ShapeDtypeStruct((t, d), BF16)],
        compiler_params=_params(1),
    )(h, o, g_post, g_pre)


def final_loss(h2, dbr, g_post, target):
    t, d = h2.shape
    tr = _tile(t, ROW_T)

    def body(h2_ref, d_ref, g_ref, tg_ref, loss_ref, dy_ref, dd_ref, dg_ref):
        i = pl.program_id(0)

        @pl.when(i == 0)
        def _():
            loss_ref[...] = jnp.zeros_like(loss_ref)
            dg_ref[...] = jnp.zeros_like(dg_ref)

        x = d_ref[...]
        g = g_ref[...]
        xn, _ = _rms_stats(x)
        diff = h2_ref[...] + xn * g - tg_ref[...]
        loss_ref[...] += jnp.sum(jnp.sum(diff * diff, axis=1, keepdims=True), axis=0, keepdims=True)
        dy = diff * (1.0 / d)
        dy_ref[...] = dy
        dx, dg = _rms_bwd(dy, x, g)
        dd_ref[...] = dx.astype(BF16)
        dg_ref[...] += dg

    return pl.pallas_call(
        body, name="final_loss", grid=(t // tr,),
        in_specs=[_row_spec(tr, d), _row_spec(tr, d), _vec_spec(d), _row_spec(tr, d)],
        out_specs=[pl.BlockSpec((1, 1), lambda i: (0, 0)), _row_spec(tr, d), _row_spec(tr, d), _vec_spec(d)],
        out_shape=[jax.ShapeDtypeStruct((1, 1), F32), jax.ShapeDtypeStruct((t, d), F32),
                   jax.ShapeDtypeStruct((t, d), BF16), jax.ShapeDtypeStruct((1, d), F32)],
        compiler_params=_params(1),
    )(h2, dbr, g_post, target)


def norm_bwd(dh_out, du, h_in, g_pre, prev=None):
    t, d = h_in.shape
    tr = _tile(t, ROW_T)
    with_prev = prev is not None

    def body(*refs):
        if with_prev:
            dho_ref, du_ref, h_ref, gpr_ref, br_ref, gpo_ref, dh_ref, dgpr_ref, dbr_ref, dgpo_ref = refs
        else:
            dho_ref, du_ref, h_ref, gpr_ref, dh_ref, dgpr_ref = refs
        i = pl.program_id(0)

        @pl.when(i == 0)
        def _():
            dgpr_ref[...] = jnp.zeros_like(dgpr_ref)
            if with_prev:
                dgpo_ref[...] = jnp.zeros_like(dgpo_ref)

        dx, dg = _rms_bwd(du_ref[...], h_ref[...], gpr_ref[...])
        dh = dho_ref[...] + dx
        dh_ref[...] = dh
        dgpr_ref[...] += dg
        if with_prev:
            dbr, dg2 = _rms_bwd(dh, br_ref[...], gpo_ref[...])
            dbr_ref[...] = dbr.astype(BF16)
            dgpo_ref[...] += dg2

    row, vec = _row_spec(tr, d), _vec_spec(d)
    in_specs = [row, row, row, vec] + ([row, vec] if with_prev else [])
    out_specs = [row, vec] + ([row, vec] if with_prev else [])
    out_shape = [jax.ShapeDtypeStruct((t, d), F32), jax.ShapeDtypeStruct((1, d), F32)]
    if with_prev:
        out_shape += [jax.ShapeDtypeStruct((t, d), BF16), jax.ShapeDtypeStruct((1, d), F32)]
    args = (dh_out, du, h_in, g_pre) + (tuple(prev) if with_prev else ())
    return pl.pallas_call(
        body, name="norm_bwd_chain" if with_prev else "norm_bwd_first", grid=(t // tr,),
        in_specs=in_specs, out_specs=out_specs, out_shape=out_shape, compiler_params=_params(1),
    )(*args)


def mix_fwd(ya, yc, yl, ga, gc, gl):
    t, wa = ya.shape
    wc, wl = yc.shape[1], yl.shape[1]
    tr = _tile(t, ROW_T)

    def body(ya_ref, yc_ref, yl_ref, ga_ref, gc_ref, gl_ref, o_ref):
        o_ref[:, pl.ds(0, wa)] = (_rms_stats(ya_ref[...])[0] * ga_ref[...]).astype(BF16)
        o_ref[:, pl.ds(wa, wc)] = (_rms_stats(yc_ref[...])[0] * gc_ref[...]).astype(BF16)
        o_ref[:, pl.ds(wa + wc, wl)] = (_rms_stats(yl_ref[...])[0] * gl_ref[...]).astype(BF16)

    return pl.pallas_call(
        body, name="mix_fwd", grid=(t // tr,),
        in_specs=[_row_spec(tr, wa), _row_spec(tr, wc), _row_spec(tr, wl), _vec_spec(wa), _vec_spec(wc), _vec_spec(wl)],
        out_specs=_row_spec(tr, wa + wc + wl),
        out_shape=jax.ShapeDtypeStruct((t, wa + wc + wl), BF16), compiler_params=_params(1),
    )(ya, yc, yl, ga, gc, gl)


def mix_bwd(dmixed, ya, yc, yl, cpre, ga, gc, gl, lng, lnb):
    t, wa = ya.shape
    wc, wl = yc.shape[1], yl.shape[1]
    tr = _tile(t, ROW_T)

    def body(dm_ref, ya_ref, yc_ref, yl_ref, c_ref, ga_ref, gc_ref, gl_ref, lg_ref, lb_ref,
             dya_ref, dc_ref, dyl_ref, dga_ref, dgc_ref, dgl_ref, dlg_ref, dlb_ref):
        i = pl.program_id(0)

        @pl.when(i == 0)
        def _():
            for r in (dga_ref, dgc_ref, dgl_ref, dlg_ref, dlb_ref):
                r[...] = jnp.zeros_like(r)

        dya, dga = _rms_bwd(dm_ref[:, pl.ds(0, wa)], ya_ref[...], ga_ref[...])
        dya_ref[...] = dya
        dga_ref[...] += dga
        dyl, dgl = _rms_bwd(dm_ref[:, pl.ds(wa + wc, wl)], yl_ref[...], gl_ref[...])
        dyl_ref[...] = dyl
        dgl_ref[...] += dgl
        dyc, dgc = _rms_bwd(dm_ref[:, pl.ds(wa, wc)], yc_ref[...], gc_ref[...])
        dgc_ref[...] += dgc
        c = c_ref[...]
        xc = c - jnp.mean(c, axis=-1, keepdims=True)
        rstd = lax.rsqrt(jnp.mean(xc * xc, axis=-1, keepdims=True) + EPS)
        xhat = xc * rstd
        ln = xhat * lg_ref[...] + lb_ref[...]
        s = _sigmoid(ln)
        dln = dyc * s * (1.0 + ln * (1.0 - s))
        dlg_ref[...] += jnp.sum(dln * xhat, axis=0, keepdims=True)
        dlb_ref[...] += jnp.sum(dln, axis=0, keepdims=True)
        dxh = dln * lg_ref[...]
        dc_ref[...] = rstd * (dxh - jnp.mean(dxh, axis=-1, keepdims=True)
                              - xhat * jnp.mean(dxh * xhat, axis=-1, keepdims=True))

    return pl.pallas_call(
        body, name="mix_bwd", grid=(t // tr,),
        in_specs=[_row_spec(tr, wa + wc + wl), _row_spec(tr, wa), _row_spec(tr, wc), _row_spec(tr, wl), _row_spec(tr, wc),
                  _vec_spec(wa), _vec_spec(wc), _vec_spec(wl), _vec_spec(wc), _vec_spec(wc)],
        out_specs=[_row_spec(tr, wa), _row_spec(tr, wc), _row_spec(tr, wl),
                   _vec_spec(wa), _vec_spec(wc), _vec_spec(wl), _vec_spec(wc), _vec_spec(wc)],
        out_shape=[jax.ShapeDtypeStruct((t, wa), F32), jax.ShapeDtypeStruct((t, wc), F32), jax.ShapeDtypeStruct((t, wl), F32),
                   jax.ShapeDtypeStruct((1, wa), F32), jax.ShapeDtypeStruct((1, wc), F32), jax.ShapeDtypeStruct((1, wl), F32),
                   jax.ShapeDtypeStruct((1, wc), F32), jax.ShapeDtypeStruct((1, wc), F32)],
        compiler_params=_params(1),
    )(dmixed, ya, yc, yl, cpre, ga, gc, gl, lng, lnb)


def _hi_lo(x):
    hi = x.astype(BF16)
    return hi, (x - hi.astype(F32)).astype(BF16)


def _lane_sums(x, tri, reverse):
    nsub = x.shape[1] // tri.shape[0]
    order = range(nsub - 1, -1, -1) if reverse else range(nsub)
    parts, beyond = {}, None
    for b in order:
        blk = x[:, b * tri.shape[0]:(b + 1) * tri.shape[0]]
        hi, lo = _hi_lo(blk)
        c = _dot(hi, tri) + _dot(lo, tri)
        parts[b] = c if beyond is None else c + beyond
        tot = jnp.sum(blk, axis=1, keepdims=True)
        beyond = tot if beyond is None else beyond + tot
    return jnp.concatenate([parts[b] for b in range(nsub)], axis=1), beyond


def _att_strip(qb, kt, thresh, scale, diff, tri_gt):
    z = _dot(qb, kt, "nt") * scale
    sp = _softplus(z)
    mask = diff < thresh
    later, total = _lane_sums(jnp.where(mask, -sp, 0.0), tri_gt, True)
    return z, sp, mask, total, (z - sp) + later


def _att_consts(tq, tk):
    diff = lax.broadcasted_iota(jnp.int32, (tq, tk), 1) - lax.broadcasted_iota(jnp.int32, (tq, tk), 0)
    cb = min(tk, ATT_SUM)
    row = lax.broadcasted_iota(jnp.int32, (cb, cb), 0)
    col = lax.broadcasted_iota(jnp.int32, (cb, cb), 1)
    return diff, (row > col).astype(BF16), (row < col).astype(BF16)


def attn_fwd(proj, n_heads):
    t = proj.shape[0]
    tq, tk = _tile(t, ATT_TQ), _tile(t, ATT_TK)
    hp = ATT_HEADS
    wd = hp * HEAD_DIM
    scale = HEAD_DIM ** -0.5

    def body(q_ref, k_ref, v_ref, o_ref, kb_ref, vb_ref, acc_ref):
        kb_ref[...] = k_ref[...].astype(BF16)
        vb_ref[...] = v_ref[...].astype(BF16)
        diff, tri_gt, _ = _att_consts(tq, tk)

        def qblock(i, _):
            q0 = pl.multiple_of(i * tq, tq)
            heads = [pl.ds(h * HEAD_DIM, HEAD_DIM) for h in range(hp)]
            qbs = [q_ref[pl.ds(q0, tq), hs].astype(BF16) for hs in heads]
            acc_ref[...] = jnp.zeros_like(acc_ref)
            n_strips = (q0 + tq + tk - 1) // tk

            def strip(jj, runs):
                k0 = pl.multiple_of((n_strips - 1 - jj) * tk, tk)
                out = []
                for h, hs in enumerate(heads):
                    _, _, mask, total, logw = _att_strip(qbs[h], kb_ref[pl.ds(k0, tk), hs], q0 - k0, scale, diff, tri_gt)
                    w = jnp.where(mask, jnp.exp(logw + runs[h]), 0.0)
                    acc_ref[:, hs] += _dot(w.astype(BF16), vb_ref[pl.ds(k0, tk), hs])
                    out.append(runs[h] + total)
                return tuple(out)

            lax.fori_loop(0, n_strips, strip, tuple(jnp.zeros((tq, 1), F32) for _ in range(hp)))
            o_ref[pl.ds(q0, tq), :] = acc_ref[...]
            return 0

        lax.fori_loop(0, t // tq, qblock, 0)

    def col_spec(base):
        return pl.BlockSpec((t, wd), lambda h, base=base: (0, base + h))

    ng = n_heads // hp
    return pl.pallas_call(
        body, name="attn_fwd", grid=(ng,),
        in_specs=[col_spec(0), col_spec(ng), col_spec(2 * ng)],
        out_specs=col_spec(0),
        out_shape=jax.ShapeDtypeStruct((t, n_heads * HEAD_DIM), F32),
        scratch_shapes=[pltpu.VMEM((t, wd), BF16), pltpu.VMEM((t, wd), BF16), pltpu.VMEM((tq, wd), F32)],
        compiler_params=_params(1),
    )(proj, proj, proj)


def attn_bwd(proj, dy, n_heads):
    t = proj.shape[0]
    tq, tk = _tile(t, ATT_TQ), _tile(t, ATT_TK)
    hp = ATT_HEADS
    wd = hp * HEAD_DIM
    scale = HEAD_DIM ** -0.5

    def body(q_ref, k_ref, v_ref, dy_ref, dq_ref, dk_ref, dv_ref, qb_ref, kb_ref, vb_ref, dob_ref, dk_acc, dv_acc,
             dq_acc, run_s):
        qb_ref[...] = q_ref[...].astype(BF16)
        kb_ref[...] = k_ref[...].astype(BF16)
        vb_ref[...] = v_ref[...].astype(BF16)
        dob_ref[...] = dy_ref[...].astype(BF16)
        dk_acc[...] = jnp.zeros_like(dk_acc)
        dv_acc[...] = jnp.zeros_like(dv_acc)
        diff, tri_gt, tri_lt = _att_consts(tq, tk)

        def qblock(i, _):
            q0 = pl.multiple_of(i * tq, tq)
            heads = [pl.ds(h * HEAD_DIM, HEAD_DIM) for h in range(hp)]
            qbs = [qb_ref[pl.ds(q0, tq), hs] for hs in heads]
            dobs = [dob_ref[pl.ds(q0, tq), hs] for hs in heads]
            dq_acc[...] = jnp.zeros_like(dq_acc)
            n_strips = (q0 + tq + tk - 1) // tk

            def sweep(jj, runs):
                si = n_strips - 1 - jj
                k0 = pl.multiple_of(si * tk, tk)
                out = []
                for h, hs in enumerate(heads):
                    sp = _softplus(_dot(qbs[h], kb_ref[pl.ds(k0, tk), hs], "nt") * scale)
                    run_s[h, si] = runs[h]
                    out.append(runs[h] + jnp.sum(jnp.where(diff < q0 - k0, -sp, 0.0), axis=1, keepdims=True))
                return tuple(out)

            zero = tuple(jnp.zeros((tq, 1), F32) for _ in range(hp))
            lax.fori_loop(0, n_strips, sweep, zero)

            def strip(si, gsums):
                k0 = pl.multiple_of(si * tk, tk)
                out = []
                for h, hs in enumerate(heads):
                    kt = kb_ref[pl.ds(k0, tk), hs]
                    vt = vb_ref[pl.ds(k0, tk), hs]
                    z, sp, mask, _, logw = _att_strip(qbs[h], kt, q0 - k0, scale, diff, tri_gt)
                    w = jnp.where(mask, jnp.exp(logw + run_s[h, si]), 0.0)
                    g = w * _dot(dobs[h], vt, "nt")
                    before, gtot = _lane_sums(g, tri_lt, False)
                    sig = jnp.exp(z - sp)
                    dz = jnp.where(mask, g * (1.0 - sig) - (before + gsums[h]) * sig, 0.0) * scale
                    dzb = dz.astype(BF16)
                    dk_acc[pl.ds(k0, tk), hs] += _dot(dzb, qbs[h], "tn")
                    dv_acc[pl.ds(k0, tk), hs] += _dot(w.astype(BF16), dobs[h], "tn")
                    dq_acc[:, hs] += _dot(dzb, kt)
                    out.append(gsums[h] + gtot)
                return tuple(out)

            lax.fori_loop(0, n_strips, strip, zero)
            dq_ref[pl.ds(q0, tq), :] = dq_acc[...].astype(BF16)
            return 0

        lax.fori_loop(0, t // tq, qblock, 0)
        dk_ref[...] = dk_acc[...].astype(BF16)
        dv_ref[...] = dv_acc[...].astype(BF16)

    def col_spec(base):
        return pl.BlockSpec((t, wd), lambda h, base=base: (0, base + h))

    ng = n_heads // hp
    return pl.pallas_call(
        body, name="attn_bwd", grid=(ng,),
        in_specs=[col_spec(0), col_spec(ng), col_spec(2 * ng), col_spec(0)],
        out_specs=[col_spec(0), col_spec(0), col_spec(0)],
        out_shape=[jax.ShapeDtypeStruct((t, n_heads * HEAD_DIM), BF16)] * 3,
        scratch_shapes=[pltpu.VMEM((t, wd), BF16)] * 4 + [pltpu.VMEM((t, wd), F32)] * 2
        + [pltpu.VMEM((tq, wd), F32), pltpu.VMEM((hp, t // tk, tq, 1), F32)],
        compiler_params=_params(1),
    )(proj, proj, proj, dy)


def _glu_halo(vc, gc, vp, gp, ubuf, i, tt, halo):
    uprev = vp[pl.ds(tt - halo, halo), :] * _sigmoid(gp[pl.ds(tt - halo, halo), :])
    ubuf[pl.ds(0, halo), :] = jnp.where(i > 0, uprev, 0.0)
    ubuf[pl.ds(halo, tt), :] = vc[...] * _sigmoid(gc[...])


def conv_fwd(proj, col0, cc, w, b, lng, lnb):
    t = proj.shape[0]
    tt = _tile(t, ROW_T)
    vi, gi = col0 // cc, col0 // cc + 1
    off = CONV_HALO - (DW_LEN - 1)

    def body(vc, gc, vp, gp, w_ref, b_ref, lg_ref, lb_ref, c_ref, y_ref, ubuf):
        i = pl.program_id(0)
        _glu_halo(vc, gc, vp, gp, ubuf, i, tt, CONV_HALO)
        for ch in range(cc // LANE):
            sl = pl.ds(ch * LANE, LANE)
            acc = jnp.zeros((tt, LANE), F32) + b_ref[:, sl]
            for tap in range(DW_LEN):
                acc = acc + w_ref[pl.ds(tap, 1), sl] * ubuf[pl.ds(off + tap, tt), sl]
            c_ref[:, sl] = acc
        c = c_ref[...]
        xc = c - jnp.mean(c, axis=-1, keepdims=True)
        ln = xc * lax.rsqrt(jnp.mean(xc * xc, axis=-1, keepdims=True) + EPS) * lg_ref[...] + lb_ref[...]
        y_ref[...] = ln * _sigmoid(ln)

    cur = lambda c: pl.BlockSpec((tt, cc), lambda i, c=c: (i, c))
    prev = lambda c: pl.BlockSpec((tt, cc), lambda i, c=c: (jnp.maximum(i - 1, 0), c))
    return pl.pallas_call(
        body, name="conv_fwd", grid=(t // tt,),
        in_specs=[cur(vi), cur(gi), prev(vi), prev(gi), pl.BlockSpec((DW_LEN, cc), lambda i: (0, 0)),
                  _vec_spec(cc), _vec_spec(cc), _vec_spec(cc)],
        out_specs=[_row_spec(tt, cc), _row_spec(tt, cc)],
        out_shape=[jax.ShapeDtypeStruct((t, cc), F32)] * 2,
        scratch_shapes=[pltpu.VMEM((CONV_HALO + tt, cc), F32)],
        compiler_params=_params(1),
    )(proj, proj, proj, proj, w, b, lng, lnb)


def conv_bwd(proj, col0, cc, dc, w):
    t = proj.shape[0]
    tt = _tile(t, ROW_T)
    nt = t // tt
    vi, gi = col0 // cc, col0 // cc + 1
    off = CONV_HALO - (DW_LEN - 1)

    def body(vc, gc, vp, gp, dcc, dcn, w_ref, dvg_ref, dw_ref, db_ref, ubuf, dbuf):
        i = pl.program_id(0)

        @pl.when(i == 0)
        def _():
            dw_ref[...] = jnp.zeros_like(dw_ref)
            db_ref[...] = jnp.zeros_like(db_ref)

        _glu_halo(vc, gc, vp, gp, ubuf, i, tt, CONV_HALO)
        dbuf[pl.ds(0, tt), :] = dcc[...]
        dbuf[pl.ds(tt, CONV_HALO), :] = jnp.where(i < nt - 1, dcn[pl.ds(0, CONV_HALO), :], 0.0)
        db_ref[...] += jnp.sum(dcc[...], axis=0, keepdims=True)
        for ch in range(cc // LANE):
            sl = pl.ds(ch * LANE, LANE)
            dcv = dbuf[pl.ds(0, tt), sl]
            du = jnp.zeros((tt, LANE), F32)
            for tap in range(DW_LEN):
                du = du + w_ref[pl.ds(tap, 1), sl] * dbuf[pl.ds(DW_LEN - 1 - tap, tt), sl]
                dw_ref[pl.ds(tap, 1), sl] += jnp.sum(dcv * ubuf[pl.ds(off + tap, tt), sl], axis=0, keepdims=True)
            s = _sigmoid(gc[:, sl])
            val = vc[:, sl]
            dvg_ref[:, sl] = (du * s).astype(BF16)
            dvg_ref[:, pl.ds(cc + ch * LANE, LANE)] = (du * val * s * (1.0 - s)).astype(BF16)

    cur = lambda c: pl.BlockSpec((tt, cc), lambda i, c=c: (i, c))
    prev = lambda c: pl.BlockSpec((tt, cc), lambda i, c=c: (jnp.maximum(i - 1, 0), c))
    return pl.pallas_call(
        body, name="conv_bwd", grid=(nt,),
        in_specs=[cur(vi), cur(gi), prev(vi), prev(gi), _row_spec(tt, cc),
                  pl.BlockSpec((tt, cc), lambda i: (jnp.minimum(i + 1, nt - 1), 0)),
                  pl.BlockSpec((DW_LEN, cc), lambda i: (0, 0))],
        out_specs=[_row_spec(tt, 2 * cc), pl.BlockSpec((DW_LEN, cc), lambda i: (0, 0)), _vec_spec(cc)],
        out_shape=[jax.ShapeDtypeStruct((t, 2 * cc), BF16), jax.ShapeDtypeStruct((DW_LEN, cc), F32),
                   jax.ShapeDtypeStruct((1, cc), F32)],
        scratch_shapes=[pltpu.VMEM((CONV_HALO + tt, cc), F32), pltpu.VMEM((tt + CONV_HALO, cc), F32)],
        compiler_params=_params(1),
    )(proj, proj, proj, proj, dc, dc, w)


def _lru_gates(xbuf, cw_ref, cb_ref, wa_ref, ba_ref, wi_ref, bi_ref, lam_ref, tt, wl):
    bd = wl // LRU_BLOCKS
    xr = jnp.zeros((tt, wl), F32) + cb_ref[...]
    for tap in range(LRU_LEN):
        xr = xr + cw_ref[pl.ds(tap, 1), :] * xbuf[pl.ds(LRU_HALO - (LRU_LEN - 1) + tap, tt), :]
    xb = xr.astype(BF16)
    ga = jnp.concatenate([_dot(xb[:, n * bd:(n + 1) * bd], wa_ref[n]) for n in range(LRU_BLOCKS)], axis=1) + ba_ref[...]
    gi = jnp.concatenate([_dot(xb[:, n * bd:(n + 1) * bd], wi_ref[n]) for n in range(LRU_BLOCKS)], axis=1) + bi_ref[...]
    r = _sigmoid(ga)
    ig = _sigmoid(gi)
    spl = _softplus(-lam_ref[...])
    log_a = -LRU_C * r * spl
    a = jnp.exp(log_a)
    m = jnp.sqrt(_neg_expm1(2.0 * log_a))
    return xr, xb, r, ig, spl, a, m


def _group_scan(a8, b8, reverse):
    rid = lax.broadcasted_iota(jnp.int32, a8.shape, 0)
    aa, bb = a8, b8
    for dist in (1, 2, 4):
        shift = SUBLANE - dist if reverse else dist
        a_sh = pltpu.roll(aa, shift, 0)
        b_sh = pltpu.roll(bb, shift, 0)
        valid = (rid < SUBLANE - dist) if reverse else (rid >= dist)
        bb = jnp.where(valid, aa * b_sh + bb, bb)
        aa = jnp.where(valid, aa * a_sh, aa)
    return aa, bb


def _pick_row(x8, r):
    rid = lax.broadcasted_iota(jnp.int32, x8.shape, 0)
    return jnp.sum(jnp.where(rid == r, x8, 0.0), axis=0, keepdims=True)


def lru_fwd(proj, col0, wl, cw, cb, wa, ba, wi, bi, lam):
    t = proj.shape[0]
    tt = _tile(t, ROW_T)
    xi, yi = col0 // wl, col0 // wl + 1

    def body(xc, xp, ry, cw_ref, cb_ref, wa_ref, ba_ref, wi_ref, bi_ref, lam_ref, hs_ref, y_ref,
             xbuf, a_s, b_s, hcar):
        i = pl.program_id(0)

        @pl.when(i == 0)
        def _():
            hcar[...] = jnp.zeros_like(hcar)

        xbuf[pl.ds(0, LRU_HALO), :] = jnp.where(i > 0, xp[pl.ds(tt - LRU_HALO, LRU_HALO), :], 0.0)
        xbuf[pl.ds(LRU_HALO, tt), :] = xc[...]
        xr, _, _, ig, _, a, m = _lru_gates(xbuf, cw_ref, cb_ref, wa_ref, ba_ref, wi_ref, bi_ref, lam_ref, tt, wl)
        a_s[...] = a
        b_s[...] = m * ig * xr

        def group(gidx, h):
            r0 = pl.multiple_of(gidx * SUBLANE, SUBLANE)
            aa, bb = _group_scan(a_s[pl.ds(r0, SUBLANE), :], b_s[pl.ds(r0, SUBLANE), :], False)
            h8 = aa * h + bb
            hs_ref[pl.ds(r0, SUBLANE), :] = h8
            return _pick_row(h8, SUBLANE - 1)

        hcar[...] = lax.fori_loop(0, tt // SUBLANE, group, hcar[...])
        gel, _ = _gelu_and_grad(ry[...])
        y_ref[...] = hs_ref[...] * gel

    cur = lambda c: pl.BlockSpec((tt, wl), lambda i, c=c: (i, c))
    full = lambda shape: pl.BlockSpec(shape, lambda i: (0,) * len(shape))
    return pl.pallas_call(
        body, name="lru_fwd", grid=(t // tt,),
        in_specs=[cur(xi), pl.BlockSpec((tt, wl), lambda i: (jnp.maximum(i - 1, 0), xi)), cur(yi),
                  full((LRU_LEN, wl)), _vec_spec(wl), full(wa.shape), _vec_spec(wl), full(wi.shape), _vec_spec(wl),
                  _vec_spec(wl)],
        out_specs=[_row_spec(tt, wl), _row_spec(tt, wl)],
        out_shape=[jax.ShapeDtypeStruct((t, wl), F32)] * 2,
        scratch_shapes=[pltpu.VMEM((LRU_HALO + tt, wl), F32), pltpu.VMEM((tt, wl), F32), pltpu.VMEM((tt, wl), F32),
                        pltpu.VMEM((1, wl), F32)],
        compiler_params=_params(1),
    )(proj, proj, proj, cw, cb, wa, ba, wi, bi, lam)


def lru_bwd(proj, col0, wl, hs, dy, cw, cb, wa, ba, wi, bi, lam):
    t = proj.shape[0]
    tt = _tile(t, ROW_T)
    nt = t // tt
    xi, yi = col0 // wl, col0 // wl + 1
    bd = wl // LRU_BLOCKS

    def body(xc, xp, ry, hc, hp, dy_ref, cw_ref, cb_ref, wa_ref, ba_ref, wi_ref, bi_ref, lam_ref,
             dxy_ref, dcw_ref, dcb_ref, dwa_ref, dba_ref, dwi_ref, dbi_ref, dlam_ref,
             xbuf, hbuf, abuf, e_s, dh_s, dxbuf, dhcar):
        i = pl.program_id(0)
        first = i == 0

        @pl.when(first)
        def _():
            for r in (dcw_ref, dcb_ref, dwa_ref, dba_ref, dwi_ref, dbi_ref, dlam_ref, dhcar):
                r[...] = jnp.zeros_like(r)
            abuf[pl.ds(tt, LRU_HALO), :] = jnp.zeros((LRU_HALO, wl), F32)
            dxbuf[pl.ds(tt, LRU_HALO), :] = jnp.zeros((LRU_HALO, wl), F32)

        has_prev = i < nt - 1
        xbuf[pl.ds(0, LRU_HALO), :] = jnp.where(has_prev, xp[pl.ds(tt - LRU_HALO, LRU_HALO), :], 0.0)
        xbuf[pl.ds(LRU_HALO, tt), :] = xc[...]
        hbuf[pl.ds(0, LRU_HALO), :] = jnp.where(has_prev, hp[pl.ds(tt - LRU_HALO, LRU_HALO), :], 0.0)
        hbuf[pl.ds(LRU_HALO, tt), :] = hc[...]
        xr, xb, r, ig, spl, a, m = _lru_gates(xbuf, cw_ref, cb_ref, wa_ref, ba_ref, wi_ref, bi_ref, lam_ref, tt, wl)
        gel, dgel = _gelu_and_grad(ry[...])
        dyv = dy_ref[...]
        e_s[...] = dyv * gel
        dxy_ref[:, pl.ds(wl, wl)] = (dyv * hc[...] * dgel).astype(BF16)
        abuf[pl.ds(0, tt), :] = a
        a_next = abuf[pl.ds(1, tt), :]
        dh_s[...] = a_next

        def group(it, dh_in):
            r0 = pl.multiple_of((tt // SUBLANE - 1 - it) * SUBLANE, SUBLANE)
            aa, bb = _group_scan(dh_s[pl.ds(r0, SUBLANE), :], e_s[pl.ds(r0, SUBLANE), :], True)
            dh8 = aa * dh_in + bb
            dh_s[pl.ds(r0, SUBLANE), :] = dh8
            return _pick_row(dh8, 0)

        dhcar[...] = lax.fori_loop(0, tt // SUBLANE, group, dhcar[...])
        abuf[pl.ds(tt, LRU_HALO), :] = a[0:LRU_HALO, :]
        dh = dh_s[...]
        h_m1 = hbuf[pl.ds(LRU_HALO - 1, tt), :]
        dlog_a = dh * h_m1 * a - dh * ig * xr * (a * a / m)
        dig = dh * m * xr
        dxr = dh * m * ig
        dga = dlog_a * (-LRU_C) * spl * r * (1.0 - r)
        dgi = dig * ig * (1.0 - ig)
        dlam_ref[...] += jnp.sum(dlog_a * r, axis=0, keepdims=True) * (LRU_C * _sigmoid(-lam_ref[...]))
        dba_ref[...] += jnp.sum(dga, axis=0, keepdims=True)
        dbi_ref[...] += jnp.sum(dgi, axis=0, keepdims=True)
        dgab = dga.astype(BF16)
        dgib = dgi.astype(BF16)
        back = []
        for n in range(LRU_BLOCKS):
            sl = slice(n * bd, (n + 1) * bd)
            dwa_ref[n] += _dot(xb[:, sl], dgab[:, sl], "tn")
            dwi_ref[n] += _dot(xb[:, sl], dgib[:, sl], "tn")
            back.append(_dot(dgab[:, sl], wa_ref[n], "nt") + _dot(dgib[:, sl], wi_ref[n], "nt"))
        dxr = dxr + jnp.concatenate(back, axis=1)
        dcb_ref[...] += jnp.sum(dxr, axis=0, keepdims=True)
        dxbuf[pl.ds(0, tt), :] = dxr
        drx = jnp.zeros((tt, wl), F32)
        for tap in range(LRU_LEN):
            drx = drx + cw_ref[pl.ds(tap, 1), :] * dxbuf[pl.ds(LRU_LEN - 1 - tap, tt), :]
            dcw_ref[pl.ds(tap, 1), :] += jnp.sum(
                dxr * xbuf[pl.ds(LRU_HALO - (LRU_LEN - 1) + tap, tt), :], axis=0, keepdims=True)
        dxbuf[pl.ds(tt, LRU_HALO), :] = dxr[0:LRU_HALO, :]
        dxy_ref[:, pl.ds(0, wl)] = drx.astype(BF16)

    rev = lambda c: pl.BlockSpec((tt, wl), lambda i, c=c: (nt - 1 - i, c))
    rev_prev = lambda c: pl.BlockSpec((tt, wl), lambda i, c=c: (jnp.maximum(nt - 2 - i, 0), c))
    full = lambda shape: pl.BlockSpec(shape, lambda i: (0,) * len(shape))
    vec = _vec_spec(wl)
    return pl.pallas_call(
        body, name="lru_bwd", grid=(nt,),
        in_specs=[rev(xi), rev_prev(xi), rev(yi), rev(0), rev_prev(0), rev(0),
                  full((LRU_LEN, wl)), vec, full(wa.shape), vec, full(wi.shape), vec, vec],
        out_specs=[pl.BlockSpec((tt, 2 * wl), lambda i: (nt - 1 - i, 0)), full((LRU_LEN, wl)), vec,
                   full(wa.shape), vec, full(wi.shape), vec, vec],
        out_shape=[jax.ShapeDtypeStruct((t, 2 * wl), BF16), jax.ShapeDtypeStruct((LRU_LEN, wl), F32),
                   jax.ShapeDtypeStruct((1, wl), F32), jax.ShapeDtypeStruct(wa.shape, F32),
                   jax.ShapeDtypeStruct((1, wl), F32), jax.ShapeDtypeStruct(wi.shape, F32),
                   jax.ShapeDtypeStruct((1, wl), F32), jax.ShapeDtypeStruct((1, wl), F32)],
        scratch_shapes=[pltpu.VMEM((LRU_HALO + tt, wl), F32), pltpu.VMEM((LRU_HALO + tt, wl), F32),
                        pltpu.VMEM((tt + LRU_HALO, wl), F32), pltpu.VMEM((tt, wl), F32), pltpu.VMEM((tt, wl), F32),
                        pltpu.VMEM((tt + LRU_HALO, wl), F32), pltpu.VMEM((1, wl), F32)],
        compiler_params=_params(1),
    )(proj, proj, proj, hs, hs, dy, cw, cb, wa, ba, wi, bi, lam)


def _adamw(w, g, m, v):
    m = ADAM_B1 * m + (1.0 - ADAM_B1) * g
    v = ADAM_B2 * v + (1.0 - ADAM_B2) * (g * g)
    m_hat = m / (1.0 - ADAM_B1 ** ADAM_STEP)
    v_hat = v / (1.0 - ADAM_B2 ** ADAM_STEP)
    delta = -ADAM_LR * (m_hat / (jnp.sqrt(v_hat) + ADAM_EPS) + ADAM_WD * w)
    return delta, m, v


def adam_big(name, w, m, v, parts, chip):
    n_layers, rows, cols = w.shape
    tr = _tile(rows, 128 if cols > 1024 else 256)
    nrt = rows // tr

    def body(chip_ref, *refs):
        w_ref, m_ref, v_ref = refs[:3]
        part_refs = refs[3:3 + 4 * n_layers]
        g_ref, d_ref, mo_ref, vo_ref = refs[3 + 4 * n_layers:]
        layer = pl.program_id(0)
        for l in range(n_layers):
            @pl.when(layer == l)
            def _(l=l):
                g = part_refs[4 * l][...].astype(F32)
                for p in range(1, 4):
                    g = g + part_refs[4 * l + p][...].astype(F32)
                delta, mn, vn = _adamw(w_ref[...], g, m_ref[...], v_ref[...])
                g_ref[...] = g
                d_ref[...] = delta
                mo_ref[...] = mn
                vo_ref[...] = vn

    wspec = pl.BlockSpec((None, tr, cols), lambda l, i, chip_ref: (l, i, 0))
    operands, in_specs = [w, m, v], [wspec, wspec, wspec]
    for l in range(n_layers):
        mine, recv = parts[l]
        operands.append(mine)
        in_specs.append(pl.BlockSpec(
            (None, tr, cols), lambda ll, i, chip_ref, l=l: (chip_ref[0], jnp.where(ll == l, i, 0), 0)))
        for p in range(3):
            operands.append(recv)
            in_specs.append(pl.BlockSpec(
                (None, tr, cols), lambda ll, i, chip_ref, l=l, p=p: (p, jnp.where(ll == l, i, 0), 0)))
    return pl.pallas_call(
        body, name=name,
        grid_spec=pltpu.PrefetchScalarGridSpec(
            num_scalar_prefetch=1, grid=(n_layers, nrt), in_specs=in_specs, out_specs=[wspec] * 4),
        out_shape=[jax.ShapeDtypeStruct(w.shape, F32)] * 4, compiler_params=_params(2),
    )(chip, *operands)


def adam_small(w, m, v, g):
    rows = w.shape[0]
    tr = _tile(rows, PACK_ROWS)

    def body(w_ref, m_ref, v_ref, g_ref, d_ref, mo_ref, vo_ref):
        delta, mn, vn = _adamw(w_ref[...], g_ref[...], m_ref[...], v_ref[...])
        d_ref[...] = delta
        mo_ref[...] = mn
        vo_ref[...] = vn

    spec = _row_spec(tr, LANE)
    return pl.pallas_call(
        body, name="adam_small", grid=(rows // tr,), in_specs=[spec] * 4, out_specs=[spec] * 3,
        out_shape=[jax.ShapeDtypeStruct(w.shape, F32)] * 3, compiler_params=_params(1),
    )(w, m, v, g)


def sum_parts(parts):
    _, rows, _ = parts.shape
    tr = _tile(rows, PACK_ROWS)

    def body(p_ref, o_ref):
        acc = p_ref[0]
        for k in range(1, N_DEV):
            acc = acc + p_ref[k]
        o_ref[...] = acc

    return pl.pallas_call(
        body, name="sum_parts", grid=(rows // tr,),
        in_specs=[pl.BlockSpec((N_DEV, tr, LANE), lambda i: (0, i, 0))], out_specs=_row_spec(tr, LANE),
        out_shape=jax.ShapeDtypeStruct((rows, LANE), F32), compiler_params=_params(1),
    )(parts)


def place_own(x, me, dtype):
    rows, cols = x.shape
    tr = _tile(rows, 256)

    def body(me_ref, x_ref, o_ref):
        o_ref[...] = x_ref[...].astype(dtype)

    return pl.pallas_call(
        body, name="place_own",
        grid_spec=pltpu.PrefetchScalarGridSpec(
            num_scalar_prefetch=1, grid=(rows // tr,),
            in_specs=[pl.BlockSpec((tr, cols), lambda i, me_ref: (i, 0))],
            out_specs=pl.BlockSpec((None, tr, cols), lambda i, me_ref: (me_ref[0], i, 0))),
        out_shape=jax.ShapeDtypeStruct((N_DEV, rows, cols), dtype), compiler_params=_params(1),
    )(me, x)


_HBM = pl.BlockSpec(memory_space=pltpu.HBM)


def _place():
    return lax.axis_index("x"), lax.axis_index("y"), lax.axis_index("c")


def _other_chips(x, y):
    return [(1 - x, y), (x, 1 - y), (1 - x, 1 - y)]


def all_gather(name, shard, me, dtype=None):
    def body(buf_ref, out_ref, send_sems, recv_sems):
        del buf_ref
        x, y, c = _place()
        mine, sibling = (x, y, c), (x, y, 1 - c)
        chips = _other_chips(x, y)

        def copy(k, block, to):
            slot = out_ref.at[4 * block[0] + 2 * block[1] + block[2]]
            return pltpu.make_async_remote_copy(
                src_ref=slot, dst_ref=slot, send_sem=send_sems.at[k], recv_sem=recv_sems.at[k],
                device_id=to, device_id_type=pl.DeviceIdType.MESH)

        first = [copy(0, mine, sibling)] + [copy(1 + j, mine, (*chip, c)) for j, chip in enumerate(chips)]
        for cp in first:
            cp.start()
        passed = [copy(4 + j, (*chip, c), sibling) for j, chip in enumerate(chips)]
        for j, chip in enumerate(chips):
            copy(1 + j, (*chip, c), mine).wait_recv()
            passed[j].start()
        copy(0, sibling, mine).wait_recv()
        for j, chip in enumerate(chips):
            copy(4 + j, (*chip, 1 - c), mine).wait_recv()
        for cp in first + passed:
            cp.wait_send()

    buf = place_own(shard, me, dtype or shard.dtype)
    return pl.pallas_call(
        body, name=name, out_shape=jax.ShapeDtypeStruct(buf.shape, buf.dtype),
        in_specs=[_HBM], out_specs=_HBM, input_output_aliases={0: 0},
        scratch_shapes=[pltpu.SemaphoreType.DMA((7,)), pltpu.SemaphoreType.DMA((7,))],
    )(buf)


def _own_block_copies(src_refs, dst_refs, send_sems, recv_sems, arrivals):
    x, y, c = _place()
    peers = [(x, y, 1 - c)] + [(*chip, c) for chip in _other_chips(x, y)]
    copies = []
    for b, (src, dst) in enumerate(zip(src_refs, dst_refs)):
        for k, peer in enumerate(peers):
            def copy(landing, b=b, k=k, peer=peer, src=src, dst=dst):
                return pltpu.make_async_remote_copy(
                    src_ref=src.at[4 * x + 2 * y + c], dst_ref=dst.at[landing],
                    send_sem=send_sems.at[4 * b + k], recv_sem=recv_sems.at[4 * b + k],
                    device_id=peer, device_id_type=pl.DeviceIdType.MESH)
            copies.append((copy(4 * x + 2 * y + c), copy(4 * peer[0] + 2 * peer[1] + peer[2]) if arrivals else None))
    return copies


def gather_start(name, bufs, after):
    n = len(bufs)

    def body(*refs):
        send_sems, recv_sems = refs[n + 1], refs[n + 2]
        thru = refs[n + 3:2 * n + 3]
        for send, _ in _own_block_copies(thru, thru, send_sems, recv_sems, False):
            send.start()
        refs[2 * n + 3][...] = jnp.zeros((SUBLANE, LANE), F32)

    return pl.pallas_call(
        body, name=name,
        out_shape=(pltpu.SemaphoreType.DMA((4 * n,)), pltpu.SemaphoreType.DMA((4 * n,)),
                   *[pltpu.HBM(b.shape, b.dtype) for b in bufs], jax.ShapeDtypeStruct((SUBLANE, LANE), F32)),
        in_specs=(*(_HBM,) * n, _ANY), out_specs=(_SEM, _SEM, *(_HBM,) * n, _TOKEN),
        input_output_aliases={b: 2 + b for b in range(n)},
        compiler_params=pltpu.CompilerParams(has_side_effects=_EFFECT),
    )(*[_hbm(b) for b in bufs], after)


def gather_wait(name, state, first, count, after):
    send_sems, recv_sems = state[:2]
    bufs = state[2 + first:2 + first + count]
    n = len(bufs)

    def body(*refs):
        ins = refs[:n]
        send_sems, recv_sems = refs[n], refs[n + 1]
        shift = 4 * first
        for send, arrival in _own_block_copies(
                ins, ins, send_sems.at[pl.ds(shift, 4 * n)], recv_sems.at[pl.ds(shift, 4 * n)], True):
            send.wait_send()
            arrival.wait_recv()

    return pl.pallas_call(
        body, name=name, out_shape=tuple(pltpu.HBM(b.shape, b.dtype) for b in bufs),
        in_specs=(*(_HBM,) * n, _SEM, _SEM, _ANY), out_specs=(_HBM,) * n,
        input_output_aliases={b: b for b in range(n)},
        compiler_params=pltpu.CompilerParams(has_side_effects=_EFFECT),
    )(*bufs, send_sems, recv_sems, after)


def gather_finish(name, bufs):
    n = len(bufs)

    def body(*refs):
        outs = refs[n:2 * n]
        send_sems, recv_sems = refs[2 * n], refs[2 * n + 1]
        x, y, c = _place()
        copies = []
        for b, out in enumerate(outs):
            for k, chip in enumerate(_other_chips(x, y)):
                sem = 3 * b + k
                copies.append((
                    pltpu.make_async_remote_copy(
                        src_ref=out.at[4 * chip[0] + 2 * chip[1] + c], dst_ref=out.at[4 * chip[0] + 2 * chip[1] + c],
                        send_sem=send_sems.at[sem], recv_sem=recv_sems.at[sem],
                        device_id=(x, y, 1 - c), device_id_type=pl.DeviceIdType.MESH),
                    pltpu.make_async_remote_copy(
                        src_ref=out.at[4 * chip[0] + 2 * chip[1] + c], dst_ref=out.at[4 * chip[0] + 2 * chip[1] + 1 - c],
                        send_sem=send_sems.at[sem], recv_sem=recv_sems.at[sem],
                        device_id=(x, y, 1 - c), device_id_type=pl.DeviceIdType.MESH)))
        for send, _ in copies:
            send.start()
        for send, arrival in copies:
            send.wait_send()
            arrival.wait_recv()

    return pl.pallas_call(
        body, name=name, out_shape=tuple(jax.ShapeDtypeStruct(b.shape, b.dtype) for b in bufs),
        in_specs=(_HBM,) * n, out_specs=(_HBM,) * n, input_output_aliases={b: b for b in range(n)},
        scratch_shapes=[pltpu.SemaphoreType.DMA((3 * n,)), pltpu.SemaphoreType.DMA((3 * n,))],
    )(*bufs)


_SEM =pl.BlockSpec(memory_space=pltpu.SEMAPHORE)
_ANY = pl.BlockSpec(memory_space=pl.ANY)
_TOKEN = pl.BlockSpec(memory_space=pltpu.VMEM)
_EFFECT = pltpu.SideEffectType.DATAFLOW_SIDE_EFFECTING


def _hbm(a):
    return pltpu.with_memory_space_constraint(a, pltpu.HBM)


def _chip_copies(p_ref, land_ref, send_sems, recv_sems):
    x, y, c = _place()
    return [pltpu.make_async_remote_copy(
        src_ref=p_ref.at[2 * px + py], dst_ref=land_ref.at[k], send_sem=send_sems.at[k], recv_sem=recv_sems.at[k],
        device_id=(px, py, c), device_id_type=pl.DeviceIdType.MESH) for k, (px, py) in enumerate(_other_chips(x, y))]


def scatter_chips_start(name, p):
    _, rows, cols = p.shape

    def body(p_ref, land_ref, send_sems, recv_sems, p_thru, land_thru, token):
        for cp in _chip_copies(p_ref, land_ref, send_sems, recv_sems):
            cp.start()
        token[...] = jnp.zeros_like(token)

    return pl.pallas_call(
        body, name=name,
        out_shape=(pltpu.SemaphoreType.DMA((3,)), pltpu.SemaphoreType.DMA((3,)), pltpu.HBM(p.shape, p.dtype),
                   pltpu.HBM((3, rows, cols), p.dtype), jax.ShapeDtypeStruct((SUBLANE, LANE), F32)),
        in_specs=(_HBM, _HBM), out_specs=(_SEM, _SEM, _HBM, _HBM, _TOKEN), input_output_aliases={0: 2, 1: 3},
        compiler_params=pltpu.CompilerParams(has_side_effects=_EFFECT),
    )(_hbm(p), _hbm(lax.empty((3, rows, cols), p.dtype)))


def scatter_chips_wait(name, send_sems, recv_sems, p_thru, land_thru, after):
    def body(p_ref, land_ref, send_sems, recv_sems, after_ref, p_out, land_out):
        for cp in _chip_copies(p_ref, land_ref, send_sems, recv_sems):
            cp.wait_send()
            cp.wait_recv()

    return pl.pallas_call(
        body, name=name,
        out_shape=(pltpu.HBM(p_thru.shape, p_thru.dtype), pltpu.HBM(land_thru.shape, land_thru.dtype)),
        in_specs=(_HBM, _HBM, _SEM, _SEM, _ANY), out_specs=(_HBM, _HBM), input_output_aliases={0: 0, 1: 1},
        compiler_params=pltpu.CompilerParams(has_side_effects=_EFFECT),
    )(p_thru, land_thru, send_sems, recv_sems, after)


def _pair_copies(g_ref, land_ref, send_sems, recv_sems):
    x, y, c = _place()
    return [pltpu.make_async_remote_copy(
        src_ref=g_ref.at[k], dst_ref=land_ref.at[k], send_sem=send_sems.at[k], recv_sem=recv_sems.at[k],
        device_id=(x, y, 1 - c), device_id_type=pl.DeviceIdType.MESH) for k in range(N_DEV // 2)]


def pair_start(name, g, after):
    n = g.shape[0]

    def body(g_ref, land_ref, after_ref, send_sems, recv_sems, g_thru, land_thru, token):
        for cp in _pair_copies(g_ref, land_ref, send_sems, recv_sems):
            cp.start()
        token[...] = jnp.zeros_like(token)

    return pl.pallas_call(
        body, name=name,
        out_shape=(pltpu.SemaphoreType.DMA((n,)), pltpu.SemaphoreType.DMA((n,)), pltpu.HBM(g.shape, g.dtype),
                   pltpu.HBM(g.shape, g.dtype), jax.ShapeDtypeStruct((SUBLANE, LANE), F32)),
        in_specs=(_HBM, _HBM, _ANY), out_specs=(_SEM, _SEM, _HBM, _HBM, _TOKEN), input_output_aliases={0: 2, 1: 3},
        compiler_params=pltpu.CompilerParams(has_side_effects=_EFFECT),
    )(_hbm(g), _hbm(lax.empty(g.shape, g.dtype)), after)


def pair_wait(name, state, after):
    send_sems, recv_sems, g_thru, land_thru, _ = state

    def body(g_ref, land_ref, send_sems, recv_sems, after_ref, g_out, land_out):
        for cp in _pair_copies(g_ref, land_ref, send_sems, recv_sems):
            cp.wait_send()
            cp.wait_recv()

    return pl.pallas_call(
        body, name=name,
        out_shape=(pltpu.HBM(g_thru.shape, g_thru.dtype), pltpu.HBM(land_thru.shape, land_thru.dtype)),
        in_specs=(_HBM, _HBM, _SEM, _SEM, _ANY), out_specs=(_HBM, _HBM), input_output_aliases={0: 0, 1: 1},
        compiler_params=pltpu.CompilerParams(has_side_effects=_EFFECT),
    )(g_thru, land_thru, send_sems, recv_sems, after)[1]


def reduce_scatter_wait(tag, state, after):
    send_sems, recv_sems, p_thru, land_thru, _ = state
    return scatter_chips_wait("rs_wait_" + tag, send_sems, recv_sems, p_thru, land_thru, after)


_SMALL = ("g_pre_mix", "g_post_mix", "g_pre_ffn", "g_post_ffn", "g_attn_grp", "g_conv_grp", "g_lru_grp",
          "dw_conv_w", "dw_conv_b", "conv_ln_g", "conv_ln_b", "lru_conv_w", "lru_conv_b",
          "lru_w_a", "lru_b_a", "lru_w_i", "lru_b_i", "lru_lambda")
_COL_SHARDED_SMALL = ("dw_conv_w", "lru_conv_w")
_BIG = ("w_in", "w_out", "w_gate", "w_up", "w_down")
_TRANSPOSED = ("w_gate", "w_up")
_ALL = ("w_in", "w_out", "g_pre_mix", "g_post_mix", "g_pre_ffn", "g_post_ffn", "g_attn_grp", "g_conv_grp", "g_lru_grp",
        "dw_conv_w", "dw_conv_b", "conv_ln_g", "conv_ln_b", "lru_conv_w", "lru_conv_b", "lru_w_a", "lru_b_a",
        "lru_w_i", "lru_b_i", "lru_lambda", "w_gate", "w_up", "w_down")


def _pack(arrays):
    flat = jnp.concatenate([a.reshape(-1) for a in arrays])
    pad = (-flat.shape[0]) % (PACK_ROWS * LANE)
    return jnp.pad(flat, (0, pad)).reshape(-1, LANE)


def _unpack(packed, shapes):
    flat = packed.reshape(-1)
    out, pos = [], 0
    for s in shapes:
        n = math.prod(s)
        out.append(flat[pos:pos + n].reshape(s))
        pos += n
    return out


def kernel(x, w_in, w_out, g_pre_mix, g_post_mix, g_pre_ffn, g_post_ffn, g_attn_grp, g_conv_grp, g_lru_grp, dw_conv_w, dw_conv_b, conv_ln_g, conv_ln_b, lru_conv_w, lru_conv_b, lru_w_a, lru_b_a, lru_w_i, lru_b_i, lru_lambda, w_gate, w_up, w_down, loss_target, m_w_in, m_w_out, m_g_pre_mix, m_g_post_mix, m_g_pre_ffn, m_g_post_ffn, m_g_attn_grp, m_g_conv_grp, m_g_lru_grp, m_dw_conv_w, m_dw_conv_b, m_conv_ln_g, m_conv_ln_b, m_lru_conv_w, m_lru_conv_b, m_lru_w_a, m_lru_b_a, m_lru_w_i, m_lru_b_i, m_lru_lambda, m_w_gate, m_w_up, m_w_down, v_w_in, v_w_out, v_g_pre_mix, v_g_post_mix, v_g_pre_ffn, v_g_post_ffn, v_g_attn_grp, v_g_conv_grp, v_g_lru_grp, v_dw_conv_w, v_dw_conv_b, v_conv_ln_g, v_conv_ln_b, v_lru_conv_w, v_lru_conv_b, v_lru_w_a, v_lru_b_a, v_lru_w_i, v_lru_b_i, v_lru_lambda, v_w_gate, v_w_up, v_w_down):
    env = dict(locals())
    wts = {n: env[n] for n in _ALL}
    mom = {n: env["m_" + n] for n in _ALL}
    var = {n: env["v_" + n] for n in _ALL}
    for group in (wts, mom, var):
        for n in _TRANSPOSED:
            group[n] = jnp.swapaxes(group[n], 1, 2)

    depth = w_in.shape[0]
    h = x[0]
    target = loss_target[0]
    t, d = h.shape
    attn_w = d // 2
    n_heads = attn_w // HEAD_DIM
    cc = d // 4
    wl = d // 4
    conv_col, lru_col = 3 * attn_w, 3 * attn_w + 2 * cc
    me = 4 * lax.axis_index("x") + 2 * lax.axis_index("y") + lax.axis_index("c")
    me_s = me.astype(jnp.int32).reshape(1)
    chip_s = (2 * lax.axis_index("x") + lax.axis_index("y")).astype(jnp.int32).reshape(1)
    core_s = lax.axis_index("c").astype(jnp.int32).reshape(1)

    n_taps = DW_LEN + LRU_LEN
    taps = jnp.concatenate([dw_conv_w, lru_conv_w], axis=1).reshape(depth * n_taps, cc // N_DEV)
    taps = all_gather("ag_taps", taps, me_s)
    taps = jnp.moveaxis(taps.reshape(N_DEV, depth, n_taps, cc // N_DEV), 0, 2).reshape(depth, n_taps, cc)
    dw_full, lcw_full = taps[:, :DW_LEN], taps[:, DW_LEN:]

    def vec(a, l):
        return a[l].reshape(1, -1)

    ag_state, started = [], taps
    for l in range(depth):
        ag_state.append(gather_start(f"ag_start_{l}", [place_own(wts[n][l], me_s, BF16) for n in _BIG], started))
        started = ag_state[l][-1]
    started = started[0:1, 0:1]

    saved = []
    u1 = rms_pre(h, vec(g_pre_mix, 0) + started)
    loss_sum = dh = dbr = None
    for l in range(depth):
        wg = {}

        def take(first, count, behind, l=l, wg=wg):
            if l < EXPOSED_GATHERS or first == 0:
                if l >= EXPOSED_GATHERS:
                    count = len(_BIG)
                got = gather_wait(f"ag_wait_{l}_{first}", ag_state[l], first, count, behind)
                wg.update(zip(_BIG[first:first + count], gather_finish(f"ag_finish_{count}", got)))

        take(0, 1, u1)
        wa_b, wi_b = lru_w_a[l].astype(BF16), lru_w_i[l].astype(BF16)
        proj = mm_proj(u1, wg["w_in"])
        y_attn = attn_fwd(proj, n_heads)
        cpre, y_conv = conv_fwd(proj, conv_col, cc, dw_full[l], vec(dw_conv_b, l), vec(conv_ln_g, l), vec(conv_ln_b, l))
        hs, y_lru = lru_fwd(proj, lru_col, wl, lcw_full[l], vec(lru_conv_b, l), wa_b, vec(lru_b_a, l), wi_b,
                            vec(lru_b_i, l), vec(lru_lambda, l))
        mixed = mix_fwd(y_attn, y_conv, y_lru, vec(g_attn_grp, l), vec(g_conv_grp, l), vec(g_lru_grp, l))
        take(1, 1, mixed)
        wg["w_out"] = wg["w_out"].reshape(attn_w + cc + wl, d)
        o = mm_plain("mm_out", mixed, wg["w_out"], "nn", F32)
        h2, u2 = res_norm(h, o, vec(g_post_mix, l), vec(g_pre_ffn, l))
        take(2, 2, u2)
        gt, up, f = ffn_up(u2, wg["w_gate"], wg["w_up"])
        take(4, 1, f)
        dn = mm_down(f, wg["w_down"])
        saved.append(dict(wg=wg, wa_b=wa_b, wi_b=wi_b, h=h, u1=u1, proj=proj, y_attn=y_attn, cpre=cpre, y_conv=y_conv,
                          hs=hs, y_lru=y_lru, mixed=mixed, o=o, h2=h2, u2=u2, gt=gt, up=up, f=f, dn=dn))
        if l + 1 < depth:
            h, u1 = res_norm(h2, dn, vec(g_post_ffn, l), vec(g_pre_mix, l + 1))
        else:
            loss_sum, dh, dbr, dg_post_ffn = final_loss(h2, dn, vec(g_post_ffn, l), target)

    loss = lax.psum(0.5 * loss_sum[0, 0] / d, MESH_AXES)

    small = {n: [None] * depth for n in _SMALL}
    rs_state = {n: [None] * depth for n in _BIG}
    after = dbr
    for l in reversed(range(depth)):
        s = saved[l]
        wg = s["wg"]
        small["g_post_ffn"][l] = dg_post_ffn
        dgt, dup = ffn_bwd(dbr, wg["w_down"], s["gt"], s["up"])
        ffn_grads = (("w_down", "down", s["f"], dbr), ("w_gate", "gate", dgt, s["u2"]), ("w_up", "up", dup, s["u2"]))
        pairs = {}
        for n, tag, a, g in ffn_grads:
            theirs = mm_dw_half("mm_dw_" + tag, "rows", a, g, core_s, True)
            pairs[n] = pair_start(f"pair_start_{tag}_{l}", theirs, after)
            after = pairs[n][-1]
        for n, tag, a, g in ffn_grads:
            recv = pair_wait(f"pair_wait_{tag}_{l}", pairs[n], after)
            after = mm_dw_half("mm_dw_" + tag, "rows", a, g, core_s, False, add=recv)
            rs_state[n][l] = scatter_chips_start(f"rs_start_{tag}_{l}", after)
        du2 = mm_dx_ffn(dgt, wg["w_gate"], dup, wg["w_up"])
        started = sum(rs_state[n][l][-1][0:1, 0:1] for n in ("w_down", "w_gate", "w_up"))
        dh2, small["g_pre_ffn"][l], do, small["g_post_mix"][l] = norm_bwd(
            dh, du2, s["h2"], vec(g_pre_ffn, l) + started, (s["o"], vec(g_post_mix, l)))
        theirs = mm_dw_half("mm_dw_out", "take", s["mixed"], do, core_s, True)
        pairs["w_out"] = pair_start(f"pair_start_out_{l}", theirs, rs_state["w_up"][l][-1])
        dmixed = mm_plain("mm_dmixed", do, wg["w_out"], "nt", F32)
        recv = pair_wait(f"pair_wait_out_{l}", pairs["w_out"], dmixed)
        rs_state["w_out"][l] = scatter_chips_start(
            f"rs_start_out_{l}", mm_dw_half("mm_dw_out", "take", s["mixed"], do, core_s, False, add=recv))
        (dya, dc, dyl, small["g_attn_grp"][l], small["g_conv_grp"][l], small["g_lru_grp"][l],
         small["conv_ln_g"][l], small["conv_ln_b"][l]) = mix_bwd(
            dmixed, s["y_attn"], s["y_conv"], s["y_lru"], s["cpre"],
            vec(g_attn_grp, l) + rs_state["w_out"][l][-1][0:1, 0:1], vec(g_conv_grp, l),
            vec(g_lru_grp, l), vec(conv_ln_g, l), vec(conv_ln_b, l))
        dq, dk, dv = attn_bwd(s["proj"], dya, n_heads)
        dvg, small["dw_conv_w"][l], small["dw_conv_b"][l] = conv_bwd(s["proj"], conv_col, cc, dc, dw_full[l])
        (dxy, small["lru_conv_w"][l], small["lru_conv_b"][l], small["lru_w_a"][l], small["lru_b_a"][l],
         small["lru_w_i"][l], small["lru_b_i"][l], small["lru_lambda"][l]) = lru_bwd(
            s["proj"], lru_col, wl, s["hs"], dyl, lcw_full[l], vec(lru_conv_b, l), s["wa_b"], vec(lru_b_a, l),
            s["wi_b"], vec(lru_b_i, l), vec(lru_lambda, l))
        dproj = jnp.concatenate([dq, dk, dv, dvg, dxy], axis=1)
        theirs = mm_dw_half("mm_dw_in", "cols", s["u1"], dproj, core_s, True)
        pairs["w_in"] = pair_start(f"pair_start_in_{l}", theirs, rs_state["w_out"][l][-1])
        du1 = mm_dx_cols("mm_dx_in", dproj, wg["w_in"])
        recv = pair_wait(f"pair_wait_in_{l}", pairs["w_in"], du1)
        after = mm_dw_half("mm_dw_in", "cols", s["u1"], dproj, core_s, False, add=recv)
        rs_state["w_in"][l] = scatter_chips_start(f"rs_start_in_{l}", after)
        g_pre = vec(g_pre_mix, l) + rs_state["w_in"][l][-1][0:1, 0:1]
        if l > 0:
            p = saved[l - 1]
            dh, small["g_pre_mix"][l], dbr, dg_post_ffn = norm_bwd(
                dh2, du1, s["h"], g_pre, (p["dn"], vec(g_post_ffn, l - 1)))
        else:
            dh, small["g_pre_mix"][l] = norm_bwd(dh2, du1, s["h"], g_pre)
    grad_x = dh[None]

    small_shapes = [(depth,) + tuple(wts[n].shape[1:]) if n not in _COL_SHARDED_SMALL
                    else (depth, wts[n].shape[1], cc) for n in _SMALL]
    part = _pack([jnp.stack([a.reshape(shp[1:]) for a in small[n]]) for n, shp in zip(_SMALL, small_shapes)])
    small_state = gather_start("ag_small_start", [place_own(part, me_s, F32)], dh)

    grads, delta, new_m, new_v = {}, {}, {}, {}
    behind = small_state[-1]
    for n in reversed(_BIG):
        parts = [reduce_scatter_wait(f"{n[2:]}_{l}", rs_state[n][l], behind) for l in range(depth)]
        shape = wts[n].shape
        _, rows, cols = parts[0][0].shape
        view = (depth, rows, cols)
        g, dl, mn, vn = adam_big("adam_" + n, wts[n].reshape(view), mom[n].reshape(view), var[n].reshape(view),
                                 parts, chip_s)
        grads[n], delta[n], new_m[n], new_v[n] = (a.reshape(shape) for a in (g, dl, mn, vn))
        behind = g
    for group in (grads, delta, new_m, new_v):
        for n in _TRANSPOSED:
            group[n] = jnp.swapaxes(group[n], 1, 2)

    gathered, = gather_finish("ag_small_finish", gather_wait("ag_small_wait", small_state, 0, 1, behind))
    g_small = _unpack(sum_parts(gathered), small_shapes)
    for n, g in zip(_SMALL, g_small):
        if n in _COL_SHARDED_SMALL:
            g = lax.dynamic_slice_in_dim(g, me * (cc // N_DEV), cc // N_DEV, axis=2)
        grads[n] = g
    local_shapes = [tuple(wts[n].shape) for n in _SMALL]
    d_small, m_small, v_small = adam_small(
        _pack([wts[n] for n in _SMALL]), _pack([mom[n] for n in _SMALL]), _pack([var[n] for n in _SMALL]),
        _pack([grads[n] for n in _SMALL]))
    delta.update(zip(_SMALL, _unpack(d_small, local_shapes)))
    new_m.update(zip(_SMALL, _unpack(m_small, local_shapes)))
    new_v.update(zip(_SMALL, _unpack(v_small, local_shapes)))

    return (loss, grad_x, *[grads[n] for n in _ALL], *[delta[n] for n in _ALL],
            *[new_m[n] for n in _ALL], *[new_v[n] for n in _ALL])
```

```python
import functools
import math

import jax
import jax.numpy as jnp
from jax import lax
from jax.experimental import pallas as pl
from jax.experimental.pallas import tpu as pltpu

F32 = jnp.float32
BF16 = jnp.bfloat16

N_DEV = 8
EPS = 1e-6
HEAD_DIM = 128
DW_LEN = 31
LRU_LEN = 4
LRU_BLOCKS = 4
LRU_C = 8.0
ATT_TQ = 512
ATT_TK = 512
ATT_SUM = 256
ATT_HEADS = 2
ROW_T = 256
CONV_HALO = 32
LRU_HALO = 8
LANE = 128
SUBLANE = 8
WIDE_TILE = 2048
PACK_ROWS = 512
EXPOSED_GATHERS = 4
VMEM_LIMIT = 56 * 1024 * 1024

ADAM_LR = 0.001
ADAM_B1 = 0.9
ADAM_B2 = 0.999
ADAM_EPS = 1e-08
ADAM_WD = 0.01
ADAM_STEP = 10

MESH_AXES = ("x", "y", "c")
_DIMS = {
    "nn": (((1,), (0,)), ((), ())),
    "nt": (((1,), (1,)), ((), ())),
    "tn": (((0,), (0,)), ((), ())),
}


def _params(n_axes):
    return pltpu.CompilerParams(
        dimension_semantics=("arbitrary",) * n_axes, vmem_limit_bytes=VMEM_LIMIT)


def _dot(a, b, mode="nn"):
    return lax.dot_general(a, b, _DIMS[mode], preferred_element_type=F32)


def _sigmoid(x):
    return 1.0 / (1.0 + jnp.exp(-x))


def _softplus(x):
    return jnp.maximum(x, 0.0) + jnp.log(1.0 + jnp.exp(-jnp.abs(x)))


def _neg_expm1(x):
    series = x * (1.0 + x * (0.5 + x * (1.0 / 6 + x * (1.0 / 24 + x * (1.0 / 120 + x * (1.0 / 720))))))
    return jnp.where(x > -0.25, -series, 1.0 - jnp.exp(x))


_GELU_C = math.sqrt(2.0 / math.pi)


def _gelu_and_grad(x):
    inner = _GELU_C * (x + 0.044715 * x * x * x)
    t = jnp.tanh(inner)
    val = 0.5 * x * (1.0 + t)
    grad = 0.5 * (1.0 + t) + 0.5 * x * (1.0 - t * t) * _GELU_C * (1.0 + 3 * 0.044715 * x * x)
    return val, grad


def _rms_stats(x):
    r = lax.rsqrt(jnp.mean(x * x, axis=-1, keepdims=True) + EPS)
    return x * r, r


def _rms_bwd(dy, x, g):
    xn, r = _rms_stats(x)
    dxn = dy * g
    dx = r * (dxn - xn * jnp.mean(dxn * xn, axis=-1, keepdims=True))
    return dx, jnp.sum(dy * xn, axis=0, keepdims=True)


def _row_spec(tr, width, col=0):
    return pl.BlockSpec((tr, width), lambda i, col=col: (i, col))


def _vec_spec(width):
    return pl.BlockSpec((1, width), lambda i: (0, 0))


def _matmul(name, mode, operands, in_specs, out_shape, out_spec, grid):
    npairs = len(operands) // 2
    nk = grid[2]
    assert nk == 1 or out_shape.dtype == F32

    def body(*refs):
        o_ref = refs[2 * npairs]

        def partial():
            acc = None
            for p in range(npairs):
                d = _dot(refs[2 * p][...], refs[2 * p + 1][...], mode)
                acc = d if acc is None else acc + d
            return acc

        if nk == 1:
            o_ref[...] = partial().astype(o_ref.dtype)
        else:
            k = pl.program_id(2)

            @pl.when(k == 0)
            def _():
                o_ref[...] = partial()

            @pl.when(k > 0)
            def _():
                o_ref[...] += partial()

    return pl.pallas_call(
        body, name=name, grid=grid, in_specs=in_specs, out_specs=out_spec, out_shape=out_shape,
        compiler_params=_params(3),
    )(*operands)


def _tile(n, t):
    if n <= t:
        return n
    return max(k for k in range(SUBLANE, t + 1, SUBLANE) if n % k == 0)


def mm_proj(u, w):
    t, d = u.shape
    nblk, _, nb = w.shape
    tm = _tile(t, WIDE_TILE)
    return _matmul(
        "mm_proj", "nn", (u, w),
        [pl.BlockSpec((tm, d), lambda j, i, k: (i, 0)), pl.BlockSpec((None, d, nb), lambda j, i, k: (j, 0, 0))],
        jax.ShapeDtypeStruct((t, nblk * nb), F32), pl.BlockSpec((tm, nb), lambda j, i, k: (i, j)),
        (nblk, t // tm, 1))


def mm_plain(name, a, b, mode, out_dtype):
    if mode == "nn":
        (m, kk), n = a.shape, b.shape[1]
    elif mode == "nt":
        (m, kk), n = a.shape, b.shape[0]
    else:
        (kk, m), n = a.shape, b.shape[1]
    tm, tn = _tile(m, 1024), _tile(n, 1024)
    a_spec = (pl.BlockSpec((kk, tm), lambda i, j, k: (0, i)) if mode == "tn"
              else pl.BlockSpec((tm, kk), lambda i, j, k: (i, 0)))
    b_spec = (pl.BlockSpec((tn, kk), lambda i, j, k: (j, 0)) if mode == "nt"
              else pl.BlockSpec((kk, tn), lambda i, j, k: (0, j)))
    return _matmul(
        name, mode, (a, b), [a_spec, b_spec],
        jax.ShapeDtypeStruct((m, n), out_dtype), pl.BlockSpec((tm, tn), lambda i, j, k: (i, j)),
        (m // tm, n // tn, 1))


def mm_down(f, w):
    nblk, t, fb = f.shape
    d = w.shape[2]
    tm, tn = _tile(t, 1024), _tile(d, WIDE_TILE)
    return _matmul(
        "mm_down", "nn", (f, w),
        [pl.BlockSpec((None, tm, fb), lambda i, j, k: (k, i, 0)), pl.BlockSpec((None, fb, tn), lambda i, j, k: (k, 0, j))],
        jax.ShapeDtypeStruct((t, d), F32), pl.BlockSpec((tm, tn), lambda i, j, k: (i, j)),
        (t // tm, d // tn, nblk))


def mm_dw_half(name, kind, a, g, core, of_sibling, add=None):
    half = N_DEV // 2
    t = g.shape[0]

    def pick(k, core_ref):
        s = 1 - core_ref[0] if of_sibling else core_ref[0]
        return 2 * k + s

    if kind == "rows":
        rows, cols = a.shape[2], g.shape[1]
        tr, tc = rows, _tile(cols, WIDE_TILE)
        a_spec = pl.BlockSpec((None, t, rows), lambda k, i, n, core_ref: (pick(k, core_ref), 0, 0))
        g_spec = pl.BlockSpec((t, tc), lambda k, i, n, core_ref: (0, n))
    elif kind == "cols":
        rows, cols = a.shape[1], g.shape[1] // N_DEV
        tr, tc = _tile(rows, WIDE_TILE), cols
        a_spec = pl.BlockSpec((t, tr), lambda k, i, n, core_ref: (0, i))
        g_spec = pl.BlockSpec((t, cols), lambda k, i, n, core_ref: (0, pick(k, core_ref)))
    else:
        rows, cols = a.shape[1] // N_DEV, g.shape[1]
        tr, tc = rows, _tile(cols, WIDE_TILE)
        a_spec = pl.BlockSpec((t, rows), lambda k, i, n, core_ref: (0, pick(k, core_ref)))
        g_spec = pl.BlockSpec((t, tc), lambda k, i, n, core_ref: (0, n))
    o_spec = pl.BlockSpec((None, tr, tc), lambda k, i, n, core_ref: (k, i, n))

    def body(core_ref, a_ref, g_ref, *rest):
        acc = _dot(a_ref[...], g_ref[...], "tn")
        if add is not None:
            acc = acc + rest[0][...].astype(F32)
        rest[-1][...] = acc.astype(BF16)

    return pl.pallas_call(
        body, name=name,
        grid_spec=pltpu.PrefetchScalarGridSpec(
            num_scalar_prefetch=1, grid=(half, rows // tr, cols // tc),
            in_specs=[a_spec, g_spec] + ([o_spec] if add is not None else []), out_specs=o_spec),
        out_shape=jax.ShapeDtypeStruct((half, rows, cols), BF16), compiler_params=_params(3),
    )(core, a, g, *(() if add is None else (add,)))


def mm_dx_cols(name, g, w):
    t = g.shape[0]
    nblk, d, nb = w.shape
    tm, tn = _tile(t, 1024), _tile(d, WIDE_TILE)
    return _matmul(
        name, "nt", (g, w),
        [pl.BlockSpec((tm, nb), lambda i, j, k: (i, k)), pl.BlockSpec((None, tn, nb), lambda i, j, k: (k, j, 0))],
        jax.ShapeDtypeStruct((t, d), F32), pl.BlockSpec((tm, tn), lambda i, j, k: (i, j)),
        (t // tm, d // tn, nblk))


def mm_dx_ffn(dgt, wg, dup, wu):
    nblk, t, fb = dgt.shape
    d = wg.shape[2]
    tm, tn = _tile(t, 1024), _tile(d, WIDE_TILE)
    a_spec = pl.BlockSpec((None, tm, fb), lambda i, j, k: (k, i, 0))
    b_spec = pl.BlockSpec((None, fb, tn), lambda i, j, k: (k, 0, j))
    return _matmul(
        "mm_dx_ffn", "nn", (dgt, wg, dup, wu), [a_spec, b_spec, a_spec, b_spec],
        jax.ShapeDtypeStruct((t, d), F32), pl.BlockSpec((tm, tn), lambda i, j, k: (i, j)),
        (t // tm, d // tn, nblk))


def ffn_up(u, wg, wu):
    t, d = u.shape
    nblk, fb, _ = wg.shape
    tm = _tile(t, 512)

    def body(u_ref, wg_ref, wu_ref, gt_ref, up_ref, f_ref):
        for r0 in range(0, t, tm):
            rows = pl.ds(r0, tm)
            uu = u_ref[rows, :]
            gt = _dot(uu, wg_ref[...], "nt")
            up = _dot(uu, wu_ref[...], "nt")
            gt_ref[rows, :] = gt.astype(BF16)
            up_ref[rows, :] = up.astype(BF16)
            f_ref[rows, :] = (gt * _sigmoid(gt) * up).astype(BF16)

    w_spec = pl.BlockSpec((None, fb, d), lambda j: (j, 0, 0))
    o_spec = pl.BlockSpec((None, t, fb), lambda j: (j, 0, 0))
    return pl.pallas_call(
        body, name="ffn_up", grid=(nblk,),
        in_specs=[pl.BlockSpec((t, d), lambda j: (0, 0)), w_spec, w_spec],
        out_specs=[o_spec, o_spec, o_spec],
        out_shape=[jax.ShapeDtypeStruct((nblk, t, fb), BF16)] * 3,
        compiler_params=_params(1),
    )(u, wg, wu)


def ffn_bwd(dd, wd, gt, up):
    t, d = dd.shape
    nblk, fb, _ = wd.shape
    tm = _tile(t, 512)

    def body(dd_ref, wd_ref, gt_ref, up_ref, dgt_ref, dup_ref):
        for r0 in range(0, t, tm):
            rows = pl.ds(r0, tm)
            df = _dot(dd_ref[rows, :], wd_ref[...], "nt")
            g = gt_ref[rows, :].astype(F32)
            s = _sigmoid(g)
            dgt_ref[rows, :] = (df * up_ref[rows, :].astype(F32) * s * (1.0 + g * (1.0 - s))).astype(BF16)
            dup_ref[rows, :] = (df * g * s).astype(BF16)

    s_spec = pl.BlockSpec((None, t, fb), lambda j: (j, 0, 0))
    return pl.pallas_call(
        body, name="ffn_bwd", grid=(nblk,),
        in_specs=[pl.BlockSpec((t, d), lambda j: (0, 0)), pl.BlockSpec((None, fb, d), lambda j: (j, 0, 0)),
                  s_spec, s_spec],
        out_specs=[s_spec, s_spec],
        out_shape=[jax.ShapeDtypeStruct((nblk, t, fb), BF16)] * 2,
        compiler_params=_params(1),
    )(dd, wd, gt, up)


def rms_pre(h, g):
    t, d = h.shape
    tr = _tile(t, ROW_T)

    def body(h_ref, g_ref, o_ref):
        xn, _ = _rms_stats(h_ref[...])
        o_ref[...] = (xn * g_ref[...]).astype(BF16)

    return pl.pallas_call(
        body, name="rms_pre", grid=(t // tr,),
        in_specs=[_row_spec(tr, d), _vec_spec(d)], out_specs=_row_spec(tr, d),
        out_shape=jax.ShapeDtypeStruct((t, d), BF16), compiler_params=_params(1),
    )(h, g)


def res_norm(h, o, g_post, g_pre):
    t, d = h.shape
    tr = _tile(t, ROW_T)

    def body(h_ref, o_ref, gpo_ref, gpr_ref, h2_ref, u_ref):
        on, _ = _rms_stats(o_ref[...])
        h2 = h_ref[...] + on * gpo_ref[...]
        h2_ref[...] = h2
        hn, _ = _rms_stats(h2)
        u_ref[...] = (hn * gpr_ref[...]).astype(BF16)

    return pl.pallas_call(
        body, name="res_norm", grid=(t // tr,),
        in_specs=[_row_spec(tr, d), _row_spec(tr, d), _vec_spec(d), _vec_spec(d)],
        out_specs=[_row_spec(tr, d), _row_spec(tr, d)],
        out_shape=[jax.ShapeDtypeStruct((t, d), F32), jax.ShapeDtypeStruct((t, d), BF16)],
        compiler_params=_params(1),
    )(h, o, g_post, g_pre)


def final_loss(h2, dbr, g_post, target):
    t, d = h2.shape
    tr = _tile(t, ROW_T)

    def body(h2_ref, d_ref, g_ref, tg_ref, loss_ref, dy_ref, dd_ref, dg_ref):
        i = pl.program_id(0)

        @pl.when(i == 0)
        def _():
            loss_ref[...] = jnp.zeros_like(loss_ref)
            dg_ref[...] = jnp.zeros_like(dg_ref)

        x = d_ref[...]
        g = g_ref[...]
        xn, _ = _rms_stats(x)
        diff = h2_ref[...] + xn * g - tg_ref[...]
        loss_ref[...] += jnp.sum(jnp.sum(diff * diff, axis=1, keepdims=True), axis=0, keepdims=True)
        dy = diff * (1.0 / d)
        dy_ref[...] = dy
        dx, dg = _rms_bwd(dy, x, g)
        dd_ref[...] = dx.astype(BF16)
        dg_ref[...] += dg

    return pl.pallas_call(
        body, name="final_loss", grid=(t // tr,),
        in_specs=[_row_spec(tr, d), _row_spec(tr, d), _vec_spec(d), _row_spec(tr, d)],
        out_specs=[pl.BlockSpec((1, 1), lambda i: (0, 0)), _row_spec(tr, d), _row_spec(tr, d), _vec_spec(d)],
        out_shape=[jax.ShapeDtypeStruct((1, 1), F32), jax.ShapeDtypeStruct((t, d), F32),
                   jax.ShapeDtypeStruct((t, d), BF16), jax.ShapeDtypeStruct((1, d), F32)],
        compiler_params=_params(1),
    )(h2, dbr, g_post, target)


def norm_bwd(dh_out, du, h_in, g_pre, prev=None):
    t, d = h_in.shape
    tr = _tile(t, ROW_T)
    with_prev = prev is not None

    def body(*refs):
        if with_prev:
            dho_ref, du_ref, h_ref, gpr_ref, br_ref, gpo_ref, dh_ref, dgpr_ref, dbr_ref, dgpo_ref = refs
        else:
            dho_ref, du_ref, h_ref, gpr_ref, dh_ref, dgpr_ref = refs
        i = pl.program_id(0)

        @pl.when(i == 0)
        def _():
            dgpr_ref[...] = jnp.zeros_like(dgpr_ref)
            if with_prev:
                dgpo_ref[...] = jnp.zeros_like(dgpo_ref)

        dx, dg = _rms_bwd(du_ref[...], h_ref[...], gpr_ref[...])
        dh = dho_ref[...] + dx
        dh_ref[...] = dh
        dgpr_ref[...] += dg
        if with_prev:
            dbr, dg2 = _rms_bwd(dh, br_ref[...], gpo_ref[...])
            dbr_ref[...] = dbr.astype(BF16)
            dgpo_ref[...] += dg2

    row, vec = _row_spec(tr, d), _vec_spec(d)
    in_specs = [row, row, row, vec] + ([row, vec] if with_prev else [])
    out_specs = [row, vec] + ([row, vec] if with_prev else [])
    out_shape = [jax.ShapeDtypeStruct((t, d), F32), jax.ShapeDtypeStruct((1, d), F32)]
    if with_prev:
        out_shape += [jax.ShapeDtypeStruct((t, d), BF16), jax.ShapeDtypeStruct((1, d), F32)]
    args = (dh_out, du, h_in, g_pre) + (tuple(prev) if with_prev else ())
    return pl.pallas_call(
        body, name="norm_bwd_chain" if with_prev else "norm_bwd_first", grid=(t // tr,),
        in_specs=in_specs, out_specs=out_specs, out_shape=out_shape, compiler_params=_params(1),
    )(*args)


def mix_fwd(ya, yc, yl, ga, gc, gl):
    t, wa = ya.shape
    wc, wl = yc.shape[1], yl.shape[1]
    tr = _tile(t, ROW_T)

    def body(ya_ref, yc_ref, yl_ref, ga_ref, gc_ref, gl_ref, o_ref):
        o_ref[:, pl.ds(0, wa)] = (_rms_stats(ya_ref[...])[0] * ga_ref[...]).astype(BF16)
        o_ref[:, pl.ds(wa, wc)] = (_rms_stats(yc_ref[...])[0] * gc_ref[...]).astype(BF16)
        o_ref[:, pl.ds(wa + wc, wl)] = (_rms_stats(yl_ref[...])[0] * gl_ref[...]).astype(BF16)

    return pl.pallas_call(
        body, name="mix_fwd", grid=(t // tr,),
        in_specs=[_row_spec(tr, wa), _row_spec(tr, wc), _row_spec(tr, wl), _vec_spec(wa), _vec_spec(wc), _vec_spec(wl)],
        out_specs=_row_spec(tr, wa + wc + wl),
        out_shape=jax.ShapeDtypeStruct((t, wa + wc + wl), BF16), compiler_params=_params(1),
    )(ya, yc, yl, ga, gc, gl)


def mix_bwd(dmixed, ya, yc, yl, cpre, ga, gc, gl, lng, lnb):
    t, wa = ya.shape
    wc, wl = yc.shape[1], yl.shape[1]
    tr = _tile(t, ROW_T)

    def body(dm_ref, ya_ref, yc_ref, yl_ref, c_ref, ga_ref, gc_ref, gl_ref, lg_ref, lb_ref,
             dya_ref, dc_ref, dyl_ref, dga_ref, dgc_ref, dgl_ref, dlg_ref, dlb_ref):
        i = pl.program_id(0)

        @pl.when(i == 0)
        def _():
            for r in (dga_ref, dgc_ref, dgl_ref, dlg_ref, dlb_ref):
                r[...] = jnp.zeros_like(r)

        dya, dga = _rms_bwd(dm_ref[:, pl.ds(0, wa)], ya_ref[...], ga_ref[...])
        dya_ref[...] = dya
        dga_ref[...] += dga
        dyl, dgl = _rms_bwd(dm_ref[:, pl.ds(wa + wc, wl)], yl_ref[...], gl_ref[...])
        dyl_ref[...] = dyl
        dgl_ref[...] += dgl
        dyc, dgc = _rms_bwd(dm_ref[:, pl.ds(wa, wc)], yc_ref[...], gc_ref[...])
        dgc_ref[...] += dgc
        c = c_ref[...]
        xc = c - jnp.mean(c, axis=-1, keepdims=True)
        rstd = lax.rsqrt(jnp.mean(xc * xc, axis=-1, keepdims=True) + EPS)
        xhat = xc * rstd
        ln = xhat * lg_ref[...] + lb_ref[...]
        s = _sigmoid(ln)
        dln = dyc * s * (1.0 + ln * (1.0 - s))
        dlg_ref[...] += jnp.sum(dln * xhat, axis=0, keepdims=True)
        dlb_ref[...] += jnp.sum(dln, axis=0, keepdims=True)
        dxh = dln * lg_ref[...]
        dc_ref[...] = rstd * (dxh - jnp.mean(dxh, axis=-1, keepdims=True)
                              - xhat * jnp.mean(dxh * xhat, axis=-1, keepdims=True))

    return pl.pallas_call(
        body, name="mix_bwd", grid=(t // tr,),
        in_specs=[_row_spec(tr, wa + wc + wl), _row_spec(tr, wa), _row_spec(tr, wc), _row_spec(tr, wl), _row_spec(tr, wc),
                  _vec_spec(wa), _vec_spec(wc), _vec_spec(wl), _vec_spec(wc), _vec_spec(wc)],
        out_specs=[_row_spec(tr, wa), _row_spec(tr, wc), _row_spec(tr, wl),
                   _vec_spec(wa), _vec_spec(wc), _vec_spec(wl), _vec_spec(wc), _vec_spec(wc)],
        out_shape=[jax.ShapeDtypeStruct((t, wa), F32), jax.ShapeDtypeStruct((t, wc), F32), jax.ShapeDtypeStruct((t, wl), F32),
                   jax.ShapeDtypeStruct((1, wa), F32), jax.ShapeDtypeStruct((1, wc), F32), jax.ShapeDtypeStruct((1, wl), F32),
                   jax.ShapeDtypeStruct((1, wc), F32), jax.ShapeDtypeStruct((1, wc), F32)],
        compiler_params=_params(1),
    )(dmixed, ya, yc, yl, cpre, ga, gc, gl, lng, lnb)


def _hi_lo(x):
    hi = x.astype(BF16)
    return hi, (x - hi.astype(F32)).astype(BF16)


def _lane_sums(x, tri, reverse):
    nsub = x.shape[1] // tri.shape[0]
    order = range(nsub - 1, -1, -1) if reverse else range(nsub)
    parts, beyond = {}, None
    for b in order:
        blk = x[:, b * tri.shape[0]:(b + 1) * tri.shape[0]]
        hi, lo = _hi_lo(blk)
        c = _dot(hi, tri) + _dot(lo, tri)
        parts[b] = c if beyond is None else c + beyond
        tot = jnp.sum(blk, axis=1, keepdims=True)
        beyond = tot if beyond is None else beyond + tot
    return jnp.concatenate([parts[b] for b in range(nsub)], axis=1), beyond


def _att_strip(qb, kt, thresh, scale, diff, tri_gt):
    z = _dot(qb, kt, "nt") * scale
    sp = _softplus(z)
    mask = diff < thresh
    later, total = _lane_sums(jnp.where(mask, -sp, 0.0), tri_gt, True)
    return z, sp, mask, total, (z - sp) + later


def _att_consts(tq, tk):
    diff = lax.broadcasted_iota(jnp.int32, (tq, tk), 1) - lax.broadcasted_iota(jnp.int32, (tq, tk), 0)
    cb = min(tk, ATT_SUM)
    row = lax.broadcasted_iota(jnp.int32, (cb, cb), 0)
    col = lax.broadcasted_iota(jnp.int32, (cb, cb), 1)
    return diff, (row > col).astype(BF16), (row < col).astype(BF16)


def attn_fwd(proj, n_heads):
    t = proj.shape[0]
    tq, tk = _tile(t, ATT_TQ), _tile(t, ATT_TK)
    hp = ATT_HEADS
    wd = hp * HEAD_DIM
    scale = HEAD_DIM ** -0.5

    def body(q_ref, k_ref, v_ref, o_ref, kb_ref, vb_ref, acc_ref):
        kb_ref[...] = k_ref[...].astype(BF16)
        vb_ref[...] = v_ref[...].astype(BF16)
        diff, tri_gt, _ = _att_consts(tq, tk)

        def qblock(i, _):
            q0 = pl.multiple_of(i * tq, tq)
            heads = [pl.ds(h * HEAD_DIM, HEAD_DIM) for h in range(hp)]
            qbs = [q_ref[pl.ds(q0, tq), hs].astype(BF16) for hs in heads]
            acc_ref[...] = jnp.zeros_like(acc_ref)
            n_strips = (q0 + tq + tk - 1) // tk

            def strip(jj, runs):
                k0 = pl.multiple_of((n_strips - 1 - jj) * tk, tk)
                out = []
                for h, hs in enumerate(heads):
                    _, _, mask, total, logw = _att_strip(qbs[h], kb_ref[pl.ds(k0, tk), hs], q0 - k0, scale, diff, tri_gt)
                    w = jnp.where(mask, jnp.exp(logw + runs[h]), 0.0)
                    acc_ref[:, hs] += _dot(w.astype(BF16), vb_ref[pl.ds(k0, tk), hs])
                    out.append(runs[h] + total)
                return tuple(out)

            lax.fori_loop(0, n_strips, strip, tuple(jnp.zeros((tq, 1), F32) for _ in range(hp)))
            o_ref[pl.ds(q0, tq), :] = acc_ref[...]
            return 0

        lax.fori_loop(0, t // tq, qblock, 0)

    def col_spec(base):
        return pl.BlockSpec((t, wd), lambda h, base=base: (0, base + h))

    ng = n_heads // hp
    return pl.pallas_call(
        body, name="attn_fwd", grid=(ng,),
        in_specs=[col_spec(0), col_spec(ng), col_spec(2 * ng)],
        out_specs=col_spec(0),
        out_shape=jax.ShapeDtypeStruct((t, n_heads * HEAD_DIM), F32),
        scratch_shapes=[pltpu.VMEM((t, wd), BF16), pltpu.VMEM((t, wd), BF16), pltpu.VMEM((tq, wd), F32)],
        compiler_params=_params(1),
    )(proj, proj, proj)


def attn_bwd(proj, dy, n_heads):
    t = proj.shape[0]
    tq, tk = _tile(t, ATT_TQ), _tile(t, ATT_TK)
    hp = ATT_HEADS
    wd = hp * HEAD_DIM
    scale = HEAD_DIM ** -0.5

    def body(q_ref, k_ref, v_ref, dy_ref, dq_ref, dk_ref, dv_ref, qb_ref, kb_ref, vb_ref, dob_ref, dk_acc, dv_acc,
             dq_acc, run_s):
        qb_ref[...] = q_ref[...].astype(BF16)
        kb_ref[...] = k_ref[...].astype(BF16)
        vb_ref[...] = v_ref[...].astype(BF16)
        dob_ref[...] = dy_ref[...].astype(BF16)
        dk_acc[...] = jnp.zeros_like(dk_acc)
        dv_acc[...] = jnp.zeros_like(dv_acc)
        diff, tri_gt, tri_lt = _att_consts(tq, tk)

        def qblock(i, _):
            q0 = pl.multiple_of(i * tq, tq)
            heads = [pl.ds(h * HEAD_DIM, HEAD_DIM) for h in range(hp)]
            qbs = [qb_ref[pl.ds(q0, tq), hs] for hs in heads]
            dobs = [dob_ref[pl.ds(q0, tq), hs] for hs in heads]
            dq_acc[...] = jnp.zeros_like(dq_acc)
            n_strips = (q0 + tq + tk - 1) // tk

            def sweep(jj, runs):
                si = n_strips - 1 - jj
                k0 = pl.multiple_of(si * tk, tk)
                out = []
                for h, hs in enumerate(heads):
                    sp = _softplus(_dot(qbs[h], kb_ref[pl.ds(k0, tk), hs], "nt") * scale)
                    run_s[h, si] = runs[h]
                    out.append(runs[h] + jnp.sum(jnp.where(diff < q0 - k0, -sp, 0.0), axis=1, keepdims=True))
                return tuple(out)

            zero = tuple(jnp.zeros((tq, 1), F32) for _ in range(hp))
            lax.fori_loop(0, n_strips, sweep, zero)

            def strip(si, gsums):
                k0 = pl.multiple_of(si * tk, tk)
                out = []
                for h, hs in enumerate(heads):
                    kt = kb_ref[pl.ds(k0, tk), hs]
                    vt = vb_ref[pl.ds(k0, tk), hs]
                    z, sp, mask, _, logw = _att_strip(qbs[h], kt, q0 - k0, scale, diff, tri_gt)
                    w = jnp.where(mask, jnp.exp(logw + run_s[h, si]), 0.0)
                    g = w * _dot(dobs[h], vt, "nt")
                    before, gtot = _lane_sums(g, tri_lt, False)
                    sig = jnp.exp(z - sp)
                    dz = jnp.where(mask, g * (1.0 - sig) - (before + gsums[h]) * sig, 0.0) * scale
                    dzb = dz.astype(BF16)
                    dk_acc[pl.ds(k0, tk), hs] += _dot(dzb, qbs[h], "tn")
                    dv_acc[pl.ds(k0, tk), hs] += _dot(w.astype(BF16), dobs[h], "tn")
                    dq_acc[:, hs] += _dot(dzb, kt)
                    out.append(gsums[h] + gtot)
                return tuple(out)

            lax.fori_loop(0, n_strips, strip, zero)
            dq_ref[pl.ds(q0, tq), :] = dq_acc[...].astype(BF16)
            return 0

        lax.fori_loop(0, t // tq, qblock, 0)
        dk_ref[...] = dk_acc[...].astype(BF16)
        dv_ref[...] = dv_acc[...].astype(BF16)

    def col_spec(base):
        return pl.BlockSpec((t, wd), lambda h, base=base: (0, base + h))

    ng = n_heads // hp
    return pl.pallas_call(
        body, name="attn_bwd", grid=(ng,),
        in_specs=[col_spec(0), col_spec(ng), col_spec(2 * ng), col_spec(0)],
        out_specs=[col_spec(0), col_spec(0), col_spec(0)],
        out_shape=[jax.ShapeDtypeStruct((t, n_heads * HEAD_DIM), BF16)] * 3,
        scratch_shapes=[pltpu.VMEM((t, wd), BF16)] * 4 + [pltpu.VMEM((t, wd), F32)] * 2
        + [pltpu.VMEM((tq, wd), F32), pltpu.VMEM((hp, t // tk, tq, 1), F32)],
        compiler_params=_params(1),
    )(proj, proj, proj, dy)


def _glu_halo(vc, gc, vp, gp, ubuf, i, tt, halo):
    uprev = vp[pl.ds(tt - halo, halo), :] * _sigmoid(gp[pl.ds(tt - halo, halo), :])
    ubuf[pl.ds(0, halo), :] = jnp.where(i > 0, uprev, 0.0)
    ubuf[pl.ds(halo, tt), :] = vc[...] * _sigmoid(gc[...])


def conv_fwd(proj, col0, cc, w, b, lng, lnb):
    t = proj.shape[0]
    tt = _tile(t, ROW_T)
    vi, gi = col0 // cc, col0 // cc + 1
    off = CONV_HALO - (DW_LEN - 1)

    def body(vc, gc, vp, gp, w_ref, b_ref, lg_ref, lb_ref, c_ref, y_ref, ubuf):
        i = pl.program_id(0)
        _glu_halo(vc, gc, vp, gp, ubuf, i, tt, CONV_HALO)
        for ch in range(cc // LANE):
            sl = pl.ds(ch * LANE, LANE)
            acc = jnp.zeros((tt, LANE), F32) + b_ref[:, sl]
            for tap in range(DW_LEN):
                acc = acc + w_ref[pl.ds(tap, 1), sl] * ubuf[pl.ds(off + tap, tt), sl]
            c_ref[:, sl] = acc
        c = c_ref[...]
        xc = c - jnp.mean(c, axis=-1, keepdims=True)
        ln = xc * lax.rsqrt(jnp.mean(xc * xc, axis=-1, keepdims=True) + EPS) * lg_ref[...] + lb_ref[...]
        y_ref[...] = ln * _sigmoid(ln)

    cur = lambda c: pl.BlockSpec((tt, cc), lambda i, c=c: (i, c))
    prev = lambda c: pl.BlockSpec((tt, cc), lambda i, c=c: (jnp.maximum(i - 1, 0), c))
    return pl.pallas_call(
        body, name="conv_fwd", grid=(t // tt,),
        in_specs=[cur(vi), cur(gi), prev(vi), prev(gi), pl.BlockSpec((DW_LEN, cc), lambda i: (0, 0)),
                  _vec_spec(cc), _vec_spec(cc), _vec_spec(cc)],
        out_specs=[_row_spec(tt, cc), _row_spec(tt, cc)],
        out_shape=[jax.ShapeDtypeStruct((t, cc), F32)] * 2,
        scratch_shapes=[pltpu.VMEM((CONV_HALO + tt, cc), F32)],
        compiler_params=_params(1),
    )(proj, proj, proj, proj, w, b, lng, lnb)


def conv_bwd(proj, col0, cc, dc, w):
    t = proj.shape[0]
    tt = _tile(t, ROW_T)
    nt = t // tt
    vi, gi = col0 // cc, col0 // cc + 1
    off = CONV_HALO - (DW_LEN - 1)

    def body(vc, gc, vp, gp, dcc, dcn, w_ref, dvg_ref, dw_ref, db_ref, ubuf, dbuf):
        i = pl.program_id(0)

        @pl.when(i == 0)
        def _():
            dw_ref[...] = jnp.zeros_like(dw_ref)
            db_ref[...] = jnp.zeros_like(db_ref)

        _glu_halo(vc, gc, vp, gp, ubuf, i, tt, CONV_HALO)
        dbuf[pl.ds(0, tt), :] = dcc[...]
        dbuf[pl.ds(tt, CONV_HALO), :] = jnp.where(i < nt - 1, dcn[pl.ds(0, CONV_HALO), :], 0.0)
        db_ref[...] += jnp.sum(dcc[...], axis=0, keepdims=True)
        for ch in range(cc // LANE):
            sl = pl.ds(ch * LANE, LANE)
            dcv = dbuf[pl.ds(0, tt), sl]
            du = jnp.zeros((tt, LANE), F32)
            for tap in range(DW_LEN):
                du = du + w_ref[pl.ds(tap, 1), sl] * dbuf[pl.ds(DW_LEN - 1 - tap, tt), sl]
                dw_ref[pl.ds(tap, 1), sl] += jnp.sum(dcv * ubuf[pl.ds(off + tap, tt), sl], axis=0, keepdims=True)
            s = _sigmoid(gc[:, sl])
            val = vc[:, sl]
            dvg_ref[:, sl] = (du * s).astype(BF16)
            dvg_ref[:, pl.ds(cc + ch * LANE, LANE)] = (du * val * s * (1.0 - s)).astype(BF16)

    cur = lambda c: pl.BlockSpec((tt, cc), lambda i, c=c: (i, c))
    prev = lambda c: pl.BlockSpec((tt, cc), lambda i, c=c: (jnp.maximum(i - 1, 0), c))
    return pl.pallas_call(
        body, name="conv_bwd", grid=(nt,),
        in_specs=[cur(vi), cur(gi), prev(vi), prev(gi), _row_spec(tt, cc),
                  pl.BlockSpec((tt, cc), lambda i: (jnp.minimum(i + 1, nt - 1), 0)),
                  pl.BlockSpec((DW_LEN, cc), lambda i: (0, 0))],
        out_specs=[_row_spec(tt, 2 * cc), pl.BlockSpec((DW_LEN, cc), lambda i: (0, 0)), _vec_spec(cc)],
        out_shape=[jax.ShapeDtypeStruct((t, 2 * cc), BF16), jax.ShapeDtypeStruct((DW_LEN, cc), F32),
                   jax.ShapeDtypeStruct((1, cc), F32)],
        scratch_shapes=[pltpu.VMEM((CONV_HALO + tt, cc), F32), pltpu.VMEM((tt + CONV_HALO, cc), F32)],
        compiler_params=_params(1),
    )(proj, proj, proj, proj, dc, dc, w)


def _lru_gates(xbuf, cw_ref, cb_ref, wa_ref, ba_ref, wi_ref, bi_ref, lam_ref, tt, wl):
    bd = wl // LRU_BLOCKS
    xr = jnp.zeros((tt, wl), F32) + cb_ref[...]
    for tap in range(LRU_LEN):
        xr = xr + cw_ref[pl.ds(tap, 1), :] * xbuf[pl.ds(LRU_HALO - (LRU_LEN - 1) + tap, tt), :]
    xb = xr.astype(BF16)
    ga = jnp.concatenate([_dot(xb[:, n * bd:(n + 1) * bd], wa_ref[n]) for n in range(LRU_BLOCKS)], axis=1) + ba_ref[...]
    gi = jnp.concatenate([_dot(xb[:, n * bd:(n + 1) * bd], wi_ref[n]) for n in range(LRU_BLOCKS)], axis=1) + bi_ref[...]
    r = _sigmoid(ga)
    ig = _sigmoid(gi)
    spl = _softplus(-lam_ref[...])
    log_a = -LRU_C * r * spl
    a = jnp.exp(log_a)
    m = jnp.sqrt(_neg_expm1(2.0 * log_a))
    return xr, xb, r, ig, spl, a, m


def _group_scan(a8, b8, reverse):
    rid = lax.broadcasted_iota(jnp.int32, a8.shape, 0)
    aa, bb = a8, b8
    for dist in (1, 2, 4):
        shift = SUBLANE - dist if reverse else dist
        a_sh = pltpu.roll(aa, shift, 0)
        b_sh = pltpu.roll(bb, shift, 0)
        valid = (rid < SUBLANE - dist) if reverse else (rid >= dist)
        bb = jnp.where(valid, aa * b_sh + bb, bb)
        aa = jnp.where(valid, aa * a_sh, aa)
    return aa, bb


def _pick_row(x8, r):
    rid = lax.broadcasted_iota(jnp.int32, x8.shape, 0)
    return jnp.sum(jnp.where(rid == r, x8, 0.0), axis=0, keepdims=True)


def lru_fwd(proj, col0, wl, cw, cb, wa, ba, wi, bi, lam):
    t = proj.shape[0]
    tt = _tile(t, ROW_T)
    xi, yi = col0 // wl, col0 // wl + 1

    def body(xc, xp, ry, cw_ref, cb_ref, wa_ref, ba_ref, wi_ref, bi_ref, lam_ref, hs_ref, y_ref,
             xbuf, a_s, b_s, hcar):
        i = pl.program_id(0)

        @pl.when(i == 0)
        def _():
            hcar[...] = jnp.zeros_like(hcar)

        xbuf[pl.ds(0, LRU_HALO), :] = jnp.where(i > 0, xp[pl.ds(tt - LRU_HALO, LRU_HALO), :], 0.0)
        xbuf[pl.ds(LRU_HALO, tt), :] = xc[...]
        xr, _, _, ig, _, a, m = _lru_gates(xbuf, cw_ref, cb_ref, wa_ref, ba_ref, wi_ref, bi_ref, lam_ref, tt, wl)
        a_s[...] = a
        b_s[...] = m * ig * xr

        def group(gidx, h):
            r0 = pl.multiple_of(gidx * SUBLANE, SUBLANE)
            aa, bb = _group_scan(a_s[pl.ds(r0, SUBLANE), :], b_s[pl.ds(r0, SUBLANE), :], False)
            h8 = aa * h + bb
            hs_ref[pl.ds(r0, SUBLANE), :] = h8
            return _pick_row(h8, SUBLANE - 1)

        hcar[...] = lax.fori_loop(0, tt // SUBLANE, group, hcar[...])
        gel, _ = _gelu_and_grad(ry[...])
        y_ref[...] = hs_ref[...] * gel

    cur = lambda c: pl.BlockSpec((tt, wl), lambda i, c=c: (i, c))
    full = lambda shape: pl.BlockSpec(shape, lambda i: (0,) * len(shape))
    return pl.pallas_call(
        body, name="lru_fwd", grid=(t // tt,),
        in_specs=[cur(xi), pl.BlockSpec((tt, wl), lambda i: (jnp.maximum(i - 1, 0), xi)), cur(yi),
                  full((LRU_LEN, wl)), _vec_spec(wl), full(wa.shape), _vec_spec(wl), full(wi.shape), _vec_spec(wl),
                  _vec_spec(wl)],
        out_specs=[_row_spec(tt, wl), _row_spec(tt, wl)],
        out_shape=[jax.ShapeDtypeStruct((t, wl), F32)] * 2,
        scratch_shapes=[pltpu.VMEM((LRU_HALO + tt, wl), F32), pltpu.VMEM((tt, wl), F32), pltpu.VMEM((tt, wl), F32),
                        pltpu.VMEM((1, wl), F32)],
        compiler_params=_params(1),
    )(proj, proj, proj, cw, cb, wa, ba, wi, bi, lam)


def lru_bwd(proj, col0, wl, hs, dy, cw, cb, wa, ba, wi, bi, lam):
    t = proj.shape[0]
    tt = _tile(t, ROW_T)
    nt = t // tt
    xi, yi = col0 // wl, col0 // wl + 1
    bd = wl // LRU_BLOCKS

    def body(xc, xp, ry, hc, hp, dy_ref, cw_ref, cb_ref, wa_ref, ba_ref, wi_ref, bi_ref, lam_ref,
             dxy_ref, dcw_ref, dcb_ref, dwa_ref, dba_ref, dwi_ref, dbi_ref, dlam_ref,
             xbuf, hbuf, abuf, e_s, dh_s, dxbuf, dhcar):
        i = pl.program_id(0)
        first = i == 0

        @pl.when(first)
        def _():
            for r in (dcw_ref, dcb_ref, dwa_ref, dba_ref, dwi_ref, dbi_ref, dlam_ref, dhcar):
                r[...] = jnp.zeros_like(r)
            abuf[pl.ds(tt, LRU_HALO), :] = jnp.zeros((LRU_HALO, wl), F32)
            dxbuf[pl.ds(tt, LRU_HALO), :] = jnp.zeros((LRU_HALO, wl), F32)

        has_prev = i < nt - 1
        xbuf[pl.ds(0, LRU_HALO), :] = jnp.where(has_prev, xp[pl.ds(tt - LRU_HALO, LRU_HALO), :], 0.0)
        xbuf[pl.ds(LRU_HALO, tt), :] = xc[...]
        hbuf[pl.ds(0, LRU_HALO), :] = jnp.where(has_prev, hp[pl.ds(tt - LRU_HALO, LRU_HALO), :], 0.0)
        hbuf[pl.ds(LRU_HALO, tt), :] = hc[...]
        xr, xb, r, ig, spl, a, m = _lru_gates(xbuf, cw_ref, cb_ref, wa_ref, ba_ref, wi_ref, bi_ref, lam_ref, tt, wl)
        gel, dgel = _gelu_and_grad(ry[...])
        dyv = dy_ref[...]
        e_s[...] = dyv * gel
        dxy_ref[:, pl.ds(wl, wl)] = (dyv * hc[...] * dgel).astype(BF16)
        abuf[pl.ds(0, tt), :] = a
        a_next = abuf[pl.ds(1, tt), :]
        dh_s[...] = a_next

        def group(it, dh_in):
            r0 = pl.multiple_of((tt // SUBLANE - 1 - it) * SUBLANE, SUBLANE)
            aa, bb = _group_scan(dh_s[pl.ds(r0, SUBLANE), :], e_s[pl.ds(r0, SUBLANE), :], True)
            dh8 = aa * dh_in + bb
            dh_s[pl.ds(r0, SUBLANE), :] = dh8
            return _pick_row(dh8, 0)

        dhcar[...] = lax.fori_loop(0, tt // SUBLANE, group, dhcar[...])
        abuf[pl.ds(tt, LRU_HALO), :] = a[0:LRU_HALO, :]
        dh = dh_s[...]
        h_m1 = hbuf[pl.ds(LRU_HALO - 1, tt), :]
        dlog_a = dh * h_m1 * a - dh * ig * xr * (a * a / m)
        dig = dh * m * xr
        dxr = dh * m * ig
        dga = dlog_a * (-LRU_C) * spl * r * (1.0 - r)
        dgi = dig * ig * (1.0 - ig)
        dlam_ref[...] += jnp.sum(dlog_a * r, axis=0, keepdims=True) * (LRU_C * _sigmoid(-lam_ref[...]))
        dba_ref[...] += jnp.sum(dga, axis=0, keepdims=True)
        dbi_ref[...] += jnp.sum(dgi, axis=0, keepdims=True)
        dgab = dga.astype(BF16)
        dgib = dgi.astype(BF16)
        back = []
        for n in range(LRU_BLOCKS):
            sl = slice(n * bd, (n + 1) * bd)
            dwa_ref[n] += _dot(xb[:, sl], dgab[:, sl], "tn")
            dwi_ref[n] += _dot(xb[:, sl], dgib[:, sl], "tn")
            back.append(_dot(dgab[:, sl], wa_ref[n], "nt") + _dot(dgib[:, sl], wi_ref[n], "nt"))
        dxr = dxr + jnp.concatenate(back, axis=1)
        dcb_ref[...] += jnp.sum(dxr, axis=0, keepdims=True)
        dxbuf[pl.ds(0, tt), :] = dxr
        drx = jnp.zeros((tt, wl), F32)
        for tap in range(LRU_LEN):
            drx = drx + cw_ref[pl.ds(tap, 1), :] * dxbuf[pl.ds(LRU_LEN - 1 - tap, tt), :]
            dcw_ref[pl.ds(tap, 1), :] += jnp.sum(
                dxr * xbuf[pl.ds(LRU_HALO - (LRU_LEN - 1) + tap, tt), :], axis=0, keepdims=True)
        dxbuf[pl.ds(tt, LRU_HALO), :] = dxr[0:LRU_HALO, :]
        dxy_ref[:, pl.ds(0, wl)] = drx.astype(BF16)

    rev = lambda c: pl.BlockSpec((tt, wl), lambda i, c=c: (nt - 1 - i, c))
    rev_prev = lambda c: pl.BlockSpec((tt, wl), lambda i, c=c: (jnp.maximum(nt - 2 - i, 0), c))
    full = lambda shape: pl.BlockSpec(shape, lambda i: (0,) * len(shape))
    vec = _vec_spec(wl)
    return pl.pallas_call(
        body, name="lru_bwd", grid=(nt,),
        in_specs=[rev(xi), rev_prev(xi), rev(yi), rev(0), rev_prev(0), rev(0),
                  full((LRU_LEN, wl)), vec, full(wa.shape), vec, full(wi.shape), vec, vec],
        out_specs=[pl.BlockSpec((tt, 2 * wl), lambda i: (nt - 1 - i, 0)), full((LRU_LEN, wl)), vec,
                   full(wa.shape), vec, full(wi.shape), vec, vec],
        out_shape=[jax.ShapeDtypeStruct((t, 2 * wl), BF16), jax.ShapeDtypeStruct((LRU_LEN, wl), F32),
                   jax.ShapeDtypeStruct((1, wl), F32), jax.ShapeDtypeStruct(wa.shape, F32),
                   jax.ShapeDtypeStruct((1, wl), F32), jax.ShapeDtypeStruct(wi.shape, F32),
                   jax.ShapeDtypeStruct((1, wl), F32), jax.ShapeDtypeStruct((1, wl), F32)],
        scratch_shapes=[pltpu.VMEM((LRU_HALO + tt, wl), F32), pltpu.VMEM((LRU_HALO + tt, wl), F32),
                        pltpu.VMEM((tt + LRU_HALO, wl), F32), pltpu.VMEM((tt, wl), F32), pltpu.VMEM((tt, wl), F32),
                        pltpu.VMEM((tt + LRU_HALO, wl), F32), pltpu.VMEM((1, wl), F32)],
        compiler_params=_params(1),
    )(proj, proj, proj, hs, hs, dy, cw, cb, wa, ba, wi, bi, lam)


def _adamw(w, g, m, v):
    m = ADAM_B1 * m + (1.0 - ADAM_B1) * g
    v = ADAM_B2 * v + (1.0 - ADAM_B2) * (g * g)
    m_hat = m / (1.0 - ADAM_B1 ** ADAM_STEP)
    v_hat = v / (1.0 - ADAM_B2 ** ADAM_STEP)
    delta = -ADAM_LR * (m_hat / (jnp.sqrt(v_hat) + ADAM_EPS) + ADAM_WD * w)
    return delta, m, v


def adam_big(name, w, m, v, parts, chip):
    n_layers, rows, cols = w.shape
    tr = _tile(rows, 128 if cols > 1024 else 256)
    nrt = rows // tr

    def body(chip_ref, *refs):
        w_ref, m_ref, v_ref = refs[:3]
        part_refs = refs[3:3 + 4 * n_layers]
        g_ref, d_ref, mo_ref, vo_ref = refs[3 + 4 * n_layers:]
        layer = pl.program_id(0)
        for l in range(n_layers):
            @pl.when(layer == l)
            def _(l=l):
                g = part_refs[4 * l][...].astype(F32)
                for p in range(1, 4):
                    g = g + part_refs[4 * l + p][...].astype(F32)
                delta, mn, vn = _adamw(w_ref[...], g, m_ref[...], v_ref[...])
                g_ref[...] = g
                d_ref[...] = delta
                mo_ref[...] = mn
                vo_ref[...] = vn

    wspec = pl.BlockSpec((None, tr, cols), lambda l, i, chip_ref: (l, i, 0))
    operands, in_specs = [w, m, v], [wspec, wspec, wspec]
    for l in range(n_layers):
        mine, recv = parts[l]
        operands.append(mine)
        in_specs.append(pl.BlockSpec(
            (None, tr, cols), lambda ll, i, chip_ref, l=l: (chip_ref[0], jnp.where(ll == l, i, 0), 0)))
        for p in range(3):
            operands.append(recv)
            in_specs.append(pl.BlockSpec(
                (None, tr, cols), lambda ll, i, chip_ref, l=l, p=p: (p, jnp.where(ll == l, i, 0), 0)))
    return pl.pallas_call(
        body, name=name,
        grid_spec=pltpu.PrefetchScalarGridSpec(
            num_scalar_prefetch=1, grid=(n_layers, nrt), in_specs=in_specs, out_specs=[wspec] * 4),
        out_shape=[jax.ShapeDtypeStruct(w.shape, F32)] * 4, compiler_params=_params(2),
    )(chip, *operands)


def adam_small(w, m, v, g):
    rows = w.shape[0]
    tr = _tile(rows, PACK_ROWS)

    def body(w_ref, m_ref, v_ref, g_ref, d_ref, mo_ref, vo_ref):
        delta, mn, vn = _adamw(w_ref[...], g_ref[...], m_ref[...], v_ref[...])
        d_ref[...] = delta
        mo_ref[...] = mn
        vo_ref[...] = vn

    spec = _row_spec(tr, LANE)
    return pl.pallas_call(
        body, name="adam_small", grid=(rows // tr,), in_specs=[spec] * 4, out_specs=[spec] * 3,
        out_shape=[jax.ShapeDtypeStruct(w.shape, F32)] * 3, compiler_params=_params(1),
    )(w, m, v, g)


def sum_parts(parts):
    _, rows, _ = parts.shape
    tr = _tile(rows, PACK_ROWS)

    def body(p_ref, o_ref):
        acc = p_ref[0]
        for k in range(1, N_DEV):
            acc = acc + p_ref[k]
        o_ref[...] = acc

    return pl.pallas_call(
        body, name="sum_parts", grid=(rows // tr,),
        in_specs=[pl.BlockSpec((N_DEV, tr, LANE), lambda i: (0, i, 0))], out_specs=_row_spec(tr, LANE),
        out_shape=jax.ShapeDtypeStruct((rows, LANE), F32), compiler_params=_params(1),
    )(parts)


def place_own(x, me, dtype):
    rows, cols = x.shape
    tr = _tile(rows, 256)

    def body(me_ref, x_ref, o_ref):
        o_ref[...] = x_ref[...].astype(dtype)

    return pl.pallas_call(
        body, name="place_own",
        grid_spec=pltpu.PrefetchScalarGridSpec(
            num_scalar_prefetch=1, grid=(rows // tr,),
            in_specs=[pl.BlockSpec((tr, cols), lambda i, me_ref: (i, 0))],
            out_specs=pl.BlockSpec((None, tr, cols), lambda i, me_ref: (me_ref[0], i, 0))),
        out_shape=jax.ShapeDtypeStruct((N_DEV, rows, cols), dtype), compiler_params=_params(1),
    )(me, x)


_HBM = pl.BlockSpec(memory_space=pltpu.HBM)


def _place():
    return lax.axis_index("x"), lax.axis_index("y"), lax.axis_index("c")


def _other_chips(x, y):
    return [(1 - x, y), (x, 1 - y), (1 - x, 1 - y)]


def all_gather(name, shard, me, dtype=None):
    def body(buf_ref, out_ref, send_sems, recv_sems):
        del buf_ref
        x, y, c = _place()
        mine, sibling = (x, y, c), (x, y, 1 - c)
        chips = _other_chips(x, y)

        def copy(k, block, to):
            slot = out_ref.at[4 * block[0] + 2 * block[1] + block[2]]
            return pltpu.make_async_remote_copy(
                src_ref=slot, dst_ref=slot, send_sem=send_sems.at[k], recv_sem=recv_sems.at[k],
                device_id=to, device_id_type=pl.DeviceIdType.MESH)

        first = [copy(0, mine, sibling)] + [copy(1 + j, mine, (*chip, c)) for j, chip in enumerate(chips)]
        for cp in first:
            cp.start()
        passed = [copy(4 + j, (*chip, c), sibling) for j, chip in enumerate(chips)]
        for j, chip in enumerate(chips):
            copy(1 + j, (*chip, c), mine).wait_recv()
            passed[j].start()
        copy(0, sibling, mine).wait_recv()
        for j, chip in enumerate(chips):
            copy(4 + j, (*chip, 1 - c), mine).wait_recv()
        for cp in first + passed:
            cp.wait_send()

    buf = place_own(shard, me, dtype or shard.dtype)
    return pl.pallas_call(
        body, name=name, out_shape=jax.ShapeDtypeStruct(buf.shape, buf.dtype),
        in_specs=[_HBM], out_specs=_HBM, input_output_aliases={0: 0},
        scratch_shapes=[pltpu.SemaphoreType.DMA((7,)), pltpu.SemaphoreType.DMA((7,))],
    )(buf)


def _own_block_copies(src_refs, dst_refs, send_sems, recv_sems, arrivals):
    x, y, c = _place()
    peers = [(x, y, 1 - c)] + [(*chip, c) for chip in _other_chips(x, y)]
    copies = []
    for b, (src, dst) in enumerate(zip(src_refs, dst_refs)):
        for k, peer in enumerate(peers):
            def copy(landing, b=b, k=k, peer=peer, src=src, dst=dst):
                return pltpu.make_async_remote_copy(
                    src_ref=src.at[4 * x + 2 * y + c], dst_ref=dst.at[landing],
                    send_sem=send_sems.at[4 * b + k], recv_sem=recv_sems.at[4 * b + k],
                    device_id=peer, device_id_type=pl.DeviceIdType.MESH)
            copies.append((copy(4 * x + 2 * y + c), copy(4 * peer[0] + 2 * peer[1] + peer[2]) if arrivals else None))
    return copies


def gather_start(name, bufs, after):
    n = len(bufs)

    def body(*refs):
        send_sems, recv_sems = refs[n + 1], refs[n + 2]
        thru = refs[n + 3:2 * n + 3]
        for send, _ in _own_block_copies(thru, thru, send_sems, recv_sems, False):
            send.start()
        refs[2 * n + 3][...] = jnp.zeros((SUBLANE, LANE), F32)

    return pl.pallas_call(
        body, name=name,
        out_shape=(pltpu.SemaphoreType.DMA((4 * n,)), pltpu.SemaphoreType.DMA((4 * n,)),
                   *[pltpu.HBM(b.shape, b.dtype) for b in bufs], jax.ShapeDtypeStruct((SUBLANE, LANE), F32)),
        in_specs=(*(_HBM,) * n, _ANY), out_specs=(_SEM, _SEM, *(_HBM,) * n, _TOKEN),
        input_output_aliases={b: 2 + b for b in range(n)},
        compiler_params=pltpu.CompilerParams(has_side_effects=_EFFECT),
    )(*[_hbm(b) for b in bufs], after)


def gather_wait(name, state, first, count, after):
    send_sems, recv_sems = state[:2]
    bufs = state[2 + first:2 + first + count]
    n = len(bufs)

    def body(*refs):
        ins = refs[:n]
        send_sems, recv_sems = refs[n], refs[n + 1]
        shift = 4 * first
        for send, arrival in _own_block_copies(
                ins, ins, send_sems.at[pl.ds(shift, 4 * n)], recv_sems.at[pl.ds(shift, 4 * n)], True):
            send.wait_send()
            arrival.wait_recv()

    return pl.pallas_call(
        body, name=name, out_shape=tuple(pltpu.HBM(b.shape, b.dtype) for b in bufs),
        in_specs=(*(_HBM,) * n, _SEM, _SEM, _ANY), out_specs=(_HBM,) * n,
        input_output_aliases={b: b for b in range(n)},
        compiler_params=pltpu.CompilerParams(has_side_effects=_EFFECT),
    )(*bufs, send_sems, recv_sems, after)


def gather_finish(name, bufs):
    n = len(bufs)

    def body(*refs):
        outs = refs[n:2 * n]
        send_sems, recv_sems = refs[2 * n], refs[2 * n + 1]
        x, y, c = _place()
        copies = []
        for b, out in enumerate(outs):
            for k, chip in enumerate(_other_chips(x, y)):
                sem = 3 * b + k
                copies.append((
                    pltpu.make_async_remote_copy(
                        src_ref=out.at[4 * chip[0] + 2 * chip[1] + c], dst_ref=out.at[4 * chip[0] + 2 * chip[1] + c],
                        send_sem=send_sems.at[sem], recv_sem=recv_sems.at[sem],
                        device_id=(x, y, 1 - c), device_id_type=pl.DeviceIdType.MESH),
                    pltpu.make_async_remote_copy(
                        src_ref=out.at[4 * chip[0] + 2 * chip[1] + c], dst_ref=out.at[4 * chip[0] + 2 * chip[1] + 1 - c],
                        send_sem=send_sems.at[sem], recv_sem=recv_sems.at[sem],
                        device_id=(x, y, 1 - c), device_id_type=pl.DeviceIdType.MESH)))
        for send, _ in copies:
            send.start()
        for send, arrival in copies:
            send.wait_send()
            arrival.wait_recv()

    return pl.pallas_call(
        body, name=name, out_shape=tuple(jax.ShapeDtypeStruct(b.shape, b.dtype) for b in bufs),
        in_specs=(_HBM,) * n, out_specs=(_HBM,) * n, input_output_aliases={b: b for b in range(n)},
        scratch_shapes=[pltpu.SemaphoreType.DMA((3 * n,)), pltpu.SemaphoreType.DMA((3 * n,))],
    )(*bufs)


_SEM =pl.BlockSpec(memory_space=pltpu.SEMAPHORE)
_ANY = pl.BlockSpec(memory_space=pl.ANY)
_TOKEN = pl.BlockSpec(memory_space=pltpu.VMEM)
_EFFECT = pltpu.SideEffectType.DATAFLOW_SIDE_EFFECTING


def _hbm(a):
    return pltpu.with_memory_space_constraint(a, pltpu.HBM)


def _chip_copies(p_ref, land_ref, send_sems, recv_sems):
    x, y, c = _place()
    return [pltpu.make_async_remote_copy(
        src_ref=p_ref.at[2 * px + py], dst_ref=land_ref.at[k], send_sem=send_sems.at[k], recv_sem=recv_sems.at[k],
        device_id=(px, py, c), device_id_type=pl.DeviceIdType.MESH) for k, (px, py) in enumerate(_other_chips(x, y))]


def scatter_chips_start(name, p):
    _, rows, cols = p.shape

    def body(p_ref, land_ref, send_sems, recv_sems, p_thru, land_thru, token):
        for cp in _chip_copies(p_ref, land_ref, send_sems, recv_sems):
            cp.start()
        token[...] = jnp.zeros_like(token)

    return pl.pallas_call(
        body, name=name,
        out_shape=(pltpu.SemaphoreType.DMA((3,)), pltpu.SemaphoreType.DMA((3,)), pltpu.HBM(p.shape, p.dtype),
                   pltpu.HBM((3, rows, cols), p.dtype), jax.ShapeDtypeStruct((SUBLANE, LANE), F32)),
        in_specs=(_HBM, _HBM), out_specs=(_SEM, _SEM, _HBM, _HBM, _TOKEN), input_output_aliases={0: 2, 1: 3},
        compiler_params=pltpu.CompilerParams(has_side_effects=_EFFECT),
    )(_hbm(p), _hbm(lax.empty((3, rows, cols), p.dtype)))


def scatter_chips_wait(name, send_sems, recv_sems, p_thru, land_thru, after):
    def body(p_ref, land_ref, send_sems, recv_sems, after_ref, p_out, land_out):
        for cp in _chip_copies(p_ref, land_ref, send_sems, recv_sems):
            cp.wait_send()
            cp.wait_recv()

    return pl.pallas_call(
        body, name=name,
        out_shape=(pltpu.HBM(p_thru.shape, p_thru.dtype), pltpu.HBM(land_thru.shape, land_thru.dtype)),
        in_specs=(_HBM, _HBM, _SEM, _SEM, _ANY), out_specs=(_HBM, _HBM), input_output_aliases={0: 0, 1: 1},
        compiler_params=pltpu.CompilerParams(has_side_effects=_EFFECT),
    )(p_thru, land_thru, send_sems, recv_sems, after)


def _pair_copies(g_ref, land_ref, send_sems, recv_sems):
    x, y, c = _place()
    return [pltpu.make_async_remote_copy(
        src_ref=g_ref.at[k], dst_ref=land_ref.at[k], send_sem=send_sems.at[k], recv_sem=recv_sems.at[k],
        device_id=(x, y, 1 - c), device_id_type=pl.DeviceIdType.MESH) for k in range(N_DEV // 2)]


def pair_start(name, g, after, carry=None):
    n = g.shape[0]

    def body(g_ref, land_ref, after_ref, *rest):
        send_sems, recv_sems = rest[-5 if carry is None else -6:][:2]
        for cp in _pair_copies(g_ref, land_ref, send_sems, recv_sems):
            cp.start()
        token = rest[-1 if carry is None else -2]
        token[...] = jnp.zeros_like(token)

    extra = () if carry is None else (carry,)
    out = pl.pallas_call(
        body, name=name,
        out_shape=(pltpu.SemaphoreType.DMA((n,)), pltpu.SemaphoreType.DMA((n,)), pltpu.HBM(g.shape, g.dtype),
                   pltpu.HBM(g.shape, g.dtype), jax.ShapeDtypeStruct((SUBLANE, LANE), F32),
                   *[pltpu.HBM(c.shape, c.dtype) for c in extra]),
        in_specs=(_HBM, _HBM, _ANY, *(_HBM,) * len(extra)), out_specs=(_SEM, _SEM, _HBM, _HBM, _TOKEN, *(_HBM,) * len(extra)),
        input_output_aliases={0: 2, 1: 3, **({3: 5} if extra else {})},
        compiler_params=pltpu.CompilerParams(has_side_effects=_EFFECT),
    )(_hbm(g), _hbm(lax.empty(g.shape, g.dtype)), after, *[_hbm(c) for c in extra])
    return out if carry is None else (out[:5], out[5])


def pair_wait(name, state, after):
    send_sems, recv_sems, g_thru, land_thru, _ = state

    def body(g_ref, land_ref, send_sems, recv_sems, after_ref, g_out, land_out):
        for cp in _pair_copies(g_ref, land_ref, send_sems, recv_sems):
            cp.wait_send()
            cp.wait_recv()

    return pl.pallas_call(
        body, name=name,
        out_shape=(pltpu.HBM(g_thru.shape, g_thru.dtype), pltpu.HBM(land_thru.shape, land_thru.dtype)),
        in_specs=(_HBM, _HBM, _SEM, _SEM, _ANY), out_specs=(_HBM, _HBM), input_output_aliases={0: 0, 1: 1},
        compiler_params=pltpu.CompilerParams(has_side_effects=_EFFECT),
    )(g_thru, land_thru, send_sems, recv_sems, after)[1]


def reduce_scatter_wait(tag, state, after):
    send_sems, recv_sems, p_thru, land_thru, _ = state
    return scatter_chips_wait("rs_wait_" + tag, send_sems, recv_sems, p_thru, land_thru, after)


_SMALL = ("g_pre_mix", "g_post_mix", "g_pre_ffn", "g_post_ffn", "g_attn_grp", "g_conv_grp", "g_lru_grp",
          "dw_conv_w", "dw_conv_b", "conv_ln_g", "conv_ln_b", "lru_conv_w", "lru_conv_b",
          "lru_w_a", "lru_b_a", "lru_w_i", "lru_b_i", "lru_lambda")
_COL_SHARDED_SMALL = ("dw_conv_w", "lru_conv_w")
_BIG = ("w_in", "w_out", "w_gate", "w_up", "w_down")
_TRANSPOSED = ("w_gate", "w_up")
_ALL = ("w_in", "w_out", "g_pre_mix", "g_post_mix", "g_pre_ffn", "g_post_ffn", "g_attn_grp", "g_conv_grp", "g_lru_grp",
        "dw_conv_w", "dw_conv_b", "conv_ln_g", "conv_ln_b", "lru_conv_w", "lru_conv_b", "lru_w_a", "lru_b_a",
        "lru_w_i", "lru_b_i", "lru_lambda", "w_gate", "w_up", "w_down")


def _pack(arrays):
    flat = jnp.concatenate([a.reshape(-1) for a in arrays])
    pad = (-flat.shape[0]) % (PACK_ROWS * LANE)
    return jnp.pad(flat, (0, pad)).reshape(-1, LANE)


def _unpack(packed, shapes):
    flat = packed.reshape(-1)
    out, pos = [], 0
    for s in shapes:
        n = math.prod(s)
        out.append(flat[pos:pos + n].reshape(s))
        pos += n
    return out


def kernel(x, w_in, w_out, g_pre_mix, g_post_mix, g_pre_ffn, g_post_ffn, g_attn_grp, g_conv_grp, g_lru_grp, dw_conv_w, dw_conv_b, conv_ln_g, conv_ln_b, lru_conv_w, lru_conv_b, lru_w_a, lru_b_a, lru_w_i, lru_b_i, lru_lambda, w_gate, w_up, w_down, loss_target, m_w_in, m_w_out, m_g_pre_mix, m_g_post_mix, m_g_pre_ffn, m_g_post_ffn, m_g_attn_grp, m_g_conv_grp, m_g_lru_grp, m_dw_conv_w, m_dw_conv_b, m_conv_ln_g, m_conv_ln_b, m_lru_conv_w, m_lru_conv_b, m_lru_w_a, m_lru_b_a, m_lru_w_i, m_lru_b_i, m_lru_lambda, m_w_gate, m_w_up, m_w_down, v_w_in, v_w_out, v_g_pre_mix, v_g_post_mix, v_g_pre_ffn, v_g_post_ffn, v_g_attn_grp, v_g_conv_grp, v_g_lru_grp, v_dw_conv_w, v_dw_conv_b, v_conv_ln_g, v_conv_ln_b, v_lru_conv_w, v_lru_conv_b, v_lru_w_a, v_lru_b_a, v_lru_w_i, v_lru_b_i, v_lru_lambda, v_w_gate, v_w_up, v_w_down):
    env = dict(locals())
    wts = {n: env[n] for n in _ALL}
    mom = {n: env["m_" + n] for n in _ALL}
    var = {n: env["v_" + n] for n in _ALL}
    for group in (wts, mom, var):
        for n in _TRANSPOSED:
            group[n] = jnp.swapaxes(group[n], 1, 2)

    depth = w_in.shape[0]
    h = x[0]
    target = loss_target[0]
    t, d = h.shape
    attn_w = d // 2
    n_heads = attn_w // HEAD_DIM
    cc = d // 4
    wl = d // 4
    conv_col, lru_col = 3 * attn_w, 3 * attn_w + 2 * cc
    me = 4 * lax.axis_index("x") + 2 * lax.axis_index("y") + lax.axis_index("c")
    me_s = me.astype(jnp.int32).reshape(1)
    chip_s = (2 * lax.axis_index("x") + lax.axis_index("y")).astype(jnp.int32).reshape(1)
    core_s = lax.axis_index("c").astype(jnp.int32).reshape(1)

    n_taps = DW_LEN + LRU_LEN
    taps = jnp.concatenate([dw_conv_w, lru_conv_w], axis=1).reshape(depth * n_taps, cc // N_DEV)
    taps = all_gather("ag_taps", taps, me_s)
    taps = jnp.moveaxis(taps.reshape(N_DEV, depth, n_taps, cc // N_DEV), 0, 2).reshape(depth, n_taps, cc)
    dw_full, lcw_full = taps[:, :DW_LEN], taps[:, DW_LEN:]

    def vec(a, l):
        return a[l].reshape(1, -1)

    ag_state, started = [], taps
    for l in range(depth):
        ag_state.append(gather_start(f"ag_start_{l}", [place_own(wts[n][l], me_s, BF16) for n in _BIG], started))
        started = ag_state[l][-1]
    started = started[0:1, 0:1]

    saved = []
    u1 = rms_pre(h, vec(g_pre_mix, 0) + started)
    loss_sum = dh = dbr = None
    for l in range(depth):
        wg = {}

        def take(first, count, behind, l=l, wg=wg):
            if l < EXPOSED_GATHERS or first == 0:
                if l >= EXPOSED_GATHERS:
                    count = len(_BIG)
                got = gather_wait(f"ag_wait_{l}_{first}", ag_state[l], first, count, behind)
                wg.update(zip(_BIG[first:first + count], gather_finish(f"ag_finish_{count}", got)))

        take(0, 1, u1)
        wa_b, wi_b = lru_w_a[l].astype(BF16), lru_w_i[l].astype(BF16)
        proj = mm_proj(u1, wg["w_in"])
        y_attn = attn_fwd(proj, n_heads)
        cpre, y_conv = conv_fwd(proj, conv_col, cc, dw_full[l], vec(dw_conv_b, l), vec(conv_ln_g, l), vec(conv_ln_b, l))
        hs, y_lru = lru_fwd(proj, lru_col, wl, lcw_full[l], vec(lru_conv_b, l), wa_b, vec(lru_b_a, l), wi_b,
                            vec(lru_b_i, l), vec(lru_lambda, l))
        mixed = mix_fwd(y_attn, y_conv, y_lru, vec(g_attn_grp, l), vec(g_conv_grp, l), vec(g_lru_grp, l))
        take(1, 1, mixed)
        wg["w_out"] = wg["w_out"].reshape(attn_w + cc + wl, d)
        o = mm_plain("mm_out", mixed, wg["w_out"], "nn", F32)
        h2, u2 = res_norm(h, o, vec(g_post_mix, l), vec(g_pre_ffn, l))
        take(2, 2, u2)
        gt, up, f = ffn_up(u2, wg["w_gate"], wg["w_up"])
        take(4, 1, f)
        dn = mm_down(f, wg["w_down"])
        saved.append(dict(wg=wg, wa_b=wa_b, wi_b=wi_b, h=h, u1=u1, proj=proj, y_attn=y_attn, cpre=cpre, y_conv=y_conv,
                          hs=hs, y_lru=y_lru, mixed=mixed, o=o, h2=h2, u2=u2, gt=gt, up=up, f=f, dn=dn))
        if l + 1 < depth:
            h, u1 = res_norm(h2, dn, vec(g_post_ffn, l), vec(g_pre_mix, l + 1))
        else:
            loss_sum, dh, dbr, dg_post_ffn = final_loss(h2, dn, vec(g_post_ffn, l), target)

    loss = lax.psum(0.5 * loss_sum[0, 0] / d, MESH_AXES)

    small = {n: [None] * depth for n in _SMALL}
    rs_state = {n: [None] * depth for n in _BIG}
    after = dbr
    for l in reversed(range(depth)):
        s = saved[l]
        wg = s["wg"]
        small["g_post_ffn"][l] = dg_post_ffn
        dgt, dup = ffn_bwd(dbr, wg["w_down"], s["gt"], s["up"])
        ffn_grads = (("w_down", "down", s["f"], dbr), ("w_gate", "gate", dgt, s["u2"]), ("w_up", "up", dup, s["u2"]))
        pairs = {}
        for n, tag, a, g in ffn_grads:
            theirs = mm_dw_half("mm_dw_" + tag, "rows", a, g, core_s, True)
            pairs[n] = pair_start(f"pair_start_{tag}_{l}", theirs, after)
            after = pairs[n][-1]
        for n, tag, a, g in ffn_grads:
            recv = pair_wait(f"pair_wait_{tag}_{l}", pairs[n], after)
            after = mm_dw_half("mm_dw_" + tag, "rows", a, g, core_s, False, add=recv)
            rs_state[n][l] = scatter_chips_start(f"rs_start_{tag}_{l}", after)
        du2 = mm_dx_ffn(dgt, wg["w_gate"], dup, wg["w_up"])
        started = sum(rs_state[n][l][-1][0:1, 0:1] for n in ("w_down", "w_gate", "w_up"))
        dh2, small["g_pre_ffn"][l], do, small["g_post_mix"][l] = norm_bwd(
            dh, du2, s["h2"], vec(g_pre_ffn, l) + started, (s["o"], vec(g_post_mix, l)))
        theirs = mm_dw_half("mm_dw_out", "take", s["mixed"], do, core_s, True)
        pairs["w_out"], do = pair_start(f"pair_start_out_{l}", theirs, rs_state["w_up"][l][-1], carry=do)
        dmixed = mm_plain("mm_dmixed", do, wg["w_out"], "nt", F32)
        recv = pair_wait(f"pair_wait_out_{l}", pairs["w_out"], dmixed)
        rs_state["w_out"][l] = scatter_chips_start(
            f"rs_start_out_{l}", mm_dw_half("mm_dw_out", "take", s["mixed"], do, core_s, False, add=recv))
        (dya, dc, dyl, small["g_attn_grp"][l], small["g_conv_grp"][l], small["g_lru_grp"][l],
         small["conv_ln_g"][l], small["conv_ln_b"][l]) = mix_bwd(
            dmixed, s["y_attn"], s["y_conv"], s["y_lru"], s["cpre"],
            vec(g_attn_grp, l) + rs_state["w_out"][l][-1][0:1, 0:1], vec(g_conv_grp, l),
            vec(g_lru_grp, l), vec(conv_ln_g, l), vec(conv_ln_b, l))
        dq, dk, dv = attn_bwd(s["proj"], dya, n_heads)
        dvg, small["dw_conv_w"][l], small["dw_conv_b"][l] = conv_bwd(s["proj"], conv_col, cc, dc, dw_full[l])
        (dxy, small["lru_conv_w"][l], small["lru_conv_b"][l], small["lru_w_a"][l], small["lru_b_a"][l],
         small["lru_w_i"][l], small["lru_b_i"][l], small["lru_lambda"][l]) = lru_bwd(
            s["proj"], lru_col, wl, s["hs"], dyl, lcw_full[l], vec(lru_conv_b, l), s["wa_b"], vec(lru_b_a, l),
            s["wi_b"], vec(lru_b_i, l), vec(lru_lambda, l))
        dproj = jnp.concatenate([dq, dk, dv, dvg, dxy], axis=1)
        theirs = mm_dw_half("mm_dw_in", "cols", s["u1"], dproj, core_s, True)
        pairs["w_in"], dproj = pair_start(f"pair_start_in_{l}", theirs, rs_state["w_out"][l][-1], carry=dproj)
        du1 = mm_dx_cols("mm_dx_in", dproj, wg["w_in"])
        recv = pair_wait(f"pair_wait_in_{l}", pairs["w_in"], du1)
        after = mm_dw_half("mm_dw_in", "cols", s["u1"], dproj, core_s, False, add=recv)
        rs_state["w_in"][l] = scatter_chips_start(f"rs_start_in_{l}", after)
        g_pre = vec(g_pre_mix, l) + rs_state["w_in"][l][-1][0:1, 0:1]
        if l > 0:
            p = saved[l - 1]
            dh, small["g_pre_mix"][l], dbr, dg_post_ffn = norm_bwd(
                dh2, du1, s["h"], g_pre, (p["dn"], vec(g_post_ffn, l - 1)))
        else:
            dh, small["g_pre_mix"][l] = norm_bwd(dh2, du1, s["h"], g_pre)
    grad_x = dh[None]

    small_shapes = [(depth,) + tuple(wts[n].shape[1:]) if n not in _COL_SHARDED_SMALL
                    else (depth, wts[n].shape[1], cc) for n in _SMALL]
    part = _pack([a for n in _SMALL for a in small[n]])
    small_state = gather_start("ag_small_start", [place_own(part, me_s, F32)], dh)

    grads, delta, new_m, new_v = {}, {}, {}, {}
    behind = small_state[-1]
    for n in reversed(_BIG):
        parts = [reduce_scatter_wait(f"{n[2:]}_{l}", rs_state[n][l], behind) for l in range(depth)]
        shape = wts[n].shape
        _, rows, cols = parts[0][0].shape
        view = (depth, rows, cols)
        g, dl, mn, vn = adam_big("adam_" + n, wts[n].reshape(view), mom[n].reshape(view), var[n].reshape(view),
                                 parts, chip_s)
        grads[n], delta[n], new_m[n], new_v[n] = (a.reshape(shape) for a in (g, dl, mn, vn))
        behind = g
    for group in (grads, delta, new_m, new_v):
        for n in _TRANSPOSED:
            group[n] = jnp.swapaxes(group[n], 1, 2)

    gathered, = gather_finish("ag_small_finish", gather_wait("ag_small_wait", small_state, 0, 1, behind))
    g_small = _unpack(sum_parts(gathered), small_shapes)
    for n, g in zip(_SMALL, g_small):
        if n in _COL_SHARDED_SMALL:
            g = lax.dynamic_slice_in_dim(g, me * (cc // N_DEV), cc // N_DEV, axis=2)
        grads[n] = g
    local_shapes = [tuple(wts[n].shape) for n in _SMALL]
    d_small, m_small, v_small = adam_small(
        _pack([wts[n] for n in _SMALL]), _pack([mom[n] for n in _SMALL]), _pack([var[n] for n in _SMALL]),
        _pack([grads[n] for n in _SMALL]))
    delta.update(zip(_SMALL, _unpack(d_small, local_shapes)))
    new_m.update(zip(_SMALL, _unpack(m_small, local_shapes)))
    new_v.update(zip(_SMALL, _unpack(v_small, local_shapes)))

    return (loss, grad_x, *[grads[n] for n in _ALL], *[delta[n] for n in _ALL],
            *[new_m[n] for n in _ALL], *[new_v[n] for n in _ALL])
```

```python
import functools
import math

import jax
import jax.numpy as jnp
from jax import lax
from jax.experimental import pallas as pl
from jax.experimental.pallas import tpu as pltpu

F32 = jnp.float32
BF16 = jnp.bfloat16

N_DEV = 8
EPS = 1e-6
HEAD_DIM = 128
DW_LEN = 31
LRU_LEN = 4
LRU_BLOCKS = 4
LRU_C = 8.0
ATT_TQ = 512
ATT_TK = 512
ATT_SUM = 256
ATT_HEADS = 2
ROW_T = 256
CONV_HALO = 32
LRU_HALO = 8
LANE = 128
SUBLANE = 8
WIDE_TILE = 2048
PACK_ROWS = 512
VMEM_LIMIT = 56 * 1024 * 1024

ADAM_LR = 0.001
ADAM_B1 = 0.9
ADAM_B2 = 0.999
ADAM_EPS = 1e-08
ADAM_WD = 0.01
ADAM_STEP = 10

MESH_AXES = ("x", "y", "c")
_DIMS = {
    "nn": (((1,), (0,)), ((), ())),
    "nt": (((1,), (1,)), ((), ())),
    "tn": (((0,), (0,)), ((), ())),
}


def _params(n_axes):
    return pltpu.CompilerParams(
        dimension_semantics=("arbitrary",) * n_axes, vmem_limit_bytes=VMEM_LIMIT)


def _dot(a, b, mode="nn"):
    return lax.dot_general(a, b, _DIMS[mode], preferred_element_type=F32)


def _sigmoid(x):
    return 1.0 / (1.0 + jnp.exp(-x))


def _softplus(x):
    return jnp.maximum(x, 0.0) + jnp.log(1.0 + jnp.exp(-jnp.abs(x)))


def _neg_expm1(x):
    series = x * (1.0 + x * (0.5 + x * (1.0 / 6 + x * (1.0 / 24 + x * (1.0 / 120 + x * (1.0 / 720))))))
    return jnp.where(x > -0.25, -series, 1.0 - jnp.exp(x))


_GELU_C = math.sqrt(2.0 / math.pi)


def _gelu_and_grad(x):
    inner = _GELU_C * (x + 0.044715 * x * x * x)
    t = jnp.tanh(inner)
    val = 0.5 * x * (1.0 + t)
    grad = 0.5 * (1.0 + t) + 0.5 * x * (1.0 - t * t) * _GELU_C * (1.0 + 3 * 0.044715 * x * x)
    return val, grad


def _rms_stats(x):
    r = lax.rsqrt(jnp.mean(x * x, axis=-1, keepdims=True) + EPS)
    return x * r, r


def _rms_bwd(dy, x, g):
    xn, r = _rms_stats(x)
    dxn = dy * g
    dx = r * (dxn - xn * jnp.mean(dxn * xn, axis=-1, keepdims=True))
    return dx, jnp.sum(dy * xn, axis=0, keepdims=True)


def _row_spec(tr, width, col=0):
    return pl.BlockSpec((tr, width), lambda i, col=col: (i, col))


def _vec_spec(width):
    return pl.BlockSpec((1, width), lambda i: (0, 0))


def _matmul(name, mode, operands, in_specs, out_shape, out_spec, grid):
    npairs = len(operands) // 2
    nk = grid[2]
    assert nk == 1 or out_shape.dtype == F32

    def body(*refs):
        o_ref = refs[2 * npairs]

        def partial():
            acc = None
            for p in range(npairs):
                d = _dot(refs[2 * p][...], refs[2 * p + 1][...], mode)
                acc = d if acc is None else acc + d
            return acc

        if nk == 1:
            o_ref[...] = partial().astype(o_ref.dtype)
        else:
            k = pl.program_id(2)

            @pl.when(k == 0)
            def _():
                o_ref[...] = partial()

            @pl.when(k > 0)
            def _():
                o_ref[...] += partial()

    return pl.pallas_call(
        body, name=name, grid=grid, in_specs=in_specs, out_specs=out_spec, out_shape=out_shape,
        compiler_params=_params(3),
    )(*operands)


def _tile(n, t):
    if n <= t:
        return n
    return max(k for k in range(SUBLANE, t + 1, SUBLANE) if n % k == 0)


def mm_proj(u, w):
    t, d = u.shape
    nblk, _, nb = w.shape
    tm = _tile(t, WIDE_TILE)
    return _matmul(
        "mm_proj", "nn", (u, w),
        [pl.BlockSpec((tm, d), lambda j, i, k: (i, 0)), pl.BlockSpec((None, d, nb), lambda j, i, k: (j, 0, 0))],
        jax.ShapeDtypeStruct((t, nblk * nb), F32), pl.BlockSpec((tm, nb), lambda j, i, k: (i, j)),
        (nblk, t // tm, 1))


def mm_plain(name, a, b, mode, out_dtype):
    if mode == "nn":
        (m, kk), n = a.shape, b.shape[1]
    elif mode == "nt":
        (m, kk), n = a.shape, b.shape[0]
    else:
        (kk, m), n = a.shape, b.shape[1]
    tm, tn = _tile(m, 1024), _tile(n, 1024)
    a_spec = (pl.BlockSpec((kk, tm), lambda i, j, k: (0, i)) if mode == "tn"
              else pl.BlockSpec((tm, kk), lambda i, j, k: (i, 0)))
    b_spec = (pl.BlockSpec((tn, kk), lambda i, j, k: (j, 0)) if mode == "nt"
              else pl.BlockSpec((kk, tn), lambda i, j, k: (0, j)))
    return _matmul(
        name, mode, (a, b), [a_spec, b_spec],
        jax.ShapeDtypeStruct((m, n), out_dtype), pl.BlockSpec((tm, tn), lambda i, j, k: (i, j)),
        (m // tm, n // tn, 1))


def mm_down(f, w):
    nblk, t, fb = f.shape
    d = w.shape[2]
    tm, tn = _tile(t, 1024), _tile(d, WIDE_TILE)
    return _matmul(
        "mm_down", "nn", (f, w),
        [pl.BlockSpec((None, tm, fb), lambda i, j, k: (k, i, 0)), pl.BlockSpec((None, fb, tn), lambda i, j, k: (k, 0, j))],
        jax.ShapeDtypeStruct((t, d), F32), pl.BlockSpec((tm, tn), lambda i, j, k: (i, j)),
        (t // tm, d // tn, nblk))


def mm_dw_half(name, kind, a, g, core, of_sibling, add=None):
    half = N_DEV // 2
    t = g.shape[0]

    def pick(k, core_ref):
        s = 1 - core_ref[0] if of_sibling else core_ref[0]
        return 2 * k + s

    if kind == "rows":
        rows, cols = a.shape[2], g.shape[1]
        tr, tc = rows, _tile(cols, WIDE_TILE)
        a_spec = pl.BlockSpec((None, t, rows), lambda k, i, n, core_ref: (pick(k, core_ref), 0, 0))
        g_spec = pl.BlockSpec((t, tc), lambda k, i, n, core_ref: (0, n))
    elif kind == "cols":
        rows, cols = a.shape[1], g.shape[1] // N_DEV
        tr, tc = _tile(rows, WIDE_TILE), cols
        a_spec = pl.BlockSpec((t, tr), lambda k, i, n, core_ref: (0, i))
        g_spec = pl.BlockSpec((t, cols), lambda k, i, n, core_ref: (0, pick(k, core_ref)))
    else:
        rows, cols = a.shape[1] // N_DEV, g.shape[1]
        tr, tc = rows, _tile(cols, WIDE_TILE)
        a_spec = pl.BlockSpec((t, rows), lambda k, i, n, core_ref: (0, pick(k, core_ref)))
        g_spec = pl.BlockSpec((t, tc), lambda k, i, n, core_ref: (0, n))
    o_spec = pl.BlockSpec((None, tr, tc), lambda k, i, n, core_ref: (k, i, n))

    def body(core_ref, a_ref, g_ref, *rest):
        acc = _dot(a_ref[...], g_ref[...], "tn")
        if add is not None:
            acc = acc + rest[0][...].astype(F32)
        rest[-1][...] = acc.astype(BF16)

    return pl.pallas_call(
        body, name=name,
        grid_spec=pltpu.PrefetchScalarGridSpec(
            num_scalar_prefetch=1, grid=(half, rows // tr, cols // tc),
            in_specs=[a_spec, g_spec] + ([o_spec] if add is not None else []), out_specs=o_spec),
        out_shape=jax.ShapeDtypeStruct((half, rows, cols), BF16), compiler_params=_params(3),
    )(core, a, g, *(() if add is None else (add,)))


def mm_dx_cols(name, g, w):
    t = g.shape[0]
    nblk, d, nb = w.shape
    tm, tn = _tile(t, 1024), _tile(d, WIDE_TILE)
    return _matmul(
        name, "nt", (g, w),
        [pl.BlockSpec((tm, nb), lambda i, j, k: (i, k)), pl.BlockSpec((None, tn, nb), lambda i, j, k: (k, j, 0))],
        jax.ShapeDtypeStruct((t, d), F32), pl.BlockSpec((tm, tn), lambda i, j, k: (i, j)),
        (t // tm, d // tn, nblk))


def mm_dx_ffn(dgt, wg, dup, wu):
    nblk, t, fb = dgt.shape
    d = wg.shape[2]
    tm, tn = _tile(t, 1024), _tile(d, WIDE_TILE)
    a_spec = pl.BlockSpec((None, tm, fb), lambda i, j, k: (k, i, 0))
    b_spec = pl.BlockSpec((None, fb, tn), lambda i, j, k: (k, 0, j))
    return _matmul(
        "mm_dx_ffn", "nn", (dgt, wg, dup, wu), [a_spec, b_spec, a_spec, b_spec],
        jax.ShapeDtypeStruct((t, d), F32), pl.BlockSpec((tm, tn), lambda i, j, k: (i, j)),
        (t // tm, d // tn, nblk))


def ffn_up(u, wg, wu):
    t, d = u.shape
    nblk, fb, _ = wg.shape
    tm = _tile(t, 512)

    def body(u_ref, wg_ref, wu_ref, gt_ref, up_ref, f_ref):
        for r0 in range(0, t, tm):
            rows = pl.ds(r0, tm)
            uu = u_ref[rows, :]
            gt = _dot(uu, wg_ref[...], "nt")
            up = _dot(uu, wu_ref[...], "nt")
            gt_ref[rows, :] = gt.astype(BF16)
            up_ref[rows, :] = up.astype(BF16)
            f_ref[rows, :] = (gt * _sigmoid(gt) * up).astype(BF16)

    w_spec = pl.BlockSpec((None, fb, d), lambda j: (j, 0, 0))
    o_spec = pl.BlockSpec((None, t, fb), lambda j: (j, 0, 0))
    return pl.pallas_call(
        body, name="ffn_up", grid=(nblk,),
        in_specs=[pl.BlockSpec((t, d), lambda j: (0, 0)), w_spec, w_spec],
        out_specs=[o_spec, o_spec, o_spec],
        out_shape=[jax.ShapeDtypeStruct((nblk, t, fb), BF16)] * 3,
        compiler_params=_params(1),
    )(u, wg, wu)


def ffn_bwd(dd, wd, gt, up):
    t, d = dd.shape
    nblk, fb, _ = wd.shape
    tm = _tile(t, 512)

    def body(dd_ref, wd_ref, gt_ref, up_ref, dgt_ref, dup_ref):
        for r0 in range(0, t, tm):
            rows = pl.ds(r0, tm)
            df = _dot(dd_ref[rows, :], wd_ref[...], "nt")
            g = gt_ref[rows, :].astype(F32)
            s = _sigmoid(g)
            dgt_ref[rows, :] = (df * up_ref[rows, :].astype(F32) * s * (1.0 + g * (1.0 - s))).astype(BF16)
            dup_ref[rows, :] = (df * g * s).astype(BF16)

    s_spec = pl.BlockSpec((None, t, fb), lambda j: (j, 0, 0))
    return pl.pallas_call(
        body, name="ffn_bwd", grid=(nblk,),
        in_specs=[pl.BlockSpec((t, d), lambda j: (0, 0)), pl.BlockSpec((None, fb, d), lambda j: (j, 0, 0)),
                  s_spec, s_spec],
        out_specs=[s_spec, s_spec],
        out_shape=[jax.ShapeDtypeStruct((nblk, t, fb), BF16)] * 2,
        compiler_params=_params(1),
    )(dd, wd, gt, up)


def rms_pre(h, g):
    t, d = h.shape
    tr = _tile(t, ROW_T)

    def body(h_ref, g_ref, o_ref):
        xn, _ = _rms_stats(h_ref[...])
        o_ref[...] = (xn * g_ref[...]).astype(BF16)

    return pl.pallas_call(
        body, name="rms_pre", grid=(t // tr,),
        in_specs=[_row_spec(tr, d), _vec_spec(d)], out_specs=_row_spec(tr, d),
        out_shape=jax.ShapeDtypeStruct((t, d), BF16), compiler_params=_params(1),
    )(h, g)


def res_norm(h, o, g_post, g_pre):
    t, d = h.shape
    tr = _tile(t, ROW_T)

    def body(h_ref, o_ref, gpo_ref, gpr_ref, h2_ref, u_ref):
        on, _ = _rms_stats(o_ref[...])
        h2 = h_ref[...] + on * gpo_ref[...]
        h2_ref[...] = h2
        hn, _ = _rms_stats(h2)
        u_ref[...] = (hn * gpr_ref[...]).astype(BF16)

    return pl.pallas_call(
        body, name="res_norm", grid=(t // tr,),
        in_specs=[_row_spec(tr, d), _row_spec(tr, d), _vec_spec(d), _vec_spec(d)],
        out_specs=[_row_spec(tr, d), _row_spec(tr, d)],
        out_shape=[jax.ShapeDtypeStruct((t, d), F32), jax.ShapeDtypeStruct((t, d), BF16)],
        compiler_params=_params(1),
    )(h, o, g_post, g_pre)


def final_loss(h2, dbr, g_post, target):
    t, d = h2.shape
    tr = _tile(t, ROW_T)

    def body(h2_ref, d_ref, g_ref, tg_ref, loss_ref, dy_ref, dd_ref, dg_ref):
        i = pl.program_id(0)

        @pl.when(i == 0)
        def _():
            loss_ref[...] = jnp.zeros_like(loss_ref)
            dg_ref[...] = jnp.zeros_like(dg_ref)

        x = d_ref[...]
        g = g_ref[...]
        xn, _ = _rms_stats(x)
        diff = h2_ref[...] + xn * g - tg_ref[...]
        loss_ref[...] += jnp.sum(jnp.sum(diff * diff, axis=1, keepdims=True), axis=0, keepdims=True)
        dy = diff * (1.0 / d)
        dy_ref[...] = dy
        dx, dg = _rms_bwd(dy, x, g)
        dd_ref[...] = dx.astype(BF16)
        dg_ref[...] += dg

    return pl.pallas_call(
        body, name="final_loss", grid=(t // tr,),
        in_specs=[_row_spec(tr, d), _row_spec(tr, d), _vec_spec(d), _row_spec(tr, d)],
        out_specs=[pl.BlockSpec((1, 1), lambda i: (0, 0)), _row_spec(tr, d), _row_spec(tr, d), _vec_spec(d)],
        out_shape=[jax.ShapeDtypeStruct((1, 1), F32), jax.ShapeDtypeStruct((t, d), F32),
                   jax.ShapeDtypeStruct((t, d), BF16), jax.ShapeDtypeStruct((1, d), F32)],
        compiler_params=_params(1),
    )(h2, dbr, g_post, target)


def norm_bwd(dh_out, du, h_in, g_pre, prev=None):
    t, d = h_in.shape
    tr = _tile(t, ROW_T)
    with_prev = prev is not None

    def body(*refs):
        if with_prev:
            dho_ref, du_ref, h_ref, gpr_ref, br_ref, gpo_ref, dh_ref, dgpr_ref, dbr_ref, dgpo_ref = refs
        else:
            dho_ref, du_ref, h_ref, gpr_ref, dh_ref, dgpr_ref = refs
        i = pl.program_id(0)

        @pl.when(i == 0)
        def _():
            dgpr_ref[...] = jnp.zeros_like(dgpr_ref)
            if with_prev:
                dgpo_ref[...] = jnp.zeros_like(dgpo_ref)

        dx, dg = _rms_bwd(du_ref[...], h_ref[...], gpr_ref[...])
        dh = dho_ref[...] + dx
        dh_ref[...] = dh
        dgpr_ref[...] += dg
        if with_prev:
            dbr, dg2 = _rms_bwd(dh, br_ref[...], gpo_ref[...])
            dbr_ref[...] = dbr.astype(BF16)
            dgpo_ref[...] += dg2

    row, vec = _row_spec(tr, d), _vec_spec(d)
    in_specs = [row, row, row, vec] + ([row, vec] if with_prev else [])
    out_specs = [row, vec] + ([row, vec] if with_prev else [])
    out_shape = [jax.ShapeDtypeStruct((t, d), F32), jax.ShapeDtypeStruct((1, d), F32)]
    if with_prev:
        out_shape += [jax.ShapeDtypeStruct((t, d), BF16), jax.ShapeDtypeStruct((1, d), F32)]
    args = (dh_out, du, h_in, g_pre) + (tuple(prev) if with_prev else ())
    return pl.pallas_call(
        body, name="norm_bwd_chain" if with_prev else "norm_bwd_first", grid=(t // tr,),
        in_specs=in_specs, out_specs=out_specs, out_shape=out_shape, compiler_params=_params(1),
    )(*args)


def mix_fwd(ya, yc, yl, ga, gc, gl):
    t, wa = ya.shape
    wc, wl = yc.shape[1], yl.shape[1]
    tr = _tile(t, ROW_T)

    def body(ya_ref, yc_ref, yl_ref, ga_ref, gc_ref, gl_ref, o_ref):
        o_ref[:, pl.ds(0, wa)] = (_rms_stats(ya_ref[...])[0] * ga_ref[...]).astype(BF16)
        o_ref[:, pl.ds(wa, wc)] = (_rms_stats(yc_ref[...])[0] * gc_ref[...]).astype(BF16)
        o_ref[:, pl.ds(wa + wc, wl)] = (_rms_stats(yl_ref[...])[0] * gl_ref[...]).astype(BF16)

    return pl.pallas_call(
        body, name="mix_fwd", grid=(t // tr,),
        in_specs=[_row_spec(tr, wa), _row_spec(tr, wc), _row_spec(tr, wl), _vec_spec(wa), _vec_spec(wc), _vec_spec(wl)],
        out_specs=_row_spec(tr, wa + wc + wl),
        out_shape=jax.ShapeDtypeStruct((t, wa + wc + wl), BF16), compiler_params=_params(1),
    )(ya, yc, yl, ga, gc, gl)


def mix_bwd(dmixed, ya, yc, yl, cpre, ga, gc, gl, lng, lnb):
    t, wa = ya.shape
    wc, wl = yc.shape[1], yl.shape[1]
    tr = _tile(t, ROW_T)

    def body(dm_ref, ya_ref, yc_ref, yl_ref, c_ref, ga_ref, gc_ref, gl_ref, lg_ref, lb_ref,
             dya_ref, dc_ref, dyl_ref, dga_ref, dgc_ref, dgl_ref, dlg_ref, dlb_ref):
        i = pl.program_id(0)

        @pl.when(i == 0)
        def _():
            for r in (dga_ref, dgc_ref, dgl_ref, dlg_ref, dlb_ref):
                r[...] = jnp.zeros_like(r)

        dya, dga = _rms_bwd(dm_ref[:, pl.ds(0, wa)], ya_ref[...], ga_ref[...])
        dya_ref[...] = dya
        dga_ref[...] += dga
        dyl, dgl = _rms_bwd(dm_ref[:, pl.ds(wa + wc, wl)], yl_ref[...], gl_ref[...])
        dyl_ref[...] = dyl
        dgl_ref[...] += dgl
        dyc, dgc = _rms_bwd(dm_ref[:, pl.ds(wa, wc)], yc_ref[...], gc_ref[...])
        dgc_ref[...] += dgc
        c = c_ref[...]
        xc = c - jnp.mean(c, axis=-1, keepdims=True)
        rstd = lax.rsqrt(jnp.mean(xc * xc, axis=-1, keepdims=True) + EPS)
        xhat = xc * rstd
        ln = xhat * lg_ref[...] + lb_ref[...]
        s = _sigmoid(ln)
        dln = dyc * s * (1.0 + ln * (1.0 - s))
        dlg_ref[...] += jnp.sum(dln * xhat, axis=0, keepdims=True)
        dlb_ref[...] += jnp.sum(dln, axis=0, keepdims=True)
        dxh = dln * lg_ref[...]
        dc_ref[...] = rstd * (dxh - jnp.mean(dxh, axis=-1, keepdims=True)
                              - xhat * jnp.mean(dxh * xhat, axis=-1, keepdims=True))

    return pl.pallas_call(
        body, name="mix_bwd", grid=(t // tr,),
        in_specs=[_row_spec(tr, wa + wc + wl), _row_spec(tr, wa), _row_spec(tr, wc), _row_spec(tr, wl), _row_spec(tr, wc),
                  _vec_spec(wa), _vec_spec(wc), _vec_spec(wl), _vec_spec(wc), _vec_spec(wc)],
        out_specs=[_row_spec(tr, wa), _row_spec(tr, wc), _row_spec(tr, wl),
                   _vec_spec(wa), _vec_spec(wc), _vec_spec(wl), _vec_spec(wc), _vec_spec(wc)],
        out_shape=[jax.ShapeDtypeStruct((t, wa), F32), jax.ShapeDtypeStruct((t, wc), F32), jax.ShapeDtypeStruct((t, wl), F32),
                   jax.ShapeDtypeStruct((1, wa), F32), jax.ShapeDtypeStruct((1, wc), F32), jax.ShapeDtypeStruct((1, wl), F32),
                   jax.ShapeDtypeStruct((1, wc), F32), jax.ShapeDtypeStruct((1, wc), F32)],
        compiler_params=_params(1),
    )(dmixed, ya, yc, yl, cpre, ga, gc, gl, lng, lnb)


def _hi_lo(x):
    hi = x.astype(BF16)
    return hi, (x - hi.astype(F32)).astype(BF16)


def _lane_sums(x, tri, reverse):
    nsub = x.shape[1] // tri.shape[0]
    order = range(nsub - 1, -1, -1) if reverse else range(nsub)
    parts, beyond = {}, None
    for b in order:
        blk = x[:, b * tri.shape[0]:(b + 1) * tri.shape[0]]
        hi, lo = _hi_lo(blk)
        c = _dot(hi, tri) + _dot(lo, tri)
        parts[b] = c if beyond is None else c + beyond
        tot = jnp.sum(blk, axis=1, keepdims=True)
        beyond = tot if beyond is None else beyond + tot
    return jnp.concatenate([parts[b] for b in range(nsub)], axis=1), beyond


def _att_strip(qb, kt, thresh, scale, diff, tri_gt):
    z = _dot(qb, kt, "nt") * scale
    sp = _softplus(z)
    mask = diff < thresh
    later, total = _lane_sums(jnp.where(mask, -sp, 0.0), tri_gt, True)
    return z, sp, mask, total, (z - sp) + later


def _att_consts(tq, tk):
    diff = lax.broadcasted_iota(jnp.int32, (tq, tk), 1) - lax.broadcasted_iota(jnp.int32, (tq, tk), 0)
    cb = min(tk, ATT_SUM)
    row = lax.broadcasted_iota(jnp.int32, (cb, cb), 0)
    col = lax.broadcasted_iota(jnp.int32, (cb, cb), 1)
    return diff, (row > col).astype(BF16), (row < col).astype(BF16)


def attn_fwd(proj, n_heads):
    t = proj.shape[0]
    tq, tk = _tile(t, ATT_TQ), _tile(t, ATT_TK)
    hp = ATT_HEADS
    wd = hp * HEAD_DIM
    scale = HEAD_DIM ** -0.5

    def body(q_ref, k_ref, v_ref, o_ref, kb_ref, vb_ref, acc_ref):
        kb_ref[...] = k_ref[...].astype(BF16)
        vb_ref[...] = v_ref[...].astype(BF16)
        diff, tri_gt, _ = _att_consts(tq, tk)

        def qblock(i, _):
            q0 = pl.multiple_of(i * tq, tq)
            heads = [pl.ds(h * HEAD_DIM, HEAD_DIM) for h in range(hp)]
            qbs = [q_ref[pl.ds(q0, tq), hs].astype(BF16) for hs in heads]
            acc_ref[...] = jnp.zeros_like(acc_ref)
            n_strips = (q0 + tq + tk - 1) // tk

            def strip(jj, runs):
                k0 = pl.multiple_of((n_strips - 1 - jj) * tk, tk)
                out = []
                for h, hs in enumerate(heads):
                    _, _, mask, total, logw = _att_strip(qbs[h], kb_ref[pl.ds(k0, tk), hs], q0 - k0, scale, diff, tri_gt)
                    w = jnp.where(mask, jnp.exp(logw + runs[h]), 0.0)
                    acc_ref[:, hs] += _dot(w.astype(BF16), vb_ref[pl.ds(k0, tk), hs])
                    out.append(runs[h] + total)
                return tuple(out)

            lax.fori_loop(0, n_strips, strip, tuple(jnp.zeros((tq, 1), F32) for _ in range(hp)))
            o_ref[pl.ds(q0, tq), :] = acc_ref[...]
            return 0

        lax.fori_loop(0, t // tq, qblock, 0)

    def col_spec(base):
        return pl.BlockSpec((t, wd), lambda h, base=base: (0, base + h))

    ng = n_heads // hp
    return pl.pallas_call(
        body, name="attn_fwd", grid=(ng,),
        in_specs=[col_spec(0), col_spec(ng), col_spec(2 * ng)],
        out_specs=col_spec(0),
        out_shape=jax.ShapeDtypeStruct((t, n_heads * HEAD_DIM), F32),
        scratch_shapes=[pltpu.VMEM((t, wd), BF16), pltpu.VMEM((t, wd), BF16), pltpu.VMEM((tq, wd), F32)],
        compiler_params=_params(1),
    )(proj, proj, proj)


def attn_bwd(proj, dy, n_heads):
    t = proj.shape[0]
    tq, tk = _tile(t, ATT_TQ), _tile(t, ATT_TK)
    hp = ATT_HEADS
    wd = hp * HEAD_DIM
    scale = HEAD_DIM ** -0.5

    def body(q_ref, k_ref, v_ref, dy_ref, dq_ref, dk_ref, dv_ref, qb_ref, kb_ref, vb_ref, dob_ref, dk_acc, dv_acc,
             dq_acc, run_s):
        qb_ref[...] = q_ref[...].astype(BF16)
        kb_ref[...] = k_ref[...].astype(BF16)
        vb_ref[...] = v_ref[...].astype(BF16)
        dob_ref[...] = dy_ref[...].astype(BF16)
        dk_acc[...] = jnp.zeros_like(dk_acc)
        dv_acc[...] = jnp.zeros_like(dv_acc)
        diff, tri_gt, tri_lt = _att_consts(tq, tk)

        def qblock(i, _):
            q0 = pl.multiple_of(i * tq, tq)
            heads = [pl.ds(h * HEAD_DIM, HEAD_DIM) for h in range(hp)]
            qbs = [qb_ref[pl.ds(q0, tq), hs] for hs in heads]
            dobs = [dob_ref[pl.ds(q0, tq), hs] for hs in heads]
            dq_acc[...] = jnp.zeros_like(dq_acc)
            n_strips = (q0 + tq + tk - 1) // tk

            def sweep(jj, runs):
                si = n_strips - 1 - jj
                k0 = pl.multiple_of(si * tk, tk)
                out = []
                for h, hs in enumerate(heads):
                    sp = _softplus(_dot(qbs[h], kb_ref[pl.ds(k0, tk), hs], "nt") * scale)
                    run_s[h, si] = runs[h]
                    out.append(runs[h] + jnp.sum(jnp.where(diff < q0 - k0, -sp, 0.0), axis=1, keepdims=True))
                return tuple(out)

            zero = tuple(jnp.zeros((tq, 1), F32) for _ in range(hp))
            lax.fori_loop(0, n_strips, sweep, zero)

            def strip(si, gsums):
                k0 = pl.multiple_of(si * tk, tk)
                out = []
                for h, hs in enumerate(heads):
                    kt = kb_ref[pl.ds(k0, tk), hs]
                    vt = vb_ref[pl.ds(k0, tk), hs]
                    z, sp, mask, _, logw = _att_strip(qbs[h], kt, q0 - k0, scale, diff, tri_gt)
                    w = jnp.where(mask, jnp.exp(logw + run_s[h, si]), 0.0)
                    g = w * _dot(dobs[h], vt, "nt")
                    before, gtot = _lane_sums(g, tri_lt, False)
                    sig = jnp.exp(z - sp)
                    dz = jnp.where(mask, g * (1.0 - sig) - (before + gsums[h]) * sig, 0.0) * scale
                    dzb = dz.astype(BF16)
                    dk_acc[pl.ds(k0, tk), hs] += _dot(dzb, qbs[h], "tn")
                    dv_acc[pl.ds(k0, tk), hs] += _dot(w.astype(BF16), dobs[h], "tn")
                    dq_acc[:, hs] += _dot(dzb, kt)
                    out.append(gsums[h] + gtot)
                return tuple(out)

            lax.fori_loop(0, n_strips, strip, zero)
            dq_ref[pl.ds(q0, tq), :] = dq_acc[...].astype(BF16)
            return 0

        lax.fori_loop(0, t // tq, qblock, 0)
        dk_ref[...] = dk_acc[...].astype(BF16)
        dv_ref[...] = dv_acc[...].astype(BF16)

    def col_spec(base):
        return pl.BlockSpec((t, wd), lambda h, base=base: (0, base + h))

    ng = n_heads // hp
    return pl.pallas_call(
        body, name="attn_bwd", grid=(ng,),
        in_specs=[col_spec(0), col_spec(ng), col_spec(2 * ng), col_spec(0)],
        out_specs=[col_spec(0), col_spec(0), col_spec(0)],
        out_shape=[jax.ShapeDtypeStruct((t, n_heads * HEAD_DIM), BF16)] * 3,
        scratch_shapes=[pltpu.VMEM((t, wd), BF16)] * 4 + [pltpu.VMEM((t, wd), F32)] * 2
        + [pltpu.VMEM((tq, wd), F32), pltpu.VMEM((hp, t // tk, tq, 1), F32)],
        compiler_params=_params(1),
    )(proj, proj, proj, dy)


def _glu_halo(vc, gc, vp, gp, ubuf, i, tt, halo):
    uprev = vp[pl.ds(tt - halo, halo), :] * _sigmoid(gp[pl.ds(tt - halo, halo), :])
    ubuf[pl.ds(0, halo), :] = jnp.where(i > 0, uprev, 0.0)
    ubuf[pl.ds(halo, tt), :] = vc[...] * _sigmoid(gc[...])


def conv_fwd(proj, col0, cc, w, b, lng, lnb):
    t = proj.shape[0]
    tt = _tile(t, ROW_T)
    vi, gi = col0 // cc, col0 // cc + 1
    off = CONV_HALO - (DW_LEN - 1)

    def body(vc, gc, vp, gp, w_ref, b_ref, lg_ref, lb_ref, c_ref, y_ref, ubuf):
        i = pl.program_id(0)
        _glu_halo(vc, gc, vp, gp, ubuf, i, tt, CONV_HALO)
        for ch in range(cc // LANE):
            sl = pl.ds(ch * LANE, LANE)
            acc = jnp.zeros((tt, LANE), F32) + b_ref[:, sl]
            for tap in range(DW_LEN):
                acc = acc + w_ref[pl.ds(tap, 1), sl] * ubuf[pl.ds(off + tap, tt), sl]
            c_ref[:, sl] = acc
        c = c_ref[...]
        xc = c - jnp.mean(c, axis=-1, keepdims=True)
        ln = xc * lax.rsqrt(jnp.mean(xc * xc, axis=-1, keepdims=True) + EPS) * lg_ref[...] + lb_ref[...]
        y_ref[...] = ln * _sigmoid(ln)

    cur = lambda c: pl.BlockSpec((tt, cc), lambda i, c=c: (i, c))
    prev = lambda c: pl.BlockSpec((tt, cc), lambda i, c=c: (jnp.maximum(i - 1, 0), c))
    return pl.pallas_call(
        body, name="conv_fwd", grid=(t // tt,),
        in_specs=[cur(vi), cur(gi), prev(vi), prev(gi), pl.BlockSpec((DW_LEN, cc), lambda i: (0, 0)),
                  _vec_spec(cc), _vec_spec(cc), _vec_spec(cc)],
        out_specs=[_row_spec(tt, cc), _row_spec(tt, cc)],
        out_shape=[jax.ShapeDtypeStruct((t, cc), F32)] * 2,
        scratch_shapes=[pltpu.VMEM((CONV_HALO + tt, cc), F32)],
        compiler_params=_params(1),
    )(proj, proj, proj, proj, w, b, lng, lnb)


def conv_bwd(proj, col0, cc, dc, w):
    t = proj.shape[0]
    tt = _tile(t, ROW_T)
    nt = t // tt
    vi, gi = col0 // cc, col0 // cc + 1
    off = CONV_HALO - (DW_LEN - 1)

    def body(vc, gc, vp, gp, dcc, dcn, w_ref, dvg_ref, dw_ref, db_ref, ubuf, dbuf):
        i = pl.program_id(0)

        @pl.when(i == 0)
        def _():
            dw_ref[...] = jnp.zeros_like(dw_ref)
            db_ref[...] = jnp.zeros_like(db_ref)

        _glu_halo(vc, gc, vp, gp, ubuf, i, tt, CONV_HALO)
        dbuf[pl.ds(0, tt), :] = dcc[...]
        dbuf[pl.ds(tt, CONV_HALO), :] = jnp.where(i < nt - 1, dcn[pl.ds(0, CONV_HALO), :], 0.0)
        db_ref[...] += jnp.sum(dcc[...], axis=0, keepdims=True)
        for ch in range(cc // LANE):
            sl = pl.ds(ch * LANE, LANE)
            dcv = dbuf[pl.ds(0, tt), sl]
            du = jnp.zeros((tt, LANE), F32)
            for tap in range(DW_LEN):
                du = du + w_ref[pl.ds(tap, 1), sl] * dbuf[pl.ds(DW_LEN - 1 - tap, tt), sl]
                dw_ref[pl.ds(tap, 1), sl] += jnp.sum(dcv * ubuf[pl.ds(off + tap, tt), sl], axis=0, keepdims=True)
            s = _sigmoid(gc[:, sl])
            val = vc[:, sl]
            dvg_ref[:, sl] = (du * s).astype(BF16)
            dvg_ref[:, pl.ds(cc + ch * LANE, LANE)] = (du * val * s * (1.0 - s)).astype(BF16)

    cur = lambda c: pl.BlockSpec((tt, cc), lambda i, c=c: (i, c))
    prev = lambda c: pl.BlockSpec((tt, cc), lambda i, c=c: (jnp.maximum(i - 1, 0), c))
    return pl.pallas_call(
        body, name="conv_bwd", grid=(nt,),
        in_specs=[cur(vi), cur(gi), prev(vi), prev(gi), _row_spec(tt, cc),
                  pl.BlockSpec((tt, cc), lambda i: (jnp.minimum(i + 1, nt - 1), 0)),
                  pl.BlockSpec((DW_LEN, cc), lambda i: (0, 0))],
        out_specs=[_row_spec(tt, 2 * cc), pl.BlockSpec((DW_LEN, cc), lambda i: (0, 0)), _vec_spec(cc)],
        out_shape=[jax.ShapeDtypeStruct((t, 2 * cc), BF16), jax.ShapeDtypeStruct((DW_LEN, cc), F32),
                   jax.ShapeDtypeStruct((1, cc), F32)],
        scratch_shapes=[pltpu.VMEM((CONV_HALO + tt, cc), F32), pltpu.VMEM((tt + CONV_HALO, cc), F32)],
        compiler_params=_params(1),
    )(proj, proj, proj, proj, dc, dc, w)


def _lru_gates(xbuf, cw_ref, cb_ref, wa_ref, ba_ref, wi_ref, bi_ref, lam_ref, tt, wl):
    bd = wl // LRU_BLOCKS
    xr = jnp.zeros((tt, wl), F32) + cb_ref[...]
    for tap in range(LRU_LEN):
        xr = xr + cw_ref[pl.ds(tap, 1), :] * xbuf[pl.ds(LRU_HALO - (LRU_LEN - 1) + tap, tt), :]
    xb = xr.astype(BF16)
    ga = jnp.concatenate([_dot(xb[:, n * bd:(n + 1) * bd], wa_ref[n]) for n in range(LRU_BLOCKS)], axis=1) + ba_ref[...]
    gi = jnp.concatenate([_dot(xb[:, n * bd:(n + 1) * bd], wi_ref[n]) for n in range(LRU_BLOCKS)], axis=1) + bi_ref[...]
    r = _sigmoid(ga)
    ig = _sigmoid(gi)
    spl = _softplus(-lam_ref[...])
    log_a = -LRU_C * r * spl
    a = jnp.exp(log_a)
    m = jnp.sqrt(_neg_expm1(2.0 * log_a))
    return xr, xb, r, ig, spl, a, m


def _group_scan(a8, b8, reverse):
    rid = lax.broadcasted_iota(jnp.int32, a8.shape, 0)
    aa, bb = a8, b8
    for dist in (1, 2, 4):
        shift = SUBLANE - dist if reverse else dist
        a_sh = pltpu.roll(aa, shift, 0)
        b_sh = pltpu.roll(bb, shift, 0)
        valid = (rid < SUBLANE - dist) if reverse else (rid >= dist)
        bb = jnp.where(valid, aa * b_sh + bb, bb)
        aa = jnp.where(valid, aa * a_sh, aa)
    return aa, bb


def _pick_row(x8, r):
    rid = lax.broadcasted_iota(jnp.int32, x8.shape, 0)
    return jnp.sum(jnp.where(rid == r, x8, 0.0), axis=0, keepdims=True)


def lru_fwd(proj, col0, wl, cw, cb, wa, ba, wi, bi, lam):
    t = proj.shape[0]
    tt = _tile(t, ROW_T)
    xi, yi = col0 // wl, col0 // wl + 1

    def body(xc, xp, ry, cw_ref, cb_ref, wa_ref, ba_ref, wi_ref, bi_ref, lam_ref, hs_ref, y_ref,
             xbuf, a_s, b_s, hcar):
        i = pl.program_id(0)

        @pl.when(i == 0)
        def _():
            hcar[...] = jnp.zeros_like(hcar)

        xbuf[pl.ds(0, LRU_HALO), :] = jnp.where(i > 0, xp[pl.ds(tt - LRU_HALO, LRU_HALO), :], 0.0)
        xbuf[pl.ds(LRU_HALO, tt), :] = xc[...]
        xr, _, _, ig, _, a, m = _lru_gates(xbuf, cw_ref, cb_ref, wa_ref, ba_ref, wi_ref, bi_ref, lam_ref, tt, wl)
        a_s[...] = a
        b_s[...] = m * ig * xr

        def group(gidx, h):
            r0 = pl.multiple_of(gidx * SUBLANE, SUBLANE)
            aa, bb = _group_scan(a_s[pl.ds(r0, SUBLANE), :], b_s[pl.ds(r0, SUBLANE), :], False)
            h8 = aa * h + bb
            hs_ref[pl.ds(r0, SUBLANE), :] = h8
            return _pick_row(h8, SUBLANE - 1)

        hcar[...] = lax.fori_loop(0, tt // SUBLANE, group, hcar[...])
        gel, _ = _gelu_and_grad(ry[...])
        y_ref[...] = hs_ref[...] * gel

    cur = lambda c: pl.BlockSpec((tt, wl), lambda i, c=c: (i, c))
    full = lambda shape: pl.BlockSpec(shape, lambda i: (0,) * len(shape))
    return pl.pallas_call(
        body, name="lru_fwd", grid=(t // tt,),
        in_specs=[cur(xi), pl.BlockSpec((tt, wl), lambda i: (jnp.maximum(i - 1, 0), xi)), cur(yi),
                  full((LRU_LEN, wl)), _vec_spec(wl), full(wa.shape), _vec_spec(wl), full(wi.shape), _vec_spec(wl),
                  _vec_spec(wl)],
        out_specs=[_row_spec(tt, wl), _row_spec(tt, wl)],
        out_shape=[jax.ShapeDtypeStruct((t, wl), F32)] * 2,
        scratch_shapes=[pltpu.VMEM((LRU_HALO + tt, wl), F32), pltpu.VMEM((tt, wl), F32), pltpu.VMEM((tt, wl), F32),
                        pltpu.VMEM((1, wl), F32)],
        compiler_params=_params(1),
    )(proj, proj, proj, cw, cb, wa, ba, wi, bi, lam)


def lru_bwd(proj, col0, wl, hs, dy, cw, cb, wa, ba, wi, bi, lam):
    t = proj.shape[0]
    tt = _tile(t, ROW_T)
    nt = t // tt
    xi, yi = col0 // wl, col0 // wl + 1
    bd = wl // LRU_BLOCKS

    def body(xc, xp, ry, hc, hp, dy_ref, cw_ref, cb_ref, wa_ref, ba_ref, wi_ref, bi_ref, lam_ref,
             dxy_ref, dcw_ref, dcb_ref, dwa_ref, dba_ref, dwi_ref, dbi_ref, dlam_ref,
             xbuf, hbuf, abuf, e_s, dh_s, dxbuf, dhcar):
        i = pl.program_id(0)
        first = i == 0

        @pl.when(first)
        def _():
            for r in (dcw_ref, dcb_ref, dwa_ref, dba_ref, dwi_ref, dbi_ref, dlam_ref, dhcar):
                r[...] = jnp.zeros_like(r)
            abuf[pl.ds(tt, LRU_HALO), :] = jnp.zeros((LRU_HALO, wl), F32)
            dxbuf[pl.ds(tt, LRU_HALO), :] = jnp.zeros((LRU_HALO, wl), F32)

        has_prev = i < nt - 1
        xbuf[pl.ds(0, LRU_HALO), :] = jnp.where(has_prev, xp[pl.ds(tt - LRU_HALO, LRU_HALO), :], 0.0)
        xbuf[pl.ds(LRU_HALO, tt), :] = xc[...]
        hbuf[pl.ds(0, LRU_HALO), :] = jnp.where(has_prev, hp[pl.ds(tt - LRU_HALO, LRU_HALO), :], 0.0)
        hbuf[pl.ds(LRU_HALO, tt), :] = hc[...]
        xr, xb, r, ig, spl, a, m = _lru_gates(xbuf, cw_ref, cb_ref, wa_ref, ba_ref, wi_ref, bi_ref, lam_ref, tt, wl)
        gel, dgel = _gelu_and_grad(ry[...])
        dyv = dy_ref[...]
        e_s[...] = dyv * gel
        dxy_ref[:, pl.ds(wl, wl)] = (dyv * hc[...] * dgel).astype(BF16)
        abuf[pl.ds(0, tt), :] = a
        a_next = abuf[pl.ds(1, tt), :]
        dh_s[...] = a_next

        def group(it, dh_in):
            r0 = pl.multiple_of((tt // SUBLANE - 1 - it) * SUBLANE, SUBLANE)
            aa, bb = _group_scan(dh_s[pl.ds(r0, SUBLANE), :], e_s[pl.ds(r0, SUBLANE), :], True)
            dh8 = aa * dh_in + bb
            dh_s[pl.ds(r0, SUBLANE), :] = dh8
            return _pick_row(dh8, 0)

        dhcar[...] = lax.fori_loop(0, tt // SUBLANE, group, dhcar[...])
        abuf[pl.ds(tt, LRU_HALO), :] = a[0:LRU_HALO, :]
        dh = dh_s[...]
        h_m1 = hbuf[pl.ds(LRU_HALO - 1, tt), :]
        dlog_a = dh * h_m1 * a - dh * ig * xr * (a * a / m)
        dig = dh * m * xr
        dxr = dh * m * ig
        dga = dlog_a * (-LRU_C) * spl * r * (1.0 - r)
        dgi = dig * ig * (1.0 - ig)
        dlam_ref[...] += jnp.sum(dlog_a * r, axis=0, keepdims=True) * (LRU_C * _sigmoid(-lam_ref[...]))
        dba_ref[...] += jnp.sum(dga, axis=0, keepdims=True)
        dbi_ref[...] += jnp.sum(dgi, axis=0, keepdims=True)
        dgab = dga.astype(BF16)
        dgib = dgi.astype(BF16)
        back = []
        for n in range(LRU_BLOCKS):
            sl = slice(n * bd, (n + 1) * bd)
            dwa_ref[n] += _dot(xb[:, sl], dgab[:, sl], "tn")
            dwi_ref[n] += _dot(xb[:, sl], dgib[:, sl], "tn")
            back.append(_dot(dgab[:, sl], wa_ref[n], "nt") + _dot(dgib[:, sl], wi_ref[n], "nt"))
        dxr = dxr + jnp.concatenate(back, axis=1)
        dcb_ref[...] += jnp.sum(dxr, axis=0, keepdims=True)
        dxbuf[pl.ds(0, tt), :] = dxr
        drx = jnp.zeros((tt, wl), F32)
        for tap in range(LRU_LEN):
            drx = drx + cw_ref[pl.ds(tap, 1), :] * dxbuf[pl.ds(LRU_LEN - 1 - tap, tt), :]
            dcw_ref[pl.ds(tap, 1), :] += jnp.sum(
                dxr * xbuf[pl.ds(LRU_HALO - (LRU_LEN - 1) + tap, tt), :], axis=0, keepdims=True)
        dxbuf[pl.ds(tt, LRU_HALO), :] = dxr[0:LRU_HALO, :]
        dxy_ref[:, pl.ds(0, wl)] = drx.astype(BF16)

    rev = lambda c: pl.BlockSpec((tt, wl), lambda i, c=c: (nt - 1 - i, c))
    rev_prev = lambda c: pl.BlockSpec((tt, wl), lambda i, c=c: (jnp.maximum(nt - 2 - i, 0), c))
    full = lambda shape: pl.BlockSpec(shape, lambda i: (0,) * len(shape))
    vec = _vec_spec(wl)
    return pl.pallas_call(
        body, name="lru_bwd", grid=(nt,),
        in_specs=[rev(xi), rev_prev(xi), rev(yi), rev(0), rev_prev(0), rev(0),
                  full((LRU_LEN, wl)), vec, full(wa.shape), vec, full(wi.shape), vec, vec],
        out_specs=[pl.BlockSpec((tt, 2 * wl), lambda i: (nt - 1 - i, 0)), full((LRU_LEN, wl)), vec,
                   full(wa.shape), vec, full(wi.shape), vec, vec],
        out_shape=[jax.ShapeDtypeStruct((t, 2 * wl), BF16), jax.ShapeDtypeStruct((LRU_LEN, wl), F32),
                   jax.ShapeDtypeStruct((1, wl), F32), jax.ShapeDtypeStruct(wa.shape, F32),
                   jax.ShapeDtypeStruct((1, wl), F32), jax.ShapeDtypeStruct(wi.shape, F32),
                   jax.ShapeDtypeStruct((1, wl), F32), jax.ShapeDtypeStruct((1, wl), F32)],
        scratch_shapes=[pltpu.VMEM((LRU_HALO + tt, wl), F32), pltpu.VMEM((LRU_HALO + tt, wl), F32),
                        pltpu.VMEM((tt + LRU_HALO, wl), F32), pltpu.VMEM((tt, wl), F32), pltpu.VMEM((tt, wl), F32),
                        pltpu.VMEM((tt + LRU_HALO, wl), F32), pltpu.VMEM((1, wl), F32)],
        compiler_params=_params(1),
    )(proj, proj, proj, hs, hs, dy, cw, cb, wa, ba, wi, bi, lam)


def _adamw(w, g, m, v):
    m = ADAM_B1 * m + (1.0 - ADAM_B1) * g
    v = ADAM_B2 * v + (1.0 - ADAM_B2) * (g * g)
    m_hat = m / (1.0 - ADAM_B1 ** ADAM_STEP)
    v_hat = v / (1.0 - ADAM_B2 ** ADAM_STEP)
    delta = -ADAM_LR * (m_hat / (jnp.sqrt(v_hat) + ADAM_EPS) + ADAM_WD * w)
    return delta, m, v


def adam_big(name, w, m, v, parts, chip):
    n_layers, rows, cols = w.shape
    tr = _tile(rows, 128 if cols > 1024 else 256)
    nrt = rows // tr

    def body(chip_ref, *refs):
        w_ref, m_ref, v_ref = refs[:3]
        part_refs = refs[3:3 + 4 * n_layers]
        g_ref, d_ref, mo_ref, vo_ref = refs[3 + 4 * n_layers:]
        layer = pl.program_id(0)
        for l in range(n_layers):
            @pl.when(layer == l)
            def _(l=l):
                g = part_refs[4 * l][...].astype(F32)
                for p in range(1, 4):
                    g = g + part_refs[4 * l + p][...].astype(F32)
                delta, mn, vn = _adamw(w_ref[...], g, m_ref[...], v_ref[...])
                g_ref[...] = g
                d_ref[...] = delta
                mo_ref[...] = mn
                vo_ref[...] = vn

    wspec = pl.BlockSpec((None, tr, cols), lambda l, i, chip_ref: (l, i, 0))
    operands, in_specs = [w, m, v], [wspec, wspec, wspec]
    for l in range(n_layers):
        mine, recv = parts[l]
        operands.append(mine)
        in_specs.append(pl.BlockSpec(
            (None, tr, cols), lambda ll, i, chip_ref, l=l: (chip_ref[0], jnp.where(ll == l, i, 0), 0)))
        for p in range(3):
            operands.append(recv)
            in_specs.append(pl.BlockSpec(
                (None, tr, cols), lambda ll, i, chip_ref, l=l, p=p: (p, jnp.where(ll == l, i, 0), 0)))
    return pl.pallas_call(
        body, name=name,
        grid_spec=pltpu.PrefetchScalarGridSpec(
            num_scalar_prefetch=1, grid=(n_layers, nrt), in_specs=in_specs, out_specs=[wspec] * 4),
        out_shape=[jax.ShapeDtypeStruct(w.shape, F32)] * 4, compiler_params=_params(2),
    )(chip, *operands)


def adam_small(w, m, v, g):
    rows = w.shape[0]
    tr = _tile(rows, PACK_ROWS)

    def body(w_ref, m_ref, v_ref, g_ref, d_ref, mo_ref, vo_ref):
        delta, mn, vn = _adamw(w_ref[...], g_ref[...], m_ref[...], v_ref[...])
        d_ref[...] = delta
        mo_ref[...] = mn
        vo_ref[...] = vn

    spec = _row_spec(tr, LANE)
    return pl.pallas_call(
        body, name="adam_small", grid=(rows // tr,), in_specs=[spec] * 4, out_specs=[spec] * 3,
        out_shape=[jax.ShapeDtypeStruct(w.shape, F32)] * 3, compiler_params=_params(1),
    )(w, m, v, g)


def sum_parts(parts):
    _, rows, _ = parts.shape
    tr = _tile(rows, PACK_ROWS)

    def body(p_ref, o_ref):
        acc = p_ref[0]
        for k in range(1, N_DEV):
            acc = acc + p_ref[k]
        o_ref[...] = acc

    return pl.pallas_call(
        body, name="sum_parts", grid=(rows // tr,),
        in_specs=[pl.BlockSpec((N_DEV, tr, LANE), lambda i: (0, i, 0))], out_specs=_row_spec(tr, LANE),
        out_shape=jax.ShapeDtypeStruct((rows, LANE), F32), compiler_params=_params(1),
    )(parts)


def place_own(x, me, dtype):
    rows, cols = x.shape
    tr = _tile(rows, 256)

    def body(me_ref, x_ref, o_ref):
        o_ref[...] = x_ref[...].astype(dtype)

    return pl.pallas_call(
        body, name="place_own",
        grid_spec=pltpu.PrefetchScalarGridSpec(
            num_scalar_prefetch=1, grid=(rows // tr,),
            in_specs=[pl.BlockSpec((tr, cols), lambda i, me_ref: (i, 0))],
            out_specs=pl.BlockSpec((None, tr, cols), lambda i, me_ref: (me_ref[0], i, 0))),
        out_shape=jax.ShapeDtypeStruct((N_DEV, rows, cols), dtype), compiler_params=_params(1),
    )(me, x)


_HBM = pl.BlockSpec(memory_space=pltpu.HBM)


def _place():
    return lax.axis_index("x"), lax.axis_index("y"), lax.axis_index("c")


def _other_chips(x, y):
    return [(1 - x, y), (x, 1 - y), (1 - x, 1 - y)]


def all_gather(name, shard, me, dtype=None):
    def body(buf_ref, out_ref, send_sems, recv_sems):
        del buf_ref
        x, y, c = _place()
        mine, sibling = (x, y, c), (x, y, 1 - c)
        chips = _other_chips(x, y)

        def copy(k, block, to):
            slot = out_ref.at[4 * block[0] + 2 * block[1] + block[2]]
            return pltpu.make_async_remote_copy(
                src_ref=slot, dst_ref=slot, send_sem=send_sems.at[k], recv_sem=recv_sems.at[k],
                device_id=to, device_id_type=pl.DeviceIdType.MESH)

        first = [copy(0, mine, sibling)] + [copy(1 + j, mine, (*chip, c)) for j, chip in enumerate(chips)]
        for cp in first:
            cp.start()
        passed = [copy(4 + j, (*chip, c), sibling) for j, chip in enumerate(chips)]
        for j, chip in enumerate(chips):
            copy(1 + j, (*chip, c), mine).wait_recv()
            passed[j].start()
        copy(0, sibling, mine).wait_recv()
        for j, chip in enumerate(chips):
            copy(4 + j, (*chip, 1 - c), mine).wait_recv()
        for cp in first + passed:
            cp.wait_send()

    buf = place_own(shard, me, dtype or shard.dtype)
    return pl.pallas_call(
        body, name=name, out_shape=jax.ShapeDtypeStruct(buf.shape, buf.dtype),
        in_specs=[_HBM], out_specs=_HBM, input_output_aliases={0: 0},
        scratch_shapes=[pltpu.SemaphoreType.DMA((7,)), pltpu.SemaphoreType.DMA((7,))],
    )(buf)


def _own_block_copies(src_refs, dst_refs, send_sems, recv_sems, arrivals):
    x, y, c = _place()
    peers = [(x, y, 1 - c)] + [(*chip, c) for chip in _other_chips(x, y)]
    copies = []
    for b, (src, dst) in enumerate(zip(src_refs, dst_refs)):
        for k, peer in enumerate(peers):
            def copy(landing, b=b, k=k, peer=peer, src=src, dst=dst):
                return pltpu.make_async_remote_copy(
                    src_ref=src.at[4 * x + 2 * y + c], dst_ref=dst.at[landing],
                    send_sem=send_sems.at[4 * b + k], recv_sem=recv_sems.at[4 * b + k],
                    device_id=peer, device_id_type=pl.DeviceIdType.MESH)
            copies.append((copy(4 * x + 2 * y + c), copy(4 * peer[0] + 2 * peer[1] + peer[2]) if arrivals else None))
    return copies


def gather_start(name, bufs, after):
    n = len(bufs)

    def body(*refs):
        send_sems, recv_sems = refs[n + 1], refs[n + 2]
        thru = refs[n + 3:2 * n + 3]
        for send, _ in _own_block_copies(thru, thru, send_sems, recv_sems, False):
            send.start()
        refs[2 * n + 3][...] = jnp.zeros((SUBLANE, LANE), F32)

    return pl.pallas_call(
        body, name=name,
        out_shape=(pltpu.SemaphoreType.DMA((4 * n,)), pltpu.SemaphoreType.DMA((4 * n,)),
                   *[pltpu.HBM(b.shape, b.dtype) for b in bufs], jax.ShapeDtypeStruct((SUBLANE, LANE), F32)),
        in_specs=(*(_HBM,) * n, _ANY), out_specs=(_SEM, _SEM, *(_HBM,) * n, _TOKEN),
        input_output_aliases={b: 2 + b for b in range(n)},
        compiler_params=pltpu.CompilerParams(has_side_effects=_EFFECT),
    )(*[_hbm(b) for b in bufs], after)


def gather_wait(name, state, first, count, after):
    send_sems, recv_sems = state[:2]
    bufs = state[2 + first:2 + first + count]
    n = len(bufs)

    def body(*refs):
        ins = refs[:n]
        send_sems, recv_sems = refs[n], refs[n + 1]
        shift = 4 * first
        for send, arrival in _own_block_copies(
                ins, ins, send_sems.at[pl.ds(shift, 4 * n)], recv_sems.at[pl.ds(shift, 4 * n)], True):
            send.wait_send()
            arrival.wait_recv()

    return pl.pallas_call(
        body, name=name, out_shape=tuple(pltpu.HBM(b.shape, b.dtype) for b in bufs),
        in_specs=(*(_HBM,) * n, _SEM, _SEM, _ANY), out_specs=(_HBM,) * n,
        input_output_aliases={b: b for b in range(n)},
        compiler_params=pltpu.CompilerParams(has_side_effects=_EFFECT),
    )(*bufs, send_sems, recv_sems, after)


def _forward_copies(bufs, send_sems, recv_sems, arrivals):
    x, y, c = _place()
    copies = []
    for b, buf in enumerate(bufs):
        for k, chip in enumerate(_other_chips(x, y)):
            def copy(core, b=b, k=k, chip=chip, buf=buf):
                return pltpu.make_async_remote_copy(
                    src_ref=buf.at[4 * chip[0] + 2 * chip[1] + c], dst_ref=buf.at[4 * chip[0] + 2 * chip[1] + core],
                    send_sem=send_sems.at[3 * b + k], recv_sem=recv_sems.at[3 * b + k],
                    device_id=(x, y, 1 - c), device_id_type=pl.DeviceIdType.MESH)
            copies.append((copy(c), copy(1 - c) if arrivals else None))
    return copies


def gather_finish(name, bufs):
    n = len(bufs)

    def body(*refs):
        copies = _forward_copies(refs[n:2 * n], refs[2 * n], refs[2 * n + 1], True)
        for send, _ in copies:
            send.start()
        for send, arrival in copies:
            send.wait_send()
            arrival.wait_recv()

    return pl.pallas_call(
        body, name=name, out_shape=tuple(jax.ShapeDtypeStruct(b.shape, b.dtype) for b in bufs),
        in_specs=(_HBM,) * n, out_specs=(_HBM,) * n, input_output_aliases={b: b for b in range(n)},
        scratch_shapes=[pltpu.SemaphoreType.DMA((3 * n,)), pltpu.SemaphoreType.DMA((3 * n,))],
    )(*bufs)


def forward_start(name, bufs, carry):
    n = len(bufs)

    def body(*refs):
        send_sems, recv_sems = refs[n + 1], refs[n + 2]
        for send, _ in _forward_copies(refs[:n], send_sems, recv_sems, False):
            send.start()

    out = pl.pallas_call(
        body, name=name,
        out_shape=(pltpu.SemaphoreType.DMA((3 * n,)), pltpu.SemaphoreType.DMA((3 * n,)),
                   *[pltpu.HBM(b.shape, b.dtype) for b in bufs], pltpu.HBM(carry.shape, carry.dtype)),
        in_specs=(_HBM,) * (n + 1), out_specs=(_SEM, _SEM, *(_HBM,) * (n + 1)),
        input_output_aliases={b: 2 + b for b in range(n + 1)},
        compiler_params=pltpu.CompilerParams(has_side_effects=_EFFECT),
    )(*[_hbm(b) for b in bufs], _hbm(carry))
    return out[:-1], out[-1]


def forward_wait(name, state, after):
    send_sems, recv_sems, *bufs = state
    n = len(bufs)

    def body(*refs):
        for send, arrival in _forward_copies(refs[:n], refs[n], refs[n + 1], True):
            send.wait_send()
            arrival.wait_recv()

    return pl.pallas_call(
        body, name=name, out_shape=tuple(pltpu.HBM(b.shape, b.dtype) for b in bufs),
        in_specs=(*(_HBM,) * n, _SEM, _SEM, _ANY), out_specs=(_HBM,) * n,
        input_output_aliases={b: b for b in range(n)},
        compiler_params=pltpu.CompilerParams(has_side_effects=_EFFECT),
    )(*bufs, send_sems, recv_sems, after)


_SEM =pl.BlockSpec(memory_space=pltpu.SEMAPHORE)
_ANY = pl.BlockSpec(memory_space=pl.ANY)
_TOKEN = pl.BlockSpec(memory_space=pltpu.VMEM)
_EFFECT = pltpu.SideEffectType.DATAFLOW_SIDE_EFFECTING


def _hbm(a):
    return pltpu.with_memory_space_constraint(a, pltpu.HBM)


def _chip_copies(p_ref, land_ref, send_sems, recv_sems):
    x, y, c = _place()
    return [pltpu.make_async_remote_copy(
        src_ref=p_ref.at[2 * px + py], dst_ref=land_ref.at[k], send_sem=send_sems.at[k], recv_sem=recv_sems.at[k],
        device_id=(px, py, c), device_id_type=pl.DeviceIdType.MESH) for k, (px, py) in enumerate(_other_chips(x, y))]


def scatter_chips_start(name, p):
    _, rows, cols = p.shape

    def body(p_ref, land_ref, send_sems, recv_sems, p_thru, land_thru, token):
        for cp in _chip_copies(p_ref, land_ref, send_sems, recv_sems):
            cp.start()
        token[...] = jnp.zeros_like(token)

    return pl.pallas_call(
        body, name=name,
        out_shape=(pltpu.SemaphoreType.DMA((3,)), pltpu.SemaphoreType.DMA((3,)), pltpu.HBM(p.shape, p.dtype),
                   pltpu.HBM((3, rows, cols), p.dtype), jax.ShapeDtypeStruct((SUBLANE, LANE), F32)),
        in_specs=(_HBM, _HBM), out_specs=(_SEM, _SEM, _HBM, _HBM, _TOKEN), input_output_aliases={0: 2, 1: 3},
        compiler_params=pltpu.CompilerParams(has_side_effects=_EFFECT),
    )(_hbm(p), _hbm(lax.empty((3, rows, cols), p.dtype)))


def scatter_chips_wait(name, send_sems, recv_sems, p_thru, land_thru, after):
    def body(p_ref, land_ref, send_sems, recv_sems, after_ref, p_out, land_out):
        for cp in _chip_copies(p_ref, land_ref, send_sems, recv_sems):
            cp.wait_send()
            cp.wait_recv()

    return pl.pallas_call(
        body, name=name,
        out_shape=(pltpu.HBM(p_thru.shape, p_thru.dtype), pltpu.HBM(land_thru.shape, land_thru.dtype)),
        in_specs=(_HBM, _HBM, _SEM, _SEM, _ANY), out_specs=(_HBM, _HBM), input_output_aliases={0: 0, 1: 1},
        compiler_params=pltpu.CompilerParams(has_side_effects=_EFFECT),
    )(p_thru, land_thru, send_sems, recv_sems, after)


def _pair_copies(g_ref, land_ref, send_sems, recv_sems):
    x, y, c = _place()
    return [pltpu.make_async_remote_copy(
        src_ref=g_ref.at[k], dst_ref=land_ref.at[k], send_sem=send_sems.at[k], recv_sem=recv_sems.at[k],
        device_id=(x, y, 1 - c), device_id_type=pl.DeviceIdType.MESH) for k in range(N_DEV // 2)]


def pair_start(name, g, after, carry=None):
    n = g.shape[0]

    def body(g_ref, land_ref, after_ref, *rest):
        send_sems, recv_sems = rest[-5 if carry is None else -6:][:2]
        for cp in _pair_copies(g_ref, land_ref, send_sems, recv_sems):
            cp.start()
        token = rest[-1 if carry is None else -2]
        token[...] = jnp.zeros_like(token)

    extra = () if carry is None else (carry,)
    out = pl.pallas_call(
        body, name=name,
        out_shape=(pltpu.SemaphoreType.DMA((n,)), pltpu.SemaphoreType.DMA((n,)), pltpu.HBM(g.shape, g.dtype),
                   pltpu.HBM(g.shape, g.dtype), jax.ShapeDtypeStruct((SUBLANE, LANE), F32),
                   *[pltpu.HBM(c.shape, c.dtype) for c in extra]),
        in_specs=(_HBM, _HBM, _ANY, *(_HBM,) * len(extra)), out_specs=(_SEM, _SEM, _HBM, _HBM, _TOKEN, *(_HBM,) * len(extra)),
        input_output_aliases={0: 2, 1: 3, **({3: 5} if extra else {})},
        compiler_params=pltpu.CompilerParams(has_side_effects=_EFFECT),
    )(_hbm(g), _hbm(lax.empty(g.shape, g.dtype)), after, *[_hbm(c) for c in extra])
    return out if carry is None else (out[:5], out[5])


def pair_wait(name, state, after):
    send_sems, recv_sems, g_thru, land_thru, _ = state

    def body(g_ref, land_ref, send_sems, recv_sems, after_ref, g_out, land_out):
        for cp in _pair_copies(g_ref, land_ref, send_sems, recv_sems):
            cp.wait_send()
            cp.wait_recv()

    return pl.pallas_call(
        body, name=name,
        out_shape=(pltpu.HBM(g_thru.shape, g_thru.dtype), pltpu.HBM(land_thru.shape, land_thru.dtype)),
        in_specs=(_HBM, _HBM, _SEM, _SEM, _ANY), out_specs=(_HBM, _HBM), input_output_aliases={0: 0, 1: 1},
        compiler_params=pltpu.CompilerParams(has_side_effects=_EFFECT),
    )(g_thru, land_thru, send_sems, recv_sems, after)[1]


def reduce_scatter_wait(tag, state, after):
    send_sems, recv_sems, p_thru, land_thru, _ = state
    return scatter_chips_wait("rs_wait_" + tag, send_sems, recv_sems, p_thru, land_thru, after)


_SMALL = ("g_pre_mix", "g_post_mix", "g_pre_ffn", "g_post_ffn", "g_attn_grp", "g_conv_grp", "g_lru_grp",
          "dw_conv_w", "dw_conv_b", "conv_ln_g", "conv_ln_b", "lru_conv_w", "lru_conv_b",
          "lru_w_a", "lru_b_a", "lru_w_i", "lru_b_i", "lru_lambda")
_COL_SHARDED_SMALL = ("dw_conv_w", "lru_conv_w")
_BIG = ("w_in", "w_out", "w_gate", "w_up", "w_down")
_TRANSPOSED = ("w_gate", "w_up")
_ALL = ("w_in", "w_out", "g_pre_mix", "g_post_mix", "g_pre_ffn", "g_post_ffn", "g_attn_grp", "g_conv_grp", "g_lru_grp",
        "dw_conv_w", "dw_conv_b", "conv_ln_g", "conv_ln_b", "lru_conv_w", "lru_conv_b", "lru_w_a", "lru_b_a",
        "lru_w_i", "lru_b_i", "lru_lambda", "w_gate", "w_up", "w_down")


def _pack(arrays):
    flat = jnp.concatenate([a.reshape(-1) for a in arrays])
    pad = (-flat.shape[0]) % (PACK_ROWS * LANE)
    return jnp.pad(flat, (0, pad)).reshape(-1, LANE)


def _unpack(packed, shapes):
    flat = packed.reshape(-1)
    out, pos = [], 0
    for s in shapes:
        n = math.prod(s)
        out.append(flat[pos:pos + n].reshape(s))
        pos += n
    return out


def kernel(x, w_in, w_out, g_pre_mix, g_post_mix, g_pre_ffn, g_post_ffn, g_attn_grp, g_conv_grp, g_lru_grp, dw_conv_w, dw_conv_b, conv_ln_g, conv_ln_b, lru_conv_w, lru_conv_b, lru_w_a, lru_b_a, lru_w_i, lru_b_i, lru_lambda, w_gate, w_up, w_down, loss_target, m_w_in, m_w_out, m_g_pre_mix, m_g_post_mix, m_g_pre_ffn, m_g_post_ffn, m_g_attn_grp, m_g_conv_grp, m_g_lru_grp, m_dw_conv_w, m_dw_conv_b, m_conv_ln_g, m_conv_ln_b, m_lru_conv_w, m_lru_conv_b, m_lru_w_a, m_lru_b_a, m_lru_w_i, m_lru_b_i, m_lru_lambda, m_w_gate, m_w_up, m_w_down, v_w_in, v_w_out, v_g_pre_mix, v_g_post_mix, v_g_pre_ffn, v_g_post_ffn, v_g_attn_grp, v_g_conv_grp, v_g_lru_grp, v_dw_conv_w, v_dw_conv_b, v_conv_ln_g, v_conv_ln_b, v_lru_conv_w, v_lru_conv_b, v_lru_w_a, v_lru_b_a, v_lru_w_i, v_lru_b_i, v_lru_lambda, v_w_gate, v_w_up, v_w_down):
    env = dict(locals())
    wts = {n: env[n] for n in _ALL}
    mom = {n: env["m_" + n] for n in _ALL}
    var = {n: env["v_" + n] for n in _ALL}
    for group in (wts, mom, var):
        for n in _TRANSPOSED:
            group[n] = jnp.swapaxes(group[n], 1, 2)

    depth = w_in.shape[0]
    h = x[0]
    target = loss_target[0]
    t, d = h.shape
    attn_w = d // 2
    n_heads = attn_w // HEAD_DIM
    cc = d // 4
    wl = d // 4
    conv_col, lru_col = 3 * attn_w, 3 * attn_w + 2 * cc
    me = 4 * lax.axis_index("x") + 2 * lax.axis_index("y") + lax.axis_index("c")
    me_s = me.astype(jnp.int32).reshape(1)
    chip_s = (2 * lax.axis_index("x") + lax.axis_index("y")).astype(jnp.int32).reshape(1)
    core_s = lax.axis_index("c").astype(jnp.int32).reshape(1)

    n_taps = DW_LEN + LRU_LEN
    taps = jnp.concatenate([dw_conv_w, lru_conv_w], axis=1).reshape(depth * n_taps, cc // N_DEV)
    taps = all_gather("ag_taps", taps, me_s)
    taps = jnp.moveaxis(taps.reshape(N_DEV, depth, n_taps, cc // N_DEV), 0, 2).reshape(depth, n_taps, cc)
    dw_full, lcw_full = taps[:, :DW_LEN], taps[:, DW_LEN:]

    def vec(a, l):
        return a[l].reshape(1, -1)

    ag_state, started = [], taps
    for l in range(depth):
        ag_state.append(gather_start(f"ag_start_{l}", [place_own(wts[n][l], me_s, BF16) for n in _BIG], started))
        started = ag_state[l][-1]
    started = started[0:1, 0:1]

    saved = []
    u1 = rms_pre(h, vec(g_pre_mix, 0) + started)
    loss_sum = dh = dbr = None
    groups = ((0, 1), (1, 1), (2, 2), (4, 1))

    def issue(l, g, carry):
        first, count = groups[g]
        landed = gather_wait(f"ag_wait_{l}_{g}", ag_state[l], first, count, carry)
        return forward_start(f"ag_fwd_start_{l}_{g}", landed, carry)

    def collect(l, g, state, behind, wg):
        first, count = groups[g]
        wg.update(zip(_BIG[first:first + count], forward_wait(f"ag_fwd_wait_{l}_{g}", state, behind)))

    pending, u1 = issue(0, 0, u1)
    for l in range(depth):
        wg = {}
        collect(l, 0, pending, u1, wg)
        pending, u1 = issue(l, 1, u1)
        wa_b, wi_b = lru_w_a[l].astype(BF16), lru_w_i[l].astype(BF16)
        proj = mm_proj(u1, wg["w_in"])
        y_attn = attn_fwd(proj, n_heads)
        cpre, y_conv = conv_fwd(proj, conv_col, cc, dw_full[l], vec(dw_conv_b, l), vec(conv_ln_g, l), vec(conv_ln_b, l))
        hs, y_lru = lru_fwd(proj, lru_col, wl, lcw_full[l], vec(lru_conv_b, l), wa_b, vec(lru_b_a, l), wi_b,
                            vec(lru_b_i, l), vec(lru_lambda, l))
        mixed = mix_fwd(y_attn, y_conv, y_lru, vec(g_attn_grp, l), vec(g_conv_grp, l), vec(g_lru_grp, l))
        collect(l, 1, pending, mixed, wg)
        pending, mixed = issue(l, 2, mixed)
        wg["w_out"] = wg["w_out"].reshape(attn_w + cc + wl, d)
        o = mm_plain("mm_out", mixed, wg["w_out"], "nn", F32)
        h2, u2 = res_norm(h, o, vec(g_post_mix, l), vec(g_pre_ffn, l))
        collect(l, 2, pending, u2, wg)
        pending, u2 = issue(l, 3, u2)
        gt, up, f = ffn_up(u2, wg["w_gate"], wg["w_up"])
        collect(l, 3, pending, f, wg)
        if l + 1 < depth:
            pending, f = issue(l + 1, 0, f)
        dn = mm_down(f, wg["w_down"])
        saved.append(dict(wg=wg, wa_b=wa_b, wi_b=wi_b, h=h, u1=u1, proj=proj, y_attn=y_attn, cpre=cpre, y_conv=y_conv,
                          hs=hs, y_lru=y_lru, mixed=mixed, o=o, h2=h2, u2=u2, gt=gt, up=up, f=f, dn=dn))
        if l + 1 < depth:
            h, u1 = res_norm(h2, dn, vec(g_post_ffn, l), vec(g_pre_mix, l + 1))
        else:
            loss_sum, dh, dbr, dg_post_ffn = final_loss(h2, dn, vec(g_post_ffn, l), target)

    loss = lax.psum(0.5 * loss_sum[0, 0] / d, MESH_AXES)

    small = {n: [None] * depth for n in _SMALL}
    rs_state = {n: [None] * depth for n in _BIG}
    after = dbr
    for l in reversed(range(depth)):
        s = saved[l]
        wg = s["wg"]
        small["g_post_ffn"][l] = dg_post_ffn
        dgt, dup = ffn_bwd(dbr, wg["w_down"], s["gt"], s["up"])
        ffn_grads = (("w_down", "down", s["f"], dbr), ("w_gate", "gate", dgt, s["u2"]), ("w_up", "up", dup, s["u2"]))
        pairs = {}
        for n, tag, a, g in ffn_grads:
            theirs = mm_dw_half("mm_dw_" + tag, "rows", a, g, core_s, True)
            pairs[n] = pair_start(f"pair_start_{tag}_{l}", theirs, after)
            after = pairs[n][-1]
        for n, tag, a, g in ffn_grads:
            recv = pair_wait(f"pair_wait_{tag}_{l}", pairs[n], after)
            after = mm_dw_half("mm_dw_" + tag, "rows", a, g, core_s, False, add=recv)
            rs_state[n][l] = scatter_chips_start(f"rs_start_{tag}_{l}", after)
        du2 = mm_dx_ffn(dgt, wg["w_gate"], dup, wg["w_up"])
        started = sum(rs_state[n][l][-1][0:1, 0:1] for n in ("w_down", "w_gate", "w_up"))
        dh2, small["g_pre_ffn"][l], do, small["g_post_mix"][l] = norm_bwd(
            dh, du2, s["h2"], vec(g_pre_ffn, l) + started, (s["o"], vec(g_post_mix, l)))
        theirs = mm_dw_half("mm_dw_out", "take", s["mixed"], do, core_s, True)
        pairs["w_out"], do = pair_start(f"pair_start_out_{l}", theirs, rs_state["w_up"][l][-1], carry=do)
        dmixed = mm_plain("mm_dmixed", do, wg["w_out"], "nt", F32)
        recv = pair_wait(f"pair_wait_out_{l}", pairs["w_out"], dmixed)
        rs_state["w_out"][l] = scatter_chips_start(
            f"rs_start_out_{l}", mm_dw_half("mm_dw_out", "take", s["mixed"], do, core_s, False, add=recv))
        (dya, dc, dyl, small["g_attn_grp"][l], small["g_conv_grp"][l], small["g_lru_grp"][l],
         small["conv_ln_g"][l], small["conv_ln_b"][l]) = mix_bwd(
            dmixed, s["y_attn"], s["y_conv"], s["y_lru"], s["cpre"],
            vec(g_attn_grp, l) + rs_state["w_out"][l][-1][0:1, 0:1], vec(g_conv_grp, l),
            vec(g_lru_grp, l), vec(conv_ln_g, l), vec(conv_ln_b, l))
        dq, dk, dv = attn_bwd(s["proj"], dya, n_heads)
        dvg, small["dw_conv_w"][l], small["dw_conv_b"][l] = conv_bwd(s["proj"], conv_col, cc, dc, dw_full[l])
        (dxy, small["lru_conv_w"][l], small["lru_conv_b"][l], small["lru_w_a"][l], small["lru_b_a"][l],
         small["lru_w_i"][l], small["lru_b_i"][l], small["lru_lambda"][l]) = lru_bwd(
            s["proj"], lru_col, wl, s["hs"], dyl, lcw_full[l], vec(lru_conv_b, l), s["wa_b"], vec(lru_b_a, l),
            s["wi_b"], vec(lru_b_i, l), vec(lru_lambda, l))
        dproj = jnp.concatenate([dq, dk, dv, dvg, dxy], axis=1)
        theirs = mm_dw_half("mm_dw_in", "cols", s["u1"], dproj, core_s, True)
        pairs["w_in"], dproj = pair_start(f"pair_start_in_{l}", theirs, rs_state["w_out"][l][-1], carry=dproj)
        du1 = mm_dx_cols("mm_dx_in", dproj, wg["w_in"])
        recv = pair_wait(f"pair_wait_in_{l}", pairs["w_in"], du1)
        after = mm_dw_half("mm_dw_in", "cols", s["u1"], dproj, core_s, False, add=recv)
        rs_state["w_in"][l] = scatter_chips_start(f"rs_start_in_{l}", after)
        g_pre = vec(g_pre_mix, l) + rs_state["w_in"][l][-1][0:1, 0:1]
        if l > 0:
            p = saved[l - 1]
            dh, small["g_pre_mix"][l], dbr, dg_post_ffn = norm_bwd(
                dh2, du1, s["h"], g_pre, (p["dn"], vec(g_post_ffn, l - 1)))
        else:
            dh, small["g_pre_mix"][l] = norm_bwd(dh2, du1, s["h"], g_pre)
    grad_x = dh[None]

    small_shapes = [(depth,) + tuple(wts[n].shape[1:]) if n not in _COL_SHARDED_SMALL
                    else (depth, wts[n].shape[1], cc) for n in _SMALL]
    part = _pack([a for n in _SMALL for a in small[n]])
    small_state = gather_start("ag_small_start", [place_own(part, me_s, F32)], dh)

    grads, delta, new_m, new_v = {}, {}, {}, {}
    behind = small_state[-1]
    for n in reversed(_BIG):
        parts = [reduce_scatter_wait(f"{n[2:]}_{l}", rs_state[n][l], behind) for l in range(depth)]
        shape = wts[n].shape
        _, rows, cols = parts[0][0].shape
        view = (depth, rows, cols)
        g, dl, mn, vn = adam_big("adam_" + n, wts[n].reshape(view), mom[n].reshape(view), var[n].reshape(view),
                                 parts, chip_s)
        grads[n], delta[n], new_m[n], new_v[n] = (a.reshape(shape) for a in (g, dl, mn, vn))
        behind = g
    for group in (grads, delta, new_m, new_v):
        for n in _TRANSPOSED:
            group[n] = jnp.swapaxes(group[n], 1, 2)

    gathered, = gather_finish("ag_small_finish", gather_wait("ag_small_wait", small_state, 0, 1, behind))
    g_small = _unpack(sum_parts(gathered), small_shapes)
    for n, g in zip(_SMALL, g_small):
        if n in _COL_SHARDED_SMALL:
            g = lax.dynamic_slice_in_dim(g, me * (cc // N_DEV), cc // N_DEV, axis=2)
        grads[n] = g
    local_shapes = [tuple(wts[n].shape) for n in _SMALL]
    d_small, m_small, v_small = adam_small(
        _pack([wts[n] for n in _SMALL]), _pack([mom[n] for n in _SMALL]), _pack([var[n] for n in _SMALL]),
        _pack([grads[n] for n in _SMALL]))
    delta.update(zip(_SMALL, _unpack(d_small, local_shapes)))
    new_m.update(zip(_SMALL, _unpack(m_small, local_shapes)))
    new_v.update(zip(_SMALL, _unpack(v_small, local_shapes)))

    return (loss, grad_x, *[grads[n] for n in _ALL], *[delta[n] for n in _ALL],
            *[new_m[n] for n in _ALL], *[new_v[n] for n in _ALL])
```

```python
import functools
import math

import jax
import jax.numpy as jnp
from jax import lax
from jax.experimental import pallas as pl
from jax.experimental.pallas import tpu as pltpu

F32 = jnp.float32
BF16 = jnp.bfloat16

N_DEV = 8
EPS = 1e-6
HEAD_DIM = 128
DW_LEN = 31
LRU_LEN = 4
LRU_BLOCKS = 4
LRU_C = 8.0
ATT_TQ = 512
ATT_TK = 512
ATT_SUM = 256
ATT_HEADS = 2
ROW_T = 256
CONV_HALO = 32
LRU_HALO = 8
LANE = 128
SUBLANE = 8
WIDE_TILE = 2048
PACK_ROWS = 512
VMEM_LIMIT = 56 * 1024 * 1024

ADAM_LR = 0.001
ADAM_B1 = 0.9
ADAM_B2 = 0.999
ADAM_EPS = 1e-08
ADAM_WD = 0.01
ADAM_STEP = 10

MESH_AXES = ("x", "y", "c")
_DIMS = {
    "nn": (((1,), (0,)), ((), ())),
    "nt": (((1,), (1,)), ((), ())),
    "tn": (((0,), (0,)), ((), ())),
}


def _params(n_axes):
    return pltpu.CompilerParams(
        dimension_semantics=("arbitrary",) * n_axes, vmem_limit_bytes=VMEM_LIMIT)


def _dot(a, b, mode="nn"):
    return lax.dot_general(a, b, _DIMS[mode], preferred_element_type=F32)


def _sigmoid(x):
    return 1.0 / (1.0 + jnp.exp(-x))


def _softplus(x):
    return jnp.maximum(x, 0.0) + jnp.log(1.0 + jnp.exp(-jnp.abs(x)))


def _neg_expm1(x):
    series = x * (1.0 + x * (0.5 + x * (1.0 / 6 + x * (1.0 / 24 + x * (1.0 / 120 + x * (1.0 / 720))))))
    return jnp.where(x > -0.25, -series, 1.0 - jnp.exp(x))


_GELU_C = math.sqrt(2.0 / math.pi)


def _gelu_and_grad(x):
    inner = _GELU_C * (x + 0.044715 * x * x * x)
    t = jnp.tanh(inner)
    val = 0.5 * x * (1.0 + t)
    grad = 0.5 * (1.0 + t) + 0.5 * x * (1.0 - t * t) * _GELU_C * (1.0 + 3 * 0.044715 * x * x)
    return val, grad


def _rms_stats(x):
    r = lax.rsqrt(jnp.mean(x * x, axis=-1, keepdims=True) + EPS)
    return x * r, r


def _rms_bwd(dy, x, g):
    xn, r = _rms_stats(x)
    dxn = dy * g
    dx = r * (dxn - xn * jnp.mean(dxn * xn, axis=-1, keepdims=True))
    return dx, jnp.sum(dy * xn, axis=0, keepdims=True)


def _row_spec(tr, width, col=0):
    return pl.BlockSpec((tr, width), lambda i, col=col: (i, col))


def _vec_spec(width):
    return pl.BlockSpec((1, width), lambda i: (0, 0))


def _matmul(name, mode, operands, in_specs, out_shape, out_spec, grid):
    npairs = len(operands) // 2
    nk = grid[2]
    assert nk == 1 or out_shape.dtype == F32

    def body(*refs):
        o_ref = refs[2 * npairs]

        def partial():
            acc = None
            for p in range(npairs):
                d = _dot(refs[2 * p][...], refs[2 * p + 1][...], mode)
                acc = d if acc is None else acc + d
            return acc

        if nk == 1:
            o_ref[...] = partial().astype(o_ref.dtype)
        else:
            k = pl.program_id(2)

            @pl.when(k == 0)
            def _():
                o_ref[...] = partial()

            @pl.when(k > 0)
            def _():
                o_ref[...] += partial()

    return pl.pallas_call(
        body, name=name, grid=grid, in_specs=in_specs, out_specs=out_spec, out_shape=out_shape,
        compiler_params=_params(3),
    )(*operands)


def _tile(n, t):
    if n <= t:
        return n
    return max(k for k in range(SUBLANE, t + 1, SUBLANE) if n % k == 0)


def mm_proj(u, w):
    t, d = u.shape
    nblk, _, nb = w.shape
    tm = _tile(t, WIDE_TILE)
    return _matmul(
        "mm_proj", "nn", (u, w),
        [pl.BlockSpec((tm, d), lambda j, i, k: (i, 0)), pl.BlockSpec((None, d, nb), lambda j, i, k: (j, 0, 0))],
        jax.ShapeDtypeStruct((t, nblk * nb), F32), pl.BlockSpec((tm, nb), lambda j, i, k: (i, j)),
        (nblk, t // tm, 1))


def mm_plain(name, a, b, mode, out_dtype):
    if mode == "nn":
        (m, kk), n = a.shape, b.shape[1]
    elif mode == "nt":
        (m, kk), n = a.shape, b.shape[0]
    else:
        (kk, m), n = a.shape, b.shape[1]
    tm, tn = _tile(m, 1024), _tile(n, 1024)
    a_spec = (pl.BlockSpec((kk, tm), lambda i, j, k: (0, i)) if mode == "tn"
              else pl.BlockSpec((tm, kk), lambda i, j, k: (i, 0)))
    b_spec = (pl.BlockSpec((tn, kk), lambda i, j, k: (j, 0)) if mode == "nt"
              else pl.BlockSpec((kk, tn), lambda i, j, k: (0, j)))
    return _matmul(
        name, mode, (a, b), [a_spec, b_spec],
        jax.ShapeDtypeStruct((m, n), out_dtype), pl.BlockSpec((tm, tn), lambda i, j, k: (i, j)),
        (m // tm, n // tn, 1))


def mm_down(f, w):
    nblk, t, fb = f.shape
    d = w.shape[2]
    tm, tn = _tile(t, 1024), _tile(d, WIDE_TILE)
    return _matmul(
        "mm_down", "nn", (f, w),
        [pl.BlockSpec((None, tm, fb), lambda i, j, k: (k, i, 0)), pl.BlockSpec((None, fb, tn), lambda i, j, k: (k, 0, j))],
        jax.ShapeDtypeStruct((t, d), F32), pl.BlockSpec((tm, tn), lambda i, j, k: (i, j)),
        (t // tm, d // tn, nblk))


def mm_dw_half(name, kind, a, g, core, of_sibling, add=None):
    half = N_DEV // 2
    t = g.shape[0]

    def pick(k, core_ref):
        s = 1 - core_ref[0] if of_sibling else core_ref[0]
        return 2 * k + s

    if kind == "rows":
        rows, cols = a.shape[2], g.shape[1]
        tr, tc = rows, _tile(cols, WIDE_TILE)
        a_spec = pl.BlockSpec((None, t, rows), lambda k, i, n, core_ref: (pick(k, core_ref), 0, 0))
        g_spec = pl.BlockSpec((t, tc), lambda k, i, n, core_ref: (0, n))
    elif kind == "cols":
        rows, cols = a.shape[1], g.shape[1] // N_DEV
        tr, tc = _tile(rows, WIDE_TILE), cols
        a_spec = pl.BlockSpec((t, tr), lambda k, i, n, core_ref: (0, i))
        g_spec = pl.BlockSpec((t, cols), lambda k, i, n, core_ref: (0, pick(k, core_ref)))
    else:
        rows, cols = a.shape[1] // N_DEV, g.shape[1]
        tr, tc = rows, _tile(cols, WIDE_TILE)
        a_spec = pl.BlockSpec((t, rows), lambda k, i, n, core_ref: (0, pick(k, core_ref)))
        g_spec = pl.BlockSpec((t, tc), lambda k, i, n, core_ref: (0, n))
    o_spec = pl.BlockSpec((None, tr, tc), lambda k, i, n, core_ref: (k, i, n))

    def body(core_ref, a_ref, g_ref, *rest):
        acc = _dot(a_ref[...], g_ref[...], "tn")
        if add is not None:
            acc = acc + rest[0][...].astype(F32)
        rest[-1][...] = acc.astype(BF16)

    return pl.pallas_call(
        body, name=name,
        grid_spec=pltpu.PrefetchScalarGridSpec(
            num_scalar_prefetch=1, grid=(half, rows // tr, cols // tc),
            in_specs=[a_spec, g_spec] + ([o_spec] if add is not None else []), out_specs=o_spec),
        out_shape=jax.ShapeDtypeStruct((half, rows, cols), BF16), compiler_params=_params(3),
    )(core, a, g, *(() if add is None else (add,)))


def mm_dx_cols(name, g, w):
    t = g.shape[0]
    nblk, d, nb = w.shape
    tm, tn = _tile(t, 1024), _tile(d, WIDE_TILE)
    return _matmul(
        name, "nt", (g, w),
        [pl.BlockSpec((tm, nb), lambda i, j, k: (i, k)), pl.BlockSpec((None, tn, nb), lambda i, j, k: (k, j, 0))],
        jax.ShapeDtypeStruct((t, d), F32), pl.BlockSpec((tm, tn), lambda i, j, k: (i, j)),
        (t // tm, d // tn, nblk))


def mm_dx_ffn(dgt, wg, dup, wu):
    nblk, t, fb = dgt.shape
    d = wg.shape[2]
    tm, tn = _tile(t, 1024), _tile(d, WIDE_TILE)
    a_spec = pl.BlockSpec((None, tm, fb), lambda i, j, k: (k, i, 0))
    b_spec = pl.BlockSpec((None, fb, tn), lambda i, j, k: (k, 0, j))
    return _matmul(
        "mm_dx_ffn", "nn", (dgt, wg, dup, wu), [a_spec, b_spec, a_spec, b_spec],
        jax.ShapeDtypeStruct((t, d), F32), pl.BlockSpec((tm, tn), lambda i, j, k: (i, j)),
        (t // tm, d // tn, nblk))


def ffn_up(u, wg, wu):
    t, d = u.shape
    nblk, fb, _ = wg.shape
    tm = _tile(t, 512)

    def body(u_ref, wg_ref, wu_ref, dgt_ref, dup_ref, f_ref):
        for r0 in range(0, t, tm):
            rows = pl.ds(r0, tm)
            uu = u_ref[rows, :]
            gt = _dot(uu, wg_ref[...], "nt")
            up = _dot(uu, wu_ref[...], "nt")
            s = _sigmoid(gt)
            silu = gt * s
            dgt_ref[rows, :] = (up * s * (1.0 + gt * (1.0 - s))).astype(BF16)
            dup_ref[rows, :] = silu.astype(BF16)
            f_ref[rows, :] = (silu * up).astype(BF16)

    w_spec = pl.BlockSpec((None, fb, d), lambda j: (j, 0, 0))
    o_spec = pl.BlockSpec((None, t, fb), lambda j: (j, 0, 0))
    return pl.pallas_call(
        body, name="ffn_up", grid=(nblk,),
        in_specs=[pl.BlockSpec((t, d), lambda j: (0, 0)), w_spec, w_spec],
        out_specs=[o_spec, o_spec, o_spec],
        out_shape=[jax.ShapeDtypeStruct((nblk, t, fb), BF16)] * 3,
        compiler_params=_params(1),
    )(u, wg, wu)


def ffn_bwd(dd, wd, f_gate, f_up):
    t, d = dd.shape
    nblk, fb, _ = wd.shape
    tm = _tile(t, 512)

    def body(dd_ref, wd_ref, fg_ref, fu_ref, dgt_ref, dup_ref):
        for r0 in range(0, t, tm):
            rows = pl.ds(r0, tm)
            df = _dot(dd_ref[rows, :], wd_ref[...], "nt")
            dgt_ref[rows, :] = (df * fg_ref[rows, :].astype(F32)).astype(BF16)
            dup_ref[rows, :] = (df * fu_ref[rows, :].astype(F32)).astype(BF16)

    s_spec = pl.BlockSpec((None, t, fb), lambda j: (j, 0, 0))
    return pl.pallas_call(
        body, name="ffn_bwd", grid=(nblk,),
        in_specs=[pl.BlockSpec((t, d), lambda j: (0, 0)), pl.BlockSpec((None, fb, d), lambda j: (j, 0, 0)),
                  s_spec, s_spec],
        out_specs=[s_spec, s_spec],
        out_shape=[jax.ShapeDtypeStruct((nblk, t, fb), BF16)] * 2,
        compiler_params=_params(1),
    )(dd, wd, f_gate, f_up)


def rms_pre(h, g):
    t, d = h.shape
    tr = _tile(t, ROW_T)

    def body(h_ref, g_ref, o_ref):
        xn, _ = _rms_stats(h_ref[...])
        o_ref[...] = (xn * g_ref[...]).astype(BF16)

    return pl.pallas_call(
        body, name="rms_pre", grid=(t // tr,),
        in_specs=[_row_spec(tr, d), _vec_spec(d)], out_specs=_row_spec(tr, d),
        out_shape=jax.ShapeDtypeStruct((t, d), BF16), compiler_params=_params(1),
    )(h, g)


def res_norm(h, o, g_post, g_pre):
    t, d = h.shape
    tr = _tile(t, ROW_T)

    def body(h_ref, o_ref, gpo_ref, gpr_ref, h2_ref, u_ref):
        on, _ = _rms_stats(o_ref[...])
        h2 = h_ref[...] + on * gpo_ref[...]
        h2_ref[...] = h2
        hn, _ = _rms_stats(h2)
        u_ref[...] = (hn * gpr_ref[...]).astype(BF16)

    return pl.pallas_call(
        body, name="res_norm", grid=(t // tr,),
        in_specs=[_row_spec(tr, d), _row_spec(tr, d), _vec_spec(d), _vec_spec(d)],
        out_specs=[_row_spec(tr, d), _row_spec(tr, d)],
        out_shape=[jax.ShapeDtypeStruct((t, d), F32), jax.ShapeDtypeStruct((t, d), BF16)],
        compiler_params=_params(1),
    )(h, o, g_post, g_pre)


def final_loss(h2, dbr, g_post, target):
    t, d = h2.shape
    tr = _tile(t, ROW_T)

    def body(h2_ref, d_ref, g_ref, tg_ref, loss_ref, dy_ref, dd_ref, dg_ref):
        i = pl.program_id(0)

        @pl.when(i == 0)
        def _():
            loss_ref[...] = jnp.zeros_like(loss_ref)
            dg_ref[...] = jnp.zeros_like(dg_ref)

        x = d_ref[...]
        g = g_ref[...]
        xn, _ = _rms_stats(x)
        diff = h2_ref[...] + xn * g - tg_ref[...]
        loss_ref[...] += jnp.sum(jnp.sum(diff * diff, axis=1, keepdims=True), axis=0, keepdims=True)
        dy = diff * (1.0 / d)
        dy_ref[...] = dy
        dx, dg = _rms_bwd(dy, x, g)
        dd_ref[...] = dx.astype(BF16)
        dg_ref[...] += dg

    return pl.pallas_call(
        body, name="final_loss", grid=(t // tr,),
        in_specs=[_row_spec(tr, d), _row_spec(tr, d), _vec_spec(d), _row_spec(tr, d)],
        out_specs=[pl.BlockSpec((1, 1), lambda i: (0, 0)), _row_spec(tr, d), _row_spec(tr, d), _vec_spec(d)],
        out_shape=[jax.ShapeDtypeStruct((1, 1), F32), jax.ShapeDtypeStruct((t, d), F32),
                   jax.ShapeDtypeStruct((t, d), BF16), jax.ShapeDtypeStruct((1, d), F32)],
        compiler_params=_params(1),
    )(h2, dbr, g_post, target)


def norm_bwd(dh_out, du, h_in, g_pre, prev=None):
    t, d = h_in.shape
    tr = _tile(t, ROW_T)
    with_prev = prev is not None

    def body(*refs):
        if with_prev:
            dho_ref, du_ref, h_ref, gpr_ref, br_ref, gpo_ref, dh_ref, dgpr_ref, dbr_ref, dgpo_ref = refs
        else:
            dho_ref, du_ref, h_ref, gpr_ref, dh_ref, dgpr_ref = refs
        i = pl.program_id(0)

        @pl.when(i == 0)
        def _():
            dgpr_ref[...] = jnp.zeros_like(dgpr_ref)
            if with_prev:
                dgpo_ref[...] = jnp.zeros_like(dgpo_ref)

        dx, dg = _rms_bwd(du_ref[...], h_ref[...], gpr_ref[...])
        dh = dho_ref[...] + dx
        dh_ref[...] = dh
        dgpr_ref[...] += dg
        if with_prev:
            dbr, dg2 = _rms_bwd(dh, br_ref[...], gpo_ref[...])
            dbr_ref[...] = dbr.astype(BF16)
            dgpo_ref[...] += dg2

    row, vec = _row_spec(tr, d), _vec_spec(d)
    in_specs = [row, row, row, vec] + ([row, vec] if with_prev else [])
    out_specs = [row, vec] + ([row, vec] if with_prev else [])
    out_shape = [jax.ShapeDtypeStruct((t, d), F32), jax.ShapeDtypeStruct((1, d), F32)]
    if with_prev:
        out_shape += [jax.ShapeDtypeStruct((t, d), BF16), jax.ShapeDtypeStruct((1, d), F32)]
    args = (dh_out, du, h_in, g_pre) + (tuple(prev) if with_prev else ())
    return pl.pallas_call(
        body, name="norm_bwd_chain" if with_prev else "norm_bwd_first", grid=(t // tr,),
        in_specs=in_specs, out_specs=out_specs, out_shape=out_shape, compiler_params=_params(1),
    )(*args)


def mix_fwd(ya, yc, yl, ga, gc, gl):
    t, wa = ya.shape
    wc, wl = yc.shape[1], yl.shape[1]
    tr = _tile(t, ROW_T)

    def body(ya_ref, yc_ref, yl_ref, ga_ref, gc_ref, gl_ref, o_ref):
        o_ref[:, pl.ds(0, wa)] = (_rms_stats(ya_ref[...])[0] * ga_ref[...]).astype(BF16)
        o_ref[:, pl.ds(wa, wc)] = (_rms_stats(yc_ref[...])[0] * gc_ref[...]).astype(BF16)
        o_ref[:, pl.ds(wa + wc, wl)] = (_rms_stats(yl_ref[...])[0] * gl_ref[...]).astype(BF16)

    return pl.pallas_call(
        body, name="mix_fwd", grid=(t // tr,),
        in_specs=[_row_spec(tr, wa), _row_spec(tr, wc), _row_spec(tr, wl), _vec_spec(wa), _vec_spec(wc), _vec_spec(wl)],
        out_specs=_row_spec(tr, wa + wc + wl),
        out_shape=jax.ShapeDtypeStruct((t, wa + wc + wl), BF16), compiler_params=_params(1),
    )(ya, yc, yl, ga, gc, gl)


def mix_bwd(dmixed, ya, yc, yl, cpre, ga, gc, gl, lng, lnb):
    t, wa = ya.shape
    wc, wl = yc.shape[1], yl.shape[1]
    tr = _tile(t, ROW_T)

    def body(dm_ref, ya_ref, yc_ref, yl_ref, c_ref, ga_ref, gc_ref, gl_ref, lg_ref, lb_ref,
             dya_ref, dc_ref, dyl_ref, dga_ref, dgc_ref, dgl_ref, dlg_ref, dlb_ref):
        i = pl.program_id(0)

        @pl.when(i == 0)
        def _():
            for r in (dga_ref, dgc_ref, dgl_ref, dlg_ref, dlb_ref):
                r[...] = jnp.zeros_like(r)

        dya, dga = _rms_bwd(dm_ref[:, pl.ds(0, wa)], ya_ref[...], ga_ref[...])
        dya_ref[...] = dya
        dga_ref[...] += dga
        dyl, dgl = _rms_bwd(dm_ref[:, pl.ds(wa + wc, wl)], yl_ref[...], gl_ref[...])
        dyl_ref[...] = dyl
        dgl_ref[...] += dgl
        dyc, dgc = _rms_bwd(dm_ref[:, pl.ds(wa, wc)], yc_ref[...], gc_ref[...])
        dgc_ref[...] += dgc
        c = c_ref[...]
        xc = c - jnp.mean(c, axis=-1, keepdims=True)
        rstd = lax.rsqrt(jnp.mean(xc * xc, axis=-1, keepdims=True) + EPS)
        xhat = xc * rstd
        ln = xhat * lg_ref[...] + lb_ref[...]
        s = _sigmoid(ln)
        dln = dyc * s * (1.0 + ln * (1.0 - s))
        dlg_ref[...] += jnp.sum(dln * xhat, axis=0, keepdims=True)
        dlb_ref[...] += jnp.sum(dln, axis=0, keepdims=True)
        dxh = dln * lg_ref[...]
        dc_ref[...] = rstd * (dxh - jnp.mean(dxh, axis=-1, keepdims=True)
                              - xhat * jnp.mean(dxh * xhat, axis=-1, keepdims=True))

    return pl.pallas_call(
        body, name="mix_bwd", grid=(t // tr,),
        in_specs=[_row_spec(tr, wa + wc + wl), _row_spec(tr, wa), _row_spec(tr, wc), _row_spec(tr, wl), _row_spec(tr, wc),
                  _vec_spec(wa), _vec_spec(wc), _vec_spec(wl), _vec_spec(wc), _vec_spec(wc)],
        out_specs=[_row_spec(tr, wa), _row_spec(tr, wc), _row_spec(tr, wl),
                   _vec_spec(wa), _vec_spec(wc), _vec_spec(wl), _vec_spec(wc), _vec_spec(wc)],
        out_shape=[jax.ShapeDtypeStruct((t, wa), F32), jax.ShapeDtypeStruct((t, wc), F32), jax.ShapeDtypeStruct((t, wl), F32),
                   jax.ShapeDtypeStruct((1, wa), F32), jax.ShapeDtypeStruct((1, wc), F32), jax.ShapeDtypeStruct((1, wl), F32),
                   jax.ShapeDtypeStruct((1, wc), F32), jax.ShapeDtypeStruct((1, wc), F32)],
        compiler_params=_params(1),
    )(dmixed, ya, yc, yl, cpre, ga, gc, gl, lng, lnb)


def _hi_lo(x):
    hi = x.astype(BF16)
    return hi, (x - hi.astype(F32)).astype(BF16)


def _lane_sums(x, tri, reverse):
    nsub = x.shape[1] // tri.shape[0]
    order = range(nsub - 1, -1, -1) if reverse else range(nsub)
    parts, beyond = {}, None
    for b in order:
        blk = x[:, b * tri.shape[0]:(b + 1) * tri.shape[0]]
        hi, lo = _hi_lo(blk)
        c = _dot(hi, tri) + _dot(lo, tri)
        parts[b] = c if beyond is None else c + beyond
        tot = jnp.sum(blk, axis=1, keepdims=True)
        beyond = tot if beyond is None else beyond + tot
    return jnp.concatenate([parts[b] for b in range(nsub)], axis=1), beyond


def _att_strip(qb, kt, thresh, scale, diff, tri_gt):
    z = _dot(qb, kt, "nt") * scale
    sp = _softplus(z)
    mask = diff < thresh
    later, total = _lane_sums(jnp.where(mask, -sp, 0.0), tri_gt, True)
    return z, sp, mask, total, (z - sp) + later


def _att_consts(tq, tk):
    diff = lax.broadcasted_iota(jnp.int32, (tq, tk), 1) - lax.broadcasted_iota(jnp.int32, (tq, tk), 0)
    cb = min(tk, ATT_SUM)
    row = lax.broadcasted_iota(jnp.int32, (cb, cb), 0)
    col = lax.broadcasted_iota(jnp.int32, (cb, cb), 1)
    return diff, (row > col).astype(BF16), (row < col).astype(BF16)


def attn_fwd(proj, n_heads):
    t = proj.shape[0]
    tq, tk = _tile(t, ATT_TQ), _tile(t, ATT_TK)
    hp = ATT_HEADS
    wd = hp * HEAD_DIM
    scale = HEAD_DIM ** -0.5

    def body(q_ref, k_ref, v_ref, o_ref, kb_ref, vb_ref, acc_ref):
        kb_ref[...] = k_ref[...].astype(BF16)
        vb_ref[...] = v_ref[...].astype(BF16)
        diff, tri_gt, _ = _att_consts(tq, tk)

        def qblock(i, _):
            q0 = pl.multiple_of(i * tq, tq)
            heads = [pl.ds(h * HEAD_DIM, HEAD_DIM) for h in range(hp)]
            qbs = [q_ref[pl.ds(q0, tq), hs].astype(BF16) for hs in heads]
            acc_ref[...] = jnp.zeros_like(acc_ref)
            n_strips = (q0 + tq + tk - 1) // tk

            def strip(jj, runs):
                k0 = pl.multiple_of((n_strips - 1 - jj) * tk, tk)
                out = []
                for h, hs in enumerate(heads):
                    _, _, mask, total, logw = _att_strip(qbs[h], kb_ref[pl.ds(k0, tk), hs], q0 - k0, scale, diff, tri_gt)
                    w = jnp.where(mask, jnp.exp(logw + runs[h]), 0.0)
                    acc_ref[:, hs] += _dot(w.astype(BF16), vb_ref[pl.ds(k0, tk), hs])
                    out.append(runs[h] + total)
                return tuple(out)

            lax.fori_loop(0, n_strips, strip, tuple(jnp.zeros((tq, 1), F32) for _ in range(hp)))
            o_ref[pl.ds(q0, tq), :] = acc_ref[...]
            return 0

        lax.fori_loop(0, t // tq, qblock, 0)

    def col_spec(base):
        return pl.BlockSpec((t, wd), lambda h, base=base: (0, base + h))

    ng = n_heads // hp
    return pl.pallas_call(
        body, name="attn_fwd", grid=(ng,),
        in_specs=[col_spec(0), col_spec(ng), col_spec(2 * ng)],
        out_specs=col_spec(0),
        out_shape=jax.ShapeDtypeStruct((t, n_heads * HEAD_DIM), F32),
        scratch_shapes=[pltpu.VMEM((t, wd), BF16), pltpu.VMEM((t, wd), BF16), pltpu.VMEM((tq, wd), F32)],
        compiler_params=_params(1),
    )(proj, proj, proj)


def attn_bwd(proj, dy, n_heads):
    t = proj.shape[0]
    tq, tk = _tile(t, ATT_TQ), _tile(t, ATT_TK)
    hp = ATT_HEADS
    wd = hp * HEAD_DIM
    scale = HEAD_DIM ** -0.5

    def body(q_ref, k_ref, v_ref, dy_ref, dq_ref, dk_ref, dv_ref, qb_ref, kb_ref, vb_ref, dob_ref, dk_acc, dv_acc,
             dq_acc, run_s):
        qb_ref[...] = q_ref[...].astype(BF16)
        kb_ref[...] = k_ref[...].astype(BF16)
        vb_ref[...] = v_ref[...].astype(BF16)
        dob_ref[...] = dy_ref[...].astype(BF16)
        dk_acc[...] = jnp.zeros_like(dk_acc)
        dv_acc[...] = jnp.zeros_like(dv_acc)
        diff, tri_gt, tri_lt = _att_consts(tq, tk)

        def qblock(i, _):
            q0 = pl.multiple_of(i * tq, tq)
            heads = [pl.ds(h * HEAD_DIM, HEAD_DIM) for h in range(hp)]
            qbs = [qb_ref[pl.ds(q0, tq), hs] for hs in heads]
            dobs = [dob_ref[pl.ds(q0, tq), hs] for hs in heads]
            dq_acc[...] = jnp.zeros_like(dq_acc)
            n_strips = (q0 + tq + tk - 1) // tk

            def sweep(jj, runs):
                si = n_strips - 1 - jj
                k0 = pl.multiple_of(si * tk, tk)
                out = []
                for h, hs in enumerate(heads):
                    sp = _softplus(_dot(qbs[h], kb_ref[pl.ds(k0, tk), hs], "nt") * scale)
                    run_s[h, si] = runs[h]
                    out.append(runs[h] + jnp.sum(jnp.where(diff < q0 - k0, -sp, 0.0), axis=1, keepdims=True))
                return tuple(out)

            zero = tuple(jnp.zeros((tq, 1), F32) for _ in range(hp))
            lax.fori_loop(0, n_strips, sweep, zero)

            def strip(si, gsums):
                k0 = pl.multiple_of(si * tk, tk)
                out = []
                for h, hs in enumerate(heads):
                    kt = kb_ref[pl.ds(k0, tk), hs]
                    vt = vb_ref[pl.ds(k0, tk), hs]
                    z, sp, mask, _, logw = _att_strip(qbs[h], kt, q0 - k0, scale, diff, tri_gt)
                    w = jnp.where(mask, jnp.exp(logw + run_s[h, si]), 0.0)
                    g = w * _dot(dobs[h], vt, "nt")
                    before, gtot = _lane_sums(g, tri_lt, False)
                    sig = jnp.exp(z - sp)
                    dz = jnp.where(mask, g * (1.0 - sig) - (before + gsums[h]) * sig, 0.0) * scale
                    dzb = dz.astype(BF16)
                    dk_acc[pl.ds(k0, tk), hs] += _dot(dzb, qbs[h], "tn")
                    dv_acc[pl.ds(k0, tk), hs] += _dot(w.astype(BF16), dobs[h], "tn")
                    dq_acc[:, hs] += _dot(dzb, kt)
                    out.append(gsums[h] + gtot)
                return tuple(out)

            lax.fori_loop(0, n_strips, strip, zero)
            dq_ref[pl.ds(q0, tq), :] = dq_acc[...].astype(BF16)
            return 0

        lax.fori_loop(0, t // tq, qblock, 0)
        dk_ref[...] = dk_acc[...].astype(BF16)
        dv_ref[...] = dv_acc[...].astype(BF16)

    def col_spec(base):
        return pl.BlockSpec((t, wd), lambda h, base=base: (0, base + h))

    ng = n_heads // hp
    return pl.pallas_call(
        body, name="attn_bwd", grid=(ng,),
        in_specs=[col_spec(0), col_spec(ng), col_spec(2 * ng), col_spec(0)],
        out_specs=[col_spec(0), col_spec(0), col_spec(0)],
        out_shape=[jax.ShapeDtypeStruct((t, n_heads * HEAD_DIM), BF16)] * 3,
        scratch_shapes=[pltpu.VMEM((t, wd), BF16)] * 4 + [pltpu.VMEM((t, wd), F32)] * 2
        + [pltpu.VMEM((tq, wd), F32), pltpu.VMEM((hp, t // tk, tq, 1), F32)],
        compiler_params=_params(1),
    )(proj, proj, proj, dy)


def _glu_halo(vc, gc, vp, gp, ubuf, i, tt, halo):
    uprev = vp[pl.ds(tt - halo, halo), :] * _sigmoid(gp[pl.ds(tt - halo, halo), :])
    ubuf[pl.ds(0, halo), :] = jnp.where(i > 0, uprev, 0.0)
    ubuf[pl.ds(halo, tt), :] = vc[...] * _sigmoid(gc[...])


def conv_fwd(proj, col0, cc, w, b, lng, lnb):
    t = proj.shape[0]
    tt = _tile(t, ROW_T)
    vi, gi = col0 // cc, col0 // cc + 1
    off = CONV_HALO - (DW_LEN - 1)

    def body(vc, gc, vp, gp, w_ref, b_ref, lg_ref, lb_ref, c_ref, y_ref, ubuf):
        i = pl.program_id(0)
        _glu_halo(vc, gc, vp, gp, ubuf, i, tt, CONV_HALO)
        for ch in range(cc // LANE):
            sl = pl.ds(ch * LANE, LANE)
            acc = jnp.zeros((tt, LANE), F32) + b_ref[:, sl]
            for tap in range(DW_LEN):
                acc = acc + w_ref[pl.ds(tap, 1), sl] * ubuf[pl.ds(off + tap, tt), sl]
            c_ref[:, sl] = acc
        c = c_ref[...]
        xc = c - jnp.mean(c, axis=-1, keepdims=True)
        ln = xc * lax.rsqrt(jnp.mean(xc * xc, axis=-1, keepdims=True) + EPS) * lg_ref[...] + lb_ref[...]
        y_ref[...] = ln * _sigmoid(ln)

    cur = lambda c: pl.BlockSpec((tt, cc), lambda i, c=c: (i, c))
    prev = lambda c: pl.BlockSpec((tt, cc), lambda i, c=c: (jnp.maximum(i - 1, 0), c))
    return pl.pallas_call(
        body, name="conv_fwd", grid=(t // tt,),
        in_specs=[cur(vi), cur(gi), prev(vi), prev(gi), pl.BlockSpec((DW_LEN, cc), lambda i: (0, 0)),
                  _vec_spec(cc), _vec_spec(cc), _vec_spec(cc)],
        out_specs=[_row_spec(tt, cc), _row_spec(tt, cc)],
        out_shape=[jax.ShapeDtypeStruct((t, cc), F32)] * 2,
        scratch_shapes=[pltpu.VMEM((CONV_HALO + tt, cc), F32)],
        compiler_params=_params(1),
    )(proj, proj, proj, proj, w, b, lng, lnb)


def conv_bwd(proj, col0, cc, dc, w):
    t = proj.shape[0]
    tt = _tile(t, ROW_T)
    nt = t // tt
    vi, gi = col0 // cc, col0 // cc + 1
    off = CONV_HALO - (DW_LEN - 1)

    def body(vc, gc, vp, gp, dcc, dcn, w_ref, dvg_ref, dw_ref, db_ref, ubuf, dbuf):
        i = pl.program_id(0)

        @pl.when(i == 0)
        def _():
            dw_ref[...] = jnp.zeros_like(dw_ref)
            db_ref[...] = jnp.zeros_like(db_ref)

        _glu_halo(vc, gc, vp, gp, ubuf, i, tt, CONV_HALO)
        dbuf[pl.ds(0, tt), :] = dcc[...]
        dbuf[pl.ds(tt, CONV_HALO), :] = jnp.where(i < nt - 1, dcn[pl.ds(0, CONV_HALO), :], 0.0)
        db_ref[...] += jnp.sum(dcc[...], axis=0, keepdims=True)
        for ch in range(cc // LANE):
            sl = pl.ds(ch * LANE, LANE)
            dcv = dbuf[pl.ds(0, tt), sl]
            du = jnp.zeros((tt, LANE), F32)
            for tap in range(DW_LEN):
                du = du + w_ref[pl.ds(tap, 1), sl] * dbuf[pl.ds(DW_LEN - 1 - tap, tt), sl]
                dw_ref[pl.ds(tap, 1), sl] += jnp.sum(dcv * ubuf[pl.ds(off + tap, tt), sl], axis=0, keepdims=True)
            s = _sigmoid(gc[:, sl])
            val = vc[:, sl]
            dvg_ref[:, sl] = (du * s).astype(BF16)
            dvg_ref[:, pl.ds(cc + ch * LANE, LANE)] = (du * val * s * (1.0 - s)).astype(BF16)

    cur = lambda c: pl.BlockSpec((tt, cc), lambda i, c=c: (i, c))
    prev = lambda c: pl.BlockSpec((tt, cc), lambda i, c=c: (jnp.maximum(i - 1, 0), c))
    return pl.pallas_call(
        body, name="conv_bwd", grid=(nt,),
        in_specs=[cur(vi), cur(gi), prev(vi), prev(gi), _row_spec(tt, cc),
                  pl.BlockSpec((tt, cc), lambda i: (jnp.minimum(i + 1, nt - 1), 0)),
                  pl.BlockSpec((DW_LEN, cc), lambda i: (0, 0))],
        out_specs=[_row_spec(tt, 2 * cc), pl.BlockSpec((DW_LEN, cc), lambda i: (0, 0)), _vec_spec(cc)],
        out_shape=[jax.ShapeDtypeStruct((t, 2 * cc), BF16), jax.ShapeDtypeStruct((DW_LEN, cc), F32),
                   jax.ShapeDtypeStruct((1, cc), F32)],
        scratch_shapes=[pltpu.VMEM((CONV_HALO + tt, cc), F32), pltpu.VMEM((tt + CONV_HALO, cc), F32)],
        compiler_params=_params(1),
    )(proj, proj, proj, proj, dc, dc, w)


def _lru_gates(xbuf, cw_ref, cb_ref, wa_ref, ba_ref, wi_ref, bi_ref, lam_ref, tt, wl):
    bd = wl // LRU_BLOCKS
    xr = jnp.zeros((tt, wl), F32) + cb_ref[...]
    for tap in range(LRU_LEN):
        xr = xr + cw_ref[pl.ds(tap, 1), :] * xbuf[pl.ds(LRU_HALO - (LRU_LEN - 1) + tap, tt), :]
    xb = xr.astype(BF16)
    ga = jnp.concatenate([_dot(xb[:, n * bd:(n + 1) * bd], wa_ref[n]) for n in range(LRU_BLOCKS)], axis=1) + ba_ref[...]
    gi = jnp.concatenate([_dot(xb[:, n * bd:(n + 1) * bd], wi_ref[n]) for n in range(LRU_BLOCKS)], axis=1) + bi_ref[...]
    r = _sigmoid(ga)
    ig = _sigmoid(gi)
    spl = _softplus(-lam_ref[...])
    log_a = -LRU_C * r * spl
    a = jnp.exp(log_a)
    m = jnp.sqrt(_neg_expm1(2.0 * log_a))
    return xr, xb, r, ig, spl, a, m


def _group_scan(a8, b8, reverse):
    rid = lax.broadcasted_iota(jnp.int32, a8.shape, 0)
    aa, bb = a8, b8
    for dist in (1, 2, 4):
        shift = SUBLANE - dist if reverse else dist
        a_sh = pltpu.roll(aa, shift, 0)
        b_sh = pltpu.roll(bb, shift, 0)
        valid = (rid < SUBLANE - dist) if reverse else (rid >= dist)
        bb = jnp.where(valid, aa * b_sh + bb, bb)
        aa = jnp.where(valid, aa * a_sh, aa)
    return aa, bb


def _pick_row(x8, r):
    rid = lax.broadcasted_iota(jnp.int32, x8.shape, 0)
    return jnp.sum(jnp.where(rid == r, x8, 0.0), axis=0, keepdims=True)


def lru_fwd(proj, col0, wl, cw, cb, wa, ba, wi, bi, lam):
    t = proj.shape[0]
    tt = _tile(t, ROW_T)
    xi, yi = col0 // wl, col0 // wl + 1

    def body(xc, xp, ry, cw_ref, cb_ref, wa_ref, ba_ref, wi_ref, bi_ref, lam_ref, hs_ref, y_ref,
             xbuf, a_s, b_s, hcar):
        i = pl.program_id(0)

        @pl.when(i == 0)
        def _():
            hcar[...] = jnp.zeros_like(hcar)

        xbuf[pl.ds(0, LRU_HALO), :] = jnp.where(i > 0, xp[pl.ds(tt - LRU_HALO, LRU_HALO), :], 0.0)
        xbuf[pl.ds(LRU_HALO, tt), :] = xc[...]
        xr, _, _, ig, _, a, m = _lru_gates(xbuf, cw_ref, cb_ref, wa_ref, ba_ref, wi_ref, bi_ref, lam_ref, tt, wl)
        a_s[...] = a
        b_s[...] = m * ig * xr

        def group(gidx, h):
            r0 = pl.multiple_of(gidx * SUBLANE, SUBLANE)
            aa, bb = _group_scan(a_s[pl.ds(r0, SUBLANE), :], b_s[pl.ds(r0, SUBLANE), :], False)
            h8 = aa * h + bb
            hs_ref[pl.ds(r0, SUBLANE), :] = h8
            return _pick_row(h8, SUBLANE - 1)

        hcar[...] = lax.fori_loop(0, tt // SUBLANE, group, hcar[...])
        gel, _ = _gelu_and_grad(ry[...])
        y_ref[...] = hs_ref[...] * gel

    cur = lambda c: pl.BlockSpec((tt, wl), lambda i, c=c: (i, c))
    full = lambda shape: pl.BlockSpec(shape, lambda i: (0,) * len(shape))
    return pl.pallas_call(
        body, name="lru_fwd", grid=(t // tt,),
        in_specs=[cur(xi), pl.BlockSpec((tt, wl), lambda i: (jnp.maximum(i - 1, 0), xi)), cur(yi),
                  full((LRU_LEN, wl)), _vec_spec(wl), full(wa.shape), _vec_spec(wl), full(wi.shape), _vec_spec(wl),
                  _vec_spec(wl)],
        out_specs=[_row_spec(tt, wl), _row_spec(tt, wl)],
        out_shape=[jax.ShapeDtypeStruct((t, wl), F32)] * 2,
        scratch_shapes=[pltpu.VMEM((LRU_HALO + tt, wl), F32), pltpu.VMEM((tt, wl), F32), pltpu.VMEM((tt, wl), F32),
                        pltpu.VMEM((1, wl), F32)],
        compiler_params=_params(1),
    )(proj, proj, proj, cw, cb, wa, ba, wi, bi, lam)


def lru_bwd(proj, col0, wl, hs, dy, cw, cb, wa, ba, wi, bi, lam):
    t = proj.shape[0]
    tt = _tile(t, ROW_T)
    nt = t // tt
    xi, yi = col0 // wl, col0 // wl + 1
    bd = wl // LRU_BLOCKS

    def body(xc, xp, ry, hc, hp, dy_ref, cw_ref, cb_ref, wa_ref, ba_ref, wi_ref, bi_ref, lam_ref,
             dxy_ref, dcw_ref, dcb_ref, dwa_ref, dba_ref, dwi_ref, dbi_ref, dlam_ref,
             xbuf, hbuf, abuf, e_s, dh_s, dxbuf, dhcar):
        i = pl.program_id(0)
        first = i == 0

        @pl.when(first)
        def _():
            for r in (dcw_ref, dcb_ref, dwa_ref, dba_ref, dwi_ref, dbi_ref, dlam_ref, dhcar):
                r[...] = jnp.zeros_like(r)
            abuf[pl.ds(tt, LRU_HALO), :] = jnp.zeros((LRU_HALO, wl), F32)
            dxbuf[pl.ds(tt, LRU_HALO), :] = jnp.zeros((LRU_HALO, wl), F32)

        has_prev = i < nt - 1
        xbuf[pl.ds(0, LRU_HALO), :] = jnp.where(has_prev, xp[pl.ds(tt - LRU_HALO, LRU_HALO), :], 0.0)
        xbuf[pl.ds(LRU_HALO, tt), :] = xc[...]
        hbuf[pl.ds(0, LRU_HALO), :] = jnp.where(has_prev, hp[pl.ds(tt - LRU_HALO, LRU_HALO), :], 0.0)
        hbuf[pl.ds(LRU_HALO, tt), :] = hc[...]
        xr, xb, r, ig, spl, a, m = _lru_gates(xbuf, cw_ref, cb_ref, wa_ref, ba_ref, wi_ref, bi_ref, lam_ref, tt, wl)
        gel, dgel = _gelu_and_grad(ry[...])
        dyv = dy_ref[...]
        e_s[...] = dyv * gel
        dxy_ref[:, pl.ds(wl, wl)] = (dyv * hc[...] * dgel).astype(BF16)
        abuf[pl.ds(0, tt), :] = a
        a_next = abuf[pl.ds(1, tt), :]
        dh_s[...] = a_next

        def group(it, dh_in):
            r0 = pl.multiple_of((tt // SUBLANE - 1 - it) * SUBLANE, SUBLANE)
            aa, bb = _group_scan(dh_s[pl.ds(r0, SUBLANE), :], e_s[pl.ds(r0, SUBLANE), :], True)
            dh8 = aa * dh_in + bb
            dh_s[pl.ds(r0, SUBLANE), :] = dh8
            return _pick_row(dh8, 0)

        dhcar[...] = lax.fori_loop(0, tt // SUBLANE, group, dhcar[...])
        abuf[pl.ds(tt, LRU_HALO), :] = a[0:LRU_HALO, :]
        dh = dh_s[...]
        h_m1 = hbuf[pl.ds(LRU_HALO - 1, tt), :]
        dlog_a = dh * h_m1 * a - dh * ig * xr * (a * a / m)
        dig = dh * m * xr
        dxr = dh * m * ig
        dga = dlog_a * (-LRU_C) * spl * r * (1.0 - r)
        dgi = dig * ig * (1.0 - ig)
        dlam_ref[...] += jnp.sum(dlog_a * r, axis=0, keepdims=True) * (LRU_C * _sigmoid(-lam_ref[...]))
        dba_ref[...] += jnp.sum(dga, axis=0, keepdims=True)
        dbi_ref[...] += jnp.sum(dgi, axis=0, keepdims=True)
        dgab = dga.astype(BF16)
        dgib = dgi.astype(BF16)
        back = []
        for n in range(LRU_BLOCKS):
            sl = slice(n * bd, (n + 1) * bd)
            dwa_ref[n] += _dot(xb[:, sl], dgab[:, sl], "tn")
            dwi_ref[n] += _dot(xb[:, sl], dgib[:, sl], "tn")
            back.append(_dot(dgab[:, sl], wa_ref[n], "nt") + _dot(dgib[:, sl], wi_ref[n], "nt"))
        dxr = dxr + jnp.concatenate(back, axis=1)
        dcb_ref[...] += jnp.sum(dxr, axis=0, keepdims=True)
        dxbuf[pl.ds(0, tt), :] = dxr
        drx = jnp.zeros((tt, wl), F32)
        for tap in range(LRU_LEN):
            drx = drx + cw_ref[pl.ds(tap, 1), :] * dxbuf[pl.ds(LRU_LEN - 1 - tap, tt), :]
            dcw_ref[pl.ds(tap, 1), :] += jnp.sum(
                dxr * xbuf[pl.ds(LRU_HALO - (LRU_LEN - 1) + tap, tt), :], axis=0, keepdims=True)
        dxbuf[pl.ds(tt, LRU_HALO), :] = dxr[0:LRU_HALO, :]
        dxy_ref[:, pl.ds(0, wl)] = drx.astype(BF16)

    rev = lambda c: pl.BlockSpec((tt, wl), lambda i, c=c: (nt - 1 - i, c))
    rev_prev = lambda c: pl.BlockSpec((tt, wl), lambda i, c=c: (jnp.maximum(nt - 2 - i, 0), c))
    full = lambda shape: pl.BlockSpec(shape, lambda i: (0,) * len(shape))
    vec = _vec_spec(wl)
    return pl.pallas_call(
        body, name="lru_bwd", grid=(nt,),
        in_specs=[rev(xi), rev_prev(xi), rev(yi), rev(0), rev_prev(0), rev(0),
                  full((LRU_LEN, wl)), vec, full(wa.shape), vec, full(wi.shape), vec, vec],
        out_specs=[pl.BlockSpec((tt, 2 * wl), lambda i: (nt - 1 - i, 0)), full((LRU_LEN, wl)), vec,
                   full(wa.shape), vec, full(wi.shape), vec, vec],
        out_shape=[jax.ShapeDtypeStruct((t, 2 * wl), BF16), jax.ShapeDtypeStruct((LRU_LEN, wl), F32),
                   jax.ShapeDtypeStruct((1, wl), F32), jax.ShapeDtypeStruct(wa.shape, F32),
                   jax.ShapeDtypeStruct((1, wl), F32), jax.ShapeDtypeStruct(wi.shape, F32),
                   jax.ShapeDtypeStruct((1, wl), F32), jax.ShapeDtypeStruct((1, wl), F32)],
        scratch_shapes=[pltpu.VMEM((LRU_HALO + tt, wl), F32), pltpu.VMEM((LRU_HALO + tt, wl), F32),
                        pltpu.VMEM((tt + LRU_HALO, wl), F32), pltpu.VMEM((tt, wl), F32), pltpu.VMEM((tt, wl), F32),
                        pltpu.VMEM((tt + LRU_HALO, wl), F32), pltpu.VMEM((1, wl), F32)],
        compiler_params=_params(1),
    )(proj, proj, proj, hs, hs, dy, cw, cb, wa, ba, wi, bi, lam)


def _adamw(w, g, m, v):
    m = ADAM_B1 * m + (1.0 - ADAM_B1) * g
    v = ADAM_B2 * v + (1.0 - ADAM_B2) * (g * g)
    m_hat = m / (1.0 - ADAM_B1 ** ADAM_STEP)
    v_hat = v / (1.0 - ADAM_B2 ** ADAM_STEP)
    delta = -ADAM_LR * (m_hat / (jnp.sqrt(v_hat) + ADAM_EPS) + ADAM_WD * w)
    return delta, m, v


def adam_big(name, w, m, v, parts, chip):
    n_layers, rows, cols = w.shape
    tr = _tile(rows, 128 if cols > 1024 else 256)
    nrt = rows // tr

    def body(chip_ref, *refs):
        w_ref, m_ref, v_ref = refs[:3]
        part_refs = refs[3:3 + 4 * n_layers]
        g_ref, d_ref, mo_ref, vo_ref = refs[3 + 4 * n_layers:]
        layer = pl.program_id(0)
        for l in range(n_layers):
            @pl.when(layer == l)
            def _(l=l):
                g = part_refs[4 * l][...].astype(F32)
                for p in range(1, 4):
                    g = g + part_refs[4 * l + p][...].astype(F32)
                delta, mn, vn = _adamw(w_ref[...], g, m_ref[...], v_ref[...])
                g_ref[...] = g
                d_ref[...] = delta
                mo_ref[...] = mn
                vo_ref[...] = vn

    wspec = pl.BlockSpec((None, tr, cols), lambda l, i, chip_ref: (l, i, 0))
    operands, in_specs = [w, m, v], [wspec, wspec, wspec]
    for l in range(n_layers):
        mine, recv = parts[l]
        operands.append(mine)
        in_specs.append(pl.BlockSpec(
            (None, tr, cols), lambda ll, i, chip_ref, l=l: (chip_ref[0], jnp.where(ll == l, i, 0), 0)))
        for p in range(3):
            operands.append(recv)
            in_specs.append(pl.BlockSpec(
                (None, tr, cols), lambda ll, i, chip_ref, l=l, p=p: (p, jnp.where(ll == l, i, 0), 0)))
    return pl.pallas_call(
        body, name=name,
        grid_spec=pltpu.PrefetchScalarGridSpec(
            num_scalar_prefetch=1, grid=(n_layers, nrt), in_specs=in_specs, out_specs=[wspec] * 4),
        out_shape=[jax.ShapeDtypeStruct(w.shape, F32)] * 4, compiler_params=_params(2),
    )(chip, *operands)


def adam_small(w, m, v, g):
    rows = w.shape[0]
    tr = _tile(rows, PACK_ROWS)

    def body(w_ref, m_ref, v_ref, g_ref, d_ref, mo_ref, vo_ref):
        delta, mn, vn = _adamw(w_ref[...], g_ref[...], m_ref[...], v_ref[...])
        d_ref[...] = delta
        mo_ref[...] = mn
        vo_ref[...] = vn

    spec = _row_spec(tr, LANE)
    return pl.pallas_call(
        body, name="adam_small", grid=(rows // tr,), in_specs=[spec] * 4, out_specs=[spec] * 3,
        out_shape=[jax.ShapeDtypeStruct(w.shape, F32)] * 3, compiler_params=_params(1),
    )(w, m, v, g)


def sum_parts(parts):
    _, rows, _ = parts.shape
    tr = _tile(rows, PACK_ROWS)

    def body(p_ref, o_ref):
        acc = p_ref[0]
        for k in range(1, N_DEV):
            acc = acc + p_ref[k]
        o_ref[...] = acc

    return pl.pallas_call(
        body, name="sum_parts", grid=(rows // tr,),
        in_specs=[pl.BlockSpec((N_DEV, tr, LANE), lambda i: (0, i, 0))], out_specs=_row_spec(tr, LANE),
        out_shape=jax.ShapeDtypeStruct((rows, LANE), F32), compiler_params=_params(1),
    )(parts)


def place_own(x, me, dtype):
    rows, cols = x.shape
    tr = _tile(rows, 256)

    def body(me_ref, x_ref, o_ref):
        o_ref[...] = x_ref[...].astype(dtype)

    return pl.pallas_call(
        body, name="place_own",
        grid_spec=pltpu.PrefetchScalarGridSpec(
            num_scalar_prefetch=1, grid=(rows // tr,),
            in_specs=[pl.BlockSpec((tr, cols), lambda i, me_ref: (i, 0))],
            out_specs=pl.BlockSpec((None, tr, cols), lambda i, me_ref: (me_ref[0], i, 0))),
        out_shape=jax.ShapeDtypeStruct((N_DEV, rows, cols), dtype), compiler_params=_params(1),
    )(me, x)


_HBM = pl.BlockSpec(memory_space=pltpu.HBM)


def _place():
    return lax.axis_index("x"), lax.axis_index("y"), lax.axis_index("c")


def _other_chips(x, y):
    return [(1 - x, y), (x, 1 - y), (1 - x, 1 - y)]


def all_gather(name, shard, me, dtype=None):
    def body(buf_ref, out_ref, send_sems, recv_sems):
        del buf_ref
        x, y, c = _place()
        mine, sibling = (x, y, c), (x, y, 1 - c)
        chips = _other_chips(x, y)

        def copy(k, block, to):
            slot = out_ref.at[4 * block[0] + 2 * block[1] + block[2]]
            return pltpu.make_async_remote_copy(
                src_ref=slot, dst_ref=slot, send_sem=send_sems.at[k], recv_sem=recv_sems.at[k],
                device_id=to, device_id_type=pl.DeviceIdType.MESH)

        first = [copy(0, mine, sibling)] + [copy(1 + j, mine, (*chip, c)) for j, chip in enumerate(chips)]
        for cp in first:
            cp.start()
        passed = [copy(4 + j, (*chip, c), sibling) for j, chip in enumerate(chips)]
        for j, chip in enumerate(chips):
            copy(1 + j, (*chip, c), mine).wait_recv()
            passed[j].start()
        copy(0, sibling, mine).wait_recv()
        for j, chip in enumerate(chips):
            copy(4 + j, (*chip, 1 - c), mine).wait_recv()
        for cp in first + passed:
            cp.wait_send()

    buf = place_own(shard, me, dtype or shard.dtype)
    return pl.pallas_call(
        body, name=name, out_shape=jax.ShapeDtypeStruct(buf.shape, buf.dtype),
        in_specs=[_HBM], out_specs=_HBM, input_output_aliases={0: 0},
        scratch_shapes=[pltpu.SemaphoreType.DMA((7,)), pltpu.SemaphoreType.DMA((7,))],
    )(buf)


def _own_block_copies(src_refs, dst_refs, send_sems, recv_sems, arrivals):
    x, y, c = _place()
    peers = [(x, y, 1 - c)] + [(*chip, c) for chip in _other_chips(x, y)]
    copies = []
    for b, (src, dst) in enumerate(zip(src_refs, dst_refs)):
        for k, peer in enumerate(peers):
            def copy(landing, b=b, k=k, peer=peer, src=src, dst=dst):
                return pltpu.make_async_remote_copy(
                    src_ref=src.at[4 * x + 2 * y + c], dst_ref=dst.at[landing],
                    send_sem=send_sems.at[4 * b + k], recv_sem=recv_sems.at[4 * b + k],
                    device_id=peer, device_id_type=pl.DeviceIdType.MESH)
            copies.append((copy(4 * x + 2 * y + c), copy(4 * peer[0] + 2 * peer[1] + peer[2]) if arrivals else None))
    return copies


def gather_start(name, bufs, after):
    n = len(bufs)

    def body(*refs):
        send_sems, recv_sems = refs[n + 1], refs[n + 2]
        thru = refs[n + 3:2 * n + 3]
        for send, _ in _own_block_copies(thru, thru, send_sems, recv_sems, False):
            send.start()
        refs[2 * n + 3][...] = jnp.zeros((SUBLANE, LANE), F32)

    return pl.pallas_call(
        body, name=name,
        out_shape=(pltpu.SemaphoreType.DMA((4 * n,)), pltpu.SemaphoreType.DMA((4 * n,)),
                   *[pltpu.HBM(b.shape, b.dtype) for b in bufs], jax.ShapeDtypeStruct((SUBLANE, LANE), F32)),
        in_specs=(*(_HBM,) * n, _ANY), out_specs=(_SEM, _SEM, *(_HBM,) * n, _TOKEN),
        input_output_aliases={b: 2 + b for b in range(n)},
        compiler_params=pltpu.CompilerParams(has_side_effects=_EFFECT),
    )(*[_hbm(b) for b in bufs], after)


def gather_wait(name, state, first, count, after):
    send_sems, recv_sems = state[:2]
    bufs = state[2 + first:2 + first + count]
    n = len(bufs)

    def body(*refs):
        ins = refs[:n]
        send_sems, recv_sems = refs[n], refs[n + 1]
        shift = 4 * first
        for send, arrival in _own_block_copies(
                ins, ins, send_sems.at[pl.ds(shift, 4 * n)], recv_sems.at[pl.ds(shift, 4 * n)], True):
            send.wait_send()
            arrival.wait_recv()

    return pl.pallas_call(
        body, name=name, out_shape=tuple(pltpu.HBM(b.shape, b.dtype) for b in bufs),
        in_specs=(*(_HBM,) * n, _SEM, _SEM, _ANY), out_specs=(_HBM,) * n,
        input_output_aliases={b: b for b in range(n)},
        compiler_params=pltpu.CompilerParams(has_side_effects=_EFFECT),
    )(*bufs, send_sems, recv_sems, after)


def _forward_copies(bufs, send_sems, recv_sems, arrivals):
    x, y, c = _place()
    copies = []
    for b, buf in enumerate(bufs):
        for k, chip in enumerate(_other_chips(x, y)):
            def copy(core, b=b, k=k, chip=chip, buf=buf):
                return pltpu.make_async_remote_copy(
                    src_ref=buf.at[4 * chip[0] + 2 * chip[1] + c], dst_ref=buf.at[4 * chip[0] + 2 * chip[1] + core],
                    send_sem=send_sems.at[3 * b + k], recv_sem=recv_sems.at[3 * b + k],
                    device_id=(x, y, 1 - c), device_id_type=pl.DeviceIdType.MESH)
            copies.append((copy(c), copy(1 - c) if arrivals else None))
    return copies


def gather_finish(name, bufs):
    n = len(bufs)

    def body(*refs):
        copies = _forward_copies(refs[n:2 * n], refs[2 * n], refs[2 * n + 1], True)
        for send, _ in copies:
            send.start()
        for send, arrival in copies:
            send.wait_send()
            arrival.wait_recv()

    return pl.pallas_call(
        body, name=name, out_shape=tuple(jax.ShapeDtypeStruct(b.shape, b.dtype) for b in bufs),
        in_specs=(_HBM,) * n, out_specs=(_HBM,) * n, input_output_aliases={b: b for b in range(n)},
        scratch_shapes=[pltpu.SemaphoreType.DMA((3 * n,)), pltpu.SemaphoreType.DMA((3 * n,))],
    )(*bufs)


def forward_start(name, bufs, carry):
    n = len(bufs)

    def body(*refs):
        send_sems, recv_sems = refs[n + 1], refs[n + 2]
        for send, _ in _forward_copies(refs[:n], send_sems, recv_sems, False):
            send.start()

    out = pl.pallas_call(
        body, name=name,
        out_shape=(pltpu.SemaphoreType.DMA((3 * n,)), pltpu.SemaphoreType.DMA((3 * n,)),
                   *[pltpu.HBM(b.shape, b.dtype) for b in bufs], pltpu.HBM(carry.shape, carry.dtype)),
        in_specs=(_HBM,) * (n + 1), out_specs=(_SEM, _SEM, *(_HBM,) * (n + 1)),
        input_output_aliases={b: 2 + b for b in range(n + 1)},
        compiler_params=pltpu.CompilerParams(has_side_effects=_EFFECT),
    )(*[_hbm(b) for b in bufs], _hbm(carry))
    return out[:-1], out[-1]


def forward_wait(name, state, after):
    send_sems, recv_sems, *bufs = state
    n = len(bufs)

    def body(*refs):
        for send, arrival in _forward_copies(refs[:n], refs[n], refs[n + 1], True):
            send.wait_send()
            arrival.wait_recv()

    return pl.pallas_call(
        body, name=name, out_shape=tuple(pltpu.HBM(b.shape, b.dtype) for b in bufs),
        in_specs=(*(_HBM,) * n, _SEM, _SEM, _ANY), out_specs=(_HBM,) * n,
        input_output_aliases={b: b for b in range(n)},
        compiler_params=pltpu.CompilerParams(has_side_effects=_EFFECT),
    )(*bufs, send_sems, recv_sems, after)


_SEM =pl.BlockSpec(memory_space=pltpu.SEMAPHORE)
_ANY = pl.BlockSpec(memory_space=pl.ANY)
_TOKEN = pl.BlockSpec(memory_space=pltpu.VMEM)
_EFFECT = pltpu.SideEffectType.DATAFLOW_SIDE_EFFECTING


def _hbm(a):
    return pltpu.with_memory_space_constraint(a, pltpu.HBM)


def _chip_copies(p_ref, land_ref, send_sems, recv_sems):
    x, y, c = _place()
    return [pltpu.make_async_remote_copy(
        src_ref=p_ref.at[2 * px + py], dst_ref=land_ref.at[k], send_sem=send_sems.at[k], recv_sem=recv_sems.at[k],
        device_id=(px, py, c), device_id_type=pl.DeviceIdType.MESH) for k, (px, py) in enumerate(_other_chips(x, y))]


def scatter_chips_start(name, p):
    _, rows, cols = p.shape

    def body(p_ref, land_ref, send_sems, recv_sems, p_thru, land_thru, token):
        for cp in _chip_copies(p_ref, land_ref, send_sems, recv_sems):
            cp.start()
        token[...] = jnp.zeros_like(token)

    return pl.pallas_call(
        body, name=name,
        out_shape=(pltpu.SemaphoreType.DMA((3,)), pltpu.SemaphoreType.DMA((3,)), pltpu.HBM(p.shape, p.dtype),
                   pltpu.HBM((3, rows, cols), p.dtype), jax.ShapeDtypeStruct((SUBLANE, LANE), F32)),
        in_specs=(_HBM, _HBM), out_specs=(_SEM, _SEM, _HBM, _HBM, _TOKEN), input_output_aliases={0: 2, 1: 3},
        compiler_params=pltpu.CompilerParams(has_side_effects=_EFFECT),
    )(_hbm(p), _hbm(lax.empty((3, rows, cols), p.dtype)))


def scatter_chips_wait(name, send_sems, recv_sems, p_thru, land_thru, after):
    def body(p_ref, land_ref, send_sems, recv_sems, after_ref, p_out, land_out):
        for cp in _chip_copies(p_ref, land_ref, send_sems, recv_sems):
            cp.wait_send()
            cp.wait_recv()

    return pl.pallas_call(
        body, name=name,
        out_shape=(pltpu.HBM(p_thru.shape, p_thru.dtype), pltpu.HBM(land_thru.shape, land_thru.dtype)),
        in_specs=(_HBM, _HBM, _SEM, _SEM, _ANY), out_specs=(_HBM, _HBM), input_output_aliases={0: 0, 1: 1},
        compiler_params=pltpu.CompilerParams(has_side_effects=_EFFECT),
    )(p_thru, land_thru, send_sems, recv_sems, after)


def _pair_copies(g_ref, land_ref, send_sems, recv_sems):
    x, y, c = _place()
    return [pltpu.make_async_remote_copy(
        src_ref=g_ref.at[k], dst_ref=land_ref.at[k], send_sem=send_sems.at[k], recv_sem=recv_sems.at[k],
        device_id=(x, y, 1 - c), device_id_type=pl.DeviceIdType.MESH) for k in range(N_DEV // 2)]


def pair_start(name, g, after, carry=None):
    n = g.shape[0]

    def body(g_ref, land_ref, after_ref, *rest):
        send_sems, recv_sems = rest[-5 if carry is None else -6:][:2]
        for cp in _pair_copies(g_ref, land_ref, send_sems, recv_sems):
            cp.start()
        token = rest[-1 if carry is None else -2]
        token[...] = jnp.zeros_like(token)

    extra = () if carry is None else (carry,)
    out = pl.pallas_call(
        body, name=name,
        out_shape=(pltpu.SemaphoreType.DMA((n,)), pltpu.SemaphoreType.DMA((n,)), pltpu.HBM(g.shape, g.dtype),
                   pltpu.HBM(g.shape, g.dtype), jax.ShapeDtypeStruct((SUBLANE, LANE), F32),
                   *[pltpu.HBM(c.shape, c.dtype) for c in extra]),
        in_specs=(_HBM, _HBM, _ANY, *(_HBM,) * len(extra)), out_specs=(_SEM, _SEM, _HBM, _HBM, _TOKEN, *(_HBM,) * len(extra)),
        input_output_aliases={0: 2, 1: 3, **({3: 5} if extra else {})},
        compiler_params=pltpu.CompilerParams(has_side_effects=_EFFECT),
    )(_hbm(g), _hbm(lax.empty(g.shape, g.dtype)), after, *[_hbm(c) for c in extra])
    return out if carry is None else (out[:5], out[5])


def pair_wait(name, state, after):
    send_sems, recv_sems, g_thru, land_thru, _ = state

    def body(g_ref, land_ref, send_sems, recv_sems, after_ref, g_out, land_out):
        for cp in _pair_copies(g_ref, land_ref, send_sems, recv_sems):
            cp.wait_send()
            cp.wait_recv()

    return pl.pallas_call(
        body, name=name,
        out_shape=(pltpu.HBM(g_thru.shape, g_thru.dtype), pltpu.HBM(land_thru.shape, land_thru.dtype)),
        in_specs=(_HBM, _HBM, _SEM, _SEM, _ANY), out_specs=(_HBM, _HBM), input_output_aliases={0: 0, 1: 1},
        compiler_params=pltpu.CompilerParams(has_side_effects=_EFFECT),
    )(g_thru, land_thru, send_sems, recv_sems, after)[1]


def reduce_scatter_wait(tag, state, after):
    send_sems, recv_sems, p_thru, land_thru, _ = state
    return scatter_chips_wait("rs_wait_" + tag, send_sems, recv_sems, p_thru, land_thru, after)


_SMALL = ("g_pre_mix", "g_post_mix", "g_pre_ffn", "g_post_ffn", "g_attn_grp", "g_conv_grp", "g_lru_grp",
          "dw_conv_w", "dw_conv_b", "conv_ln_g", "conv_ln_b", "lru_conv_w", "lru_conv_b",
          "lru_w_a", "lru_b_a", "lru_w_i", "lru_b_i", "lru_lambda")
_COL_SHARDED_SMALL = ("dw_conv_w", "lru_conv_w")
_BIG = ("w_in", "w_out", "w_gate", "w_up", "w_down")
_TRANSPOSED = ("w_gate", "w_up")
_ALL = ("w_in", "w_out", "g_pre_mix", "g_post_mix", "g_pre_ffn", "g_post_ffn", "g_attn_grp", "g_conv_grp", "g_lru_grp",
        "dw_conv_w", "dw_conv_b", "conv_ln_g", "conv_ln_b", "lru_conv_w", "lru_conv_b", "lru_w_a", "lru_b_a",
        "lru_w_i", "lru_b_i", "lru_lambda", "w_gate", "w_up", "w_down")


def _pack(arrays):
    flat = jnp.concatenate([a.reshape(-1) for a in arrays])
    pad = (-flat.shape[0]) % (PACK_ROWS * LANE)
    return jnp.pad(flat, (0, pad)).reshape(-1, LANE)


def _unpack(packed, shapes):
    flat = packed.reshape(-1)
    out, pos = [], 0
    for s in shapes:
        n = math.prod(s)
        out.append(flat[pos:pos + n].reshape(s))
        pos += n
    return out


def kernel(x, w_in, w_out, g_pre_mix, g_post_mix, g_pre_ffn, g_post_ffn, g_attn_grp, g_conv_grp, g_lru_grp, dw_conv_w, dw_conv_b, conv_ln_g, conv_ln_b, lru_conv_w, lru_conv_b, lru_w_a, lru_b_a, lru_w_i, lru_b_i, lru_lambda, w_gate, w_up, w_down, loss_target, m_w_in, m_w_out, m_g_pre_mix, m_g_post_mix, m_g_pre_ffn, m_g_post_ffn, m_g_attn_grp, m_g_conv_grp, m_g_lru_grp, m_dw_conv_w, m_dw_conv_b, m_conv_ln_g, m_conv_ln_b, m_lru_conv_w, m_lru_conv_b, m_lru_w_a, m_lru_b_a, m_lru_w_i, m_lru_b_i, m_lru_lambda, m_w_gate, m_w_up, m_w_down, v_w_in, v_w_out, v_g_pre_mix, v_g_post_mix, v_g_pre_ffn, v_g_post_ffn, v_g_attn_grp, v_g_conv_grp, v_g_lru_grp, v_dw_conv_w, v_dw_conv_b, v_conv_ln_g, v_conv_ln_b, v_lru_conv_w, v_lru_conv_b, v_lru_w_a, v_lru_b_a, v_lru_w_i, v_lru_b_i, v_lru_lambda, v_w_gate, v_w_up, v_w_down):
    env = dict(locals())
    wts = {n: env[n] for n in _ALL}
    mom = {n: env["m_" + n] for n in _ALL}
    var = {n: env["v_" + n] for n in _ALL}
    for group in (wts, mom, var):
        for n in _TRANSPOSED:
            group[n] = jnp.swapaxes(group[n], 1, 2)

    depth = w_in.shape[0]
    h = x[0]
    target = loss_target[0]
    t, d = h.shape
    attn_w = d // 2
    n_heads = attn_w // HEAD_DIM
    cc = d // 4
    wl = d // 4
    conv_col, lru_col = 3 * attn_w, 3 * attn_w + 2 * cc
    me = 4 * lax.axis_index("x") + 2 * lax.axis_index("y") + lax.axis_index("c")
    me_s = me.astype(jnp.int32).reshape(1)
    chip_s = (2 * lax.axis_index("x") + lax.axis_index("y")).astype(jnp.int32).reshape(1)
    core_s = lax.axis_index("c").astype(jnp.int32).reshape(1)

    n_taps = DW_LEN + LRU_LEN
    taps = jnp.concatenate([dw_conv_w, lru_conv_w], axis=1).reshape(depth * n_taps, cc // N_DEV)
    taps = all_gather("ag_taps", taps, me_s)
    taps = jnp.moveaxis(taps.reshape(N_DEV, depth, n_taps, cc // N_DEV), 0, 2).reshape(depth, n_taps, cc)
    dw_full, lcw_full = taps[:, :DW_LEN], taps[:, DW_LEN:]

    def vec(a, l):
        return a[l].reshape(1, -1)

    ag_state, started = [], taps
    for l in range(depth):
        ag_state.append(gather_start(f"ag_start_{l}", [place_own(wts[n][l], me_s, BF16) for n in _BIG], started))
        started = ag_state[l][-1]
    started = started[0:1, 0:1]

    saved = []
    u1 = rms_pre(h, vec(g_pre_mix, 0) + started)
    loss_sum = dh = dbr = None
    groups = ((0, 1), (1, 1), (2, 2), (4, 1))

    def issue(l, g, carry):
        first, count = groups[g]
        landed = gather_wait(f"ag_wait_{l}_{g}", ag_state[l], first, count, carry)
        return forward_start(f"ag_fwd_start_{l}_{g}", landed, carry)

    def collect(l, g, state, behind, wg):
        first, count = groups[g]
        wg.update(zip(_BIG[first:first + count], forward_wait(f"ag_fwd_wait_{l}_{g}", state, behind)))

    pending, u1 = issue(0, 0, u1)
    for l in range(depth):
        wg = {}
        collect(l, 0, pending, u1, wg)
        pending, u1 = issue(l, 1, u1)
        wa_b, wi_b = lru_w_a[l].astype(BF16), lru_w_i[l].astype(BF16)
        proj = mm_proj(u1, wg["w_in"])
        y_attn = attn_fwd(proj, n_heads)
        cpre, y_conv = conv_fwd(proj, conv_col, cc, dw_full[l], vec(dw_conv_b, l), vec(conv_ln_g, l), vec(conv_ln_b, l))
        hs, y_lru = lru_fwd(proj, lru_col, wl, lcw_full[l], vec(lru_conv_b, l), wa_b, vec(lru_b_a, l), wi_b,
                            vec(lru_b_i, l), vec(lru_lambda, l))
        mixed = mix_fwd(y_attn, y_conv, y_lru, vec(g_attn_grp, l), vec(g_conv_grp, l), vec(g_lru_grp, l))
        collect(l, 1, pending, mixed, wg)
        pending, mixed = issue(l, 2, mixed)
        wg["w_out"] = wg["w_out"].reshape(attn_w + cc + wl, d)
        o = mm_plain("mm_out", mixed, wg["w_out"], "nn", F32)
        h2, u2 = res_norm(h, o, vec(g_post_mix, l), vec(g_pre_ffn, l))
        collect(l, 2, pending, u2, wg)
        pending, u2 = issue(l, 3, u2)
        f_gate, f_up, f = ffn_up(u2, wg["w_gate"], wg["w_up"])
        collect(l, 3, pending, f, wg)
        if l + 1 < depth:
            pending, f = issue(l + 1, 0, f)
        dn = mm_down(f, wg["w_down"])
        saved.append(dict(wg=wg, wa_b=wa_b, wi_b=wi_b, h=h, u1=u1, proj=proj, y_attn=y_attn, cpre=cpre, y_conv=y_conv,
                          hs=hs, y_lru=y_lru, mixed=mixed, o=o, h2=h2, u2=u2, f_gate=f_gate, f_up=f_up, f=f, dn=dn))
        if l + 1 < depth:
            h, u1 = res_norm(h2, dn, vec(g_post_ffn, l), vec(g_pre_mix, l + 1))
        else:
            loss_sum, dh, dbr, dg_post_ffn = final_loss(h2, dn, vec(g_post_ffn, l), target)

    loss = lax.psum(0.5 * loss_sum[0, 0] / d, MESH_AXES)

    small = {n: [None] * depth for n in _SMALL}
    rs_state = {n: [None] * depth for n in _BIG}
    after = dbr
    for l in reversed(range(depth)):
        s = saved[l]
        wg = s["wg"]
        small["g_post_ffn"][l] = dg_post_ffn
        dgt, dup = ffn_bwd(dbr, wg["w_down"], s["f_gate"], s["f_up"])
        ffn_grads = (("w_down", "down", s["f"], dbr), ("w_gate", "gate", dgt, s["u2"]), ("w_up", "up", dup, s["u2"]))
        pairs = {}
        for n, tag, a, g in ffn_grads:
            theirs = mm_dw_half("mm_dw_" + tag, "rows", a, g, core_s, True)
            pairs[n] = pair_start(f"pair_start_{tag}_{l}", theirs, after)
            after = pairs[n][-1]
        for n, tag, a, g in ffn_grads:
            recv = pair_wait(f"pair_wait_{tag}_{l}", pairs[n], after)
            after = mm_dw_half("mm_dw_" + tag, "rows", a, g, core_s, False, add=recv)
            rs_state[n][l] = scatter_chips_start(f"rs_start_{tag}_{l}", after)
        du2 = mm_dx_ffn(dgt, wg["w_gate"], dup, wg["w_up"])
        started = sum(rs_state[n][l][-1][0:1, 0:1] for n in ("w_down", "w_gate", "w_up"))
        dh2, small["g_pre_ffn"][l], do, small["g_post_mix"][l] = norm_bwd(
            dh, du2, s["h2"], vec(g_pre_ffn, l) + started, (s["o"], vec(g_post_mix, l)))
        theirs = mm_dw_half("mm_dw_out", "take", s["mixed"], do, core_s, True)
        pairs["w_out"], do = pair_start(f"pair_start_out_{l}", theirs, rs_state["w_up"][l][-1], carry=do)
        dmixed = mm_plain("mm_dmixed", do, wg["w_out"], "nt", F32)
        recv = pair_wait(f"pair_wait_out_{l}", pairs["w_out"], dmixed)
        rs_state["w_out"][l] = scatter_chips_start(
            f"rs_start_out_{l}", mm_dw_half("mm_dw_out", "take", s["mixed"], do, core_s, False, add=recv))
        (dya, dc, dyl, small["g_attn_grp"][l], small["g_conv_grp"][l], small["g_lru_grp"][l],
         small["conv_ln_g"][l], small["conv_ln_b"][l]) = mix_bwd(
            dmixed, s["y_attn"], s["y_conv"], s["y_lru"], s["cpre"],
            vec(g_attn_grp, l) + rs_state["w_out"][l][-1][0:1, 0:1], vec(g_conv_grp, l),
            vec(g_lru_grp, l), vec(conv_ln_g, l), vec(conv_ln_b, l))
        dq, dk, dv = attn_bwd(s["proj"], dya, n_heads)
        dvg, small["dw_conv_w"][l], small["dw_conv_b"][l] = conv_bwd(s["proj"], conv_col, cc, dc, dw_full[l])
        (dxy, small["lru_conv_w"][l], small["lru_conv_b"][l], small["lru_w_a"][l], small["lru_b_a"][l],
         small["lru_w_i"][l], small["lru_b_i"][l], small["lru_lambda"][l]) = lru_bwd(
            s["proj"], lru_col, wl, s["hs"], dyl, lcw_full[l], vec(lru_conv_b, l), s["wa_b"], vec(lru_b_a, l),
            s["wi_b"], vec(lru_b_i, l), vec(lru_lambda, l))
        dproj = jnp.concatenate([dq, dk, dv, dvg, dxy], axis=1)
        theirs = mm_dw_half("mm_dw_in", "cols", s["u1"], dproj, core_s, True)
        pairs["w_in"], dproj = pair_start(f"pair_start_in_{l}", theirs, rs_state["w_out"][l][-1], carry=dproj)
        du1 = mm_dx_cols("mm_dx_in", dproj, wg["w_in"])
        recv = pair_wait(f"pair_wait_in_{l}", pairs["w_in"], du1)
        after = mm_dw_half("mm_dw_in", "cols", s["u1"], dproj, core_s, False, add=recv)
        rs_state["w_in"][l] = scatter_chips_start(f"rs_start_in_{l}", after)
        g_pre = vec(g_pre_mix, l) + rs_state["w_in"][l][-1][0:1, 0:1]
        if l > 0:
            p = saved[l - 1]
            dh, small["g_pre_mix"][l], dbr, dg_post_ffn = norm_bwd(
                dh2, du1, s["h"], g_pre, (p["dn"], vec(g_post_ffn, l - 1)))
        else:
            dh, small["g_pre_mix"][l] = norm_bwd(dh2, du1, s["h"], g_pre)
    grad_x = dh[None]

    small_shapes = [(depth,) + tuple(wts[n].shape[1:]) if n not in _COL_SHARDED_SMALL
                    else (depth, wts[n].shape[1], cc) for n in _SMALL]
    part = _pack([a for n in _SMALL for a in small[n]])
    small_state = gather_start("ag_small_start", [place_own(part, me_s, F32)], dh)

    grads, delta, new_m, new_v = {}, {}, {}, {}
    behind = small_state[-1]
    for n in reversed(_BIG):
        parts = [reduce_scatter_wait(f"{n[2:]}_{l}", rs_state[n][l], behind) for l in range(depth)]
        shape = wts[n].shape
        _, rows, cols = parts[0][0].shape
        view = (depth, rows, cols)
        g, dl, mn, vn = adam_big("adam_" + n, wts[n].reshape(view), mom[n].reshape(view), var[n].reshape(view),
                                 parts, chip_s)
        grads[n], delta[n], new_m[n], new_v[n] = (a.reshape(shape) for a in (g, dl, mn, vn))
        behind = g
    for group in (grads, delta, new_m, new_v):
        for n in _TRANSPOSED:
            group[n] = jnp.swapaxes(group[n], 1, 2)

    gathered, = gather_finish("ag_small_finish", gather_wait("ag_small_wait", small_state, 0, 1, behind))
    g_small = _unpack(sum_parts(gathered), small_shapes)
    for n, g in zip(_SMALL, g_small):
        if n in _COL_SHARDED_SMALL:
            g = lax.dynamic_slice_in_dim(g, me * (cc // N_DEV), cc // N_DEV, axis=2)
        grads[n] = g
    local_shapes = [tuple(wts[n].shape) for n in _SMALL]
    d_small, m_small, v_small = adam_small(
        _pack([wts[n] for n in _SMALL]), _pack([mom[n] for n in _SMALL]), _pack([var[n] for n in _SMALL]),
        _pack([grads[n] for n in _SMALL]))
    delta.update(zip(_SMALL, _unpack(d_small, local_shapes)))
    new_m.update(zip(_SMALL, _unpack(m_small, local_shapes)))
    new_v.update(zip(_SMALL, _unpack(v_small, local_shapes)))

    return (loss, grad_x, *[grads[n] for n in _ALL], *[delta[n] for n in _ALL],
            *[new_m[n] for n in _ALL], *[new_v[n] for n in _ALL])
```

```python
import functools
import math

import jax
import jax.numpy as jnp
from jax import lax
from jax.experimental import pallas as pl
from jax.experimental.pallas import tpu as pltpu

F32 = jnp.float32
BF16 = jnp.bfloat16

N_DEV = 8
EPS = 1e-6
HEAD_DIM = 128
DW_LEN = 31
LRU_LEN = 4
LRU_BLOCKS = 4
LRU_C = 8.0
ATT_TQ = 512
ATT_TK = 512
ATT_SUM = 256
ATT_HEADS = 2
ROW_T = 256
CONV_HALO = 32
LRU_HALO = 8
LANE = 128
SUBLANE = 8
WIDE_TILE = 2048
PACK_ROWS = 512
VMEM_LIMIT = 56 * 1024 * 1024

ADAM_LR = 0.001
ADAM_B1 = 0.9
ADAM_B2 = 0.999
ADAM_EPS = 1e-08
ADAM_WD = 0.01
ADAM_STEP = 10

MESH_AXES = ("x", "y", "c")
_DIMS = {
    "nn": (((1,), (0,)), ((), ())),
    "nt": (((1,), (1,)), ((), ())),
    "tn": (((0,), (0,)), ((), ())),
}


def _params(n_axes):
    return pltpu.CompilerParams(
        dimension_semantics=("arbitrary",) * n_axes, vmem_limit_bytes=VMEM_LIMIT)


def _dot(a, b, mode="nn"):
    return lax.dot_general(a, b, _DIMS[mode], preferred_element_type=F32)


def _sigmoid(x):
    return 1.0 / (1.0 + jnp.exp(-x))


def _softplus(x):
    return jnp.maximum(x, 0.0) + jnp.log(1.0 + jnp.exp(-jnp.abs(x)))


def _neg_expm1(x):
    series = x * (1.0 + x * (0.5 + x * (1.0 / 6 + x * (1.0 / 24 + x * (1.0 / 120 + x * (1.0 / 720))))))
    return jnp.where(x > -0.25, -series, 1.0 - jnp.exp(x))


_GELU_C = math.sqrt(2.0 / math.pi)


def _gelu_and_grad(x):
    inner = _GELU_C * (x + 0.044715 * x * x * x)
    t = jnp.tanh(inner)
    val = 0.5 * x * (1.0 + t)
    grad = 0.5 * (1.0 + t) + 0.5 * x * (1.0 - t * t) * _GELU_C * (1.0 + 3 * 0.044715 * x * x)
    return val, grad


def _rms_stats(x):
    r = lax.rsqrt(jnp.mean(x * x, axis=-1, keepdims=True) + EPS)
    return x * r, r


def _rms_bwd(dy, x, g):
    xn, r = _rms_stats(x)
    dxn = dy * g
    dx = r * (dxn - xn * jnp.mean(dxn * xn, axis=-1, keepdims=True))
    return dx, jnp.sum(dy * xn, axis=0, keepdims=True)


def _row_spec(tr, width, col=0):
    return pl.BlockSpec((tr, width), lambda i, col=col: (i, col))


def _vec_spec(width):
    return pl.BlockSpec((1, width), lambda i: (0, 0))


def _matmul(name, mode, operands, in_specs, out_shape, out_spec, grid):
    npairs = len(operands) // 2
    nk = grid[2]
    assert nk == 1 or out_shape.dtype == F32

    def body(*refs):
        o_ref = refs[2 * npairs]

        def partial():
            acc = None
            for p in range(npairs):
                d = _dot(refs[2 * p][...], refs[2 * p + 1][...], mode)
                acc = d if acc is None else acc + d
            return acc

        if nk == 1:
            o_ref[...] = partial().astype(o_ref.dtype)
        else:
            k = pl.program_id(2)

            @pl.when(k == 0)
            def _():
                o_ref[...] = partial()

            @pl.when(k > 0)
            def _():
                o_ref[...] += partial()

    return pl.pallas_call(
        body, name=name, grid=grid, in_specs=in_specs, out_specs=out_spec, out_shape=out_shape,
        compiler_params=_params(3),
    )(*operands)


def _tile(n, t):
    if n <= t:
        return n
    return max(k for k in range(SUBLANE, t + 1, SUBLANE) if n % k == 0)


def mm_proj(u, w):
    t, d = u.shape
    nblk, _, nb = w.shape
    tm = _tile(t, WIDE_TILE)
    return _matmul(
        "mm_proj", "nn", (u, w),
        [pl.BlockSpec((tm, d), lambda j, i, k: (i, 0)), pl.BlockSpec((None, d, nb), lambda j, i, k: (j, 0, 0))],
        jax.ShapeDtypeStruct((t, nblk * nb), F32), pl.BlockSpec((tm, nb), lambda j, i, k: (i, j)),
        (nblk, t // tm, 1))


def mm_plain(name, a, b, mode, out_dtype):
    if mode == "nn":
        (m, kk), n = a.shape, b.shape[1]
    elif mode == "nt":
        (m, kk), n = a.shape, b.shape[0]
    else:
        (kk, m), n = a.shape, b.shape[1]
    tm, tn = _tile(m, 1024), _tile(n, 1024)
    a_spec = (pl.BlockSpec((kk, tm), lambda i, j, k: (0, i)) if mode == "tn"
              else pl.BlockSpec((tm, kk), lambda i, j, k: (i, 0)))
    b_spec = (pl.BlockSpec((tn, kk), lambda i, j, k: (j, 0)) if mode == "nt"
              else pl.BlockSpec((kk, tn), lambda i, j, k: (0, j)))
    return _matmul(
        name, mode, (a, b), [a_spec, b_spec],
        jax.ShapeDtypeStruct((m, n), out_dtype), pl.BlockSpec((tm, tn), lambda i, j, k: (i, j)),
        (m // tm, n // tn, 1))


def mm_down(f, w):
    nblk, t, fb = f.shape
    d = w.shape[2]
    tm, tn = _tile(t, 1024), _tile(d, WIDE_TILE)
    return _matmul(
        "mm_down", "nn", (f, w),
        [pl.BlockSpec((None, tm, fb), lambda i, j, k: (k, i, 0)), pl.BlockSpec((None, fb, tn), lambda i, j, k: (k, 0, j))],
        jax.ShapeDtypeStruct((t, d), F32), pl.BlockSpec((tm, tn), lambda i, j, k: (i, j)),
        (t // tm, d // tn, nblk))


def mm_dw_half(name, kind, a, g, core, of_sibling, add=None):
    half = N_DEV // 2
    t = g.shape[0]

    def pick(k, core_ref):
        s = 1 - core_ref[0] if of_sibling else core_ref[0]
        return 2 * k + s

    if kind == "rows":
        rows, cols = a.shape[2], g.shape[1]
        tr, tc = rows, _tile(cols, WIDE_TILE)
        a_spec = pl.BlockSpec((None, t, rows), lambda k, i, n, core_ref: (pick(k, core_ref), 0, 0))
        g_spec = pl.BlockSpec((t, tc), lambda k, i, n, core_ref: (0, n))
    elif kind == "cols":
        rows, cols = a.shape[1], g.shape[1] // N_DEV
        tr, tc = _tile(rows, WIDE_TILE), cols
        a_spec = pl.BlockSpec((t, tr), lambda k, i, n, core_ref: (0, i))
        g_spec = pl.BlockSpec((t, cols), lambda k, i, n, core_ref: (0, pick(k, core_ref)))
    else:
        rows, cols = a.shape[1] // N_DEV, g.shape[1]
        tr, tc = rows, _tile(cols, WIDE_TILE)
        a_spec = pl.BlockSpec((t, rows), lambda k, i, n, core_ref: (0, pick(k, core_ref)))
        g_spec = pl.BlockSpec((t, tc), lambda k, i, n, core_ref: (0, n))
    o_spec = pl.BlockSpec((None, tr, tc), lambda k, i, n, core_ref: (k, i, n))

    def body(core_ref, a_ref, g_ref, *rest):
        acc = _dot(a_ref[...], g_ref[...], "tn")
        if add is not None:
            acc = acc + rest[0][...].astype(F32)
        rest[-1][...] = acc.astype(BF16)

    return pl.pallas_call(
        body, name=name,
        grid_spec=pltpu.PrefetchScalarGridSpec(
            num_scalar_prefetch=1, grid=(half, rows // tr, cols // tc),
            in_specs=[a_spec, g_spec] + ([o_spec] if add is not None else []), out_specs=o_spec),
        out_shape=jax.ShapeDtypeStruct((half, rows, cols), BF16), compiler_params=_params(3),
    )(core, a, g, *(() if add is None else (add,)))


def mm_dx_cols(name, g, w):
    t = g.shape[0]
    nblk, d, nb = w.shape
    tm, tn = _tile(t, 1024), _tile(d, WIDE_TILE)
    return _matmul(
        name, "nt", (g, w),
        [pl.BlockSpec((tm, nb), lambda i, j, k: (i, k)), pl.BlockSpec((None, tn, nb), lambda i, j, k: (k, j, 0))],
        jax.ShapeDtypeStruct((t, d), F32), pl.BlockSpec((tm, tn), lambda i, j, k: (i, j)),
        (t // tm, d // tn, nblk))


def mm_dx_ffn(dgt, wg, dup, wu):
    nblk, t, fb = dgt.shape
    d = wg.shape[2]
    tm, tn = _tile(t, 1024), _tile(d, WIDE_TILE)
    a_spec = pl.BlockSpec((None, tm, fb), lambda i, j, k: (k, i, 0))
    b_spec = pl.BlockSpec((None, fb, tn), lambda i, j, k: (k, 0, j))
    return _matmul(
        "mm_dx_ffn", "nn", (dgt, wg, dup, wu), [a_spec, b_spec, a_spec, b_spec],
        jax.ShapeDtypeStruct((t, d), F32), pl.BlockSpec((tm, tn), lambda i, j, k: (i, j)),
        (t // tm, d // tn, nblk))


def ffn_up(u, wg, wu):
    t, d = u.shape
    nblk, fb, _ = wg.shape
    tm = _tile(t, 512)

    def body(u_ref, wg_ref, wu_ref, dgt_ref, dup_ref, f_ref):
        for r0 in range(0, t, tm):
            rows = pl.ds(r0, tm)
            uu = u_ref[rows, :]
            gt = _dot(uu, wg_ref[...], "nt")
            up = _dot(uu, wu_ref[...], "nt")
            s = _sigmoid(gt)
            silu = gt * s
            dgt_ref[rows, :] = (up * s * (1.0 + gt * (1.0 - s))).astype(BF16)
            dup_ref[rows, :] = silu.astype(BF16)
            f_ref[rows, :] = (silu * up).astype(BF16)

    w_spec = pl.BlockSpec((None, fb, d), lambda j: (j, 0, 0))
    o_spec = pl.BlockSpec((None, t, fb), lambda j: (j, 0, 0))
    return pl.pallas_call(
        body, name="ffn_up", grid=(nblk,),
        in_specs=[pl.BlockSpec((t, d), lambda j: (0, 0)), w_spec, w_spec],
        out_specs=[o_spec, o_spec, o_spec],
        out_shape=[jax.ShapeDtypeStruct((nblk, t, fb), BF16)] * 3,
        compiler_params=_params(1),
    )(u, wg, wu)


def ffn_bwd(dd, wd, f_gate, f_up):
    t, d = dd.shape
    nblk, fb, _ = wd.shape
    tm = _tile(t, 512)

    def body(dd_ref, wd_ref, fg_ref, fu_ref, dgt_ref, dup_ref):
        for r0 in range(0, t, tm):
            rows = pl.ds(r0, tm)
            df = _dot(dd_ref[rows, :], wd_ref[...], "nt")
            dgt_ref[rows, :] = (df * fg_ref[rows, :].astype(F32)).astype(BF16)
            dup_ref[rows, :] = (df * fu_ref[rows, :].astype(F32)).astype(BF16)

    s_spec = pl.BlockSpec((None, t, fb), lambda j: (j, 0, 0))
    return pl.pallas_call(
        body, name="ffn_bwd", grid=(nblk,),
        in_specs=[pl.BlockSpec((t, d), lambda j: (0, 0)), pl.BlockSpec((None, fb, d), lambda j: (j, 0, 0)),
                  s_spec, s_spec],
        out_specs=[s_spec, s_spec],
        out_shape=[jax.ShapeDtypeStruct((nblk, t, fb), BF16)] * 2,
        compiler_params=_params(1),
    )(dd, wd, f_gate, f_up)


def rms_pre(h, g):
    t, d = h.shape
    tr = _tile(t, ROW_T)

    def body(h_ref, g_ref, o_ref):
        xn, _ = _rms_stats(h_ref[...])
        o_ref[...] = (xn * g_ref[...]).astype(BF16)

    return pl.pallas_call(
        body, name="rms_pre", grid=(t // tr,),
        in_specs=[_row_spec(tr, d), _vec_spec(d)], out_specs=_row_spec(tr, d),
        out_shape=jax.ShapeDtypeStruct((t, d), BF16), compiler_params=_params(1),
    )(h, g)


def res_norm(h, o, g_post, g_pre):
    t, d = h.shape
    tr = _tile(t, ROW_T)

    def body(h_ref, o_ref, gpo_ref, gpr_ref, h2_ref, u_ref):
        on, _ = _rms_stats(o_ref[...])
        h2 = h_ref[...] + on * gpo_ref[...]
        h2_ref[...] = h2
        hn, _ = _rms_stats(h2)
        u_ref[...] = (hn * gpr_ref[...]).astype(BF16)

    return pl.pallas_call(
        body, name="res_norm", grid=(t // tr,),
        in_specs=[_row_spec(tr, d), _row_spec(tr, d), _vec_spec(d), _vec_spec(d)],
        out_specs=[_row_spec(tr, d), _row_spec(tr, d)],
        out_shape=[jax.ShapeDtypeStruct((t, d), F32), jax.ShapeDtypeStruct((t, d), BF16)],
        compiler_params=_params(1),
    )(h, o, g_post, g_pre)


def final_loss(h2, dbr, g_post, target):
    t, d = h2.shape
    tr = _tile(t, ROW_T)

    def body(h2_ref, d_ref, g_ref, tg_ref, loss_ref, dy_ref, dd_ref, dg_ref):
        i = pl.program_id(0)

        @pl.when(i == 0)
        def _():
            loss_ref[...] = jnp.zeros_like(loss_ref)
            dg_ref[...] = jnp.zeros_like(dg_ref)

        x = d_ref[...]
        g = g_ref[...]
        xn, _ = _rms_stats(x)
        diff = h2_ref[...] + xn * g - tg_ref[...]
        loss_ref[...] += jnp.sum(jnp.sum(diff * diff, axis=1, keepdims=True), axis=0, keepdims=True)
        dy = diff * (1.0 / d)
        dy_ref[...] = dy
        dx, dg = _rms_bwd(dy, x, g)
        dd_ref[...] = dx.astype(BF16)
        dg_ref[...] += dg

    return pl.pallas_call(
        body, name="final_loss", grid=(t // tr,),
        in_specs=[_row_spec(tr, d), _row_spec(tr, d), _vec_spec(d), _row_spec(tr, d)],
        out_specs=[pl.BlockSpec((1, 1), lambda i: (0, 0)), _row_spec(tr, d), _row_spec(tr, d), _vec_spec(d)],
        out_shape=[jax.ShapeDtypeStruct((1, 1), F32), jax.ShapeDtypeStruct((t, d), F32),
                   jax.ShapeDtypeStruct((t, d), BF16), jax.ShapeDtypeStruct((1, d), F32)],
        compiler_params=_params(1),
    )(h2, dbr, g_post, target)


def norm_bwd(dh_out, du, h_in, g_pre, prev=None):
    t, d = h_in.shape
    tr = _tile(t, ROW_T)
    with_prev = prev is not None

    def body(*refs):
        if with_prev:
            dho_ref, du_ref, h_ref, gpr_ref, br_ref, gpo_ref, dh_ref, dgpr_ref, dbr_ref, dgpo_ref = refs
        else:
            dho_ref, du_ref, h_ref, gpr_ref, dh_ref, dgpr_ref = refs
        i = pl.program_id(0)

        @pl.when(i == 0)
        def _():
            dgpr_ref[...] = jnp.zeros_like(dgpr_ref)
            if with_prev:
                dgpo_ref[...] = jnp.zeros_like(dgpo_ref)

        dx, dg = _rms_bwd(du_ref[...], h_ref[...], gpr_ref[...])
        dh = dho_ref[...] + dx
        dh_ref[...] = dh
        dgpr_ref[...] += dg
        if with_prev:
            dbr, dg2 = _rms_bwd(dh, br_ref[...], gpo_ref[...])
            dbr_ref[...] = dbr.astype(BF16)
            dgpo_ref[...] += dg2

    row, vec = _row_spec(tr, d), _vec_spec(d)
    in_specs = [row, row, row, vec] + ([row, vec] if with_prev else [])
    out_specs = [row, vec] + ([row, vec] if with_prev else [])
    out_shape = [jax.ShapeDtypeStruct((t, d), F32), jax.ShapeDtypeStruct((1, d), F32)]
    if with_prev:
        out_shape += [jax.ShapeDtypeStruct((t, d), BF16), jax.ShapeDtypeStruct((1, d), F32)]
    args = (dh_out, du, h_in, g_pre) + (tuple(prev) if with_prev else ())
    return pl.pallas_call(
        body, name="norm_bwd_chain" if with_prev else "norm_bwd_first", grid=(t // tr,),
        in_specs=in_specs, out_specs=out_specs, out_shape=out_shape, compiler_params=_params(1),
    )(*args)


def mix_fwd(ya, yc, yl, ga, gc, gl):
    t, wa = ya.shape
    wc, wl = yc.shape[1], yl.shape[1]
    tr = _tile(t, ROW_T)

    def body(ya_ref, yc_ref, yl_ref, ga_ref, gc_ref, gl_ref, o_ref):
        o_ref[:, pl.ds(0, wa)] = (_rms_stats(ya_ref[...])[0] * ga_ref[...]).astype(BF16)
        o_ref[:, pl.ds(wa, wc)] = (_rms_stats(yc_ref[...])[0] * gc_ref[...]).astype(BF16)
        o_ref[:, pl.ds(wa + wc, wl)] = (_rms_stats(yl_ref[...])[0] * gl_ref[...]).astype(BF16)

    return pl.pallas_call(
        body, name="mix_fwd", grid=(t // tr,),
        in_specs=[_row_spec(tr, wa), _row_spec(tr, wc), _row_spec(tr, wl), _vec_spec(wa), _vec_spec(wc), _vec_spec(wl)],
        out_specs=_row_spec(tr, wa + wc + wl),
        out_shape=jax.ShapeDtypeStruct((t, wa + wc + wl), BF16), compiler_params=_params(1),
    )(ya, yc, yl, ga, gc, gl)


def mix_bwd(dmixed, ya, yc, yl, cpre, ga, gc, gl, lng, lnb):
    t, wa = ya.shape
    wc, wl = yc.shape[1], yl.shape[1]
    tr = _tile(t, ROW_T)

    def body(dm_ref, ya_ref, yc_ref, yl_ref, c_ref, ga_ref, gc_ref, gl_ref, lg_ref, lb_ref,
             dya_ref, dc_ref, dyl_ref, dga_ref, dgc_ref, dgl_ref, dlg_ref, dlb_ref):
        i = pl.program_id(0)

        @pl.when(i == 0)
        def _():
            for r in (dga_ref, dgc_ref, dgl_ref, dlg_ref, dlb_ref):
                r[...] = jnp.zeros_like(r)

        dya, dga = _rms_bwd(dm_ref[:, pl.ds(0, wa)], ya_ref[...], ga_ref[...])
        dya_ref[...] = dya
        dga_ref[...] += dga
        dyl, dgl = _rms_bwd(dm_ref[:, pl.ds(wa + wc, wl)], yl_ref[...], gl_ref[...])
        dyl_ref[...] = dyl
        dgl_ref[...] += dgl
        dyc, dgc = _rms_bwd(dm_ref[:, pl.ds(wa, wc)], yc_ref[...], gc_ref[...])
        dgc_ref[...] += dgc
        c = c_ref[...]
        xc = c - jnp.mean(c, axis=-1, keepdims=True)
        rstd = lax.rsqrt(jnp.mean(xc * xc, axis=-1, keepdims=True) + EPS)
        xhat = xc * rstd
        ln = xhat * lg_ref[...] + lb_ref[...]
        s = _sigmoid(ln)
        dln = dyc * s * (1.0 + ln * (1.0 - s))
        dlg_ref[...] += jnp.sum(dln * xhat, axis=0, keepdims=True)
        dlb_ref[...] += jnp.sum(dln, axis=0, keepdims=True)
        dxh = dln * lg_ref[...]
        dc_ref[...] = rstd * (dxh - jnp.mean(dxh, axis=-1, keepdims=True)
                              - xhat * jnp.mean(dxh * xhat, axis=-1, keepdims=True))

    return pl.pallas_call(
        body, name="mix_bwd", grid=(t // tr,),
        in_specs=[_row_spec(tr, wa + wc + wl), _row_spec(tr, wa), _row_spec(tr, wc), _row_spec(tr, wl), _row_spec(tr, wc),
                  _vec_spec(wa), _vec_spec(wc), _vec_spec(wl), _vec_spec(wc), _vec_spec(wc)],
        out_specs=[_row_spec(tr, wa), _row_spec(tr, wc), _row_spec(tr, wl),
                   _vec_spec(wa), _vec_spec(wc), _vec_spec(wl), _vec_spec(wc), _vec_spec(wc)],
        out_shape=[jax.ShapeDtypeStruct((t, wa), F32), jax.ShapeDtypeStruct((t, wc), F32), jax.ShapeDtypeStruct((t, wl), F32),
                   jax.ShapeDtypeStruct((1, wa), F32), jax.ShapeDtypeStruct((1, wc), F32), jax.ShapeDtypeStruct((1, wl), F32),
                   jax.ShapeDtypeStruct((1, wc), F32), jax.ShapeDtypeStruct((1, wc), F32)],
        compiler_params=_params(1),
    )(dmixed, ya, yc, yl, cpre, ga, gc, gl, lng, lnb)


def _hi_lo(x):
    hi = x.astype(BF16)
    return hi, (x - hi.astype(F32)).astype(BF16)


def _lane_sums(x, tri, reverse):
    nsub = x.shape[1] // tri.shape[0]
    order = range(nsub - 1, -1, -1) if reverse else range(nsub)
    parts, beyond, halves = {}, None, []
    for b in order:
        blk = x[:, b * tri.shape[0]:(b + 1) * tri.shape[0]]
        hi, lo = _hi_lo(blk)
        halves += [hi, lo]
        c = _dot(hi, tri) + _dot(lo, tri)
        parts[b] = c if beyond is None else c + beyond
        tot = jnp.sum(blk, axis=1, keepdims=True)
        beyond = tot if beyond is None else beyond + tot
    return jnp.concatenate([parts[b] for b in range(nsub)], axis=1), beyond, halves


def _att_strip(qb, kt, thresh, scale, diff, tri_gt):
    z = _dot(qb, kt, "nt") * scale
    sp = _softplus(z)
    mask = diff < thresh
    later, total, halves = _lane_sums(jnp.where(mask, -sp, 0.0), tri_gt, True)
    return z, sp, mask, total, (z - sp) + later, halves


def _att_consts(tq, tk):
    diff = lax.broadcasted_iota(jnp.int32, (tq, tk), 1) - lax.broadcasted_iota(jnp.int32, (tq, tk), 0)
    cb = min(tk, ATT_SUM)
    row = lax.broadcasted_iota(jnp.int32, (cb, cb), 0)
    col = lax.broadcasted_iota(jnp.int32, (cb, cb), 1)
    return diff, (row > col).astype(BF16), (row < col).astype(BF16)


def attn_fwd(proj, n_heads):
    t = proj.shape[0]
    tq, tk = _tile(t, ATT_TQ), _tile(t, ATT_TK)
    hp = ATT_HEADS
    wd = hp * HEAD_DIM
    scale = HEAD_DIM ** -0.5

    def body(q_ref, k_ref, v_ref, o_ref, r_ref, kb_ref, vb_ref, acc_ref):
        kb_ref[...] = k_ref[...].astype(BF16)
        vb_ref[...] = v_ref[...].astype(BF16)
        diff, tri_gt, _ = _att_consts(tq, tk)
        ones = jnp.ones((2 * SUBLANE, min(tk, ATT_SUM)), BF16)

        def qblock(i, _):
            q0 = pl.multiple_of(i * tq, tq)
            heads = [pl.ds(h * HEAD_DIM, HEAD_DIM) for h in range(hp)]
            qbs = [q_ref[pl.ds(q0, tq), hs].astype(BF16) for hs in heads]
            acc_ref[...] = jnp.zeros_like(acc_ref)
            n_strips = (q0 + tq + tk - 1) // tk

            def strip(jj, carry):
                si = n_strips - 1 - jj
                k0 = pl.multiple_of(si * tk, tk)
                out = []
                for h, hs in enumerate(heads):
                    run, run_row = carry[2 * h], carry[2 * h + 1]
                    _, _, mask, total, logw, halves = _att_strip(
                        qbs[h], kb_ref[pl.ds(k0, tk), hs], q0 - k0, scale, diff, tri_gt)
                    w = jnp.where(mask, jnp.exp(logw + run), 0.0)
                    acc_ref[:, hs] += _dot(w.astype(BF16), vb_ref[pl.ds(k0, tk), hs])
                    r_ref[h, i, si] = run_row
                    total_row = sum(_dot(ones, half, "nt") for half in halves)[:SUBLANE]
                    out += [run + total, run_row + total_row]
                return tuple(out)

            init = (jnp.zeros((tq, 1), F32), jnp.zeros((SUBLANE, tq), F32)) * hp
            lax.fori_loop(0, n_strips, strip, init)
            o_ref[pl.ds(q0, tq), :] = acc_ref[...]
            return 0

        lax.fori_loop(0, t // tq, qblock, 0)

    def col_spec(base):
        return pl.BlockSpec((t, wd), lambda h, base=base: (0, base + h))

    ng = n_heads // hp
    runs_shape = (n_heads, t // tq, t // tk, SUBLANE, tq)
    return pl.pallas_call(
        body, name="attn_fwd", grid=(ng,),
        in_specs=[col_spec(0), col_spec(ng), col_spec(2 * ng)],
        out_specs=[col_spec(0), pl.BlockSpec((hp,) + runs_shape[1:], lambda h: (h, 0, 0, 0, 0))],
        out_shape=[jax.ShapeDtypeStruct((t, n_heads * HEAD_DIM), F32), jax.ShapeDtypeStruct(runs_shape, F32)],
        scratch_shapes=[pltpu.VMEM((t, wd), BF16), pltpu.VMEM((t, wd), BF16), pltpu.VMEM((tq, wd), F32)],
        compiler_params=_params(1),
    )(proj, proj, proj)


def attn_bwd(proj, dy, runs, n_heads):
    t = proj.shape[0]
    tq, tk = _tile(t, ATT_TQ), _tile(t, ATT_TK)
    hp = ATT_HEADS
    wd = hp * HEAD_DIM
    scale = HEAD_DIM ** -0.5

    def body(q_ref, k_ref, v_ref, dy_ref, r_ref, dq_ref, dk_ref, dv_ref, qb_ref, kb_ref, vb_ref, dob_ref, dk_acc,
             dv_acc, dq_acc):
        qb_ref[...] = q_ref[...].astype(BF16)
        kb_ref[...] = k_ref[...].astype(BF16)
        vb_ref[...] = v_ref[...].astype(BF16)
        dob_ref[...] = dy_ref[...].astype(BF16)
        dk_acc[...] = jnp.zeros_like(dk_acc)
        dv_acc[...] = jnp.zeros_like(dv_acc)
        diff, tri_gt, tri_lt = _att_consts(tq, tk)

        def qblock(i, _):
            q0 = pl.multiple_of(i * tq, tq)
            heads = [pl.ds(h * HEAD_DIM, HEAD_DIM) for h in range(hp)]
            qbs = [qb_ref[pl.ds(q0, tq), hs] for hs in heads]
            dobs = [dob_ref[pl.ds(q0, tq), hs] for hs in heads]
            dq_acc[...] = jnp.zeros_like(dq_acc)
            n_strips = (q0 + tq + tk - 1) // tk

            zero = tuple(jnp.zeros((tq, 1), F32) for _ in range(hp))

            def strip(si, gsums):
                k0 = pl.multiple_of(si * tk, tk)
                out = []
                for h, hs in enumerate(heads):
                    kt = kb_ref[pl.ds(k0, tk), hs]
                    vt = vb_ref[pl.ds(k0, tk), hs]
                    z, sp, mask, _, logw, _ = _att_strip(qbs[h], kt, q0 - k0, scale, diff, tri_gt)
                    run = jnp.concatenate([r_ref[h, i, si]] * (LANE // SUBLANE), axis=0).T
                    w = jnp.where(mask, jnp.exp(logw + jnp.concatenate([run] * (tk // LANE), axis=1)), 0.0)
                    g = w * _dot(dobs[h], vt, "nt")
                    before, gtot, _ = _lane_sums(g, tri_lt, False)
                    sig = jnp.exp(z - sp)
                    dz = jnp.where(mask, g * (1.0 - sig) - (before + gsums[h]) * sig, 0.0) * scale
                    dzb = dz.astype(BF16)
                    dk_acc[pl.ds(k0, tk), hs] += _dot(dzb, qbs[h], "tn")
                    dv_acc[pl.ds(k0, tk), hs] += _dot(w.astype(BF16), dobs[h], "tn")
                    dq_acc[:, hs] += _dot(dzb, kt)
                    out.append(gsums[h] + gtot)
                return tuple(out)

            lax.fori_loop(0, n_strips, strip, zero)
            dq_ref[pl.ds(q0, tq), :] = dq_acc[...].astype(BF16)
            return 0

        lax.fori_loop(0, t // tq, qblock, 0)
        dk_ref[...] = dk_acc[...].astype(BF16)
        dv_ref[...] = dv_acc[...].astype(BF16)

    def col_spec(base):
        return pl.BlockSpec((t, wd), lambda h, base=base: (0, base + h))

    ng = n_heads // hp
    return pl.pallas_call(
        body, name="attn_bwd", grid=(ng,),
        in_specs=[col_spec(0), col_spec(ng), col_spec(2 * ng), col_spec(0),
                  pl.BlockSpec((hp,) + runs.shape[1:], lambda h: (h, 0, 0, 0, 0))],
        out_specs=[col_spec(0), col_spec(0), col_spec(0)],
        out_shape=[jax.ShapeDtypeStruct((t, n_heads * HEAD_DIM), BF16)] * 3,
        scratch_shapes=[pltpu.VMEM((t, wd), BF16)] * 4 + [pltpu.VMEM((t, wd), F32)] * 2 + [pltpu.VMEM((tq, wd), F32)],
        compiler_params=_params(1),
    )(proj, proj, proj, dy, runs)


def _glu_halo(vc, gc, vp, gp, ubuf, i, tt, halo):
    uprev = vp[pl.ds(tt - halo, halo), :] * _sigmoid(gp[pl.ds(tt - halo, halo), :])
    ubuf[pl.ds(0, halo), :] = jnp.where(i > 0, uprev, 0.0)
    ubuf[pl.ds(halo, tt), :] = vc[...] * _sigmoid(gc[...])


def conv_fwd(proj, col0, cc, w, b, lng, lnb):
    t = proj.shape[0]
    tt = _tile(t, ROW_T)
    vi, gi = col0 // cc, col0 // cc + 1
    off = CONV_HALO - (DW_LEN - 1)

    def body(vc, gc, vp, gp, w_ref, b_ref, lg_ref, lb_ref, c_ref, y_ref, ubuf):
        i = pl.program_id(0)
        _glu_halo(vc, gc, vp, gp, ubuf, i, tt, CONV_HALO)
        for ch in range(cc // LANE):
            sl = pl.ds(ch * LANE, LANE)
            acc = jnp.zeros((tt, LANE), F32) + b_ref[:, sl]
            for tap in range(DW_LEN):
                acc = acc + w_ref[pl.ds(tap, 1), sl] * ubuf[pl.ds(off + tap, tt), sl]
            c_ref[:, sl] = acc
        c = c_ref[...]
        xc = c - jnp.mean(c, axis=-1, keepdims=True)
        ln = xc * lax.rsqrt(jnp.mean(xc * xc, axis=-1, keepdims=True) + EPS) * lg_ref[...] + lb_ref[...]
        y_ref[...] = ln * _sigmoid(ln)

    cur = lambda c: pl.BlockSpec((tt, cc), lambda i, c=c: (i, c))
    prev = lambda c: pl.BlockSpec((tt, cc), lambda i, c=c: (jnp.maximum(i - 1, 0), c))
    return pl.pallas_call(
        body, name="conv_fwd", grid=(t // tt,),
        in_specs=[cur(vi), cur(gi), prev(vi), prev(gi), pl.BlockSpec((DW_LEN, cc), lambda i: (0, 0)),
                  _vec_spec(cc), _vec_spec(cc), _vec_spec(cc)],
        out_specs=[_row_spec(tt, cc), _row_spec(tt, cc)],
        out_shape=[jax.ShapeDtypeStruct((t, cc), F32)] * 2,
        scratch_shapes=[pltpu.VMEM((CONV_HALO + tt, cc), F32)],
        compiler_params=_params(1),
    )(proj, proj, proj, proj, w, b, lng, lnb)


def conv_bwd(proj, col0, cc, dc, w):
    t = proj.shape[0]
    tt = _tile(t, ROW_T)
    nt = t // tt
    vi, gi = col0 // cc, col0 // cc + 1
    off = CONV_HALO - (DW_LEN - 1)

    def body(vc, gc, vp, gp, dcc, dcn, w_ref, dvg_ref, dw_ref, db_ref, ubuf, dbuf):
        i = pl.program_id(0)

        @pl.when(i == 0)
        def _():
            dw_ref[...] = jnp.zeros_like(dw_ref)
            db_ref[...] = jnp.zeros_like(db_ref)

        _glu_halo(vc, gc, vp, gp, ubuf, i, tt, CONV_HALO)
        dbuf[pl.ds(0, tt), :] = dcc[...]
        dbuf[pl.ds(tt, CONV_HALO), :] = jnp.where(i < nt - 1, dcn[pl.ds(0, CONV_HALO), :], 0.0)
        db_ref[...] += jnp.sum(dcc[...], axis=0, keepdims=True)
        for ch in range(cc // LANE):
            sl = pl.ds(ch * LANE, LANE)
            dcv = dbuf[pl.ds(0, tt), sl]
            du = jnp.zeros((tt, LANE), F32)
            for tap in range(DW_LEN):
                du = du + w_ref[pl.ds(tap, 1), sl] * dbuf[pl.ds(DW_LEN - 1 - tap, tt), sl]
                dw_ref[pl.ds(tap, 1), sl] += jnp.sum(dcv * ubuf[pl.ds(off + tap, tt), sl], axis=0, keepdims=True)
            s = _sigmoid(gc[:, sl])
            val = vc[:, sl]
            dvg_ref[:, sl] = (du * s).astype(BF16)
            dvg_ref[:, pl.ds(cc + ch * LANE, LANE)] = (du * val * s * (1.0 - s)).astype(BF16)

    cur = lambda c: pl.BlockSpec((tt, cc), lambda i, c=c: (i, c))
    prev = lambda c: pl.BlockSpec((tt, cc), lambda i, c=c: (jnp.maximum(i - 1, 0), c))
    return pl.pallas_call(
        body, name="conv_bwd", grid=(nt,),
        in_specs=[cur(vi), cur(gi), prev(vi), prev(gi), _row_spec(tt, cc),
                  pl.BlockSpec((tt, cc), lambda i: (jnp.minimum(i + 1, nt - 1), 0)),
                  pl.BlockSpec((DW_LEN, cc), lambda i: (0, 0))],
        out_specs=[_row_spec(tt, 2 * cc), pl.BlockSpec((DW_LEN, cc), lambda i: (0, 0)), _vec_spec(cc)],
        out_shape=[jax.ShapeDtypeStruct((t, 2 * cc), BF16), jax.ShapeDtypeStruct((DW_LEN, cc), F32),
                   jax.ShapeDtypeStruct((1, cc), F32)],
        scratch_shapes=[pltpu.VMEM((CONV_HALO + tt, cc), F32), pltpu.VMEM((tt + CONV_HALO, cc), F32)],
        compiler_params=_params(1),
    )(proj, proj, proj, proj, dc, dc, w)


def _lru_gates(xbuf, cw_ref, cb_ref, wa_ref, ba_ref, wi_ref, bi_ref, lam_ref, tt, wl):
    bd = wl // LRU_BLOCKS
    xr = jnp.zeros((tt, wl), F32) + cb_ref[...]
    for tap in range(LRU_LEN):
        xr = xr + cw_ref[pl.ds(tap, 1), :] * xbuf[pl.ds(LRU_HALO - (LRU_LEN - 1) + tap, tt), :]
    xb = xr.astype(BF16)
    ga = jnp.concatenate([_dot(xb[:, n * bd:(n + 1) * bd], wa_ref[n]) for n in range(LRU_BLOCKS)], axis=1) + ba_ref[...]
    gi = jnp.concatenate([_dot(xb[:, n * bd:(n + 1) * bd], wi_ref[n]) for n in range(LRU_BLOCKS)], axis=1) + bi_ref[...]
    r = _sigmoid(ga)
    ig = _sigmoid(gi)
    spl = _softplus(-lam_ref[...])
    log_a = -LRU_C * r * spl
    a = jnp.exp(log_a)
    m = jnp.sqrt(_neg_expm1(2.0 * log_a))
    return xr, xb, r, ig, spl, a, m


def _group_scan(a8, b8, reverse):
    rid = lax.broadcasted_iota(jnp.int32, a8.shape, 0)
    aa, bb = a8, b8
    for dist in (1, 2, 4):
        shift = SUBLANE - dist if reverse else dist
        a_sh = pltpu.roll(aa, shift, 0)
        b_sh = pltpu.roll(bb, shift, 0)
        valid = (rid < SUBLANE - dist) if reverse else (rid >= dist)
        bb = jnp.where(valid, aa * b_sh + bb, bb)
        aa = jnp.where(valid, aa * a_sh, aa)
    return aa, bb


def _pick_row(x8, r):
    rid = lax.broadcasted_iota(jnp.int32, x8.shape, 0)
    return jnp.sum(jnp.where(rid == r, x8, 0.0), axis=0, keepdims=True)


def lru_fwd(proj, col0, wl, cw, cb, wa, ba, wi, bi, lam):
    t = proj.shape[0]
    tt = _tile(t, ROW_T)
    xi, yi = col0 // wl, col0 // wl + 1

    def body(xc, xp, ry, cw_ref, cb_ref, wa_ref, ba_ref, wi_ref, bi_ref, lam_ref, hs_ref, y_ref,
             xbuf, a_s, b_s, hcar):
        i = pl.program_id(0)

        @pl.when(i == 0)
        def _():
            hcar[...] = jnp.zeros_like(hcar)

        xbuf[pl.ds(0, LRU_HALO), :] = jnp.where(i > 0, xp[pl.ds(tt - LRU_HALO, LRU_HALO), :], 0.0)
        xbuf[pl.ds(LRU_HALO, tt), :] = xc[...]
        xr, _, _, ig, _, a, m = _lru_gates(xbuf, cw_ref, cb_ref, wa_ref, ba_ref, wi_ref, bi_ref, lam_ref, tt, wl)
        a_s[...] = a
        b_s[...] = m * ig * xr

        def group(gidx, h):
            r0 = pl.multiple_of(gidx * SUBLANE, SUBLANE)
            aa, bb = _group_scan(a_s[pl.ds(r0, SUBLANE), :], b_s[pl.ds(r0, SUBLANE), :], False)
            h8 = aa * h + bb
            hs_ref[pl.ds(r0, SUBLANE), :] = h8
            return _pick_row(h8, SUBLANE - 1)

        hcar[...] = lax.fori_loop(0, tt // SUBLANE, group, hcar[...])
        gel, _ = _gelu_and_grad(ry[...])
        y_ref[...] = hs_ref[...] * gel

    cur = lambda c: pl.BlockSpec((tt, wl), lambda i, c=c: (i, c))
    full = lambda shape: pl.BlockSpec(shape, lambda i: (0,) * len(shape))
    return pl.pallas_call(
        body, name="lru_fwd", grid=(t // tt,),
        in_specs=[cur(xi), pl.BlockSpec((tt, wl), lambda i: (jnp.maximum(i - 1, 0), xi)), cur(yi),
                  full((LRU_LEN, wl)), _vec_spec(wl), full(wa.shape), _vec_spec(wl), full(wi.shape), _vec_spec(wl),
                  _vec_spec(wl)],
        out_specs=[_row_spec(tt, wl), _row_spec(tt, wl)],
        out_shape=[jax.ShapeDtypeStruct((t, wl), F32)] * 2,
        scratch_shapes=[pltpu.VMEM((LRU_HALO + tt, wl), F32), pltpu.VMEM((tt, wl), F32), pltpu.VMEM((tt, wl), F32),
                        pltpu.VMEM((1, wl), F32)],
        compiler_params=_params(1),
    )(proj, proj, proj, cw, cb, wa, ba, wi, bi, lam)


def lru_bwd(proj, col0, wl, hs, dy, cw, cb, wa, ba, wi, bi, lam):
    t = proj.shape[0]
    tt = _tile(t, ROW_T)
    nt = t // tt
    xi, yi = col0 // wl, col0 // wl + 1
    bd = wl // LRU_BLOCKS

    def body(xc, xp, ry, hc, hp, dy_ref, cw_ref, cb_ref, wa_ref, ba_ref, wi_ref, bi_ref, lam_ref,
             dxy_ref, dcw_ref, dcb_ref, dwa_ref, dba_ref, dwi_ref, dbi_ref, dlam_ref,
             xbuf, hbuf, abuf, e_s, dh_s, dxbuf, dhcar):
        i = pl.program_id(0)
        first = i == 0

        @pl.when(first)
        def _():
            for r in (dcw_ref, dcb_ref, dwa_ref, dba_ref, dwi_ref, dbi_ref, dlam_ref, dhcar):
                r[...] = jnp.zeros_like(r)
            abuf[pl.ds(tt, LRU_HALO), :] = jnp.zeros((LRU_HALO, wl), F32)
            dxbuf[pl.ds(tt, LRU_HALO), :] = jnp.zeros((LRU_HALO, wl), F32)

        has_prev = i < nt - 1
        xbuf[pl.ds(0, LRU_HALO), :] = jnp.where(has_prev, xp[pl.ds(tt - LRU_HALO, LRU_HALO), :], 0.0)
        xbuf[pl.ds(LRU_HALO, tt), :] = xc[...]
        hbuf[pl.ds(0, LRU_HALO), :] = jnp.where(has_prev, hp[pl.ds(tt - LRU_HALO, LRU_HALO), :], 0.0)
        hbuf[pl.ds(LRU_HALO, tt), :] = hc[...]
        xr, xb, r, ig, spl, a, m = _lru_gates(xbuf, cw_ref, cb_ref, wa_ref, ba_ref, wi_ref, bi_ref, lam_ref, tt, wl)
        gel, dgel = _gelu_and_grad(ry[...])
        dyv = dy_ref[...]
        e_s[...] = dyv * gel
        dxy_ref[:, pl.ds(wl, wl)] = (dyv * hc[...] * dgel).astype(BF16)
        abuf[pl.ds(0, tt), :] = a
        a_next = abuf[pl.ds(1, tt), :]
        dh_s[...] = a_next

        def group(it, dh_in):
            r0 = pl.multiple_of((tt // SUBLANE - 1 - it) * SUBLANE, SUBLANE)
            aa, bb = _group_scan(dh_s[pl.ds(r0, SUBLANE), :], e_s[pl.ds(r0, SUBLANE), :], True)
            dh8 = aa * dh_in + bb
            dh_s[pl.ds(r0, SUBLANE), :] = dh8
            return _pick_row(dh8, 0)

        dhcar[...] = lax.fori_loop(0, tt // SUBLANE, group, dhcar[...])
        abuf[pl.ds(tt, LRU_HALO), :] = a[0:LRU_HALO, :]
        dh = dh_s[...]
        h_m1 = hbuf[pl.ds(LRU_HALO - 1, tt), :]
        dlog_a = dh * h_m1 * a - dh * ig * xr * (a * a / m)
        dig = dh * m * xr
        dxr = dh * m * ig
        dga = dlog_a * (-LRU_C) * spl * r * (1.0 - r)
        dgi = dig * ig * (1.0 - ig)
        dlam_ref[...] += jnp.sum(dlog_a * r, axis=0, keepdims=True) * (LRU_C * _sigmoid(-lam_ref[...]))
        dba_ref[...] += jnp.sum(dga, axis=0, keepdims=True)
        dbi_ref[...] += jnp.sum(dgi, axis=0, keepdims=True)
        dgab = dga.astype(BF16)
        dgib = dgi.astype(BF16)
        back = []
        for n in range(LRU_BLOCKS):
            sl = slice(n * bd, (n + 1) * bd)
            dwa_ref[n] += _dot(xb[:, sl], dgab[:, sl], "tn")
            dwi_ref[n] += _dot(xb[:, sl], dgib[:, sl], "tn")
            back.append(_dot(dgab[:, sl], wa_ref[n], "nt") + _dot(dgib[:, sl], wi_ref[n], "nt"))
        dxr = dxr + jnp.concatenate(back, axis=1)
        dcb_ref[...] += jnp.sum(dxr, axis=0, keepdims=True)
        dxbuf[pl.ds(0, tt), :] = dxr
        drx = jnp.zeros((tt, wl), F32)
        for tap in range(LRU_LEN):
            drx = drx + cw_ref[pl.ds(tap, 1), :] * dxbuf[pl.ds(LRU_LEN - 1 - tap, tt), :]
            dcw_ref[pl.ds(tap, 1), :] += jnp.sum(
                dxr * xbuf[pl.ds(LRU_HALO - (LRU_LEN - 1) + tap, tt), :], axis=0, keepdims=True)
        dxbuf[pl.ds(tt, LRU_HALO), :] = dxr[0:LRU_HALO, :]
        dxy_ref[:, pl.ds(0, wl)] = drx.astype(BF16)

    rev = lambda c: pl.BlockSpec((tt, wl), lambda i, c=c: (nt - 1 - i, c))
    rev_prev = lambda c: pl.BlockSpec((tt, wl), lambda i, c=c: (jnp.maximum(nt - 2 - i, 0), c))
    full = lambda shape: pl.BlockSpec(shape, lambda i: (0,) * len(shape))
    vec = _vec_spec(wl)
    return pl.pallas_call(
        body, name="lru_bwd", grid=(nt,),
        in_specs=[rev(xi), rev_prev(xi), rev(yi), rev(0), rev_prev(0), rev(0),
                  full((LRU_LEN, wl)), vec, full(wa.shape), vec, full(wi.shape), vec, vec],
        out_specs=[pl.BlockSpec((tt, 2 * wl), lambda i: (nt - 1 - i, 0)), full((LRU_LEN, wl)), vec,
                   full(wa.shape), vec, full(wi.shape), vec, vec],
        out_shape=[jax.ShapeDtypeStruct((t, 2 * wl), BF16), jax.ShapeDtypeStruct((LRU_LEN, wl), F32),
                   jax.ShapeDtypeStruct((1, wl), F32), jax.ShapeDtypeStruct(wa.shape, F32),
                   jax.ShapeDtypeStruct((1, wl), F32), jax.ShapeDtypeStruct(wi.shape, F32),
                   jax.ShapeDtypeStruct((1, wl), F32), jax.ShapeDtypeStruct((1, wl), F32)],
        scratch_shapes=[pltpu.VMEM((LRU_HALO + tt, wl), F32), pltpu.VMEM((LRU_HALO + tt, wl), F32),
                        pltpu.VMEM((tt + LRU_HALO, wl), F32), pltpu.VMEM((tt, wl), F32), pltpu.VMEM((tt, wl), F32),
                        pltpu.VMEM((tt + LRU_HALO, wl), F32), pltpu.VMEM((1, wl), F32)],
        compiler_params=_params(1),
    )(proj, proj, proj, hs, hs, dy, cw, cb, wa, ba, wi, bi, lam)


def _adamw(w, g, m, v):
    m = ADAM_B1 * m + (1.0 - ADAM_B1) * g
    v = ADAM_B2 * v + (1.0 - ADAM_B2) * (g * g)
    m_hat = m / (1.0 - ADAM_B1 ** ADAM_STEP)
    v_hat = v / (1.0 - ADAM_B2 ** ADAM_STEP)
    delta = -ADAM_LR * (m_hat / (jnp.sqrt(v_hat) + ADAM_EPS) + ADAM_WD * w)
    return delta, m, v


def adam_big(name, w, m, v, parts, chip):
    n_layers, rows, cols = w.shape
    tr = _tile(rows, 128 if cols > 1024 else 256)
    nrt = rows // tr

    def body(chip_ref, *refs):
        w_ref, m_ref, v_ref = refs[:3]
        part_refs = refs[3:3 + 4 * n_layers]
        g_ref, d_ref, mo_ref, vo_ref = refs[3 + 4 * n_layers:]
        layer = pl.program_id(0)
        for l in range(n_layers):
            @pl.when(layer == l)
            def _(l=l):
                g = part_refs[4 * l][...].astype(F32)
                for p in range(1, 4):
                    g = g + part_refs[4 * l + p][...].astype(F32)
                delta, mn, vn = _adamw(w_ref[...], g, m_ref[...], v_ref[...])
                g_ref[...] = g
                d_ref[...] = delta
                mo_ref[...] = mn
                vo_ref[...] = vn

    wspec = pl.BlockSpec((None, tr, cols), lambda l, i, chip_ref: (l, i, 0))
    operands, in_specs = [w, m, v], [wspec, wspec, wspec]
    for l in range(n_layers):
        mine, recv = parts[l]
        operands.append(mine)
        in_specs.append(pl.BlockSpec(
            (None, tr, cols), lambda ll, i, chip_ref, l=l: (chip_ref[0], jnp.where(ll == l, i, 0), 0)))
        for p in range(3):
            operands.append(recv)
            in_specs.append(pl.BlockSpec(
                (None, tr, cols), lambda ll, i, chip_ref, l=l, p=p: (p, jnp.where(ll == l, i, 0), 0)))
    return pl.pallas_call(
        body, name=name,
        grid_spec=pltpu.PrefetchScalarGridSpec(
            num_scalar_prefetch=1, grid=(n_layers, nrt), in_specs=in_specs, out_specs=[wspec] * 4),
        out_shape=[jax.ShapeDtypeStruct(w.shape, F32)] * 4, compiler_params=_params(2),
    )(chip, *operands)


def adam_small(w, m, v, g):
    rows = w.shape[0]
    tr = _tile(rows, PACK_ROWS)

    def body(w_ref, m_ref, v_ref, g_ref, d_ref, mo_ref, vo_ref):
        delta, mn, vn = _adamw(w_ref[...], g_ref[...], m_ref[...], v_ref[...])
        d_ref[...] = delta
        mo_ref[...] = mn
        vo_ref[...] = vn

    spec = _row_spec(tr, LANE)
    return pl.pallas_call(
        body, name="adam_small", grid=(rows // tr,), in_specs=[spec] * 4, out_specs=[spec] * 3,
        out_shape=[jax.ShapeDtypeStruct(w.shape, F32)] * 3, compiler_params=_params(1),
    )(w, m, v, g)


def sum_parts(parts):
    _, rows, _ = parts.shape
    tr = _tile(rows, PACK_ROWS)

    def body(p_ref, o_ref):
        acc = p_ref[0]
        for k in range(1, N_DEV):
            acc = acc + p_ref[k]
        o_ref[...] = acc

    return pl.pallas_call(
        body, name="sum_parts", grid=(rows // tr,),
        in_specs=[pl.BlockSpec((N_DEV, tr, LANE), lambda i: (0, i, 0))], out_specs=_row_spec(tr, LANE),
        out_shape=jax.ShapeDtypeStruct((rows, LANE), F32), compiler_params=_params(1),
    )(parts)


def place_own(x, me, dtype):
    rows, cols = x.shape
    tr = _tile(rows, 256)

    def body(me_ref, x_ref, o_ref):
        o_ref[...] = x_ref[...].astype(dtype)

    return pl.pallas_call(
        body, name="place_own",
        grid_spec=pltpu.PrefetchScalarGridSpec(
            num_scalar_prefetch=1, grid=(rows // tr,),
            in_specs=[pl.BlockSpec((tr, cols), lambda i, me_ref: (i, 0))],
            out_specs=pl.BlockSpec((None, tr, cols), lambda i, me_ref: (me_ref[0], i, 0))),
        out_shape=jax.ShapeDtypeStruct((N_DEV, rows, cols), dtype), compiler_params=_params(1),
    )(me, x)


_HBM = pl.BlockSpec(memory_space=pltpu.HBM)


def _place():
    return lax.axis_index("x"), lax.axis_index("y"), lax.axis_index("c")


def _other_chips(x, y):
    return [(1 - x, y), (x, 1 - y), (1 - x, 1 - y)]


def all_gather(name, shard, me, dtype=None):
    def body(buf_ref, out_ref, send_sems, recv_sems):
        del buf_ref
        x, y, c = _place()
        mine, sibling = (x, y, c), (x, y, 1 - c)
        chips = _other_chips(x, y)

        def copy(k, block, to):
            slot = out_ref.at[4 * block[0] + 2 * block[1] + block[2]]
            return pltpu.make_async_remote_copy(
                src_ref=slot, dst_ref=slot, send_sem=send_sems.at[k], recv_sem=recv_sems.at[k],
                device_id=to, device_id_type=pl.DeviceIdType.MESH)

        first = [copy(0, mine, sibling)] + [copy(1 + j, mine, (*chip, c)) for j, chip in enumerate(chips)]
        for cp in first:
            cp.start()
        passed = [copy(4 + j, (*chip, c), sibling) for j, chip in enumerate(chips)]
        for j, chip in enumerate(chips):
            copy(1 + j, (*chip, c), mine).wait_recv()
            passed[j].start()
        copy(0, sibling, mine).wait_recv()
        for j, chip in enumerate(chips):
            copy(4 + j, (*chip, 1 - c), mine).wait_recv()
        for cp in first + passed:
            cp.wait_send()

    buf = place_own(shard, me, dtype or shard.dtype)
    return pl.pallas_call(
        body, name=name, out_shape=jax.ShapeDtypeStruct(buf.shape, buf.dtype),
        in_specs=[_HBM], out_specs=_HBM, input_output_aliases={0: 0},
        scratch_shapes=[pltpu.SemaphoreType.DMA((7,)), pltpu.SemaphoreType.DMA((7,))],
    )(buf)


def _own_block_copies(src_refs, dst_refs, send_sems, recv_sems, arrivals):
    x, y, c = _place()
    peers = [(x, y, 1 - c)] + [(*chip, c) for chip in _other_chips(x, y)]
    copies = []
    for b, (src, dst) in enumerate(zip(src_refs, dst_refs)):
        for k, peer in enumerate(peers):
            def copy(landing, b=b, k=k, peer=peer, src=src, dst=dst):
                return pltpu.make_async_remote_copy(
                    src_ref=src.at[4 * x + 2 * y + c], dst_ref=dst.at[landing],
                    send_sem=send_sems.at[4 * b + k], recv_sem=recv_sems.at[4 * b + k],
                    device_id=peer, device_id_type=pl.DeviceIdType.MESH)
            copies.append((copy(4 * x + 2 * y + c), copy(4 * peer[0] + 2 * peer[1] + peer[2]) if arrivals else None))
    return copies


def gather_start(name, bufs, after):
    n = len(bufs)

    def body(*refs):
        send_sems, recv_sems = refs[n + 1], refs[n + 2]
        thru = refs[n + 3:2 * n + 3]
        for send, _ in _own_block_copies(thru, thru, send_sems, recv_sems, False):
            send.start()
        refs[2 * n + 3][...] = jnp.zeros((SUBLANE, LANE), F32)

    return pl.pallas_call(
        body, name=name,
        out_shape=(pltpu.SemaphoreType.DMA((4 * n,)), pltpu.SemaphoreType.DMA((4 * n,)),
                   *[pltpu.HBM(b.shape, b.dtype) for b in bufs], jax.ShapeDtypeStruct((SUBLANE, LANE), F32)),
        in_specs=(*(_HBM,) * n, _ANY), out_specs=(_SEM, _SEM, *(_HBM,) * n, _TOKEN),
        input_output_aliases={b: 2 + b for b in range(n)},
        compiler_params=pltpu.CompilerParams(has_side_effects=_EFFECT),
    )(*[_hbm(b) for b in bufs], after)


def gather_wait(name, state, first, count, after):
    send_sems, recv_sems = state[:2]
    bufs = state[2 + first:2 + first + count]
    n = len(bufs)

    def body(*refs):
        ins = refs[:n]
        send_sems, recv_sems = refs[n], refs[n + 1]
        shift = 4 * first
        for send, arrival in _own_block_copies(
                ins, ins, send_sems.at[pl.ds(shift, 4 * n)], recv_sems.at[pl.ds(shift, 4 * n)], True):
            send.wait_send()
            arrival.wait_recv()

    return pl.pallas_call(
        body, name=name, out_shape=tuple(pltpu.HBM(b.shape, b.dtype) for b in bufs),
        in_specs=(*(_HBM,) * n, _SEM, _SEM, _ANY), out_specs=(_HBM,) * n,
        input_output_aliases={b: b for b in range(n)},
        compiler_params=pltpu.CompilerParams(has_side_effects=_EFFECT),
    )(*bufs, send_sems, recv_sems, after)


def _forward_copies(bufs, send_sems, recv_sems, arrivals):
    x, y, c = _place()
    copies = []
    for b, buf in enumerate(bufs):
        for k, chip in enumerate(_other_chips(x, y)):
            def copy(core, b=b, k=k, chip=chip, buf=buf):
                return pltpu.make_async_remote_copy(
                    src_ref=buf.at[4 * chip[0] + 2 * chip[1] + c], dst_ref=buf.at[4 * chip[0] + 2 * chip[1] + core],
                    send_sem=send_sems.at[3 * b + k], recv_sem=recv_sems.at[3 * b + k],
                    device_id=(x, y, 1 - c), device_id_type=pl.DeviceIdType.MESH)
            copies.append((copy(c), copy(1 - c) if arrivals else None))
    return copies


def gather_finish(name, bufs):
    n = len(bufs)

    def body(*refs):
        copies = _forward_copies(refs[n:2 * n], refs[2 * n], refs[2 * n + 1], True)
        for send, _ in copies:
            send.start()
        for send, arrival in copies:
            send.wait_send()
            arrival.wait_recv()

    return pl.pallas_call(
        body, name=name, out_shape=tuple(jax.ShapeDtypeStruct(b.shape, b.dtype) for b in bufs),
        in_specs=(_HBM,) * n, out_specs=(_HBM,) * n, input_output_aliases={b: b for b in range(n)},
        scratch_shapes=[pltpu.SemaphoreType.DMA((3 * n,)), pltpu.SemaphoreType.DMA((3 * n,))],
    )(*bufs)


def forward_start(name, bufs, carry):
    n = len(bufs)

    def body(*refs):
        send_sems, recv_sems = refs[n + 1], refs[n + 2]
        for send, _ in _forward_copies(refs[:n], send_sems, recv_sems, False):
            send.start()

    out = pl.pallas_call(
        body, name=name,
        out_shape=(pltpu.SemaphoreType.DMA((3 * n,)), pltpu.SemaphoreType.DMA((3 * n,)),
                   *[pltpu.HBM(b.shape, b.dtype) for b in bufs], pltpu.HBM(carry.shape, carry.dtype)),
        in_specs=(_HBM,) * (n + 1), out_specs=(_SEM, _SEM, *(_HBM,) * (n + 1)),
        input_output_aliases={b: 2 + b for b in range(n + 1)},
        compiler_params=pltpu.CompilerParams(has_side_effects=_EFFECT),
    )(*[_hbm(b) for b in bufs], _hbm(carry))
    return out[:-1], out[-1]


def forward_wait(name, state, after):
    send_sems, recv_sems, *bufs = state
    n = len(bufs)

    def body(*refs):
        for send, arrival in _forward_copies(refs[:n], refs[n], refs[n + 1], True):
            send.wait_send()
            arrival.wait_recv()

    return pl.pallas_call(
        body, name=name, out_shape=tuple(pltpu.HBM(b.shape, b.dtype) for b in bufs),
        in_specs=(*(_HBM,) * n, _SEM, _SEM, _ANY), out_specs=(_HBM,) * n,
        input_output_aliases={b: b for b in range(n)},
        compiler_params=pltpu.CompilerParams(has_side_effects=_EFFECT),
    )(*bufs, send_sems, recv_sems, after)


_SEM =pl.BlockSpec(memory_space=pltpu.SEMAPHORE)
_ANY = pl.BlockSpec(memory_space=pl.ANY)
_TOKEN = pl.BlockSpec(memory_space=pltpu.VMEM)
_EFFECT = pltpu.SideEffectType.DATAFLOW_SIDE_EFFECTING


def _hbm(a):
    return pltpu.with_memory_space_constraint(a, pltpu.HBM)


def _chip_copies(p_ref, land_ref, send_sems, recv_sems):
    x, y, c = _place()
    return [pltpu.make_async_remote_copy(
        src_ref=p_ref.at[2 * px + py], dst_ref=land_ref.at[k], send_sem=send_sems.at[k], recv_sem=recv_sems.at[k],
        device_id=(px, py, c), device_id_type=pl.DeviceIdType.MESH) for k, (px, py) in enumerate(_other_chips(x, y))]


def scatter_chips_start(name, p):
    _, rows, cols = p.shape

    def body(p_ref, land_ref, send_sems, recv_sems, p_thru, land_thru, token):
        for cp in _chip_copies(p_ref, land_ref, send_sems, recv_sems):
            cp.start()
        token[...] = jnp.zeros_like(token)

    return pl.pallas_call(
        body, name=name,
        out_shape=(pltpu.SemaphoreType.DMA((3,)), pltpu.SemaphoreType.DMA((3,)), pltpu.HBM(p.shape, p.dtype),
                   pltpu.HBM((3, rows, cols), p.dtype), jax.ShapeDtypeStruct((SUBLANE, LANE), F32)),
        in_specs=(_HBM, _HBM), out_specs=(_SEM, _SEM, _HBM, _HBM, _TOKEN), input_output_aliases={0: 2, 1: 3},
        compiler_params=pltpu.CompilerParams(has_side_effects=_EFFECT),
    )(_hbm(p), _hbm(lax.empty((3, rows, cols), p.dtype)))


def scatter_chips_wait(name, send_sems, recv_sems, p_thru, land_thru, after):
    def body(p_ref, land_ref, send_sems, recv_sems, after_ref, p_out, land_out):
        for cp in _chip_copies(p_ref, land_ref, send_sems, recv_sems):
            cp.wait_send()
            cp.wait_recv()

    return pl.pallas_call(
        body, name=name,
        out_shape=(pltpu.HBM(p_thru.shape, p_thru.dtype), pltpu.HBM(land_thru.shape, land_thru.dtype)),
        in_specs=(_HBM, _HBM, _SEM, _SEM, _ANY), out_specs=(_HBM, _HBM), input_output_aliases={0: 0, 1: 1},
        compiler_params=pltpu.CompilerParams(has_side_effects=_EFFECT),
    )(p_thru, land_thru, send_sems, recv_sems, after)


def _pair_copies(g_ref, land_ref, send_sems, recv_sems):
    x, y, c = _place()
    return [pltpu.make_async_remote_copy(
        src_ref=g_ref.at[k], dst_ref=land_ref.at[k], send_sem=send_sems.at[k], recv_sem=recv_sems.at[k],
        device_id=(x, y, 1 - c), device_id_type=pl.DeviceIdType.MESH) for k in range(N_DEV // 2)]


def pair_start(name, g, after, carry=None):
    n = g.shape[0]

    def body(g_ref, land_ref, after_ref, *rest):
        send_sems, recv_sems = rest[-5 if carry is None else -6:][:2]
        for cp in _pair_copies(g_ref, land_ref, send_sems, recv_sems):
            cp.start()
        token = rest[-1 if carry is None else -2]
        token[...] = jnp.zeros_like(token)

    extra = () if carry is None else (carry,)
    out = pl.pallas_call(
        body, name=name,
        out_shape=(pltpu.SemaphoreType.DMA((n,)), pltpu.SemaphoreType.DMA((n,)), pltpu.HBM(g.shape, g.dtype),
                   pltpu.HBM(g.shape, g.dtype), jax.ShapeDtypeStruct((SUBLANE, LANE), F32),
                   *[pltpu.HBM(c.shape, c.dtype) for c in extra]),
        in_specs=(_HBM, _HBM, _ANY, *(_HBM,) * len(extra)), out_specs=(_SEM, _SEM, _HBM, _HBM, _TOKEN, *(_HBM,) * len(extra)),
        input_output_aliases={0: 2, 1: 3, **({3: 5} if extra else {})},
        compiler_params=pltpu.CompilerParams(has_side_effects=_EFFECT),
    )(_hbm(g), _hbm(lax.empty(g.shape, g.dtype)), after, *[_hbm(c) for c in extra])
    return out if carry is None else (out[:5], out[5])


def pair_wait(name, state, after):
    send_sems, recv_sems, g_thru, land_thru, _ = state

    def body(g_ref, land_ref, send_sems, recv_sems, after_ref, g_out, land_out):
        for cp in _pair_copies(g_ref, land_ref, send_sems, recv_sems):
            cp.wait_send()
            cp.wait_recv()

    return pl.pallas_call(
        body, name=name,
        out_shape=(pltpu.HBM(g_thru.shape, g_thru.dtype), pltpu.HBM(land_thru.shape, land_thru.dtype)),
        in_specs=(_HBM, _HBM, _SEM, _SEM, _ANY), out_specs=(_HBM, _HBM), input_output_aliases={0: 0, 1: 1},
        compiler_params=pltpu.CompilerParams(has_side_effects=_EFFECT),
    )(g_thru, land_thru, send_sems, recv_sems, after)[1]


def reduce_scatter_wait(tag, state, after):
    send_sems, recv_sems, p_thru, land_thru, _ = state
    return scatter_chips_wait("rs_wait_" + tag, send_sems, recv_sems, p_thru, land_thru, after)


_SMALL = ("g_pre_mix", "g_post_mix", "g_pre_ffn", "g_post_ffn", "g_attn_grp", "g_conv_grp", "g_lru_grp",
          "dw_conv_w", "dw_conv_b", "conv_ln_g", "conv_ln_b", "lru_conv_w", "lru_conv_b",
          "lru_w_a", "lru_b_a", "lru_w_i", "lru_b_i", "lru_lambda")
_COL_SHARDED_SMALL = ("dw_conv_w", "lru_conv_w")
_BIG = ("w_in", "w_out", "w_gate", "w_up", "w_down")
_TRANSPOSED = ("w_gate", "w_up")
_ALL = ("w_in", "w_out", "g_pre_mix", "g_post_mix", "g_pre_ffn", "g_post_ffn", "g_attn_grp", "g_conv_grp", "g_lru_grp",
        "dw_conv_w", "dw_conv_b", "conv_ln_g", "conv_ln_b", "lru_conv_w", "lru_conv_b", "lru_w_a", "lru_b_a",
        "lru_w_i", "lru_b_i", "lru_lambda", "w_gate", "w_up", "w_down")


def _pack(arrays):
    flat = jnp.concatenate([a.reshape(-1) for a in arrays])
    pad = (-flat.shape[0]) % (PACK_ROWS * LANE)
    return jnp.pad(flat, (0, pad)).reshape(-1, LANE)


def _unpack(packed, shapes):
    flat = packed.reshape(-1)
    out, pos = [], 0
    for s in shapes:
        n = math.prod(s)
        out.append(flat[pos:pos + n].reshape(s))
        pos += n
    return out


def kernel(x, w_in, w_out, g_pre_mix, g_post_mix, g_pre_ffn, g_post_ffn, g_attn_grp, g_conv_grp, g_lru_grp, dw_conv_w, dw_conv_b, conv_ln_g, conv_ln_b, lru_conv_w, lru_conv_b, lru_w_a, lru_b_a, lru_w_i, lru_b_i, lru_lambda, w_gate, w_up, w_down, loss_target, m_w_in, m_w_out, m_g_pre_mix, m_g_post_mix, m_g_pre_ffn, m_g_post_ffn, m_g_attn_grp, m_g_conv_grp, m_g_lru_grp, m_dw_conv_w, m_dw_conv_b, m_conv_ln_g, m_conv_ln_b, m_lru_conv_w, m_lru_conv_b, m_lru_w_a, m_lru_b_a, m_lru_w_i, m_lru_b_i, m_lru_lambda, m_w_gate, m_w_up, m_w_down, v_w_in, v_w_out, v_g_pre_mix, v_g_post_mix, v_g_pre_ffn, v_g_post_ffn, v_g_attn_grp, v_g_conv_grp, v_g_lru_grp, v_dw_conv_w, v_dw_conv_b, v_conv_ln_g, v_conv_ln_b, v_lru_conv_w, v_lru_conv_b, v_lru_w_a, v_lru_b_a, v_lru_w_i, v_lru_b_i, v_lru_lambda, v_w_gate, v_w_up, v_w_down):
    env = dict(locals())
    wts = {n: env[n] for n in _ALL}
    mom = {n: env["m_" + n] for n in _ALL}
    var = {n: env["v_" + n] for n in _ALL}
    for group in (wts, mom, var):
        for n in _TRANSPOSED:
            group[n] = jnp.swapaxes(group[n], 1, 2)

    depth = w_in.shape[0]
    h = x[0]
    target = loss_target[0]
    t, d = h.shape
    attn_w = d // 2
    n_heads = attn_w // HEAD_DIM
    cc = d // 4
    wl = d // 4
    conv_col, lru_col = 3 * attn_w, 3 * attn_w + 2 * cc
    me = 4 * lax.axis_index("x") + 2 * lax.axis_index("y") + lax.axis_index("c")
    me_s = me.astype(jnp.int32).reshape(1)
    chip_s = (2 * lax.axis_index("x") + lax.axis_index("y")).astype(jnp.int32).reshape(1)
    core_s = lax.axis_index("c").astype(jnp.int32).reshape(1)

    n_taps = DW_LEN + LRU_LEN
    taps = jnp.concatenate([dw_conv_w, lru_conv_w], axis=1).reshape(depth * n_taps, cc // N_DEV)
    taps = all_gather("ag_taps", taps, me_s)
    taps = jnp.moveaxis(taps.reshape(N_DEV, depth, n_taps, cc // N_DEV), 0, 2).reshape(depth, n_taps, cc)
    dw_full, lcw_full = taps[:, :DW_LEN], taps[:, DW_LEN:]

    def vec(a, l):
        return a[l].reshape(1, -1)

    ag_state, started = [], taps
    for l in range(depth):
        ag_state.append(gather_start(f"ag_start_{l}", [place_own(wts[n][l], me_s, BF16) for n in _BIG], started))
        started = ag_state[l][-1]
    started = started[0:1, 0:1]

    saved = []
    u1 = rms_pre(h, vec(g_pre_mix, 0) + started)
    loss_sum = dh = dbr = None
    groups = ((0, 1), (1, 1), (2, 2), (4, 1))

    def issue(l, g, carry):
        first, count = groups[g]
        landed = gather_wait(f"ag_wait_{l}_{g}", ag_state[l], first, count, carry)
        return forward_start(f"ag_fwd_start_{l}_{g}", landed, carry)

    def collect(l, g, state, behind, wg):
        first, count = groups[g]
        wg.update(zip(_BIG[first:first + count], forward_wait(f"ag_fwd_wait_{l}_{g}", state, behind)))

    pending, u1 = issue(0, 0, u1)
    for l in range(depth):
        wg = {}
        collect(l, 0, pending, u1, wg)
        pending, u1 = issue(l, 1, u1)
        wa_b, wi_b = lru_w_a[l].astype(BF16), lru_w_i[l].astype(BF16)
        proj = mm_proj(u1, wg["w_in"])
        y_attn, att_runs = attn_fwd(proj, n_heads)
        cpre, y_conv = conv_fwd(proj, conv_col, cc, dw_full[l], vec(dw_conv_b, l), vec(conv_ln_g, l), vec(conv_ln_b, l))
        hs, y_lru = lru_fwd(proj, lru_col, wl, lcw_full[l], vec(lru_conv_b, l), wa_b, vec(lru_b_a, l), wi_b,
                            vec(lru_b_i, l), vec(lru_lambda, l))
        mixed = mix_fwd(y_attn, y_conv, y_lru, vec(g_attn_grp, l), vec(g_conv_grp, l), vec(g_lru_grp, l))
        collect(l, 1, pending, mixed, wg)
        pending, mixed = issue(l, 2, mixed)
        wg["w_out"] = wg["w_out"].reshape(attn_w + cc + wl, d)
        o = mm_plain("mm_out", mixed, wg["w_out"], "nn", F32)
        h2, u2 = res_norm(h, o, vec(g_post_mix, l), vec(g_pre_ffn, l))
        collect(l, 2, pending, u2, wg)
        pending, u2 = issue(l, 3, u2)
        f_gate, f_up, f = ffn_up(u2, wg["w_gate"], wg["w_up"])
        collect(l, 3, pending, f, wg)
        if l + 1 < depth:
            pending, f = issue(l + 1, 0, f)
        dn = mm_down(f, wg["w_down"])
        saved.append(dict(wg=wg, wa_b=wa_b, wi_b=wi_b, h=h, u1=u1, proj=proj, y_attn=y_attn, att_runs=att_runs, cpre=cpre, y_conv=y_conv,
                          hs=hs, y_lru=y_lru, mixed=mixed, o=o, h2=h2, u2=u2, f_gate=f_gate, f_up=f_up, f=f, dn=dn))
        if l + 1 < depth:
            h, u1 = res_norm(h2, dn, vec(g_post_ffn, l), vec(g_pre_mix, l + 1))
        else:
            loss_sum, dh, dbr, dg_post_ffn = final_loss(h2, dn, vec(g_post_ffn, l), target)

    loss = lax.psum(0.5 * loss_sum[0, 0] / d, MESH_AXES)

    small = {n: [None] * depth for n in _SMALL}
    rs_state = {n: [None] * depth for n in _BIG}
    after = dbr
    for l in reversed(range(depth)):
        s = saved[l]
        wg = s["wg"]
        small["g_post_ffn"][l] = dg_post_ffn
        dgt, dup = ffn_bwd(dbr, wg["w_down"], s["f_gate"], s["f_up"])
        ffn_grads = (("w_down", "down", s["f"], dbr), ("w_gate", "gate", dgt, s["u2"]), ("w_up", "up", dup, s["u2"]))
        pairs = {}
        for n, tag, a, g in ffn_grads:
            theirs = mm_dw_half("mm_dw_" + tag, "rows", a, g, core_s, True)
            pairs[n] = pair_start(f"pair_start_{tag}_{l}", theirs, after)
            after = pairs[n][-1]
        for n, tag, a, g in ffn_grads:
            recv = pair_wait(f"pair_wait_{tag}_{l}", pairs[n], after)
            after = mm_dw_half("mm_dw_" + tag, "rows", a, g, core_s, False, add=recv)
            rs_state[n][l] = scatter_chips_start(f"rs_start_{tag}_{l}", after)
        du2 = mm_dx_ffn(dgt, wg["w_gate"], dup, wg["w_up"])
        started = sum(rs_state[n][l][-1][0:1, 0:1] for n in ("w_down", "w_gate", "w_up"))
        dh2, small["g_pre_ffn"][l], do, small["g_post_mix"][l] = norm_bwd(
            dh, du2, s["h2"], vec(g_pre_ffn, l) + started, (s["o"], vec(g_post_mix, l)))
        theirs = mm_dw_half("mm_dw_out", "take", s["mixed"], do, core_s, True)
        pairs["w_out"], do = pair_start(f"pair_start_out_{l}", theirs, rs_state["w_up"][l][-1], carry=do)
        dmixed = mm_plain("mm_dmixed", do, wg["w_out"], "nt", F32)
        recv = pair_wait(f"pair_wait_out_{l}", pairs["w_out"], dmixed)
        rs_state["w_out"][l] = scatter_chips_start(
            f"rs_start_out_{l}", mm_dw_half("mm_dw_out", "take", s["mixed"], do, core_s, False, add=recv))
        (dya, dc, dyl, small["g_attn_grp"][l], small["g_conv_grp"][l], small["g_lru_grp"][l],
         small["conv_ln_g"][l], small["conv_ln_b"][l]) = mix_bwd(
            dmixed, s["y_attn"], s["y_conv"], s["y_lru"], s["cpre"],
            vec(g_attn_grp, l) + rs_state["w_out"][l][-1][0:1, 0:1], vec(g_conv_grp, l),
            vec(g_lru_grp, l), vec(conv_ln_g, l), vec(conv_ln_b, l))
        dq, dk, dv = attn_bwd(s["proj"], dya, s["att_runs"], n_heads)
        dvg, small["dw_conv_w"][l], small["dw_conv_b"][l] = conv_bwd(s["proj"], conv_col, cc, dc, dw_full[l])
        (dxy, small["lru_conv_w"][l], small["lru_conv_b"][l], small["lru_w_a"][l], small["lru_b_a"][l],
         small["lru_w_i"][l], small["lru_b_i"][l], small["lru_lambda"][l]) = lru_bwd(
            s["proj"], lru_col, wl, s["hs"], dyl, lcw_full[l], vec(lru_conv_b, l), s["wa_b"], vec(lru_b_a, l),
            s["wi_b"], vec(lru_b_i, l), vec(lru_lambda, l))
        dproj = jnp.concatenate([dq, dk, dv, dvg, dxy], axis=1)
        theirs = mm_dw_half("mm_dw_in", "cols", s["u1"], dproj, core_s, True)
        pairs["w_in"], dproj = pair_start(f"pair_start_in_{l}", theirs, rs_state["w_out"][l][-1], carry=dproj)
        du1 = mm_dx_cols("mm_dx_in", dproj, wg["w_in"])
        recv = pair_wait(f"pair_wait_in_{l}", pairs["w_in"], du1)
        after = mm_dw_half("mm_dw_in", "cols", s["u1"], dproj, core_s, False, add=recv)
        rs_state["w_in"][l] = scatter_chips_start(f"rs_start_in_{l}", after)
        g_pre = vec(g_pre_mix, l) + rs_state["w_in"][l][-1][0:1, 0:1]
        if l > 0:
            p = saved[l - 1]
            dh, small["g_pre_mix"][l], dbr, dg_post_ffn = norm_bwd(
                dh2, du1, s["h"], g_pre, (p["dn"], vec(g_post_ffn, l - 1)))
        else:
            dh, small["g_pre_mix"][l] = norm_bwd(dh2, du1, s["h"], g_pre)
    grad_x = dh[None]

    small_shapes = [(depth,) + tuple(wts[n].shape[1:]) if n not in _COL_SHARDED_SMALL
                    else (depth, wts[n].shape[1], cc) for n in _SMALL]
    part = _pack([a for n in _SMALL for a in small[n]])
    small_state = gather_start("ag_small_start", [place_own(part, me_s, F32)], dh)

    grads, delta, new_m, new_v = {}, {}, {}, {}
    behind = small_state[-1]
    for n in reversed(_BIG):
        parts = [reduce_scatter_wait(f"{n[2:]}_{l}", rs_state[n][l], behind) for l in range(depth)]
        shape = wts[n].shape
        _, rows, cols = parts[0][0].shape
        view = (depth, rows, cols)
        g, dl, mn, vn = adam_big("adam_" + n, wts[n].reshape(view), mom[n].reshape(view), var[n].reshape(view),
                                 parts, chip_s)
        grads[n], delta[n], new_m[n], new_v[n] = (a.reshape(shape) for a in (g, dl, mn, vn))
        behind = g
    for group in (grads, delta, new_m, new_v):
        for n in _TRANSPOSED:
            group[n] = jnp.swapaxes(group[n], 1, 2)

    gathered, = gather_finish("ag_small_finish", gather_wait("ag_small_wait", small_state, 0, 1, behind))
    g_small = _unpack(sum_parts(gathered), small_shapes)
    for n, g in zip(_SMALL, g_small):
        if n in _COL_SHARDED_SMALL:
            g = lax.dynamic_slice_in_dim(g, me * (cc // N_DEV), cc // N_DEV, axis=2)
        grads[n] = g
    local_shapes = [tuple(wts[n].shape) for n in _SMALL]
    d_small, m_small, v_small = adam_small(
        _pack([wts[n] for n in _SMALL]), _pack([mom[n] for n in _SMALL]), _pack([var[n] for n in _SMALL]),
        _pack([grads[n] for n in _SMALL]))
    delta.update(zip(_SMALL, _unpack(d_small, local_shapes)))
    new_m.update(zip(_SMALL, _unpack(m_small, local_shapes)))
    new_v.update(zip(_SMALL, _unpack(v_small, local_shapes)))

    return (loss, grad_x, *[grads[n] for n in _ALL], *[delta[n] for n in _ALL],
            *[new_m[n] for n in _ALL], *[new_v[n] for n in _ALL])
```

```python
import functools
import math

import jax
import jax.numpy as jnp
from jax import lax
from jax.experimental import pallas as pl
from jax.experimental.pallas import tpu as pltpu

F32 = jnp.float32
BF16 = jnp.bfloat16

N_DEV = 8
EPS = 1e-6
HEAD_DIM = 128
DW_LEN = 31
LRU_LEN = 4
LRU_BLOCKS = 4
LRU_C = 8.0
ATT_TQ = 512
ATT_TK = 512
ATT_SUM = 256
ATT_HEADS = 2
ROW_T = 256
CONV_HALO = 32
LRU_HALO = 8
LANE = 128
SUBLANE = 8
WIDE_TILE = 2048
PACK_ROWS = 512
VMEM_LIMIT = 56 * 1024 * 1024

ADAM_LR = 0.001
ADAM_B1 = 0.9
ADAM_B2 = 0.999
ADAM_EPS = 1e-08
ADAM_WD = 0.01
ADAM_STEP = 10

MESH_AXES = ("x", "y", "c")
_DIMS = {
    "nn": (((1,), (0,)), ((), ())),
    "nt": (((1,), (1,)), ((), ())),
    "tn": (((0,), (0,)), ((), ())),
}


def _params(n_axes):
    return pltpu.CompilerParams(
        dimension_semantics=("arbitrary",) * n_axes, vmem_limit_bytes=VMEM_LIMIT)


def _dot(a, b, mode="nn"):
    return lax.dot_general(a, b, _DIMS[mode], preferred_element_type=F32)


def _sigmoid(x):
    return 1.0 / (1.0 + jnp.exp(-x))


def _softplus(x):
    return jnp.maximum(x, 0.0) + jnp.log(1.0 + jnp.exp(-jnp.abs(x)))


def _neg_expm1(x):
    series = x * (1.0 + x * (0.5 + x * (1.0 / 6 + x * (1.0 / 24 + x * (1.0 / 120 + x * (1.0 / 720))))))
    return jnp.where(x > -0.25, -series, 1.0 - jnp.exp(x))


_GELU_C = math.sqrt(2.0 / math.pi)


def _gelu_and_grad(x):
    inner = _GELU_C * (x + 0.044715 * x * x * x)
    t = jnp.tanh(inner)
    val = 0.5 * x * (1.0 + t)
    grad = 0.5 * (1.0 + t) + 0.5 * x * (1.0 - t * t) * _GELU_C * (1.0 + 3 * 0.044715 * x * x)
    return val, grad


def _rms_stats(x):
    r = lax.rsqrt(jnp.mean(x * x, axis=-1, keepdims=True) + EPS)
    return x * r, r


def _rms_bwd(dy, x, g):
    xn, r = _rms_stats(x)
    dxn = dy * g
    dx = r * (dxn - xn * jnp.mean(dxn * xn, axis=-1, keepdims=True))
    return dx, jnp.sum(dy * xn, axis=0, keepdims=True)


def _row_spec(tr, width, col=0):
    return pl.BlockSpec((tr, width), lambda i, col=col: (i, col))


def _vec_spec(width):
    return pl.BlockSpec((1, width), lambda i: (0, 0))


def _matmul(name, mode, operands, in_specs, out_shape, out_spec, grid):
    npairs = len(operands) // 2
    nk = grid[2]
    assert nk == 1 or out_shape.dtype == F32

    def body(*refs):
        o_ref = refs[2 * npairs]

        def partial():
            acc = None
            for p in range(npairs):
                d = _dot(refs[2 * p][...], refs[2 * p + 1][...], mode)
                acc = d if acc is None else acc + d
            return acc

        if nk == 1:
            o_ref[...] = partial().astype(o_ref.dtype)
        else:
            k = pl.program_id(2)

            @pl.when(k == 0)
            def _():
                o_ref[...] = partial()

            @pl.when(k > 0)
            def _():
                o_ref[...] += partial()

    return pl.pallas_call(
        body, name=name, grid=grid, in_specs=in_specs, out_specs=out_spec, out_shape=out_shape,
        compiler_params=_params(3),
    )(*operands)


def _tile(n, t):
    if n <= t:
        return n
    return max(k for k in range(SUBLANE, t + 1, SUBLANE) if n % k == 0)


def mm_proj(u, w):
    t, d = u.shape
    nblk, _, nb = w.shape
    tm = _tile(t, WIDE_TILE)
    return _matmul(
        "mm_proj", "nn", (u, w),
        [pl.BlockSpec((tm, d), lambda j, i, k: (i, 0)), pl.BlockSpec((None, d, nb), lambda j, i, k: (j, 0, 0))],
        jax.ShapeDtypeStruct((t, nblk * nb), F32), pl.BlockSpec((tm, nb), lambda j, i, k: (i, j)),
        (nblk, t // tm, 1))


def mm_plain(name, a, b, mode, out_dtype):
    if mode == "nn":
        (m, kk), n = a.shape, b.shape[1]
    elif mode == "nt":
        (m, kk), n = a.shape, b.shape[0]
    else:
        (kk, m), n = a.shape, b.shape[1]
    tm, tn = _tile(m, 1024), _tile(n, 1024)
    a_spec = (pl.BlockSpec((kk, tm), lambda i, j, k: (0, i)) if mode == "tn"
              else pl.BlockSpec((tm, kk), lambda i, j, k: (i, 0)))
    b_spec = (pl.BlockSpec((tn, kk), lambda i, j, k: (j, 0)) if mode == "nt"
              else pl.BlockSpec((kk, tn), lambda i, j, k: (0, j)))
    return _matmul(
        name, mode, (a, b), [a_spec, b_spec],
        jax.ShapeDtypeStruct((m, n), out_dtype), pl.BlockSpec((tm, tn), lambda i, j, k: (i, j)),
        (m // tm, n // tn, 1))


def mm_down(f, w):
    nblk, t, fb = f.shape
    d = w.shape[2]
    tm, tn = _tile(t, 1024), _tile(d, WIDE_TILE)
    return _matmul(
        "mm_down", "nn", (f, w),
        [pl.BlockSpec((None, tm, fb), lambda i, j, k: (k, i, 0)), pl.BlockSpec((None, fb, tn), lambda i, j, k: (k, 0, j))],
        jax.ShapeDtypeStruct((t, d), F32), pl.BlockSpec((tm, tn), lambda i, j, k: (i, j)),
        (t // tm, d // tn, nblk))


def mm_dw_half(name, kind, a, g, core, of_sibling, add=None):
    half = N_DEV // 2
    t = g.shape[0]

    def pick(k, core_ref):
        s = 1 - core_ref[0] if of_sibling else core_ref[0]
        return 2 * k + s

    if kind == "rows":
        rows, cols = a.shape[2], g.shape[1]
        tr, tc = rows, _tile(cols, WIDE_TILE)
        a_spec = pl.BlockSpec((None, t, rows), lambda k, i, n, core_ref: (pick(k, core_ref), 0, 0))
        g_spec = pl.BlockSpec((t, tc), lambda k, i, n, core_ref: (0, n))
    elif kind == "cols":
        rows, cols = a.shape[1], g.shape[1] // N_DEV
        tr, tc = _tile(rows, WIDE_TILE), cols
        a_spec = pl.BlockSpec((t, tr), lambda k, i, n, core_ref: (0, i))
        g_spec = pl.BlockSpec((t, cols), lambda k, i, n, core_ref: (0, pick(k, core_ref)))
    else:
        rows, cols = a.shape[1] // N_DEV, g.shape[1]
        tr, tc = rows, _tile(cols, WIDE_TILE)
        a_spec = pl.BlockSpec((t, rows), lambda k, i, n, core_ref: (0, pick(k, core_ref)))
        g_spec = pl.BlockSpec((t, tc), lambda k, i, n, core_ref: (0, n))
    o_spec = pl.BlockSpec((None, tr, tc), lambda k, i, n, core_ref: (k, i, n))

    def body(core_ref, a_ref, g_ref, *rest):
        acc = _dot(a_ref[...], g_ref[...], "tn")
        if add is not None:
            acc = acc + rest[0][...].astype(F32)
        rest[-1][...] = acc.astype(BF16)

    return pl.pallas_call(
        body, name=name,
        grid_spec=pltpu.PrefetchScalarGridSpec(
            num_scalar_prefetch=1, grid=(half, rows // tr, cols // tc),
            in_specs=[a_spec, g_spec] + ([o_spec] if add is not None else []), out_specs=o_spec),
        out_shape=jax.ShapeDtypeStruct((half, rows, cols), BF16), compiler_params=_params(3),
    )(core, a, g, *(() if add is None else (add,)))


def mm_dx_cols(name, g, w):
    t = g.shape[0]
    nblk, d, nb = w.shape
    tm, tn = _tile(t, 1024), _tile(d, WIDE_TILE)
    return _matmul(
        name, "nt", (g, w),
        [pl.BlockSpec((tm, nb), lambda i, j, k: (i, k)), pl.BlockSpec((None, tn, nb), lambda i, j, k: (k, j, 0))],
        jax.ShapeDtypeStruct((t, d), F32), pl.BlockSpec((tm, tn), lambda i, j, k: (i, j)),
        (t // tm, d // tn, nblk))


def mm_dx_ffn(dgt, wg, dup, wu):
    nblk, t, fb = dgt.shape
    d = wg.shape[2]
    tm, tn = _tile(t, 1024), _tile(d, WIDE_TILE)
    a_spec = pl.BlockSpec((None, tm, fb), lambda i, j, k: (k, i, 0))
    b_spec = pl.BlockSpec((None, fb, tn), lambda i, j, k: (k, 0, j))
    return _matmul(
        "mm_dx_ffn", "nn", (dgt, wg, dup, wu), [a_spec, b_spec, a_spec, b_spec],
        jax.ShapeDtypeStruct((t, d), F32), pl.BlockSpec((tm, tn), lambda i, j, k: (i, j)),
        (t // tm, d // tn, nblk))


def ffn_up(u, wg, wu):
    t, d = u.shape
    nblk, fb, _ = wg.shape
    tm = _tile(t, 512)

    def body(u_ref, wg_ref, wu_ref, dgt_ref, dup_ref, f_ref):
        for r0 in range(0, t, tm):
            rows = pl.ds(r0, tm)
            uu = u_ref[rows, :]
            gt = _dot(uu, wg_ref[...], "nt")
            up = _dot(uu, wu_ref[...], "nt")
            s = _sigmoid(gt)
            silu = gt * s
            dgt_ref[rows, :] = (up * s * (1.0 + gt * (1.0 - s))).astype(BF16)
            dup_ref[rows, :] = silu.astype(BF16)
            f_ref[rows, :] = (silu * up).astype(BF16)

    w_spec = pl.BlockSpec((None, fb, d), lambda j: (j, 0, 0))
    o_spec = pl.BlockSpec((None, t, fb), lambda j: (j, 0, 0))
    return pl.pallas_call(
        body, name="ffn_up", grid=(nblk,),
        in_specs=[pl.BlockSpec((t, d), lambda j: (0, 0)), w_spec, w_spec],
        out_specs=[o_spec, o_spec, o_spec],
        out_shape=[jax.ShapeDtypeStruct((nblk, t, fb), BF16)] * 3,
        compiler_params=_params(1),
    )(u, wg, wu)


def ffn_bwd(dd, wd, f_gate, f_up):
    t, d = dd.shape
    nblk, fb, _ = wd.shape
    tm = _tile(t, 512)

    def body(dd_ref, wd_ref, fg_ref, fu_ref, dgt_ref, dup_ref):
        for r0 in range(0, t, tm):
            rows = pl.ds(r0, tm)
            df = _dot(dd_ref[rows, :], wd_ref[...], "nt")
            dgt_ref[rows, :] = (df * fg_ref[rows, :].astype(F32)).astype(BF16)
            dup_ref[rows, :] = (df * fu_ref[rows, :].astype(F32)).astype(BF16)

    s_spec = pl.BlockSpec((None, t, fb), lambda j: (j, 0, 0))
    return pl.pallas_call(
        body, name="ffn_bwd", grid=(nblk,),
        in_specs=[pl.BlockSpec((t, d), lambda j: (0, 0)), pl.BlockSpec((None, fb, d), lambda j: (j, 0, 0)),
                  s_spec, s_spec],
        out_specs=[s_spec, s_spec],
        out_shape=[jax.ShapeDtypeStruct((nblk, t, fb), BF16)] * 2,
        compiler_params=_params(1),
    )(dd, wd, f_gate, f_up)


def rms_pre(h, g):
    t, d = h.shape
    tr = _tile(t, ROW_T)

    def body(h_ref, g_ref, o_ref):
        xn, _ = _rms_stats(h_ref[...])
        o_ref[...] = (xn * g_ref[...]).astype(BF16)

    return pl.pallas_call(
        body, name="rms_pre", grid=(t // tr,),
        in_specs=[_row_spec(tr, d), _vec_spec(d)], out_specs=_row_spec(tr, d),
        out_shape=jax.ShapeDtypeStruct((t, d), BF16), compiler_params=_params(1),
    )(h, g)


def res_norm(h, o, g_post, g_pre):
    t, d = h.shape
    tr = _tile(t, ROW_T)

    def body(h_ref, o_ref, gpo_ref, gpr_ref, h2_ref, u_ref):
        on, _ = _rms_stats(o_ref[...])
        h2 = h_ref[...] + on * gpo_ref[...]
        h2_ref[...] = h2
        hn, _ = _rms_stats(h2)
        u_ref[...] = (hn * gpr_ref[...]).astype(BF16)

    return pl.pallas_call(
        body, name="res_norm", grid=(t // tr,),
        in_specs=[_row_spec(tr, d), _row_spec(tr, d), _vec_spec(d), _vec_spec(d)],
        out_specs=[_row_spec(tr, d), _row_spec(tr, d)],
        out_shape=[jax.ShapeDtypeStruct((t, d), F32), jax.ShapeDtypeStruct((t, d), BF16)],
        compiler_params=_params(1),
    )(h, o, g_post, g_pre)


def final_loss(h2, dbr, g_post, target):
    t, d = h2.shape
    tr = _tile(t, ROW_T)

    def body(h2_ref, d_ref, g_ref, tg_ref, loss_ref, dy_ref, dd_ref, dg_ref):
        i = pl.program_id(0)

        @pl.when(i == 0)
        def _():
            loss_ref[...] = jnp.zeros_like(loss_ref)
            dg_ref[...] = jnp.zeros_like(dg_ref)

        x = d_ref[...]
        g = g_ref[...]
        xn, _ = _rms_stats(x)
        diff = h2_ref[...] + xn * g - tg_ref[...]
        loss_ref[...] += jnp.sum(jnp.sum(diff * diff, axis=1, keepdims=True), axis=0, keepdims=True)
        dy = diff * (1.0 / d)
        dy_ref[...] = dy
        dx, dg = _rms_bwd(dy, x, g)
        dd_ref[...] = dx.astype(BF16)
        dg_ref[...] += dg

    return pl.pallas_call(
        body, name="final_loss", grid=(t // tr,),
        in_specs=[_row_spec(tr, d), _row_spec(tr, d), _vec_spec(d), _row_spec(tr, d)],
        out_specs=[pl.BlockSpec((1, 1), lambda i: (0, 0)), _row_spec(tr, d), _row_spec(tr, d), _vec_spec(d)],
        out_shape=[jax.ShapeDtypeStruct((1, 1), F32), jax.ShapeDtypeStruct((t, d), F32),
                   jax.ShapeDtypeStruct((t, d), BF16), jax.ShapeDtypeStruct((1, d), F32)],
        compiler_params=_params(1),
    )(h2, dbr, g_post, target)


def norm_bwd(dh_out, du, h_in, g_pre, prev=None):
    t, d = h_in.shape
    tr = _tile(t, ROW_T)
    with_prev = prev is not None

    def body(*refs):
        if with_prev:
            dho_ref, du_ref, h_ref, gpr_ref, br_ref, gpo_ref, dh_ref, dgpr_ref, dbr_ref, dgpo_ref = refs
        else:
            dho_ref, du_ref, h_ref, gpr_ref, dh_ref, dgpr_ref = refs
        i = pl.program_id(0)

        @pl.when(i == 0)
        def _():
            dgpr_ref[...] = jnp.zeros_like(dgpr_ref)
            if with_prev:
                dgpo_ref[...] = jnp.zeros_like(dgpo_ref)

        dx, dg = _rms_bwd(du_ref[...], h_ref[...], gpr_ref[...])
        dh = dho_ref[...] + dx
        dh_ref[...] = dh
        dgpr_ref[...] += dg
        if with_prev:
            dbr, dg2 = _rms_bwd(dh, br_ref[...], gpo_ref[...])
            dbr_ref[...] = dbr.astype(BF16)
            dgpo_ref[...] += dg2

    row, vec = _row_spec(tr, d), _vec_spec(d)
    in_specs = [row, row, row, vec] + ([row, vec] if with_prev else [])
    out_specs = [row, vec] + ([row, vec] if with_prev else [])
    out_shape = [jax.ShapeDtypeStruct((t, d), F32), jax.ShapeDtypeStruct((1, d), F32)]
    if with_prev:
        out_shape += [jax.ShapeDtypeStruct((t, d), BF16), jax.ShapeDtypeStruct((1, d), F32)]
    args = (dh_out, du, h_in, g_pre) + (tuple(prev) if with_prev else ())
    return pl.pallas_call(
        body, name="norm_bwd_chain" if with_prev else "norm_bwd_first", grid=(t // tr,),
        in_specs=in_specs, out_specs=out_specs, out_shape=out_shape, compiler_params=_params(1),
    )(*args)


def mix_fwd(ya, yc, yl, ga, gc, gl):
    t, wa = ya.shape
    wc, wl = yc.shape[1], yl.shape[1]
    tr = _tile(t, ROW_T)

    def body(ya_ref, yc_ref, yl_ref, ga_ref, gc_ref, gl_ref, o_ref):
        o_ref[:, pl.ds(0, wa)] = (_rms_stats(ya_ref[...])[0] * ga_ref[...]).astype(BF16)
        o_ref[:, pl.ds(wa, wc)] = (_rms_stats(yc_ref[...])[0] * gc_ref[...]).astype(BF16)
        o_ref[:, pl.ds(wa + wc, wl)] = (_rms_stats(yl_ref[...])[0] * gl_ref[...]).astype(BF16)

    return pl.pallas_call(
        body, name="mix_fwd", grid=(t // tr,),
        in_specs=[_row_spec(tr, wa), _row_spec(tr, wc), _row_spec(tr, wl), _vec_spec(wa), _vec_spec(wc), _vec_spec(wl)],
        out_specs=_row_spec(tr, wa + wc + wl),
        out_shape=jax.ShapeDtypeStruct((t, wa + wc + wl), BF16), compiler_params=_params(1),
    )(ya, yc, yl, ga, gc, gl)


def mix_bwd(dmixed, ya, yc, yl, cpre, ga, gc, gl, lng, lnb):
    t, wa = ya.shape
    wc, wl = yc.shape[1], yl.shape[1]
    tr = _tile(t, ROW_T)

    def body(dm_ref, ya_ref, yc_ref, yl_ref, c_ref, ga_ref, gc_ref, gl_ref, lg_ref, lb_ref,
             dya_ref, dc_ref, dyl_ref, dga_ref, dgc_ref, dgl_ref, dlg_ref, dlb_ref):
        i = pl.program_id(0)

        @pl.when(i == 0)
        def _():
            for r in (dga_ref, dgc_ref, dgl_ref, dlg_ref, dlb_ref):
                r[...] = jnp.zeros_like(r)

        dya, dga = _rms_bwd(dm_ref[:, pl.ds(0, wa)], ya_ref[...], ga_ref[...])
        dya_ref[...] = dya
        dga_ref[...] += dga
        dyl, dgl = _rms_bwd(dm_ref[:, pl.ds(wa + wc, wl)], yl_ref[...], gl_ref[...])
        dyl_ref[...] = dyl
        dgl_ref[...] += dgl
        dyc, dgc = _rms_bwd(dm_ref[:, pl.ds(wa, wc)], yc_ref[...], gc_ref[...])
        dgc_ref[...] += dgc
        c = c_ref[...]
        xc = c - jnp.mean(c, axis=-1, keepdims=True)
        rstd = lax.rsqrt(jnp.mean(xc * xc, axis=-1, keepdims=True) + EPS)
        xhat = xc * rstd
        ln = xhat * lg_ref[...] + lb_ref[...]
        s = _sigmoid(ln)
        dln = dyc * s * (1.0 + ln * (1.0 - s))
        dlg_ref[...] += jnp.sum(dln * xhat, axis=0, keepdims=True)
        dlb_ref[...] += jnp.sum(dln, axis=0, keepdims=True)
        dxh = dln * lg_ref[...]
        dc_ref[...] = rstd * (dxh - jnp.mean(dxh, axis=-1, keepdims=True)
                              - xhat * jnp.mean(dxh * xhat, axis=-1, keepdims=True))

    return pl.pallas_call(
        body, name="mix_bwd", grid=(t // tr,),
        in_specs=[_row_spec(tr, wa + wc + wl), _row_spec(tr, wa), _row_spec(tr, wc), _row_spec(tr, wl), _row_spec(tr, wc),
                  _vec_spec(wa), _vec_spec(wc), _vec_spec(wl), _vec_spec(wc), _vec_spec(wc)],
        out_specs=[_row_spec(tr, wa), _row_spec(tr, wc), _row_spec(tr, wl),
                   _vec_spec(wa), _vec_spec(wc), _vec_spec(wl), _vec_spec(wc), _vec_spec(wc)],
        out_shape=[jax.ShapeDtypeStruct((t, wa), F32), jax.ShapeDtypeStruct((t, wc), F32), jax.ShapeDtypeStruct((t, wl), F32),
                   jax.ShapeDtypeStruct((1, wa), F32), jax.ShapeDtypeStruct((1, wc), F32), jax.ShapeDtypeStruct((1, wl), F32),
                   jax.ShapeDtypeStruct((1, wc), F32), jax.ShapeDtypeStruct((1, wc), F32)],
        compiler_params=_params(1),
    )(dmixed, ya, yc, yl, cpre, ga, gc, gl, lng, lnb)


def _hi_lo(x):
    hi = x.astype(BF16)
    return hi, (x - hi.astype(F32)).astype(BF16)


def _lane_sums(x, tri, reverse):
    nsub = x.shape[1] // tri.shape[0]
    order = range(nsub - 1, -1, -1) if reverse else range(nsub)
    parts, beyond, halves = {}, None, []
    for b in order:
        blk = x[:, b * tri.shape[0]:(b + 1) * tri.shape[0]]
        hi, lo = _hi_lo(blk)
        halves += [hi, lo]
        c = _dot(hi, tri) + _dot(lo, tri)
        parts[b] = c if beyond is None else c + beyond
        tot = jnp.sum(blk, axis=1, keepdims=True)
        beyond = tot if beyond is None else beyond + tot
    return jnp.concatenate([parts[b] for b in range(nsub)], axis=1), beyond, halves


def _att_strip(qb, kt, thresh, scale, diff, tri_gt):
    z = _dot(qb, kt, "nt") * scale
    sp = _softplus(z)
    mask = diff < thresh
    later, total, halves = _lane_sums(jnp.where(mask, -sp, 0.0), tri_gt, True)
    return z, sp, mask, total, (z - sp) + later, halves


def _att_consts(tq, tk):
    diff = lax.broadcasted_iota(jnp.int32, (tq, tk), 1) - lax.broadcasted_iota(jnp.int32, (tq, tk), 0)
    cb = min(tk, ATT_SUM)
    row = lax.broadcasted_iota(jnp.int32, (cb, cb), 0)
    col = lax.broadcasted_iota(jnp.int32, (cb, cb), 1)
    return diff, (row > col).astype(BF16), (row < col).astype(BF16)


def attn_fwd(proj, n_heads):
    t = proj.shape[0]
    tq, tk = _tile(t, ATT_TQ), _tile(t, ATT_TK)
    hp = ATT_HEADS
    wd = hp * HEAD_DIM
    scale = HEAD_DIM ** -0.5

    def body(q_ref, k_ref, v_ref, o_ref, r_ref, kb_ref, vb_ref, acc_ref):
        kb_ref[...] = k_ref[...].astype(BF16)
        vb_ref[...] = v_ref[...].astype(BF16)
        diff, tri_gt, _ = _att_consts(tq, tk)
        ones = jnp.ones((2 * SUBLANE, min(tk, ATT_SUM)), BF16)

        def qblock(i, _):
            q0 = pl.multiple_of(i * tq, tq)
            heads = [pl.ds(h * HEAD_DIM, HEAD_DIM) for h in range(hp)]
            qbs = [q_ref[pl.ds(q0, tq), hs].astype(BF16) for hs in heads]
            acc_ref[...] = jnp.zeros_like(acc_ref)
            n_strips = (q0 + tq + tk - 1) // tk

            def strip(jj, carry):
                si = n_strips - 1 - jj
                k0 = pl.multiple_of(si * tk, tk)
                out = []
                for h, hs in enumerate(heads):
                    run, run_row = carry[2 * h], carry[2 * h + 1]
                    _, _, mask, total, logw, halves = _att_strip(
                        qbs[h], kb_ref[pl.ds(k0, tk), hs], q0 - k0, scale, diff, tri_gt)
                    w = jnp.where(mask, jnp.exp(logw + run), 0.0)
                    acc_ref[:, hs] += _dot(w.astype(BF16), vb_ref[pl.ds(k0, tk), hs])
                    r_ref[h, i, si] = run_row
                    total_row = sum(_dot(ones, half, "nt") for half in halves)[:SUBLANE]
                    out += [run + total, run_row + total_row]
                return tuple(out)

            init = (jnp.zeros((tq, 1), F32), jnp.zeros((SUBLANE, tq), F32)) * hp
            lax.fori_loop(0, n_strips, strip, init)
            o_ref[pl.ds(q0, tq), :] = acc_ref[...]
            return 0

        lax.fori_loop(0, t // tq, qblock, 0)

    def col_spec(base):
        return pl.BlockSpec((t, wd), lambda h, base=base: (0, base + h))

    ng = n_heads // hp
    runs_shape = (n_heads, t // tq, t // tk, SUBLANE, tq)
    return pl.pallas_call(
        body, name="attn_fwd", grid=(ng,),
        in_specs=[col_spec(0), col_spec(ng), col_spec(2 * ng)],
        out_specs=[col_spec(0), pl.BlockSpec((hp,) + runs_shape[1:], lambda h: (h, 0, 0, 0, 0))],
        out_shape=[jax.ShapeDtypeStruct((t, n_heads * HEAD_DIM), F32), jax.ShapeDtypeStruct(runs_shape, F32)],
        scratch_shapes=[pltpu.VMEM((t, wd), BF16), pltpu.VMEM((t, wd), BF16), pltpu.VMEM((tq, wd), F32)],
        compiler_params=_params(1),
    )(proj, proj, proj)


def attn_bwd(proj, dy, runs, n_heads):
    t = proj.shape[0]
    tq, tk = _tile(t, ATT_TQ), _tile(t, ATT_TK)
    hp = ATT_HEADS
    wd = hp * HEAD_DIM
    scale = HEAD_DIM ** -0.5

    def body(q_ref, k_ref, v_ref, dy_ref, r_ref, dq_ref, dk_ref, dv_ref, qb_ref, kb_ref, vb_ref, dob_ref, dk_acc,
             dv_acc, dq_acc):
        qb_ref[...] = q_ref[...].astype(BF16)
        kb_ref[...] = k_ref[...].astype(BF16)
        vb_ref[...] = v_ref[...].astype(BF16)
        dob_ref[...] = dy_ref[...].astype(BF16)
        dk_acc[...] = jnp.zeros_like(dk_acc)
        dv_acc[...] = jnp.zeros_like(dv_acc)
        diff, tri_gt, tri_lt = _att_consts(tq, tk)

        def qblock(i, _):
            q0 = pl.multiple_of(i * tq, tq)
            heads = [pl.ds(h * HEAD_DIM, HEAD_DIM) for h in range(hp)]
            qbs = [qb_ref[pl.ds(q0, tq), hs] for hs in heads]
            dobs = [dob_ref[pl.ds(q0, tq), hs] for hs in heads]
            dq_acc[...] = jnp.zeros_like(dq_acc)
            n_strips = (q0 + tq + tk - 1) // tk

            zero = tuple(jnp.zeros((tq, 1), F32) for _ in range(hp))

            def strip(si, gsums):
                k0 = pl.multiple_of(si * tk, tk)
                out = []
                for h, hs in enumerate(heads):
                    kt = kb_ref[pl.ds(k0, tk), hs]
                    vt = vb_ref[pl.ds(k0, tk), hs]
                    z, sp, mask, _, logw, _ = _att_strip(qbs[h], kt, q0 - k0, scale, diff, tri_gt)
                    run = jnp.concatenate([r_ref[h, i, si]] * (LANE // SUBLANE), axis=0).T
                    w = jnp.where(mask, jnp.exp(logw + jnp.concatenate([run] * (tk // LANE), axis=1)), 0.0)
                    g = w * _dot(dobs[h], vt, "nt")
                    before, gtot, _ = _lane_sums(g, tri_lt, False)
                    sig = jnp.exp(z - sp)
                    dz = jnp.where(mask, g * (1.0 - sig) - (before + gsums[h]) * sig, 0.0) * scale
                    dzb = dz.astype(BF16)
                    dk_acc[pl.ds(k0, tk), hs] += _dot(dzb, qbs[h], "tn")
                    dv_acc[pl.ds(k0, tk), hs] += _dot(w.astype(BF16), dobs[h], "tn")
                    dq_acc[:, hs] += _dot(dzb, kt)
                    out.append(gsums[h] + gtot)
                return tuple(out)

            lax.fori_loop(0, n_strips, strip, zero)
            dq_ref[pl.ds(q0, tq), :] = dq_acc[...].astype(BF16)
            return 0

        lax.fori_loop(0, t // tq, qblock, 0)
        dk_ref[...] = dk_acc[...].astype(BF16)
        dv_ref[...] = dv_acc[...].astype(BF16)

    def col_spec(base):
        return pl.BlockSpec((t, wd), lambda h, base=base: (0, base + h))

    ng = n_heads // hp
    return pl.pallas_call(
        body, name="attn_bwd", grid=(ng,),
        in_specs=[col_spec(0), col_spec(ng), col_spec(2 * ng), col_spec(0),
                  pl.BlockSpec((hp,) + runs.shape[1:], lambda h: (h, 0, 0, 0, 0))],
        out_specs=[col_spec(0), col_spec(0), col_spec(0)],
        out_shape=[jax.ShapeDtypeStruct((t, n_heads * HEAD_DIM), BF16)] * 3,
        scratch_shapes=[pltpu.VMEM((t, wd), BF16)] * 4 + [pltpu.VMEM((t, wd), F32)] * 2 + [pltpu.VMEM((tq, wd), F32)],
        compiler_params=_params(1),
    )(proj, proj, proj, dy, runs)


def _glu_halo(vc, gc, vp, gp, ubuf, i, tt, halo):
    uprev = vp[pl.ds(tt - halo, halo), :] * _sigmoid(gp[pl.ds(tt - halo, halo), :])
    ubuf[pl.ds(0, halo), :] = jnp.where(i > 0, uprev, 0.0)
    ubuf[pl.ds(halo, tt), :] = vc[...] * _sigmoid(gc[...])


def conv_fwd(proj, col0, cc, w, b, lng, lnb):
    t = proj.shape[0]
    tt = _tile(t, ROW_T)
    vi, gi = col0 // cc, col0 // cc + 1
    off = CONV_HALO - (DW_LEN - 1)

    def body(vc, gc, vp, gp, w_ref, b_ref, lg_ref, lb_ref, c_ref, y_ref, ubuf):
        i = pl.program_id(0)
        _glu_halo(vc, gc, vp, gp, ubuf, i, tt, CONV_HALO)
        for ch in range(cc // LANE):
            sl = pl.ds(ch * LANE, LANE)
            acc = jnp.zeros((tt, LANE), F32) + b_ref[:, sl]
            for tap in range(DW_LEN):
                acc = acc + w_ref[pl.ds(tap, 1), sl] * ubuf[pl.ds(off + tap, tt), sl]
            c_ref[:, sl] = acc
        c = c_ref[...]
        xc = c - jnp.mean(c, axis=-1, keepdims=True)
        ln = xc * lax.rsqrt(jnp.mean(xc * xc, axis=-1, keepdims=True) + EPS) * lg_ref[...] + lb_ref[...]
        y_ref[...] = ln * _sigmoid(ln)

    cur = lambda c: pl.BlockSpec((tt, cc), lambda i, c=c: (i, c))
    prev = lambda c: pl.BlockSpec((tt, cc), lambda i, c=c: (jnp.maximum(i - 1, 0), c))
    return pl.pallas_call(
        body, name="conv_fwd", grid=(t // tt,),
        in_specs=[cur(vi), cur(gi), prev(vi), prev(gi), pl.BlockSpec((DW_LEN, cc), lambda i: (0, 0)),
                  _vec_spec(cc), _vec_spec(cc), _vec_spec(cc)],
        out_specs=[_row_spec(tt, cc), _row_spec(tt, cc)],
        out_shape=[jax.ShapeDtypeStruct((t, cc), F32)] * 2,
        scratch_shapes=[pltpu.VMEM((CONV_HALO + tt, cc), F32)],
        compiler_params=_params(1),
    )(proj, proj, proj, proj, w, b, lng, lnb)


def conv_bwd(proj, col0, cc, dc, w):
    t = proj.shape[0]
    tt = _tile(t, ROW_T)
    nt = t // tt
    vi, gi = col0 // cc, col0 // cc + 1

    def body(vc, gc, dcc, dcn, w_ref, dvg_ref, dw_ref, db_ref, dbuf):
        i = pl.program_id(0)

        @pl.when(i == 0)
        def _():
            dw_ref[...] = jnp.zeros_like(dw_ref)
            db_ref[...] = jnp.zeros_like(db_ref)

        dbuf[pl.ds(0, tt), :] = dcc[...]
        dbuf[pl.ds(tt, CONV_HALO), :] = jnp.where(i < nt - 1, dcn[pl.ds(0, CONV_HALO), :], 0.0)
        db_ref[...] += jnp.sum(dcc[...], axis=0, keepdims=True)
        for ch in range(cc // LANE):
            sl = pl.ds(ch * LANE, LANE)
            s = _sigmoid(gc[:, sl])
            val = vc[:, sl]
            u = val * s
            du = jnp.zeros((tt, LANE), F32)
            for tap in range(DW_LEN):
                ahead = dbuf[pl.ds(DW_LEN - 1 - tap, tt), sl]
                du = du + w_ref[pl.ds(tap, 1), sl] * ahead
                dw_ref[pl.ds(tap, 1), sl] += jnp.sum(ahead * u, axis=0, keepdims=True)
            dvg_ref[:, sl] = (du * s).astype(BF16)
            dvg_ref[:, pl.ds(cc + ch * LANE, LANE)] = (du * val * s * (1.0 - s)).astype(BF16)

    cur = lambda c: pl.BlockSpec((tt, cc), lambda i, c=c: (i, c))
    return pl.pallas_call(
        body, name="conv_bwd", grid=(nt,),
        in_specs=[cur(vi), cur(gi), _row_spec(tt, cc),
                  pl.BlockSpec((tt, cc), lambda i: (jnp.minimum(i + 1, nt - 1), 0)),
                  pl.BlockSpec((DW_LEN, cc), lambda i: (0, 0))],
        out_specs=[_row_spec(tt, 2 * cc), pl.BlockSpec((DW_LEN, cc), lambda i: (0, 0)), _vec_spec(cc)],
        out_shape=[jax.ShapeDtypeStruct((t, 2 * cc), BF16), jax.ShapeDtypeStruct((DW_LEN, cc), F32),
                   jax.ShapeDtypeStruct((1, cc), F32)],
        scratch_shapes=[pltpu.VMEM((tt + CONV_HALO, cc), F32)],
        compiler_params=_params(1),
    )(proj, proj, dc, dc, w)


def _lru_gates(xbuf, cw_ref, cb_ref, wa_ref, ba_ref, wi_ref, bi_ref, lam_ref, tt, wl):
    bd = wl // LRU_BLOCKS
    xr = jnp.zeros((tt, wl), F32) + cb_ref[...]
    for tap in range(LRU_LEN):
        xr = xr + cw_ref[pl.ds(tap, 1), :] * xbuf[pl.ds(LRU_HALO - (LRU_LEN - 1) + tap, tt), :]
    xb = xr.astype(BF16)
    ga = jnp.concatenate([_dot(xb[:, n * bd:(n + 1) * bd], wa_ref[n]) for n in range(LRU_BLOCKS)], axis=1) + ba_ref[...]
    gi = jnp.concatenate([_dot(xb[:, n * bd:(n + 1) * bd], wi_ref[n]) for n in range(LRU_BLOCKS)], axis=1) + bi_ref[...]
    r = _sigmoid(ga)
    ig = _sigmoid(gi)
    spl = _softplus(-lam_ref[...])
    log_a = -LRU_C * r * spl
    a = jnp.exp(log_a)
    m = jnp.sqrt(_neg_expm1(2.0 * log_a))
    return xr, xb, r, ig, spl, a, m


def _group_scan(a8, b8, reverse):
    rid = lax.broadcasted_iota(jnp.int32, a8.shape, 0)
    aa, bb = a8, b8
    for dist in (1, 2, 4):
        shift = SUBLANE - dist if reverse else dist
        a_sh = pltpu.roll(aa, shift, 0)
        b_sh = pltpu.roll(bb, shift, 0)
        valid = (rid < SUBLANE - dist) if reverse else (rid >= dist)
        bb = jnp.where(valid, aa * b_sh + bb, bb)
        aa = jnp.where(valid, aa * a_sh, aa)
    return aa, bb


def _pick_row(x8, r):
    rid = lax.broadcasted_iota(jnp.int32, x8.shape, 0)
    return jnp.sum(jnp.where(rid == r, x8, 0.0), axis=0, keepdims=True)


def lru_fwd(proj, col0, wl, cw, cb, wa, ba, wi, bi, lam):
    t = proj.shape[0]
    tt = _tile(t, ROW_T)
    xi, yi = col0 // wl, col0 // wl + 1

    def body(xc, xp, ry, cw_ref, cb_ref, wa_ref, ba_ref, wi_ref, bi_ref, lam_ref, hs_ref, y_ref,
             xbuf, a_s, b_s, hcar):
        i = pl.program_id(0)

        @pl.when(i == 0)
        def _():
            hcar[...] = jnp.zeros_like(hcar)

        xbuf[pl.ds(0, LRU_HALO), :] = jnp.where(i > 0, xp[pl.ds(tt - LRU_HALO, LRU_HALO), :], 0.0)
        xbuf[pl.ds(LRU_HALO, tt), :] = xc[...]
        xr, _, _, ig, _, a, m = _lru_gates(xbuf, cw_ref, cb_ref, wa_ref, ba_ref, wi_ref, bi_ref, lam_ref, tt, wl)
        a_s[...] = a
        b_s[...] = m * ig * xr

        def group(gidx, h):
            r0 = pl.multiple_of(gidx * SUBLANE, SUBLANE)
            aa, bb = _group_scan(a_s[pl.ds(r0, SUBLANE), :], b_s[pl.ds(r0, SUBLANE), :], False)
            h8 = aa * h + bb
            hs_ref[pl.ds(r0, SUBLANE), :] = h8
            return _pick_row(h8, SUBLANE - 1)

        hcar[...] = lax.fori_loop(0, tt // SUBLANE, group, hcar[...])
        gel, _ = _gelu_and_grad(ry[...])
        y_ref[...] = hs_ref[...] * gel

    cur = lambda c: pl.BlockSpec((tt, wl), lambda i, c=c: (i, c))
    full = lambda shape: pl.BlockSpec(shape, lambda i: (0,) * len(shape))
    return pl.pallas_call(
        body, name="lru_fwd", grid=(t // tt,),
        in_specs=[cur(xi), pl.BlockSpec((tt, wl), lambda i: (jnp.maximum(i - 1, 0), xi)), cur(yi),
                  full((LRU_LEN, wl)), _vec_spec(wl), full(wa.shape), _vec_spec(wl), full(wi.shape), _vec_spec(wl),
                  _vec_spec(wl)],
        out_specs=[_row_spec(tt, wl), _row_spec(tt, wl)],
        out_shape=[jax.ShapeDtypeStruct((t, wl), F32)] * 2,
        scratch_shapes=[pltpu.VMEM((LRU_HALO + tt, wl), F32), pltpu.VMEM((tt, wl), F32), pltpu.VMEM((tt, wl), F32),
                        pltpu.VMEM((1, wl), F32)],
        compiler_params=_params(1),
    )(proj, proj, proj, cw, cb, wa, ba, wi, bi, lam)


def lru_bwd(proj, col0, wl, hs, dy, cw, cb, wa, ba, wi, bi, lam):
    t = proj.shape[0]
    tt = _tile(t, ROW_T)
    nt = t // tt
    xi, yi = col0 // wl, col0 // wl + 1
    bd = wl // LRU_BLOCKS

    def body(xc, xp, ry, hc, hp, dy_ref, cw_ref, cb_ref, wa_ref, ba_ref, wi_ref, bi_ref, lam_ref,
             dxy_ref, dcw_ref, dcb_ref, dwa_ref, dba_ref, dwi_ref, dbi_ref, dlam_ref,
             xbuf, hbuf, abuf, e_s, dh_s, dxbuf, dhcar):
        i = pl.program_id(0)
        first = i == 0

        @pl.when(first)
        def _():
            for r in (dcw_ref, dcb_ref, dwa_ref, dba_ref, dwi_ref, dbi_ref, dlam_ref, dhcar):
                r[...] = jnp.zeros_like(r)
            abuf[pl.ds(tt, LRU_HALO), :] = jnp.zeros((LRU_HALO, wl), F32)
            dxbuf[pl.ds(tt, LRU_HALO), :] = jnp.zeros((LRU_HALO, wl), F32)

        has_prev = i < nt - 1
        xbuf[pl.ds(0, LRU_HALO), :] = jnp.where(has_prev, xp[pl.ds(tt - LRU_HALO, LRU_HALO), :], 0.0)
        xbuf[pl.ds(LRU_HALO, tt), :] = xc[...]
        hbuf[pl.ds(0, LRU_HALO), :] = jnp.where(has_prev, hp[pl.ds(tt - LRU_HALO, LRU_HALO), :], 0.0)
        hbuf[pl.ds(LRU_HALO, tt), :] = hc[...]
        xr, xb, r, ig, spl, a, m = _lru_gates(xbuf, cw_ref, cb_ref, wa_ref, ba_ref, wi_ref, bi_ref, lam_ref, tt, wl)
        gel, dgel = _gelu_and_grad(ry[...])
        dyv = dy_ref[...]
        e_s[...] = dyv * gel
        dxy_ref[:, pl.ds(wl, wl)] = (dyv * hc[...] * dgel).astype(BF16)
        abuf[pl.ds(0, tt), :] = a
        a_next = abuf[pl.ds(1, tt), :]
        dh_s[...] = a_next

        def group(it, dh_in):
            r0 = pl.multiple_of((tt // SUBLANE - 1 - it) * SUBLANE, SUBLANE)
            aa, bb = _group_scan(dh_s[pl.ds(r0, SUBLANE), :], e_s[pl.ds(r0, SUBLANE), :], True)
            dh8 = aa * dh_in + bb
            dh_s[pl.ds(r0, SUBLANE), :] = dh8
            return _pick_row(dh8, 0)

        dhcar[...] = lax.fori_loop(0, tt // SUBLANE, group, dhcar[...])
        abuf[pl.ds(tt, LRU_HALO), :] = a[0:LRU_HALO, :]
        dh = dh_s[...]
        h_m1 = hbuf[pl.ds(LRU_HALO - 1, tt), :]
        dlog_a = dh * h_m1 * a - dh * ig * xr * (a * a / m)
        dig = dh * m * xr
        dxr = dh * m * ig
        dga = dlog_a * (-LRU_C) * spl * r * (1.0 - r)
        dgi = dig * ig * (1.0 - ig)
        dlam_ref[...] += jnp.sum(dlog_a * r, axis=0, keepdims=True) * (LRU_C * _sigmoid(-lam_ref[...]))
        dba_ref[...] += jnp.sum(dga, axis=0, keepdims=True)
        dbi_ref[...] += jnp.sum(dgi, axis=0, keepdims=True)
        dgab = dga.astype(BF16)
        dgib = dgi.astype(BF16)
        back = []
        for n in range(LRU_BLOCKS):
            sl = slice(n * bd, (n + 1) * bd)
            dwa_ref[n] += _dot(xb[:, sl], dgab[:, sl], "tn")
            dwi_ref[n] += _dot(xb[:, sl], dgib[:, sl], "tn")
            back.append(_dot(dgab[:, sl], wa_ref[n], "nt") + _dot(dgib[:, sl], wi_ref[n], "nt"))
        dxr = dxr + jnp.concatenate(back, axis=1)
        dcb_ref[...] += jnp.sum(dxr, axis=0, keepdims=True)
        dxbuf[pl.ds(0, tt), :] = dxr
        drx = jnp.zeros((tt, wl), F32)
        for tap in range(LRU_LEN):
            drx = drx + cw_ref[pl.ds(tap, 1), :] * dxbuf[pl.ds(LRU_LEN - 1 - tap, tt), :]
            dcw_ref[pl.ds(tap, 1), :] += jnp.sum(
                dxr * xbuf[pl.ds(LRU_HALO - (LRU_LEN - 1) + tap, tt), :], axis=0, keepdims=True)
        dxbuf[pl.ds(tt, LRU_HALO), :] = dxr[0:LRU_HALO, :]
        dxy_ref[:, pl.ds(0, wl)] = drx.astype(BF16)

    rev = lambda c: pl.BlockSpec((tt, wl), lambda i, c=c: (nt - 1 - i, c))
    rev_prev = lambda c: pl.BlockSpec((tt, wl), lambda i, c=c: (jnp.maximum(nt - 2 - i, 0), c))
    full = lambda shape: pl.BlockSpec(shape, lambda i: (0,) * len(shape))
    vec = _vec_spec(wl)
    return pl.pallas_call(
        body, name="lru_bwd", grid=(nt,),
        in_specs=[rev(xi), rev_prev(xi), rev(yi), rev(0), rev_prev(0), rev(0),
                  full((LRU_LEN, wl)), vec, full(wa.shape), vec, full(wi.shape), vec, vec],
        out_specs=[pl.BlockSpec((tt, 2 * wl), lambda i: (nt - 1 - i, 0)), full((LRU_LEN, wl)), vec,
                   full(wa.shape), vec, full(wi.shape), vec, vec],
        out_shape=[jax.ShapeDtypeStruct((t, 2 * wl), BF16), jax.ShapeDtypeStruct((LRU_LEN, wl), F32),
                   jax.ShapeDtypeStruct((1, wl), F32), jax.ShapeDtypeStruct(wa.shape, F32),
                   jax.ShapeDtypeStruct((1, wl), F32), jax.ShapeDtypeStruct(wi.shape, F32),
                   jax.ShapeDtypeStruct((1, wl), F32), jax.ShapeDtypeStruct((1, wl), F32)],
        scratch_shapes=[pltpu.VMEM((LRU_HALO + tt, wl), F32), pltpu.VMEM((LRU_HALO + tt, wl), F32),
                        pltpu.VMEM((tt + LRU_HALO, wl), F32), pltpu.VMEM((tt, wl), F32), pltpu.VMEM((tt, wl), F32),
                        pltpu.VMEM((tt + LRU_HALO, wl), F32), pltpu.VMEM((1, wl), F32)],
        compiler_params=_params(1),
    )(proj, proj, proj, hs, hs, dy, cw, cb, wa, ba, wi, bi, lam)


def _adamw(w, g, m, v):
    m = ADAM_B1 * m + (1.0 - ADAM_B1) * g
    v = ADAM_B2 * v + (1.0 - ADAM_B2) * (g * g)
    m_hat = m / (1.0 - ADAM_B1 ** ADAM_STEP)
    v_hat = v / (1.0 - ADAM_B2 ** ADAM_STEP)
    delta = -ADAM_LR * (m_hat / (jnp.sqrt(v_hat) + ADAM_EPS) + ADAM_WD * w)
    return delta, m, v


def adam_big(name, w, m, v, parts, chip):
    n_layers, rows, cols = w.shape
    tr = _tile(rows, 128 if cols > 1024 else 256)
    nrt = rows // tr

    def body(chip_ref, *refs):
        w_ref, m_ref, v_ref = refs[:3]
        part_refs = refs[3:3 + 4 * n_layers]
        g_ref, d_ref, mo_ref, vo_ref = refs[3 + 4 * n_layers:]
        layer = pl.program_id(0)
        for l in range(n_layers):
            @pl.when(layer == l)
            def _(l=l):
                g = part_refs[4 * l][...].astype(F32)
                for p in range(1, 4):
                    g = g + part_refs[4 * l + p][...].astype(F32)
                delta, mn, vn = _adamw(w_ref[...], g, m_ref[...], v_ref[...])
                g_ref[...] = g
                d_ref[...] = delta
                mo_ref[...] = mn
                vo_ref[...] = vn

    wspec = pl.BlockSpec((None, tr, cols), lambda l, i, chip_ref: (l, i, 0))
    operands, in_specs = [w, m, v], [wspec, wspec, wspec]
    for l in range(n_layers):
        mine, recv = parts[l]
        operands.append(mine)
        in_specs.append(pl.BlockSpec(
            (None, tr, cols), lambda ll, i, chip_ref, l=l: (chip_ref[0], jnp.where(ll == l, i, 0), 0)))
        for p in range(3):
            operands.append(recv)
            in_specs.append(pl.BlockSpec(
                (None, tr, cols), lambda ll, i, chip_ref, l=l, p=p: (p, jnp.where(ll == l, i, 0), 0)))
    return pl.pallas_call(
        body, name=name,
        grid_spec=pltpu.PrefetchScalarGridSpec(
            num_scalar_prefetch=1, grid=(n_layers, nrt), in_specs=in_specs, out_specs=[wspec] * 4),
        out_shape=[jax.ShapeDtypeStruct(w.shape, F32)] * 4, compiler_params=_params(2),
    )(chip, *operands)


def adam_small(w, m, v, g):
    rows = w.shape[0]
    tr = _tile(rows, PACK_ROWS)

    def body(w_ref, m_ref, v_ref, g_ref, d_ref, mo_ref, vo_ref):
        delta, mn, vn = _adamw(w_ref[...], g_ref[...], m_ref[...], v_ref[...])
        d_ref[...] = delta
        mo_ref[...] = mn
        vo_ref[...] = vn

    spec = _row_spec(tr, LANE)
    return pl.pallas_call(
        body, name="adam_small", grid=(rows // tr,), in_specs=[spec] * 4, out_specs=[spec] * 3,
        out_shape=[jax.ShapeDtypeStruct(w.shape, F32)] * 3, compiler_params=_params(1),
    )(w, m, v, g)


def sum_parts(parts):
    _, rows, _ = parts.shape
    tr = _tile(rows, PACK_ROWS)

    def body(p_ref, o_ref):
        acc = p_ref[0]
        for k in range(1, N_DEV):
            acc = acc + p_ref[k]
        o_ref[...] = acc

    return pl.pallas_call(
        body, name="sum_parts", grid=(rows // tr,),
        in_specs=[pl.BlockSpec((N_DEV, tr, LANE), lambda i: (0, i, 0))], out_specs=_row_spec(tr, LANE),
        out_shape=jax.ShapeDtypeStruct((rows, LANE), F32), compiler_params=_params(1),
    )(parts)


def place_own(x, me, dtype):
    rows, cols = x.shape
    tr = _tile(rows, 256)

    def body(me_ref, x_ref, o_ref):
        o_ref[...] = x_ref[...].astype(dtype)

    return pl.pallas_call(
        body, name="place_own",
        grid_spec=pltpu.PrefetchScalarGridSpec(
            num_scalar_prefetch=1, grid=(rows // tr,),
            in_specs=[pl.BlockSpec((tr, cols), lambda i, me_ref: (i, 0))],
            out_specs=pl.BlockSpec((None, tr, cols), lambda i, me_ref: (me_ref[0], i, 0))),
        out_shape=jax.ShapeDtypeStruct((N_DEV, rows, cols), dtype), compiler_params=_params(1),
    )(me, x)


_HBM = pl.BlockSpec(memory_space=pltpu.HBM)


def _place():
    return lax.axis_index("x"), lax.axis_index("y"), lax.axis_index("c")


def _other_chips(x, y):
    return [(1 - x, y), (x, 1 - y), (1 - x, 1 - y)]


def all_gather(name, shard, me, dtype=None):
    def body(buf_ref, out_ref, send_sems, recv_sems):
        del buf_ref
        x, y, c = _place()
        mine, sibling = (x, y, c), (x, y, 1 - c)
        chips = _other_chips(x, y)

        def copy(k, block, to):
            slot = out_ref.at[4 * block[0] + 2 * block[1] + block[2]]
            return pltpu.make_async_remote_copy(
                src_ref=slot, dst_ref=slot, send_sem=send_sems.at[k], recv_sem=recv_sems.at[k],
                device_id=to, device_id_type=pl.DeviceIdType.MESH)

        first = [copy(0, mine, sibling)] + [copy(1 + j, mine, (*chip, c)) for j, chip in enumerate(chips)]
        for cp in first:
            cp.start()
        passed = [copy(4 + j, (*chip, c), sibling) for j, chip in enumerate(chips)]
        for j, chip in enumerate(chips):
            copy(1 + j, (*chip, c), mine).wait_recv()
            passed[j].start()
        copy(0, sibling, mine).wait_recv()
        for j, chip in enumerate(chips):
            copy(4 + j, (*chip, 1 - c), mine).wait_recv()
        for cp in first + passed:
            cp.wait_send()

    buf = place_own(shard, me, dtype or shard.dtype)
    return pl.pallas_call(
        body, name=name, out_shape=jax.ShapeDtypeStruct(buf.shape, buf.dtype),
        in_specs=[_HBM], out_specs=_HBM, input_output_aliases={0: 0},
        scratch_shapes=[pltpu.SemaphoreType.DMA((7,)), pltpu.SemaphoreType.DMA((7,))],
    )(buf)


def _own_block_copies(src_refs, dst_refs, send_sems, recv_sems, arrivals):
    x, y, c = _place()
    peers = [(x, y, 1 - c)] + [(*chip, c) for chip in _other_chips(x, y)]
    copies = []
    for b, (src, dst) in enumerate(zip(src_refs, dst_refs)):
        for k, peer in enumerate(peers):
            def copy(landing, b=b, k=k, peer=peer, src=src, dst=dst):
                return pltpu.make_async_remote_copy(
                    src_ref=src.at[4 * x + 2 * y + c], dst_ref=dst.at[landing],
                    send_sem=send_sems.at[4 * b + k], recv_sem=recv_sems.at[4 * b + k],
                    device_id=peer, device_id_type=pl.DeviceIdType.MESH)
            copies.append((copy(4 * x + 2 * y + c), copy(4 * peer[0] + 2 * peer[1] + peer[2]) if arrivals else None))
    return copies


def gather_start(name, bufs, after):
    n = len(bufs)

    def body(*refs):
        send_sems, recv_sems = refs[n + 1], refs[n + 2]
        thru = refs[n + 3:2 * n + 3]
        for send, _ in _own_block_copies(thru, thru, send_sems, recv_sems, False):
            send.start()
        refs[2 * n + 3][...] = jnp.zeros((SUBLANE, LANE), F32)

    return pl.pallas_call(
        body, name=name,
        out_shape=(pltpu.SemaphoreType.DMA((4 * n,)), pltpu.SemaphoreType.DMA((4 * n,)),
                   *[pltpu.HBM(b.shape, b.dtype) for b in bufs], jax.ShapeDtypeStruct((SUBLANE, LANE), F32)),
        in_specs=(*(_HBM,) * n, _ANY), out_specs=(_SEM, _SEM, *(_HBM,) * n, _TOKEN),
        input_output_aliases={b: 2 + b for b in range(n)},
        compiler_params=pltpu.CompilerParams(has_side_effects=_EFFECT),
    )(*[_hbm(b) for b in bufs], after)


def gather_wait(name, state, first, count, after):
    send_sems, recv_sems = state[:2]
    bufs = state[2 + first:2 + first + count]
    n = len(bufs)

    def body(*refs):
        ins = refs[:n]
        send_sems, recv_sems = refs[n], refs[n + 1]
        shift = 4 * first
        for send, arrival in _own_block_copies(
                ins, ins, send_sems.at[pl.ds(shift, 4 * n)], recv_sems.at[pl.ds(shift, 4 * n)], True):
            send.wait_send()
            arrival.wait_recv()

    return pl.pallas_call(
        body, name=name, out_shape=tuple(pltpu.HBM(b.shape, b.dtype) for b in bufs),
        in_specs=(*(_HBM,) * n, _SEM, _SEM, _ANY), out_specs=(_HBM,) * n,
        input_output_aliases={b: b for b in range(n)},
        compiler_params=pltpu.CompilerParams(has_side_effects=_EFFECT),
    )(*bufs, send_sems, recv_sems, after)


def _forward_copies(bufs, send_sems, recv_sems, arrivals):
    x, y, c = _place()
    copies = []
    for b, buf in enumerate(bufs):
        for k, chip in enumerate(_other_chips(x, y)):
            def copy(core, b=b, k=k, chip=chip, buf=buf):
                return pltpu.make_async_remote_copy(
                    src_ref=buf.at[4 * chip[0] + 2 * chip[1] + c], dst_ref=buf.at[4 * chip[0] + 2 * chip[1] + core],
                    send_sem=send_sems.at[3 * b + k], recv_sem=recv_sems.at[3 * b + k],
                    device_id=(x, y, 1 - c), device_id_type=pl.DeviceIdType.MESH)
            copies.append((copy(c), copy(1 - c) if arrivals else None))
    return copies


def gather_finish(name, bufs):
    n = len(bufs)

    def body(*refs):
        copies = _forward_copies(refs[n:2 * n], refs[2 * n], refs[2 * n + 1], True)
        for send, _ in copies:
            send.start()
        for send, arrival in copies:
            send.wait_send()
            arrival.wait_recv()

    return pl.pallas_call(
        body, name=name, out_shape=tuple(jax.ShapeDtypeStruct(b.shape, b.dtype) for b in bufs),
        in_specs=(_HBM,) * n, out_specs=(_HBM,) * n, input_output_aliases={b: b for b in range(n)},
        scratch_shapes=[pltpu.SemaphoreType.DMA((3 * n,)), pltpu.SemaphoreType.DMA((3 * n,))],
    )(*bufs)


def forward_start(name, bufs, carry):
    n = len(bufs)

    def body(*refs):
        send_sems, recv_sems = refs[n + 1], refs[n + 2]
        for send, _ in _forward_copies(refs[:n], send_sems, recv_sems, False):
            send.start()

    out = pl.pallas_call(
        body, name=name,
        out_shape=(pltpu.SemaphoreType.DMA((3 * n,)), pltpu.SemaphoreType.DMA((3 * n,)),
                   *[pltpu.HBM(b.shape, b.dtype) for b in bufs], pltpu.HBM(carry.shape, carry.dtype)),
        in_specs=(_HBM,) * (n + 1), out_specs=(_SEM, _SEM, *(_HBM,) * (n + 1)),
        input_output_aliases={b: 2 + b for b in range(n + 1)},
        compiler_params=pltpu.CompilerParams(has_side_effects=_EFFECT),
    )(*[_hbm(b) for b in bufs], _hbm(carry))
    return out[:-1], out[-1]


def forward_wait(name, state, after):
    send_sems, recv_sems, *bufs = state
    n = len(bufs)

    def body(*refs):
        for send, arrival in _forward_copies(refs[:n], refs[n], refs[n + 1], True):
            send.wait_send()
            arrival.wait_recv()

    return pl.pallas_call(
        body, name=name, out_shape=tuple(pltpu.HBM(b.shape, b.dtype) for b in bufs),
        in_specs=(*(_HBM,) * n, _SEM, _SEM, _ANY), out_specs=(_HBM,) * n,
        input_output_aliases={b: b for b in range(n)},
        compiler_params=pltpu.CompilerParams(has_side_effects=_EFFECT),
    )(*bufs, send_sems, recv_sems, after)


_SEM =pl.BlockSpec(memory_space=pltpu.SEMAPHORE)
_ANY = pl.BlockSpec(memory_space=pl.ANY)
_TOKEN = pl.BlockSpec(memory_space=pltpu.VMEM)
_EFFECT = pltpu.SideEffectType.DATAFLOW_SIDE_EFFECTING


def _hbm(a):
    return pltpu.with_memory_space_constraint(a, pltpu.HBM)


def _chip_copies(p_ref, land_ref, send_sems, recv_sems):
    x, y, c = _place()
    return [pltpu.make_async_remote_copy(
        src_ref=p_ref.at[2 * px + py], dst_ref=land_ref.at[k], send_sem=send_sems.at[k], recv_sem=recv_sems.at[k],
        device_id=(px, py, c), device_id_type=pl.DeviceIdType.MESH) for k, (px, py) in enumerate(_other_chips(x, y))]


def scatter_chips_start(name, p):
    _, rows, cols = p.shape

    def body(p_ref, land_ref, send_sems, recv_sems, p_thru, land_thru, token):
        for cp in _chip_copies(p_ref, land_ref, send_sems, recv_sems):
            cp.start()
        token[...] = jnp.zeros_like(token)

    return pl.pallas_call(
        body, name=name,
        out_shape=(pltpu.SemaphoreType.DMA((3,)), pltpu.SemaphoreType.DMA((3,)), pltpu.HBM(p.shape, p.dtype),
                   pltpu.HBM((3, rows, cols), p.dtype), jax.ShapeDtypeStruct((SUBLANE, LANE), F32)),
        in_specs=(_HBM, _HBM), out_specs=(_SEM, _SEM, _HBM, _HBM, _TOKEN), input_output_aliases={0: 2, 1: 3},
        compiler_params=pltpu.CompilerParams(has_side_effects=_EFFECT),
    )(_hbm(p), _hbm(lax.empty((3, rows, cols), p.dtype)))


def scatter_chips_wait(name, send_sems, recv_sems, p_thru, land_thru, after):
    def body(p_ref, land_ref, send_sems, recv_sems, after_ref, p_out, land_out):
        for cp in _chip_copies(p_ref, land_ref, send_sems, recv_sems):
            cp.wait_send()
            cp.wait_recv()

    return pl.pallas_call(
        body, name=name,
        out_shape=(pltpu.HBM(p_thru.shape, p_thru.dtype), pltpu.HBM(land_thru.shape, land_thru.dtype)),
        in_specs=(_HBM, _HBM, _SEM, _SEM, _ANY), out_specs=(_HBM, _HBM), input_output_aliases={0: 0, 1: 1},
        compiler_params=pltpu.CompilerParams(has_side_effects=_EFFECT),
    )(p_thru, land_thru, send_sems, recv_sems, after)


def _pair_copies(g_ref, land_ref, send_sems, recv_sems):
    x, y, c = _place()
    return [pltpu.make_async_remote_copy(
        src_ref=g_ref.at[k], dst_ref=land_ref.at[k], send_sem=send_sems.at[k], recv_sem=recv_sems.at[k],
        device_id=(x, y, 1 - c), device_id_type=pl.DeviceIdType.MESH) for k in range(N_DEV // 2)]


def pair_start(name, g, after, carry=None):
    n = g.shape[0]

    def body(g_ref, land_ref, after_ref, *rest):
        send_sems, recv_sems = rest[-5 if carry is None else -6:][:2]
        for cp in _pair_copies(g_ref, land_ref, send_sems, recv_sems):
            cp.start()
        token = rest[-1 if carry is None else -2]
        token[...] = jnp.zeros_like(token)

    extra = () if carry is None else (carry,)
    out = pl.pallas_call(
        body, name=name,
        out_shape=(pltpu.SemaphoreType.DMA((n,)), pltpu.SemaphoreType.DMA((n,)), pltpu.HBM(g.shape, g.dtype),
                   pltpu.HBM(g.shape, g.dtype), jax.ShapeDtypeStruct((SUBLANE, LANE), F32),
                   *[pltpu.HBM(c.shape, c.dtype) for c in extra]),
        in_specs=(_HBM, _HBM, _ANY, *(_HBM,) * len(extra)), out_specs=(_SEM, _SEM, _HBM, _HBM, _TOKEN, *(_HBM,) * len(extra)),
        input_output_aliases={0: 2, 1: 3, **({3: 5} if extra else {})},
        compiler_params=pltpu.CompilerParams(has_side_effects=_EFFECT),
    )(_hbm(g), _hbm(lax.empty(g.shape, g.dtype)), after, *[_hbm(c) for c in extra])
    return out if carry is None else (out[:5], out[5])


def pair_wait(name, state, after):
    send_sems, recv_sems, g_thru, land_thru, _ = state

    def body(g_ref, land_ref, send_sems, recv_sems, after_ref, g_out, land_out):
        for cp in _pair_copies(g_ref, land_ref, send_sems, recv_sems):
            cp.wait_send()
            cp.wait_recv()

    return pl.pallas_call(
        body, name=name,
        out_shape=(pltpu.HBM(g_thru.shape, g_thru.dtype), pltpu.HBM(land_thru.shape, land_thru.dtype)),
        in_specs=(_HBM, _HBM, _SEM, _SEM, _ANY), out_specs=(_HBM, _HBM), input_output_aliases={0: 0, 1: 1},
        compiler_params=pltpu.CompilerParams(has_side_effects=_EFFECT),
    )(g_thru, land_thru, send_sems, recv_sems, after)[1]


def reduce_scatter_wait(tag, state, after):
    send_sems, recv_sems, p_thru, land_thru, _ = state
    return scatter_chips_wait("rs_wait_" + tag, send_sems, recv_sems, p_thru, land_thru, after)


_SMALL = ("g_pre_mix", "g_post_mix", "g_pre_ffn", "g_post_ffn", "g_attn_grp", "g_conv_grp", "g_lru_grp",
          "dw_conv_w", "dw_conv_b", "conv_ln_g", "conv_ln_b", "lru_conv_w", "lru_conv_b",
          "lru_w_a", "lru_b_a", "lru_w_i", "lru_b_i", "lru_lambda")
_COL_SHARDED_SMALL = ("dw_conv_w", "lru_conv_w")
_BIG = ("w_in", "w_out", "w_gate", "w_up", "w_down")
_TRANSPOSED = ("w_gate", "w_up")
_ALL = ("w_in", "w_out", "g_pre_mix", "g_post_mix", "g_pre_ffn", "g_post_ffn", "g_attn_grp", "g_conv_grp", "g_lru_grp",
        "dw_conv_w", "dw_conv_b", "conv_ln_g", "conv_ln_b", "lru_conv_w", "lru_conv_b", "lru_w_a", "lru_b_a",
        "lru_w_i", "lru_b_i", "lru_lambda", "w_gate", "w_up", "w_down")


def _pack(arrays):
    flat = jnp.concatenate([a.reshape(-1) for a in arrays])
    pad = (-flat.shape[0]) % (PACK_ROWS * LANE)
    return jnp.pad(flat, (0, pad)).reshape(-1, LANE)


def _unpack(packed, shapes):
    flat = packed.reshape(-1)
    out, pos = [], 0
    for s in shapes:
        n = math.prod(s)
        out.append(flat[pos:pos + n].reshape(s))
        pos += n
    return out


def kernel(x, w_in, w_out, g_pre_mix, g_post_mix, g_pre_ffn, g_post_ffn, g_attn_grp, g_conv_grp, g_lru_grp, dw_conv_w, dw_conv_b, conv_ln_g, conv_ln_b, lru_conv_w, lru_conv_b, lru_w_a, lru_b_a, lru_w_i, lru_b_i, lru_lambda, w_gate, w_up, w_down, loss_target, m_w_in, m_w_out, m_g_pre_mix, m_g_post_mix, m_g_pre_ffn, m_g_post_ffn, m_g_attn_grp, m_g_conv_grp, m_g_lru_grp, m_dw_conv_w, m_dw_conv_b, m_conv_ln_g, m_conv_ln_b, m_lru_conv_w, m_lru_conv_b, m_lru_w_a, m_lru_b_a, m_lru_w_i, m_lru_b_i, m_lru_lambda, m_w_gate, m_w_up, m_w_down, v_w_in, v_w_out, v_g_pre_mix, v_g_post_mix, v_g_pre_ffn, v_g_post_ffn, v_g_attn_grp, v_g_conv_grp, v_g_lru_grp, v_dw_conv_w, v_dw_conv_b, v_conv_ln_g, v_conv_ln_b, v_lru_conv_w, v_lru_conv_b, v_lru_w_a, v_lru_b_a, v_lru_w_i, v_lru_b_i, v_lru_lambda, v_w_gate, v_w_up, v_w_down):
    env = dict(locals())
    wts = {n: env[n] for n in _ALL}
    mom = {n: env["m_" + n] for n in _ALL}
    var = {n: env["v_" + n] for n in _ALL}
    for group in (wts, mom, var):
        for n in _TRANSPOSED:
            group[n] = jnp.swapaxes(group[n], 1, 2)

    depth = w_in.shape[0]
    h = x[0]
    target = loss_target[0]
    t, d = h.shape
    attn_w = d // 2
    n_heads = attn_w // HEAD_DIM
    cc = d // 4
    wl = d // 4
    conv_col, lru_col = 3 * attn_w, 3 * attn_w + 2 * cc
    me = 4 * lax.axis_index("x") + 2 * lax.axis_index("y") + lax.axis_index("c")
    me_s = me.astype(jnp.int32).reshape(1)
    chip_s = (2 * lax.axis_index("x") + lax.axis_index("y")).astype(jnp.int32).reshape(1)
    core_s = lax.axis_index("c").astype(jnp.int32).reshape(1)

    n_taps = DW_LEN + LRU_LEN
    taps = jnp.concatenate([dw_conv_w, lru_conv_w], axis=1).reshape(depth * n_taps, cc // N_DEV)
    taps = all_gather("ag_taps", taps, me_s)
    taps = jnp.moveaxis(taps.reshape(N_DEV, depth, n_taps, cc // N_DEV), 0, 2).reshape(depth, n_taps, cc)
    dw_full, lcw_full = taps[:, :DW_LEN], taps[:, DW_LEN:]

    def vec(a, l):
        return a[l].reshape(1, -1)

    ag_state, started = [], taps
    for l in range(depth):
        ag_state.append(gather_start(f"ag_start_{l}", [place_own(wts[n][l], me_s, BF16) for n in _BIG], started))
        started = ag_state[l][-1]
    started = started[0:1, 0:1]

    saved = []
    u1 = rms_pre(h, vec(g_pre_mix, 0) + started)
    loss_sum = dh = dbr = None
    groups = ((0, 1), (1, 1), (2, 2), (4, 1))

    def issue(l, g, carry):
        first, count = groups[g]
        landed = gather_wait(f"ag_wait_{l}_{g}", ag_state[l], first, count, carry)
        return forward_start(f"ag_fwd_start_{l}_{g}", landed, carry)

    def collect(l, g, state, behind, wg):
        first, count = groups[g]
        wg.update(zip(_BIG[first:first + count], forward_wait(f"ag_fwd_wait_{l}_{g}", state, behind)))

    pending, u1 = issue(0, 0, u1)
    for l in range(depth):
        wg = {}
        collect(l, 0, pending, u1, wg)
        pending, u1 = issue(l, 1, u1)
        wa_b, wi_b = lru_w_a[l].astype(BF16), lru_w_i[l].astype(BF16)
        proj = mm_proj(u1, wg["w_in"])
        y_attn, att_runs = attn_fwd(proj, n_heads)
        cpre, y_conv = conv_fwd(proj, conv_col, cc, dw_full[l], vec(dw_conv_b, l), vec(conv_ln_g, l), vec(conv_ln_b, l))
        hs, y_lru = lru_fwd(proj, lru_col, wl, lcw_full[l], vec(lru_conv_b, l), wa_b, vec(lru_b_a, l), wi_b,
                            vec(lru_b_i, l), vec(lru_lambda, l))
        mixed = mix_fwd(y_attn, y_conv, y_lru, vec(g_attn_grp, l), vec(g_conv_grp, l), vec(g_lru_grp, l))
        collect(l, 1, pending, mixed, wg)
        pending, mixed = issue(l, 2, mixed)
        wg["w_out"] = wg["w_out"].reshape(attn_w + cc + wl, d)
        o = mm_plain("mm_out", mixed, wg["w_out"], "nn", F32)
        h2, u2 = res_norm(h, o, vec(g_post_mix, l), vec(g_pre_ffn, l))
        collect(l, 2, pending, u2, wg)
        pending, u2 = issue(l, 3, u2)
        f_gate, f_up, f = ffn_up(u2, wg["w_gate"], wg["w_up"])
        collect(l, 3, pending, f, wg)
        if l + 1 < depth:
            pending, f = issue(l + 1, 0, f)
        dn = mm_down(f, wg["w_down"])
        saved.append(dict(wg=wg, wa_b=wa_b, wi_b=wi_b, h=h, u1=u1, proj=proj, y_attn=y_attn, att_runs=att_runs, cpre=cpre, y_conv=y_conv,
                          hs=hs, y_lru=y_lru, mixed=mixed, o=o, h2=h2, u2=u2, f_gate=f_gate, f_up=f_up, f=f, dn=dn))
        if l + 1 < depth:
            h, u1 = res_norm(h2, dn, vec(g_post_ffn, l), vec(g_pre_mix, l + 1))
        else:
            loss_sum, dh, dbr, dg_post_ffn = final_loss(h2, dn, vec(g_post_ffn, l), target)

    loss = lax.psum(0.5 * loss_sum[0, 0] / d, MESH_AXES)

    small = {n: [None] * depth for n in _SMALL}
    rs_state = {n: [None] * depth for n in _BIG}
    after = dbr
    for l in reversed(range(depth)):
        s = saved[l]
        wg = s["wg"]
        small["g_post_ffn"][l] = dg_post_ffn
        dgt, dup = ffn_bwd(dbr, wg["w_down"], s["f_gate"], s["f_up"])
        ffn_grads = (("w_down", "down", s["f"], dbr), ("w_gate", "gate", dgt, s["u2"]), ("w_up", "up", dup, s["u2"]))
        pairs = {}
        for n, tag, a, g in ffn_grads:
            theirs = mm_dw_half("mm_dw_" + tag, "rows", a, g, core_s, True)
            pairs[n] = pair_start(f"pair_start_{tag}_{l}", theirs, after)
            after = pairs[n][-1]
        for n, tag, a, g in ffn_grads:
            recv = pair_wait(f"pair_wait_{tag}_{l}", pairs[n], after)
            after = mm_dw_half("mm_dw_" + tag, "rows", a, g, core_s, False, add=recv)
            rs_state[n][l] = scatter_chips_start(f"rs_start_{tag}_{l}", after)
        du2 = mm_dx_ffn(dgt, wg["w_gate"], dup, wg["w_up"])
        started = sum(rs_state[n][l][-1][0:1, 0:1] for n in ("w_down", "w_gate", "w_up"))
        dh2, small["g_pre_ffn"][l], do, small["g_post_mix"][l] = norm_bwd(
            dh, du2, s["h2"], vec(g_pre_ffn, l) + started, (s["o"], vec(g_post_mix, l)))
        theirs = mm_dw_half("mm_dw_out", "take", s["mixed"], do, core_s, True)
        pairs["w_out"], do = pair_start(f"pair_start_out_{l}", theirs, rs_state["w_up"][l][-1], carry=do)
        dmixed = mm_plain("mm_dmixed", do, wg["w_out"], "nt", F32)
        recv = pair_wait(f"pair_wait_out_{l}", pairs["w_out"], dmixed)
        rs_state["w_out"][l] = scatter_chips_start(
            f"rs_start_out_{l}", mm_dw_half("mm_dw_out", "take", s["mixed"], do, core_s, False, add=recv))
        (dya, dc, dyl, small["g_attn_grp"][l], small["g_conv_grp"][l], small["g_lru_grp"][l],
         small["conv_ln_g"][l], small["conv_ln_b"][l]) = mix_bwd(
            dmixed, s["y_attn"], s["y_conv"], s["y_lru"], s["cpre"],
            vec(g_attn_grp, l) + rs_state["w_out"][l][-1][0:1, 0:1], vec(g_conv_grp, l),
            vec(g_lru_grp, l), vec(conv_ln_g, l), vec(conv_ln_b, l))
        dq, dk, dv = attn_bwd(s["proj"], dya, s["att_runs"], n_heads)
        dvg, small["dw_conv_w"][l], small["dw_conv_b"][l] = conv_bwd(s["proj"], conv_col, cc, dc, dw_full[l])
        (dxy, small["lru_conv_w"][l], small["lru_conv_b"][l], small["lru_w_a"][l], small["lru_b_a"][l],
         small["lru_w_i"][l], small["lru_b_i"][l], small["lru_lambda"][l]) = lru_bwd(
            s["proj"], lru_col, wl, s["hs"], dyl, lcw_full[l], vec(lru_conv_b, l), s["wa_b"], vec(lru_b_a, l),
            s["wi_b"], vec(lru_b_i, l), vec(lru_lambda, l))
        dproj = jnp.concatenate([dq, dk, dv, dvg, dxy], axis=1)
        theirs = mm_dw_half("mm_dw_in", "cols", s["u1"], dproj, core_s, True)
        pairs["w_in"], dproj = pair_start(f"pair_start_in_{l}", theirs, rs_state["w_out"][l][-1], carry=dproj)
        du1 = mm_dx_cols("mm_dx_in", dproj, wg["w_in"])
        recv = pair_wait(f"pair_wait_in_{l}", pairs["w_in"], du1)
        after = mm_dw_half("mm_dw_in", "cols", s["u1"], dproj, core_s, False, add=recv)
        rs_state["w_in"][l] = scatter_chips_start(f"rs_start_in_{l}", after)
        g_pre = vec(g_pre_mix, l) + rs_state["w_in"][l][-1][0:1, 0:1]
        if l > 0:
            p = saved[l - 1]
            dh, small["g_pre_mix"][l], dbr, dg_post_ffn = norm_bwd(
                dh2, du1, s["h"], g_pre, (p["dn"], vec(g_post_ffn, l - 1)))
        else:
            dh, small["g_pre_mix"][l] = norm_bwd(dh2, du1, s["h"], g_pre)
    grad_x = dh[None]

    small_shapes = [(depth,) + tuple(wts[n].shape[1:]) if n not in _COL_SHARDED_SMALL
                    else (depth, wts[n].shape[1], cc) for n in _SMALL]
    part = _pack([a for n in _SMALL for a in small[n]])
    small_state = gather_start("ag_small_start", [place_own(part, me_s, F32)], dh)

    grads, delta, new_m, new_v = {}, {}, {}, {}
    behind = small_state[-1]
    for n in reversed(_BIG):
        parts = [reduce_scatter_wait(f"{n[2:]}_{l}", rs_state[n][l], behind) for l in range(depth)]
        shape = wts[n].shape
        _, rows, cols = parts[0][0].shape
        view = (depth, rows, cols)
        g, dl, mn, vn = adam_big("adam_" + n, wts[n].reshape(view), mom[n].reshape(view), var[n].reshape(view),
                                 parts, chip_s)
        grads[n], delta[n], new_m[n], new_v[n] = (a.reshape(shape) for a in (g, dl, mn, vn))
        behind = g
    for group in (grads, delta, new_m, new_v):
        for n in _TRANSPOSED:
            group[n] = jnp.swapaxes(group[n], 1, 2)

    gathered, = gather_finish("ag_small_finish", gather_wait("ag_small_wait", small_state, 0, 1, behind))
    g_small = _unpack(sum_parts(gathered), small_shapes)
    for n, g in zip(_SMALL, g_small):
        if n in _COL_SHARDED_SMALL:
            g = lax.dynamic_slice_in_dim(g, me * (cc // N_DEV), cc // N_DEV, axis=2)
        grads[n] = g
    local_shapes = [tuple(wts[n].shape) for n in _SMALL]
    d_small, m_small, v_small = adam_small(
        _pack([wts[n] for n in _SMALL]), _pack([mom[n] for n in _SMALL]), _pack([var[n] for n in _SMALL]),
        _pack([grads[n] for n in _SMALL]))
    delta.update(zip(_SMALL, _unpack(d_small, local_shapes)))
    new_m.update(zip(_SMALL, _unpack(m_small, local_shapes)))
    new_v.update(zip(_SMALL, _unpack(v_small, local_shapes)))

    return (loss, grad_x, *[grads[n] for n in _ALL], *[delta[n] for n in _ALL],
            *[new_m[n] for n in _ALL], *[new_v[n] for n in _ALL])
```

```python
import functools
import math

import jax
import jax.numpy as jnp
from jax import lax
from jax.experimental import pallas as pl
from jax.experimental.pallas import tpu as pltpu

F32 = jnp.float32
BF16 = jnp.bfloat16

N_DEV = 8
EPS = 1e-6
HEAD_DIM = 128
DW_LEN = 31
LRU_LEN = 4
LRU_BLOCKS = 4
LRU_C = 8.0
ATT_TQ = 512
ATT_TK = 512
ATT_SUM = 256
ATT_HEADS = 2
ROW_T = 256
CONV_HALO = 32
LRU_HALO = 8
LANE = 128
SUBLANE = 8
WIDE_TILE = 2048
PACK_ROWS = 512
VMEM_LIMIT = 56 * 1024 * 1024

ADAM_LR = 0.001
ADAM_B1 = 0.9
ADAM_B2 = 0.999
ADAM_EPS = 1e-08
ADAM_WD = 0.01
ADAM_STEP = 10

MESH_AXES = ("x", "y", "c")
_DIMS = {
    "nn": (((1,), (0,)), ((), ())),
    "nt": (((1,), (1,)), ((), ())),
    "tn": (((0,), (0,)), ((), ())),
}


def _params(n_axes):
    return pltpu.CompilerParams(
        dimension_semantics=("arbitrary",) * n_axes, vmem_limit_bytes=VMEM_LIMIT)


def _dot(a, b, mode="nn"):
    return lax.dot_general(a, b, _DIMS[mode], preferred_element_type=F32)


def _sigmoid(x):
    return 1.0 / (1.0 + jnp.exp(-x))


def _softplus(x):
    return jnp.maximum(x, 0.0) + jnp.log(1.0 + jnp.exp(-jnp.abs(x)))


def _neg_expm1(x):
    series = x * (1.0 + x * (0.5 + x * (1.0 / 6 + x * (1.0 / 24 + x * (1.0 / 120 + x * (1.0 / 720))))))
    return jnp.where(x > -0.25, -series, 1.0 - jnp.exp(x))


_GELU_C = math.sqrt(2.0 / math.pi)


def _gelu_and_grad(x):
    inner = _GELU_C * (x + 0.044715 * x * x * x)
    t = jnp.tanh(inner)
    val = 0.5 * x * (1.0 + t)
    grad = 0.5 * (1.0 + t) + 0.5 * x * (1.0 - t * t) * _GELU_C * (1.0 + 3 * 0.044715 * x * x)
    return val, grad


def _rms_stats(x):
    r = lax.rsqrt(jnp.mean(x * x, axis=-1, keepdims=True) + EPS)
    return x * r, r


def _rms_bwd(dy, x, g):
    xn, r = _rms_stats(x)
    dxn = dy * g
    dx = r * (dxn - xn * jnp.mean(dxn * xn, axis=-1, keepdims=True))
    return dx, jnp.sum(dy * xn, axis=0, keepdims=True)


def _row_spec(tr, width, col=0):
    return pl.BlockSpec((tr, width), lambda i, col=col: (i, col))


def _vec_spec(width):
    return pl.BlockSpec((1, width), lambda i: (0, 0))


def _matmul(name, mode, operands, in_specs, out_shape, out_spec, grid):
    npairs = len(operands) // 2
    nk = grid[2]
    assert nk == 1 or out_shape.dtype == F32

    def body(*refs):
        o_ref = refs[2 * npairs]

        def partial():
            acc = None
            for p in range(npairs):
                d = _dot(refs[2 * p][...], refs[2 * p + 1][...], mode)
                acc = d if acc is None else acc + d
            return acc

        if nk == 1:
            o_ref[...] = partial().astype(o_ref.dtype)
        else:
            k = pl.program_id(2)

            @pl.when(k == 0)
            def _():
                o_ref[...] = partial()

            @pl.when(k > 0)
            def _():
                o_ref[...] += partial()

    return pl.pallas_call(
        body, name=name, grid=grid, in_specs=in_specs, out_specs=out_spec, out_shape=out_shape,
        compiler_params=_params(3),
    )(*operands)


def _tile(n, t):
    if n <= t:
        return n
    return max(k for k in range(SUBLANE, t + 1, SUBLANE) if n % k == 0)


def mm_proj(u, w):
    t, d = u.shape
    nblk, _, nb = w.shape
    tm = _tile(t, WIDE_TILE)
    return _matmul(
        "mm_proj", "nn", (u, w),
        [pl.BlockSpec((tm, d), lambda j, i, k: (i, 0)), pl.BlockSpec((None, d, nb), lambda j, i, k: (j, 0, 0))],
        jax.ShapeDtypeStruct((t, nblk * nb), F32), pl.BlockSpec((tm, nb), lambda j, i, k: (i, j)),
        (nblk, t // tm, 1))


def mm_plain(name, a, b, mode, out_dtype):
    if mode == "nn":
        (m, kk), n = a.shape, b.shape[1]
    elif mode == "nt":
        (m, kk), n = a.shape, b.shape[0]
    else:
        (kk, m), n = a.shape, b.shape[1]
    tm, tn = _tile(m, 1024), _tile(n, WIDE_TILE)
    a_spec = (pl.BlockSpec((kk, tm), lambda i, j, k: (0, i)) if mode == "tn"
              else pl.BlockSpec((tm, kk), lambda i, j, k: (i, 0)))
    b_spec = (pl.BlockSpec((tn, kk), lambda i, j, k: (j, 0)) if mode == "nt"
              else pl.BlockSpec((kk, tn), lambda i, j, k: (0, j)))
    return _matmul(
        name, mode, (a, b), [a_spec, b_spec],
        jax.ShapeDtypeStruct((m, n), out_dtype), pl.BlockSpec((tm, tn), lambda i, j, k: (i, j)),
        (m // tm, n // tn, 1))


def mm_down(f, w):
    nblk, t, fb = f.shape
    d = w.shape[2]
    tm, tn = _tile(t, 1024), _tile(d, WIDE_TILE)
    return _matmul(
        "mm_down", "nn", (f, w),
        [pl.BlockSpec((None, tm, fb), lambda i, j, k: (k, i, 0)), pl.BlockSpec((None, fb, tn), lambda i, j, k: (k, 0, j))],
        jax.ShapeDtypeStruct((t, d), F32), pl.BlockSpec((tm, tn), lambda i, j, k: (i, j)),
        (t // tm, d // tn, nblk))


def mm_dw_half(name, kind, a, g, core, of_sibling, add=None):
    half = N_DEV // 2
    t = g.shape[0]

    def pick(k, core_ref):
        s = 1 - core_ref[0] if of_sibling else core_ref[0]
        return 2 * k + s

    if kind == "rows":
        rows, cols = a.shape[2], g.shape[1]
        tr, tc = rows, _tile(cols, WIDE_TILE)
        a_spec = pl.BlockSpec((None, t, rows), lambda k, i, n, core_ref: (pick(k, core_ref), 0, 0))
        g_spec = pl.BlockSpec((t, tc), lambda k, i, n, core_ref: (0, n))
    elif kind == "cols":
        rows, cols = a.shape[1], g.shape[1] // N_DEV
        tr, tc = _tile(rows, WIDE_TILE), cols
        a_spec = pl.BlockSpec((t, tr), lambda k, i, n, core_ref: (0, i))
        g_spec = pl.BlockSpec((t, cols), lambda k, i, n, core_ref: (0, pick(k, core_ref)))
    else:
        rows, cols = a.shape[1] // N_DEV, g.shape[1]
        tr, tc = rows, _tile(cols, WIDE_TILE)
        a_spec = pl.BlockSpec((t, rows), lambda k, i, n, core_ref: (0, pick(k, core_ref)))
        g_spec = pl.BlockSpec((t, tc), lambda k, i, n, core_ref: (0, n))
    o_spec = pl.BlockSpec((None, tr, tc), lambda k, i, n, core_ref: (k, i, n))

    def body(core_ref, a_ref, g_ref, *rest):
        acc = _dot(a_ref[...], g_ref[...], "tn")
        if add is not None:
            acc = acc + rest[0][...].astype(F32)
        rest[-1][...] = acc.astype(BF16)

    return pl.pallas_call(
        body, name=name,
        grid_spec=pltpu.PrefetchScalarGridSpec(
            num_scalar_prefetch=1, grid=(half, rows // tr, cols // tc),
            in_specs=[a_spec, g_spec] + ([o_spec] if add is not None else []), out_specs=o_spec),
        out_shape=jax.ShapeDtypeStruct((half, rows, cols), BF16), compiler_params=_params(3),
    )(core, a, g, *(() if add is None else (add,)))


def mm_dx_cols(name, g, w):
    t = g.shape[0]
    nblk, d, nb = w.shape
    tm, tn = _tile(t, 1024), _tile(d, WIDE_TILE)
    return _matmul(
        name, "nt", (g, w),
        [pl.BlockSpec((tm, nb), lambda i, j, k: (i, k)), pl.BlockSpec((None, tn, nb), lambda i, j, k: (k, j, 0))],
        jax.ShapeDtypeStruct((t, d), F32), pl.BlockSpec((tm, tn), lambda i, j, k: (i, j)),
        (t // tm, d // tn, nblk))


def mm_dx_ffn(dgt, wg, dup, wu):
    nblk, t, fb = dgt.shape
    d = wg.shape[2]
    tm, tn = _tile(t, 1024), _tile(d, WIDE_TILE)
    a_spec = pl.BlockSpec((None, tm, fb), lambda i, j, k: (k, i, 0))
    b_spec = pl.BlockSpec((None, fb, tn), lambda i, j, k: (k, 0, j))
    return _matmul(
        "mm_dx_ffn", "nn", (dgt, wg, dup, wu), [a_spec, b_spec, a_spec, b_spec],
        jax.ShapeDtypeStruct((t, d), F32), pl.BlockSpec((tm, tn), lambda i, j, k: (i, j)),
        (t // tm, d // tn, nblk))


def ffn_up(u, wg, wu):
    t, d = u.shape
    nblk, fb, _ = wg.shape
    tm = _tile(t, 512)

    def body(u_ref, wg_ref, wu_ref, dgt_ref, dup_ref, f_ref):
        for r0 in range(0, t, tm):
            rows = pl.ds(r0, tm)
            uu = u_ref[rows, :]
            gt = _dot(uu, wg_ref[...], "nt")
            up = _dot(uu, wu_ref[...], "nt")
            s = _sigmoid(gt)
            silu = gt * s
            dgt_ref[rows, :] = (up * s * (1.0 + gt * (1.0 - s))).astype(BF16)
            dup_ref[rows, :] = silu.astype(BF16)
            f_ref[rows, :] = (silu * up).astype(BF16)

    w_spec = pl.BlockSpec((None, fb, d), lambda j: (j, 0, 0))
    o_spec = pl.BlockSpec((None, t, fb), lambda j: (j, 0, 0))
    return pl.pallas_call(
        body, name="ffn_up", grid=(nblk,),
        in_specs=[pl.BlockSpec((t, d), lambda j: (0, 0)), w_spec, w_spec],
        out_specs=[o_spec, o_spec, o_spec],
        out_shape=[jax.ShapeDtypeStruct((nblk, t, fb), BF16)] * 3,
        compiler_params=_params(1),
    )(u, wg, wu)


def ffn_bwd(dd, wd, f_gate, f_up):
    t, d = dd.shape
    nblk, fb, _ = wd.shape
    tm = _tile(t, 512)

    def body(dd_ref, wd_ref, fg_ref, fu_ref, dgt_ref, dup_ref):
        for r0 in range(0, t, tm):
            rows = pl.ds(r0, tm)
            df = _dot(dd_ref[rows, :], wd_ref[...], "nt")
            dgt_ref[rows, :] = (df * fg_ref[rows, :].astype(F32)).astype(BF16)
            dup_ref[rows, :] = (df * fu_ref[rows, :].astype(F32)).astype(BF16)

    s_spec = pl.BlockSpec((None, t, fb), lambda j: (j, 0, 0))
    return pl.pallas_call(
        body, name="ffn_bwd", grid=(nblk,),
        in_specs=[pl.BlockSpec((t, d), lambda j: (0, 0)), pl.BlockSpec((None, fb, d), lambda j: (j, 0, 0)),
                  s_spec, s_spec],
        out_specs=[s_spec, s_spec],
        out_shape=[jax.ShapeDtypeStruct((nblk, t, fb), BF16)] * 2,
        compiler_params=_params(1),
    )(dd, wd, f_gate, f_up)


def rms_pre(h, g):
    t, d = h.shape
    tr = _tile(t, ROW_T)

    def body(h_ref, g_ref, o_ref):
        xn, _ = _rms_stats(h_ref[...])
        o_ref[...] = (xn * g_ref[...]).astype(BF16)

    return pl.pallas_call(
        body, name="rms_pre", grid=(t // tr,),
        in_specs=[_row_spec(tr, d), _vec_spec(d)], out_specs=_row_spec(tr, d),
        out_shape=jax.ShapeDtypeStruct((t, d), BF16), compiler_params=_params(1),
    )(h, g)


def res_norm(h, o, g_post, g_pre):
    t, d = h.shape
    tr = _tile(t, ROW_T)

    def body(h_ref, o_ref, gpo_ref, gpr_ref, h2_ref, u_ref):
        on, _ = _rms_stats(o_ref[...])
        h2 = h_ref[...] + on * gpo_ref[...]
        h2_ref[...] = h2
        hn, _ = _rms_stats(h2)
        u_ref[...] = (hn * gpr_ref[...]).astype(BF16)

    return pl.pallas_call(
        body, name="res_norm", grid=(t // tr,),
        in_specs=[_row_spec(tr, d), _row_spec(tr, d), _vec_spec(d), _vec_spec(d)],
        out_specs=[_row_spec(tr, d), _row_spec(tr, d)],
        out_shape=[jax.ShapeDtypeStruct((t, d), F32), jax.ShapeDtypeStruct((t, d), BF16)],
        compiler_params=_params(1),
    )(h, o, g_post, g_pre)


def final_loss(h2, dbr, g_post, target):
    t, d = h2.shape
    tr = _tile(t, ROW_T)

    def body(h2_ref, d_ref, g_ref, tg_ref, loss_ref, dy_ref, dd_ref, dg_ref):
        i = pl.program_id(0)

        @pl.when(i == 0)
        def _():
            loss_ref[...] = jnp.zeros_like(loss_ref)
            dg_ref[...] = jnp.zeros_like(dg_ref)

        x = d_ref[...]
        g = g_ref[...]
        xn, _ = _rms_stats(x)
        diff = h2_ref[...] + xn * g - tg_ref[...]
        loss_ref[...] += jnp.sum(jnp.sum(diff * diff, axis=1, keepdims=True), axis=0, keepdims=True)
        dy = diff * (1.0 / d)
        dy_ref[...] = dy
        dx, dg = _rms_bwd(dy, x, g)
        dd_ref[...] = dx.astype(BF16)
        dg_ref[...] += dg

    return pl.pallas_call(
        body, name="final_loss", grid=(t // tr,),
        in_specs=[_row_spec(tr, d), _row_spec(tr, d), _vec_spec(d), _row_spec(tr, d)],
        out_specs=[pl.BlockSpec((1, 1), lambda i: (0, 0)), _row_spec(tr, d), _row_spec(tr, d), _vec_spec(d)],
        out_shape=[jax.ShapeDtypeStruct((1, 1), F32), jax.ShapeDtypeStruct((t, d), F32),
                   jax.ShapeDtypeStruct((t, d), BF16), jax.ShapeDtypeStruct((1, d), F32)],
        compiler_params=_params(1),
    )(h2, dbr, g_post, target)


def norm_bwd(dh_out, du, h_in, g_pre, prev=None):
    t, d = h_in.shape
    tr = _tile(t, ROW_T)
    with_prev = prev is not None

    def body(*refs):
        if with_prev:
            dho_ref, du_ref, h_ref, gpr_ref, br_ref, gpo_ref, dh_ref, dgpr_ref, dbr_ref, dgpo_ref = refs
        else:
            dho_ref, du_ref, h_ref, gpr_ref, dh_ref, dgpr_ref = refs
        i = pl.program_id(0)

        @pl.when(i == 0)
        def _():
            dgpr_ref[...] = jnp.zeros_like(dgpr_ref)
            if with_prev:
                dgpo_ref[...] = jnp.zeros_like(dgpo_ref)

        dx, dg = _rms_bwd(du_ref[...], h_ref[...], gpr_ref[...])
        dh = dho_ref[...] + dx
        dh_ref[...] = dh
        dgpr_ref[...] += dg
        if with_prev:
            dbr, dg2 = _rms_bwd(dh, br_ref[...], gpo_ref[...])
            dbr_ref[...] = dbr.astype(BF16)
            dgpo_ref[...] += dg2

    row, vec = _row_spec(tr, d), _vec_spec(d)
    in_specs = [row, row, row, vec] + ([row, vec] if with_prev else [])
    out_specs = [row, vec] + ([row, vec] if with_prev else [])
    out_shape = [jax.ShapeDtypeStruct((t, d), F32), jax.ShapeDtypeStruct((1, d), F32)]
    if with_prev:
        out_shape += [jax.ShapeDtypeStruct((t, d), BF16), jax.ShapeDtypeStruct((1, d), F32)]
    args = (dh_out, du, h_in, g_pre) + (tuple(prev) if with_prev else ())
    return pl.pallas_call(
        body, name="norm_bwd_chain" if with_prev else "norm_bwd_first", grid=(t // tr,),
        in_specs=in_specs, out_specs=out_specs, out_shape=out_shape, compiler_params=_params(1),
    )(*args)


def mix_fwd(ya, yc, yl, ga, gc, gl):
    t, wa = ya.shape
    wc, wl = yc.shape[1], yl.shape[1]
    tr = _tile(t, ROW_T)

    def body(ya_ref, yc_ref, yl_ref, ga_ref, gc_ref, gl_ref, o_ref):
        o_ref[:, pl.ds(0, wa)] = (_rms_stats(ya_ref[...])[0] * ga_ref[...]).astype(BF16)
        o_ref[:, pl.ds(wa, wc)] = (_rms_stats(yc_ref[...])[0] * gc_ref[...]).astype(BF16)
        o_ref[:, pl.ds(wa + wc, wl)] = (_rms_stats(yl_ref[...])[0] * gl_ref[...]).astype(BF16)

    return pl.pallas_call(
        body, name="mix_fwd", grid=(t // tr,),
        in_specs=[_row_spec(tr, wa), _row_spec(tr, wc), _row_spec(tr, wl), _vec_spec(wa), _vec_spec(wc), _vec_spec(wl)],
        out_specs=_row_spec(tr, wa + wc + wl),
        out_shape=jax.ShapeDtypeStruct((t, wa + wc + wl), BF16), compiler_params=_params(1),
    )(ya, yc, yl, ga, gc, gl)


def mix_bwd(dmixed, ya, yc, yl, cpre, ga, gc, gl, lng, lnb):
    t, wa = ya.shape
    wc, wl = yc.shape[1], yl.shape[1]
    tr = _tile(t, ROW_T)

    def body(dm_ref, ya_ref, yc_ref, yl_ref, c_ref, ga_ref, gc_ref, gl_ref, lg_ref, lb_ref,
             dya_ref, dc_ref, dyl_ref, dga_ref, dgc_ref, dgl_ref, dlg_ref, dlb_ref):
        i = pl.program_id(0)

        @pl.when(i == 0)
        def _():
            for r in (dga_ref, dgc_ref, dgl_ref, dlg_ref, dlb_ref):
                r[...] = jnp.zeros_like(r)

        dya, dga = _rms_bwd(dm_ref[:, pl.ds(0, wa)], ya_ref[...], ga_ref[...])
        dya_ref[...] = dya
        dga_ref[...] += dga
        dyl, dgl = _rms_bwd(dm_ref[:, pl.ds(wa + wc, wl)], yl_ref[...], gl_ref[...])
        dyl_ref[...] = dyl
        dgl_ref[...] += dgl
        dyc, dgc = _rms_bwd(dm_ref[:, pl.ds(wa, wc)], yc_ref[...], gc_ref[...])
        dgc_ref[...] += dgc
        c = c_ref[...]
        xc = c - jnp.mean(c, axis=-1, keepdims=True)
        rstd = lax.rsqrt(jnp.mean(xc * xc, axis=-1, keepdims=True) + EPS)
        xhat = xc * rstd
        ln = xhat * lg_ref[...] + lb_ref[...]
        s = _sigmoid(ln)
        dln = dyc * s * (1.0 + ln * (1.0 - s))
        dlg_ref[...] += jnp.sum(dln * xhat, axis=0, keepdims=True)
        dlb_ref[...] += jnp.sum(dln, axis=0, keepdims=True)
        dxh = dln * lg_ref[...]
        dc_ref[...] = rstd * (dxh - jnp.mean(dxh, axis=-1, keepdims=True)
                              - xhat * jnp.mean(dxh * xhat, axis=-1, keepdims=True))

    return pl.pallas_call(
        body, name="mix_bwd", grid=(t // tr,),
        in_specs=[_row_spec(tr, wa + wc + wl), _row_spec(tr, wa), _row_spec(tr, wc), _row_spec(tr, wl), _row_spec(tr, wc),
                  _vec_spec(wa), _vec_spec(wc), _vec_spec(wl), _vec_spec(wc), _vec_spec(wc)],
        out_specs=[_row_spec(tr, wa), _row_spec(tr, wc), _row_spec(tr, wl),
                   _vec_spec(wa), _vec_spec(wc), _vec_spec(wl), _vec_spec(wc), _vec_spec(wc)],
        out_shape=[jax.ShapeDtypeStruct((t, wa), F32), jax.ShapeDtypeStruct((t, wc), F32), jax.ShapeDtypeStruct((t, wl), F32),
                   jax.ShapeDtypeStruct((1, wa), F32), jax.ShapeDtypeStruct((1, wc), F32), jax.ShapeDtypeStruct((1, wl), F32),
                   jax.ShapeDtypeStruct((1, wc), F32), jax.ShapeDtypeStruct((1, wc), F32)],
        compiler_params=_params(1),
    )(dmixed, ya, yc, yl, cpre, ga, gc, gl, lng, lnb)


def _hi_lo(x):
    hi = x.astype(BF16)
    return hi, (x - hi.astype(F32)).astype(BF16)


def _lane_sums(x, tri, reverse):
    nsub = x.shape[1] // tri.shape[0]
    order = range(nsub - 1, -1, -1) if reverse else range(nsub)
    parts, beyond, halves = {}, None, []
    for b in order:
        blk = x[:, b * tri.shape[0]:(b + 1) * tri.shape[0]]
        hi, lo = _hi_lo(blk)
        halves += [hi, lo]
        c = _dot(hi, tri) + _dot(lo, tri)
        parts[b] = c if beyond is None else c + beyond
        tot = jnp.sum(blk, axis=1, keepdims=True)
        beyond = tot if beyond is None else beyond + tot
    return jnp.concatenate([parts[b] for b in range(nsub)], axis=1), beyond, halves


def _att_strip(qb, kt, thresh, scale, diff, tri_gt):
    z = _dot(qb, kt, "nt") * scale
    sp = _softplus(z)
    mask = diff < thresh
    later, total, halves = _lane_sums(jnp.where(mask, -sp, 0.0), tri_gt, True)
    return z, sp, mask, total, (z - sp) + later, halves


def _att_consts(tq, tk):
    diff = lax.broadcasted_iota(jnp.int32, (tq, tk), 1) - lax.broadcasted_iota(jnp.int32, (tq, tk), 0)
    cb = min(tk, ATT_SUM)
    row = lax.broadcasted_iota(jnp.int32, (cb, cb), 0)
    col = lax.broadcasted_iota(jnp.int32, (cb, cb), 1)
    return diff, (row > col).astype(BF16), (row < col).astype(BF16)


def attn_fwd(proj, n_heads):
    t = proj.shape[0]
    tq, tk = _tile(t, ATT_TQ), _tile(t, ATT_TK)
    hp = ATT_HEADS
    wd = hp * HEAD_DIM
    scale = HEAD_DIM ** -0.5

    def body(q_ref, k_ref, v_ref, o_ref, r_ref, kb_ref, vb_ref, acc_ref):
        kb_ref[...] = k_ref[...].astype(BF16)
        vb_ref[...] = v_ref[...].astype(BF16)
        diff, tri_gt, _ = _att_consts(tq, tk)
        ones = jnp.ones((2 * SUBLANE, min(tk, ATT_SUM)), BF16)

        def qblock(i, _):
            q0 = pl.multiple_of(i * tq, tq)
            heads = [pl.ds(h * HEAD_DIM, HEAD_DIM) for h in range(hp)]
            qbs = [q_ref[pl.ds(q0, tq), hs].astype(BF16) for hs in heads]
            acc_ref[...] = jnp.zeros_like(acc_ref)
            n_strips = (q0 + tq + tk - 1) // tk

            def strip(jj, carry):
                si = n_strips - 1 - jj
                k0 = pl.multiple_of(si * tk, tk)
                out = []
                for h, hs in enumerate(heads):
                    run, run_row = carry[2 * h], carry[2 * h + 1]
                    _, _, mask, total, logw, halves = _att_strip(
                        qbs[h], kb_ref[pl.ds(k0, tk), hs], q0 - k0, scale, diff, tri_gt)
                    w = jnp.where(mask, jnp.exp(logw + run), 0.0)
                    acc_ref[:, hs] += _dot(w.astype(BF16), vb_ref[pl.ds(k0, tk), hs])
                    r_ref[h, i, si] = run_row
                    total_row = sum(_dot(ones, half, "nt") for half in halves)[:SUBLANE]
                    out += [run + total, run_row + total_row]
                return tuple(out)

            init = (jnp.zeros((tq, 1), F32), jnp.zeros((SUBLANE, tq), F32)) * hp
            lax.fori_loop(0, n_strips, strip, init)
            o_ref[pl.ds(q0, tq), :] = acc_ref[...]
            return 0

        lax.fori_loop(0, t // tq, qblock, 0)

    def col_spec(base):
        return pl.BlockSpec((t, wd), lambda h, base=base: (0, base + h))

    ng = n_heads // hp
    runs_shape = (n_heads, t // tq, t // tk, SUBLANE, tq)
    return pl.pallas_call(
        body, name="attn_fwd", grid=(ng,),
        in_specs=[col_spec(0), col_spec(ng), col_spec(2 * ng)],
        out_specs=[col_spec(0), pl.BlockSpec((hp,) + runs_shape[1:], lambda h: (h, 0, 0, 0, 0))],
        out_shape=[jax.ShapeDtypeStruct((t, n_heads * HEAD_DIM), F32), jax.ShapeDtypeStruct(runs_shape, F32)],
        scratch_shapes=[pltpu.VMEM((t, wd), BF16), pltpu.VMEM((t, wd), BF16), pltpu.VMEM((tq, wd), F32)],
        compiler_params=_params(1),
    )(proj, proj, proj)


def attn_bwd(proj, dy, runs, n_heads):
    t = proj.shape[0]
    tq, tk = _tile(t, ATT_TQ), _tile(t, ATT_TK)
    hp = ATT_HEADS
    wd = hp * HEAD_DIM
    scale = HEAD_DIM ** -0.5

    def body(q_ref, k_ref, v_ref, dy_ref, r_ref, dq_ref, dk_ref, dv_ref, qb_ref, kb_ref, vb_ref, dob_ref, dk_acc,
             dv_acc, dq_acc):
        qb_ref[...] = q_ref[...].astype(BF16)
        kb_ref[...] = k_ref[...].astype(BF16)
        vb_ref[...] = v_ref[...].astype(BF16)
        dob_ref[...] = dy_ref[...].astype(BF16)
        dk_acc[...] = jnp.zeros_like(dk_acc)
        dv_acc[...] = jnp.zeros_like(dv_acc)
        diff, tri_gt, tri_lt = _att_consts(tq, tk)

        def qblock(i, _):
            q0 = pl.multiple_of(i * tq, tq)
            heads = [pl.ds(h * HEAD_DIM, HEAD_DIM) for h in range(hp)]
            qbs = [qb_ref[pl.ds(q0, tq), hs] for hs in heads]
            dobs = [dob_ref[pl.ds(q0, tq), hs] for hs in heads]
            dq_acc[...] = jnp.zeros_like(dq_acc)
            n_strips = (q0 + tq + tk - 1) // tk

            zero = tuple(jnp.zeros((tq, 1), F32) for _ in range(hp))

            def strip(si, gsums):
                k0 = pl.multiple_of(si * tk, tk)
                out = []
                for h, hs in enumerate(heads):
                    kt = kb_ref[pl.ds(k0, tk), hs]
                    vt = vb_ref[pl.ds(k0, tk), hs]
                    z, sp, mask, _, logw, _ = _att_strip(qbs[h], kt, q0 - k0, scale, diff, tri_gt)
                    run = jnp.concatenate([r_ref[h, i, si]] * (LANE // SUBLANE), axis=0).T
                    w = jnp.where(mask, jnp.exp(logw + jnp.concatenate([run] * (tk // LANE), axis=1)), 0.0)
                    g = w * _dot(dobs[h], vt, "nt")
                    before, gtot, _ = _lane_sums(g, tri_lt, False)
                    sig = jnp.exp(z - sp)
                    dz = jnp.where(mask, g * (1.0 - sig) - (before + gsums[h]) * sig, 0.0) * scale
                    dzb = dz.astype(BF16)
                    dk_acc[pl.ds(k0, tk), hs] += _dot(dzb, qbs[h], "tn")
                    dv_acc[pl.ds(k0, tk), hs] += _dot(w.astype(BF16), dobs[h], "tn")
                    dq_acc[:, hs] += _dot(dzb, kt)
                    out.append(gsums[h] + gtot)
                return tuple(out)

            lax.fori_loop(0, n_strips, strip, zero)
            dq_ref[pl.ds(q0, tq), :] = dq_acc[...].astype(BF16)
            return 0

        lax.fori_loop(0, t // tq, qblock, 0)
        dk_ref[...] = dk_acc[...].astype(BF16)
        dv_ref[...] = dv_acc[...].astype(BF16)

    def col_spec(base):
        return pl.BlockSpec((t, wd), lambda h, base=base: (0, base + h))

    ng = n_heads // hp
    return pl.pallas_call(
        body, name="attn_bwd", grid=(ng,),
        in_specs=[col_spec(0), col_spec(ng), col_spec(2 * ng), col_spec(0),
                  pl.BlockSpec((hp,) + runs.shape[1:], lambda h: (h, 0, 0, 0, 0))],
        out_specs=[col_spec(0), col_spec(0), col_spec(0)],
        out_shape=[jax.ShapeDtypeStruct((t, n_heads * HEAD_DIM), BF16)] * 3,
        scratch_shapes=[pltpu.VMEM((t, wd), BF16)] * 4 + [pltpu.VMEM((t, wd), F32)] * 2 + [pltpu.VMEM((tq, wd), F32)],
        compiler_params=_params(1),
    )(proj, proj, proj, dy, runs)


def _glu_halo(vc, gc, vp, gp, ubuf, i, tt, halo):
    uprev = vp[pl.ds(tt - halo, halo), :] * _sigmoid(gp[pl.ds(tt - halo, halo), :])
    ubuf[pl.ds(0, halo), :] = jnp.where(i > 0, uprev, 0.0)
    ubuf[pl.ds(halo, tt), :] = vc[...] * _sigmoid(gc[...])


def conv_fwd(proj, col0, cc, w, b, lng, lnb):
    t = proj.shape[0]
    tt = _tile(t, ROW_T)
    vi, gi = col0 // cc, col0 // cc + 1
    off = CONV_HALO - (DW_LEN - 1)

    def body(vc, gc, vp, gp, w_ref, b_ref, lg_ref, lb_ref, c_ref, y_ref, ubuf):
        i = pl.program_id(0)
        _glu_halo(vc, gc, vp, gp, ubuf, i, tt, CONV_HALO)
        for ch in range(cc // LANE):
            sl = pl.ds(ch * LANE, LANE)
            acc = jnp.zeros((tt, LANE), F32) + b_ref[:, sl]
            for tap in range(DW_LEN):
                acc = acc + w_ref[pl.ds(tap, 1), sl] * ubuf[pl.ds(off + tap, tt), sl]
            c_ref[:, sl] = acc
        c = c_ref[...]
        xc = c - jnp.mean(c, axis=-1, keepdims=True)
        ln = xc * lax.rsqrt(jnp.mean(xc * xc, axis=-1, keepdims=True) + EPS) * lg_ref[...] + lb_ref[...]
        y_ref[...] = ln * _sigmoid(ln)

    cur = lambda c: pl.BlockSpec((tt, cc), lambda i, c=c: (i, c))
    prev = lambda c: pl.BlockSpec((tt, cc), lambda i, c=c: (jnp.maximum(i - 1, 0), c))
    return pl.pallas_call(
        body, name="conv_fwd", grid=(t // tt,),
        in_specs=[cur(vi), cur(gi), prev(vi), prev(gi), pl.BlockSpec((DW_LEN, cc), lambda i: (0, 0)),
                  _vec_spec(cc), _vec_spec(cc), _vec_spec(cc)],
        out_specs=[_row_spec(tt, cc), _row_spec(tt, cc)],
        out_shape=[jax.ShapeDtypeStruct((t, cc), F32)] * 2,
        scratch_shapes=[pltpu.VMEM((CONV_HALO + tt, cc), F32)],
        compiler_params=_params(1),
    )(proj, proj, proj, proj, w, b, lng, lnb)


def conv_bwd(proj, col0, cc, dc, w):
    t = proj.shape[0]
    tt = _tile(t, ROW_T)
    nt = t // tt
    vi, gi = col0 // cc, col0 // cc + 1

    def body(vc, gc, dcc, dcn, w_ref, dvg_ref, dw_ref, db_ref, dbuf):
        i = pl.program_id(0)

        @pl.when(i == 0)
        def _():
            dw_ref[...] = jnp.zeros_like(dw_ref)
            db_ref[...] = jnp.zeros_like(db_ref)

        dbuf[pl.ds(0, tt), :] = dcc[...]
        dbuf[pl.ds(tt, CONV_HALO), :] = jnp.where(i < nt - 1, dcn[pl.ds(0, CONV_HALO), :], 0.0)
        db_ref[...] += jnp.sum(dcc[...], axis=0, keepdims=True)
        for ch in range(cc // LANE):
            sl = pl.ds(ch * LANE, LANE)
            s = _sigmoid(gc[:, sl])
            val = vc[:, sl]
            u = val * s
            du = jnp.zeros((tt, LANE), F32)
            for tap in range(DW_LEN):
                ahead = dbuf[pl.ds(DW_LEN - 1 - tap, tt), sl]
                du = du + w_ref[pl.ds(tap, 1), sl] * ahead
                dw_ref[pl.ds(tap, 1), sl] += jnp.sum(ahead * u, axis=0, keepdims=True)
            dvg_ref[:, sl] = (du * s).astype(BF16)
            dvg_ref[:, pl.ds(cc + ch * LANE, LANE)] = (du * val * s * (1.0 - s)).astype(BF16)

    cur = lambda c: pl.BlockSpec((tt, cc), lambda i, c=c: (i, c))
    return pl.pallas_call(
        body, name="conv_bwd", grid=(nt,),
        in_specs=[cur(vi), cur(gi), _row_spec(tt, cc),
                  pl.BlockSpec((tt, cc), lambda i: (jnp.minimum(i + 1, nt - 1), 0)),
                  pl.BlockSpec((DW_LEN, cc), lambda i: (0, 0))],
        out_specs=[_row_spec(tt, 2 * cc), pl.BlockSpec((DW_LEN, cc), lambda i: (0, 0)), _vec_spec(cc)],
        out_shape=[jax.ShapeDtypeStruct((t, 2 * cc), BF16), jax.ShapeDtypeStruct((DW_LEN, cc), F32),
                   jax.ShapeDtypeStruct((1, cc), F32)],
        scratch_shapes=[pltpu.VMEM((tt + CONV_HALO, cc), F32)],
        compiler_params=_params(1),
    )(proj, proj, dc, dc, w)


def _lru_gates(xbuf, cw_ref, cb_ref, wa_ref, ba_ref, wi_ref, bi_ref, lam_ref, tt, wl):
    bd = wl // LRU_BLOCKS
    xr = jnp.zeros((tt, wl), F32) + cb_ref[...]
    for tap in range(LRU_LEN):
        xr = xr + cw_ref[pl.ds(tap, 1), :] * xbuf[pl.ds(LRU_HALO - (LRU_LEN - 1) + tap, tt), :]
    xb = xr.astype(BF16)
    ga = jnp.concatenate([_dot(xb[:, n * bd:(n + 1) * bd], wa_ref[n]) for n in range(LRU_BLOCKS)], axis=1) + ba_ref[...]
    gi = jnp.concatenate([_dot(xb[:, n * bd:(n + 1) * bd], wi_ref[n]) for n in range(LRU_BLOCKS)], axis=1) + bi_ref[...]
    r = _sigmoid(ga)
    ig = _sigmoid(gi)
    spl = _softplus(-lam_ref[...])
    log_a = -LRU_C * r * spl
    a = jnp.exp(log_a)
    m = jnp.sqrt(_neg_expm1(2.0 * log_a))
    return xr, xb, r, ig, spl, a, m


def _group_scan(a8, b8, reverse):
    rid = lax.broadcasted_iota(jnp.int32, a8.shape, 0)
    aa, bb = a8, b8
    for dist in (1, 2, 4):
        shift = SUBLANE - dist if reverse else dist
        a_sh = pltpu.roll(aa, shift, 0)
        b_sh = pltpu.roll(bb, shift, 0)
        valid = (rid < SUBLANE - dist) if reverse else (rid >= dist)
        bb = jnp.where(valid, aa * b_sh + bb, bb)
        aa = jnp.where(valid, aa * a_sh, aa)
    return aa, bb


def _pick_row(x8, r):
    rid = lax.broadcasted_iota(jnp.int32, x8.shape, 0)
    return jnp.sum(jnp.where(rid == r, x8, 0.0), axis=0, keepdims=True)


def lru_fwd(proj, col0, wl, cw, cb, wa, ba, wi, bi, lam):
    t = proj.shape[0]
    tt = _tile(t, ROW_T)
    xi, yi = col0 // wl, col0 // wl + 1

    def body(xc, xp, ry, cw_ref, cb_ref, wa_ref, ba_ref, wi_ref, bi_ref, lam_ref, hs_ref, y_ref,
             xbuf, a_s, b_s, hcar):
        i = pl.program_id(0)

        @pl.when(i == 0)
        def _():
            hcar[...] = jnp.zeros_like(hcar)

        xbuf[pl.ds(0, LRU_HALO), :] = jnp.where(i > 0, xp[pl.ds(tt - LRU_HALO, LRU_HALO), :], 0.0)
        xbuf[pl.ds(LRU_HALO, tt), :] = xc[...]
        xr, _, _, ig, _, a, m = _lru_gates(xbuf, cw_ref, cb_ref, wa_ref, ba_ref, wi_ref, bi_ref, lam_ref, tt, wl)
        a_s[...] = a
        b_s[...] = m * ig * xr

        def group(gidx, h):
            r0 = pl.multiple_of(gidx * SUBLANE, SUBLANE)
            aa, bb = _group_scan(a_s[pl.ds(r0, SUBLANE), :], b_s[pl.ds(r0, SUBLANE), :], False)
            h8 = aa * h + bb
            hs_ref[pl.ds(r0, SUBLANE), :] = h8
            return _pick_row(h8, SUBLANE - 1)

        hcar[...] = lax.fori_loop(0, tt // SUBLANE, group, hcar[...])
        gel, _ = _gelu_and_grad(ry[...])
        y_ref[...] = hs_ref[...] * gel

    cur = lambda c: pl.BlockSpec((tt, wl), lambda i, c=c: (i, c))
    full = lambda shape: pl.BlockSpec(shape, lambda i: (0,) * len(shape))
    return pl.pallas_call(
        body, name="lru_fwd", grid=(t // tt,),
        in_specs=[cur(xi), pl.BlockSpec((tt, wl), lambda i: (jnp.maximum(i - 1, 0), xi)), cur(yi),
                  full((LRU_LEN, wl)), _vec_spec(wl), full(wa.shape), _vec_spec(wl), full(wi.shape), _vec_spec(wl),
                  _vec_spec(wl)],
        out_specs=[_row_spec(tt, wl), _row_spec(tt, wl)],
        out_shape=[jax.ShapeDtypeStruct((t, wl), F32)] * 2,
        scratch_shapes=[pltpu.VMEM((LRU_HALO + tt, wl), F32), pltpu.VMEM((tt, wl), F32), pltpu.VMEM((tt, wl), F32),
                        pltpu.VMEM((1, wl), F32)],
        compiler_params=_params(1),
    )(proj, proj, proj, cw, cb, wa, ba, wi, bi, lam)


def lru_bwd(proj, col0, wl, hs, dy, cw, cb, wa, ba, wi, bi, lam):
    t = proj.shape[0]
    tt = _tile(t, ROW_T)
    nt = t // tt
    xi, yi = col0 // wl, col0 // wl + 1
    bd = wl // LRU_BLOCKS

    def body(xc, xp, ry, hc, hp, dy_ref, cw_ref, cb_ref, wa_ref, ba_ref, wi_ref, bi_ref, lam_ref,
             dxy_ref, dcw_ref, dcb_ref, dwa_ref, dba_ref, dwi_ref, dbi_ref, dlam_ref,
             xbuf, hbuf, abuf, e_s, dh_s, dxbuf, dhcar):
        i = pl.program_id(0)
        first = i == 0

        @pl.when(first)
        def _():
            for r in (dcw_ref, dcb_ref, dwa_ref, dba_ref, dwi_ref, dbi_ref, dlam_ref, dhcar):
                r[...] = jnp.zeros_like(r)
            abuf[pl.ds(tt, LRU_HALO), :] = jnp.zeros((LRU_HALO, wl), F32)
            dxbuf[pl.ds(tt, LRU_HALO), :] = jnp.zeros((LRU_HALO, wl), F32)

        has_prev = i < nt - 1
        xbuf[pl.ds(0, LRU_HALO), :] = jnp.where(has_prev, xp[pl.ds(tt - LRU_HALO, LRU_HALO), :], 0.0)
        xbuf[pl.ds(LRU_HALO, tt), :] = xc[...]
        hbuf[pl.ds(0, LRU_HALO), :] = jnp.where(has_prev, hp[pl.ds(tt - LRU_HALO, LRU_HALO), :], 0.0)
        hbuf[pl.ds(LRU_HALO, tt), :] = hc[...]
        xr, xb, r, ig, spl, a, m = _lru_gates(xbuf, cw_ref, cb_ref, wa_ref, ba_ref, wi_ref, bi_ref, lam_ref, tt, wl)
        gel, dgel = _gelu_and_grad(ry[...])
        dyv = dy_ref[...]
        e_s[...] = dyv * gel
        dxy_ref[:, pl.ds(wl, wl)] = (dyv * hc[...] * dgel).astype(BF16)
        abuf[pl.ds(0, tt), :] = a
        a_next = abuf[pl.ds(1, tt), :]
        dh_s[...] = a_next

        def group(it, dh_in):
            r0 = pl.multiple_of((tt // SUBLANE - 1 - it) * SUBLANE, SUBLANE)
            aa, bb = _group_scan(dh_s[pl.ds(r0, SUBLANE), :], e_s[pl.ds(r0, SUBLANE), :], True)
            dh8 = aa * dh_in + bb
            dh_s[pl.ds(r0, SUBLANE), :] = dh8
            return _pick_row(dh8, 0)

        dhcar[...] = lax.fori_loop(0, tt // SUBLANE, group, dhcar[...])
        abuf[pl.ds(tt, LRU_HALO), :] = a[0:LRU_HALO, :]
        dh = dh_s[...]
        h_m1 = hbuf[pl.ds(LRU_HALO - 1, tt), :]
        dlog_a = dh * h_m1 * a - dh * ig * xr * (a * a / m)
        dig = dh * m * xr
        dxr = dh * m * ig
        dga = dlog_a * (-LRU_C) * spl * r * (1.0 - r)
        dgi = dig * ig * (1.0 - ig)
        dlam_ref[...] += jnp.sum(dlog_a * r, axis=0, keepdims=True) * (LRU_C * _sigmoid(-lam_ref[...]))
        dba_ref[...] += jnp.sum(dga, axis=0, keepdims=True)
        dbi_ref[...] += jnp.sum(dgi, axis=0, keepdims=True)
        dgab = dga.astype(BF16)
        dgib = dgi.astype(BF16)
        back = []
        for n in range(LRU_BLOCKS):
            sl = slice(n * bd, (n + 1) * bd)
            dwa_ref[n] += _dot(xb[:, sl], dgab[:, sl], "tn")
            dwi_ref[n] += _dot(xb[:, sl], dgib[:, sl], "tn")
            back.append(_dot(dgab[:, sl], wa_ref[n], "nt") + _dot(dgib[:, sl], wi_ref[n], "nt"))
        dxr = dxr + jnp.concatenate(back, axis=1)
        dcb_ref[...] += jnp.sum(dxr, axis=0, keepdims=True)
        dxbuf[pl.ds(0, tt), :] = dxr
        drx = jnp.zeros((tt, wl), F32)
        for tap in range(LRU_LEN):
            drx = drx + cw_ref[pl.ds(tap, 1), :] * dxbuf[pl.ds(LRU_LEN - 1 - tap, tt), :]
            dcw_ref[pl.ds(tap, 1), :] += jnp.sum(
                dxr * xbuf[pl.ds(LRU_HALO - (LRU_LEN - 1) + tap, tt), :], axis=0, keepdims=True)
        dxbuf[pl.ds(tt, LRU_HALO), :] = dxr[0:LRU_HALO, :]
        dxy_ref[:, pl.ds(0, wl)] = drx.astype(BF16)

    rev = lambda c: pl.BlockSpec((tt, wl), lambda i, c=c: (nt - 1 - i, c))
    rev_prev = lambda c: pl.BlockSpec((tt, wl), lambda i, c=c: (jnp.maximum(nt - 2 - i, 0), c))
    full = lambda shape: pl.BlockSpec(shape, lambda i: (0,) * len(shape))
    vec = _vec_spec(wl)
    return pl.pallas_call(
        body, name="lru_bwd", grid=(nt,),
        in_specs=[rev(xi), rev_prev(xi), rev(yi), rev(0), rev_prev(0), rev(0),
                  full((LRU_LEN, wl)), vec, full(wa.shape), vec, full(wi.shape), vec, vec],
        out_specs=[pl.BlockSpec((tt, 2 * wl), lambda i: (nt - 1 - i, 0)), full((LRU_LEN, wl)), vec,
                   full(wa.shape), vec, full(wi.shape), vec, vec],
        out_shape=[jax.ShapeDtypeStruct((t, 2 * wl), BF16), jax.ShapeDtypeStruct((LRU_LEN, wl), F32),
                   jax.ShapeDtypeStruct((1, wl), F32), jax.ShapeDtypeStruct(wa.shape, F32),
                   jax.ShapeDtypeStruct((1, wl), F32), jax.ShapeDtypeStruct(wi.shape, F32),
                   jax.ShapeDtypeStruct((1, wl), F32), jax.ShapeDtypeStruct((1, wl), F32)],
        scratch_shapes=[pltpu.VMEM((LRU_HALO + tt, wl), F32), pltpu.VMEM((LRU_HALO + tt, wl), F32),
                        pltpu.VMEM((tt + LRU_HALO, wl), F32), pltpu.VMEM((tt, wl), F32), pltpu.VMEM((tt, wl), F32),
                        pltpu.VMEM((tt + LRU_HALO, wl), F32), pltpu.VMEM((1, wl), F32)],
        compiler_params=_params(1),
    )(proj, proj, proj, hs, hs, dy, cw, cb, wa, ba, wi, bi, lam)


def _adamw(w, g, m, v):
    m = ADAM_B1 * m + (1.0 - ADAM_B1) * g
    v = ADAM_B2 * v + (1.0 - ADAM_B2) * (g * g)
    m_hat = m / (1.0 - ADAM_B1 ** ADAM_STEP)
    v_hat = v / (1.0 - ADAM_B2 ** ADAM_STEP)
    delta = -ADAM_LR * (m_hat / (jnp.sqrt(v_hat) + ADAM_EPS) + ADAM_WD * w)
    return delta, m, v


def adam_big(name, w, m, v, parts, chip):
    n_layers, rows, cols = w.shape
    tr = _tile(rows, 128 if cols > 1024 else 256)
    nrt = rows // tr

    def body(chip_ref, *refs):
        w_ref, m_ref, v_ref = refs[:3]
        part_refs = refs[3:3 + 4 * n_layers]
        g_ref, d_ref, mo_ref, vo_ref = refs[3 + 4 * n_layers:]
        layer = pl.program_id(0)
        for l in range(n_layers):
            @pl.when(layer == l)
            def _(l=l):
                g = part_refs[4 * l][...].astype(F32)
                for p in range(1, 4):
                    g = g + part_refs[4 * l + p][...].astype(F32)
                delta, mn, vn = _adamw(w_ref[...], g, m_ref[...], v_ref[...])
                g_ref[...] = g
                d_ref[...] = delta
                mo_ref[...] = mn
                vo_ref[...] = vn

    wspec = pl.BlockSpec((None, tr, cols), lambda l, i, chip_ref: (l, i, 0))
    operands, in_specs = [w, m, v], [wspec, wspec, wspec]
    for l in range(n_layers):
        mine, recv = parts[l]
        operands.append(mine)
        in_specs.append(pl.BlockSpec(
            (None, tr, cols), lambda ll, i, chip_ref, l=l: (chip_ref[0], jnp.where(ll == l, i, 0), 0)))
        for p in range(3):
            operands.append(recv)
            in_specs.append(pl.BlockSpec(
                (None, tr, cols), lambda ll, i, chip_ref, l=l, p=p: (p, jnp.where(ll == l, i, 0), 0)))
    return pl.pallas_call(
        body, name=name,
        grid_spec=pltpu.PrefetchScalarGridSpec(
            num_scalar_prefetch=1, grid=(n_layers, nrt), in_specs=in_specs, out_specs=[wspec] * 4),
        out_shape=[jax.ShapeDtypeStruct(w.shape, F32)] * 4, compiler_params=_params(2),
    )(chip, *operands)


def adam_small(w, m, v, g):
    rows = w.shape[0]
    tr = _tile(rows, PACK_ROWS)

    def body(w_ref, m_ref, v_ref, g_ref, d_ref, mo_ref, vo_ref):
        delta, mn, vn = _adamw(w_ref[...], g_ref[...], m_ref[...], v_ref[...])
        d_ref[...] = delta
        mo_ref[...] = mn
        vo_ref[...] = vn

    spec = _row_spec(tr, LANE)
    return pl.pallas_call(
        body, name="adam_small", grid=(rows // tr,), in_specs=[spec] * 4, out_specs=[spec] * 3,
        out_shape=[jax.ShapeDtypeStruct(w.shape, F32)] * 3, compiler_params=_params(1),
    )(w, m, v, g)


def sum_parts(parts):
    _, rows, _ = parts.shape
    tr = _tile(rows, PACK_ROWS)

    def body(p_ref, o_ref):
        acc = p_ref[0]
        for k in range(1, N_DEV):
            acc = acc + p_ref[k]
        o_ref[...] = acc

    return pl.pallas_call(
        body, name="sum_parts", grid=(rows // tr,),
        in_specs=[pl.BlockSpec((N_DEV, tr, LANE), lambda i: (0, i, 0))], out_specs=_row_spec(tr, LANE),
        out_shape=jax.ShapeDtypeStruct((rows, LANE), F32), compiler_params=_params(1),
    )(parts)


def place_own(x, me, dtype):
    rows, cols = x.shape
    tr = _tile(rows, 256)

    def body(me_ref, x_ref, o_ref):
        o_ref[...] = x_ref[...].astype(dtype)

    return pl.pallas_call(
        body, name="place_own",
        grid_spec=pltpu.PrefetchScalarGridSpec(
            num_scalar_prefetch=1, grid=(rows // tr,),
            in_specs=[pl.BlockSpec((tr, cols), lambda i, me_ref: (i, 0))],
            out_specs=pl.BlockSpec((None, tr, cols), lambda i, me_ref: (me_ref[0], i, 0))),
        out_shape=jax.ShapeDtypeStruct((N_DEV, rows, cols), dtype), compiler_params=_params(1),
    )(me, x)


_HBM = pl.BlockSpec(memory_space=pltpu.HBM)


def _place():
    return lax.axis_index("x"), lax.axis_index("y"), lax.axis_index("c")


def _other_chips(x, y):
    return [(1 - x, y), (x, 1 - y), (1 - x, 1 - y)]


def all_gather(name, shard, me, dtype=None):
    def body(buf_ref, out_ref, send_sems, recv_sems):
        del buf_ref
        x, y, c = _place()
        mine, sibling = (x, y, c), (x, y, 1 - c)
        chips = _other_chips(x, y)

        def copy(k, block, to):
            slot = out_ref.at[4 * block[0] + 2 * block[1] + block[2]]
            return pltpu.make_async_remote_copy(
                src_ref=slot, dst_ref=slot, send_sem=send_sems.at[k], recv_sem=recv_sems.at[k],
                device_id=to, device_id_type=pl.DeviceIdType.MESH)

        first = [copy(0, mine, sibling)] + [copy(1 + j, mine, (*chip, c)) for j, chip in enumerate(chips)]
        for cp in first:
            cp.start()
        passed = [copy(4 + j, (*chip, c), sibling) for j, chip in enumerate(chips)]
        for j, chip in enumerate(chips):
            copy(1 + j, (*chip, c), mine).wait_recv()
            passed[j].start()
        copy(0, sibling, mine).wait_recv()
        for j, chip in enumerate(chips):
            copy(4 + j, (*chip, 1 - c), mine).wait_recv()
        for cp in first + passed:
            cp.wait_send()

    buf = place_own(shard, me, dtype or shard.dtype)
    return pl.pallas_call(
        body, name=name, out_shape=jax.ShapeDtypeStruct(buf.shape, buf.dtype),
        in_specs=[_HBM], out_specs=_HBM, input_output_aliases={0: 0},
        scratch_shapes=[pltpu.SemaphoreType.DMA((7,)), pltpu.SemaphoreType.DMA((7,))],
    )(buf)


def _own_block_copies(src_refs, dst_refs, send_sems, recv_sems, arrivals):
    x, y, c = _place()
    peers = [(x, y, 1 - c)] + [(*chip, c) for chip in _other_chips(x, y)]
    copies = []
    for b, (src, dst) in enumerate(zip(src_refs, dst_refs)):
        for k, peer in enumerate(peers):
            def copy(landing, b=b, k=k, peer=peer, src=src, dst=dst):
                return pltpu.make_async_remote_copy(
                    src_ref=src.at[4 * x + 2 * y + c], dst_ref=dst.at[landing],
                    send_sem=send_sems.at[4 * b + k], recv_sem=recv_sems.at[4 * b + k],
                    device_id=peer, device_id_type=pl.DeviceIdType.MESH)
            copies.append((copy(4 * x + 2 * y + c), copy(4 * peer[0] + 2 * peer[1] + peer[2]) if arrivals else None))
    return copies


def gather_start(name, bufs, after):
    n = len(bufs)

    def body(*refs):
        send_sems, recv_sems = refs[n + 1], refs[n + 2]
        thru = refs[n + 3:2 * n + 3]
        for send, _ in _own_block_copies(thru, thru, send_sems, recv_sems, False):
            send.start()
        refs[2 * n + 3][...] = jnp.zeros((SUBLANE, LANE), F32)

    return pl.pallas_call(
        body, name=name,
        out_shape=(pltpu.SemaphoreType.DMA((4 * n,)), pltpu.SemaphoreType.DMA((4 * n,)),
                   *[pltpu.HBM(b.shape, b.dtype) for b in bufs], jax.ShapeDtypeStruct((SUBLANE, LANE), F32)),
        in_specs=(*(_HBM,) * n, _ANY), out_specs=(_SEM, _SEM, *(_HBM,) * n, _TOKEN),
        input_output_aliases={b: 2 + b for b in range(n)},
        compiler_params=pltpu.CompilerParams(has_side_effects=_EFFECT),
    )(*[_hbm(b) for b in bufs], after)


def gather_wait(name, state, first, count, after):
    send_sems, recv_sems = state[:2]
    bufs = state[2 + first:2 + first + count]
    n = len(bufs)

    def body(*refs):
        ins = refs[:n]
        send_sems, recv_sems = refs[n], refs[n + 1]
        shift = 4 * first
        for send, arrival in _own_block_copies(
                ins, ins, send_sems.at[pl.ds(shift, 4 * n)], recv_sems.at[pl.ds(shift, 4 * n)], True):
            send.wait_send()
            arrival.wait_recv()

    return pl.pallas_call(
        body, name=name, out_shape=tuple(pltpu.HBM(b.shape, b.dtype) for b in bufs),
        in_specs=(*(_HBM,) * n, _SEM, _SEM, _ANY), out_specs=(_HBM,) * n,
        input_output_aliases={b: b for b in range(n)},
        compiler_params=pltpu.CompilerParams(has_side_effects=_EFFECT),
    )(*bufs, send_sems, recv_sems, after)


def _forward_copies(bufs, send_sems, recv_sems, arrivals):
    x, y, c = _place()
    copies = []
    for b, buf in enumerate(bufs):
        for k, chip in enumerate(_other_chips(x, y)):
            def copy(core, b=b, k=k, chip=chip, buf=buf):
                return pltpu.make_async_remote_copy(
                    src_ref=buf.at[4 * chip[0] + 2 * chip[1] + c], dst_ref=buf.at[4 * chip[0] + 2 * chip[1] + core],
                    send_sem=send_sems.at[3 * b + k], recv_sem=recv_sems.at[3 * b + k],
                    device_id=(x, y, 1 - c), device_id_type=pl.DeviceIdType.MESH)
            copies.append((copy(c), copy(1 - c) if arrivals else None))
    return copies


def gather_finish(name, bufs):
    n = len(bufs)

    def body(*refs):
        copies = _forward_copies(refs[n:2 * n], refs[2 * n], refs[2 * n + 1], True)
        for send, _ in copies:
            send.start()
        for send, arrival in copies:
            send.wait_send()
            arrival.wait_recv()

    return pl.pallas_call(
        body, name=name, out_shape=tuple(jax.ShapeDtypeStruct(b.shape, b.dtype) for b in bufs),
        in_specs=(_HBM,) * n, out_specs=(_HBM,) * n, input_output_aliases={b: b for b in range(n)},
        scratch_shapes=[pltpu.SemaphoreType.DMA((3 * n,)), pltpu.SemaphoreType.DMA((3 * n,))],
    )(*bufs)


def forward_start(name, bufs, carry):
    n = len(bufs)

    def body(*refs):
        send_sems, recv_sems = refs[n + 1], refs[n + 2]
        for send, _ in _forward_copies(refs[:n], send_sems, recv_sems, False):
            send.start()

    out = pl.pallas_call(
        body, name=name,
        out_shape=(pltpu.SemaphoreType.DMA((3 * n,)), pltpu.SemaphoreType.DMA((3 * n,)),
                   *[pltpu.HBM(b.shape, b.dtype) for b in bufs], pltpu.HBM(carry.shape, carry.dtype)),
        in_specs=(_HBM,) * (n + 1), out_specs=(_SEM, _SEM, *(_HBM,) * (n + 1)),
        input_output_aliases={b: 2 + b for b in range(n + 1)},
        compiler_params=pltpu.CompilerParams(has_side_effects=_EFFECT),
    )(*[_hbm(b) for b in bufs], _hbm(carry))
    return out[:-1], out[-1]


def forward_wait(name, state, after):
    send_sems, recv_sems, *bufs = state
    n = len(bufs)

    def body(*refs):
        for send, arrival in _forward_copies(refs[:n], refs[n], refs[n + 1], True):
            send.wait_send()
            arrival.wait_recv()

    return pl.pallas_call(
        body, name=name, out_shape=tuple(pltpu.HBM(b.shape, b.dtype) for b in bufs),
        in_specs=(*(_HBM,) * n, _SEM, _SEM, _ANY), out_specs=(_HBM,) * n,
        input_output_aliases={b: b for b in range(n)},
        compiler_params=pltpu.CompilerParams(has_side_effects=_EFFECT),
    )(*bufs, send_sems, recv_sems, after)


_SEM =pl.BlockSpec(memory_space=pltpu.SEMAPHORE)
_ANY = pl.BlockSpec(memory_space=pl.ANY)
_TOKEN = pl.BlockSpec(memory_space=pltpu.VMEM)
_EFFECT = pltpu.SideEffectType.DATAFLOW_SIDE_EFFECTING


def _hbm(a):
    return pltpu.with_memory_space_constraint(a, pltpu.HBM)


def _chip_copies(p_ref, land_ref, send_sems, recv_sems):
    x, y, c = _place()
    return [pltpu.make_async_remote_copy(
        src_ref=p_ref.at[2 * px + py], dst_ref=land_ref.at[k], send_sem=send_sems.at[k], recv_sem=recv_sems.at[k],
        device_id=(px, py, c), device_id_type=pl.DeviceIdType.MESH) for k, (px, py) in enumerate(_other_chips(x, y))]


def scatter_chips_start(name, p):
    _, rows, cols = p.shape

    def body(p_ref, land_ref, send_sems, recv_sems, p_thru, land_thru, token):
        for cp in _chip_copies(p_ref, land_ref, send_sems, recv_sems):
            cp.start()
        token[...] = jnp.zeros_like(token)

    return pl.pallas_call(
        body, name=name,
        out_shape=(pltpu.SemaphoreType.DMA((3,)), pltpu.SemaphoreType.DMA((3,)), pltpu.HBM(p.shape, p.dtype),
                   pltpu.HBM((3, rows, cols), p.dtype), jax.ShapeDtypeStruct((SUBLANE, LANE), F32)),
        in_specs=(_HBM, _HBM), out_specs=(_SEM, _SEM, _HBM, _HBM, _TOKEN), input_output_aliases={0: 2, 1: 3},
        compiler_params=pltpu.CompilerParams(has_side_effects=_EFFECT),
    )(_hbm(p), _hbm(lax.empty((3, rows, cols), p.dtype)))


def scatter_chips_wait(name, send_sems, recv_sems, p_thru, land_thru, after):
    def body(p_ref, land_ref, send_sems, recv_sems, after_ref, p_out, land_out):
        for cp in _chip_copies(p_ref, land_ref, send_sems, recv_sems):
            cp.wait_send()
            cp.wait_recv()

    return pl.pallas_call(
        body, name=name,
        out_shape=(pltpu.HBM(p_thru.shape, p_thru.dtype), pltpu.HBM(land_thru.shape, land_thru.dtype)),
        in_specs=(_HBM, _HBM, _SEM, _SEM, _ANY), out_specs=(_HBM, _HBM), input_output_aliases={0: 0, 1: 1},
        compiler_params=pltpu.CompilerParams(has_side_effects=_EFFECT),
    )(p_thru, land_thru, send_sems, recv_sems, after)


def _pair_copies(g_ref, land_ref, send_sems, recv_sems):
    x, y, c = _place()
    return [pltpu.make_async_remote_copy(
        src_ref=g_ref.at[k], dst_ref=land_ref.at[k], send_sem=send_sems.at[k], recv_sem=recv_sems.at[k],
        device_id=(x, y, 1 - c), device_id_type=pl.DeviceIdType.MESH) for k in range(N_DEV // 2)]


def pair_start(name, g, after, carry=None):
    n = g.shape[0]

    def body(g_ref, land_ref, after_ref, *rest):
        send_sems, recv_sems = rest[-5 if carry is None else -6:][:2]
        for cp in _pair_copies(g_ref, land_ref, send_sems, recv_sems):
            cp.start()
        token = rest[-1 if carry is None else -2]
        token[...] = jnp.zeros_like(token)

    extra = () if carry is None else (carry,)
    out = pl.pallas_call(
        body, name=name,
        out_shape=(pltpu.SemaphoreType.DMA((n,)), pltpu.SemaphoreType.DMA((n,)), pltpu.HBM(g.shape, g.dtype),
                   pltpu.HBM(g.shape, g.dtype), jax.ShapeDtypeStruct((SUBLANE, LANE), F32),
                   *[pltpu.HBM(c.shape, c.dtype) for c in extra]),
        in_specs=(_HBM, _HBM, _ANY, *(_HBM,) * len(extra)), out_specs=(_SEM, _SEM, _HBM, _HBM, _TOKEN, *(_HBM,) * len(extra)),
        input_output_aliases={0: 2, 1: 3, **({3: 5} if extra else {})},
        compiler_params=pltpu.CompilerParams(has_side_effects=_EFFECT),
    )(_hbm(g), _hbm(lax.empty(g.shape, g.dtype)), after, *[_hbm(c) for c in extra])
    return out if carry is None else (out[:5], out[5])


def pair_wait(name, state, after):
    send_sems, recv_sems, g_thru, land_thru, _ = state

    def body(g_ref, land_ref, send_sems, recv_sems, after_ref, g_out, land_out):
        for cp in _pair_copies(g_ref, land_ref, send_sems, recv_sems):
            cp.wait_send()
            cp.wait_recv()

    return pl.pallas_call(
        body, name=name,
        out_shape=(pltpu.HBM(g_thru.shape, g_thru.dtype), pltpu.HBM(land_thru.shape, land_thru.dtype)),
        in_specs=(_HBM, _HBM, _SEM, _SEM, _ANY), out_specs=(_HBM, _HBM), input_output_aliases={0: 0, 1: 1},
        compiler_params=pltpu.CompilerParams(has_side_effects=_EFFECT),
    )(g_thru, land_thru, send_sems, recv_sems, after)[1]


def reduce_scatter_wait(tag, state, after):
    send_sems, recv_sems, p_thru, land_thru, _ = state
    return scatter_chips_wait("rs_wait_" + tag, send_sems, recv_sems, p_thru, land_thru, after)


_SMALL = ("g_pre_mix", "g_post_mix", "g_pre_ffn", "g_post_ffn", "g_attn_grp", "g_conv_grp", "g_lru_grp",
          "dw_conv_w", "dw_conv_b", "conv_ln_g", "conv_ln_b", "lru_conv_w", "lru_conv_b",
          "lru_w_a", "lru_b_a", "lru_w_i", "lru_b_i", "lru_lambda")
_COL_SHARDED_SMALL = ("dw_conv_w", "lru_conv_w")
_BIG = ("w_in", "w_out", "w_gate", "w_up", "w_down")
_TRANSPOSED = ("w_gate", "w_up")
_ALL = ("w_in", "w_out", "g_pre_mix", "g_post_mix", "g_pre_ffn", "g_post_ffn", "g_attn_grp", "g_conv_grp", "g_lru_grp",
        "dw_conv_w", "dw_conv_b", "conv_ln_g", "conv_ln_b", "lru_conv_w", "lru_conv_b", "lru_w_a", "lru_b_a",
        "lru_w_i", "lru_b_i", "lru_lambda", "w_gate", "w_up", "w_down")


def _pack(arrays):
    flat = jnp.concatenate([a.reshape(-1) for a in arrays])
    pad = (-flat.shape[0]) % (PACK_ROWS * LANE)
    return jnp.pad(flat, (0, pad)).reshape(-1, LANE)


def _unpack(packed, shapes):
    flat = packed.reshape(-1)
    out, pos = [], 0
    for s in shapes:
        n = math.prod(s)
        out.append(flat[pos:pos + n].reshape(s))
        pos += n
    return out


def kernel(x, w_in, w_out, g_pre_mix, g_post_mix, g_pre_ffn, g_post_ffn, g_attn_grp, g_conv_grp, g_lru_grp, dw_conv_w, dw_conv_b, conv_ln_g, conv_ln_b, lru_conv_w, lru_conv_b, lru_w_a, lru_b_a, lru_w_i, lru_b_i, lru_lambda, w_gate, w_up, w_down, loss_target, m_w_in, m_w_out, m_g_pre_mix, m_g_post_mix, m_g_pre_ffn, m_g_post_ffn, m_g_attn_grp, m_g_conv_grp, m_g_lru_grp, m_dw_conv_w, m_dw_conv_b, m_conv_ln_g, m_conv_ln_b, m_lru_conv_w, m_lru_conv_b, m_lru_w_a, m_lru_b_a, m_lru_w_i, m_lru_b_i, m_lru_lambda, m_w_gate, m_w_up, m_w_down, v_w_in, v_w_out, v_g_pre_mix, v_g_post_mix, v_g_pre_ffn, v_g_post_ffn, v_g_attn_grp, v_g_conv_grp, v_g_lru_grp, v_dw_conv_w, v_dw_conv_b, v_conv_ln_g, v_conv_ln_b, v_lru_conv_w, v_lru_conv_b, v_lru_w_a, v_lru_b_a, v_lru_w_i, v_lru_b_i, v_lru_lambda, v_w_gate, v_w_up, v_w_down):
    env = dict(locals())
    wts = {n: env[n] for n in _ALL}
    mom = {n: env["m_" + n] for n in _ALL}
    var = {n: env["v_" + n] for n in _ALL}
    for group in (wts, mom, var):
        for n in _TRANSPOSED:
            group[n] = jnp.swapaxes(group[n], 1, 2)

    depth = w_in.shape[0]
    h = x[0]
    target = loss_target[0]
    t, d = h.shape
    attn_w = d // 2
    n_heads = attn_w // HEAD_DIM
    cc = d // 4
    wl = d // 4
    conv_col, lru_col = 3 * attn_w, 3 * attn_w + 2 * cc
    me = 4 * lax.axis_index("x") + 2 * lax.axis_index("y") + lax.axis_index("c")
    me_s = me.astype(jnp.int32).reshape(1)
    chip_s = (2 * lax.axis_index("x") + lax.axis_index("y")).astype(jnp.int32).reshape(1)
    core_s = lax.axis_index("c").astype(jnp.int32).reshape(1)

    n_taps = DW_LEN + LRU_LEN
    taps = jnp.concatenate([dw_conv_w, lru_conv_w], axis=1).reshape(depth * n_taps, cc // N_DEV)
    taps = all_gather("ag_taps", taps, me_s)
    taps = jnp.moveaxis(taps.reshape(N_DEV, depth, n_taps, cc // N_DEV), 0, 2).reshape(depth, n_taps, cc)
    dw_full, lcw_full = taps[:, :DW_LEN], taps[:, DW_LEN:]

    def vec(a, l):
        return a[l].reshape(1, -1)

    ag_state, started = [], taps
    for l in range(depth):
        ag_state.append(gather_start(f"ag_start_{l}", [place_own(wts[n][l], me_s, BF16) for n in _BIG], started))
        started = ag_state[l][-1]
    started = started[0:1, 0:1]

    saved = []
    u1 = rms_pre(h, vec(g_pre_mix, 0) + started)
    loss_sum = dh = dbr = None
    groups = ((0, 1), (1, 1), (2, 2), (4, 1))

    def issue(l, g, carry):
        first, count = groups[g]
        landed = gather_wait(f"ag_wait_{l}_{g}", ag_state[l], first, count, carry)
        return forward_start(f"ag_fwd_start_{l}_{g}", landed, carry)

    def collect(l, g, state, behind, wg):
        first, count = groups[g]
        wg.update(zip(_BIG[first:first + count], forward_wait(f"ag_fwd_wait_{l}_{g}", state, behind)))

    pending, u1 = issue(0, 0, u1)
    for l in range(depth):
        wg = {}
        collect(l, 0, pending, u1, wg)
        pending, u1 = issue(l, 1, u1)
        wa_b, wi_b = lru_w_a[l].astype(BF16), lru_w_i[l].astype(BF16)
        proj = mm_proj(u1, wg["w_in"])
        y_attn, att_runs = attn_fwd(proj, n_heads)
        cpre, y_conv = conv_fwd(proj, conv_col, cc, dw_full[l], vec(dw_conv_b, l), vec(conv_ln_g, l), vec(conv_ln_b, l))
        hs, y_lru = lru_fwd(proj, lru_col, wl, lcw_full[l], vec(lru_conv_b, l), wa_b, vec(lru_b_a, l), wi_b,
                            vec(lru_b_i, l), vec(lru_lambda, l))
        mixed = mix_fwd(y_attn, y_conv, y_lru, vec(g_attn_grp, l), vec(g_conv_grp, l), vec(g_lru_grp, l))
        collect(l, 1, pending, mixed, wg)
        pending, mixed = issue(l, 2, mixed)
        wg["w_out"] = wg["w_out"].reshape(attn_w + cc + wl, d)
        o = mm_plain("mm_out", mixed, wg["w_out"], "nn", F32)
        h2, u2 = res_norm(h, o, vec(g_post_mix, l), vec(g_pre_ffn, l))
        collect(l, 2, pending, u2, wg)
        pending, u2 = issue(l, 3, u2)
        f_gate, f_up, f = ffn_up(u2, wg["w_gate"], wg["w_up"])
        collect(l, 3, pending, f, wg)
        if l + 1 < depth:
            pending, f = issue(l + 1, 0, f)
        dn = mm_down(f, wg["w_down"])
        saved.append(dict(wg=wg, wa_b=wa_b, wi_b=wi_b, h=h, u1=u1, proj=proj, y_attn=y_attn, att_runs=att_runs, cpre=cpre, y_conv=y_conv,
                          hs=hs, y_lru=y_lru, mixed=mixed, o=o, h2=h2, u2=u2, f_gate=f_gate, f_up=f_up, f=f, dn=dn))
        if l + 1 < depth:
            h, u1 = res_norm(h2, dn, vec(g_post_ffn, l), vec(g_pre_mix, l + 1))
        else:
            loss_sum, dh, dbr, dg_post_ffn = final_loss(h2, dn, vec(g_post_ffn, l), target)

    loss = lax.psum(0.5 * loss_sum[0, 0] / d, MESH_AXES)

    small = {n: [None] * depth for n in _SMALL}
    rs_state = {n: [None] * depth for n in _BIG}
    after = dbr
    for l in reversed(range(depth)):
        s = saved[l]
        wg = s["wg"]
        small["g_post_ffn"][l] = dg_post_ffn
        dgt, dup = ffn_bwd(dbr, wg["w_down"], s["f_gate"], s["f_up"])
        ffn_grads = (("w_down", "down", s["f"], dbr), ("w_gate", "gate", dgt, s["u2"]), ("w_up", "up", dup, s["u2"]))
        pairs = {}
        for n, tag, a, g in ffn_grads:
            theirs = mm_dw_half("mm_dw_" + tag, "rows", a, g, core_s, True)
            pairs[n] = pair_start(f"pair_start_{tag}_{l}", theirs, after)
            after = pairs[n][-1]
        for n, tag, a, g in ffn_grads:
            recv = pair_wait(f"pair_wait_{tag}_{l}", pairs[n], after)
            after = mm_dw_half("mm_dw_" + tag, "rows", a, g, core_s, False, add=recv)
            rs_state[n][l] = scatter_chips_start(f"rs_start_{tag}_{l}", after)
        du2 = mm_dx_ffn(dgt, wg["w_gate"], dup, wg["w_up"])
        started = sum(rs_state[n][l][-1][0:1, 0:1] for n in ("w_down", "w_gate", "w_up"))
        dh2, small["g_pre_ffn"][l], do, small["g_post_mix"][l] = norm_bwd(
            dh, du2, s["h2"], vec(g_pre_ffn, l) + started, (s["o"], vec(g_post_mix, l)))
        theirs = mm_dw_half("mm_dw_out", "take", s["mixed"], do, core_s, True)
        pairs["w_out"], do = pair_start(f"pair_start_out_{l}", theirs, rs_state["w_up"][l][-1], carry=do)
        dmixed = mm_plain("mm_dmixed", do, wg["w_out"], "nt", F32)
        recv = pair_wait(f"pair_wait_out_{l}", pairs["w_out"], dmixed)
        rs_state["w_out"][l] = scatter_chips_start(
            f"rs_start_out_{l}", mm_dw_half("mm_dw_out", "take", s["mixed"], do, core_s, False, add=recv))
        (dya, dc, dyl, small["g_attn_grp"][l], small["g_conv_grp"][l], small["g_lru_grp"][l],
         small["conv_ln_g"][l], small["conv_ln_b"][l]) = mix_bwd(
            dmixed, s["y_attn"], s["y_conv"], s["y_lru"], s["cpre"],
            vec(g_attn_grp, l) + rs_state["w_out"][l][-1][0:1, 0:1], vec(g_conv_grp, l),
            vec(g_lru_grp, l), vec(conv_ln_g, l), vec(conv_ln_b, l))
        dq, dk, dv = attn_bwd(s["proj"], dya, s["att_runs"], n_heads)
        dvg, small["dw_conv_w"][l], small["dw_conv_b"][l] = conv_bwd(s["proj"], conv_col, cc, dc, dw_full[l])
        (dxy, small["lru_conv_w"][l], small["lru_conv_b"][l], small["lru_w_a"][l], small["lru_b_a"][l],
         small["lru_w_i"][l], small["lru_b_i"][l], small["lru_lambda"][l]) = lru_bwd(
            s["proj"], lru_col, wl, s["hs"], dyl, lcw_full[l], vec(lru_conv_b, l), s["wa_b"], vec(lru_b_a, l),
            s["wi_b"], vec(lru_b_i, l), vec(lru_lambda, l))
        dproj = jnp.concatenate([dq, dk, dv, dvg, dxy], axis=1)
        theirs = mm_dw_half("mm_dw_in", "cols", s["u1"], dproj, core_s, True)
        pairs["w_in"], dproj = pair_start(f"pair_start_in_{l}", theirs, rs_state["w_out"][l][-1], carry=dproj)
        du1 = mm_dx_cols("mm_dx_in", dproj, wg["w_in"])
        recv = pair_wait(f"pair_wait_in_{l}", pairs["w_in"], du1)
        after = mm_dw_half("mm_dw_in", "cols", s["u1"], dproj, core_s, False, add=recv)
        rs_state["w_in"][l] = scatter_chips_start(f"rs_start_in_{l}", after)
        g_pre = vec(g_pre_mix, l) + rs_state["w_in"][l][-1][0:1, 0:1]
        if l > 0:
            p = saved[l - 1]
            dh, small["g_pre_mix"][l], dbr, dg_post_ffn = norm_bwd(
                dh2, du1, s["h"], g_pre, (p["dn"], vec(g_post_ffn, l - 1)))
        else:
            dh, small["g_pre_mix"][l] = norm_bwd(dh2, du1, s["h"], g_pre)
    grad_x = dh[None]

    small_shapes = [(depth,) + tuple(wts[n].shape[1:]) if n not in _COL_SHARDED_SMALL
                    else (depth, wts[n].shape[1], cc) for n in _SMALL]
    part = _pack([a for n in _SMALL for a in small[n]])
    small_state = gather_start("ag_small_start", [place_own(part, me_s, F32)], dh)

    grads, delta, new_m, new_v = {}, {}, {}, {}
    behind = small_state[-1]
    for n in reversed(_BIG):
        parts = [reduce_scatter_wait(f"{n[2:]}_{l}", rs_state[n][l], behind) for l in range(depth)]
        shape = wts[n].shape
        _, rows, cols = parts[0][0].shape
        view = (depth, rows, cols)
        g, dl, mn, vn = adam_big("adam_" + n, wts[n].reshape(view), mom[n].reshape(view), var[n].reshape(view),
                                 parts, chip_s)
        grads[n], delta[n], new_m[n], new_v[n] = (a.reshape(shape) for a in (g, dl, mn, vn))
        behind = g
    for group in (grads, delta, new_m, new_v):
        for n in _TRANSPOSED:
            group[n] = jnp.swapaxes(group[n], 1, 2)

    gathered, = gather_finish("ag_small_finish", gather_wait("ag_small_wait", small_state, 0, 1, behind))
    g_small = _unpack(sum_parts(gathered), small_shapes)
    for n, g in zip(_SMALL, g_small):
        if n in _COL_SHARDED_SMALL:
            g = lax.dynamic_slice_in_dim(g, me * (cc // N_DEV), cc // N_DEV, axis=2)
        grads[n] = g
    local_shapes = [tuple(wts[n].shape) for n in _SMALL]
    d_small, m_small, v_small = adam_small(
        _pack([wts[n] for n in _SMALL]), _pack([mom[n] for n in _SMALL]), _pack([var[n] for n in _SMALL]),
        _pack([grads[n] for n in _SMALL]))
    delta.update(zip(_SMALL, _unpack(d_small, local_shapes)))
    new_m.update(zip(_SMALL, _unpack(m_small, local_shapes)))
    new_v.update(zip(_SMALL, _unpack(v_small, local_shapes)))

    return (loss, grad_x, *[grads[n] for n in _ALL], *[delta[n] for n in _ALL],
            *[new_m[n] for n in _ALL], *[new_v[n] for n in _ALL])
```
